```python
import jax
import jax.numpy as jnp
from jax import lax
import numpy as np

D_MODEL = 1024
BATCH = 4
SEQ = 4096
DEPTH = 4
DEC_BATCH = 128
DEC_SEQ = 1
PAST_LEN = 8192
PAGE_SIZE = 128

N_MIXERS = 3
N_SWA_LAYERS = (DEPTH + N_MIXERS - 1) // N_MIXERS
N_LRU_LAYERS = (DEPTH + N_MIXERS - 2) // N_MIXERS
N_RWKV_LAYERS = (DEPTH + N_MIXERS - 3) // N_MIXERS
HEAD_DIM = 64
SWA_HEADS = D_MODEL // HEAD_DIM
SWA_KV_HEADS = 4
SWA_GROUP = SWA_HEADS // SWA_KV_HEADS
Q_WIDTH = SWA_HEADS * HEAD_DIM
KV_WIDTH = SWA_KV_HEADS * HEAD_DIM
WINDOW = 128
ROT_DIM = HEAD_DIM // 4
ROPE_THETA = 500000.0
LRU_WIDTH = D_MODEL
LRU_BLOCKS = 16
LRU_BLOCK_W = LRU_WIDTH // LRU_BLOCKS
CONV_W = 4
LRU_C = 8.0
RWKV_HEAD_DIM = 64
RWKV_HEADS = D_MODEL // RWKV_HEAD_DIM
DECAY_LORA = 64
AAA_LORA = 64
GATE_LORA = 128
RWKV_GN_EPS = 64e-5
MEM_LEN = 256
MEM_HEADS = 4
MEM_HEAD_DIM = D_MODEL // MEM_HEADS
N_EXPERTS = 16
N_GROUPS = 4
EXPERTS_PER_GROUP = N_EXPERTS // N_GROUPS
TOP_K = 2
EXPERT_FF = 512
MOE_BLOCK = 128
LN_EPS = 1e-5
DEEPNORM_ALPHA = (2.0 * DEPTH) ** 0.25
DEEPNORM_BETA = (8.0 * DEPTH) ** -0.25
NEG_INF = -1e30

kernel_name = 'hybrid_swa_rglru_rwkv7_memxattn_groupmoe_step'


def _layer_norm(x, g, b):
    xf = x.astype(jnp.float32)
    mu = jnp.mean(xf, axis=-1, keepdims=True)
    var = jnp.mean(jnp.square(xf - mu), axis=-1, keepdims=True)
    y = (xf - mu) * lax.rsqrt(var + LN_EPS) * g.astype(jnp.float32) + b.astype(jnp.float32)
    return y.astype(x.dtype)


def _partial_rope(x, pos):
    half = ROT_DIM // 2
    inv_freq = ROPE_THETA ** (-jnp.arange(half, dtype=jnp.float32) / half)
    ang = pos.astype(jnp.float32)[:, None] * inv_freq
    cos, sin = jnp.cos(ang)[:, None, :], jnp.sin(ang)[:, None, :]
    x1 = x[..., :half].astype(jnp.float32)
    x2 = x[..., half:ROT_DIM].astype(jnp.float32)
    rot = jnp.concatenate([x1 * cos - x2 * sin, x2 * cos + x1 * sin], axis=-1).astype(x.dtype)
    return jnp.concatenate([rot, x[..., ROT_DIM:]], axis=-1)


def _swa_project(x, w_qkv, pos):
    n, t, _ = x.shape
    q, k, v = jnp.split(x @ w_qkv, [Q_WIDTH, Q_WIDTH + KV_WIDTH], axis=-1)
    q = _partial_rope(q.reshape(n, t, SWA_HEADS, HEAD_DIM), pos)
    k = _partial_rope(k.reshape(n, t, SWA_KV_HEADS, HEAD_DIM), pos)
    v = v.reshape(n, t, SWA_KV_HEADS, HEAD_DIM)
    return q.reshape(n, t, SWA_KV_HEADS, SWA_GROUP, HEAD_DIM), k, v


def _sink_attention(q, k, v, qpos, kpos, sinks):
    s = jnp.einsum('...qhgd,...khd->...hgqk', q, k).astype(jnp.float32) * (HEAD_DIM ** -0.5)
    dist = qpos[..., :, None] - kpos[..., None, :]
    valid = (dist >= 0) & (dist < WINDOW) & (kpos[..., None, :] >= 0)
    s = jnp.where(valid[..., None, None, :, :], s, NEG_INF)
    sink = jnp.broadcast_to(sinks.astype(jnp.float32).reshape(SWA_KV_HEADS, SWA_GROUP, 1, 1), s.shape[:-1] + (1,))
    p = jax.nn.softmax(jnp.concatenate([s, sink], axis=-1), axis=-1)[..., :-1]
    return jnp.einsum('...hgqk,...khd->...qhgd', p.astype(v.dtype), v)


def _swa_prompt(x, w_qkv, sinks, w_o):
    n, t, _ = x.shape
    pos = jnp.arange(t)
    q, k, v = _swa_project(x, w_qkv, pos)
    nb = t // WINDOW
    qb = q.reshape(n, nb, WINDOW, SWA_KV_HEADS, SWA_GROUP, HEAD_DIM)

    def with_prev(a):
        ab = a.reshape(n, nb, WINDOW, SWA_KV_HEADS, HEAD_DIM)
        prev = jnp.pad(ab, ((0, 0), (1, 0), (0, 0), (0, 0), (0, 0)))[:, :-1]
        return jnp.concatenate([prev, ab], axis=2)

    qpos = pos.reshape(nb, WINDOW)
    kpos = qpos[:, :1] - WINDOW + jnp.arange(2 * WINDOW)
    o = _sink_attention(qb, with_prev(k), with_prev(v), qpos, kpos, sinks)
    keep = min(WINDOW, t)
    return o.reshape(n, t, Q_WIDTH) @ w_o, k[:, t - keep:], v[:, t - keep:]


def _swa_sample(x, buf_k, buf_v, w_qkv, sinks, w_o):
    n, t, _ = x.shape
    wb = buf_k.shape[1]
    pos = PAST_LEN + jnp.arange(t)
    q, k, v = _swa_project(x, w_qkv, pos)
    k_all = jnp.concatenate([buf_k.astype(k.dtype), k], axis=1)
    v_all = jnp.concatenate([buf_v.astype(v.dtype), v], axis=1)
    kpos = PAST_LEN - wb + jnp.arange(wb + t)
    o = _sink_attention(q, k_all, v_all, pos, kpos, sinks)
    return o.reshape(n, t, Q_WIDTH) @ w_o, k_all[:, -wb:], v_all[:, -wb:]


def _linear_scan(a, b, h0):
    b = b.at[:, 0].add(a[:, 0] * h0)

    def combine(left, right):
        a_l, b_l = left
        a_r, b_r = right
        return a_l * a_r, a_r * b_l + b_r

    _, h = lax.associative_scan(combine, (a, b), axis=1)
    return h


def _rglru(x, conv_state, h0, w_in, b_in, conv_w, conv_b, w_a, b_a, w_i, b_i, lam, w_o):
    n, t, _ = x.shape
    xb, yb = jnp.split(x @ w_in + b_in, 2, axis=-1)
    y_gate = jax.nn.gelu(yb)
    xpad = jnp.concatenate([conv_state.astype(xb.dtype), xb], axis=1)
    xc = conv_b + sum(xpad[:, j:j + t] * conv_w[j] for j in range(CONV_W))
    xblk = xc.reshape(n, t, LRU_BLOCKS, LRU_BLOCK_W)
    r = jax.nn.sigmoid(jnp.einsum('ntbi,bij->ntbj', xblk, w_a).reshape(n, t, LRU_WIDTH) + b_a)
    i = jax.nn.sigmoid(jnp.einsum('ntbi,bij->ntbj', xblk, w_i).reshape(n, t, LRU_WIDTH) + b_i)
    log_a = -LRU_C * r.astype(jnp.float32) * jax.nn.softplus(-lam.astype(jnp.float32))
    a = jnp.exp(log_a)
    b = jnp.sqrt(-jnp.expm1(2.0 * log_a)) * (i * xc).astype(jnp.float32)
    h = _linear_scan(a, b, h0.astype(jnp.float32))
    out = (h.astype(x.dtype) * y_gate) @ w_o
    return out, xpad[:, -(CONV_W - 1):], h[:, -1]


def _wkv7_scan(r, decay, k, v, kk, akk, s0):
    def step(s, inp):
        r_t, d_t, k_t, v_t, kk_t, b_t = inp
        skk = jnp.einsum('nhvk,nhk->nhv', s, kk_t)
        s = s * d_t[:, :, None, :] - skk[..., None] * b_t[:, :, None, :] + v_t[..., None] * k_t[:, :, None, :]
        return s, jnp.einsum('nhvk,nhk->nhv', s, r_t)

    seq = tuple(jnp.moveaxis(a, 1, 0) for a in (r, decay, k, v, kk, akk))
    s, o = lax.scan(step, s0, seq)
    return s, jnp.moveaxis(o, 0, 1)


def _rwkv7(x, shift_state, wkv_state, mu, w_r, w_k, w_v, w0, w1, w2, a0, a1, a2, g1, g2,
           k_k, k_a, r_k, gn_g, gn_b, w_o):
    n, t, d = x.shape
    x_prev = jnp.concatenate([shift_state[:, None].astype(x.dtype), x[:, :-1]], axis=1)
    xx = x_prev - x
    xr, xw, xk, xv, xa, xg = (x + xx * mu[j] for j in range(6))
    r = xr @ w_r
    k = xk @ w_k
    v = xv @ w_v
    w = -jax.nn.softplus(-(w0 + jnp.tanh(xw @ w1) @ w2).astype(jnp.float32)) - 0.5
    decay = jnp.exp(-jnp.exp(w))
    a = jax.nn.sigmoid((a0 + (xa @ a1) @ a2).astype(jnp.float32))
    g = jax.nn.sigmoid(xg @ g1) @ g2

    def heads(z):
        return z.reshape(n, t, RWKV_HEADS, RWKV_HEAD_DIM).astype(jnp.float32)

    kk = heads(k * k_k)
    kk = kk / jnp.maximum(jnp.sqrt(jnp.sum(jnp.square(kk), axis=-1, keepdims=True)), 1e-12)
    k = k.astype(jnp.float32) * (1.0 + (a - 1.0) * k_a.astype(jnp.float32))
    rh, kh, vh, ah, dh = heads(r), heads(k), heads(v), heads(a), heads(decay)
    s, o = _wkv7_scan(rh, dh, kh, vh, kk, kk * ah, wkv_state.astype(jnp.float32))
    mu_o = jnp.mean(o, axis=-1, keepdims=True)
    var_o = jnp.mean(jnp.square(o - mu_o), axis=-1, keepdims=True)
    o = ((o - mu_o) * lax.rsqrt(var_o + RWKV_GN_EPS)).reshape(n, t, d) * gn_g + gn_b
    bonus = jnp.sum(rh * kh * r_k.astype(jnp.float32), axis=-1, keepdims=True) * vh
    o = o + bonus.reshape(n, t, d)
    out = (o.astype(x.dtype) * g) @ w_o
    return out, x[:, -1], s


def _mem_kv(mem, w_kv):
    n, m, _ = mem.shape
    k, v = jnp.split(mem @ w_kv, 2, axis=-1)
    return k.reshape(n, m, MEM_HEADS, MEM_HEAD_DIM), v.reshape(n, m, MEM_HEADS, MEM_HEAD_DIM)


def _mem_attention(x, mk, mv, w_q, w_o):
    n, t, _ = x.shape
    q = (x @ w_q).reshape(n, t, MEM_HEADS, MEM_HEAD_DIM)
    s = jnp.einsum('nthd,nmhd->nhtm', q, mk.astype(q.dtype)).astype(jnp.float32) * (MEM_HEAD_DIM ** -0.5)
    p = jax.nn.softmax(s, axis=-1)
    o = jnp.einsum('nhtm,nmhd->nthd', p.astype(q.dtype), mv.astype(q.dtype))
    return o.reshape(n, t, MEM_HEADS * MEM_HEAD_DIM) @ w_o


def _route(x, router_w, router_b):
    t = x.shape[0]
    probs = jax.nn.softmax((x @ router_w).astype(jnp.float32), axis=-1)
    sel = (probs + router_b.astype(jnp.float32)).reshape(t, N_GROUPS, EXPERTS_PER_GROUP)
    group_score = jnp.sum(lax.top_k(sel, TOP_K)[0], axis=-1)
    best = jnp.argmax(group_score, axis=-1)
    in_group = (jnp.arange(N_GROUPS)[None, :] == best[:, None])[..., None]
    masked = jnp.where(in_group, sel, -jnp.inf).reshape(t, N_EXPERTS)
    _, idx = lax.top_k(masked, TOP_K)
    w = jnp.take_along_axis(probs, idx, axis=-1)
    return idx, w / jnp.sum(w, axis=-1, keepdims=True)


def _moe(x, router_w, router_b, w_gate, w_up, w_down):
    lead = x.shape[:-1]
    xt = x.reshape(-1, D_MODEL)
    t = xt.shape[0]
    idx, gw = _route(xt, router_w, router_b)
    n_assign = t * TOP_K
    flat_e = idx.reshape(n_assign)
    order = jnp.argsort(flat_e)
    e_sorted = flat_e[order]
    tok_sorted = (order // TOP_K).astype(jnp.int32)
    w_sorted = gw.reshape(n_assign)[order]
    counts = jnp.bincount(flat_e, length=N_EXPERTS)
    padded = (counts + MOE_BLOCK - 1) // MOE_BLOCK * MOE_BLOCK
    pad_end = jnp.cumsum(padded)
    rank = jnp.arange(n_assign) - (jnp.cumsum(counts) - counts)[e_sorted]
    dest = (pad_end - padded)[e_sorted] + rank
    n_blocks = -(-n_assign // MOE_BLOCK) + N_EXPERTS
    rows = n_blocks * MOE_BLOCK
    src = jnp.full((rows,), t, jnp.int32).at[dest].set(tok_sorted)
    x_rows = jnp.concatenate([xt, jnp.zeros((1, D_MODEL), xt.dtype)], axis=0)[src]
    block_e = jnp.minimum(jnp.searchsorted(pad_end, jnp.arange(n_blocks) * MOE_BLOCK, side='right'), N_EXPERTS - 1)

    def expert_block(args):
        xb, e = args
        return (jax.nn.silu(xb @ w_gate[e]) * (xb @ w_up[e])) @ w_down[e]

    y_rows = lax.map(expert_block, (x_rows.reshape(n_blocks, MOE_BLOCK, D_MODEL), block_e)).reshape(rows, D_MODEL)
    y = jax.ops.segment_sum(y_rows[dest] * w_sorted[:, None].astype(y_rows.dtype), tok_sorted, num_segments=t)
    return y.reshape(lead + (D_MODEL,))


def setup_inputs(seed: int = 0) -> dict:
    key = jax.random.key(seed)
    ks = iter(jax.random.split(key, 64))

    def nrm(shape, scale=1.0):
        return jax.random.normal(next(ks), shape, jnp.float32) * scale

    def unif(shape, lo, hi):
        return jax.random.uniform(next(ks), shape, jnp.float32, lo, hi)

    d = D_MODEL
    win_buf = min(WINDOW, PAST_LEN)
    lam_s = unif((N_LRU_LAYERS, LRU_WIDTH), 0.9, 0.999) ** (1.0 / LRU_C)
    lru_lambda = jnp.log(lam_s) - jnp.log1p(-lam_s)
    return {
        'x_prompt': nrm((BATCH, SEQ, d)),
        'x_sample': nrm((DEC_BATCH, DEC_SEQ, d)),
        'cache_swa_k': nrm((N_SWA_LAYERS, DEC_BATCH, win_buf, SWA_KV_HEADS, HEAD_DIM)),
        'cache_swa_v': nrm((N_SWA_LAYERS, DEC_BATCH, win_buf, SWA_KV_HEADS, HEAD_DIM)),
        'state_lru_conv': nrm((N_LRU_LAYERS, DEC_BATCH, CONV_W - 1, LRU_WIDTH)),
        'state_lru_h': nrm((N_LRU_LAYERS, DEC_BATCH, LRU_WIDTH), 0.5),
        'state_rwkv_shift': nrm((N_RWKV_LAYERS, DEC_BATCH, d)),
        'state_rwkv_wkv': nrm((N_RWKV_LAYERS, DEC_BATCH, RWKV_HEADS, RWKV_HEAD_DIM, RWKV_HEAD_DIM), 0.3),
        'cache_mem_k': nrm((DEPTH, DEC_BATCH, MEM_LEN, MEM_HEADS, MEM_HEAD_DIM)),
        'cache_mem_v': nrm((DEPTH, DEC_BATCH, MEM_LEN, MEM_HEADS, MEM_HEAD_DIM)),
        'mem_prompt': nrm((BATCH, MEM_LEN, d)),
        'swa_w_qkv': nrm((N_SWA_LAYERS, d, Q_WIDTH + 2 * KV_WIDTH), d ** -0.5),
        'swa_sinks': nrm((N_SWA_LAYERS, SWA_HEADS), 0.5),
        'swa_w_o': nrm((N_SWA_LAYERS, Q_WIDTH, d), DEEPNORM_BETA * Q_WIDTH ** -0.5),
        'lru_w_in': nrm((N_LRU_LAYERS, d, 2 * LRU_WIDTH), d ** -0.5),
        'lru_b_in': nrm((N_LRU_LAYERS, 2 * LRU_WIDTH), 0.01),
        'lru_conv_w': nrm((N_LRU_LAYERS, CONV_W, LRU_WIDTH), CONV_W ** -0.5),
        'lru_conv_b': nrm((N_LRU_LAYERS, LRU_WIDTH), 0.01),
        'lru_w_a': nrm((N_LRU_LAYERS, LRU_BLOCKS, LRU_BLOCK_W, LRU_BLOCK_W), LRU_BLOCK_W ** -0.5),
        'lru_b_a': nrm((N_LRU_LAYERS, LRU_WIDTH), 0.01),
        'lru_w_i': nrm((N_LRU_LAYERS, LRU_BLOCKS, LRU_BLOCK_W, LRU_BLOCK_W), LRU_BLOCK_W ** -0.5),
        'lru_b_i': nrm((N_LRU_LAYERS, LRU_WIDTH), 0.01),
        'lru_lambda': lru_lambda,
        'lru_w_o': nrm((N_LRU_LAYERS, LRU_WIDTH, d), DEEPNORM_BETA * LRU_WIDTH ** -0.5),
        'rwkv_mu': unif((N_RWKV_LAYERS, 6, d), 0.0, 1.0),
        'rwkv_w_r': nrm((N_RWKV_LAYERS, d, d), d ** -0.5),
        'rwkv_w_k': nrm((N_RWKV_LAYERS, d, d), d ** -0.5),
        'rwkv_w_v': nrm((N_RWKV_LAYERS, d, d), d ** -0.5),
        'rwkv_w0': unif((N_RWKV_LAYERS, d), -6.0, -1.0),
        'rwkv_w1': nrm((N_RWKV_LAYERS, d, DECAY_LORA), d ** -0.5),
        'rwkv_w2': nrm((N_RWKV_LAYERS, DECAY_LORA, d), 0.1 * DECAY_LORA ** -0.5),
        'rwkv_a0': nrm((N_RWKV_LAYERS, d), 0.1),
        'rwkv_a1': nrm((N_RWKV_LAYERS, d, AAA_LORA), d ** -0.5),
        'rwkv_a2': nrm((N_RWKV_LAYERS, AAA_LORA, d), 0.1 * AAA_LORA ** -0.5),
        'rwkv_g1': nrm((N_RWKV_LAYERS, d, GATE_LORA), d ** -0.5),
        'rwkv_g2': nrm((N_RWKV_LAYERS, GATE_LORA, d), GATE_LORA ** -0.5),
        'rwkv_k_k': 0.85 + nrm((N_RWKV_LAYERS, d), 0.05),
        'rwkv_k_a': 1.0 + nrm((N_RWKV_LAYERS, d), 0.05),
        'rwkv_r_k': nrm((N_RWKV_LAYERS, RWKV_HEADS, RWKV_HEAD_DIM), 0.1),
        'rwkv_gn_g': 1.0 + nrm((N_RWKV_LAYERS, d), 0.02),
        'rwkv_gn_b': nrm((N_RWKV_LAYERS, d), 0.02),
        'rwkv_w_o': nrm((N_RWKV_LAYERS, d, d), DEEPNORM_BETA * d ** -0.5),
        'mem_w_q': nrm((DEPTH, d, MEM_HEADS * MEM_HEAD_DIM), d ** -0.5),
        'mem_w_kv': nrm((DEPTH, d, 2 * MEM_HEADS * MEM_HEAD_DIM), d ** -0.5),
        'mem_w_o': nrm((DEPTH, MEM_HEADS * MEM_HEAD_DIM, d), DEEPNORM_BETA * (MEM_HEADS * MEM_HEAD_DIM) ** -0.5),
        'ln_g': 1.0 + nrm((DEPTH, 3, d), 0.02),
        'ln_b': nrm((DEPTH, 3, d), 0.02),
        'router_w': nrm((d, N_EXPERTS), d ** -0.5),
        'router_b': nrm((N_EXPERTS,), 0.01),
        'moe_w_gate': nrm((DEPTH, N_EXPERTS, d, EXPERT_FF), d ** -0.5),
        'moe_w_up': nrm((DEPTH, N_EXPERTS, d, EXPERT_FF), d ** -0.5),
        'moe_w_down': nrm((DEPTH, N_EXPERTS, EXPERT_FF, d), DEEPNORM_BETA * EXPERT_FF ** -0.5),
    }


def reference(x_prompt, x_sample, cache_swa_k, cache_swa_v, state_lru_conv, state_lru_h,
              state_rwkv_shift, state_rwkv_wkv, cache_mem_k, cache_mem_v, mem_prompt,
              swa_w_qkv, swa_sinks, swa_w_o,
              lru_w_in, lru_b_in, lru_conv_w, lru_conv_b, lru_w_a, lru_b_a, lru_w_i, lru_b_i, lru_lambda, lru_w_o,
              rwkv_mu, rwkv_w_r, rwkv_w_k, rwkv_w_v, rwkv_w0, rwkv_w1, rwkv_w2, rwkv_a0, rwkv_a1, rwkv_a2,
              rwkv_g1, rwkv_g2, rwkv_k_k, rwkv_k_a, rwkv_r_k, rwkv_gn_g, rwkv_gn_b, rwkv_w_o,
              mem_w_q, mem_w_kv, mem_w_o, ln_g, ln_b,
              router_w, router_b, moe_w_gate, moe_w_up, moe_w_down):
    xp, xs = x_prompt, x_sample
    n_p = xp.shape[0]
    swa_k_p, swa_v_p, swa_k_s, swa_v_s = [], [], [], []
    lru_c_p, lru_h_p, lru_c_s, lru_h_s = [], [], [], []
    rw_x_p, rw_s_p, rw_x_s, rw_s_s = [], [], [], []
    mem_k_p, mem_v_p = [], []
    for layer in range(DEPTH):
        kind, i = layer % N_MIXERS, layer // N_MIXERS
        if kind == 0:
            mp, kp, vp = _swa_prompt(xp, swa_w_qkv[i], swa_sinks[i], swa_w_o[i])
            ms, kn, vn = _swa_sample(xs, cache_swa_k[i], cache_swa_v[i], swa_w_qkv[i], swa_sinks[i], swa_w_o[i])
            swa_k_p.append(kp)
            swa_v_p.append(vp)
            swa_k_s.append(kn)
            swa_v_s.append(vn)
        elif kind == 1:
            lru = (lru_w_in[i], lru_b_in[i], lru_conv_w[i], lru_conv_b[i], lru_w_a[i], lru_b_a[i],
                   lru_w_i[i], lru_b_i[i], lru_lambda[i], lru_w_o[i])
            mp, cp, hp = _rglru(xp, jnp.zeros((n_p, CONV_W - 1, LRU_WIDTH), xp.dtype),
                                jnp.zeros((n_p, LRU_WIDTH), jnp.float32), *lru)
            ms, cn, hn = _rglru(xs, state_lru_conv[i], state_lru_h[i], *lru)
            lru_c_p.append(cp)
            lru_h_p.append(hp)
            lru_c_s.append(cn)
            lru_h_s.append(hn)
        else:
            rw = (rwkv_mu[i], rwkv_w_r[i], rwkv_w_k[i], rwkv_w_v[i], rwkv_w0[i], rwkv_w1[i], rwkv_w2[i],
                  rwkv_a0[i], rwkv_a1[i], rwkv_a2[i], rwkv_g1[i], rwkv_g2[i], rwkv_k_k[i], rwkv_k_a[i],
                  rwkv_r_k[i], rwkv_gn_g[i], rwkv_gn_b[i], rwkv_w_o[i])
            mp, sxp, sp = _rwkv7(xp, jnp.zeros((n_p, D_MODEL), xp.dtype),
                                 jnp.zeros((n_p, RWKV_HEADS, RWKV_HEAD_DIM, RWKV_HEAD_DIM), jnp.float32), *rw)
            ms, sxn, sn = _rwkv7(xs, state_rwkv_shift[i], state_rwkv_wkv[i], *rw)
            rw_x_p.append(sxp)
            rw_s_p.append(sp)
            rw_x_s.append(sxn)
            rw_s_s.append(sn)
        xp = _layer_norm(DEEPNORM_ALPHA * xp + mp, ln_g[layer, 0], ln_b[layer, 0])
        xs = _layer_norm(DEEPNORM_ALPHA * xs + ms, ln_g[layer, 0], ln_b[layer, 0])
        mk, mv = _mem_kv(mem_prompt, mem_w_kv[layer])
        mem_k_p.append(mk)
        mem_v_p.append(mv)
        xp = _layer_norm(DEEPNORM_ALPHA * xp + _mem_attention(xp, mk, mv, mem_w_q[layer], mem_w_o[layer]),
                         ln_g[layer, 1], ln_b[layer, 1])
        xs = _layer_norm(DEEPNORM_ALPHA * xs + _mem_attention(xs, cache_mem_k[layer], cache_mem_v[layer],
                                                              mem_w_q[layer], mem_w_o[layer]),
                         ln_g[layer, 1], ln_b[layer, 1])
        xp = _layer_norm(DEEPNORM_ALPHA * xp + _moe(xp, router_w, router_b, moe_w_gate[layer], moe_w_up[layer],
                                                    moe_w_down[layer]), ln_g[layer, 2], ln_b[layer, 2])
        xs = _layer_norm(DEEPNORM_ALPHA * xs + _moe(xs, router_w, router_b, moe_w_gate[layer], moe_w_up[layer],
                                                    moe_w_down[layer]), ln_g[layer, 2], ln_b[layer, 2])
    return (xp, xs,
            jnp.stack(swa_k_p), jnp.stack(swa_v_p), jnp.stack(lru_c_p), jnp.stack(lru_h_p),
            jnp.stack(rw_x_p), jnp.stack(rw_s_p), jnp.stack(mem_k_p), jnp.stack(mem_v_p),
            jnp.stack(swa_k_s), jnp.stack(swa_v_s), jnp.stack(lru_c_s), jnp.stack(lru_h_s),
            jnp.stack(rw_x_s), jnp.stack(rw_s_s))
```

```python
import functools

import jax
import jax.numpy as jnp
from jax import lax
from jax.experimental import pallas as pl
from jax.experimental.pallas import tpu as pltpu

F32 = jnp.float32
BF16 = jnp.bfloat16

D = 1024
DEPTH = 4
N_MIXERS = 3
HEAD_DIM = 64
SWA_HEADS = D // HEAD_DIM
SWA_KV_HEADS = 4
SWA_GROUP = SWA_HEADS // SWA_KV_HEADS
Q_WIDTH = SWA_HEADS * HEAD_DIM
KV_WIDTH = SWA_KV_HEADS * HEAD_DIM
WINDOW = 128
ROT_DIM = HEAD_DIM // 4
ROPE_THETA = 500000.0
LRU_BLOCKS = 16
CONV_W = 4
LRU_C = 8.0
RWKV_HEADS = 16
RWKV_HD = 64
RWKV_GN_EPS = 64e-5
MEM_HEADS = 4
MEM_HD = D // MEM_HEADS
N_EXPERTS = 16
N_GROUPS = 4
EXPERTS_PER_GROUP = 4
EXPERT_FF = 512
LN_EPS = 1e-5
ALPHA = (2.0 * DEPTH) ** 0.25
NEG_INF = -1e30

LANES = 128
SUBLANES = 8
VMEM_LIMIT = 56 * 1024 * 1024
WKV_CHUNK = 64
N_BUCKETS = N_GROUPS * 6
BUCKET_ROWS = 32


def _cparams(sem):
    return pltpu.CompilerParams(dimension_semantics=sem, vmem_limit_bytes=VMEM_LIMIT)


def _dot(a, b):
    return jnp.dot(a, b, preferred_element_type=F32)


def _dot_nt(a, b):
    return lax.dot_general(a, b, (((1,), (1,)), ((), ())), preferred_element_type=F32)


def _dot_tn(a, b):
    return lax.dot_general(a, b, (((0,), (0,)), ((), ())), preferred_element_type=F32)


def _ln(z, g, b):
    mu = jnp.mean(z, axis=-1, keepdims=True)
    zc = z - mu
    var = jnp.mean(zc * zc, axis=-1, keepdims=True)
    return zc * lax.rsqrt(var + LN_EPS) * g + b


def _softplus(z):
    return jnp.maximum(z, 0.0) + jnp.log1p(jnp.exp(-jnp.abs(z)))


def _sigmoid(z):
    return 1.0 / (1.0 + jnp.exp(-z))


def _full(shape):
    nd = len(shape)
    return pl.BlockSpec(shape, lambda *_: (0,) * nd)


def _mm_kernel(a_ref, w_ref, o_ref):
    o_ref[...] = _dot(a_ref[...].astype(BF16), w_ref[...]).astype(o_ref.dtype)


def _matmul(a, w, tm, out_dtype=F32):
    t, k = a.shape
    n = w.shape[1]
    return pl.pallas_call(
        _mm_kernel, grid=(t // tm,),
        in_specs=[pl.BlockSpec((tm, k), lambda i: (i, 0)), _full((k, n))],
        out_specs=pl.BlockSpec((tm, n), lambda i: (i, 0)),
        out_shape=jax.ShapeDtypeStruct((t, n), out_dtype),
        compiler_params=_cparams(("parallel",)), name="matmul")(a, w)


def _proj_ln_kernel(a_ref, w_ref, x_ref, g_ref, b_ref, o_ref):
    acc = _dot(a_ref[...].astype(BF16), w_ref[...])
    o_ref[...] = _ln(ALPHA * x_ref[...] + acc, g_ref[...], b_ref[...])


def _proj_ln(a, w, x, g, b, tm):
    t, k = a.shape
    return pl.pallas_call(
        _proj_ln_kernel, grid=(t // tm,),
        in_specs=[pl.BlockSpec((tm, k), lambda i: (i, 0)), _full((k, D)),
                  pl.BlockSpec((tm, D), lambda i: (i, 0)), _full((1, D)), _full((1, D))],
        out_specs=pl.BlockSpec((tm, D), lambda i: (i, 0)),
        out_shape=jax.ShapeDtypeStruct((t, D), F32),
        compiler_params=_cparams(("parallel",)), name="proj_ln")(a, w, x, g, b)


def _rope_tables(pos):
    half = ROT_DIM // 2
    inv_freq = ROPE_THETA ** (-jnp.arange(half, dtype=F32) / half)
    ang = pos.astype(F32)[:, None] * inv_freq
    cos, sin = jnp.cos(ang), jnp.sin(ang)
    one = jnp.ones((pos.shape[0], HEAD_DIM - ROT_DIM), F32)
    zero = jnp.zeros((pos.shape[0], HEAD_DIM - ROT_DIM), F32)
    zh = jnp.zeros_like(sin)
    c = jnp.concatenate([cos, cos, one], axis=1)
    s1 = jnp.concatenate([-sin, zh, zero], axis=1)
    s2 = jnp.concatenate([zh, sin, zero], axis=1)
    rep = LANES // HEAD_DIM
    return jnp.tile(c, (1, rep)), jnp.tile(s1, (1, rep)), jnp.tile(s2, (1, rep))


def _swa_qkv_kernel(x_ref, w_ref, c_ref, s1_ref, s2_ref, q_ref, k_ref, v_ref, kv_ref, *, tm, keep):
    acc = _dot(x_ref[...].astype(BF16), w_ref[...])
    c, s1, s2 = c_ref[...], s1_ref[...], s2_ref[...]
    half = ROT_DIM // 2
    n_q = Q_WIDTH // LANES
    n_k = KV_WIDTH // LANES
    for cg in range(n_q + n_k):
        xg = acc[:, cg * LANES:(cg + 1) * LANES]
        rot = xg * c + pltpu.roll(xg, LANES - half, 1) * s1 + pltpu.roll(xg, half, 1) * s2
        if cg < n_q:
            q_ref[:, cg * LANES:(cg + 1) * LANES] = rot.astype(q_ref.dtype)
        else:
            ck = cg - n_q
            k_ref[:, ck * LANES:(ck + 1) * LANES] = rot.astype(k_ref.dtype)
            kv_ref[0, :, ck * LANES:(ck + 1) * LANES] = rot[tm - keep:, :]
    v = acc[:, Q_WIDTH + KV_WIDTH:]
    v_ref[...] = v.astype(v_ref.dtype)
    kv_ref[0, :, KV_WIDTH:] = v[tm - keep:, :]


def _swa_qkv(x, w_qkv, pos, n_seq, tm, keep, qdtype):
    t = x.shape[0]
    s = t // n_seq
    nb = s // tm
    c, s1, s2 = _rope_tables(pos)
    row = lambda n, i: (n * nb + i, 0)
    tab = pl.BlockSpec((tm, LANES), lambda n, i: (i, 0))
    kern = functools.partial(_swa_qkv_kernel, tm=tm, keep=keep)
    return pl.pallas_call(
        kern, grid=(n_seq, nb),
        in_specs=[pl.BlockSpec((tm, D), row), _full((D, Q_WIDTH + 2 * KV_WIDTH)), tab, tab, tab],
        out_specs=[pl.BlockSpec((tm, Q_WIDTH), row), pl.BlockSpec((tm, KV_WIDTH), row),
                   pl.BlockSpec((tm, KV_WIDTH), row),
                   pl.BlockSpec((1, keep, 2 * KV_WIDTH), lambda n, i: (n, 0, 0))],
        out_shape=[jax.ShapeDtypeStruct((t, Q_WIDTH), qdtype), jax.ShapeDtypeStruct((t, KV_WIDTH), qdtype),
                   jax.ShapeDtypeStruct((t, KV_WIDTH), qdtype),
                   jax.ShapeDtypeStruct((n_seq, keep, 2 * KV_WIDTH), F32)],
        compiler_params=_cparams(("parallel", "arbitrary")), name="swa_qkv")(x, w_qkv, c, s1, s2)


def _swa_attn_kernel(sink_ref, q_ref, kp_ref, kc_ref, vp_ref, vc_ref, o_ref):
    j = pl.program_id(1)
    r = lax.broadcasted_iota(jnp.int32, (WINDOW, WINDOW), 0)
    c = lax.broadcasted_iota(jnp.int32, (WINDOW, WINDOW), 1)
    cur_ok = c <= r
    prev_ok = jnp.logical_and(c > r, j > 0)
    scale = HEAD_DIM ** -0.5
    for h in range(SWA_KV_HEADS):
        sl = slice(h * HEAD_DIM, (h + 1) * HEAD_DIM)
        kp, kc, vp, vc = kp_ref[:, sl], kc_ref[:, sl], vp_ref[:, sl], vc_ref[:, sl]
        for g in range(SWA_GROUP):
            hq = h * SWA_GROUP + g
            qs = slice(hq * HEAD_DIM, (hq + 1) * HEAD_DIM)
            qh = q_ref[:, qs]
            sp = jnp.where(prev_ok, _dot_nt(qh, kp) * scale, NEG_INF)
            sc = jnp.where(cur_ok, _dot_nt(qh, kc) * scale, NEG_INF)
            sink = sink_ref[hq]
            m = jnp.maximum(jnp.maximum(jnp.max(sp, axis=-1, keepdims=True),
                                        jnp.max(sc, axis=-1, keepdims=True)), sink)
            pp = jnp.exp(sp - m)
            pc = jnp.exp(sc - m)
            den = (jnp.sum(pp, axis=-1, keepdims=True) + jnp.sum(pc, axis=-1, keepdims=True)
                   + jnp.exp(sink - m))
            o = _dot(pp.astype(BF16), vp) + _dot(pc.astype(BF16), vc)
            o_ref[:, qs] = (o / den).astype(o_ref.dtype)


def _swa_attn_prompt(q, k, v, sinks, n_seq):
    t = q.shape[0]
    nb = t // n_seq // WINDOW
    cur = lambda n, j: (n * nb + j, 0)
    prev = lambda n, j: (n * nb + jnp.maximum(j - 1, 0), 0)
    kv = lambda im: pl.BlockSpec((WINDOW, KV_WIDTH), im)
    return pl.pallas_call(
        _swa_attn_kernel, grid=(n_seq, nb),
        in_specs=[pl.BlockSpec(memory_space=pltpu.SMEM), pl.BlockSpec((WINDOW, Q_WIDTH), cur),
                  kv(prev), kv(cur), kv(prev), kv(cur)],
        out_specs=pl.BlockSpec((WINDOW, Q_WIDTH), cur),
        out_shape=jax.ShapeDtypeStruct((t, Q_WIDTH), BF16),
        compiler_params=_cparams(("parallel", "arbitrary")), name="swa_attn")(sinks, q, k, k, v, v)


def _swa_sample_kernel(sink_ref, q_ref, kn_ref, vn_ref, ck_ref, cv_ref, o_ref, *, wb):
    kidx = lax.broadcasted_iota(jnp.int32, (1, wb, 1), 1)
    valid = (wb - kidx) < WINDOW
    scale = HEAD_DIM ** -0.5
    for h in range(SWA_KV_HEADS):
        sl = slice(h * HEAD_DIM, (h + 1) * HEAD_DIM)
        ck, cv = ck_ref[:, :, sl], cv_ref[:, :, sl]
        kn, vn = kn_ref[:, :, sl], vn_ref[:, :, sl]
        for g in range(SWA_GROUP):
            hq = h * SWA_GROUP + g
            qs = slice(hq * HEAD_DIM, (hq + 1) * HEAD_DIM)
            qh = q_ref[:, :, qs]
            s = jnp.where(valid, jnp.sum(ck * qh, axis=-1, keepdims=True) * scale, NEG_INF)
            sn = jnp.sum(kn * qh, axis=-1, keepdims=True) * scale
            sink = sink_ref[hq]
            m = jnp.maximum(jnp.maximum(jnp.max(s, axis=1, keepdims=True), sn), sink)
            p = jnp.exp(s - m)
            pn = jnp.exp(sn - m)
            den = jnp.sum(p, axis=1, keepdims=True) + pn + jnp.exp(sink - m)
            o = jnp.sum(p * cv, axis=1, keepdims=True) + pn * vn
            o_ref[:, :, qs] = o / den


def _swa_attn_sample(q, kn, vn, cache_k, cache_v, sinks, bs):
    b, wb = cache_k.shape[0], cache_k.shape[1]
    blk3 = lambda w: pl.BlockSpec((bs, 1, w), lambda i: (i, 0, 0))
    cblk = pl.BlockSpec((bs, wb, KV_WIDTH), lambda i: (i, 0, 0))
    out = pl.pallas_call(
        functools.partial(_swa_sample_kernel, wb=wb), grid=(b // bs,),
        in_specs=[pl.BlockSpec(memory_space=pltpu.SMEM), blk3(Q_WIDTH), blk3(KV_WIDTH), blk3(KV_WIDTH), cblk, cblk],
        out_specs=blk3(Q_WIDTH), out_shape=jax.ShapeDtypeStruct((b, 1, Q_WIDTH), F32),
        compiler_params=_cparams(("parallel",)), name="swa_sample")(
            sinks, q.reshape(b, 1, Q_WIDTH), kn.reshape(b, 1, KV_WIDTH), vn.reshape(b, 1, KV_WIDTH),
            cache_k.reshape(b, wb, KV_WIDTH), cache_v.reshape(b, wb, KV_WIDTH))
    return out.reshape(b, Q_WIDTH)


def _gelu_tanh(x):
    return 0.5 * x * (1.0 + jnp.tanh(0.7978845608028654 * (x + 0.044715 * x * x * x)))


def _lru_gates(xc, wa_ref, ba, wi_ref, bi, lam):
    xcb = xc.astype(BF16)
    gw = wa_ref.shape[1]
    ra, ia = [], []
    for gi in range(wa_ref.shape[0]):
        xs = xcb[:, gi * gw:(gi + 1) * gw]
        ra.append(_dot(xs, wa_ref[gi]))
        ia.append(_dot(xs, wi_ref[gi]))
    r = _sigmoid(jnp.concatenate(ra, axis=-1) + ba)
    ig = _sigmoid(jnp.concatenate(ia, axis=-1) + bi)
    log_a = -LRU_C * r * _softplus(-lam)
    a = jnp.exp(log_a)
    b = jnp.sqrt(-jnp.tanh(log_a) * (a * a + 1.0)) * (ig * xc)
    return a, b


def _shift_rows(ext, s, tm):
    return pltpu.roll(ext, s, 0)[SUBLANES:SUBLANES + tm]


def _lru_prompt_kernel(x_ref, win_ref, bin_ref, cw_ref, cb_ref, wa_ref, ba_ref, wi_ref, bi_ref, lam_ref,
                       wo_ref, g_ref, b_ref, o_ref, conv_ref, hl_ref, cx_ref, ch_ref, *, tm):
    i = pl.program_id(1)

    @pl.when(i == 0)
    def _():
        cx_ref[...] = jnp.zeros_like(cx_ref)
        ch_ref[...] = jnp.zeros_like(ch_ref)

    x = x_ref[...]
    xy = _dot(x.astype(BF16), win_ref[...]) + bin_ref[...]
    xb = xy[:, :D]
    y_gate = _gelu_tanh(xy[:, D:])
    ext = jnp.concatenate([cx_ref[...], xb], axis=0)
    cw = cw_ref[...]
    xc = cb_ref[...] + xb * cw[CONV_W - 1:CONV_W]
    for s in range(1, CONV_W):
        xc = xc + _shift_rows(ext, s, tm) * cw[CONV_W - 1 - s:CONV_W - s]
    cx_ref[...] = xb[tm - SUBLANES:]
    conv_ref[0] = xb[tm - SUBLANES:]

    a, b = _lru_gates(xc, wa_ref, ba_ref[...], wi_ref, bi_ref[...], lam_ref[...])
    row = lax.broadcasted_iota(jnp.int32, (tm, 1), 0)
    s = 1
    while s < tm:
        keep = row >= s
        a_sh = jnp.where(keep, pltpu.roll(a, s, 0), 1.0)
        b_sh = jnp.where(keep, pltpu.roll(b, s, 0), 0.0)
        b = a * b_sh + b
        a = a * a_sh
        s *= 2
    h = a * ch_ref[SUBLANES - 1:SUBLANES, :] + b
    ch_ref[...] = h[tm - SUBLANES:]
    hl_ref[0] = h[tm - SUBLANES:]
    acc = _dot((h * y_gate).astype(BF16), wo_ref[...])
    o_ref[...] = _ln(ALPHA * x + acc, g_ref[...], b_ref[...])


def _lru_weights(w_in, b_in, conv_w, conv_b, w_a, b_a, w_i, b_i, lam, w_o):
    gsz = 4
    ng = LRU_BLOCKS // gsz
    bw = D // LRU_BLOCKS

    def grouped(w):
        w4 = w.reshape(ng, gsz, bw, bw)
        return jnp.einsum('gaij,ab->gaibj', w4, jnp.eye(gsz, dtype=w.dtype)).reshape(ng, gsz * bw, gsz * bw).astype(BF16)

    row = lambda v: v.reshape(1, -1)
    return (w_in.astype(BF16), row(b_in), conv_w, row(conv_b), grouped(w_a), row(b_a), grouped(w_i), row(b_i),
            row(lam), w_o.astype(BF16))


def _lru_prompt(x, wts, g, b, n_seq, tm):
    t = x.shape[0]
    nb = t // n_seq // tm
    row = lambda n, i: (n * nb + i, 0)
    last = pl.BlockSpec((1, SUBLANES, D), lambda n, i: (n, 0, 0))
    w_in, b_in, cw, cb, wa, ba, wi, bi, lam, wo = wts
    return pl.pallas_call(
        functools.partial(_lru_prompt_kernel, tm=tm), grid=(n_seq, nb),
        in_specs=[pl.BlockSpec((tm, D), row), _full(w_in.shape), _full(b_in.shape), _full(cw.shape), _full(cb.shape),
                  _full(wa.shape), _full(ba.shape), _full(wi.shape), _full(bi.shape), _full(lam.shape),
                  _full(wo.shape), _full((1, D)), _full((1, D))],
        out_specs=[pl.BlockSpec((tm, D), row), last, last],
        out_shape=[jax.ShapeDtypeStruct((t, D), F32), jax.ShapeDtypeStruct((n_seq, SUBLANES, D), F32),
                   jax.ShapeDtypeStruct((n_seq, SUBLANES, D), F32)],
        scratch_shapes=[pltpu.VMEM((SUBLANES, D), F32), pltpu.VMEM((SUBLANES, D), F32)],
        compiler_params=_cparams(("parallel", "arbitrary")), name="lru_prompt")(x, *wts, g, b)


def _lru_sample_kernel(x_ref, c0_ref, c1_ref, c2_ref, h0_ref, win_ref, bin_ref, cw_ref, cb_ref, wa_ref, ba_ref,
                       wi_ref, bi_ref, lam_ref, wo_ref, g_ref, b_ref, o_ref, xb_ref, h_ref):
    x = x_ref[...]
    xy = _dot(x.astype(BF16), win_ref[...]) + bin_ref[...]
    xb = xy[:, :D]
    y_gate = _gelu_tanh(xy[:, D:])
    cw = cw_ref[...]
    xc = (cb_ref[...] + c0_ref[...] * cw[0:1] + c1_ref[...] * cw[1:2] + c2_ref[...] * cw[2:3] + xb * cw[3:4])
    a, b = _lru_gates(xc, wa_ref, ba_ref[...], wi_ref, bi_ref[...], lam_ref[...])
    h = a * h0_ref[...] + b
    xb_ref[...] = xb
    h_ref[...] = h
    acc = _dot((h * y_gate).astype(BF16), wo_ref[...])
    o_ref[...] = _ln(ALPHA * x + acc, g_ref[...], b_ref[...])


def _lru_sample(x, conv_state, h0, wts, g, b):
    t = x.shape[0]
    args = (x, conv_state[:, 0], conv_state[:, 1], conv_state[:, 2], h0, *wts, g, b)
    sd = jax.ShapeDtypeStruct((t, D), F32)
    return pl.pallas_call(
        _lru_sample_kernel, grid=(1,),
        in_specs=[_full(a.shape) for a in args],
        out_specs=[_full((t, D))] * 3, out_shape=[sd, sd, sd],
        compiler_params=_cparams(("arbitrary",)), name="lru_sample")(*args)


def _rwkv_pre_kernel(x_ref, xp_ref, mu_ref, wr_ref, wk_ref, wv_ref, w0_ref, w1_ref, w2_ref, a0_ref, a1_ref, a2_ref,
                     g1_ref, g2_ref, r_ref, k_ref, v_ref, a_ref, ld_ref, g_ref, *scratch, tm, seq):
    x = x_ref[...]
    if seq:
        cx_ref, = scratch
        i = pl.program_id(1)

        @pl.when(i == 0)
        def _():
            cx_ref[...] = xp_ref[0]

        x_prev = _shift_rows(jnp.concatenate([cx_ref[...], x], axis=0), 1, tm)
        cx_ref[...] = x[tm - SUBLANES:]
    else:
        x_prev = xp_ref[...]
    xx = x_prev - x
    mu = mu_ref[...]
    mix = lambda j: (x + xx * mu[j:j + 1]).astype(BF16)
    r_ref[...] = _dot(mix(0), wr_ref[...]).astype(r_ref.dtype)
    wl = _dot(jnp.tanh(_dot(mix(1), w1_ref[...])).astype(BF16), w2_ref[...])
    w = -_softplus(-(w0_ref[...] + wl)) - 0.5
    ld_ref[...] = -jnp.exp(w)
    k_ref[...] = _dot(mix(2), wk_ref[...]).astype(k_ref.dtype)
    v_ref[...] = _dot(mix(3), wv_ref[...]).astype(v_ref.dtype)
    al = _dot(_dot(mix(4), a1_ref[...]).astype(BF16), a2_ref[...])
    a_ref[...] = _sigmoid(a0_ref[...] + al).astype(a_ref.dtype)
    g_ref[...] = _dot(_sigmoid(_dot(mix(5), g1_ref[...])).astype(BF16), g2_ref[...]).astype(g_ref.dtype)


def _rwkv_pre(x, x_prev, wts, n_seq, tm, seq, dtype):
    t = x.shape[0]
    nb = t // n_seq // tm
    row = lambda n, i: (n * nb + i, 0)
    xp_spec = pl.BlockSpec((1, SUBLANES, D), lambda n, i: (n, 0, 0)) if seq else pl.BlockSpec((tm, D), row)
    sd = lambda dt: jax.ShapeDtypeStruct((t, D), dt)
    blk = pl.BlockSpec((tm, D), row)
    return pl.pallas_call(
        functools.partial(_rwkv_pre_kernel, tm=tm, seq=seq), grid=(n_seq, nb),
        in_specs=[blk, xp_spec] + [_full(w.shape) for w in wts],
        out_specs=[blk] * 6,
        out_shape=[sd(dtype), sd(dtype), sd(dtype), sd(dtype), sd(F32), sd(dtype)],
        scratch_shapes=[pltpu.VMEM((SUBLANES, D), F32)] if seq else [],
        compiler_params=_cparams(("parallel", "arbitrary")), name="rwkv_pre")(x, x_prev, *wts)


def _seg_sum(x, first):
    s0 = jnp.sum(jnp.where(first, x, 0.0), axis=-1, keepdims=True)
    s1 = jnp.sum(jnp.where(first, 0.0, x), axis=-1, keepdims=True)
    return jnp.where(first, s0, s1)


def _wkv_kernel(r_ref, k_ref, v_ref, a_ref, ld_ref, g_ref, kk_ref, ka_ref, rk_ref, gg_ref, gb_ref,
                o_ref, s_ref, st_ref):
    c = pl.program_id(1)
    L = WKV_CHUNK
    P2 = 2 * L

    @pl.when(c == 0)
    def _():
        st_ref[...] = jnp.zeros_like(st_ref)

    ld_all = ld_ref[...]
    tri = (lax.broadcasted_iota(jnp.int32, (L, L), 0) >= lax.broadcasted_iota(jnp.int32, (L, L), 1)).astype(BF16)
    hi = ld_all.astype(BF16)
    r1 = ld_all - hi.astype(F32)
    mid = r1.astype(BF16)
    lo = (r1 - mid.astype(F32)).astype(BF16)
    cum_all = _dot(tri, hi) + _dot(tri, mid) + _dot(tri, lo)

    lane = lax.broadcasted_iota(jnp.int32, (1, LANES), 1)
    first = lane < RWKV_HD
    ri = lax.broadcasted_iota(jnp.int32, (P2, P2), 0)
    ci = lax.broadcasted_iota(jnp.int32, (P2, P2), 1)
    same_head = (ri // L) == (ci // L)
    rt, ct = ri % L, ci % L
    strict = jnp.logical_and(same_head, rt > ct)
    incl = jnp.logical_and(same_head, rt >= ct)
    eye = ri == ci

    def stack(xv):
        return jnp.concatenate([jnp.where(first, xv, 0.0), jnp.where(first, 0.0, xv)], axis=0).astype(BF16)

    for p in range(RWKV_HEADS // 2):
        sl = slice(p * LANES, (p + 1) * LANES)
        rp, kp, vp, ap = (ref[:, sl].astype(F32) for ref in (r_ref, k_ref, v_ref, a_ref))
        ldp, cum = ld_all[:, sl], cum_all[:, sl]
        kk = kp * kk_ref[:, sl]
        kk = kk / jnp.maximum(jnp.sqrt(_seg_sum(kk * kk, first)), 1e-12)
        kmod = kp * (1.0 + (ap - 1.0) * ka_ref[:, sl])
        bp = kk * ap
        cum_l = cum[L - 1:L, :]
        g_inv = jnp.exp(-cum)
        g_to_end = jnp.exp(cum_l - cum)
        ws = stack(kk * jnp.exp(cum - ldp))
        us = stack(bp * g_inv)
        ks = stack(kmod * g_inv)
        rs = stack(rp * jnp.exp(cum))
        ul = stack(bp * g_to_end)
        kl = stack(kmod * g_to_end)
        vs = stack(vp)

        n_mat = jnp.where(strict, _dot_nt(ws, us), 0.0)
        m_mat = jnp.where(strict, _dot_nt(ws, ks), 0.0).astype(BF16)
        nr_mat = jnp.where(incl, _dot_nt(rs, us), 0.0).astype(BF16)
        mr_mat = jnp.where(incl, _dot_nt(rs, ks), 0.0).astype(BF16)

        x_inv = jnp.where(eye, 1.0, 0.0).astype(F32)
        sz = 1
        while sz < L:
            sub = jnp.logical_and(jnp.logical_and((rt // sz) % 2 == 1, (ct // sz) % 2 == 0),
                                  (rt // (2 * sz)) == (ct // (2 * sz)))
            c_mat = jnp.where(jnp.logical_and(sub, same_head), n_mat, 0.0).astype(BF16)
            xb = x_inv.astype(BF16)
            x_inv = x_inv - _dot(_dot(xb, c_mat).astype(BF16), xb)
            sz *= 2

        a0 = st_ref[p]
        a0b = a0.astype(BF16)
        rhs = _dot(ws, a0b) + _dot(m_mat, vs)
        pm = (-_dot(x_inv.astype(BF16), rhs.astype(BF16))).astype(BF16)
        o_st = _dot(rs, a0b) + _dot(nr_mat, pm) + _dot(mr_mat, vs)
        g_l = jnp.exp(cum_l)
        g_col = jnp.sum(jnp.where(eye, jnp.broadcast_to(g_l, (P2, P2)), 0.0), axis=-1, keepdims=True)
        st_ref[p] = g_col * a0 + _dot_tn(ul, pm) + _dot_tn(kl, vs)

        o = o_st[:L] + o_st[L:]
        inv_n = 1.0 / RWKV_HD
        mu = _seg_sum(o, first) * inv_n
        oc = o - mu
        var = _seg_sum(oc * oc, first) * inv_n
        on = oc * lax.rsqrt(var + RWKV_GN_EPS) * gg_ref[:, sl] + gb_ref[:, sl]
        bonus = _seg_sum(rp * kmod * rk_ref[:, sl], first) * vp
        o_ref[:, sl] = ((on + bonus) * g_ref[:, sl].astype(F32)).astype(o_ref.dtype)

    s_ref[0] = st_ref[...]


def _wkv_prompt(r, k, v, a, ld, g, hp, n_seq):
    t = r.shape[0]
    L = WKV_CHUNK
    nc = t // n_seq // L
    row = lambda n, c: (n * nc + c, 0)
    blk = pl.BlockSpec((L, D), row)
    npair = RWKV_HEADS // 2
    return pl.pallas_call(
        _wkv_kernel, grid=(n_seq, nc),
        in_specs=[blk] * 6 + [_full((1, D))] * 5,
        out_specs=[blk, pl.BlockSpec((1, npair, LANES, LANES), lambda n, c: (n, 0, 0, 0))],
        out_shape=[jax.ShapeDtypeStruct((t, D), BF16), jax.ShapeDtypeStruct((n_seq, npair, LANES, LANES), F32)],
        scratch_shapes=[pltpu.VMEM((npair, LANES, LANES), F32)],
        compiler_params=_cparams(("parallel", "arbitrary")), name="wkv_chunk")(r, k, v, a, ld, g, *hp)


def _wkv_sample_kernel(r_ref, k_ref, v_ref, a_ref, ld_ref, g_ref, s_ref, kk_ref, ka_ref, rk_ref, gg_ref, gb_ref,
                       o_ref, so_ref):
    hd = RWKV_HD
    eye = lax.broadcasted_iota(jnp.int32, (1, hd, hd), 1) == lax.broadcasted_iota(jnp.int32, (1, hd, hd), 2)
    for h in range(RWKV_HEADS):
        hs = slice(h, h + 1)
        r, k, v, a, ld, g = (ref[:, hs, :] for ref in (r_ref, k_ref, v_ref, a_ref, ld_ref, g_ref))
        s = s_ref[:, h]
        kk = k * kk_ref[hs, :]
        kk = kk / jnp.maximum(jnp.sqrt(jnp.sum(kk * kk, axis=-1, keepdims=True)), 1e-12)
        kmod = k * (1.0 + (a - 1.0) * ka_ref[hs, :])
        skk = jnp.sum(s * kk, axis=-1, keepdims=True)
        v_col = jnp.sum(jnp.where(eye, v, 0.0), axis=-1, keepdims=True)
        s_new = s * jnp.exp(ld) - skk * (kk * a) + v_col * kmod
        so_ref[:, h] = s_new
        o_col = jnp.sum(s_new * r, axis=-1, keepdims=True)
        o = jnp.sum(jnp.where(eye, o_col, 0.0), axis=1, keepdims=True)
        mu = jnp.mean(o, axis=-1, keepdims=True)
        oc = o - mu
        var = jnp.mean(oc * oc, axis=-1, keepdims=True)
        on = oc * lax.rsqrt(var + RWKV_GN_EPS) * gg_ref[hs, :] + gb_ref[hs, :]
        bonus = jnp.sum(r * kmod * rk_ref[hs, :], axis=-1, keepdims=True) * v
        o_ref[:, hs, :] = (on + bonus) * g


def _wkv_sample(r, k, v, a, ld, g, state, hp, bs):
    b = r.shape[0]
    h3 = lambda z: z.reshape(b, RWKV_HEADS, RWKV_HD)
    blk = pl.BlockSpec((bs, RWKV_HEADS, RWKV_HD), lambda i: (i, 0, 0))
    sblk = pl.BlockSpec((bs, RWKV_HEADS, RWKV_HD, RWKV_HD), lambda i: (i, 0, 0, 0))
    hp3 = [z.reshape(RWKV_HEADS, RWKV_HD) for z in hp]
    o, s_new = pl.pallas_call(
        _wkv_sample_kernel, grid=(b // bs,),
        in_specs=[blk] * 6 + [sblk] + [_full((RWKV_HEADS, RWKV_HD))] * 5,
        out_specs=[blk, sblk],
        out_shape=[jax.ShapeDtypeStruct((b, RWKV_HEADS, RWKV_HD), F32), jax.ShapeDtypeStruct(state.shape, F32)],
        compiler_params=_cparams(("parallel",)), name="wkv_sample")(
            h3(r), h3(k), h3(v), h3(a), h3(ld), h3(g), state, *hp3)
    return o.reshape(b, D), s_new


def _mem_prompt_kernel(x_ref, wq_ref, mk_ref, mv_ref, wo_ref, g_ref, b_ref, o_ref):
    x = x_ref[...]
    q = _dot(x.astype(BF16), wq_ref[...]).astype(BF16)
    scale = MEM_HD ** -0.5
    outs = []
    for h in range(MEM_HEADS):
        sl = slice(h * MEM_HD, (h + 1) * MEM_HD)
        s = _dot_nt(q[:, sl], mk_ref[0, :, sl]) * scale
        p = jnp.exp(s - jnp.max(s, axis=-1, keepdims=True))
        den = jnp.sum(p, axis=-1, keepdims=True)
        outs.append((_dot(p.astype(BF16), mv_ref[0, :, sl]) / den).astype(BF16))
    acc = _dot(jnp.concatenate(outs, axis=-1), wo_ref[...])
    o_ref[...] = _ln(ALPHA * x + acc, g_ref[...], b_ref[...])


def _mem_attn_prompt(x, w_q, mk, mv, w_o, g, b, n_seq, tm):
    t = x.shape[0]
    nb = t // n_seq // tm
    m = mk.shape[1]
    row = lambda n, i: (n * nb + i, 0)
    mem = pl.BlockSpec((1, m, D), lambda n, i: (n, 0, 0))
    return pl.pallas_call(
        _mem_prompt_kernel, grid=(n_seq, nb),
        in_specs=[pl.BlockSpec((tm, D), row), _full((D, D)), mem, mem, _full((D, D)), _full((1, D)), _full((1, D))],
        out_specs=pl.BlockSpec((tm, D), row), out_shape=jax.ShapeDtypeStruct((t, D), F32),
        compiler_params=_cparams(("parallel", "arbitrary")), name="mem_attn")(x, w_q, mk, mv, w_o, g, b)


def _mem_sample_kernel(q_ref, ck_ref, cv_ref, o_ref):
    scale = MEM_HD ** -0.5
    for h in range(MEM_HEADS):
        sl = slice(h * MEM_HD, (h + 1) * MEM_HD)
        s = jnp.sum(ck_ref[:, :, sl] * q_ref[:, :, sl], axis=-1, keepdims=True) * scale
        p = jnp.exp(s - jnp.max(s, axis=1, keepdims=True))
        den = jnp.sum(p, axis=1, keepdims=True)
        o_ref[:, :, sl] = jnp.sum(p * cv_ref[:, :, sl], axis=1, keepdims=True) / den


def _mem_attn_sample(q, cache_k, cache_v, bs):
    b, m = cache_k.shape[0], cache_k.shape[1]
    qb = pl.BlockSpec((bs, 1, D), lambda i: (i, 0, 0))
    cb = pl.BlockSpec((bs, m, D), lambda i: (i, 0, 0))
    out = pl.pallas_call(
        _mem_sample_kernel, grid=(b // bs,), in_specs=[qb, cb, cb], out_specs=qb,
        out_shape=jax.ShapeDtypeStruct((b, 1, D), F32),
        compiler_params=_cparams(("parallel",)), name="mem_sample")(
            q.reshape(b, 1, D), cache_k.reshape(b, m, D), cache_v.reshape(b, m, D))
    return out.reshape(b, D)


_PAIRS = ((0, 1), (0, 2), (0, 3), (1, 2), (1, 3), (2, 3))


def _router_kernel(x_ref, rw_ref, rb_ref, bucket_ref, rank_ref, cnt_ref, base_ref, *, tm):
    i = pl.program_id(0)

    @pl.when(i == 0)
    def _():
        base_ref[...] = jnp.zeros_like(base_ref)

    logits = _dot_nt(rw_ref[...], x_ref[...].astype(BF16))
    e = jnp.exp(logits - jnp.max(logits, axis=0, keepdims=True))
    sel = e / jnp.sum(e, axis=0, keepdims=True) + rb_ref[...]
    s = [sel[j:j + 1, :] for j in range(N_EXPERTS)]
    neg = jnp.float32(-jnp.inf)

    best = jnp.zeros((1, tm), jnp.int32)
    best_score = None
    for gi in range(N_GROUPS):
        s0, s1, s2, s3 = s[4 * gi:4 * gi + 4]
        hi01, lo01, hi23, lo23 = jnp.maximum(s0, s1), jnp.minimum(s0, s1), jnp.maximum(s2, s3), jnp.minimum(s2, s3)
        score = jnp.maximum(hi01, hi23) + jnp.maximum(jnp.minimum(hi01, hi23), jnp.maximum(lo01, lo23))
        if gi == 0:
            best_score = score
        else:
            take = score > best_score
            best = jnp.where(take, gi, best)
            best_score = jnp.where(take, score, best_score)
    vals = []
    for j in range(EXPERTS_PER_GROUP):
        vj = s[j]
        for gi in range(1, N_GROUPS):
            vj = jnp.where(best == gi, s[4 * gi + j], vj)
        vals.append(vj)

    def argmax4(v):
        idx, mx = jnp.zeros((1, tm), jnp.int32), v[0]
        for j in range(1, EXPERTS_PER_GROUP):
            take = v[j] > mx
            idx = jnp.where(take, j, idx)
            mx = jnp.where(take, v[j], mx)
        return idx

    i1 = argmax4(vals)
    i2 = argmax4([jnp.where(i1 == j, neg, vals[j]) for j in range(EXPERTS_PER_GROUP)])
    lo, hi = jnp.minimum(i1, i2), jnp.maximum(i1, i2)
    pair = jnp.zeros((1, tm), jnp.int32)
    for pi, (pa, pb) in enumerate(_PAIRS):
        pair = jnp.where(jnp.logical_and(lo == pa, hi == pb), pi, pair)
    bucket = best * len(_PAIRS) + pair
    bucket_ref[0] = bucket

    onehot = (lax.broadcasted_iota(jnp.int32, (BUCKET_ROWS, tm), 0) == bucket).astype(F32)
    upper = (lax.broadcasted_iota(jnp.int32, (tm, tm), 0) <= lax.broadcasted_iota(jnp.int32, (tm, tm), 1)).astype(BF16)
    cum = _dot(onehot.astype(BF16), upper)
    base = base_ref[...]
    rank = jnp.sum(onehot * (cum + base), axis=0, keepdims=True) - 1.0
    rank_ref[0] = rank.astype(jnp.int32)
    base = base + jnp.sum(onehot, axis=1, keepdims=True)
    base_ref[...] = base
    cnt_ref[...] = jnp.broadcast_to(base, cnt_ref.shape)


def _router(x, rw_t, rb, tm):
    t = x.shape[0]
    nb = t // tm
    ib = pl.BlockSpec((1, 1, tm), lambda i: (i, 0, 0))
    bucket, rank, cnt = pl.pallas_call(
        functools.partial(_router_kernel, tm=tm), grid=(nb,),
        in_specs=[pl.BlockSpec((tm, D), lambda i: (i, 0)), _full(rw_t.shape), _full(rb.shape)],
        out_specs=[ib, ib, _full((BUCKET_ROWS, LANES))],
        out_shape=[jax.ShapeDtypeStruct((nb, 1, tm), jnp.int32), jax.ShapeDtypeStruct((nb, 1, tm), jnp.int32),
                   jax.ShapeDtypeStruct((BUCKET_ROWS, LANES), F32)],
        scratch_shapes=[pltpu.VMEM((BUCKET_ROWS, 1), F32)],
        compiler_params=_cparams(("arbitrary",)), name="router")(x, rw_t, rb)
    return bucket.reshape(t), rank.reshape(t), cnt[:N_BUCKETS, 0].astype(jnp.int32)


def _dispatch_kernel(dest_ref, x_hbm, init_hbm, o_hbm, sem, *, tc):
    del init_hbm
    base = pl.program_id(0) * tc

    def copy(t):
        return pltpu.make_async_copy(x_hbm.at[pl.ds(base + t, 1)], o_hbm.at[pl.ds(dest_ref[base + t], 1)], sem)

    def start(t, carry):
        copy(t).start()
        return carry

    def wait(t, carry):
        copy(t).wait()
        return carry

    lax.fori_loop(0, tc, start, 0)
    lax.fori_loop(0, tc, wait, 0)


def _dispatch(x, dest, rows, tc):
    t = x.shape[0]
    init = jnp.zeros((rows, D), x.dtype)
    anyspec = pl.BlockSpec(memory_space=pl.ANY)
    return pl.pallas_call(
        functools.partial(_dispatch_kernel, tc=tc),
        grid_spec=pltpu.PrefetchScalarGridSpec(
            num_scalar_prefetch=1, grid=(t // tc,), in_specs=[anyspec, anyspec], out_specs=anyspec,
            scratch_shapes=[pltpu.SemaphoreType.DMA(())]),
        out_shape=jax.ShapeDtypeStruct((rows, D), x.dtype),
        input_output_aliases={2: 0},
        compiler_params=_cparams(("arbitrary",)), name="moe_dispatch")(dest, x, init)


def _ffn_kernel(lo_ref, hi_ref, used_ref, x_ref, rw_ref, g0_ref, u0_ref, d0_ref, g1_ref, u1_ref, d1_ref, o_ref):
    i = pl.program_id(0)

    @pl.when(i < used_ref[0])
    def _():
        xb = x_ref[...].astype(BF16)
        logits = _dot(xb, rw_ref[...])
        lane = lax.broadcasted_iota(jnp.int32, logits.shape, 1)
        l_lo = jnp.sum(jnp.where(lane == lo_ref[i], logits, 0.0), axis=-1, keepdims=True)
        l_hi = jnp.sum(jnp.where(lane == hi_ref[i], logits, 0.0), axis=-1, keepdims=True)
        w_lo = _sigmoid(l_lo - l_hi)

        def expert(g_ref, u_ref, d_ref):
            gate = _dot(xb, g_ref[0])
            act = gate * _sigmoid(gate) * _dot(xb, u_ref[0])
            return _dot(act.astype(BF16), d_ref[0])

        y_lo = expert(g0_ref, u0_ref, d0_ref)
        y_hi = expert(g1_ref, u1_ref, d1_ref)
        o_ref[...] = w_lo * y_lo + (1.0 - w_lo) * y_hi

    @pl.when(i >= used_ref[0])
    def _():
        o_ref[...] = jnp.zeros_like(o_ref)


def _ffn(x_rows, blk_lo, blk_hi, n_used, rw, w_gate, w_up, w_down, blk):
    rows = x_rows.shape[0]
    nblk = rows // blk
    wg = lambda sel: pl.BlockSpec((1, D, EXPERT_FF), lambda i, lo, hi, used: ((lo, hi)[sel][i], 0, 0))
    wd = lambda sel: pl.BlockSpec((1, EXPERT_FF, D), lambda i, lo, hi, used: ((lo, hi)[sel][i], 0, 0))
    rowb = pl.BlockSpec((blk, D), lambda i, lo, hi, used: (i, 0))
    return pl.pallas_call(
        _ffn_kernel,
        grid_spec=pltpu.PrefetchScalarGridSpec(
            num_scalar_prefetch=3, grid=(nblk,),
            in_specs=[rowb, pl.BlockSpec(rw.shape, lambda i, lo, hi, used: (0, 0)),
                      wg(0), wg(0), wd(0), wg(1), wg(1), wd(1)],
            out_specs=rowb),
        out_shape=jax.ShapeDtypeStruct((rows, D), F32),
        compiler_params=_cparams(("arbitrary",)), name="moe_ffn")(
            blk_lo, blk_hi, n_used, x_rows, rw, w_gate, w_up, w_down, w_gate, w_up, w_down)


def _combine_ln_kernel(dest_ref, x_ref, y_hbm, g_ref, b_ref, o_ref, ybuf, sem, *, tm):
    base = pl.program_id(0) * tm

    def copy(t):
        return pltpu.make_async_copy(y_hbm.at[pl.ds(dest_ref[base + t], 1)], ybuf.at[pl.ds(t, 1)], sem)

    def start(t, carry):
        copy(t).start()
        return carry

    def wait(t, carry):
        copy(t).wait()
        return carry

    lax.fori_loop(0, tm, start, 0)
    lax.fori_loop(0, tm, wait, 0)
    o_ref[...] = _ln(ALPHA * x_ref[...] + ybuf[...], g_ref[...], b_ref[...])


def _combine_ln(x, y_rows, dest, g, b, tm):
    t = x.shape[0]
    rowb = pl.BlockSpec((tm, D), lambda i, d: (i, 0))
    vec = pl.BlockSpec((1, D), lambda i, d: (0, 0))
    return pl.pallas_call(
        functools.partial(_combine_ln_kernel, tm=tm),
        grid_spec=pltpu.PrefetchScalarGridSpec(
            num_scalar_prefetch=1, grid=(t // tm,),
            in_specs=[rowb, pl.BlockSpec(memory_space=pl.ANY), vec, vec], out_specs=rowb,
            scratch_shapes=[pltpu.VMEM((tm, D), F32), pltpu.SemaphoreType.DMA(())]),
        out_shape=jax.ShapeDtypeStruct((t, D), F32),
        compiler_params=_cparams(("arbitrary",)), name="moe_combine_ln")(dest, x, y_rows, g, b)


def _moe_ln(x, rw_t, rb, rw_pad, w_gate, w_up, w_down, g, b, tm_router, blk, tc, tm_comb):
    t = x.shape[0]
    bucket, rank, counts = _router(x, rw_t, rb, tm_router)
    padded = (counts + blk - 1) // blk * blk
    ends = jnp.cumsum(padded)
    dest = ((ends - padded)[bucket] + rank).astype(jnp.int32)
    nblk = t // blk + N_BUCKETS
    blk_bucket = jnp.minimum(jnp.searchsorted(ends, jnp.arange(nblk) * blk, side='right'), N_BUCKETS - 1)
    pair_lo = jnp.array([p[0] for p in _PAIRS], jnp.int32)
    pair_hi = jnp.array([p[1] for p in _PAIRS], jnp.int32)
    grp, pr = blk_bucket // len(_PAIRS), blk_bucket % len(_PAIRS)
    blk_lo = (grp * EXPERTS_PER_GROUP + pair_lo[pr]).astype(jnp.int32)
    blk_hi = (grp * EXPERTS_PER_GROUP + pair_hi[pr]).astype(jnp.int32)
    n_used = (ends[-1:] // blk).astype(jnp.int32)
    x_rows = _dispatch(x, dest, nblk * blk, tc)
    y_rows = _ffn(x_rows, blk_lo, blk_hi, n_used, rw_pad, w_gate, w_up, w_down, blk)
    return _combine_ln(x, y_rows, dest, g, b, tm_comb)


def kernel(x_prompt, x_sample, cache_swa_k, cache_swa_v, state_lru_conv, state_lru_h, state_rwkv_shift, state_rwkv_wkv, cache_mem_k, cache_mem_v, mem_prompt, swa_w_qkv, swa_sinks, swa_w_o, lru_w_in, lru_b_in, lru_conv_w, lru_conv_b, lru_w_a, lru_b_a, lru_w_i, lru_b_i, lru_lambda, lru_w_o, rwkv_mu, rwkv_w_r, rwkv_w_k, rwkv_w_v, rwkv_w0, rwkv_w1, rwkv_w2, rwkv_a0, rwkv_a1, rwkv_a2, rwkv_g1, rwkv_g2, rwkv_k_k, rwkv_k_a, rwkv_r_k, rwkv_gn_g, rwkv_gn_b, rwkv_w_o, mem_w_q, mem_w_kv, mem_w_o, ln_g, ln_b, router_w, router_b, moe_w_gate, moe_w_up, moe_w_down):
    n_p, seq, _ = x_prompt.shape
    n_s, dec_seq, _ = x_sample.shape
    assert dec_seq == 1
    past_len = 8192
    xp = x_prompt.reshape(n_p * seq, D)
    xs = x_sample.reshape(n_s, D)
    row = lambda v: v.reshape(1, -1)
    bf = lambda w: w.astype(BF16)

    rw_t = bf(router_w.T)
    rb = router_b.reshape(N_EXPERTS, 1)
    rw_pad = bf(jnp.pad(router_w, ((0, 0), (0, LANES - N_EXPERTS))))
    mem_p = mem_prompt.reshape(n_p * mem_prompt.shape[1], D)
    m_len = mem_prompt.shape[1]

    swa_k_p, swa_v_p, swa_k_s, swa_v_s = [], [], [], []
    lru_c_p, lru_h_p, lru_c_s, lru_h_s = [], [], [], []
    rw_x_p, rw_s_p, rw_x_s, rw_s_s = [], [], [], []
    mem_k_p, mem_v_p = [], []

    for layer in range(DEPTH):
        kind, i = layer % N_MIXERS, layer // N_MIXERS
        g0, b0 = row(ln_g[layer, 0]), row(ln_b[layer, 0])
        if kind == 0:
            w_qkv, w_o = bf(swa_w_qkv[i]), bf(swa_w_o[i])
            keep = min(WINDOW, seq)
            q, k, v, kv_last = _swa_qkv(xp, w_qkv, jnp.arange(seq), n_p, 512, keep, BF16)
            o = _swa_attn_prompt(q, k, v, swa_sinks[i], n_p)
            swa_k_p.append(kv_last[:, :, :KV_WIDTH].reshape(n_p, keep, SWA_KV_HEADS, HEAD_DIM))
            swa_v_p.append(kv_last[:, :, KV_WIDTH:].reshape(n_p, keep, SWA_KV_HEADS, HEAD_DIM))
            xp = _proj_ln(o, w_o, xp, g0, b0, 512)

            qs, _, _, kv_new = _swa_qkv(xs, w_qkv, jnp.full((n_s,), past_len), 1, n_s, n_s, F32)
            kn, vn = kv_new[0, :, :KV_WIDTH], kv_new[0, :, KV_WIDTH:]
            os_ = _swa_attn_sample(qs, kn, vn, cache_swa_k[i], cache_swa_v[i], swa_sinks[i], 8)
            wb = cache_swa_k.shape[2]
            k_all = jnp.concatenate([cache_swa_k[i], kn.reshape(n_s, 1, SWA_KV_HEADS, HEAD_DIM)], axis=1)
            v_all = jnp.concatenate([cache_swa_v[i], vn.reshape(n_s, 1, SWA_KV_HEADS, HEAD_DIM)], axis=1)
            swa_k_s.append(k_all[:, -wb:])
            swa_v_s.append(v_all[:, -wb:])
            xs = _proj_ln(os_, w_o, xs, g0, b0, n_s)
        elif kind == 1:
            wts = _lru_weights(lru_w_in[i], lru_b_in[i], lru_conv_w[i], lru_conv_b[i], lru_w_a[i], lru_b_a[i],
                               lru_w_i[i], lru_b_i[i], lru_lambda[i], lru_w_o[i])
            xp, conv_last, h_last = _lru_prompt(xp, wts, g0, b0, n_p, 256)
            lru_c_p.append(conv_last[:, SUBLANES - (CONV_W - 1):])
            lru_h_p.append(h_last[:, SUBLANES - 1])
            xs, xb_s, h_s = _lru_sample(xs, state_lru_conv[i], state_lru_h[i], wts, g0, b0)
            lru_c_s.append(jnp.concatenate([state_lru_conv[i][:, 1:], xb_s[:, None]], axis=1))
            lru_h_s.append(h_s)
        else:
            wts = (rwkv_mu[i], bf(rwkv_w_r[i]), bf(rwkv_w_k[i]), bf(rwkv_w_v[i]), row(rwkv_w0[i]), bf(rwkv_w1[i]),
                   bf(rwkv_w2[i]), row(rwkv_a0[i]), bf(rwkv_a1[i]), bf(rwkv_a2[i]), bf(rwkv_g1[i]), bf(rwkv_g2[i]))
            hp = (row(rwkv_k_k[i]), row(rwkv_k_a[i]), row(rwkv_r_k[i]), row(rwkv_gn_g[i]), row(rwkv_gn_b[i]))
            w_o = bf(rwkv_w_o[i])
            rw_x_p.append(xp.reshape(n_p, seq, D)[:, -1])
            rw_x_s.append(xs)
            r, k, v, a, ld, g = _rwkv_pre(xp, jnp.zeros((n_p, SUBLANES, D), F32), wts, n_p, 256, True, BF16)
            o, st = _wkv_prompt(r, k, v, a, ld, g, hp, n_p)
            hd = RWKV_HD
            st = jnp.stack([st[:, :, :hd, :hd], st[:, :, hd:, hd:]], axis=2).reshape(n_p, RWKV_HEADS, hd, hd)
            rw_s_p.append(jnp.swapaxes(st, -1, -2))
            xp = _proj_ln(o, w_o, xp, g0, b0, 512)

            r, k, v, a, ld, g = _rwkv_pre(xs, state_rwkv_shift[i], wts, 1, n_s, False, F32)
            os_, s_new = _wkv_sample(r, k, v, a, ld, g, state_rwkv_wkv[i], hp, 8)
            rw_s_s.append(s_new)
            xs = _proj_ln(os_, w_o, xs, g0, b0, n_s)

        g1, b1 = row(ln_g[layer, 1]), row(ln_b[layer, 1])
        w_q, w_o = bf(mem_w_q[layer]), bf(mem_w_o[layer])
        mkv = _matmul(mem_p, bf(mem_w_kv[layer]), 512)
        mk, mv = mkv[:, :D], mkv[:, D:]
        mem_k_p.append(mk.reshape(n_p, m_len, MEM_HEADS, MEM_HD))
        mem_v_p.append(mv.reshape(n_p, m_len, MEM_HEADS, MEM_HD))
        xp = _mem_attn_prompt(xp, w_q, bf(mk).reshape(n_p, m_len, D), bf(mv).reshape(n_p, m_len, D), w_o, g1, b1,
                              n_p, 512)
        qs = _matmul(xs, w_q, n_s)
        os_ = _mem_attn_sample(qs, cache_mem_k[layer], cache_mem_v[layer], 4)
        xs = _proj_ln(os_, w_o, xs, g1, b1, n_s)

        g2, b2 = row(ln_g[layer, 2]), row(ln_b[layer, 2])
        wg, wu, wd = bf(moe_w_gate[layer]), bf(moe_w_up[layer]), bf(moe_w_down[layer])
        xp = _moe_ln(xp, rw_t, rb, rw_pad, wg, wu, wd, g2, b2, 512, 256, 2048, 256)
        xs = _moe_ln(xs, rw_t, rb, rw_pad, wg, wu, wd, g2, b2, n_s, 8, n_s, n_s)

    return (xp.reshape(n_p, seq, D), xs.reshape(n_s, 1, D),
            jnp.stack(swa_k_p), jnp.stack(swa_v_p), jnp.stack(lru_c_p), jnp.stack(lru_h_p),
            jnp.stack(rw_x_p), jnp.stack(rw_s_p), jnp.stack(mem_k_p), jnp.stack(mem_v_p),
            jnp.stack(swa_k_s), jnp.stack(swa_v_s), jnp.stack(lru_c_s), jnp.stack(lru_h_s),
            jnp.stack(rw_x_s), jnp.stack(rw_s_s))
```

```python
import functools

import jax
import jax.numpy as jnp
from jax import lax
from jax.experimental import pallas as pl
from jax.experimental.pallas import tpu as pltpu

F32 = jnp.float32
BF16 = jnp.bfloat16

D = 1024
DEPTH = 4
N_MIXERS = 3
HEAD_DIM = 64
SWA_HEADS = D // HEAD_DIM
SWA_KV_HEADS = 4
SWA_GROUP = SWA_HEADS // SWA_KV_HEADS
Q_WIDTH = SWA_HEADS * HEAD_DIM
KV_WIDTH = SWA_KV_HEADS * HEAD_DIM
WINDOW = 128
ROT_DIM = HEAD_DIM // 4
ROPE_THETA = 500000.0
LRU_BLOCKS = 16
CONV_W = 4
LRU_C = 8.0
RWKV_HEADS = 16
RWKV_HD = 64
RWKV_GN_EPS = 64e-5
MEM_HEADS = 4
MEM_HD = D // MEM_HEADS
N_EXPERTS = 16
N_GROUPS = 4
EXPERTS_PER_GROUP = 4
EXPERT_FF = 512
LN_EPS = 1e-5
ALPHA = (2.0 * DEPTH) ** 0.25
NEG_INF = -1e30

LANES = 128
SUBLANES = 8
VMEM_LIMIT = 56 * 1024 * 1024
WKV_CHUNK = 64
N_BUCKETS = N_GROUPS * 6
BUCKET_ROWS = 32


def _cparams(sem):
    return pltpu.CompilerParams(dimension_semantics=sem, vmem_limit_bytes=VMEM_LIMIT)


def _dot(a, b):
    return jnp.dot(a, b, preferred_element_type=F32)


def _dot_nt(a, b):
    return lax.dot_general(a, b, (((1,), (1,)), ((), ())), preferred_element_type=F32)


def _dot_tn(a, b):
    return lax.dot_general(a, b, (((0,), (0,)), ((), ())), preferred_element_type=F32)


def _ln(z, g, b):
    mu = jnp.mean(z, axis=-1, keepdims=True)
    zc = z - mu
    var = jnp.mean(zc * zc, axis=-1, keepdims=True)
    return zc * lax.rsqrt(var + LN_EPS) * g + b


def _softplus(z):
    return jnp.maximum(z, 0.0) + jnp.log1p(jnp.exp(-jnp.abs(z)))


def _sigmoid(z):
    return 1.0 / (1.0 + jnp.exp(-z))


def _round_bf16(x):
    return x.astype(BF16).astype(F32)


def _full(shape):
    nd = len(shape)
    return pl.BlockSpec(shape, lambda *_: (0,) * nd)


def _mm_kernel(a_ref, w_ref, o_ref):
    o_ref[...] = _dot(a_ref[...].astype(BF16), w_ref[...]).astype(o_ref.dtype)


def _matmul(a, w, tm, out_dtype=F32):
    t, k = a.shape
    n = w.shape[1]
    return pl.pallas_call(
        _mm_kernel, grid=(t // tm,),
        in_specs=[pl.BlockSpec((tm, k), lambda i: (i, 0)), _full((k, n))],
        out_specs=pl.BlockSpec((tm, n), lambda i: (i, 0)),
        out_shape=jax.ShapeDtypeStruct((t, n), out_dtype),
        compiler_params=_cparams(("parallel",)), name="matmul")(a, w)


def _proj_ln_kernel(a_ref, w_ref, x_ref, g_ref, b_ref, o_ref):
    acc = _dot(a_ref[...].astype(BF16), w_ref[...])
    o_ref[...] = _ln(ALPHA * x_ref[...] + acc, g_ref[...], b_ref[...])


def _proj_ln(a, w, x, g, b, tm):
    t, k = a.shape
    return pl.pallas_call(
        _proj_ln_kernel, grid=(t // tm,),
        in_specs=[pl.BlockSpec((tm, k), lambda i: (i, 0)), _full((k, D)),
                  pl.BlockSpec((tm, D), lambda i: (i, 0)), _full((1, D)), _full((1, D))],
        out_specs=pl.BlockSpec((tm, D), lambda i: (i, 0)),
        out_shape=jax.ShapeDtypeStruct((t, D), F32),
        compiler_params=_cparams(("parallel",)), name="proj_ln")(a, w, x, g, b)


def _rope_tables(pos):
    half = ROT_DIM // 2
    inv_freq = ROPE_THETA ** (-jnp.arange(half, dtype=F32) / half)
    ang = pos.astype(F32)[:, None] * inv_freq
    cos, sin = jnp.cos(ang), jnp.sin(ang)
    one = jnp.ones((pos.shape[0], HEAD_DIM - ROT_DIM), F32)
    zero = jnp.zeros((pos.shape[0], HEAD_DIM - ROT_DIM), F32)
    zh = jnp.zeros_like(sin)
    c = jnp.concatenate([cos, cos, one], axis=1)
    s1 = jnp.concatenate([-sin, zh, zero], axis=1)
    s2 = jnp.concatenate([zh, sin, zero], axis=1)
    rep = LANES // HEAD_DIM
    return jnp.tile(c, (1, rep)), jnp.tile(s1, (1, rep)), jnp.tile(s2, (1, rep))


def _swa_qkv_kernel(x_ref, w_ref, c_ref, s1_ref, s2_ref, q_ref, k_ref, v_ref, kv_ref, *, tm, keep):
    acc = _dot(x_ref[...].astype(BF16), w_ref[...])
    c, s1, s2 = c_ref[...], s1_ref[...], s2_ref[...]
    half = ROT_DIM // 2
    n_q = Q_WIDTH // LANES
    n_k = KV_WIDTH // LANES
    for cg in range(n_q + n_k):
        xg = acc[:, cg * LANES:(cg + 1) * LANES]
        rot = xg * c + pltpu.roll(xg, LANES - half, 1) * s1 + pltpu.roll(xg, half, 1) * s2
        if cg < n_q:
            q_ref[:, cg * LANES:(cg + 1) * LANES] = rot.astype(q_ref.dtype)
        else:
            ck = cg - n_q
            k_ref[:, ck * LANES:(ck + 1) * LANES] = rot.astype(k_ref.dtype)
            kv_ref[0, :, ck * LANES:(ck + 1) * LANES] = rot[tm - keep:, :]
    v = acc[:, Q_WIDTH + KV_WIDTH:]
    v_ref[...] = v.astype(v_ref.dtype)
    kv_ref[0, :, KV_WIDTH:] = v[tm - keep:, :]


def _swa_qkv(x, w_qkv, pos, n_seq, tm, keep, qdtype):
    t = x.shape[0]
    s = t // n_seq
    nb = s // tm
    c, s1, s2 = _rope_tables(pos)
    row = lambda n, i: (n * nb + i, 0)
    tab = pl.BlockSpec((tm, LANES), lambda n, i: (i, 0))
    kern = functools.partial(_swa_qkv_kernel, tm=tm, keep=keep)
    return pl.pallas_call(
        kern, grid=(n_seq, nb),
        in_specs=[pl.BlockSpec((tm, D), row), _full((D, Q_WIDTH + 2 * KV_WIDTH)), tab, tab, tab],
        out_specs=[pl.BlockSpec((tm, Q_WIDTH), row), pl.BlockSpec((tm, KV_WIDTH), row),
                   pl.BlockSpec((tm, KV_WIDTH), row),
                   pl.BlockSpec((1, keep, 2 * KV_WIDTH), lambda n, i: (n, 0, 0))],
        out_shape=[jax.ShapeDtypeStruct((t, Q_WIDTH), qdtype), jax.ShapeDtypeStruct((t, KV_WIDTH), qdtype),
                   jax.ShapeDtypeStruct((t, KV_WIDTH), qdtype),
                   jax.ShapeDtypeStruct((n_seq, keep, 2 * KV_WIDTH), F32)],
        compiler_params=_cparams(("parallel", "arbitrary")), name="swa_qkv")(x, w_qkv, c, s1, s2)


def _swa_attn_kernel(sink_ref, q_ref, kp_ref, kc_ref, vp_ref, vc_ref, o_ref):
    j = pl.program_id(1)
    w, grp = WINDOW, SWA_GROUP
    r = lax.broadcasted_iota(jnp.int32, (grp * w, 2 * w), 0) % w
    c = lax.broadcasted_iota(jnp.int32, (grp * w, 2 * w), 1)
    ok = jnp.logical_or(jnp.logical_and(jnp.logical_and(c < w, c > r), j > 0),
                        jnp.logical_and(c >= w, (c - w) <= r))
    scale = HEAD_DIM ** -0.5
    kvh = range(SWA_KV_HEADS)
    sls = [slice(h * HEAD_DIM, (h + 1) * HEAD_DIM) for h in kvh]
    heads = [[h * grp + g for g in range(grp)] for h in kvh]
    kcat = [jnp.concatenate([kp_ref[:, sl], kc_ref[:, sl]], axis=0) for sl in sls]
    vcat = [jnp.concatenate([vp_ref[:, sl], vc_ref[:, sl]], axis=0) for sl in sls]
    q4 = [jnp.concatenate([q_ref[:, hq * HEAD_DIM:(hq + 1) * HEAD_DIM] for hq in heads[h]], axis=0) for h in kvh]
    sink = [jnp.concatenate([jnp.full((w, 1), sink_ref[hq], F32) for hq in heads[h]], axis=0) for h in kvh]
    s = [jnp.where(ok, _dot_nt(q4[h], kcat[h]) * scale, NEG_INF) for h in kvh]
    m = [jnp.maximum(jnp.max(s[h], axis=-1, keepdims=True), sink[h]) for h in kvh]
    p = [jnp.exp(s[h] - m[h]) for h in kvh]
    den = [jnp.sum(p[h], axis=-1, keepdims=True) + jnp.exp(sink[h] - m[h]) for h in kvh]
    o = [_dot((p[h] / den[h]).astype(BF16), vcat[h]) for h in kvh]
    for h in kvh:
        for g, hq in enumerate(heads[h]):
            o_ref[:, hq * HEAD_DIM:(hq + 1) * HEAD_DIM] = o[h][g * w:(g + 1) * w].astype(o_ref.dtype)


def _swa_attn_prompt(q, k, v, sinks, n_seq):
    t = q.shape[0]
    nb = t // n_seq // WINDOW
    cur = lambda n, j: (n * nb + j, 0)
    prev = lambda n, j: (n * nb + jnp.maximum(j - 1, 0), 0)
    kv = lambda im: pl.BlockSpec((WINDOW, KV_WIDTH), im)
    return pl.pallas_call(
        _swa_attn_kernel, grid=(n_seq, nb),
        in_specs=[pl.BlockSpec(memory_space=pltpu.SMEM), pl.BlockSpec((WINDOW, Q_WIDTH), cur),
                  kv(prev), kv(cur), kv(prev), kv(cur)],
        out_specs=pl.BlockSpec((WINDOW, Q_WIDTH), cur),
        out_shape=jax.ShapeDtypeStruct((t, Q_WIDTH), BF16),
        compiler_params=_cparams(("parallel", "arbitrary")), name="swa_attn")(sinks, q, k, k, v, v)


def _swa_sample_kernel(sink_ref, q_ref, kn_ref, vn_ref, ck_ref, cv_ref, o_ref, *, wb):
    kidx = lax.broadcasted_iota(jnp.int32, (1, wb, 1), 1)
    valid = (wb - kidx) < WINDOW
    scale = HEAD_DIM ** -0.5
    for h in range(SWA_KV_HEADS):
        sl = slice(h * HEAD_DIM, (h + 1) * HEAD_DIM)
        ck, cv = _round_bf16(ck_ref[:, :, sl]), _round_bf16(cv_ref[:, :, sl])
        kn, vn = _round_bf16(kn_ref[:, :, sl]), _round_bf16(vn_ref[:, :, sl])
        for g in range(SWA_GROUP):
            hq = h * SWA_GROUP + g
            qs = slice(hq * HEAD_DIM, (hq + 1) * HEAD_DIM)
            qh = _round_bf16(q_ref[:, :, qs])
            s = jnp.where(valid, jnp.sum(ck * qh, axis=-1, keepdims=True) * scale, NEG_INF)
            sn = jnp.sum(kn * qh, axis=-1, keepdims=True) * scale
            sink = sink_ref[hq]
            m = jnp.maximum(jnp.maximum(jnp.max(s, axis=1, keepdims=True), sn), sink)
            p = jnp.exp(s - m)
            pn = jnp.exp(sn - m)
            den = jnp.sum(p, axis=1, keepdims=True) + pn + jnp.exp(sink - m)
            p, pn = _round_bf16(p / den), _round_bf16(pn / den)
            o_ref[:, :, qs] = jnp.sum(p * cv, axis=1, keepdims=True) + pn * vn


def _swa_attn_sample(q, kn, vn, cache_k, cache_v, sinks, bs):
    b, wb = cache_k.shape[0], cache_k.shape[1]
    blk3 = lambda w: pl.BlockSpec((bs, 1, w), lambda i: (i, 0, 0))
    cblk = pl.BlockSpec((bs, wb, KV_WIDTH), lambda i: (i, 0, 0))
    out = pl.pallas_call(
        functools.partial(_swa_sample_kernel, wb=wb), grid=(b // bs,),
        in_specs=[pl.BlockSpec(memory_space=pltpu.SMEM), blk3(Q_WIDTH), blk3(KV_WIDTH), blk3(KV_WIDTH), cblk, cblk],
        out_specs=blk3(Q_WIDTH), out_shape=jax.ShapeDtypeStruct((b, 1, Q_WIDTH), F32),
        compiler_params=_cparams(("parallel",)), name="swa_sample")(
            sinks, q.reshape(b, 1, Q_WIDTH), kn.reshape(b, 1, KV_WIDTH), vn.reshape(b, 1, KV_WIDTH),
            cache_k.reshape(b, wb, KV_WIDTH), cache_v.reshape(b, wb, KV_WIDTH))
    return out.reshape(b, Q_WIDTH)


def _gelu_tanh(x):
    return 0.5 * x * (1.0 + jnp.tanh(0.7978845608028654 * (x + 0.044715 * x * x * x)))


def _lru_gates(xc, wa_ref, ba, wi_ref, bi, lam):
    xcb = xc.astype(BF16)
    gw = wa_ref.shape[1]
    ra, ia = [], []
    for gi in range(wa_ref.shape[0]):
        xs = xcb[:, gi * gw:(gi + 1) * gw]
        ra.append(_dot(xs, wa_ref[gi]))
        ia.append(_dot(xs, wi_ref[gi]))
    r = _sigmoid(jnp.concatenate(ra, axis=-1) + ba)
    ig = _sigmoid(jnp.concatenate(ia, axis=-1) + bi)
    log_a = -LRU_C * r * _softplus(-lam)
    a = jnp.exp(log_a)
    b = jnp.sqrt(-jnp.tanh(log_a) * (a * a + 1.0)) * (ig * xc)
    return a, b


def _shift_rows(ext, s, tm):
    return pltpu.roll(ext, s, 0)[SUBLANES:SUBLANES + tm]


def _lru_prompt_kernel(x_ref, win_ref, bin_ref, cw_ref, cb_ref, wa_ref, ba_ref, wi_ref, bi_ref, lam_ref,
                       wo_ref, g_ref, b_ref, o_ref, conv_ref, hl_ref, cx_ref, ch_ref, *, tm):
    i = pl.program_id(1)

    @pl.when(i == 0)
    def _():
        cx_ref[...] = jnp.zeros_like(cx_ref)
        ch_ref[...] = jnp.zeros_like(ch_ref)

    x = x_ref[...]
    xy = _dot(x.astype(BF16), win_ref[...]) + bin_ref[...]
    xb = xy[:, :D]
    y_gate = _gelu_tanh(xy[:, D:])
    ext = jnp.concatenate([cx_ref[...], xb], axis=0)
    cw = cw_ref[...]
    xc = cb_ref[...] + xb * cw[CONV_W - 1:CONV_W]
    for s in range(1, CONV_W):
        xc = xc + _shift_rows(ext, s, tm) * cw[CONV_W - 1 - s:CONV_W - s]
    cx_ref[...] = xb[tm - SUBLANES:]
    conv_ref[0] = xb[tm - SUBLANES:]

    a, b = _lru_gates(xc, wa_ref, ba_ref[...], wi_ref, bi_ref[...], lam_ref[...])
    row = lax.broadcasted_iota(jnp.int32, (tm, 1), 0)
    s = 1
    while s < tm:
        keep = row >= s
        a_sh = jnp.where(keep, pltpu.roll(a, s, 0), 1.0)
        b_sh = jnp.where(keep, pltpu.roll(b, s, 0), 0.0)
        b = a * b_sh + b
        a = a * a_sh
        s *= 2
    h = a * ch_ref[SUBLANES - 1:SUBLANES, :] + b
    ch_ref[...] = h[tm - SUBLANES:]
    hl_ref[0] = h[tm - SUBLANES:]
    acc = _dot((h * y_gate).astype(BF16), wo_ref[...])
    o_ref[...] = _ln(ALPHA * x + acc, g_ref[...], b_ref[...])


def _lru_weights(w_in, b_in, conv_w, conv_b, w_a, b_a, w_i, b_i, lam, w_o):
    gsz = 4
    ng = LRU_BLOCKS // gsz
    bw = D // LRU_BLOCKS

    def grouped(w):
        w4 = w.reshape(ng, gsz, bw, bw)
        return jnp.einsum('gaij,ab->gaibj', w4, jnp.eye(gsz, dtype=w.dtype)).reshape(ng, gsz * bw, gsz * bw).astype(BF16)

    row = lambda v: v.reshape(1, -1)
    return (w_in.astype(BF16), row(b_in), conv_w, row(conv_b), grouped(w_a), row(b_a), grouped(w_i), row(b_i),
            row(lam), w_o.astype(BF16))


def _lru_prompt(x, wts, g, b, n_seq, tm):
    t = x.shape[0]
    nb = t // n_seq // tm
    row = lambda n, i: (n * nb + i, 0)
    last = pl.BlockSpec((1, SUBLANES, D), lambda n, i: (n, 0, 0))
    w_in, b_in, cw, cb, wa, ba, wi, bi, lam, wo = wts
    return pl.pallas_call(
        functools.partial(_lru_prompt_kernel, tm=tm), grid=(n_seq, nb),
        in_specs=[pl.BlockSpec((tm, D), row), _full(w_in.shape), _full(b_in.shape), _full(cw.shape), _full(cb.shape),
                  _full(wa.shape), _full(ba.shape), _full(wi.shape), _full(bi.shape), _full(lam.shape),
                  _full(wo.shape), _full((1, D)), _full((1, D))],
        out_specs=[pl.BlockSpec((tm, D), row), last, last],
        out_shape=[jax.ShapeDtypeStruct((t, D), F32), jax.ShapeDtypeStruct((n_seq, SUBLANES, D), F32),
                   jax.ShapeDtypeStruct((n_seq, SUBLANES, D), F32)],
        scratch_shapes=[pltpu.VMEM((SUBLANES, D), F32), pltpu.VMEM((SUBLANES, D), F32)],
        compiler_params=_cparams(("parallel", "arbitrary")), name="lru_prompt")(x, *wts, g, b)


def _lru_sample_kernel(x_ref, c0_ref, c1_ref, c2_ref, h0_ref, win_ref, bin_ref, cw_ref, cb_ref, wa_ref, ba_ref,
                       wi_ref, bi_ref, lam_ref, wo_ref, g_ref, b_ref, o_ref, xb_ref, h_ref):
    x = x_ref[...]
    xy = _dot(x.astype(BF16), win_ref[...]) + bin_ref[...]
    xb = xy[:, :D]
    y_gate = _gelu_tanh(xy[:, D:])
    cw = cw_ref[...]
    xc = (cb_ref[...] + c0_ref[...] * cw[0:1] + c1_ref[...] * cw[1:2] + c2_ref[...] * cw[2:3] + xb * cw[3:4])
    a, b = _lru_gates(xc, wa_ref, ba_ref[...], wi_ref, bi_ref[...], lam_ref[...])
    h = a * h0_ref[...] + b
    xb_ref[...] = xb
    h_ref[...] = h
    acc = _dot((h * y_gate).astype(BF16), wo_ref[...])
    o_ref[...] = _ln(ALPHA * x + acc, g_ref[...], b_ref[...])


def _lru_sample(x, conv_state, h0, wts, g, b):
    t = x.shape[0]
    args = (x, conv_state[:, 0], conv_state[:, 1], conv_state[:, 2], h0, *wts, g, b)
    sd = jax.ShapeDtypeStruct((t, D), F32)
    return pl.pallas_call(
        _lru_sample_kernel, grid=(1,),
        in_specs=[_full(a.shape) for a in args],
        out_specs=[_full((t, D))] * 3, out_shape=[sd, sd, sd],
        compiler_params=_cparams(("arbitrary",)), name="lru_sample")(*args)


def _rwkv_pre_kernel(x_ref, xp_ref, mu_ref, wr_ref, wk_ref, wv_ref, w0_ref, w1_ref, w2_ref, a0_ref, a1_ref, a2_ref,
                     g1_ref, g2_ref, r_ref, k_ref, v_ref, a_ref, ld_ref, g_ref, *scratch, tm, seq):
    x = x_ref[...]
    if seq:
        cx_ref, = scratch
        i = pl.program_id(1)

        @pl.when(i == 0)
        def _():
            cx_ref[...] = xp_ref[0]

        x_prev = _shift_rows(jnp.concatenate([cx_ref[...], x], axis=0), 1, tm)
        cx_ref[...] = x[tm - SUBLANES:]
    else:
        x_prev = xp_ref[...]
    xx = x_prev - x
    mu = mu_ref[...]
    mix = lambda j: (x + xx * mu[j:j + 1]).astype(BF16)
    r_ref[...] = _dot(mix(0), wr_ref[...]).astype(r_ref.dtype)
    wl = _dot(jnp.tanh(_dot(mix(1), w1_ref[...])).astype(BF16), w2_ref[...])
    w = -_softplus(-(w0_ref[...] + wl)) - 0.5
    ld_ref[...] = -jnp.exp(w)
    k_ref[...] = _dot(mix(2), wk_ref[...]).astype(k_ref.dtype)
    v_ref[...] = _dot(mix(3), wv_ref[...]).astype(v_ref.dtype)
    al = _dot(_dot(mix(4), a1_ref[...]).astype(BF16), a2_ref[...])
    a_ref[...] = _sigmoid(a0_ref[...] + al).astype(a_ref.dtype)
    g_ref[...] = _dot(_sigmoid(_dot(mix(5), g1_ref[...])).astype(BF16), g2_ref[...]).astype(g_ref.dtype)


def _rwkv_pre(x, x_prev, wts, n_seq, tm, seq, dtype):
    t = x.shape[0]
    nb = t // n_seq // tm
    row = lambda n, i: (n * nb + i, 0)
    xp_spec = pl.BlockSpec((1, SUBLANES, D), lambda n, i: (n, 0, 0)) if seq else pl.BlockSpec((tm, D), row)
    sd = lambda dt: jax.ShapeDtypeStruct((t, D), dt)
    blk = pl.BlockSpec((tm, D), row)
    return pl.pallas_call(
        functools.partial(_rwkv_pre_kernel, tm=tm, seq=seq), grid=(n_seq, nb),
        in_specs=[blk, xp_spec] + [_full(w.shape) for w in wts],
        out_specs=[blk] * 6,
        out_shape=[sd(dtype), sd(dtype), sd(dtype), sd(dtype), sd(F32), sd(dtype)],
        scratch_shapes=[pltpu.VMEM((SUBLANES, D), F32)] if seq else [],
        compiler_params=_cparams(("parallel", "arbitrary")), name="rwkv_pre")(x, x_prev, *wts)


def _seg_sum(x, first):
    s0 = jnp.sum(jnp.where(first, x, 0.0), axis=-1, keepdims=True)
    s1 = jnp.sum(jnp.where(first, 0.0, x), axis=-1, keepdims=True)
    return jnp.where(first, s0, s1)


def _wkv_kernel(r_ref, k_ref, v_ref, a_ref, ld_ref, g_ref, kk_ref, ka_ref, rk_ref, gg_ref, gb_ref,
                o_ref, s_ref, st_ref):
    c = pl.program_id(1)
    L = WKV_CHUNK
    P2 = 2 * L

    @pl.when(c == 0)
    def _():
        st_ref[...] = jnp.zeros_like(st_ref)

    ld_all = ld_ref[...]
    tri = (lax.broadcasted_iota(jnp.int32, (L, L), 0) >= lax.broadcasted_iota(jnp.int32, (L, L), 1)).astype(BF16)
    hi = ld_all.astype(BF16)
    r1 = ld_all - hi.astype(F32)
    mid = r1.astype(BF16)
    lo = (r1 - mid.astype(F32)).astype(BF16)
    cum_all = _dot(tri, hi) + _dot(tri, mid) + _dot(tri, lo)

    lane = lax.broadcasted_iota(jnp.int32, (1, LANES), 1)
    first = lane < RWKV_HD
    ri = lax.broadcasted_iota(jnp.int32, (P2, P2), 0)
    ci = lax.broadcasted_iota(jnp.int32, (P2, P2), 1)
    same_head = (ri // L) == (ci // L)
    rt, ct = ri % L, ci % L
    strict = jnp.logical_and(same_head, rt > ct)
    incl = jnp.logical_and(same_head, rt >= ct)
    eye = ri == ci

    def stack(xv):
        return jnp.concatenate([jnp.where(first, xv, 0.0), jnp.where(first, 0.0, xv)], axis=0).astype(BF16)

    pairs = range(RWKV_HEADS // 2)
    sls = [slice(p * LANES, (p + 1) * LANES) for p in pairs]
    ws, us, ks, rs, ul, kl, vs, g_l, bonus = ([] for _ in range(9))
    for sl in sls:
        rp, kp, vp, ap = (ref[:, sl].astype(F32) for ref in (r_ref, k_ref, v_ref, a_ref))
        ldp, cum = ld_all[:, sl], cum_all[:, sl]
        kk = kp * kk_ref[:, sl]
        kk = kk / jnp.maximum(jnp.sqrt(_seg_sum(kk * kk, first)), 1e-12)
        kmod = kp * (1.0 + (ap - 1.0) * ka_ref[:, sl])
        bp = kk * ap
        cum_l = cum[L - 1:L, :]
        g_inv = jnp.exp(-cum)
        g_to_end = jnp.exp(cum_l - cum)
        ws.append(stack(kk * jnp.exp(cum - ldp)))
        us.append(stack(bp * g_inv))
        ks.append(stack(kmod * g_inv))
        rs.append(stack(rp * jnp.exp(cum)))
        ul.append(stack(bp * g_to_end))
        kl.append(stack(kmod * g_to_end))
        vs.append(stack(vp))
        g_l.append(jnp.exp(cum_l))
        bonus.append(_seg_sum(rp * kmod * rk_ref[:, sl], first) * vp)

    n_mat = [jnp.where(strict, _dot_nt(ws[p], us[p]), 0.0) for p in pairs]
    m_mat = [jnp.where(strict, _dot_nt(ws[p], ks[p]), 0.0).astype(BF16) for p in pairs]
    nr_mat = [jnp.where(incl, _dot_nt(rs[p], us[p]), 0.0).astype(BF16) for p in pairs]
    mr_mat = [jnp.where(incl, _dot_nt(rs[p], ks[p]), 0.0).astype(BF16) for p in pairs]

    def level_mask(sz):
        sub = jnp.logical_and((rt // sz) % 2 == 1, (ct // sz) % 2 == 0)
        return jnp.logical_and(jnp.logical_and(sub, (rt // (2 * sz)) == (ct // (2 * sz))), same_head)

    x_inv = [jnp.where(eye, 1.0, 0.0) - jnp.where(level_mask(1), n_mat[p], 0.0) for p in pairs]
    sz = 2
    while sz < L:
        mask = level_mask(sz)
        xb = [x_inv[p].astype(BF16) for p in pairs]
        xc = [_dot(xb[p], jnp.where(mask, n_mat[p], 0.0).astype(BF16)).astype(BF16) for p in pairs]
        x_inv = [x_inv[p] - _dot(xc[p], xb[p]) for p in pairs]
        sz *= 2

    a0 = [st_ref[p] for p in pairs]
    a0b = [a0[p].astype(BF16) for p in pairs]
    rhs = [(_dot(ws[p], a0b[p]) + _dot(m_mat[p], vs[p])).astype(BF16) for p in pairs]
    pm = [(-_dot(x_inv[p].astype(BF16), rhs[p])).astype(BF16) for p in pairs]
    o_st = [_dot(rs[p], a0b[p]) + _dot(nr_mat[p], pm[p]) + _dot(mr_mat[p], vs[p]) for p in pairs]
    for p in pairs:
        g_col = jnp.sum(jnp.where(eye, jnp.broadcast_to(g_l[p], (P2, P2)), 0.0), axis=-1, keepdims=True)
        st_ref[p] = g_col * a0[p] + _dot_tn(ul[p], pm[p]) + _dot_tn(kl[p], vs[p])

    inv_n = 1.0 / RWKV_HD
    for p, sl in zip(pairs, sls):
        o = o_st[p][:L] + o_st[p][L:]
        mu = _seg_sum(o, first) * inv_n
        oc = o - mu
        var = _seg_sum(oc * oc, first) * inv_n
        on = oc * lax.rsqrt(var + RWKV_GN_EPS) * gg_ref[:, sl] + gb_ref[:, sl]
        o_ref[:, sl] = ((on + bonus[p]) * g_ref[:, sl].astype(F32)).astype(o_ref.dtype)

    s_ref[0] = st_ref[...]


def _wkv_prompt(r, k, v, a, ld, g, hp, n_seq):
    t = r.shape[0]
    L = WKV_CHUNK
    nc = t // n_seq // L
    row = lambda n, c: (n * nc + c, 0)
    blk = pl.BlockSpec((L, D), row)
    npair = RWKV_HEADS // 2
    return pl.pallas_call(
        _wkv_kernel, grid=(n_seq, nc),
        in_specs=[blk] * 6 + [_full((1, D))] * 5,
        out_specs=[blk, pl.BlockSpec((1, npair, LANES, LANES), lambda n, c: (n, 0, 0, 0))],
        out_shape=[jax.ShapeDtypeStruct((t, D), BF16), jax.ShapeDtypeStruct((n_seq, npair, LANES, LANES), F32)],
        scratch_shapes=[pltpu.VMEM((npair, LANES, LANES), F32)],
        compiler_params=_cparams(("parallel", "arbitrary")), name="wkv_chunk")(r, k, v, a, ld, g, *hp)


def _wkv_sample_kernel(r_ref, k_ref, v_ref, a_ref, ld_ref, g_ref, s_ref, kk_ref, ka_ref, rk_ref, gg_ref, gb_ref,
                       o_ref, so_ref):
    hd = RWKV_HD
    eye = lax.broadcasted_iota(jnp.int32, (1, hd, hd), 1) == lax.broadcasted_iota(jnp.int32, (1, hd, hd), 2)
    for h in range(RWKV_HEADS):
        hs = slice(h, h + 1)
        r, k, v, a, ld, g = (ref[:, hs, :] for ref in (r_ref, k_ref, v_ref, a_ref, ld_ref, g_ref))
        s = s_ref[:, h]
        kk = k * kk_ref[hs, :]
        kk = kk / jnp.maximum(jnp.sqrt(jnp.sum(kk * kk, axis=-1, keepdims=True)), 1e-12)
        kmod = k * (1.0 + (a - 1.0) * ka_ref[hs, :])
        skk = jnp.sum(s * kk, axis=-1, keepdims=True)
        v_col = jnp.sum(jnp.where(eye, v, 0.0), axis=-1, keepdims=True)
        s_new = s * jnp.exp(ld) - skk * (kk * a) + v_col * kmod
        so_ref[:, h] = s_new
        o_col = jnp.sum(s_new * r, axis=-1, keepdims=True)
        o = jnp.sum(jnp.where(eye, o_col, 0.0), axis=1, keepdims=True)
        mu = jnp.mean(o, axis=-1, keepdims=True)
        oc = o - mu
        var = jnp.mean(oc * oc, axis=-1, keepdims=True)
        on = oc * lax.rsqrt(var + RWKV_GN_EPS) * gg_ref[hs, :] + gb_ref[hs, :]
        bonus = jnp.sum(r * kmod * rk_ref[hs, :], axis=-1, keepdims=True) * v
        o_ref[:, hs, :] = (on + bonus) * g


def _wkv_sample(r, k, v, a, ld, g, state, hp, bs):
    b = r.shape[0]
    h3 = lambda z: z.reshape(b, RWKV_HEADS, RWKV_HD)
    blk = pl.BlockSpec((bs, RWKV_HEADS, RWKV_HD), lambda i: (i, 0, 0))
    sblk = pl.BlockSpec((bs, RWKV_HEADS, RWKV_HD, RWKV_HD), lambda i: (i, 0, 0, 0))
    hp3 = [z.reshape(RWKV_HEADS, RWKV_HD) for z in hp]
    o, s_new = pl.pallas_call(
        _wkv_sample_kernel, grid=(b // bs,),
        in_specs=[blk] * 6 + [sblk] + [_full((RWKV_HEADS, RWKV_HD))] * 5,
        out_specs=[blk, sblk],
        out_shape=[jax.ShapeDtypeStruct((b, RWKV_HEADS, RWKV_HD), F32), jax.ShapeDtypeStruct(state.shape, F32)],
        compiler_params=_cparams(("parallel",)), name="wkv_sample")(
            h3(r), h3(k), h3(v), h3(a), h3(ld), h3(g), state, *hp3)
    return o.reshape(b, D), s_new


def _mem_prompt_kernel(x_ref, wq_ref, mk_ref, mv_ref, wo_ref, g_ref, b_ref, o_ref):
    x = x_ref[...]
    q = _dot(x.astype(BF16), wq_ref[...]).astype(BF16)
    scale = MEM_HD ** -0.5
    outs = []
    for h in range(MEM_HEADS):
        sl = slice(h * MEM_HD, (h + 1) * MEM_HD)
        s = _dot_nt(q[:, sl], mk_ref[0, :, sl]) * scale
        p = jnp.exp(s - jnp.max(s, axis=-1, keepdims=True))
        den = jnp.sum(p, axis=-1, keepdims=True)
        outs.append(_dot((p / den).astype(BF16), mv_ref[0, :, sl]).astype(BF16))
    acc = _dot(jnp.concatenate(outs, axis=-1), wo_ref[...])
    o_ref[...] = _ln(ALPHA * x + acc, g_ref[...], b_ref[...])


def _mem_attn_prompt(x, w_q, mk, mv, w_o, g, b, n_seq, tm):
    t = x.shape[0]
    nb = t // n_seq // tm
    m = mk.shape[1]
    row = lambda n, i: (n * nb + i, 0)
    mem = pl.BlockSpec((1, m, D), lambda n, i: (n, 0, 0))
    return pl.pallas_call(
        _mem_prompt_kernel, grid=(n_seq, nb),
        in_specs=[pl.BlockSpec((tm, D), row), _full((D, D)), mem, mem, _full((D, D)), _full((1, D)), _full((1, D))],
        out_specs=pl.BlockSpec((tm, D), row), out_shape=jax.ShapeDtypeStruct((t, D), F32),
        compiler_params=_cparams(("parallel", "arbitrary")), name="mem_attn")(x, w_q, mk, mv, w_o, g, b)


def _mem_sample_kernel(q_ref, ck_ref, cv_ref, o_ref, *, bs):
    scale = MEM_HD ** -0.5
    for b in range(bs):
        s = jnp.sum(_round_bf16(ck_ref[b]) * _round_bf16(q_ref[b]), axis=-1, keepdims=True) * scale
        p = jnp.exp(s - jnp.max(s, axis=0, keepdims=True))
        p = _round_bf16(p / jnp.sum(p, axis=0, keepdims=True))
        o_ref[b] = jnp.sum(p * _round_bf16(cv_ref[b]), axis=0, keepdims=True)


def _mem_attn_sample(q, cache_k, cache_v, layer, bs):
    _, b, m, nh, hd = cache_k.shape
    qb = pl.BlockSpec((bs, 1, nh, hd), lambda i: (i, 0, 0, 0))
    cb = pl.BlockSpec((None, bs, m, nh, hd), lambda i: (layer, i, 0, 0, 0))
    out = pl.pallas_call(
        functools.partial(_mem_sample_kernel, bs=bs), grid=(b // bs,), in_specs=[qb, cb, cb], out_specs=qb,
        out_shape=jax.ShapeDtypeStruct((b, 1, nh, hd), F32),
        compiler_params=_cparams(("parallel",)), name="mem_sample")(q.reshape(b, 1, nh, hd), cache_k, cache_v)
    return out.reshape(b, D)


_PAIRS = ((0, 1), (0, 2), (0, 3), (1, 2), (1, 3), (2, 3))


def _router_kernel(x_ref, rw_ref, rb_ref, bucket_ref, rank_ref, cnt_ref, base_ref, *, tm):
    i = pl.program_id(0)

    @pl.when(i == 0)
    def _():
        base_ref[...] = jnp.zeros_like(base_ref)

    logits = _dot_nt(rw_ref[...], x_ref[...].astype(BF16))
    e = jnp.exp(logits - jnp.max(logits, axis=0, keepdims=True))
    sel = e / jnp.sum(e, axis=0, keepdims=True) + rb_ref[...]
    s = [sel[j:j + 1, :] for j in range(N_EXPERTS)]
    neg = jnp.float32(-jnp.inf)

    best = jnp.zeros((1, tm), jnp.int32)
    best_score = None
    for gi in range(N_GROUPS):
        s0, s1, s2, s3 = s[4 * gi:4 * gi + 4]
        hi01, lo01, hi23, lo23 = jnp.maximum(s0, s1), jnp.minimum(s0, s1), jnp.maximum(s2, s3), jnp.minimum(s2, s3)
        score = jnp.maximum(hi01, hi23) + jnp.maximum(jnp.minimum(hi01, hi23), jnp.maximum(lo01, lo23))
        if gi == 0:
            best_score = score
        else:
            take = score > best_score
            best = jnp.where(take, gi, best)
            best_score = jnp.where(take, score, best_score)
    vals = []
    for j in range(EXPERTS_PER_GROUP):
        vj = s[j]
        for gi in range(1, N_GROUPS):
            vj = jnp.where(best == gi, s[4 * gi + j], vj)
        vals.append(vj)

    def argmax4(v):
        idx, mx = jnp.zeros((1, tm), jnp.int32), v[0]
        for j in range(1, EXPERTS_PER_GROUP):
            take = v[j] > mx
            idx = jnp.where(take, j, idx)
            mx = jnp.where(take, v[j], mx)
        return idx

    i1 = argmax4(vals)
    i2 = argmax4([jnp.where(i1 == j, neg, vals[j]) for j in range(EXPERTS_PER_GROUP)])
    lo, hi = jnp.minimum(i1, i2), jnp.maximum(i1, i2)
    pair = jnp.zeros((1, tm), jnp.int32)
    for pi, (pa, pb) in enumerate(_PAIRS):
        pair = jnp.where(jnp.logical_and(lo == pa, hi == pb), pi, pair)
    bucket = best * len(_PAIRS) + pair
    bucket_ref[0] = bucket

    onehot = (lax.broadcasted_iota(jnp.int32, (BUCKET_ROWS, tm), 0) == bucket).astype(F32)
    upper = (lax.broadcasted_iota(jnp.int32, (tm, tm), 0) <= lax.broadcasted_iota(jnp.int32, (tm, tm), 1)).astype(BF16)
    cum = _dot(onehot.astype(BF16), upper)
    base = base_ref[...]
    rank = jnp.sum(onehot * (cum + base), axis=0, keepdims=True) - 1.0
    rank_ref[0] = rank.astype(jnp.int32)
    base = base + jnp.sum(onehot, axis=1, keepdims=True)
    base_ref[...] = base
    cnt_ref[...] = jnp.broadcast_to(base, cnt_ref.shape)


def _router(x, rw_t, rb, tm):
    t = x.shape[0]
    nb = t // tm
    ib = pl.BlockSpec((1, 1, tm), lambda i: (i, 0, 0))
    bucket, rank, cnt = pl.pallas_call(
        functools.partial(_router_kernel, tm=tm), grid=(nb,),
        in_specs=[pl.BlockSpec((tm, D), lambda i: (i, 0)), _full(rw_t.shape), _full(rb.shape)],
        out_specs=[ib, ib, _full((BUCKET_ROWS, LANES))],
        out_shape=[jax.ShapeDtypeStruct((nb, 1, tm), jnp.int32), jax.ShapeDtypeStruct((nb, 1, tm), jnp.int32),
                   jax.ShapeDtypeStruct((BUCKET_ROWS, LANES), F32)],
        scratch_shapes=[pltpu.VMEM((BUCKET_ROWS, 1), F32)],
        compiler_params=_cparams(("arbitrary",)), name="router")(x, rw_t, rb)
    return bucket.reshape(t), rank.reshape(t), cnt[:N_BUCKETS, 0].astype(jnp.int32)


def _row_copies(idx_ref, base, src_hbm, dst, sem, n, wait):
    def body(r, carry):
        cp = pltpu.make_async_copy(src_hbm.at[pl.ds(idx_ref[base + r], 1)], dst.at[pl.ds(r, 1)], sem)
        if wait:
            cp.wait()
        else:
            cp.start()
        return carry

    lax.fori_loop(0, n, body, 0, unroll=8)


def _ffn_kernel(src_ref, lo_ref, hi_ref, used_ref, x_hbm, rw_ref, g0_ref, u0_ref, d0_ref, g1_ref, u1_ref, d1_ref,
                o_ref, xbuf, sem, *, blk):
    i = pl.program_id(0)
    used = used_ref[0]
    slot = i % 2

    @pl.when(jnp.logical_and(i == 0, used > 0))
    def _():
        _row_copies(src_ref, 0, x_hbm, xbuf.at[0], sem.at[0], blk, False)

    @pl.when(i + 1 < used)
    def _():
        _row_copies(src_ref, (i + 1) * blk, x_hbm, xbuf.at[1 - slot], sem.at[1 - slot], blk, False)

    @pl.when(i < used)
    def _():
        _row_copies(src_ref, i * blk, x_hbm, xbuf.at[slot], sem.at[slot], blk, True)
        xb = xbuf[slot].astype(BF16)
        logits = _dot(xb, rw_ref[...])
        lane = lax.broadcasted_iota(jnp.int32, logits.shape, 1)
        l_lo = jnp.sum(jnp.where(lane == lo_ref[i], logits, 0.0), axis=-1, keepdims=True)
        l_hi = jnp.sum(jnp.where(lane == hi_ref[i], logits, 0.0), axis=-1, keepdims=True)
        w_lo = _sigmoid(l_lo - l_hi)

        def expert(g_ref, u_ref, d_ref):
            gate = _dot(xb, g_ref[0])
            act = gate * _sigmoid(gate) * _dot(xb, u_ref[0])
            return _dot(act.astype(BF16), d_ref[0])

        y_lo = expert(g0_ref, u0_ref, d0_ref)
        y_hi = expert(g1_ref, u1_ref, d1_ref)
        o_ref[...] = w_lo * y_lo + (1.0 - w_lo) * y_hi

    @pl.when(i >= used)
    def _():
        o_ref[...] = jnp.zeros_like(o_ref)


def _ffn(x, src, blk_lo, blk_hi, n_used, rw, w_gate, w_up, w_down, layer, blk):
    rows = src.shape[0]
    nblk = rows // blk
    wg = lambda sel: pl.BlockSpec((None, 1, D, EXPERT_FF),
                                  lambda i, s, lo, hi, used: (layer, (lo, hi)[sel][i], 0, 0))
    wd = lambda sel: pl.BlockSpec((None, 1, EXPERT_FF, D),
                                  lambda i, s, lo, hi, used: (layer, (lo, hi)[sel][i], 0, 0))
    return pl.pallas_call(
        functools.partial(_ffn_kernel, blk=blk),
        grid_spec=pltpu.PrefetchScalarGridSpec(
            num_scalar_prefetch=4, grid=(nblk,),
            in_specs=[pl.BlockSpec(memory_space=pl.ANY), pl.BlockSpec(rw.shape, lambda i, s, lo, hi, used: (0, 0)),
                      wg(0), wg(0), wd(0), wg(1), wg(1), wd(1)],
            out_specs=pl.BlockSpec((blk, D), lambda i, s, lo, hi, used: (i, 0)),
            scratch_shapes=[pltpu.VMEM((2, blk, D), F32), pltpu.SemaphoreType.DMA((2,))]),
        out_shape=jax.ShapeDtypeStruct((rows, D), F32),
        compiler_params=_cparams(("arbitrary",)), name="moe_ffn")(
            src, blk_lo, blk_hi, n_used, x, rw, w_gate, w_up, w_down, w_gate, w_up, w_down)


def _combine_ln_kernel(dest_ref, x_ref, y_hbm, g_ref, b_ref, o_ref, ybuf, sem, *, tm):
    i = pl.program_id(0)
    slot = i % 2

    @pl.when(i == 0)
    def _():
        _row_copies(dest_ref, 0, y_hbm, ybuf.at[0], sem.at[0], tm, False)

    @pl.when(i + 1 < pl.num_programs(0))
    def _():
        _row_copies(dest_ref, (i + 1) * tm, y_hbm, ybuf.at[1 - slot], sem.at[1 - slot], tm, False)

    _row_copies(dest_ref, i * tm, y_hbm, ybuf.at[slot], sem.at[slot], tm, True)
    o_ref[...] = _ln(ALPHA * x_ref[...] + ybuf[slot], g_ref[...], b_ref[...])


def _combine_ln(x, y_rows, dest, g, b, tm):
    t = x.shape[0]
    rowb = pl.BlockSpec((tm, D), lambda i, d: (i, 0))
    vec = pl.BlockSpec((1, D), lambda i, d: (0, 0))
    return pl.pallas_call(
        functools.partial(_combine_ln_kernel, tm=tm),
        grid_spec=pltpu.PrefetchScalarGridSpec(
            num_scalar_prefetch=1, grid=(t // tm,),
            in_specs=[rowb, pl.BlockSpec(memory_space=pl.ANY), vec, vec], out_specs=rowb,
            scratch_shapes=[pltpu.VMEM((2, tm, D), F32), pltpu.SemaphoreType.DMA((2,))]),
        out_shape=jax.ShapeDtypeStruct((t, D), F32),
        compiler_params=_cparams(("arbitrary",)), name="moe_combine_ln")(dest, x, y_rows, g, b)


def _moe_ln(x, rw_t, rb, rw_pad, w_gate, w_up, w_down, layer, g, b, tm_router, blk, tm_comb):
    t = x.shape[0]
    bucket, rank, counts = _router(x, rw_t, rb, tm_router)
    padded = (counts + blk - 1) // blk * blk
    ends = jnp.cumsum(padded)
    dest = ((ends - padded)[bucket] + rank).astype(jnp.int32)
    nblk = t // blk + N_BUCKETS
    src = jnp.zeros((nblk * blk,), jnp.int32).at[dest].set(jnp.arange(t, dtype=jnp.int32))
    blk_bucket = jnp.minimum(jnp.searchsorted(ends, jnp.arange(nblk) * blk, side='right'), N_BUCKETS - 1)
    pair_lo = jnp.array([p[0] for p in _PAIRS], jnp.int32)
    pair_hi = jnp.array([p[1] for p in _PAIRS], jnp.int32)
    grp, pr = blk_bucket // len(_PAIRS), blk_bucket % len(_PAIRS)
    blk_lo = (grp * EXPERTS_PER_GROUP + pair_lo[pr]).astype(jnp.int32)
    blk_hi = (grp * EXPERTS_PER_GROUP + pair_hi[pr]).astype(jnp.int32)
    n_used = (ends[-1:] // blk).astype(jnp.int32)
    y_rows = _ffn(x, src, blk_lo, blk_hi, n_used, rw_pad, w_gate, w_up, w_down, layer, blk)
    return _combine_ln(x, y_rows, dest, g, b, tm_comb)


def kernel(x_prompt, x_sample, cache_swa_k, cache_swa_v, state_lru_conv, state_lru_h, state_rwkv_shift, state_rwkv_wkv, cache_mem_k, cache_mem_v, mem_prompt, swa_w_qkv, swa_sinks, swa_w_o, lru_w_in, lru_b_in, lru_conv_w, lru_conv_b, lru_w_a, lru_b_a, lru_w_i, lru_b_i, lru_lambda, lru_w_o, rwkv_mu, rwkv_w_r, rwkv_w_k, rwkv_w_v, rwkv_w0, rwkv_w1, rwkv_w2, rwkv_a0, rwkv_a1, rwkv_a2, rwkv_g1, rwkv_g2, rwkv_k_k, rwkv_k_a, rwkv_r_k, rwkv_gn_g, rwkv_gn_b, rwkv_w_o, mem_w_q, mem_w_kv, mem_w_o, ln_g, ln_b, router_w, router_b, moe_w_gate, moe_w_up, moe_w_down):
    n_p, seq, _ = x_prompt.shape
    n_s, dec_seq, _ = x_sample.shape
    assert dec_seq == 1
    past_len = 8192
    xp = x_prompt.reshape(n_p * seq, D)
    xs = x_sample.reshape(n_s, D)
    row = lambda v: v.reshape(1, -1)
    bf = lambda w: w.astype(BF16)

    rw_t = bf(router_w.T)
    rb = router_b.reshape(N_EXPERTS, 1)
    rw_pad = bf(jnp.pad(router_w, ((0, 0), (0, LANES - N_EXPERTS))))
    wg, wu, wd = bf(moe_w_gate), bf(moe_w_up), bf(moe_w_down)
    mem_p = mem_prompt.reshape(n_p * mem_prompt.shape[1], D)
    m_len = mem_prompt.shape[1]

    swa_k_p, swa_v_p, swa_k_s, swa_v_s = [], [], [], []
    lru_c_p, lru_h_p, lru_c_s, lru_h_s = [], [], [], []
    rw_x_p, rw_s_p, rw_x_s, rw_s_s = [], [], [], []
    mem_k_p, mem_v_p = [], []

    for layer in range(DEPTH):
        kind, i = layer % N_MIXERS, layer // N_MIXERS
        g0, b0 = row(ln_g[layer, 0]), row(ln_b[layer, 0])
        if kind == 0:
            w_qkv, w_o = bf(swa_w_qkv[i]), bf(swa_w_o[i])
            keep = min(WINDOW, seq)
            q, k, v, kv_last = _swa_qkv(xp, w_qkv, jnp.arange(seq), n_p, 512, keep, BF16)
            o = _swa_attn_prompt(q, k, v, swa_sinks[i], n_p)
            swa_k_p.append(kv_last[:, :, :KV_WIDTH].reshape(n_p, keep, SWA_KV_HEADS, HEAD_DIM))
            swa_v_p.append(kv_last[:, :, KV_WIDTH:].reshape(n_p, keep, SWA_KV_HEADS, HEAD_DIM))
            xp = _proj_ln(o, w_o, xp, g0, b0, 512)

            qs, _, _, kv_new = _swa_qkv(xs, w_qkv, jnp.full((n_s,), past_len), 1, n_s, n_s, F32)
            kn, vn = kv_new[0, :, :KV_WIDTH], kv_new[0, :, KV_WIDTH:]
            os_ = _swa_attn_sample(qs, kn, vn, cache_swa_k[i], cache_swa_v[i], swa_sinks[i], 8)
            wb = cache_swa_k.shape[2]
            k_all = jnp.concatenate([cache_swa_k[i], kn.reshape(n_s, 1, SWA_KV_HEADS, HEAD_DIM)], axis=1)
            v_all = jnp.concatenate([cache_swa_v[i], vn.reshape(n_s, 1, SWA_KV_HEADS, HEAD_DIM)], axis=1)
            swa_k_s.append(k_all[:, -wb:])
            swa_v_s.append(v_all[:, -wb:])
            xs = _proj_ln(os_, w_o, xs, g0, b0, n_s)
        elif kind == 1:
            wts = _lru_weights(lru_w_in[i], lru_b_in[i], lru_conv_w[i], lru_conv_b[i], lru_w_a[i], lru_b_a[i],
                               lru_w_i[i], lru_b_i[i], lru_lambda[i], lru_w_o[i])
            xp, conv_last, h_last = _lru_prompt(xp, wts, g0, b0, n_p, 256)
            lru_c_p.append(conv_last[:, SUBLANES - (CONV_W - 1):])
            lru_h_p.append(h_last[:, SUBLANES - 1])
            xs, xb_s, h_s = _lru_sample(xs, state_lru_conv[i], state_lru_h[i], wts, g0, b0)
            lru_c_s.append(jnp.concatenate([state_lru_conv[i][:, 1:], xb_s[:, None]], axis=1))
            lru_h_s.append(h_s)
        else:
            wts = (rwkv_mu[i], bf(rwkv_w_r[i]), bf(rwkv_w_k[i]), bf(rwkv_w_v[i]), row(rwkv_w0[i]), bf(rwkv_w1[i]),
                   bf(rwkv_w2[i]), row(rwkv_a0[i]), bf(rwkv_a1[i]), bf(rwkv_a2[i]), bf(rwkv_g1[i]), bf(rwkv_g2[i]))
            hp = (row(rwkv_k_k[i]), row(rwkv_k_a[i]), row(rwkv_r_k[i]), row(rwkv_gn_g[i]), row(rwkv_gn_b[i]))
            w_o = bf(rwkv_w_o[i])
            rw_x_p.append(xp.reshape(n_p, seq, D)[:, -1])
            rw_x_s.append(xs)
            r, k, v, a, ld, g = _rwkv_pre(xp, jnp.zeros((n_p, SUBLANES, D), F32), wts, n_p, 256, True, BF16)
            o, st = _wkv_prompt(r, k, v, a, ld, g, hp, n_p)
            hd = RWKV_HD
            st = jnp.stack([st[:, :, :hd, :hd], st[:, :, hd:, hd:]], axis=2).reshape(n_p, RWKV_HEADS, hd, hd)
            rw_s_p.append(jnp.swapaxes(st, -1, -2))
            xp = _proj_ln(o, w_o, xp, g0, b0, 512)

            r, k, v, a, ld, g = _rwkv_pre(xs, state_rwkv_shift[i], wts, 1, n_s, False, F32)
            os_, s_new = _wkv_sample(r, k, v, a, ld, g, state_rwkv_wkv[i], hp, 8)
            rw_s_s.append(s_new)
            xs = _proj_ln(os_, w_o, xs, g0, b0, n_s)

        g1, b1 = row(ln_g[layer, 1]), row(ln_b[layer, 1])
        w_q, w_o = bf(mem_w_q[layer]), bf(mem_w_o[layer])
        mkv = _matmul(mem_p, bf(mem_w_kv[layer]), 512)
        mk, mv = mkv[:, :D], mkv[:, D:]
        mem_k_p.append(mk.reshape(n_p, m_len, MEM_HEADS, MEM_HD))
        mem_v_p.append(mv.reshape(n_p, m_len, MEM_HEADS, MEM_HD))
        xp = _mem_attn_prompt(xp, w_q, bf(mk).reshape(n_p, m_len, D), bf(mv).reshape(n_p, m_len, D), w_o, g1, b1,
                              n_p, 512)
        qs = _matmul(xs, w_q, n_s)
        os_ = _mem_attn_sample(qs, cache_mem_k, cache_mem_v, layer, 4)
        xs = _proj_ln(os_, w_o, xs, g1, b1, n_s)

        g2, b2 = row(ln_g[layer, 2]), row(ln_b[layer, 2])
        xp = _moe_ln(xp, rw_t, rb, rw_pad, wg, wu, wd, layer, g2, b2, 512, 256, 256)
        xs = _moe_ln(xs, rw_t, rb, rw_pad, wg, wu, wd, layer, g2, b2, n_s, 8, n_s)

    return (xp.reshape(n_p, seq, D), xs.reshape(n_s, 1, D),
            jnp.stack(swa_k_p), jnp.stack(swa_v_p), jnp.stack(lru_c_p), jnp.stack(lru_h_p),
            jnp.stack(rw_x_p), jnp.stack(rw_s_p), jnp.stack(mem_k_p), jnp.stack(mem_v_p),
            jnp.stack(swa_k_s), jnp.stack(swa_v_s), jnp.stack(lru_c_s), jnp.stack(lru_h_s),
            jnp.stack(rw_x_s), jnp.stack(rw_s_s))
```

```python
import functools

import jax
import jax.numpy as jnp
from jax import lax
from jax.experimental import pallas as pl
from jax.experimental.pallas import tpu as pltpu

F32 = jnp.float32
BF16 = jnp.bfloat16

D = 1024
DEPTH = 4
N_MIXERS = 3
HEAD_DIM = 64
SWA_HEADS = D // HEAD_DIM
SWA_KV_HEADS = 4
SWA_GROUP = SWA_HEADS // SWA_KV_HEADS
Q_WIDTH = SWA_HEADS * HEAD_DIM
KV_WIDTH = SWA_KV_HEADS * HEAD_DIM
WINDOW = 128
ROT_DIM = HEAD_DIM // 4
ROPE_THETA = 500000.0
LRU_BLOCKS = 16
CONV_W = 4
LRU_C = 8.0
RWKV_HEADS = 16
RWKV_HD = 64
RWKV_GN_EPS = 64e-5
MEM_HEADS = 4
MEM_HD = D // MEM_HEADS
N_EXPERTS = 16
N_GROUPS = 4
EXPERTS_PER_GROUP = 4
EXPERT_FF = 512
LN_EPS = 1e-5
ALPHA = (2.0 * DEPTH) ** 0.25
NEG_INF = -1e30

LANES = 128
SUBLANES = 8
VMEM_LIMIT = 56 * 1024 * 1024
WKV_CHUNK = 64
N_BUCKETS = N_GROUPS * 6
BUCKET_ROWS = 32


def _cparams(sem):
    return pltpu.CompilerParams(dimension_semantics=sem, vmem_limit_bytes=VMEM_LIMIT)


def _dot(a, b):
    return jnp.dot(a, b, preferred_element_type=F32)


def _dot_nt(a, b):
    return lax.dot_general(a, b, (((1,), (1,)), ((), ())), preferred_element_type=F32)


def _dot_tn(a, b):
    return lax.dot_general(a, b, (((0,), (0,)), ((), ())), preferred_element_type=F32)


def _ln(z, g, b):
    mu = jnp.mean(z, axis=-1, keepdims=True)
    zc = z - mu
    var = jnp.mean(zc * zc, axis=-1, keepdims=True)
    return zc * lax.rsqrt(var + LN_EPS) * g + b


def _softplus(z):
    return jnp.maximum(z, 0.0) + jnp.log1p(jnp.exp(-jnp.abs(z)))


def _sigmoid(z):
    return 1.0 / (1.0 + jnp.exp(-z))


def _round_bf16(x):
    return x.astype(BF16).astype(F32)


def _full(shape):
    nd = len(shape)
    return pl.BlockSpec(shape, lambda *_: (0,) * nd)


def _mm_kernel(a_ref, w_ref, o_ref):
    o_ref[...] = _dot(a_ref[...].astype(BF16), w_ref[...]).astype(o_ref.dtype)


def _matmul(a, w, tm, out_dtype=F32):
    t, k = a.shape
    n = w.shape[1]
    return pl.pallas_call(
        _mm_kernel, grid=(t // tm,),
        in_specs=[pl.BlockSpec((tm, k), lambda i: (i, 0)), _full((k, n))],
        out_specs=pl.BlockSpec((tm, n), lambda i: (i, 0)),
        out_shape=jax.ShapeDtypeStruct((t, n), out_dtype),
        compiler_params=_cparams(("parallel",)), name="matmul")(a, w)


def _proj_ln_kernel(a_ref, w_ref, x_ref, g_ref, b_ref, o_ref):
    acc = _dot(a_ref[...].astype(BF16), w_ref[...])
    o_ref[...] = _ln(ALPHA * x_ref[...] + acc, g_ref[...], b_ref[...])


def _proj_ln(a, w, x, g, b, tm):
    t, k = a.shape
    return pl.pallas_call(
        _proj_ln_kernel, grid=(t // tm,),
        in_specs=[pl.BlockSpec((tm, k), lambda i: (i, 0)), _full((k, D)),
                  pl.BlockSpec((tm, D), lambda i: (i, 0)), _full((1, D)), _full((1, D))],
        out_specs=pl.BlockSpec((tm, D), lambda i: (i, 0)),
        out_shape=jax.ShapeDtypeStruct((t, D), F32),
        compiler_params=_cparams(("parallel",)), name="proj_ln")(a, w, x, g, b)


def _rope_tables(pos):
    half = ROT_DIM // 2
    inv_freq = ROPE_THETA ** (-jnp.arange(half, dtype=F32) / half)
    ang = pos.astype(F32)[:, None] * inv_freq
    cos, sin = jnp.cos(ang), jnp.sin(ang)
    one = jnp.ones((pos.shape[0], HEAD_DIM - ROT_DIM), F32)
    zero = jnp.zeros((pos.shape[0], HEAD_DIM - ROT_DIM), F32)
    zh = jnp.zeros_like(sin)
    c = jnp.concatenate([cos, cos, one], axis=1)
    s1 = jnp.concatenate([-sin, zh, zero], axis=1)
    s2 = jnp.concatenate([zh, sin, zero], axis=1)
    rep = LANES // HEAD_DIM
    return jnp.tile(c, (1, rep)), jnp.tile(s1, (1, rep)), jnp.tile(s2, (1, rep))


def _swa_qkv_kernel(x_ref, w_ref, c_ref, s1_ref, s2_ref, q_ref, k_ref, v_ref, kv_ref, *, tm, keep):
    acc = _dot(x_ref[...].astype(BF16), w_ref[...])
    c, s1, s2 = c_ref[...], s1_ref[...], s2_ref[...]
    half = ROT_DIM // 2
    n_q = Q_WIDTH // LANES
    n_k = KV_WIDTH // LANES
    for cg in range(n_q + n_k):
        xg = acc[:, cg * LANES:(cg + 1) * LANES]
        rot = xg * c + pltpu.roll(xg, LANES - half, 1) * s1 + pltpu.roll(xg, half, 1) * s2
        if cg < n_q:
            q_ref[:, cg * LANES:(cg + 1) * LANES] = rot.astype(q_ref.dtype)
        else:
            ck = cg - n_q
            k_ref[:, ck * LANES:(ck + 1) * LANES] = rot.astype(k_ref.dtype)
            kv_ref[0, :, ck * LANES:(ck + 1) * LANES] = rot[tm - keep:, :]
    v = acc[:, Q_WIDTH + KV_WIDTH:]
    v_ref[...] = v.astype(v_ref.dtype)
    kv_ref[0, :, KV_WIDTH:] = v[tm - keep:, :]


def _swa_qkv(x, w_qkv, pos, n_seq, tm, keep, qdtype):
    t = x.shape[0]
    s = t // n_seq
    nb = s // tm
    c, s1, s2 = _rope_tables(pos)
    row = lambda n, i: (n * nb + i, 0)
    tab = pl.BlockSpec((tm, LANES), lambda n, i: (i, 0))
    kern = functools.partial(_swa_qkv_kernel, tm=tm, keep=keep)
    return pl.pallas_call(
        kern, grid=(n_seq, nb),
        in_specs=[pl.BlockSpec((tm, D), row), _full((D, Q_WIDTH + 2 * KV_WIDTH)), tab, tab, tab],
        out_specs=[pl.BlockSpec((tm, Q_WIDTH), row), pl.BlockSpec((tm, KV_WIDTH), row),
                   pl.BlockSpec((tm, KV_WIDTH), row),
                   pl.BlockSpec((1, keep, 2 * KV_WIDTH), lambda n, i: (n, 0, 0))],
        out_shape=[jax.ShapeDtypeStruct((t, Q_WIDTH), qdtype), jax.ShapeDtypeStruct((t, KV_WIDTH), qdtype),
                   jax.ShapeDtypeStruct((t, KV_WIDTH), qdtype),
                   jax.ShapeDtypeStruct((n_seq, keep, 2 * KV_WIDTH), F32)],
        compiler_params=_cparams(("parallel", "arbitrary")), name="swa_qkv")(x, w_qkv, c, s1, s2)


def _swa_attn_kernel(sink_ref, q_ref, kp_ref, kc_ref, vp_ref, vc_ref, o_ref):
    j = pl.program_id(1)
    w, grp = WINDOW, SWA_GROUP
    r = lax.broadcasted_iota(jnp.int32, (grp * w, 2 * w), 0) % w
    c = lax.broadcasted_iota(jnp.int32, (grp * w, 2 * w), 1)
    ok = jnp.logical_or(jnp.logical_and(jnp.logical_and(c < w, c > r), j > 0),
                        jnp.logical_and(c >= w, (c - w) <= r))
    scale = HEAD_DIM ** -0.5
    kvh = range(SWA_KV_HEADS)
    sls = [slice(h * HEAD_DIM, (h + 1) * HEAD_DIM) for h in kvh]
    heads = [[h * grp + g for g in range(grp)] for h in kvh]
    kcat = [jnp.concatenate([kp_ref[:, sl], kc_ref[:, sl]], axis=0) for sl in sls]
    vcat = [jnp.concatenate([vp_ref[:, sl], vc_ref[:, sl]], axis=0) for sl in sls]
    q4 = [jnp.concatenate([q_ref[:, hq * HEAD_DIM:(hq + 1) * HEAD_DIM] for hq in heads[h]], axis=0) for h in kvh]
    sink = [jnp.concatenate([jnp.full((w, 1), sink_ref[hq], F32) for hq in heads[h]], axis=0) for h in kvh]
    s = [jnp.where(ok, _dot_nt(q4[h], kcat[h]) * scale, NEG_INF) for h in kvh]
    m = [jnp.maximum(jnp.max(s[h], axis=-1, keepdims=True), sink[h]) for h in kvh]
    p = [jnp.exp(s[h] - m[h]) for h in kvh]
    den = [jnp.sum(p[h], axis=-1, keepdims=True) + jnp.exp(sink[h] - m[h]) for h in kvh]
    o = [_dot((p[h] / den[h]).astype(BF16), vcat[h]) for h in kvh]
    for h in kvh:
        for g, hq in enumerate(heads[h]):
            o_ref[:, hq * HEAD_DIM:(hq + 1) * HEAD_DIM] = o[h][g * w:(g + 1) * w].astype(o_ref.dtype)


def _swa_attn_prompt(q, k, v, sinks, n_seq):
    t = q.shape[0]
    nb = t // n_seq // WINDOW
    cur = lambda n, j: (n * nb + j, 0)
    prev = lambda n, j: (n * nb + jnp.maximum(j - 1, 0), 0)
    kv = lambda im: pl.BlockSpec((WINDOW, KV_WIDTH), im)
    return pl.pallas_call(
        _swa_attn_kernel, grid=(n_seq, nb),
        in_specs=[pl.BlockSpec(memory_space=pltpu.SMEM), pl.BlockSpec((WINDOW, Q_WIDTH), cur),
                  kv(prev), kv(cur), kv(prev), kv(cur)],
        out_specs=pl.BlockSpec((WINDOW, Q_WIDTH), cur),
        out_shape=jax.ShapeDtypeStruct((t, Q_WIDTH), BF16),
        compiler_params=_cparams(("parallel", "arbitrary")), name="swa_attn")(sinks, q, k, k, v, v)


def _swa_sample_kernel(sink_ref, q_ref, kn_ref, vn_ref, ck_ref, cv_ref, o_ref, *, wb):
    kidx = lax.broadcasted_iota(jnp.int32, (1, wb, 1), 1)
    valid = (wb - kidx) < WINDOW
    scale = HEAD_DIM ** -0.5
    for h in range(SWA_KV_HEADS):
        sl = slice(h * HEAD_DIM, (h + 1) * HEAD_DIM)
        ck, cv = _round_bf16(ck_ref[:, :, sl]), _round_bf16(cv_ref[:, :, sl])
        kn, vn = _round_bf16(kn_ref[:, :, sl]), _round_bf16(vn_ref[:, :, sl])
        for g in range(SWA_GROUP):
            hq = h * SWA_GROUP + g
            qs = slice(hq * HEAD_DIM, (hq + 1) * HEAD_DIM)
            qh = _round_bf16(q_ref[:, :, qs])
            s = jnp.where(valid, jnp.sum(ck * qh, axis=-1, keepdims=True) * scale, NEG_INF)
            sn = jnp.sum(kn * qh, axis=-1, keepdims=True) * scale
            sink = sink_ref[hq]
            m = jnp.maximum(jnp.maximum(jnp.max(s, axis=1, keepdims=True), sn), sink)
            p = jnp.exp(s - m)
            pn = jnp.exp(sn - m)
            den = jnp.sum(p, axis=1, keepdims=True) + pn + jnp.exp(sink - m)
            p, pn = _round_bf16(p / den), _round_bf16(pn / den)
            o_ref[:, :, qs] = jnp.sum(p * cv, axis=1, keepdims=True) + pn * vn


def _swa_attn_sample(q, kn, vn, cache_k, cache_v, sinks, bs):
    b, wb = cache_k.shape[0], cache_k.shape[1]
    blk3 = lambda w: pl.BlockSpec((bs, 1, w), lambda i: (i, 0, 0))
    cblk = pl.BlockSpec((bs, wb, KV_WIDTH), lambda i: (i, 0, 0))
    out = pl.pallas_call(
        functools.partial(_swa_sample_kernel, wb=wb), grid=(b // bs,),
        in_specs=[pl.BlockSpec(memory_space=pltpu.SMEM), blk3(Q_WIDTH), blk3(KV_WIDTH), blk3(KV_WIDTH), cblk, cblk],
        out_specs=blk3(Q_WIDTH), out_shape=jax.ShapeDtypeStruct((b, 1, Q_WIDTH), F32),
        compiler_params=_cparams(("parallel",)), name="swa_sample")(
            sinks, q.reshape(b, 1, Q_WIDTH), kn.reshape(b, 1, KV_WIDTH), vn.reshape(b, 1, KV_WIDTH),
            cache_k.reshape(b, wb, KV_WIDTH), cache_v.reshape(b, wb, KV_WIDTH))
    return out.reshape(b, Q_WIDTH)


def _gelu_tanh(x):
    return 0.5 * x * (1.0 + jnp.tanh(0.7978845608028654 * (x + 0.044715 * x * x * x)))


def _lru_gates(xc, wa_ref, ba, wi_ref, bi, lam):
    xcb = xc.astype(BF16)
    gw = wa_ref.shape[1]
    ra, ia = [], []
    for gi in range(wa_ref.shape[0]):
        xs = xcb[:, gi * gw:(gi + 1) * gw]
        ra.append(_dot(xs, wa_ref[gi]))
        ia.append(_dot(xs, wi_ref[gi]))
    r = _sigmoid(jnp.concatenate(ra, axis=-1) + ba)
    ig = _sigmoid(jnp.concatenate(ia, axis=-1) + bi)
    log_a = -LRU_C * r * _softplus(-lam)
    a = jnp.exp(log_a)
    b = jnp.sqrt(-jnp.tanh(log_a) * (a * a + 1.0)) * (ig * xc)
    return a, b


def _shift_rows(ext, s, tm):
    return pltpu.roll(ext, s, 0)[SUBLANES:SUBLANES + tm]


def _lru_prompt_kernel(x_ref, win_ref, bin_ref, cw_ref, cb_ref, wa_ref, ba_ref, wi_ref, bi_ref, lam_ref,
                       wo_ref, g_ref, b_ref, o_ref, conv_ref, hl_ref, cx_ref, ch_ref, *, tm):
    i = pl.program_id(1)

    @pl.when(i == 0)
    def _():
        cx_ref[...] = jnp.zeros_like(cx_ref)
        ch_ref[...] = jnp.zeros_like(ch_ref)

    x = x_ref[...]
    xy = _dot(x.astype(BF16), win_ref[...]) + bin_ref[...]
    xb = xy[:, :D]
    y_gate = _gelu_tanh(xy[:, D:])
    ext = jnp.concatenate([cx_ref[...], xb], axis=0)
    cw = cw_ref[...]
    xc = cb_ref[...] + xb * cw[CONV_W - 1:CONV_W]
    for s in range(1, CONV_W):
        xc = xc + _shift_rows(ext, s, tm) * cw[CONV_W - 1 - s:CONV_W - s]
    cx_ref[...] = xb[tm - SUBLANES:]
    conv_ref[0] = xb[tm - SUBLANES:]

    a, b = _lru_gates(xc, wa_ref, ba_ref[...], wi_ref, bi_ref[...], lam_ref[...])
    row = lax.broadcasted_iota(jnp.int32, (tm, 1), 0)
    s = 1
    while s < tm:
        keep = row >= s
        a_sh = jnp.where(keep, pltpu.roll(a, s, 0), 1.0)
        b_sh = jnp.where(keep, pltpu.roll(b, s, 0), 0.0)
        b = a * b_sh + b
        a = a * a_sh
        s *= 2
    h = a * ch_ref[SUBLANES - 1:SUBLANES, :] + b
    ch_ref[...] = h[tm - SUBLANES:]
    hl_ref[0] = h[tm - SUBLANES:]
    acc = _dot((h * y_gate).astype(BF16), wo_ref[...])
    o_ref[...] = _ln(ALPHA * x + acc, g_ref[...], b_ref[...])


def _lru_weights(w_in, b_in, conv_w, conv_b, w_a, b_a, w_i, b_i, lam, w_o):
    gsz = 4
    ng = LRU_BLOCKS // gsz
    bw = D // LRU_BLOCKS

    def grouped(w):
        w4 = w.reshape(ng, gsz, bw, bw)
        return jnp.einsum('gaij,ab->gaibj', w4, jnp.eye(gsz, dtype=w.dtype)).reshape(ng, gsz * bw, gsz * bw).astype(BF16)

    row = lambda v: v.reshape(1, -1)
    return (w_in.astype(BF16), row(b_in), conv_w, row(conv_b), grouped(w_a), row(b_a), grouped(w_i), row(b_i),
            row(lam), w_o.astype(BF16))


def _lru_prompt(x, wts, g, b, n_seq, tm):
    t = x.shape[0]
    nb = t // n_seq // tm
    row = lambda n, i: (n * nb + i, 0)
    last = pl.BlockSpec((1, SUBLANES, D), lambda n, i: (n, 0, 0))
    w_in, b_in, cw, cb, wa, ba, wi, bi, lam, wo = wts
    return pl.pallas_call(
        functools.partial(_lru_prompt_kernel, tm=tm), grid=(n_seq, nb),
        in_specs=[pl.BlockSpec((tm, D), row), _full(w_in.shape), _full(b_in.shape), _full(cw.shape), _full(cb.shape),
                  _full(wa.shape), _full(ba.shape), _full(wi.shape), _full(bi.shape), _full(lam.shape),
                  _full(wo.shape), _full((1, D)), _full((1, D))],
        out_specs=[pl.BlockSpec((tm, D), row), last, last],
        out_shape=[jax.ShapeDtypeStruct((t, D), F32), jax.ShapeDtypeStruct((n_seq, SUBLANES, D), F32),
                   jax.ShapeDtypeStruct((n_seq, SUBLANES, D), F32)],
        scratch_shapes=[pltpu.VMEM((SUBLANES, D), F32), pltpu.VMEM((SUBLANES, D), F32)],
        compiler_params=_cparams(("parallel", "arbitrary")), name="lru_prompt")(x, *wts, g, b)


def _lru_sample_kernel(x_ref, c0_ref, c1_ref, c2_ref, h0_ref, win_ref, bin_ref, cw_ref, cb_ref, wa_ref, ba_ref,
                       wi_ref, bi_ref, lam_ref, wo_ref, g_ref, b_ref, o_ref, xb_ref, h_ref):
    x = x_ref[...]
    xy = _dot(x.astype(BF16), win_ref[...]) + bin_ref[...]
    xb = xy[:, :D]
    y_gate = _gelu_tanh(xy[:, D:])
    cw = cw_ref[...]
    xc = (cb_ref[...] + c0_ref[...] * cw[0:1] + c1_ref[...] * cw[1:2] + c2_ref[...] * cw[2:3] + xb * cw[3:4])
    a, b = _lru_gates(xc, wa_ref, ba_ref[...], wi_ref, bi_ref[...], lam_ref[...])
    h = a * h0_ref[...] + b
    xb_ref[...] = xb
    h_ref[...] = h
    acc = _dot((h * y_gate).astype(BF16), wo_ref[...])
    o_ref[...] = _ln(ALPHA * x + acc, g_ref[...], b_ref[...])


def _lru_sample(x, conv_state, h0, wts, g, b):
    t = x.shape[0]
    args = (x, conv_state[:, 0], conv_state[:, 1], conv_state[:, 2], h0, *wts, g, b)
    sd = jax.ShapeDtypeStruct((t, D), F32)
    return pl.pallas_call(
        _lru_sample_kernel, grid=(1,),
        in_specs=[_full(a.shape) for a in args],
        out_specs=[_full((t, D))] * 3, out_shape=[sd, sd, sd],
        compiler_params=_cparams(("arbitrary",)), name="lru_sample")(*args)


def _rwkv_pre_kernel(x_ref, xp_ref, mu_ref, wr_ref, wk_ref, wv_ref, w0_ref, w1_ref, w2_ref, a0_ref, a1_ref, a2_ref,
                     g1_ref, g2_ref, r_ref, k_ref, v_ref, a_ref, ld_ref, g_ref, *scratch, tm, seq):
    x = x_ref[...]
    if seq:
        cx_ref, = scratch
        i = pl.program_id(1)

        @pl.when(i == 0)
        def _():
            cx_ref[...] = xp_ref[0]

        x_prev = _shift_rows(jnp.concatenate([cx_ref[...], x], axis=0), 1, tm)
        cx_ref[...] = x[tm - SUBLANES:]
    else:
        x_prev = xp_ref[...]
    xx = x_prev - x
    mu = mu_ref[...]
    mix = lambda j: (x + xx * mu[j:j + 1]).astype(BF16)
    r_ref[...] = _dot(mix(0), wr_ref[...]).astype(r_ref.dtype)
    wl = _dot(jnp.tanh(_dot(mix(1), w1_ref[...])).astype(BF16), w2_ref[...])
    w = -_softplus(-(w0_ref[...] + wl)) - 0.5
    ld_ref[...] = -jnp.exp(w)
    k_ref[...] = _dot(mix(2), wk_ref[...]).astype(k_ref.dtype)
    v_ref[...] = _dot(mix(3), wv_ref[...]).astype(v_ref.dtype)
    al = _dot(_dot(mix(4), a1_ref[...]).astype(BF16), a2_ref[...])
    a_ref[...] = _sigmoid(a0_ref[...] + al).astype(a_ref.dtype)
    g_ref[...] = _dot(_sigmoid(_dot(mix(5), g1_ref[...])).astype(BF16), g2_ref[...]).astype(g_ref.dtype)


def _rwkv_pre(x, x_prev, wts, n_seq, tm, seq, dtype):
    t = x.shape[0]
    nb = t // n_seq // tm
    row = lambda n, i: (n * nb + i, 0)
    xp_spec = pl.BlockSpec((1, SUBLANES, D), lambda n, i: (n, 0, 0)) if seq else pl.BlockSpec((tm, D), row)
    sd = lambda dt: jax.ShapeDtypeStruct((t, D), dt)
    blk = pl.BlockSpec((tm, D), row)
    return pl.pallas_call(
        functools.partial(_rwkv_pre_kernel, tm=tm, seq=seq), grid=(n_seq, nb),
        in_specs=[blk, xp_spec] + [_full(w.shape) for w in wts],
        out_specs=[blk] * 6,
        out_shape=[sd(dtype), sd(dtype), sd(dtype), sd(dtype), sd(F32), sd(dtype)],
        scratch_shapes=[pltpu.VMEM((SUBLANES, D), F32)] if seq else [],
        compiler_params=_cparams(("parallel", "arbitrary")), name="rwkv_pre")(x, x_prev, *wts)


def _seg_sum(x, first):
    s0 = jnp.sum(jnp.where(first, x, 0.0), axis=-1, keepdims=True)
    s1 = jnp.sum(jnp.where(first, 0.0, x), axis=-1, keepdims=True)
    return jnp.where(first, s0, s1)


def _wkv_kernel(r_ref, k_ref, v_ref, a_ref, ld_ref, g_ref, kk_ref, ka_ref, rk_ref, gg_ref, gb_ref,
                o_ref, s_ref, st_ref):
    c = pl.program_id(1)
    L = WKV_CHUNK
    P2 = 2 * L

    @pl.when(c == 0)
    def _():
        st_ref[...] = jnp.zeros_like(st_ref)

    ld_all = ld_ref[...]
    tri = (lax.broadcasted_iota(jnp.int32, (L, L), 0) >= lax.broadcasted_iota(jnp.int32, (L, L), 1)).astype(BF16)
    hi = ld_all.astype(BF16)
    r1 = ld_all - hi.astype(F32)
    mid = r1.astype(BF16)
    lo = (r1 - mid.astype(F32)).astype(BF16)
    cum_all = _dot(tri, hi) + _dot(tri, mid) + _dot(tri, lo)

    lane = lax.broadcasted_iota(jnp.int32, (1, LANES), 1)
    first = lane < RWKV_HD
    ri = lax.broadcasted_iota(jnp.int32, (P2, P2), 0)
    ci = lax.broadcasted_iota(jnp.int32, (P2, P2), 1)
    same_head = (ri // L) == (ci // L)
    rt, ct = ri % L, ci % L
    strict = jnp.logical_and(same_head, rt > ct)
    incl = jnp.logical_and(same_head, rt >= ct)
    eye = ri == ci

    def stack(xv):
        return jnp.concatenate([jnp.where(first, xv, 0.0), jnp.where(first, 0.0, xv)], axis=0).astype(BF16)

    pairs = range(RWKV_HEADS // 2)
    sls = [slice(p * LANES, (p + 1) * LANES) for p in pairs]
    ws, us, ks, rs, ul, kl, vs, g_l, bonus = ([] for _ in range(9))
    for sl in sls:
        rp, kp, vp, ap = (ref[:, sl].astype(F32) for ref in (r_ref, k_ref, v_ref, a_ref))
        ldp, cum = ld_all[:, sl], cum_all[:, sl]
        kk = kp * kk_ref[:, sl]
        kk = kk / jnp.maximum(jnp.sqrt(_seg_sum(kk * kk, first)), 1e-12)
        kmod = kp * (1.0 + (ap - 1.0) * ka_ref[:, sl])
        bp = kk * ap
        cum_l = cum[L - 1:L, :]
        g_inv = jnp.exp(-cum)
        g_to_end = jnp.exp(cum_l - cum)
        ws.append(stack(kk * jnp.exp(cum - ldp)))
        us.append(stack(bp * g_inv))
        ks.append(stack(kmod * g_inv))
        rs.append(stack(rp * jnp.exp(cum)))
        ul.append(stack(bp * g_to_end))
        kl.append(stack(kmod * g_to_end))
        vs.append(stack(vp))
        g_l.append(jnp.exp(cum_l))
        bonus.append(_seg_sum(rp * kmod * rk_ref[:, sl], first) * vp)

    n_mat = [jnp.where(strict, _dot_nt(ws[p], us[p]), 0.0) for p in pairs]
    m_mat = [jnp.where(strict, _dot_nt(ws[p], ks[p]), 0.0).astype(BF16) for p in pairs]
    nr_mat = [jnp.where(incl, _dot_nt(rs[p], us[p]), 0.0).astype(BF16) for p in pairs]
    mr_mat = [jnp.where(incl, _dot_nt(rs[p], ks[p]), 0.0).astype(BF16) for p in pairs]

    def level_mask(sz):
        sub = jnp.logical_and((rt // sz) % 2 == 1, (ct // sz) % 2 == 0)
        return jnp.logical_and(jnp.logical_and(sub, (rt // (2 * sz)) == (ct // (2 * sz))), same_head)

    x_inv = [jnp.where(eye, 1.0, 0.0) - jnp.where(level_mask(1), n_mat[p], 0.0) for p in pairs]
    sz = 2
    while sz < L:
        mask = level_mask(sz)
        xb = [x_inv[p].astype(BF16) for p in pairs]
        xc = [_dot(xb[p], jnp.where(mask, n_mat[p], 0.0).astype(BF16)).astype(BF16) for p in pairs]
        x_inv = [x_inv[p] - _dot(xc[p], xb[p]) for p in pairs]
        sz *= 2

    a0 = [st_ref[p] for p in pairs]
    a0b = [a0[p].astype(BF16) for p in pairs]
    rhs = [(_dot(ws[p], a0b[p]) + _dot(m_mat[p], vs[p])).astype(BF16) for p in pairs]
    pm = [(-_dot(x_inv[p].astype(BF16), rhs[p])).astype(BF16) for p in pairs]
    o_st = [_dot(rs[p], a0b[p]) + _dot(nr_mat[p], pm[p]) + _dot(mr_mat[p], vs[p]) for p in pairs]
    for p in pairs:
        g_col = jnp.sum(jnp.where(eye, jnp.broadcast_to(g_l[p], (P2, P2)), 0.0), axis=-1, keepdims=True)
        st_ref[p] = g_col * a0[p] + _dot_tn(ul[p], pm[p]) + _dot_tn(kl[p], vs[p])

    inv_n = 1.0 / RWKV_HD
    for p, sl in zip(pairs, sls):
        o = o_st[p][:L] + o_st[p][L:]
        mu = _seg_sum(o, first) * inv_n
        oc = o - mu
        var = _seg_sum(oc * oc, first) * inv_n
        on = oc * lax.rsqrt(var + RWKV_GN_EPS) * gg_ref[:, sl] + gb_ref[:, sl]
        o_ref[:, sl] = ((on + bonus[p]) * g_ref[:, sl].astype(F32)).astype(o_ref.dtype)

    s_ref[0] = st_ref[...]


def _wkv_prompt(r, k, v, a, ld, g, hp, n_seq):
    t = r.shape[0]
    L = WKV_CHUNK
    nc = t // n_seq // L
    row = lambda n, c: (n * nc + c, 0)
    blk = pl.BlockSpec((L, D), row)
    npair = RWKV_HEADS // 2
    return pl.pallas_call(
        _wkv_kernel, grid=(n_seq, nc),
        in_specs=[blk] * 6 + [_full((1, D))] * 5,
        out_specs=[blk, pl.BlockSpec((1, npair, LANES, LANES), lambda n, c: (n, 0, 0, 0))],
        out_shape=[jax.ShapeDtypeStruct((t, D), BF16), jax.ShapeDtypeStruct((n_seq, npair, LANES, LANES), F32)],
        scratch_shapes=[pltpu.VMEM((npair, LANES, LANES), F32)],
        compiler_params=_cparams(("parallel", "arbitrary")), name="wkv_chunk")(r, k, v, a, ld, g, *hp)


def _wkv_sample_kernel(r_ref, k_ref, v_ref, a_ref, ld_ref, g_ref, s_ref, kk_ref, ka_ref, rk_ref, gg_ref, gb_ref,
                       o_ref, so_ref):
    hd = RWKV_HD
    eye = lax.broadcasted_iota(jnp.int32, (1, hd, hd), 1) == lax.broadcasted_iota(jnp.int32, (1, hd, hd), 2)
    for h in range(RWKV_HEADS):
        hs = slice(h, h + 1)
        r, k, v, a, ld, g = (ref[:, hs, :] for ref in (r_ref, k_ref, v_ref, a_ref, ld_ref, g_ref))
        s = s_ref[:, h]
        kk = k * kk_ref[hs, :]
        kk = kk / jnp.maximum(jnp.sqrt(jnp.sum(kk * kk, axis=-1, keepdims=True)), 1e-12)
        kmod = k * (1.0 + (a - 1.0) * ka_ref[hs, :])
        skk = jnp.sum(s * kk, axis=-1, keepdims=True)
        v_col = jnp.sum(jnp.where(eye, v, 0.0), axis=-1, keepdims=True)
        s_new = s * jnp.exp(ld) - skk * (kk * a) + v_col * kmod
        so_ref[:, h] = s_new
        o_col = jnp.sum(s_new * r, axis=-1, keepdims=True)
        o = jnp.sum(jnp.where(eye, o_col, 0.0), axis=1, keepdims=True)
        mu = jnp.mean(o, axis=-1, keepdims=True)
        oc = o - mu
        var = jnp.mean(oc * oc, axis=-1, keepdims=True)
        on = oc * lax.rsqrt(var + RWKV_GN_EPS) * gg_ref[hs, :] + gb_ref[hs, :]
        bonus = jnp.sum(r * kmod * rk_ref[hs, :], axis=-1, keepdims=True) * v
        o_ref[:, hs, :] = (on + bonus) * g


def _wkv_sample(r, k, v, a, ld, g, state, hp, bs):
    b = r.shape[0]
    h3 = lambda z: z.reshape(b, RWKV_HEADS, RWKV_HD)
    blk = pl.BlockSpec((bs, RWKV_HEADS, RWKV_HD), lambda i: (i, 0, 0))
    sblk = pl.BlockSpec((bs, RWKV_HEADS, RWKV_HD, RWKV_HD), lambda i: (i, 0, 0, 0))
    hp3 = [z.reshape(RWKV_HEADS, RWKV_HD) for z in hp]
    o, s_new = pl.pallas_call(
        _wkv_sample_kernel, grid=(b // bs,),
        in_specs=[blk] * 6 + [sblk] + [_full((RWKV_HEADS, RWKV_HD))] * 5,
        out_specs=[blk, sblk],
        out_shape=[jax.ShapeDtypeStruct((b, RWKV_HEADS, RWKV_HD), F32), jax.ShapeDtypeStruct(state.shape, F32)],
        compiler_params=_cparams(("parallel",)), name="wkv_sample")(
            h3(r), h3(k), h3(v), h3(a), h3(ld), h3(g), state, *hp3)
    return o.reshape(b, D), s_new


def _mem_prompt_kernel(x_ref, wq_ref, mk_ref, mv_ref, wo_ref, g_ref, b_ref, o_ref):
    x = x_ref[...]
    q = _dot(x.astype(BF16), wq_ref[...]).astype(BF16)
    scale = MEM_HD ** -0.5
    outs = []
    for h in range(MEM_HEADS):
        sl = slice(h * MEM_HD, (h + 1) * MEM_HD)
        s = _dot_nt(q[:, sl], mk_ref[0, :, sl]) * scale
        p = jnp.exp(s - jnp.max(s, axis=-1, keepdims=True))
        den = jnp.sum(p, axis=-1, keepdims=True)
        outs.append(_dot((p / den).astype(BF16), mv_ref[0, :, sl]).astype(BF16))
    acc = _dot(jnp.concatenate(outs, axis=-1), wo_ref[...])
    o_ref[...] = _ln(ALPHA * x + acc, g_ref[...], b_ref[...])


def _mem_attn_prompt(x, w_q, mk, mv, w_o, g, b, n_seq, tm):
    t = x.shape[0]
    nb = t // n_seq // tm
    m = mk.shape[1]
    row = lambda n, i: (n * nb + i, 0)
    mem = pl.BlockSpec((1, m, D), lambda n, i: (n, 0, 0))
    return pl.pallas_call(
        _mem_prompt_kernel, grid=(n_seq, nb),
        in_specs=[pl.BlockSpec((tm, D), row), _full((D, D)), mem, mem, _full((D, D)), _full((1, D)), _full((1, D))],
        out_specs=pl.BlockSpec((tm, D), row), out_shape=jax.ShapeDtypeStruct((t, D), F32),
        compiler_params=_cparams(("parallel", "arbitrary")), name="mem_attn")(x, w_q, mk, mv, w_o, g, b)


def _mem_sample_kernel(q_ref, ck_ref, cv_ref, o_ref, *, bs):
    scale = MEM_HD ** -0.5
    for b in range(bs):
        s = jnp.sum(_round_bf16(ck_ref[b]) * _round_bf16(q_ref[b]), axis=-1, keepdims=True) * scale
        p = jnp.exp(s - jnp.max(s, axis=0, keepdims=True))
        p = _round_bf16(p / jnp.sum(p, axis=0, keepdims=True))
        o_ref[b] = jnp.sum(p * _round_bf16(cv_ref[b]), axis=0, keepdims=True)


def _mem_attn_sample(q, cache_k, cache_v, layer, bs):
    _, b, m, nh, hd = cache_k.shape
    qb = pl.BlockSpec((bs, 1, nh, hd), lambda i: (i, 0, 0, 0))
    cb = pl.BlockSpec((None, bs, m, nh, hd), lambda i: (layer, i, 0, 0, 0))
    out = pl.pallas_call(
        functools.partial(_mem_sample_kernel, bs=bs), grid=(b // bs,), in_specs=[qb, cb, cb], out_specs=qb,
        out_shape=jax.ShapeDtypeStruct((b, 1, nh, hd), F32),
        compiler_params=_cparams(("parallel",)), name="mem_sample")(q.reshape(b, 1, nh, hd), cache_k, cache_v)
    return out.reshape(b, D)


_PAIRS = ((0, 1), (0, 2), (0, 3), (1, 2), (1, 3), (2, 3))


def _router_kernel(x_ref, rw_ref, rb_ref, bucket_ref, rank_ref, cnt_ref, base_ref, *, tm):
    i = pl.program_id(0)

    @pl.when(i == 0)
    def _():
        base_ref[...] = jnp.zeros_like(base_ref)

    logits = _dot_nt(rw_ref[...], x_ref[...].astype(BF16))
    e = jnp.exp(logits - jnp.max(logits, axis=0, keepdims=True))
    sel = e / jnp.sum(e, axis=0, keepdims=True) + rb_ref[...]
    s = [sel[j:j + 1, :] for j in range(N_EXPERTS)]
    neg = jnp.float32(-jnp.inf)

    best = jnp.zeros((1, tm), jnp.int32)
    best_score = None
    for gi in range(N_GROUPS):
        s0, s1, s2, s3 = s[4 * gi:4 * gi + 4]
        hi01, lo01, hi23, lo23 = jnp.maximum(s0, s1), jnp.minimum(s0, s1), jnp.maximum(s2, s3), jnp.minimum(s2, s3)
        score = jnp.maximum(hi01, hi23) + jnp.maximum(jnp.minimum(hi01, hi23), jnp.maximum(lo01, lo23))
        if gi == 0:
            best_score = score
        else:
            take = score > best_score
            best = jnp.where(take, gi, best)
            best_score = jnp.where(take, score, best_score)
    vals = []
    for j in range(EXPERTS_PER_GROUP):
        vj = s[j]
        for gi in range(1, N_GROUPS):
            vj = jnp.where(best == gi, s[4 * gi + j], vj)
        vals.append(vj)

    def argmax4(v):
        idx, mx = jnp.zeros((1, tm), jnp.int32), v[0]
        for j in range(1, EXPERTS_PER_GROUP):
            take = v[j] > mx
            idx = jnp.where(take, j, idx)
            mx = jnp.where(take, v[j], mx)
        return idx

    i1 = argmax4(vals)
    i2 = argmax4([jnp.where(i1 == j, neg, vals[j]) for j in range(EXPERTS_PER_GROUP)])
    lo, hi = jnp.minimum(i1, i2), jnp.maximum(i1, i2)
    pair = jnp.zeros((1, tm), jnp.int32)
    for pi, (pa, pb) in enumerate(_PAIRS):
        pair = jnp.where(jnp.logical_and(lo == pa, hi == pb), pi, pair)
    bucket = best * len(_PAIRS) + pair
    bucket_ref[0] = bucket

    onehot = (lax.broadcasted_iota(jnp.int32, (BUCKET_ROWS, tm), 0) == bucket).astype(F32)
    upper = (lax.broadcasted_iota(jnp.int32, (tm, tm), 0) <= lax.broadcasted_iota(jnp.int32, (tm, tm), 1)).astype(BF16)
    cum = _dot(onehot.astype(BF16), upper)
    base = base_ref[...]
    rank = jnp.sum(onehot * (cum + base), axis=0, keepdims=True) - 1.0
    rank_ref[0] = rank.astype(jnp.int32)
    base = base + jnp.sum(onehot, axis=1, keepdims=True)
    base_ref[...] = base
    cnt_ref[...] = jnp.broadcast_to(base, cnt_ref.shape)


def _router(x, rw_t, rb, tm):
    t = x.shape[0]
    nb = t // tm
    ib = pl.BlockSpec((1, 1, tm), lambda i: (i, 0, 0))
    bucket, rank, cnt = pl.pallas_call(
        functools.partial(_router_kernel, tm=tm), grid=(nb,),
        in_specs=[pl.BlockSpec((tm, D), lambda i: (i, 0)), _full(rw_t.shape), _full(rb.shape)],
        out_specs=[ib, ib, _full((BUCKET_ROWS, LANES))],
        out_shape=[jax.ShapeDtypeStruct((nb, 1, tm), jnp.int32), jax.ShapeDtypeStruct((nb, 1, tm), jnp.int32),
                   jax.ShapeDtypeStruct((BUCKET_ROWS, LANES), F32)],
        scratch_shapes=[pltpu.VMEM((BUCKET_ROWS, 1), F32)],
        compiler_params=_cparams(("arbitrary",)), name="router")(x, rw_t, rb)
    return bucket.reshape(t), rank.reshape(t), cnt[:N_BUCKETS, 0].astype(jnp.int32)


def _ffn_kernel(tgt_ref, lo_ref, hi_ref, used_ref, x_hbm, rw_ref, g0_ref, u0_ref, d0_ref, g1_ref, u1_ref, d1_ref,
                y_hbm, xbuf, ybuf, gsem, ssem, *, blk, n_tok):
    i = pl.program_id(0)
    used = used_ref[0]
    slot = i % 2
    other = 1 - slot
    n_blocks_ext = y_hbm.shape[0] // blk

    def gather_copy(b, r, s):
        tok = jnp.minimum(tgt_ref[(b + 2) * blk + r], n_tok - 1)
        return pltpu.make_async_copy(x_hbm.at[pl.ds(tok, 1)], xbuf.at[s, pl.ds(r, 1)], gsem.at[s])

    def scatter_copy(b, r, s):
        return pltpu.make_async_copy(ybuf.at[s, pl.ds(r, 1)], y_hbm.at[pl.ds(tgt_ref[(b + 2) * blk + r], 1)],
                                     ssem.at[s])

    def wait_all(copy, b, s):
        def body(r, carry):
            copy(b, r, s).wait()
            return carry
        lax.fori_loop(0, blk, body, 0, unroll=8)

    @pl.when(jnp.logical_and(i == 0, used > 0))
    def _():
        ybuf[...] = jnp.zeros_like(ybuf)
        for r in range(blk):
            gather_copy(0, r, 0).start()
            scatter_copy(-2, r, 0).start()

    @pl.when(i < used)
    def _():
        wait_all(gather_copy, i, slot)
        wait_all(scatter_copy, i - 2, slot)
        xb = xbuf[slot].astype(BF16)
        for r in range(blk):
            gather_copy(i + 1, r, other).start()
            scatter_copy(i - 1, r, other).start()
        logits = _dot(xb, rw_ref[...])
        lane = lax.broadcasted_iota(jnp.int32, logits.shape, 1)
        l_lo = jnp.sum(jnp.where(lane == lo_ref[i], logits, 0.0), axis=-1, keepdims=True)
        l_hi = jnp.sum(jnp.where(lane == hi_ref[i], logits, 0.0), axis=-1, keepdims=True)
        w_lo = _sigmoid(l_lo - l_hi)

        def expert(g_ref, u_ref, d_ref):
            gate = _dot(xb, g_ref[0])
            act = gate * _sigmoid(gate) * _dot(xb, u_ref[0])
            return _dot(act.astype(BF16), d_ref[0])

        y_lo = expert(g0_ref, u0_ref, d0_ref)
        y_hi = expert(g1_ref, u1_ref, d1_ref)
        ybuf[slot] = w_lo * y_lo + (1.0 - w_lo) * y_hi

    @pl.when(jnp.logical_and(i == used, used > 0))
    def _():
        wait_all(gather_copy, i, slot)
        wait_all(scatter_copy, i - 2, slot)
        for r in range(blk):
            scatter_copy(i - 1, r, other).start()
        wait_all(scatter_copy, i - 1, other)
        xbuf[0] = jnp.zeros((blk, D), F32)

        def zero_copy(b):
            return pltpu.make_async_copy(xbuf.at[0], y_hbm.at[pl.ds(b * blk, blk)], gsem.at[0])

        def zero_start(b, carry):
            zero_copy(b).start()
            return carry

        def zero_wait(b, carry):
            zero_copy(b).wait()
            return carry

        lax.fori_loop(used + 2, n_blocks_ext, zero_start, 0)
        lax.fori_loop(used + 2, n_blocks_ext, zero_wait, 0)


def _ffn(x, tgt, blk_lo, blk_hi, n_used, rw, w_gate, w_up, w_down, layer, blk):
    t = x.shape[0]
    nblk = tgt.shape[0] // blk - 3
    clamp = lambda i: jnp.minimum(i, nblk - 1)
    wg = lambda sel: pl.BlockSpec((None, 1, D, EXPERT_FF),
                                  lambda i, s, lo, hi, used: (layer, (lo, hi)[sel][clamp(i)], 0, 0))
    wd = lambda sel: pl.BlockSpec((None, 1, EXPERT_FF, D),
                                  lambda i, s, lo, hi, used: (layer, (lo, hi)[sel][clamp(i)], 0, 0))
    anyspec = pl.BlockSpec(memory_space=pl.ANY)
    return pl.pallas_call(
        functools.partial(_ffn_kernel, blk=blk, n_tok=t),
        grid_spec=pltpu.PrefetchScalarGridSpec(
            num_scalar_prefetch=4, grid=(nblk + 1,),
            in_specs=[anyspec, pl.BlockSpec(rw.shape, lambda i, s, lo, hi, used: (0, 0)),
                      wg(0), wg(0), wd(0), wg(1), wg(1), wd(1)],
            out_specs=anyspec,
            scratch_shapes=[pltpu.VMEM((2, blk, D), F32), pltpu.VMEM((2, blk, D), F32),
                            pltpu.SemaphoreType.DMA((2,)), pltpu.SemaphoreType.DMA((2,))]),
        out_shape=jax.ShapeDtypeStruct((tgt.shape[0], D), F32),
        compiler_params=_cparams(("arbitrary",)), name="moe_ffn")(
            tgt, blk_lo, blk_hi, n_used, x, rw, w_gate, w_up, w_down, w_gate, w_up, w_down)


def _add_ln_kernel(x_ref, y_ref, g_ref, b_ref, o_ref):
    o_ref[...] = _ln(ALPHA * x_ref[...] + y_ref[...], g_ref[...], b_ref[...])


def _add_ln(x, y, g, b, tm):
    t = x.shape[0]
    rowb = pl.BlockSpec((tm, D), lambda i: (i, 0))
    return pl.pallas_call(
        _add_ln_kernel, grid=(t // tm,), in_specs=[rowb, rowb, _full((1, D)), _full((1, D))], out_specs=rowb,
        out_shape=jax.ShapeDtypeStruct((t, D), F32),
        compiler_params=_cparams(("parallel",)), name="add_ln")(x, y, g, b)


def _moe_ln(x, rw_t, rb, rw_pad, w_gate, w_up, w_down, layer, g, b, tm_router, blk, tm_ln):
    t = x.shape[0]
    bucket, rank, counts = _router(x, rw_t, rb, tm_router)
    padded = (counts + blk - 1) // blk * blk
    ends = jnp.cumsum(padded)
    dest = ((ends - padded)[bucket] + rank).astype(jnp.int32)
    nblk = t // blk + N_BUCKETS
    n_ext = (nblk + 3) * blk
    tgt = jnp.full((n_ext,), -1, jnp.int32).at[dest + 2 * blk].set(jnp.arange(t, dtype=jnp.int32))
    empty = tgt < 0
    tgt = jnp.where(empty, t - 1 + jnp.cumsum(empty.astype(jnp.int32)), tgt)
    blk_bucket = jnp.minimum(jnp.searchsorted(ends, jnp.arange(nblk) * blk, side='right'), N_BUCKETS - 1)
    pair_lo = jnp.array([p[0] for p in _PAIRS], jnp.int32)
    pair_hi = jnp.array([p[1] for p in _PAIRS], jnp.int32)
    grp, pr = blk_bucket // len(_PAIRS), blk_bucket % len(_PAIRS)
    blk_lo = (grp * EXPERTS_PER_GROUP + pair_lo[pr]).astype(jnp.int32)
    blk_hi = (grp * EXPERTS_PER_GROUP + pair_hi[pr]).astype(jnp.int32)
    n_used = (ends[-1:] // blk).astype(jnp.int32)
    y_tok = _ffn(x, tgt, blk_lo, blk_hi, n_used, rw_pad, w_gate, w_up, w_down, layer, blk)
    return _add_ln(x, y_tok, g, b, tm_ln)


def kernel(x_prompt, x_sample, cache_swa_k, cache_swa_v, state_lru_conv, state_lru_h, state_rwkv_shift, state_rwkv_wkv, cache_mem_k, cache_mem_v, mem_prompt, swa_w_qkv, swa_sinks, swa_w_o, lru_w_in, lru_b_in, lru_conv_w, lru_conv_b, lru_w_a, lru_b_a, lru_w_i, lru_b_i, lru_lambda, lru_w_o, rwkv_mu, rwkv_w_r, rwkv_w_k, rwkv_w_v, rwkv_w0, rwkv_w1, rwkv_w2, rwkv_a0, rwkv_a1, rwkv_a2, rwkv_g1, rwkv_g2, rwkv_k_k, rwkv_k_a, rwkv_r_k, rwkv_gn_g, rwkv_gn_b, rwkv_w_o, mem_w_q, mem_w_kv, mem_w_o, ln_g, ln_b, router_w, router_b, moe_w_gate, moe_w_up, moe_w_down):
    n_p, seq, _ = x_prompt.shape
    n_s, dec_seq, _ = x_sample.shape
    assert dec_seq == 1
    past_len = 8192
    xp = x_prompt.reshape(n_p * seq, D)
    xs = x_sample.reshape(n_s, D)
    row = lambda v: v.reshape(1, -1)
    bf = lambda w: w.astype(BF16)

    rw_t = bf(router_w.T)
    rb = router_b.reshape(N_EXPERTS, 1)
    rw_pad = bf(jnp.pad(router_w, ((0, 0), (0, LANES - N_EXPERTS))))
    wg, wu, wd = bf(moe_w_gate), bf(moe_w_up), bf(moe_w_down)
    mem_p = mem_prompt.reshape(n_p * mem_prompt.shape[1], D)
    m_len = mem_prompt.shape[1]

    swa_k_p, swa_v_p, swa_k_s, swa_v_s = [], [], [], []
    lru_c_p, lru_h_p, lru_c_s, lru_h_s = [], [], [], []
    rw_x_p, rw_s_p, rw_x_s, rw_s_s = [], [], [], []
    mem_k_p, mem_v_p = [], []

    for layer in range(DEPTH):
        kind, i = layer % N_MIXERS, layer // N_MIXERS
        g0, b0 = row(ln_g[layer, 0]), row(ln_b[layer, 0])
        if kind == 0:
            w_qkv, w_o = bf(swa_w_qkv[i]), bf(swa_w_o[i])
            keep = min(WINDOW, seq)
            q, k, v, kv_last = _swa_qkv(xp, w_qkv, jnp.arange(seq), n_p, 512, keep, BF16)
            o = _swa_attn_prompt(q, k, v, swa_sinks[i], n_p)
            swa_k_p.append(kv_last[:, :, :KV_WIDTH].reshape(n_p, keep, SWA_KV_HEADS, HEAD_DIM))
            swa_v_p.append(kv_last[:, :, KV_WIDTH:].reshape(n_p, keep, SWA_KV_HEADS, HEAD_DIM))
            xp = _proj_ln(o, w_o, xp, g0, b0, 512)

            qs, _, _, kv_new = _swa_qkv(xs, w_qkv, jnp.full((n_s,), past_len), 1, n_s, n_s, F32)
            kn, vn = kv_new[0, :, :KV_WIDTH], kv_new[0, :, KV_WIDTH:]
            os_ = _swa_attn_sample(qs, kn, vn, cache_swa_k[i], cache_swa_v[i], swa_sinks[i], 8)
            wb = cache_swa_k.shape[2]
            k_all = jnp.concatenate([cache_swa_k[i], kn.reshape(n_s, 1, SWA_KV_HEADS, HEAD_DIM)], axis=1)
            v_all = jnp.concatenate([cache_swa_v[i], vn.reshape(n_s, 1, SWA_KV_HEADS, HEAD_DIM)], axis=1)
            swa_k_s.append(k_all[:, -wb:])
            swa_v_s.append(v_all[:, -wb:])
            xs = _proj_ln(os_, w_o, xs, g0, b0, n_s)
        elif kind == 1:
            wts = _lru_weights(lru_w_in[i], lru_b_in[i], lru_conv_w[i], lru_conv_b[i], lru_w_a[i], lru_b_a[i],
                               lru_w_i[i], lru_b_i[i], lru_lambda[i], lru_w_o[i])
            xp, conv_last, h_last = _lru_prompt(xp, wts, g0, b0, n_p, 256)
            lru_c_p.append(conv_last[:, SUBLANES - (CONV_W - 1):])
            lru_h_p.append(h_last[:, SUBLANES - 1])
            xs, xb_s, h_s = _lru_sample(xs, state_lru_conv[i], state_lru_h[i], wts, g0, b0)
            lru_c_s.append(jnp.concatenate([state_lru_conv[i][:, 1:], xb_s[:, None]], axis=1))
            lru_h_s.append(h_s)
        else:
            wts = (rwkv_mu[i], bf(rwkv_w_r[i]), bf(rwkv_w_k[i]), bf(rwkv_w_v[i]), row(rwkv_w0[i]), bf(rwkv_w1[i]),
                   bf(rwkv_w2[i]), row(rwkv_a0[i]), bf(rwkv_a1[i]), bf(rwkv_a2[i]), bf(rwkv_g1[i]), bf(rwkv_g2[i]))
            hp = (row(rwkv_k_k[i]), row(rwkv_k_a[i]), row(rwkv_r_k[i]), row(rwkv_gn_g[i]), row(rwkv_gn_b[i]))
            w_o = bf(rwkv_w_o[i])
            rw_x_p.append(xp.reshape(n_p, seq, D)[:, -1])
            rw_x_s.append(xs)
            r, k, v, a, ld, g = _rwkv_pre(xp, jnp.zeros((n_p, SUBLANES, D), F32), wts, n_p, 256, True, BF16)
            o, st = _wkv_prompt(r, k, v, a, ld, g, hp, n_p)
            hd = RWKV_HD
            st = jnp.stack([st[:, :, :hd, :hd], st[:, :, hd:, hd:]], axis=2).reshape(n_p, RWKV_HEADS, hd, hd)
            rw_s_p.append(jnp.swapaxes(st, -1, -2))
            xp = _proj_ln(o, w_o, xp, g0, b0, 512)

            r, k, v, a, ld, g = _rwkv_pre(xs, state_rwkv_shift[i], wts, 1, n_s, False, F32)
            os_, s_new = _wkv_sample(r, k, v, a, ld, g, state_rwkv_wkv[i], hp, 8)
            rw_s_s.append(s_new)
            xs = _proj_ln(os_, w_o, xs, g0, b0, n_s)

        g1, b1 = row(ln_g[layer, 1]), row(ln_b[layer, 1])
        w_q, w_o = bf(mem_w_q[layer]), bf(mem_w_o[layer])
        mkv = _matmul(mem_p, bf(mem_w_kv[layer]), 512)
        mk, mv = mkv[:, :D], mkv[:, D:]
        mem_k_p.append(mk.reshape(n_p, m_len, MEM_HEADS, MEM_HD))
        mem_v_p.append(mv.reshape(n_p, m_len, MEM_HEADS, MEM_HD))
        xp = _mem_attn_prompt(xp, w_q, bf(mk).reshape(n_p, m_len, D), bf(mv).reshape(n_p, m_len, D), w_o, g1, b1,
                              n_p, 512)
        qs = _matmul(xs, w_q, n_s)
        os_ = _mem_attn_sample(qs, cache_mem_k, cache_mem_v, layer, 4)
        xs = _proj_ln(os_, w_o, xs, g1, b1, n_s)

        g2, b2 = row(ln_g[layer, 2]), row(ln_b[layer, 2])
        xp = _moe_ln(xp, rw_t, rb, rw_pad, wg, wu, wd, layer, g2, b2, 512, 256, 512)
        xs = _moe_ln(xs, rw_t, rb, rw_pad, wg, wu, wd, layer, g2, b2, n_s, 8, n_s)

    return (xp.reshape(n_p, seq, D), xs.reshape(n_s, 1, D),
            jnp.stack(swa_k_p), jnp.stack(swa_v_p), jnp.stack(lru_c_p), jnp.stack(lru_h_p),
            jnp.stack(rw_x_p), jnp.stack(rw_s_p), jnp.stack(mem_k_p), jnp.stack(mem_v_p),
            jnp.stack(swa_k_s), jnp.stack(swa_v_s), jnp.stack(lru_c_s), jnp.stack(lru_h_s),
            jnp.stack(rw_x_s), jnp.stack(rw_s_s))
```

```python
import functools

import jax
import jax.numpy as jnp
from jax import lax
from jax.experimental import pallas as pl
from jax.experimental.pallas import tpu as pltpu

F32 = jnp.float32
BF16 = jnp.bfloat16

D = 1024
DEPTH = 4
N_MIXERS = 3
HEAD_DIM = 64
SWA_HEADS = D // HEAD_DIM
SWA_KV_HEADS = 4
SWA_GROUP = SWA_HEADS // SWA_KV_HEADS
Q_WIDTH = SWA_HEADS * HEAD_DIM
KV_WIDTH = SWA_KV_HEADS * HEAD_DIM
WINDOW = 128
ROT_DIM = HEAD_DIM // 4
ROPE_THETA = 500000.0
LRU_BLOCKS = 16
CONV_W = 4
LRU_C = 8.0
RWKV_HEADS = 16
RWKV_HD = 64
RWKV_GN_EPS = 64e-5
MEM_HEADS = 4
MEM_HD = D // MEM_HEADS
N_EXPERTS = 16
N_GROUPS = 4
EXPERTS_PER_GROUP = 4
EXPERT_FF = 512
LN_EPS = 1e-5
ALPHA = (2.0 * DEPTH) ** 0.25
NEG_INF = -1e30

LANES = 128
SUBLANES = 8
VMEM_LIMIT = 56 * 1024 * 1024
WKV_CHUNK = 64
N_BUCKETS = N_GROUPS * 6
BUCKET_ROWS = 32


def _cparams(sem):
    return pltpu.CompilerParams(dimension_semantics=sem, vmem_limit_bytes=VMEM_LIMIT)


def _dot(a, b):
    return jnp.dot(a, b, preferred_element_type=F32)


def _dot_nt(a, b):
    return lax.dot_general(a, b, (((1,), (1,)), ((), ())), preferred_element_type=F32)


def _dot_tn(a, b):
    return lax.dot_general(a, b, (((0,), (0,)), ((), ())), preferred_element_type=F32)


def _ln(z, g, b):
    mu = jnp.mean(z, axis=-1, keepdims=True)
    zc = z - mu
    var = jnp.mean(zc * zc, axis=-1, keepdims=True)
    return zc * lax.rsqrt(var + LN_EPS) * g + b


def _softplus(z):
    return jnp.maximum(z, 0.0) + jnp.log1p(jnp.exp(-jnp.abs(z)))


def _sigmoid(z):
    return 1.0 / (1.0 + jnp.exp(-z))


def _round_bf16(x):
    return x.astype(BF16).astype(F32)


def _full(shape):
    nd = len(shape)
    return pl.BlockSpec(shape, lambda *_: (0,) * nd)


def _mm_kernel(a_ref, w_ref, o_ref):
    o_ref[...] = _dot(a_ref[...].astype(BF16), w_ref[...]).astype(o_ref.dtype)


def _matmul(a, w, tm, out_dtype=F32):
    t, k = a.shape
    n = w.shape[1]
    return pl.pallas_call(
        _mm_kernel, grid=(t // tm,),
        in_specs=[pl.BlockSpec((tm, k), lambda i: (i, 0)), _full((k, n))],
        out_specs=pl.BlockSpec((tm, n), lambda i: (i, 0)),
        out_shape=jax.ShapeDtypeStruct((t, n), out_dtype),
        compiler_params=_cparams(("parallel",)), name="matmul")(a, w)


def _proj_ln_kernel(a_ref, w_ref, x_ref, g_ref, b_ref, o_ref):
    acc = _dot(a_ref[...].astype(BF16), w_ref[...])
    o_ref[...] = _ln(ALPHA * x_ref[...] + acc, g_ref[...], b_ref[...])


def _proj_ln(a, w, x, g, b, tm):
    t, k = a.shape
    return pl.pallas_call(
        _proj_ln_kernel, grid=(t // tm,),
        in_specs=[pl.BlockSpec((tm, k), lambda i: (i, 0)), _full((k, D)),
                  pl.BlockSpec((tm, D), lambda i: (i, 0)), _full((1, D)), _full((1, D))],
        out_specs=pl.BlockSpec((tm, D), lambda i: (i, 0)),
        out_shape=jax.ShapeDtypeStruct((t, D), F32),
        compiler_params=_cparams(("parallel",)), name="proj_ln")(a, w, x, g, b)


def _rope_tables(pos):
    half = ROT_DIM // 2
    inv_freq = ROPE_THETA ** (-jnp.arange(half, dtype=F32) / half)
    ang = pos.astype(F32)[:, None] * inv_freq
    cos, sin = jnp.cos(ang), jnp.sin(ang)
    one = jnp.ones((pos.shape[0], HEAD_DIM - ROT_DIM), F32)
    zero = jnp.zeros((pos.shape[0], HEAD_DIM - ROT_DIM), F32)
    zh = jnp.zeros_like(sin)
    c = jnp.concatenate([cos, cos, one], axis=1)
    s1 = jnp.concatenate([-sin, zh, zero], axis=1)
    s2 = jnp.concatenate([zh, sin, zero], axis=1)
    rep = LANES // HEAD_DIM
    return jnp.tile(c, (1, rep)), jnp.tile(s1, (1, rep)), jnp.tile(s2, (1, rep))


def _swa_qkv_kernel(x_ref, w_ref, c_ref, s1_ref, s2_ref, q_ref, k_ref, v_ref, kv_ref, *, tm, keep):
    acc = _dot(x_ref[...].astype(BF16), w_ref[...])
    c, s1, s2 = c_ref[...], s1_ref[...], s2_ref[...]
    half = ROT_DIM // 2
    n_q = Q_WIDTH // LANES
    n_k = KV_WIDTH // LANES
    for cg in range(n_q + n_k):
        xg = acc[:, cg * LANES:(cg + 1) * LANES]
        rot = xg * c + pltpu.roll(xg, LANES - half, 1) * s1 + pltpu.roll(xg, half, 1) * s2
        if cg < n_q:
            q_ref[:, cg * LANES:(cg + 1) * LANES] = rot.astype(q_ref.dtype)
        else:
            ck = cg - n_q
            k_ref[:, ck * LANES:(ck + 1) * LANES] = rot.astype(k_ref.dtype)
            kv_ref[0, :, ck * LANES:(ck + 1) * LANES] = rot[tm - keep:, :]
    v = acc[:, Q_WIDTH + KV_WIDTH:]
    v_ref[...] = v.astype(v_ref.dtype)
    kv_ref[0, :, KV_WIDTH:] = v[tm - keep:, :]


def _swa_qkv(x, w_qkv, pos, n_seq, tm, keep, qdtype):
    t = x.shape[0]
    s = t // n_seq
    nb = s // tm
    c, s1, s2 = _rope_tables(pos)
    row = lambda n, i: (n * nb + i, 0)
    tab = pl.BlockSpec((tm, LANES), lambda n, i: (i, 0))
    kern = functools.partial(_swa_qkv_kernel, tm=tm, keep=keep)
    return pl.pallas_call(
        kern, grid=(n_seq, nb),
        in_specs=[pl.BlockSpec((tm, D), row), _full((D, Q_WIDTH + 2 * KV_WIDTH)), tab, tab, tab],
        out_specs=[pl.BlockSpec((tm, Q_WIDTH), row), pl.BlockSpec((tm, KV_WIDTH), row),
                   pl.BlockSpec((tm, KV_WIDTH), row),
                   pl.BlockSpec((1, keep, 2 * KV_WIDTH), lambda n, i: (n, 0, 0))],
        out_shape=[jax.ShapeDtypeStruct((t, Q_WIDTH), qdtype), jax.ShapeDtypeStruct((t, KV_WIDTH), qdtype),
                   jax.ShapeDtypeStruct((t, KV_WIDTH), qdtype),
                   jax.ShapeDtypeStruct((n_seq, keep, 2 * KV_WIDTH), F32)],
        compiler_params=_cparams(("parallel", "arbitrary")), name="swa_qkv")(x, w_qkv, c, s1, s2)


def _swa_attn_kernel(sink_ref, q_ref, kp_ref, kc_ref, vp_ref, vc_ref, o_ref, *, nq):
    j = pl.program_id(1)
    w, grp = WINDOW, SWA_GROUP
    r = lax.broadcasted_iota(jnp.int32, (grp * w, 2 * w), 0) % w
    c = lax.broadcasted_iota(jnp.int32, (grp * w, 2 * w), 1)
    in_prev = jnp.logical_and(c < w, c > r)
    in_cur = jnp.logical_and(c >= w, (c - w) <= r)
    ok_inner = jnp.logical_or(in_prev, in_cur)
    ok_first = jnp.logical_or(jnp.logical_and(in_prev, j > 0), in_cur)
    scale = HEAD_DIM ** -0.5
    combos = [(u, h) for u in range(nq) for h in range(SWA_KV_HEADS)]
    kcat, vcat, q4, sink, ok = [], [], [], [], []
    for u, h in combos:
        sl = slice(h * HEAD_DIM, (h + 1) * HEAD_DIM)
        rows = slice(u * w, (u + 1) * w)
        before = slice((u - 1) * w, u * w)
        k_prev = kp_ref[:, sl] if u == 0 else kc_ref[before, sl]
        v_prev = vp_ref[:, sl] if u == 0 else vc_ref[before, sl]
        kcat.append(jnp.concatenate([k_prev, kc_ref[rows, sl]], axis=0))
        vcat.append(jnp.concatenate([v_prev, vc_ref[rows, sl]], axis=0))
        heads = [h * grp + g for g in range(grp)]
        q4.append(jnp.concatenate([q_ref[rows, hq * HEAD_DIM:(hq + 1) * HEAD_DIM] for hq in heads], axis=0))
        sink.append(jnp.concatenate([jnp.full((w, 1), sink_ref[hq], F32) for hq in heads], axis=0))
        ok.append(ok_first if u == 0 else ok_inner)
    n = range(len(combos))
    s = [jnp.where(ok[i], _dot_nt(q4[i], kcat[i]) * scale, NEG_INF) for i in n]
    m = [jnp.maximum(jnp.max(s[i], axis=-1, keepdims=True), sink[i]) for i in n]
    p = [jnp.exp(s[i] - m[i]) for i in n]
    den = [jnp.sum(p[i], axis=-1, keepdims=True) + jnp.exp(sink[i] - m[i]) for i in n]
    o = [_dot((p[i] / den[i]).astype(BF16), vcat[i]) for i in n]
    for i, (u, h) in enumerate(combos):
        for g in range(grp):
            hq = h * grp + g
            o_ref[u * w:(u + 1) * w, hq * HEAD_DIM:(hq + 1) * HEAD_DIM] = o[i][g * w:(g + 1) * w].astype(o_ref.dtype)


def _swa_attn_prompt(q, k, v, sinks, n_seq, nq):
    t = q.shape[0]
    nb = t // n_seq // WINDOW
    ns = nb // nq
    cur = lambda n, j: (n * ns + j, 0)
    prev = lambda n, j: (n * nb + jnp.maximum(j * nq - 1, 0), 0)
    return pl.pallas_call(
        functools.partial(_swa_attn_kernel, nq=nq), grid=(n_seq, ns),
        in_specs=[pl.BlockSpec(memory_space=pltpu.SMEM), pl.BlockSpec((nq * WINDOW, Q_WIDTH), cur),
                  pl.BlockSpec((WINDOW, KV_WIDTH), prev), pl.BlockSpec((nq * WINDOW, KV_WIDTH), cur),
                  pl.BlockSpec((WINDOW, KV_WIDTH), prev), pl.BlockSpec((nq * WINDOW, KV_WIDTH), cur)],
        out_specs=pl.BlockSpec((nq * WINDOW, Q_WIDTH), cur),
        out_shape=jax.ShapeDtypeStruct((t, Q_WIDTH), BF16),
        compiler_params=_cparams(("parallel", "arbitrary")), name="swa_attn")(sinks, q, k, k, v, v)


def _swa_sample_kernel(sink_ref, q_ref, kn_ref, vn_ref, ck_ref, cv_ref, o_ref, *, wb):
    kidx = lax.broadcasted_iota(jnp.int32, (1, wb, 1), 1)
    valid = (wb - kidx) < WINDOW
    scale = HEAD_DIM ** -0.5
    for h in range(SWA_KV_HEADS):
        sl = slice(h * HEAD_DIM, (h + 1) * HEAD_DIM)
        ck, cv = _round_bf16(ck_ref[:, :, sl]), _round_bf16(cv_ref[:, :, sl])
        kn, vn = _round_bf16(kn_ref[:, :, sl]), _round_bf16(vn_ref[:, :, sl])
        for g in range(SWA_GROUP):
            hq = h * SWA_GROUP + g
            qs = slice(hq * HEAD_DIM, (hq + 1) * HEAD_DIM)
            qh = _round_bf16(q_ref[:, :, qs])
            s = jnp.where(valid, jnp.sum(ck * qh, axis=-1, keepdims=True) * scale, NEG_INF)
            sn = jnp.sum(kn * qh, axis=-1, keepdims=True) * scale
            sink = sink_ref[hq]
            m = jnp.maximum(jnp.maximum(jnp.max(s, axis=1, keepdims=True), sn), sink)
            p = jnp.exp(s - m)
            pn = jnp.exp(sn - m)
            den = jnp.sum(p, axis=1, keepdims=True) + pn + jnp.exp(sink - m)
            p, pn = _round_bf16(p / den), _round_bf16(pn / den)
            o_ref[:, :, qs] = jnp.sum(p * cv, axis=1, keepdims=True) + pn * vn


def _swa_attn_sample(q, kn, vn, cache_k, cache_v, sinks, bs):
    b, wb = cache_k.shape[0], cache_k.shape[1]
    blk3 = lambda w: pl.BlockSpec((bs, 1, w), lambda i: (i, 0, 0))
    cblk = pl.BlockSpec((bs, wb, KV_WIDTH), lambda i: (i, 0, 0))
    out = pl.pallas_call(
        functools.partial(_swa_sample_kernel, wb=wb), grid=(b // bs,),
        in_specs=[pl.BlockSpec(memory_space=pltpu.SMEM), blk3(Q_WIDTH), blk3(KV_WIDTH), blk3(KV_WIDTH), cblk, cblk],
        out_specs=blk3(Q_WIDTH), out_shape=jax.ShapeDtypeStruct((b, 1, Q_WIDTH), F32),
        compiler_params=_cparams(("parallel",)), name="swa_sample")(
            sinks, q.reshape(b, 1, Q_WIDTH), kn.reshape(b, 1, KV_WIDTH), vn.reshape(b, 1, KV_WIDTH),
            cache_k.reshape(b, wb, KV_WIDTH), cache_v.reshape(b, wb, KV_WIDTH))
    return out.reshape(b, Q_WIDTH)


def _gelu_tanh(x):
    return 0.5 * x * (1.0 + jnp.tanh(0.7978845608028654 * (x + 0.044715 * x * x * x)))


def _lru_gates(xc, wa_ref, ba, wi_ref, bi, lam):
    xcb = xc.astype(BF16)
    gw = wa_ref.shape[1]
    ra, ia = [], []
    for gi in range(wa_ref.shape[0]):
        xs = xcb[:, gi * gw:(gi + 1) * gw]
        ra.append(_dot(xs, wa_ref[gi]))
        ia.append(_dot(xs, wi_ref[gi]))
    r = _sigmoid(jnp.concatenate(ra, axis=-1) + ba)
    ig = _sigmoid(jnp.concatenate(ia, axis=-1) + bi)
    log_a = -LRU_C * r * _softplus(-lam)
    a = jnp.exp(log_a)
    b = jnp.sqrt(-jnp.tanh(log_a) * (a * a + 1.0)) * (ig * xc)
    return a, b


def _shift_rows(ext, s, tm):
    return pltpu.roll(ext, s, 0)[SUBLANES:SUBLANES + tm]


def _lru_prompt_kernel(x_ref, win_ref, bin_ref, cw_ref, cb_ref, wa_ref, ba_ref, wi_ref, bi_ref, lam_ref,
                       wo_ref, g_ref, b_ref, o_ref, conv_ref, hl_ref, cx_ref, ch_ref, *, tm):
    i = pl.program_id(1)

    @pl.when(i == 0)
    def _():
        cx_ref[...] = jnp.zeros_like(cx_ref)
        ch_ref[...] = jnp.zeros_like(ch_ref)

    x = x_ref[...]
    xy = _dot(x.astype(BF16), win_ref[...]) + bin_ref[...]
    xb = xy[:, :D]
    y_gate = _gelu_tanh(xy[:, D:])
    ext = jnp.concatenate([cx_ref[...], xb], axis=0)
    cw = cw_ref[...]
    xc = cb_ref[...] + xb * cw[CONV_W - 1:CONV_W]
    for s in range(1, CONV_W):
        xc = xc + _shift_rows(ext, s, tm) * cw[CONV_W - 1 - s:CONV_W - s]
    cx_ref[...] = xb[tm - SUBLANES:]
    conv_ref[0] = xb[tm - SUBLANES:]

    a, b = _lru_gates(xc, wa_ref, ba_ref[...], wi_ref, bi_ref[...], lam_ref[...])
    sub = lax.broadcasted_iota(jnp.int32, (tm, 1), 0) % SUBLANES
    s = 1
    while s < SUBLANES:
        keep = sub >= s
        a_sh = jnp.where(keep, pltpu.roll(a, s, 0), 1.0)
        b_sh = jnp.where(keep, pltpu.roll(b, s, 0), 0.0)
        b = a * b_sh + b
        a = a * a_sh
        s *= 2
    carry = ch_ref[SUBLANES - 1:SUBLANES, :]
    groups = []
    for gi in range(tm // SUBLANES):
        rows = slice(gi * SUBLANES, (gi + 1) * SUBLANES)
        hg = a[rows] * carry + b[rows]
        groups.append(hg)
        carry = hg[SUBLANES - 1:SUBLANES]
    h = jnp.concatenate(groups, axis=0)
    ch_ref[...] = h[tm - SUBLANES:]
    hl_ref[0] = h[tm - SUBLANES:]
    acc = _dot((h * y_gate).astype(BF16), wo_ref[...])
    o_ref[...] = _ln(ALPHA * x + acc, g_ref[...], b_ref[...])


def _lru_weights(w_in, b_in, conv_w, conv_b, w_a, b_a, w_i, b_i, lam, w_o):
    gsz = 4
    ng = LRU_BLOCKS // gsz
    bw = D // LRU_BLOCKS

    def grouped(w):
        w4 = w.reshape(ng, gsz, bw, bw)
        return jnp.einsum('gaij,ab->gaibj', w4, jnp.eye(gsz, dtype=w.dtype)).reshape(ng, gsz * bw, gsz * bw).astype(BF16)

    row = lambda v: v.reshape(1, -1)
    return (w_in.astype(BF16), row(b_in), conv_w, row(conv_b), grouped(w_a), row(b_a), grouped(w_i), row(b_i),
            row(lam), w_o.astype(BF16))


def _lru_prompt(x, wts, g, b, n_seq, tm):
    t = x.shape[0]
    nb = t // n_seq // tm
    row = lambda n, i: (n * nb + i, 0)
    last = pl.BlockSpec((1, SUBLANES, D), lambda n, i: (n, 0, 0))
    w_in, b_in, cw, cb, wa, ba, wi, bi, lam, wo = wts
    return pl.pallas_call(
        functools.partial(_lru_prompt_kernel, tm=tm), grid=(n_seq, nb),
        in_specs=[pl.BlockSpec((tm, D), row), _full(w_in.shape), _full(b_in.shape), _full(cw.shape), _full(cb.shape),
                  _full(wa.shape), _full(ba.shape), _full(wi.shape), _full(bi.shape), _full(lam.shape),
                  _full(wo.shape), _full((1, D)), _full((1, D))],
        out_specs=[pl.BlockSpec((tm, D), row), last, last],
        out_shape=[jax.ShapeDtypeStruct((t, D), F32), jax.ShapeDtypeStruct((n_seq, SUBLANES, D), F32),
                   jax.ShapeDtypeStruct((n_seq, SUBLANES, D), F32)],
        scratch_shapes=[pltpu.VMEM((SUBLANES, D), F32), pltpu.VMEM((SUBLANES, D), F32)],
        compiler_params=_cparams(("parallel", "arbitrary")), name="lru_prompt")(x, *wts, g, b)


def _lru_sample_kernel(x_ref, c0_ref, c1_ref, c2_ref, h0_ref, win_ref, bin_ref, cw_ref, cb_ref, wa_ref, ba_ref,
                       wi_ref, bi_ref, lam_ref, wo_ref, g_ref, b_ref, o_ref, xb_ref, h_ref):
    x = x_ref[...]
    xy = _dot(x.astype(BF16), win_ref[...]) + bin_ref[...]
    xb = xy[:, :D]
    y_gate = _gelu_tanh(xy[:, D:])
    cw = cw_ref[...]
    xc = (cb_ref[...] + c0_ref[...] * cw[0:1] + c1_ref[...] * cw[1:2] + c2_ref[...] * cw[2:3] + xb * cw[3:4])
    a, b = _lru_gates(xc, wa_ref, ba_ref[...], wi_ref, bi_ref[...], lam_ref[...])
    h = a * h0_ref[...] + b
    xb_ref[...] = xb
    h_ref[...] = h
    acc = _dot((h * y_gate).astype(BF16), wo_ref[...])
    o_ref[...] = _ln(ALPHA * x + acc, g_ref[...], b_ref[...])


def _lru_sample(x, conv_state, h0, wts, g, b):
    t = x.shape[0]
    args = (x, conv_state[:, 0], conv_state[:, 1], conv_state[:, 2], h0, *wts, g, b)
    sd = jax.ShapeDtypeStruct((t, D), F32)
    return pl.pallas_call(
        _lru_sample_kernel, grid=(1,),
        in_specs=[_full(a.shape) for a in args],
        out_specs=[_full((t, D))] * 3, out_shape=[sd, sd, sd],
        compiler_params=_cparams(("arbitrary",)), name="lru_sample")(*args)


def _rwkv_pre_kernel(x_ref, xp_ref, mu_ref, wr_ref, wk_ref, wv_ref, w0_ref, w1_ref, w2_ref, a0_ref, a1_ref, a2_ref,
                     g1_ref, g2_ref, r_ref, k_ref, v_ref, a_ref, ld_ref, g_ref, *scratch, tm, seq):
    x = x_ref[...]
    if seq:
        cx_ref, = scratch
        i = pl.program_id(1)

        @pl.when(i == 0)
        def _():
            cx_ref[...] = xp_ref[0]

        x_prev = _shift_rows(jnp.concatenate([cx_ref[...], x], axis=0), 1, tm)
        cx_ref[...] = x[tm - SUBLANES:]
    else:
        x_prev = xp_ref[...]
    xx = x_prev - x
    mu = mu_ref[...]
    mix = lambda j: (x + xx * mu[j:j + 1]).astype(BF16)
    r_ref[...] = _dot(mix(0), wr_ref[...]).astype(r_ref.dtype)
    wl = _dot(jnp.tanh(_dot(mix(1), w1_ref[...])).astype(BF16), w2_ref[...])
    w = -_softplus(-(w0_ref[...] + wl)) - 0.5
    ld_ref[...] = -jnp.exp(w)
    k_ref[...] = _dot(mix(2), wk_ref[...]).astype(k_ref.dtype)
    v_ref[...] = _dot(mix(3), wv_ref[...]).astype(v_ref.dtype)
    al = _dot(_dot(mix(4), a1_ref[...]).astype(BF16), a2_ref[...])
    a_ref[...] = _sigmoid(a0_ref[...] + al).astype(a_ref.dtype)
    g_ref[...] = _dot(_sigmoid(_dot(mix(5), g1_ref[...])).astype(BF16), g2_ref[...]).astype(g_ref.dtype)


def _rwkv_pre(x, x_prev, wts, n_seq, tm, seq, dtype):
    t = x.shape[0]
    nb = t // n_seq // tm
    row = lambda n, i: (n * nb + i, 0)
    xp_spec = pl.BlockSpec((1, SUBLANES, D), lambda n, i: (n, 0, 0)) if seq else pl.BlockSpec((tm, D), row)
    sd = lambda dt: jax.ShapeDtypeStruct((t, D), dt)
    blk = pl.BlockSpec((tm, D), row)
    return pl.pallas_call(
        functools.partial(_rwkv_pre_kernel, tm=tm, seq=seq), grid=(n_seq, nb),
        in_specs=[blk, xp_spec] + [_full(w.shape) for w in wts],
        out_specs=[blk] * 6,
        out_shape=[sd(dtype), sd(dtype), sd(dtype), sd(dtype), sd(F32), sd(dtype)],
        scratch_shapes=[pltpu.VMEM((SUBLANES, D), F32)] if seq else [],
        compiler_params=_cparams(("parallel", "arbitrary")), name="rwkv_pre")(x, x_prev, *wts)


def _seg_sum(x, first):
    s0 = jnp.sum(jnp.where(first, x, 0.0), axis=-1, keepdims=True)
    s1 = jnp.sum(jnp.where(first, 0.0, x), axis=-1, keepdims=True)
    return jnp.where(first, s0, s1)


def _wkv_kernel(r_ref, k_ref, v_ref, a_ref, ld_ref, g_ref, kk_ref, ka_ref, rk_ref, gg_ref, gb_ref,
                o_ref, s_ref, st_ref):
    c = pl.program_id(1)
    L = WKV_CHUNK
    P2 = 2 * L

    @pl.when(c == 0)
    def _():
        st_ref[...] = jnp.zeros_like(st_ref)

    ld_all = ld_ref[...]
    tri = (lax.broadcasted_iota(jnp.int32, (L, L), 0) >= lax.broadcasted_iota(jnp.int32, (L, L), 1)).astype(BF16)
    hi = ld_all.astype(BF16)
    r1 = ld_all - hi.astype(F32)
    mid = r1.astype(BF16)
    lo = (r1 - mid.astype(F32)).astype(BF16)
    cum_all = _dot(tri, hi) + _dot(tri, mid) + _dot(tri, lo)

    lane = lax.broadcasted_iota(jnp.int32, (1, LANES), 1)
    first = lane < RWKV_HD
    ri = lax.broadcasted_iota(jnp.int32, (P2, P2), 0)
    ci = lax.broadcasted_iota(jnp.int32, (P2, P2), 1)
    same_head = (ri // L) == (ci // L)
    rt, ct = ri % L, ci % L
    strict = jnp.logical_and(same_head, rt > ct)
    incl = jnp.logical_and(same_head, rt >= ct)
    eye = ri == ci

    def stack(xv):
        return jnp.concatenate([jnp.where(first, xv, 0.0), jnp.where(first, 0.0, xv)], axis=0).astype(BF16)

    pairs = range(RWKV_HEADS // 2)
    sls = [slice(p * LANES, (p + 1) * LANES) for p in pairs]
    ws, us, ks, rs, ul, kl, vs, g_l, bonus = ([] for _ in range(9))
    for sl in sls:
        rp, kp, vp, ap = (ref[:, sl].astype(F32) for ref in (r_ref, k_ref, v_ref, a_ref))
        ldp, cum = ld_all[:, sl], cum_all[:, sl]
        kk = kp * kk_ref[:, sl]
        kk = kk / jnp.maximum(jnp.sqrt(_seg_sum(kk * kk, first)), 1e-12)
        kmod = kp * (1.0 + (ap - 1.0) * ka_ref[:, sl])
        bp = kk * ap
        cum_l = cum[L - 1:L, :]
        g_inv = jnp.exp(-cum)
        g_to_end = jnp.exp(cum_l - cum)
        ws.append(stack(kk * jnp.exp(cum - ldp)))
        us.append(stack(bp * g_inv))
        ks.append(stack(kmod * g_inv))
        rs.append(stack(rp * jnp.exp(cum)))
        ul.append(stack(bp * g_to_end))
        kl.append(stack(kmod * g_to_end))
        vs.append(stack(vp))
        g_l.append(jnp.exp(cum_l))
        bonus.append(_seg_sum(rp * kmod * rk_ref[:, sl], first) * vp)

    n_mat = [jnp.where(strict, _dot_nt(ws[p], us[p]), 0.0) for p in pairs]
    m_mat = [jnp.where(strict, _dot_nt(ws[p], ks[p]), 0.0).astype(BF16) for p in pairs]
    nr_mat = [jnp.where(incl, _dot_nt(rs[p], us[p]), 0.0).astype(BF16) for p in pairs]
    mr_mat = [jnp.where(incl, _dot_nt(rs[p], ks[p]), 0.0).astype(BF16) for p in pairs]

    def level_mask(sz):
        sub = jnp.logical_and((rt // sz) % 2 == 1, (ct // sz) % 2 == 0)
        return jnp.logical_and(jnp.logical_and(sub, (rt // (2 * sz)) == (ct // (2 * sz))), same_head)

    x_inv = [jnp.where(eye, 1.0, 0.0) - jnp.where(level_mask(1), n_mat[p], 0.0) for p in pairs]
    sz = 2
    while sz < L:
        mask = level_mask(sz)
        xb = [x_inv[p].astype(BF16) for p in pairs]
        xc = [_dot(xb[p], jnp.where(mask, n_mat[p], 0.0).astype(BF16)).astype(BF16) for p in pairs]
        x_inv = [x_inv[p] - _dot(xc[p], xb[p]) for p in pairs]
        sz *= 2

    a0 = [st_ref[p] for p in pairs]
    a0b = [a0[p].astype(BF16) for p in pairs]
    rhs = [(_dot(ws[p], a0b[p]) + _dot(m_mat[p], vs[p])).astype(BF16) for p in pairs]
    pm = [(-_dot(x_inv[p].astype(BF16), rhs[p])).astype(BF16) for p in pairs]
    o_st = [_dot(rs[p], a0b[p]) + _dot(nr_mat[p], pm[p]) + _dot(mr_mat[p], vs[p]) for p in pairs]
    for p in pairs:
        g_col = jnp.sum(jnp.where(eye, jnp.broadcast_to(g_l[p], (P2, P2)), 0.0), axis=-1, keepdims=True)
        st_ref[p] = g_col * a0[p] + _dot_tn(ul[p], pm[p]) + _dot_tn(kl[p], vs[p])

    inv_n = 1.0 / RWKV_HD
    for p, sl in zip(pairs, sls):
        o = o_st[p][:L] + o_st[p][L:]
        mu = _seg_sum(o, first) * inv_n
        oc = o - mu
        var = _seg_sum(oc * oc, first) * inv_n
        on = oc * lax.rsqrt(var + RWKV_GN_EPS) * gg_ref[:, sl] + gb_ref[:, sl]
        o_ref[:, sl] = ((on + bonus[p]) * g_ref[:, sl].astype(F32)).astype(o_ref.dtype)

    s_ref[0] = st_ref[...]


def _wkv_prompt(r, k, v, a, ld, g, hp, n_seq):
    t = r.shape[0]
    L = WKV_CHUNK
    nc = t // n_seq // L
    row = lambda n, c: (n * nc + c, 0)
    blk = pl.BlockSpec((L, D), row)
    npair = RWKV_HEADS // 2
    return pl.pallas_call(
        _wkv_kernel, grid=(n_seq, nc),
        in_specs=[blk] * 6 + [_full((1, D))] * 5,
        out_specs=[blk, pl.BlockSpec((1, npair, LANES, LANES), lambda n, c: (n, 0, 0, 0))],
        out_shape=[jax.ShapeDtypeStruct((t, D), BF16), jax.ShapeDtypeStruct((n_seq, npair, LANES, LANES), F32)],
        scratch_shapes=[pltpu.VMEM((npair, LANES, LANES), F32)],
        compiler_params=_cparams(("parallel", "arbitrary")), name="wkv_chunk")(r, k, v, a, ld, g, *hp)


def _wkv_sample_kernel(r_ref, k_ref, v_ref, a_ref, ld_ref, g_ref, s_ref, kk_ref, ka_ref, rk_ref, gg_ref, gb_ref,
                       o_ref, so_ref):
    hd = RWKV_HD
    eye = lax.broadcasted_iota(jnp.int32, (1, hd, hd), 1) == lax.broadcasted_iota(jnp.int32, (1, hd, hd), 2)
    for h in range(RWKV_HEADS):
        hs = slice(h, h + 1)
        r, k, v, a, ld, g = (ref[:, hs, :] for ref in (r_ref, k_ref, v_ref, a_ref, ld_ref, g_ref))
        s = s_ref[:, h]
        kk = k * kk_ref[hs, :]
        kk = kk / jnp.maximum(jnp.sqrt(jnp.sum(kk * kk, axis=-1, keepdims=True)), 1e-12)
        kmod = k * (1.0 + (a - 1.0) * ka_ref[hs, :])
        skk = jnp.sum(s * kk, axis=-1, keepdims=True)
        v_col = jnp.sum(jnp.where(eye, v, 0.0), axis=-1, keepdims=True)
        s_new = s * jnp.exp(ld) - skk * (kk * a) + v_col * kmod
        so_ref[:, h] = s_new
        o_col = jnp.sum(s_new * r, axis=-1, keepdims=True)
        o = jnp.sum(jnp.where(eye, o_col, 0.0), axis=1, keepdims=True)
        mu = jnp.mean(o, axis=-1, keepdims=True)
        oc = o - mu
        var = jnp.mean(oc * oc, axis=-1, keepdims=True)
        on = oc * lax.rsqrt(var + RWKV_GN_EPS) * gg_ref[hs, :] + gb_ref[hs, :]
        bonus = jnp.sum(r * kmod * rk_ref[hs, :], axis=-1, keepdims=True) * v
        o_ref[:, hs, :] = (on + bonus) * g


def _wkv_sample(r, k, v, a, ld, g, state, hp, bs):
    b = r.shape[0]
    h3 = lambda z: z.reshape(b, RWKV_HEADS, RWKV_HD)
    blk = pl.BlockSpec((bs, RWKV_HEADS, RWKV_HD), lambda i: (i, 0, 0))
    sblk = pl.BlockSpec((bs, RWKV_HEADS, RWKV_HD, RWKV_HD), lambda i: (i, 0, 0, 0))
    hp3 = [z.reshape(RWKV_HEADS, RWKV_HD) for z in hp]
    o, s_new = pl.pallas_call(
        _wkv_sample_kernel, grid=(b // bs,),
        in_specs=[blk] * 6 + [sblk] + [_full((RWKV_HEADS, RWKV_HD))] * 5,
        out_specs=[blk, sblk],
        out_shape=[jax.ShapeDtypeStruct((b, RWKV_HEADS, RWKV_HD), F32), jax.ShapeDtypeStruct(state.shape, F32)],
        compiler_params=_cparams(("parallel",)), name="wkv_sample")(
            h3(r), h3(k), h3(v), h3(a), h3(ld), h3(g), state, *hp3)
    return o.reshape(b, D), s_new


def _mem_prompt_kernel(x_ref, wq_ref, mk_ref, mv_ref, wo_ref, g_ref, b_ref, o_ref):
    x = x_ref[...]
    q = _dot(x.astype(BF16), wq_ref[...]).astype(BF16)
    scale = MEM_HD ** -0.5
    outs = []
    for h in range(MEM_HEADS):
        sl = slice(h * MEM_HD, (h + 1) * MEM_HD)
        s = _dot_nt(q[:, sl], mk_ref[0, :, sl]) * scale
        p = jnp.exp(s - jnp.max(s, axis=-1, keepdims=True))
        den = jnp.sum(p, axis=-1, keepdims=True)
        outs.append(_dot((p / den).astype(BF16), mv_ref[0, :, sl]).astype(BF16))
    acc = _dot(jnp.concatenate(outs, axis=-1), wo_ref[...])
    o_ref[...] = _ln(ALPHA * x + acc, g_ref[...], b_ref[...])


def _mem_attn_prompt(x, w_q, mk, mv, w_o, g, b, n_seq, tm):
    t = x.shape[0]
    nb = t // n_seq // tm
    m = mk.shape[1]
    row = lambda n, i: (n * nb + i, 0)
    mem = pl.BlockSpec((1, m, D), lambda n, i: (n, 0, 0))
    return pl.pallas_call(
        _mem_prompt_kernel, grid=(n_seq, nb),
        in_specs=[pl.BlockSpec((tm, D), row), _full((D, D)), mem, mem, _full((D, D)), _full((1, D)), _full((1, D))],
        out_specs=pl.BlockSpec((tm, D), row), out_shape=jax.ShapeDtypeStruct((t, D), F32),
        compiler_params=_cparams(("parallel", "arbitrary")), name="mem_attn")(x, w_q, mk, mv, w_o, g, b)


def _mem_sample_kernel(q_ref, ck_ref, cv_ref, o_ref, *, bs):
    scale = MEM_HD ** -0.5
    for b in range(bs):
        s = jnp.sum(_round_bf16(ck_ref[b]) * (_round_bf16(q_ref[b]) * scale), axis=-1, keepdims=True)
        p = jnp.exp(s - jnp.max(s, axis=0, keepdims=True))
        p = _round_bf16(p / jnp.sum(p, axis=0, keepdims=True))
        o_ref[b] = jnp.sum(p * _round_bf16(cv_ref[b]), axis=0, keepdims=True)


def _mem_attn_sample(q, cache_k, cache_v, layer, bs):
    _, b, m, nh, hd = cache_k.shape
    qb = pl.BlockSpec((bs, 1, nh, hd), lambda i: (i, 0, 0, 0))
    cb = pl.BlockSpec((None, bs, m, nh, hd), lambda i: (layer, i, 0, 0, 0))
    out = pl.pallas_call(
        functools.partial(_mem_sample_kernel, bs=bs), grid=(b // bs,), in_specs=[qb, cb, cb], out_specs=qb,
        out_shape=jax.ShapeDtypeStruct((b, 1, nh, hd), F32),
        compiler_params=_cparams(("parallel",)), name="mem_sample")(q.reshape(b, 1, nh, hd), cache_k, cache_v)
    return out.reshape(b, D)


_PAIRS = ((0, 1), (0, 2), (0, 3), (1, 2), (1, 3), (2, 3))


def _router_kernel(x_ref, rw_ref, rb_ref, bucket_ref, rank_ref, cnt_ref, base_ref, *, tm):
    i = pl.program_id(0)

    @pl.when(i == 0)
    def _():
        base_ref[...] = jnp.zeros_like(base_ref)

    logits = _dot_nt(rw_ref[...], x_ref[...].astype(BF16))
    e = jnp.exp(logits - jnp.max(logits, axis=0, keepdims=True))
    sel = e / jnp.sum(e, axis=0, keepdims=True) + rb_ref[...]
    s = [sel[j:j + 1, :] for j in range(N_EXPERTS)]
    neg = jnp.float32(-jnp.inf)

    best = jnp.zeros((1, tm), jnp.int32)
    best_score = None
    for gi in range(N_GROUPS):
        s0, s1, s2, s3 = s[4 * gi:4 * gi + 4]
        hi01, lo01, hi23, lo23 = jnp.maximum(s0, s1), jnp.minimum(s0, s1), jnp.maximum(s2, s3), jnp.minimum(s2, s3)
        score = jnp.maximum(hi01, hi23) + jnp.maximum(jnp.minimum(hi01, hi23), jnp.maximum(lo01, lo23))
        if gi == 0:
            best_score = score
        else:
            take = score > best_score
            best = jnp.where(take, gi, best)
            best_score = jnp.where(take, score, best_score)
    vals = []
    for j in range(EXPERTS_PER_GROUP):
        vj = s[j]
        for gi in range(1, N_GROUPS):
            vj = jnp.where(best == gi, s[4 * gi + j], vj)
        vals.append(vj)

    def argmax4(v):
        idx, mx = jnp.zeros((1, tm), jnp.int32), v[0]
        for j in range(1, EXPERTS_PER_GROUP):
            take = v[j] > mx
            idx = jnp.where(take, j, idx)
            mx = jnp.where(take, v[j], mx)
        return idx

    i1 = argmax4(vals)
    i2 = argmax4([jnp.where(i1 == j, neg, vals[j]) for j in range(EXPERTS_PER_GROUP)])
    lo, hi = jnp.minimum(i1, i2), jnp.maximum(i1, i2)
    pair = jnp.zeros((1, tm), jnp.int32)
    for pi, (pa, pb) in enumerate(_PAIRS):
        pair = jnp.where(jnp.logical_and(lo == pa, hi == pb), pi, pair)
    bucket = best * len(_PAIRS) + pair
    bucket_ref[0] = bucket

    onehot = (lax.broadcasted_iota(jnp.int32, (BUCKET_ROWS, tm), 0) == bucket).astype(F32)
    upper = (lax.broadcasted_iota(jnp.int32, (tm, tm), 0) <= lax.broadcasted_iota(jnp.int32, (tm, tm), 1)).astype(BF16)
    cum = _dot(onehot.astype(BF16), upper)
    base = base_ref[...]
    rank = jnp.sum(onehot * (cum + base), axis=0, keepdims=True) - 1.0
    rank_ref[0] = rank.astype(jnp.int32)
    base = base + jnp.sum(onehot, axis=1, keepdims=True)
    base_ref[...] = base
    cnt_ref[...] = jnp.broadcast_to(base, cnt_ref.shape)


def _router(x, rw_t, rb, tm):
    t = x.shape[0]
    nb = t // tm
    ib = pl.BlockSpec((1, 1, tm), lambda i: (i, 0, 0))
    bucket, rank, cnt = pl.pallas_call(
        functools.partial(_router_kernel, tm=tm), grid=(nb,),
        in_specs=[pl.BlockSpec((tm, D), lambda i: (i, 0)), _full(rw_t.shape), _full(rb.shape)],
        out_specs=[ib, ib, _full((BUCKET_ROWS, LANES))],
        out_shape=[jax.ShapeDtypeStruct((nb, 1, tm), jnp.int32), jax.ShapeDtypeStruct((nb, 1, tm), jnp.int32),
                   jax.ShapeDtypeStruct((BUCKET_ROWS, LANES), F32)],
        scratch_shapes=[pltpu.VMEM((BUCKET_ROWS, 1), F32)],
        compiler_params=_cparams(("arbitrary",)), name="router")(x, rw_t, rb)
    return bucket.reshape(t), rank.reshape(t), cnt[:N_BUCKETS, 0].astype(jnp.int32)


def _row_copies(idx_ref, base, src_hbm, dst, sem, n, wait):
    def body(r, carry):
        cp = pltpu.make_async_copy(src_hbm.at[pl.ds(idx_ref[base + r], 1)], dst.at[pl.ds(r, 1)], sem)
        if wait:
            cp.wait()
        else:
            cp.start()
        return carry

    lax.fori_loop(0, n, body, 0, unroll=8)


def _ffn_kernel(src_ref, lo_ref, hi_ref, used_ref, x_hbm, rw_ref, g0_ref, u0_ref, d0_ref, g1_ref, u1_ref, d1_ref,
                o_ref, xbuf, sem, *, blk):
    i = pl.program_id(0)
    used = used_ref[0]
    slot = i % 2

    @pl.when(jnp.logical_and(i == 0, used > 0))
    def _():
        _row_copies(src_ref, 0, x_hbm, xbuf.at[0], sem.at[0], blk, False)

    @pl.when(i + 1 < used)
    def _():
        _row_copies(src_ref, (i + 1) * blk, x_hbm, xbuf.at[1 - slot], sem.at[1 - slot], blk, False)

    @pl.when(i < used)
    def _():
        _row_copies(src_ref, i * blk, x_hbm, xbuf.at[slot], sem.at[slot], blk, True)
        xb = xbuf[slot].astype(BF16)
        logits = _dot(xb, rw_ref[...])
        lane = lax.broadcasted_iota(jnp.int32, logits.shape, 1)
        l_lo = jnp.sum(jnp.where(lane == lo_ref[i], logits, 0.0), axis=-1, keepdims=True)
        l_hi = jnp.sum(jnp.where(lane == hi_ref[i], logits, 0.0), axis=-1, keepdims=True)
        w_lo = _sigmoid(l_lo - l_hi)

        def expert(g_ref, u_ref, d_ref):
            gate = _dot(xb, g_ref[0])
            act = gate * _sigmoid(gate) * _dot(xb, u_ref[0])
            return _dot(act.astype(BF16), d_ref[0])

        y_lo = expert(g0_ref, u0_ref, d0_ref)
        y_hi = expert(g1_ref, u1_ref, d1_ref)
        o_ref[...] = w_lo * y_lo + (1.0 - w_lo) * y_hi

    @pl.when(i >= used)
    def _():
        o_ref[...] = jnp.zeros_like(o_ref)


def _ffn(x, src, blk_lo, blk_hi, n_used, rw, w_gate, w_up, w_down, layer, blk):
    rows = src.shape[0]
    nblk = rows // blk
    wg = lambda sel: pl.BlockSpec((None, 1, D, EXPERT_FF),
                                  lambda i, s, lo, hi, used: (layer, (lo, hi)[sel][i], 0, 0))
    wd = lambda sel: pl.BlockSpec((None, 1, EXPERT_FF, D),
                                  lambda i, s, lo, hi, used: (layer, (lo, hi)[sel][i], 0, 0))
    return pl.pallas_call(
        functools.partial(_ffn_kernel, blk=blk),
        grid_spec=pltpu.PrefetchScalarGridSpec(
            num_scalar_prefetch=4, grid=(nblk,),
            in_specs=[pl.BlockSpec(memory_space=pl.ANY), pl.BlockSpec(rw.shape, lambda i, s, lo, hi, used: (0, 0)),
                      wg(0), wg(0), wd(0), wg(1), wg(1), wd(1)],
            out_specs=pl.BlockSpec((blk, D), lambda i, s, lo, hi, used: (i, 0)),
            scratch_shapes=[pltpu.VMEM((2, blk, D), F32), pltpu.SemaphoreType.DMA((2,))]),
        out_shape=jax.ShapeDtypeStruct((rows, D), F32),
        compiler_params=_cparams(("arbitrary",)), name="moe_ffn")(
            src, blk_lo, blk_hi, n_used, x, rw, w_gate, w_up, w_down, w_gate, w_up, w_down)


def _combine_ln_kernel(dest_ref, x_ref, y_hbm, g_ref, b_ref, o_ref, ybuf, sem, *, tm):
    i = pl.program_id(0)
    slot = i % 2

    @pl.when(i == 0)
    def _():
        _row_copies(dest_ref, 0, y_hbm, ybuf.at[0], sem.at[0], tm, False)

    @pl.when(i + 1 < pl.num_programs(0))
    def _():
        _row_copies(dest_ref, (i + 1) * tm, y_hbm, ybuf.at[1 - slot], sem.at[1 - slot], tm, False)

    _row_copies(dest_ref, i * tm, y_hbm, ybuf.at[slot], sem.at[slot], tm, True)
    o_ref[...] = _ln(ALPHA * x_ref[...] + ybuf[slot], g_ref[...], b_ref[...])


def _combine_ln(x, y_rows, dest, g, b, tm):
    t = x.shape[0]
    rowb = pl.BlockSpec((tm, D), lambda i, d: (i, 0))
    vec = pl.BlockSpec((1, D), lambda i, d: (0, 0))
    return pl.pallas_call(
        functools.partial(_combine_ln_kernel, tm=tm),
        grid_spec=pltpu.PrefetchScalarGridSpec(
            num_scalar_prefetch=1, grid=(t // tm,),
            in_specs=[rowb, pl.BlockSpec(memory_space=pl.ANY), vec, vec], out_specs=rowb,
            scratch_shapes=[pltpu.VMEM((2, tm, D), F32), pltpu.SemaphoreType.DMA((2,))]),
        out_shape=jax.ShapeDtypeStruct((t, D), F32),
        compiler_params=_cparams(("arbitrary",)), name="moe_combine_ln")(dest, x, y_rows, g, b)


def _moe_ln(x, rw_t, rb, rw_pad, w_gate, w_up, w_down, layer, g, b, tm_router, blk, tm_comb):
    t = x.shape[0]
    bucket, rank, counts = _router(x, rw_t, rb, tm_router)
    padded = (counts + blk - 1) // blk * blk
    ends = jnp.cumsum(padded)
    dest = ((ends - padded)[bucket] + rank).astype(jnp.int32)
    nblk = t // blk + N_BUCKETS
    src = jnp.zeros((nblk * blk,), jnp.int32).at[dest].set(jnp.arange(t, dtype=jnp.int32))
    blk_bucket = jnp.minimum(jnp.searchsorted(ends, jnp.arange(nblk) * blk, side='right'), N_BUCKETS - 1)
    pair_lo = jnp.array([p[0] for p in _PAIRS], jnp.int32)
    pair_hi = jnp.array([p[1] for p in _PAIRS], jnp.int32)
    grp, pr = blk_bucket // len(_PAIRS), blk_bucket % len(_PAIRS)
    blk_lo = (grp * EXPERTS_PER_GROUP + pair_lo[pr]).astype(jnp.int32)
    blk_hi = (grp * EXPERTS_PER_GROUP + pair_hi[pr]).astype(jnp.int32)
    n_used = (ends[-1:] // blk).astype(jnp.int32)
    y_rows = _ffn(x, src, blk_lo, blk_hi, n_used, rw_pad, w_gate, w_up, w_down, layer, blk)
    return _combine_ln(x, y_rows, dest, g, b, tm_comb)


def _moe_dense_kernel(x_ref, lo_ref, hi_ref, rw_ref, wg_ref, wu_ref, wd_ref, g_ref, b_ref, o_ref, acc_ref):
    e = pl.program_id(0)

    @pl.when(e == 0)
    def _():
        acc_ref[...] = jnp.zeros_like(acc_ref)

    x = x_ref[...]
    xb = x.astype(BF16)
    logits = _dot(xb, rw_ref[...])
    lane = lax.broadcasted_iota(jnp.int32, logits.shape, 1)
    lo, hi = lo_ref[...], hi_ref[...]
    l_lo = jnp.sum(jnp.where(lane == lo, logits, 0.0), axis=-1, keepdims=True)
    l_hi = jnp.sum(jnp.where(lane == hi, logits, 0.0), axis=-1, keepdims=True)
    w_lo = _sigmoid(l_lo - l_hi)
    coef = jnp.where(lo == e, w_lo, 0.0) + jnp.where(hi == e, 1.0 - w_lo, 0.0)
    gate = _dot(xb, wg_ref[0])
    act = gate * _sigmoid(gate) * _dot(xb, wu_ref[0])
    acc_ref[...] += coef * _dot(act.astype(BF16), wd_ref[0])

    @pl.when(e == pl.num_programs(0) - 1)
    def _():
        o_ref[...] = _ln(ALPHA * x + acc_ref[...], g_ref[...], b_ref[...])


def _moe_ln_dense(x, rw_t, rb, rw_pad, w_gate, w_up, w_down, layer, g, b):
    t = x.shape[0]
    bucket, _, _ = _router(x, rw_t, rb, t)
    pair_lo = jnp.array([p[0] for p in _PAIRS], jnp.int32)
    pair_hi = jnp.array([p[1] for p in _PAIRS], jnp.int32)
    grp, pr = bucket // len(_PAIRS), bucket % len(_PAIRS)
    lo = (grp * EXPERTS_PER_GROUP + pair_lo[pr]).astype(jnp.int32).reshape(t, 1)
    hi = (grp * EXPERTS_PER_GROUP + pair_hi[pr]).astype(jnp.int32).reshape(t, 1)
    wg = pl.BlockSpec((None, 1, D, EXPERT_FF), lambda e: (layer, e, 0, 0))
    wd = pl.BlockSpec((None, 1, EXPERT_FF, D), lambda e: (layer, e, 0, 0))
    return pl.pallas_call(
        _moe_dense_kernel, grid=(N_EXPERTS,),
        in_specs=[_full((t, D)), _full((t, 1)), _full((t, 1)), _full(rw_pad.shape), wg, wg, wd,
                  _full((1, D)), _full((1, D))],
        out_specs=_full((t, D)), out_shape=jax.ShapeDtypeStruct((t, D), F32),
        scratch_shapes=[pltpu.VMEM((t, D), F32)],
        compiler_params=_cparams(("arbitrary",)), name="moe_dense")(x, lo, hi, rw_pad, w_gate, w_up, w_down, g, b)


def kernel(x_prompt, x_sample, cache_swa_k, cache_swa_v, state_lru_conv, state_lru_h, state_rwkv_shift, state_rwkv_wkv, cache_mem_k, cache_mem_v, mem_prompt, swa_w_qkv, swa_sinks, swa_w_o, lru_w_in, lru_b_in, lru_conv_w, lru_conv_b, lru_w_a, lru_b_a, lru_w_i, lru_b_i, lru_lambda, lru_w_o, rwkv_mu, rwkv_w_r, rwkv_w_k, rwkv_w_v, rwkv_w0, rwkv_w1, rwkv_w2, rwkv_a0, rwkv_a1, rwkv_a2, rwkv_g1, rwkv_g2, rwkv_k_k, rwkv_k_a, rwkv_r_k, rwkv_gn_g, rwkv_gn_b, rwkv_w_o, mem_w_q, mem_w_kv, mem_w_o, ln_g, ln_b, router_w, router_b, moe_w_gate, moe_w_up, moe_w_down):
    n_p, seq, _ = x_prompt.shape
    n_s, dec_seq, _ = x_sample.shape
    assert dec_seq == 1
    past_len = 8192
    xp = x_prompt.reshape(n_p * seq, D)
    xs = x_sample.reshape(n_s, D)
    row = lambda v: v.reshape(1, -1)
    bf = lambda w: w.astype(BF16)

    rw_t = bf(router_w.T)
    rb = router_b.reshape(N_EXPERTS, 1)
    rw_pad = bf(jnp.pad(router_w, ((0, 0), (0, LANES - N_EXPERTS))))
    wg, wu, wd = bf(moe_w_gate), bf(moe_w_up), bf(moe_w_down)
    mem_p = mem_prompt.reshape(n_p * mem_prompt.shape[1], D)
    m_len = mem_prompt.shape[1]

    swa_k_p, swa_v_p, swa_k_s, swa_v_s = [], [], [], []
    lru_c_p, lru_h_p, lru_c_s, lru_h_s = [], [], [], []
    rw_x_p, rw_s_p, rw_x_s, rw_s_s = [], [], [], []
    mem_k_p, mem_v_p = [], []

    for layer in range(DEPTH):
        kind, i = layer % N_MIXERS, layer // N_MIXERS
        g0, b0 = row(ln_g[layer, 0]), row(ln_b[layer, 0])
        if kind == 0:
            w_qkv, w_o = bf(swa_w_qkv[i]), bf(swa_w_o[i])
            keep = min(WINDOW, seq)
            q, k, v, kv_last = _swa_qkv(xp, w_qkv, jnp.arange(seq), n_p, 512, keep, BF16)
            o = _swa_attn_prompt(q, k, v, swa_sinks[i], n_p, 2)
            swa_k_p.append(kv_last[:, :, :KV_WIDTH].reshape(n_p, keep, SWA_KV_HEADS, HEAD_DIM))
            swa_v_p.append(kv_last[:, :, KV_WIDTH:].reshape(n_p, keep, SWA_KV_HEADS, HEAD_DIM))
            xp = _proj_ln(o, w_o, xp, g0, b0, 512)

            qs, _, _, kv_new = _swa_qkv(xs, w_qkv, jnp.full((n_s,), past_len), 1, n_s, n_s, F32)
            kn, vn = kv_new[0, :, :KV_WIDTH], kv_new[0, :, KV_WIDTH:]
            os_ = _swa_attn_sample(qs, kn, vn, cache_swa_k[i], cache_swa_v[i], swa_sinks[i], 8)
            wb = cache_swa_k.shape[2]
            k_all = jnp.concatenate([cache_swa_k[i], kn.reshape(n_s, 1, SWA_KV_HEADS, HEAD_DIM)], axis=1)
            v_all = jnp.concatenate([cache_swa_v[i], vn.reshape(n_s, 1, SWA_KV_HEADS, HEAD_DIM)], axis=1)
            swa_k_s.append(k_all[:, -wb:])
            swa_v_s.append(v_all[:, -wb:])
            xs = _proj_ln(os_, w_o, xs, g0, b0, n_s)
        elif kind == 1:
            wts = _lru_weights(lru_w_in[i], lru_b_in[i], lru_conv_w[i], lru_conv_b[i], lru_w_a[i], lru_b_a[i],
                               lru_w_i[i], lru_b_i[i], lru_lambda[i], lru_w_o[i])
            xp, conv_last, h_last = _lru_prompt(xp, wts, g0, b0, n_p, 256)
            lru_c_p.append(conv_last[:, SUBLANES - (CONV_W - 1):])
            lru_h_p.append(h_last[:, SUBLANES - 1])
            xs, xb_s, h_s = _lru_sample(xs, state_lru_conv[i], state_lru_h[i], wts, g0, b0)
            lru_c_s.append(jnp.concatenate([state_lru_conv[i][:, 1:], xb_s[:, None]], axis=1))
            lru_h_s.append(h_s)
        else:
            wts = (rwkv_mu[i], bf(rwkv_w_r[i]), bf(rwkv_w_k[i]), bf(rwkv_w_v[i]), row(rwkv_w0[i]), bf(rwkv_w1[i]),
                   bf(rwkv_w2[i]), row(rwkv_a0[i]), bf(rwkv_a1[i]), bf(rwkv_a2[i]), bf(rwkv_g1[i]), bf(rwkv_g2[i]))
            hp = (row(rwkv_k_k[i]), row(rwkv_k_a[i]), row(rwkv_r_k[i]), row(rwkv_gn_g[i]), row(rwkv_gn_b[i]))
            w_o = bf(rwkv_w_o[i])
            rw_x_p.append(xp.reshape(n_p, seq, D)[:, -1])
            rw_x_s.append(xs)
            r, k, v, a, ld, g = _rwkv_pre(xp, jnp.zeros((n_p, SUBLANES, D), F32), wts, n_p, 256, True, BF16)
            o, st = _wkv_prompt(r, k, v, a, ld, g, hp, n_p)
            hd = RWKV_HD
            st = jnp.stack([st[:, :, :hd, :hd], st[:, :, hd:, hd:]], axis=2).reshape(n_p, RWKV_HEADS, hd, hd)
            rw_s_p.append(jnp.swapaxes(st, -1, -2))
            xp = _proj_ln(o, w_o, xp, g0, b0, 512)

            r, k, v, a, ld, g = _rwkv_pre(xs, state_rwkv_shift[i], wts, 1, n_s, False, F32)
            os_, s_new = _wkv_sample(r, k, v, a, ld, g, state_rwkv_wkv[i], hp, 8)
            rw_s_s.append(s_new)
            xs = _proj_ln(os_, w_o, xs, g0, b0, n_s)

        g1, b1 = row(ln_g[layer, 1]), row(ln_b[layer, 1])
        w_q, w_o = bf(mem_w_q[layer]), bf(mem_w_o[layer])
        mkv = _matmul(mem_p, bf(mem_w_kv[layer]), 512)
        mk, mv = mkv[:, :D], mkv[:, D:]
        mem_k_p.append(mk.reshape(n_p, m_len, MEM_HEADS, MEM_HD))
        mem_v_p.append(mv.reshape(n_p, m_len, MEM_HEADS, MEM_HD))
        xp = _mem_attn_prompt(xp, w_q, bf(mk).reshape(n_p, m_len, D), bf(mv).reshape(n_p, m_len, D), w_o, g1, b1,
                              n_p, 512)
        qs = _matmul(xs, w_q, n_s)
        os_ = _mem_attn_sample(qs, cache_mem_k, cache_mem_v, layer, 4)
        xs = _proj_ln(os_, w_o, xs, g1, b1, n_s)

        g2, b2 = row(ln_g[layer, 2]), row(ln_b[layer, 2])
        xp = _moe_ln(xp, rw_t, rb, rw_pad, wg, wu, wd, layer, g2, b2, 512, 256, 256)
        xs = _moe_ln_dense(xs, rw_t, rb, rw_pad, wg, wu, wd, layer, g2, b2)

    return (xp.reshape(n_p, seq, D), xs.reshape(n_s, 1, D),
            jnp.stack(swa_k_p), jnp.stack(swa_v_p), jnp.stack(lru_c_p), jnp.stack(lru_h_p),
            jnp.stack(rw_x_p), jnp.stack(rw_s_p), jnp.stack(mem_k_p), jnp.stack(mem_v_p),
            jnp.stack(swa_k_s), jnp.stack(swa_v_s), jnp.stack(lru_c_s), jnp.stack(lru_h_s),
            jnp.stack(rw_x_s), jnp.stack(rw_s_s))
```

```python
import functools

import jax
import jax.numpy as jnp
from jax import lax
from jax.experimental import pallas as pl
from jax.experimental.pallas import tpu as pltpu

F32 = jnp.float32
BF16 = jnp.bfloat16

D = 1024
DEPTH = 4
N_MIXERS = 3
HEAD_DIM = 64
SWA_HEADS = D // HEAD_DIM
SWA_KV_HEADS = 4
SWA_GROUP = SWA_HEADS // SWA_KV_HEADS
Q_WIDTH = SWA_HEADS * HEAD_DIM
KV_WIDTH = SWA_KV_HEADS * HEAD_DIM
WINDOW = 128
ROT_DIM = HEAD_DIM // 4
ROPE_THETA = 500000.0
LRU_BLOCKS = 16
CONV_W = 4
LRU_C = 8.0
RWKV_HEADS = 16
RWKV_HD = 64
RWKV_GN_EPS = 64e-5
MEM_HEADS = 4
MEM_HD = D // MEM_HEADS
N_EXPERTS = 16
N_GROUPS = 4
EXPERTS_PER_GROUP = 4
EXPERT_FF = 512
LN_EPS = 1e-5
ALPHA = (2.0 * DEPTH) ** 0.25
NEG_INF = -1e30

LANES = 128
SUBLANES = 8
VMEM_LIMIT = 56 * 1024 * 1024
WKV_CHUNK = 64
N_BUCKETS = N_GROUPS * 6
BUCKET_ROWS = 32


def _cparams(sem):
    return pltpu.CompilerParams(dimension_semantics=sem, vmem_limit_bytes=VMEM_LIMIT)


def _dot(a, b):
    return jnp.dot(a, b, preferred_element_type=F32)


def _dot_nt(a, b):
    return lax.dot_general(a, b, (((1,), (1,)), ((), ())), preferred_element_type=F32)


def _dot_tn(a, b):
    return lax.dot_general(a, b, (((0,), (0,)), ((), ())), preferred_element_type=F32)


def _ln(z, g, b):
    mu = jnp.mean(z, axis=-1, keepdims=True)
    zc = z - mu
    var = jnp.mean(zc * zc, axis=-1, keepdims=True)
    return zc * lax.rsqrt(var + LN_EPS) * g + b


def _softplus(z):
    return jnp.maximum(z, 0.0) + jnp.log1p(jnp.exp(-jnp.abs(z)))


def _sigmoid(z):
    return 1.0 / (1.0 + jnp.exp(-z))


def _round_bf16(x):
    return x.astype(BF16).astype(F32)


def _full(shape):
    nd = len(shape)
    return pl.BlockSpec(shape, lambda *_: (0,) * nd)


def _mm_kernel(a_ref, w_ref, o_ref):
    o_ref[...] = _dot(a_ref[...].astype(BF16), w_ref[...]).astype(o_ref.dtype)


def _matmul(a, w, tm, out_dtype=F32):
    t, k = a.shape
    n = w.shape[1]
    return pl.pallas_call(
        _mm_kernel, grid=(t // tm,),
        in_specs=[pl.BlockSpec((tm, k), lambda i: (i, 0)), _full((k, n))],
        out_specs=pl.BlockSpec((tm, n), lambda i: (i, 0)),
        out_shape=jax.ShapeDtypeStruct((t, n), out_dtype),
        compiler_params=_cparams(("parallel",)), name="matmul")(a, w)


def _proj_ln_kernel(a_ref, w_ref, x_ref, g_ref, b_ref, o_ref):
    acc = _dot(a_ref[...].astype(BF16), w_ref[...])
    o_ref[...] = _ln(ALPHA * x_ref[...] + acc, g_ref[...], b_ref[...])


def _proj_ln(a, w, x, g, b, tm):
    t, k = a.shape
    return pl.pallas_call(
        _proj_ln_kernel, grid=(t // tm,),
        in_specs=[pl.BlockSpec((tm, k), lambda i: (i, 0)), _full((k, D)),
                  pl.BlockSpec((tm, D), lambda i: (i, 0)), _full((1, D)), _full((1, D))],
        out_specs=pl.BlockSpec((tm, D), lambda i: (i, 0)),
        out_shape=jax.ShapeDtypeStruct((t, D), F32),
        compiler_params=_cparams(("parallel",)), name="proj_ln")(a, w, x, g, b)


def _rope_tables(pos):
    half = ROT_DIM // 2
    inv_freq = ROPE_THETA ** (-jnp.arange(half, dtype=F32) / half)
    ang = pos.astype(F32)[:, None] * inv_freq
    cos, sin = jnp.cos(ang), jnp.sin(ang)
    one = jnp.ones((pos.shape[0], HEAD_DIM - ROT_DIM), F32)
    zero = jnp.zeros((pos.shape[0], HEAD_DIM - ROT_DIM), F32)
    zh = jnp.zeros_like(sin)
    c = jnp.concatenate([cos, cos, one], axis=1)
    s1 = jnp.concatenate([-sin, zh, zero], axis=1)
    s2 = jnp.concatenate([zh, sin, zero], axis=1)
    rep = LANES // HEAD_DIM
    return jnp.tile(c, (1, rep)), jnp.tile(s1, (1, rep)), jnp.tile(s2, (1, rep))


def _swa_qkv_kernel(x_ref, w_ref, c_ref, s1_ref, s2_ref, q_ref, k_ref, v_ref, kv_ref, *, tm, keep):
    acc = _dot(x_ref[...].astype(BF16), w_ref[...])
    c, s1, s2 = c_ref[...], s1_ref[...], s2_ref[...]
    half = ROT_DIM // 2
    n_q = Q_WIDTH // LANES
    n_k = KV_WIDTH // LANES
    for cg in range(n_q + n_k):
        xg = acc[:, cg * LANES:(cg + 1) * LANES]
        rot = xg * c + pltpu.roll(xg, LANES - half, 1) * s1 + pltpu.roll(xg, half, 1) * s2
        if cg < n_q:
            q_ref[:, cg * LANES:(cg + 1) * LANES] = rot.astype(q_ref.dtype)
        else:
            ck = cg - n_q
            k_ref[:, ck * LANES:(ck + 1) * LANES] = rot.astype(k_ref.dtype)
            kv_ref[0, :, ck * LANES:(ck + 1) * LANES] = rot[tm - keep:, :]
    v = acc[:, Q_WIDTH + KV_WIDTH:]
    v_ref[...] = v.astype(v_ref.dtype)
    kv_ref[0, :, KV_WIDTH:] = v[tm - keep:, :]


def _swa_qkv(x, w_qkv, pos, n_seq, tm, keep, qdtype):
    t = x.shape[0]
    s = t // n_seq
    nb = s // tm
    c, s1, s2 = _rope_tables(pos)
    row = lambda n, i: (n * nb + i, 0)
    tab = pl.BlockSpec((tm, LANES), lambda n, i: (i, 0))
    kern = functools.partial(_swa_qkv_kernel, tm=tm, keep=keep)
    return pl.pallas_call(
        kern, grid=(n_seq, nb),
        in_specs=[pl.BlockSpec((tm, D), row), _full((D, Q_WIDTH + 2 * KV_WIDTH)), tab, tab, tab],
        out_specs=[pl.BlockSpec((tm, Q_WIDTH), row), pl.BlockSpec((tm, KV_WIDTH), row),
                   pl.BlockSpec((tm, KV_WIDTH), row),
                   pl.BlockSpec((1, keep, 2 * KV_WIDTH), lambda n, i: (n, 0, 0))],
        out_shape=[jax.ShapeDtypeStruct((t, Q_WIDTH), qdtype), jax.ShapeDtypeStruct((t, KV_WIDTH), qdtype),
                   jax.ShapeDtypeStruct((t, KV_WIDTH), qdtype),
                   jax.ShapeDtypeStruct((n_seq, keep, 2 * KV_WIDTH), F32)],
        compiler_params=_cparams(("parallel", "arbitrary")), name="swa_qkv")(x, w_qkv, c, s1, s2)


def _swa_attn_kernel(sink_ref, q_ref, kp_ref, kc_ref, vp_ref, vc_ref, o_ref, *, nq):
    j = pl.program_id(1)
    w, grp = WINDOW, SWA_GROUP
    r = lax.broadcasted_iota(jnp.int32, (grp * w, 2 * w), 0) % w
    c = lax.broadcasted_iota(jnp.int32, (grp * w, 2 * w), 1)
    in_prev = jnp.logical_and(c < w, c > r)
    in_cur = jnp.logical_and(c >= w, (c - w) <= r)
    ok_inner = jnp.logical_or(in_prev, in_cur)
    ok_first = jnp.logical_or(jnp.logical_and(in_prev, j > 0), in_cur)
    scale = HEAD_DIM ** -0.5
    combos = [(u, h) for u in range(nq) for h in range(SWA_KV_HEADS)]
    kcat, vcat, q4, sink, ok = [], [], [], [], []
    for u, h in combos:
        sl = slice(h * HEAD_DIM, (h + 1) * HEAD_DIM)
        rows = slice(u * w, (u + 1) * w)
        before = slice((u - 1) * w, u * w)
        k_prev = kp_ref[:, sl] if u == 0 else kc_ref[before, sl]
        v_prev = vp_ref[:, sl] if u == 0 else vc_ref[before, sl]
        kcat.append(jnp.concatenate([k_prev, kc_ref[rows, sl]], axis=0))
        vcat.append(jnp.concatenate([v_prev, vc_ref[rows, sl]], axis=0))
        heads = [h * grp + g for g in range(grp)]
        q4.append(jnp.concatenate([q_ref[rows, hq * HEAD_DIM:(hq + 1) * HEAD_DIM] for hq in heads], axis=0))
        sink.append(jnp.concatenate([jnp.full((w, 1), sink_ref[hq], F32) for hq in heads], axis=0))
        ok.append(ok_first if u == 0 else ok_inner)
    n = range(len(combos))
    s = [jnp.where(ok[i], _dot_nt(q4[i], kcat[i]) * scale, NEG_INF) for i in n]
    m = [jnp.maximum(jnp.max(s[i], axis=-1, keepdims=True), sink[i]) for i in n]
    p = [jnp.exp(s[i] - m[i]) for i in n]
    den = [jnp.sum(p[i], axis=-1, keepdims=True) + jnp.exp(sink[i] - m[i]) for i in n]
    o = [_dot((p[i] / den[i]).astype(BF16), vcat[i]) for i in n]
    for i, (u, h) in enumerate(combos):
        for g in range(grp):
            hq = h * grp + g
            o_ref[u * w:(u + 1) * w, hq * HEAD_DIM:(hq + 1) * HEAD_DIM] = o[i][g * w:(g + 1) * w].astype(o_ref.dtype)


def _swa_attn_prompt(q, k, v, sinks, n_seq, nq):
    t = q.shape[0]
    nb = t // n_seq // WINDOW
    ns = nb // nq
    cur = lambda n, j: (n * ns + j, 0)
    prev = lambda n, j: (n * nb + jnp.maximum(j * nq - 1, 0), 0)
    return pl.pallas_call(
        functools.partial(_swa_attn_kernel, nq=nq), grid=(n_seq, ns),
        in_specs=[pl.BlockSpec(memory_space=pltpu.SMEM), pl.BlockSpec((nq * WINDOW, Q_WIDTH), cur),
                  pl.BlockSpec((WINDOW, KV_WIDTH), prev), pl.BlockSpec((nq * WINDOW, KV_WIDTH), cur),
                  pl.BlockSpec((WINDOW, KV_WIDTH), prev), pl.BlockSpec((nq * WINDOW, KV_WIDTH), cur)],
        out_specs=pl.BlockSpec((nq * WINDOW, Q_WIDTH), cur),
        out_shape=jax.ShapeDtypeStruct((t, Q_WIDTH), BF16),
        compiler_params=_cparams(("parallel", "arbitrary")), name="swa_attn")(sinks, q, k, k, v, v)


def _swa_sample_kernel(sink_ref, q_ref, kn_ref, vn_ref, ck_ref, cv_ref, o_ref, *, wb):
    kidx = lax.broadcasted_iota(jnp.int32, (1, wb, 1), 1)
    valid = (wb - kidx) < WINDOW
    scale = HEAD_DIM ** -0.5
    for h in range(SWA_KV_HEADS):
        sl = slice(h * HEAD_DIM, (h + 1) * HEAD_DIM)
        ck, cv = _round_bf16(ck_ref[:, :, sl]), _round_bf16(cv_ref[:, :, sl])
        kn, vn = _round_bf16(kn_ref[:, :, sl]), _round_bf16(vn_ref[:, :, sl])
        for g in range(SWA_GROUP):
            hq = h * SWA_GROUP + g
            qs = slice(hq * HEAD_DIM, (hq + 1) * HEAD_DIM)
            qh = _round_bf16(q_ref[:, :, qs])
            s = jnp.where(valid, jnp.sum(ck * qh, axis=-1, keepdims=True) * scale, NEG_INF)
            sn = jnp.sum(kn * qh, axis=-1, keepdims=True) * scale
            sink = sink_ref[hq]
            m = jnp.maximum(jnp.maximum(jnp.max(s, axis=1, keepdims=True), sn), sink)
            p = jnp.exp(s - m)
            pn = jnp.exp(sn - m)
            den = jnp.sum(p, axis=1, keepdims=True) + pn + jnp.exp(sink - m)
            p, pn = _round_bf16(p / den), _round_bf16(pn / den)
            o_ref[:, :, qs] = jnp.sum(p * cv, axis=1, keepdims=True) + pn * vn


def _swa_attn_sample(q, kn, vn, cache_k, cache_v, sinks, bs):
    b, wb = cache_k.shape[0], cache_k.shape[1]
    blk3 = lambda w: pl.BlockSpec((bs, 1, w), lambda i: (i, 0, 0))
    cblk = pl.BlockSpec((bs, wb, KV_WIDTH), lambda i: (i, 0, 0))
    out = pl.pallas_call(
        functools.partial(_swa_sample_kernel, wb=wb), grid=(b // bs,),
        in_specs=[pl.BlockSpec(memory_space=pltpu.SMEM), blk3(Q_WIDTH), blk3(KV_WIDTH), blk3(KV_WIDTH), cblk, cblk],
        out_specs=blk3(Q_WIDTH), out_shape=jax.ShapeDtypeStruct((b, 1, Q_WIDTH), F32),
        compiler_params=_cparams(("parallel",)), name="swa_sample")(
            sinks, q.reshape(b, 1, Q_WIDTH), kn.reshape(b, 1, KV_WIDTH), vn.reshape(b, 1, KV_WIDTH),
            cache_k.reshape(b, wb, KV_WIDTH), cache_v.reshape(b, wb, KV_WIDTH))
    return out.reshape(b, Q_WIDTH)


def _gelu_tanh(x):
    return 0.5 * x * (1.0 + jnp.tanh(0.7978845608028654 * (x + 0.044715 * x * x * x)))


def _lru_gates(xc, wa_ref, ba, wi_ref, bi, lam):
    xcb = xc.astype(BF16)
    gw = wa_ref.shape[1]
    ra, ia = [], []
    for gi in range(wa_ref.shape[0]):
        xs = xcb[:, gi * gw:(gi + 1) * gw]
        ra.append(_dot(xs, wa_ref[gi]))
        ia.append(_dot(xs, wi_ref[gi]))
    r = _sigmoid(jnp.concatenate(ra, axis=-1) + ba)
    ig = _sigmoid(jnp.concatenate(ia, axis=-1) + bi)
    log_a = -LRU_C * r * _softplus(-lam)
    a = jnp.exp(log_a)
    b = jnp.sqrt(-jnp.tanh(log_a) * (a * a + 1.0)) * (ig * xc)
    return a, b


def _shift_rows(ext, s, tm):
    return pltpu.roll(ext, s, 0)[SUBLANES:SUBLANES + tm]


def _lru_prompt_kernel(x_ref, win_ref, bin_ref, cw_ref, cb_ref, wa_ref, ba_ref, wi_ref, bi_ref, lam_ref,
                       wo_ref, g_ref, b_ref, o_ref, conv_ref, hl_ref, cx_ref, ch_ref, *, tm):
    i = pl.program_id(1)

    @pl.when(i == 0)
    def _():
        cx_ref[...] = jnp.zeros_like(cx_ref)
        ch_ref[...] = jnp.zeros_like(ch_ref)

    x = x_ref[...]
    xy = _dot(x.astype(BF16), win_ref[...]) + bin_ref[...]
    xb = xy[:, :D]
    y_gate = _gelu_tanh(xy[:, D:])
    ext = jnp.concatenate([cx_ref[...], xb], axis=0)
    cw = cw_ref[...]
    xc = cb_ref[...] + xb * cw[CONV_W - 1:CONV_W]
    for s in range(1, CONV_W):
        xc = xc + _shift_rows(ext, s, tm) * cw[CONV_W - 1 - s:CONV_W - s]
    cx_ref[...] = xb[tm - SUBLANES:]
    conv_ref[0] = xb[tm - SUBLANES:]

    a, b = _lru_gates(xc, wa_ref, ba_ref[...], wi_ref, bi_ref[...], lam_ref[...])
    sub = lax.broadcasted_iota(jnp.int32, (tm, 1), 0) % SUBLANES
    s = 1
    while s < SUBLANES:
        keep = sub >= s
        a_sh = jnp.where(keep, pltpu.roll(a, s, 0), 1.0)
        b_sh = jnp.where(keep, pltpu.roll(b, s, 0), 0.0)
        b = a * b_sh + b
        a = a * a_sh
        s *= 2
    carry = ch_ref[SUBLANES - 1:SUBLANES, :]
    groups = []
    for gi in range(tm // SUBLANES):
        rows = slice(gi * SUBLANES, (gi + 1) * SUBLANES)
        hg = a[rows] * carry + b[rows]
        groups.append(hg)
        carry = hg[SUBLANES - 1:SUBLANES]
    h = jnp.concatenate(groups, axis=0)
    ch_ref[...] = h[tm - SUBLANES:]
    hl_ref[0] = h[tm - SUBLANES:]
    acc = _dot((h * y_gate).astype(BF16), wo_ref[...])
    o_ref[...] = _ln(ALPHA * x + acc, g_ref[...], b_ref[...])


def _lru_weights(w_in, b_in, conv_w, conv_b, w_a, b_a, w_i, b_i, lam, w_o):
    gsz = 4
    ng = LRU_BLOCKS // gsz
    bw = D // LRU_BLOCKS

    def grouped(w):
        w4 = w.reshape(ng, gsz, bw, bw)
        return jnp.einsum('gaij,ab->gaibj', w4, jnp.eye(gsz, dtype=w.dtype)).reshape(ng, gsz * bw, gsz * bw).astype(BF16)

    row = lambda v: v.reshape(1, -1)
    return (w_in.astype(BF16), row(b_in), conv_w, row(conv_b), grouped(w_a), row(b_a), grouped(w_i), row(b_i),
            row(lam), w_o.astype(BF16))


def _lru_prompt(x, wts, g, b, n_seq, tm):
    t = x.shape[0]
    nb = t // n_seq // tm
    row = lambda n, i: (n * nb + i, 0)
    last = pl.BlockSpec((1, SUBLANES, D), lambda n, i: (n, 0, 0))
    w_in, b_in, cw, cb, wa, ba, wi, bi, lam, wo = wts
    return pl.pallas_call(
        functools.partial(_lru_prompt_kernel, tm=tm), grid=(n_seq, nb),
        in_specs=[pl.BlockSpec((tm, D), row), _full(w_in.shape), _full(b_in.shape), _full(cw.shape), _full(cb.shape),
                  _full(wa.shape), _full(ba.shape), _full(wi.shape), _full(bi.shape), _full(lam.shape),
                  _full(wo.shape), _full((1, D)), _full((1, D))],
        out_specs=[pl.BlockSpec((tm, D), row), last, last],
        out_shape=[jax.ShapeDtypeStruct((t, D), F32), jax.ShapeDtypeStruct((n_seq, SUBLANES, D), F32),
                   jax.ShapeDtypeStruct((n_seq, SUBLANES, D), F32)],
        scratch_shapes=[pltpu.VMEM((SUBLANES, D), F32), pltpu.VMEM((SUBLANES, D), F32)],
        compiler_params=_cparams(("parallel", "arbitrary")), name="lru_prompt")(x, *wts, g, b)


def _lru_sample_kernel(x_ref, c0_ref, c1_ref, c2_ref, h0_ref, win_ref, bin_ref, cw_ref, cb_ref, wa_ref, ba_ref,
                       wi_ref, bi_ref, lam_ref, wo_ref, g_ref, b_ref, o_ref, xb_ref, h_ref):
    x = x_ref[...]
    xy = _dot(x.astype(BF16), win_ref[...]) + bin_ref[...]
    xb = xy[:, :D]
    y_gate = _gelu_tanh(xy[:, D:])
    cw = cw_ref[...]
    xc = (cb_ref[...] + c0_ref[...] * cw[0:1] + c1_ref[...] * cw[1:2] + c2_ref[...] * cw[2:3] + xb * cw[3:4])
    a, b = _lru_gates(xc, wa_ref, ba_ref[...], wi_ref, bi_ref[...], lam_ref[...])
    h = a * h0_ref[...] + b
    xb_ref[...] = xb
    h_ref[...] = h
    acc = _dot((h * y_gate).astype(BF16), wo_ref[...])
    o_ref[...] = _ln(ALPHA * x + acc, g_ref[...], b_ref[...])


def _lru_sample(x, conv_state, h0, wts, g, b):
    t = x.shape[0]
    args = (x, conv_state[:, 0], conv_state[:, 1], conv_state[:, 2], h0, *wts, g, b)
    sd = jax.ShapeDtypeStruct((t, D), F32)
    return pl.pallas_call(
        _lru_sample_kernel, grid=(1,),
        in_specs=[_full(a.shape) for a in args],
        out_specs=[_full((t, D))] * 3, out_shape=[sd, sd, sd],
        compiler_params=_cparams(("arbitrary",)), name="lru_sample")(*args)


def _rwkv_pre_kernel(x_ref, xp_ref, mu_ref, wr_ref, wk_ref, wv_ref, w0_ref, w1_ref, w2_ref, a0_ref, a1_ref, a2_ref,
                     g1_ref, g2_ref, r_ref, k_ref, v_ref, a_ref, ld_ref, g_ref, *scratch, tm, seq):
    x = x_ref[...]
    if seq:
        cx_ref, = scratch
        i = pl.program_id(1)

        @pl.when(i == 0)
        def _():
            cx_ref[...] = xp_ref[0]

        x_prev = _shift_rows(jnp.concatenate([cx_ref[...], x], axis=0), 1, tm)
        cx_ref[...] = x[tm - SUBLANES:]
    else:
        x_prev = xp_ref[...]
    xx = x_prev - x
    mu = mu_ref[...]
    mix = lambda j: (x + xx * mu[j:j + 1]).astype(BF16)
    r_ref[...] = _dot(mix(0), wr_ref[...]).astype(r_ref.dtype)
    wl = _dot(jnp.tanh(_dot(mix(1), w1_ref[...])).astype(BF16), w2_ref[...])
    w = -_softplus(-(w0_ref[...] + wl)) - 0.5
    ld_ref[...] = -jnp.exp(w)
    k_ref[...] = _dot(mix(2), wk_ref[...]).astype(k_ref.dtype)
    v_ref[...] = _dot(mix(3), wv_ref[...]).astype(v_ref.dtype)
    al = _dot(_dot(mix(4), a1_ref[...]).astype(BF16), a2_ref[...])
    a_ref[...] = _sigmoid(a0_ref[...] + al).astype(a_ref.dtype)
    g_ref[...] = _dot(_sigmoid(_dot(mix(5), g1_ref[...])).astype(BF16), g2_ref[...]).astype(g_ref.dtype)


def _rwkv_pre(x, x_prev, wts, n_seq, tm, seq, dtype):
    t = x.shape[0]
    nb = t // n_seq // tm
    row = lambda n, i: (n * nb + i, 0)
    xp_spec = pl.BlockSpec((1, SUBLANES, D), lambda n, i: (n, 0, 0)) if seq else pl.BlockSpec((tm, D), row)
    sd = lambda dt: jax.ShapeDtypeStruct((t, D), dt)
    blk = pl.BlockSpec((tm, D), row)
    return pl.pallas_call(
        functools.partial(_rwkv_pre_kernel, tm=tm, seq=seq), grid=(n_seq, nb),
        in_specs=[blk, xp_spec] + [_full(w.shape) for w in wts],
        out_specs=[blk] * 6,
        out_shape=[sd(dtype), sd(dtype), sd(dtype), sd(dtype), sd(F32), sd(dtype)],
        scratch_shapes=[pltpu.VMEM((SUBLANES, D), F32)] if seq else [],
        compiler_params=_cparams(("parallel", "arbitrary")), name="rwkv_pre")(x, x_prev, *wts)


def _seg_sum(x, first):
    s0 = jnp.sum(jnp.where(first, x, 0.0), axis=-1, keepdims=True)
    s1 = jnp.sum(jnp.where(first, 0.0, x), axis=-1, keepdims=True)
    return jnp.where(first, s0, s1)


def _wkv_kernel(r_ref, k_ref, v_ref, a_ref, ld_ref, g_ref, kk_ref, ka_ref, rk_ref, gg_ref, gb_ref,
                o_ref, s_ref, st_ref):
    c = pl.program_id(1)
    L = WKV_CHUNK
    P2 = 2 * L

    @pl.when(c == 0)
    def _():
        st_ref[...] = jnp.zeros_like(st_ref)

    ld_all = ld_ref[...]
    tri = (lax.broadcasted_iota(jnp.int32, (L, L), 0) >= lax.broadcasted_iota(jnp.int32, (L, L), 1)).astype(BF16)
    hi = ld_all.astype(BF16)
    r1 = ld_all - hi.astype(F32)
    mid = r1.astype(BF16)
    lo = (r1 - mid.astype(F32)).astype(BF16)
    cum_all = _dot(tri, hi) + _dot(tri, mid) + _dot(tri, lo)

    lane = lax.broadcasted_iota(jnp.int32, (1, LANES), 1)
    first = lane < RWKV_HD
    ri = lax.broadcasted_iota(jnp.int32, (P2, P2), 0)
    ci = lax.broadcasted_iota(jnp.int32, (P2, P2), 1)
    same_head = (ri // L) == (ci // L)
    rt, ct = ri % L, ci % L
    strict = jnp.logical_and(same_head, rt > ct)
    incl = jnp.logical_and(same_head, rt >= ct)
    eye = ri == ci

    def stack(xv):
        return jnp.concatenate([jnp.where(first, xv, 0.0), jnp.where(first, 0.0, xv)], axis=0).astype(BF16)

    pairs = range(RWKV_HEADS // 2)
    sls = [slice(p * LANES, (p + 1) * LANES) for p in pairs]
    ws, us, ks, rs, ul, kl, vs, g_l, bonus = ([] for _ in range(9))
    for sl in sls:
        rp, kp, vp, ap = (ref[:, sl].astype(F32) for ref in (r_ref, k_ref, v_ref, a_ref))
        ldp, cum = ld_all[:, sl], cum_all[:, sl]
        kk = kp * kk_ref[:, sl]
        kk = kk / jnp.maximum(jnp.sqrt(_seg_sum(kk * kk, first)), 1e-12)
        kmod = kp * (1.0 + (ap - 1.0) * ka_ref[:, sl])
        bp = kk * ap
        cum_l = cum[L - 1:L, :]
        g_inv = jnp.exp(-cum)
        g_to_end = jnp.exp(cum_l - cum)
        ws.append(stack(kk * jnp.exp(cum - ldp)))
        us.append(stack(bp * g_inv))
        ks.append(stack(kmod * g_inv))
        rs.append(stack(rp * jnp.exp(cum)))
        ul.append(stack(bp * g_to_end))
        kl.append(stack(kmod * g_to_end))
        vs.append(stack(vp))
        g_l.append(jnp.exp(cum_l))
        bonus.append(_seg_sum(rp * kmod * rk_ref[:, sl], first) * vp)

    gram = [_dot_nt(jnp.concatenate([ws[p], rs[p]], axis=0), jnp.concatenate([us[p], ks[p]], axis=0)) for p in pairs]
    n_mat = [jnp.where(strict, gram[p][:P2, :P2], 0.0) for p in pairs]
    m_mat = [jnp.where(strict, gram[p][:P2, P2:], 0.0).astype(BF16) for p in pairs]
    nr_mat = [jnp.where(incl, gram[p][P2:, :P2], 0.0).astype(BF16) for p in pairs]
    mr_mat = [jnp.where(incl, gram[p][P2:, P2:], 0.0).astype(BF16) for p in pairs]

    def level_mask(sz):
        sub = jnp.logical_and((rt // sz) % 2 == 1, (ct // sz) % 2 == 0)
        return jnp.logical_and(jnp.logical_and(sub, (rt // (2 * sz)) == (ct // (2 * sz))), same_head)

    x_inv = [jnp.where(eye, 1.0, 0.0) - jnp.where(level_mask(1), n_mat[p], 0.0) for p in pairs]
    sz = 2
    while sz < L:
        mask = level_mask(sz)
        xb = [x_inv[p].astype(BF16) for p in pairs]
        xc = [_dot(xb[p], jnp.where(mask, n_mat[p], 0.0).astype(BF16)).astype(BF16) for p in pairs]
        x_inv = [x_inv[p] - _dot(xc[p], xb[p]) for p in pairs]
        sz *= 2

    a0 = [st_ref[p] for p in pairs]
    a0b = [a0[p].astype(BF16) for p in pairs]
    rhs = [_dot(jnp.concatenate([ws[p], m_mat[p]], axis=1), jnp.concatenate([a0b[p], vs[p]], axis=0)).astype(BF16)
           for p in pairs]
    pm = [(-_dot(x_inv[p].astype(BF16), rhs[p])).astype(BF16) for p in pairs]
    o_st = [_dot(jnp.concatenate([rs[p], nr_mat[p], mr_mat[p]], axis=1),
                 jnp.concatenate([a0b[p], pm[p], vs[p]], axis=0)) for p in pairs]
    for p in pairs:
        g_col = jnp.sum(jnp.where(eye, jnp.broadcast_to(g_l[p], (P2, P2)), 0.0), axis=-1, keepdims=True)
        st_ref[p] = g_col * a0[p] + _dot_tn(jnp.concatenate([ul[p], kl[p]], axis=0),
                                            jnp.concatenate([pm[p], vs[p]], axis=0))

    inv_n = 1.0 / RWKV_HD
    for p, sl in zip(pairs, sls):
        o = o_st[p][:L] + o_st[p][L:]
        mu = _seg_sum(o, first) * inv_n
        oc = o - mu
        var = _seg_sum(oc * oc, first) * inv_n
        on = oc * lax.rsqrt(var + RWKV_GN_EPS) * gg_ref[:, sl] + gb_ref[:, sl]
        o_ref[:, sl] = ((on + bonus[p]) * g_ref[:, sl].astype(F32)).astype(o_ref.dtype)

    s_ref[0] = st_ref[...]


def _wkv_prompt(r, k, v, a, ld, g, hp, n_seq):
    t = r.shape[0]
    L = WKV_CHUNK
    nc = t // n_seq // L
    row = lambda n, c: (n * nc + c, 0)
    blk = pl.BlockSpec((L, D), row)
    npair = RWKV_HEADS // 2
    return pl.pallas_call(
        _wkv_kernel, grid=(n_seq, nc),
        in_specs=[blk] * 6 + [_full((1, D))] * 5,
        out_specs=[blk, pl.BlockSpec((1, npair, LANES, LANES), lambda n, c: (n, 0, 0, 0))],
        out_shape=[jax.ShapeDtypeStruct((t, D), BF16), jax.ShapeDtypeStruct((n_seq, npair, LANES, LANES), F32)],
        scratch_shapes=[pltpu.VMEM((npair, LANES, LANES), F32)],
        compiler_params=_cparams(("parallel", "arbitrary")), name="wkv_chunk")(r, k, v, a, ld, g, *hp)


def _wkv_sample_kernel(r_ref, k_ref, v_ref, a_ref, ld_ref, g_ref, s_ref, kk_ref, ka_ref, rk_ref, gg_ref, gb_ref,
                       o_ref, so_ref):
    hd = RWKV_HD
    eye = lax.broadcasted_iota(jnp.int32, (1, hd, hd), 1) == lax.broadcasted_iota(jnp.int32, (1, hd, hd), 2)
    for h in range(RWKV_HEADS):
        hs = slice(h, h + 1)
        r, k, v, a, ld, g = (ref[:, hs, :] for ref in (r_ref, k_ref, v_ref, a_ref, ld_ref, g_ref))
        s = s_ref[:, h]
        kk = k * kk_ref[hs, :]
        kk = kk / jnp.maximum(jnp.sqrt(jnp.sum(kk * kk, axis=-1, keepdims=True)), 1e-12)
        kmod = k * (1.0 + (a - 1.0) * ka_ref[hs, :])
        skk = jnp.sum(s * kk, axis=-1, keepdims=True)
        v_col = jnp.sum(jnp.where(eye, v, 0.0), axis=-1, keepdims=True)
        s_new = s * jnp.exp(ld) - skk * (kk * a) + v_col * kmod
        so_ref[:, h] = s_new
        o_col = jnp.sum(s_new * r, axis=-1, keepdims=True)
        o = jnp.sum(jnp.where(eye, o_col, 0.0), axis=1, keepdims=True)
        mu = jnp.mean(o, axis=-1, keepdims=True)
        oc = o - mu
        var = jnp.mean(oc * oc, axis=-1, keepdims=True)
        on = oc * lax.rsqrt(var + RWKV_GN_EPS) * gg_ref[hs, :] + gb_ref[hs, :]
        bonus = jnp.sum(r * kmod * rk_ref[hs, :], axis=-1, keepdims=True) * v
        o_ref[:, hs, :] = (on + bonus) * g


def _wkv_sample(r, k, v, a, ld, g, state, hp, bs):
    b = r.shape[0]
    h3 = lambda z: z.reshape(b, RWKV_HEADS, RWKV_HD)
    blk = pl.BlockSpec((bs, RWKV_HEADS, RWKV_HD), lambda i: (i, 0, 0))
    sblk = pl.BlockSpec((bs, RWKV_HEADS, RWKV_HD, RWKV_HD), lambda i: (i, 0, 0, 0))
    hp3 = [z.reshape(RWKV_HEADS, RWKV_HD) for z in hp]
    o, s_new = pl.pallas_call(
        _wkv_sample_kernel, grid=(b // bs,),
        in_specs=[blk] * 6 + [sblk] + [_full((RWKV_HEADS, RWKV_HD))] * 5,
        out_specs=[blk, sblk],
        out_shape=[jax.ShapeDtypeStruct((b, RWKV_HEADS, RWKV_HD), F32), jax.ShapeDtypeStruct(state.shape, F32)],
        compiler_params=_cparams(("parallel",)), name="wkv_sample")(
            h3(r), h3(k), h3(v), h3(a), h3(ld), h3(g), state, *hp3)
    return o.reshape(b, D), s_new


def _mem_prompt_kernel(x_ref, wq_ref, mk_ref, mv_ref, wo_ref, g_ref, b_ref, o_ref):
    x = x_ref[...]
    q = _dot(x.astype(BF16), wq_ref[...]).astype(BF16)
    scale = MEM_HD ** -0.5
    outs = []
    for h in range(MEM_HEADS):
        sl = slice(h * MEM_HD, (h + 1) * MEM_HD)
        s = _dot_nt(q[:, sl], mk_ref[0, :, sl]) * scale
        p = jnp.exp(s - jnp.max(s, axis=-1, keepdims=True))
        den = jnp.sum(p, axis=-1, keepdims=True)
        outs.append(_dot((p / den).astype(BF16), mv_ref[0, :, sl]).astype(BF16))
    acc = _dot(jnp.concatenate(outs, axis=-1), wo_ref[...])
    o_ref[...] = _ln(ALPHA * x + acc, g_ref[...], b_ref[...])


def _mem_attn_prompt(x, w_q, mk, mv, w_o, g, b, n_seq, tm):
    t = x.shape[0]
    nb = t // n_seq // tm
    m = mk.shape[1]
    row = lambda n, i: (n * nb + i, 0)
    mem = pl.BlockSpec((1, m, D), lambda n, i: (n, 0, 0))
    return pl.pallas_call(
        _mem_prompt_kernel, grid=(n_seq, nb),
        in_specs=[pl.BlockSpec((tm, D), row), _full((D, D)), mem, mem, _full((D, D)), _full((1, D)), _full((1, D))],
        out_specs=pl.BlockSpec((tm, D), row), out_shape=jax.ShapeDtypeStruct((t, D), F32),
        compiler_params=_cparams(("parallel", "arbitrary")), name="mem_attn")(x, w_q, mk, mv, w_o, g, b)


def _mem_sample_kernel(q_ref, ck_ref, cv_ref, o_ref, *, bs):
    scale = MEM_HD ** -0.5
    for b in range(bs):
        s = jnp.sum(_round_bf16(ck_ref[b]) * (_round_bf16(q_ref[b]) * scale), axis=-1, keepdims=True)
        p = jnp.exp(s - jnp.max(s, axis=0, keepdims=True))
        p = _round_bf16(p / jnp.sum(p, axis=0, keepdims=True))
        o_ref[b] = jnp.sum(p * _round_bf16(cv_ref[b]), axis=0, keepdims=True)


def _mem_attn_sample(q, cache_k, cache_v, layer, bs):
    _, b, m, nh, hd = cache_k.shape
    qb = pl.BlockSpec((bs, 1, nh, hd), lambda i: (i, 0, 0, 0))
    cb = pl.BlockSpec((None, bs, m, nh, hd), lambda i: (layer, i, 0, 0, 0))
    out = pl.pallas_call(
        functools.partial(_mem_sample_kernel, bs=bs), grid=(b // bs,), in_specs=[qb, cb, cb], out_specs=qb,
        out_shape=jax.ShapeDtypeStruct((b, 1, nh, hd), F32),
        compiler_params=_cparams(("parallel",)), name="mem_sample")(q.reshape(b, 1, nh, hd), cache_k, cache_v)
    return out.reshape(b, D)


_PAIRS = ((0, 1), (0, 2), (0, 3), (1, 2), (1, 3), (2, 3))


def _router_kernel(x_ref, rw_ref, rb_ref, bucket_ref, rank_ref, cnt_ref, base_ref, *, tm):
    i = pl.program_id(0)

    @pl.when(i == 0)
    def _():
        base_ref[...] = jnp.zeros_like(base_ref)

    logits = _dot_nt(rw_ref[...], x_ref[...].astype(BF16))
    e = jnp.exp(logits - jnp.max(logits, axis=0, keepdims=True))
    sel = e / jnp.sum(e, axis=0, keepdims=True) + rb_ref[...]
    s = [sel[j:j + 1, :] for j in range(N_EXPERTS)]
    neg = jnp.float32(-jnp.inf)

    best = jnp.zeros((1, tm), jnp.int32)
    best_score = None
    for gi in range(N_GROUPS):
        s0, s1, s2, s3 = s[4 * gi:4 * gi + 4]
        hi01, lo01, hi23, lo23 = jnp.maximum(s0, s1), jnp.minimum(s0, s1), jnp.maximum(s2, s3), jnp.minimum(s2, s3)
        score = jnp.maximum(hi01, hi23) + jnp.maximum(jnp.minimum(hi01, hi23), jnp.maximum(lo01, lo23))
        if gi == 0:
            best_score = score
        else:
            take = score > best_score
            best = jnp.where(take, gi, best)
            best_score = jnp.where(take, score, best_score)
    vals = []
    for j in range(EXPERTS_PER_GROUP):
        vj = s[j]
        for gi in range(1, N_GROUPS):
            vj = jnp.where(best == gi, s[4 * gi + j], vj)
        vals.append(vj)

    def argmax4(v):
        idx, mx = jnp.zeros((1, tm), jnp.int32), v[0]
        for j in range(1, EXPERTS_PER_GROUP):
            take = v[j] > mx
            idx = jnp.where(take, j, idx)
            mx = jnp.where(take, v[j], mx)
        return idx

    i1 = argmax4(vals)
    i2 = argmax4([jnp.where(i1 == j, neg, vals[j]) for j in range(EXPERTS_PER_GROUP)])
    lo, hi = jnp.minimum(i1, i2), jnp.maximum(i1, i2)
    pair = jnp.zeros((1, tm), jnp.int32)
    for pi, (pa, pb) in enumerate(_PAIRS):
        pair = jnp.where(jnp.logical_and(lo == pa, hi == pb), pi, pair)
    bucket = best * len(_PAIRS) + pair
    bucket_ref[0] = bucket

    onehot = (lax.broadcasted_iota(jnp.int32, (BUCKET_ROWS, tm), 0) == bucket).astype(F32)
    upper = (lax.broadcasted_iota(jnp.int32, (tm, tm), 0) <= lax.broadcasted_iota(jnp.int32, (tm, tm), 1)).astype(BF16)
    cum = _dot(onehot.astype(BF16), upper)
    base = base_ref[...]
    rank = jnp.sum(onehot * (cum + base), axis=0, keepdims=True) - 1.0
    rank_ref[0] = rank.astype(jnp.int32)
    base = base + jnp.sum(onehot, axis=1, keepdims=True)
    base_ref[...] = base
    cnt_ref[...] = jnp.broadcast_to(base, cnt_ref.shape)


def _router(x, rw_t, rb, tm):
    t = x.shape[0]
    nb = t // tm
    ib = pl.BlockSpec((1, 1, tm), lambda i: (i, 0, 0))
    bucket, rank, cnt = pl.pallas_call(
        functools.partial(_router_kernel, tm=tm), grid=(nb,),
        in_specs=[pl.BlockSpec((tm, D), lambda i: (i, 0)), _full(rw_t.shape), _full(rb.shape)],
        out_specs=[ib, ib, _full((BUCKET_ROWS, LANES))],
        out_shape=[jax.ShapeDtypeStruct((nb, 1, tm), jnp.int32), jax.ShapeDtypeStruct((nb, 1, tm), jnp.int32),
                   jax.ShapeDtypeStruct((BUCKET_ROWS, LANES), F32)],
        scratch_shapes=[pltpu.VMEM((BUCKET_ROWS, 1), F32)],
        compiler_params=_cparams(("arbitrary",)), name="router")(x, rw_t, rb)
    return bucket.reshape(t), rank.reshape(t), cnt[:N_BUCKETS, 0].astype(jnp.int32)


def _row_copies(idx_ref, base, src_hbm, dst, sem, n, wait):
    def body(r, carry):
        cp = pltpu.make_async_copy(src_hbm.at[pl.ds(idx_ref[base + r], 1)], dst.at[pl.ds(r, 1)], sem)
        if wait:
            cp.wait()
        else:
            cp.start()
        return carry

    lax.fori_loop(0, n, body, 0, unroll=8)


def _ffn_kernel(src_ref, lo_ref, hi_ref, used_ref, x_hbm, rw_ref, g0_ref, u0_ref, d0_ref, g1_ref, u1_ref, d1_ref,
                o_ref, xbuf, sem, *, blk):
    i = pl.program_id(0)
    used = used_ref[0]
    slot = i % 2

    @pl.when(jnp.logical_and(i == 0, used > 0))
    def _():
        _row_copies(src_ref, 0, x_hbm, xbuf.at[0], sem.at[0], blk, False)

    @pl.when(i < used)
    def _():
        _row_copies(src_ref, i * blk, x_hbm, xbuf.at[slot], sem.at[slot], blk, True)
        xb = xbuf[slot].astype(BF16)
        for r in range(blk):
            pltpu.make_async_copy(x_hbm.at[pl.ds(src_ref[(i + 1) * blk + r], 1)], xbuf.at[1 - slot, pl.ds(r, 1)],
                                  sem.at[1 - slot]).start()
        logits = _dot(xb, rw_ref[...])
        lane = lax.broadcasted_iota(jnp.int32, logits.shape, 1)
        l_lo = jnp.sum(jnp.where(lane == lo_ref[i], logits, 0.0), axis=-1, keepdims=True)
        l_hi = jnp.sum(jnp.where(lane == hi_ref[i], logits, 0.0), axis=-1, keepdims=True)
        w_lo = _sigmoid(l_lo - l_hi)

        def expert(g_ref, u_ref, d_ref):
            gate = _dot(xb, g_ref[0])
            act = gate * _sigmoid(gate) * _dot(xb, u_ref[0])
            return _dot(act.astype(BF16), d_ref[0])

        y_lo = expert(g0_ref, u0_ref, d0_ref)
        y_hi = expert(g1_ref, u1_ref, d1_ref)
        o_ref[...] = w_lo * y_lo + (1.0 - w_lo) * y_hi

    @pl.when(jnp.logical_and(i == used, used > 0))
    def _():
        _row_copies(src_ref, i * blk, x_hbm, xbuf.at[slot], sem.at[slot], blk, True)

    @pl.when(i >= used)
    def _():
        o_ref[...] = jnp.zeros_like(o_ref)


def _ffn(x, src, blk_lo, blk_hi, n_used, rw, w_gate, w_up, w_down, layer, blk):
    rows = src.shape[0]
    nblk = rows // blk
    wg = lambda sel: pl.BlockSpec((None, 1, D, EXPERT_FF),
                                  lambda i, s, lo, hi, used: (layer, (lo, hi)[sel][i], 0, 0))
    wd = lambda sel: pl.BlockSpec((None, 1, EXPERT_FF, D),
                                  lambda i, s, lo, hi, used: (layer, (lo, hi)[sel][i], 0, 0))
    return pl.pallas_call(
        functools.partial(_ffn_kernel, blk=blk),
        grid_spec=pltpu.PrefetchScalarGridSpec(
            num_scalar_prefetch=4, grid=(nblk,),
            in_specs=[pl.BlockSpec(memory_space=pl.ANY), pl.BlockSpec(rw.shape, lambda i, s, lo, hi, used: (0, 0)),
                      wg(0), wg(0), wd(0), wg(1), wg(1), wd(1)],
            out_specs=pl.BlockSpec((blk, D), lambda i, s, lo, hi, used: (i, 0)),
            scratch_shapes=[pltpu.VMEM((2, blk, D), F32), pltpu.SemaphoreType.DMA((2,))]),
        out_shape=jax.ShapeDtypeStruct((rows, D), F32),
        compiler_params=_cparams(("arbitrary",)), name="moe_ffn")(
            src, blk_lo, blk_hi, n_used, x, rw, w_gate, w_up, w_down, w_gate, w_up, w_down)


def _combine_ln_kernel(dest_ref, x_ref, y_hbm, g_ref, b_ref, o_ref, ybuf, sem, *, tm):
    i = pl.program_id(0)
    slot = i % 2

    @pl.when(i == 0)
    def _():
        _row_copies(dest_ref, 0, y_hbm, ybuf.at[0], sem.at[0], tm, False)

    @pl.when(i + 1 < pl.num_programs(0))
    def _():
        _row_copies(dest_ref, (i + 1) * tm, y_hbm, ybuf.at[1 - slot], sem.at[1 - slot], tm, False)

    _row_copies(dest_ref, i * tm, y_hbm, ybuf.at[slot], sem.at[slot], tm, True)
    o_ref[...] = _ln(ALPHA * x_ref[...] + ybuf[slot], g_ref[...], b_ref[...])


def _combine_ln(x, y_rows, dest, g, b, tm):
    t = x.shape[0]
    rowb = pl.BlockSpec((tm, D), lambda i, d: (i, 0))
    vec = pl.BlockSpec((1, D), lambda i, d: (0, 0))
    return pl.pallas_call(
        functools.partial(_combine_ln_kernel, tm=tm),
        grid_spec=pltpu.PrefetchScalarGridSpec(
            num_scalar_prefetch=1, grid=(t // tm,),
            in_specs=[rowb, pl.BlockSpec(memory_space=pl.ANY), vec, vec], out_specs=rowb,
            scratch_shapes=[pltpu.VMEM((2, tm, D), F32), pltpu.SemaphoreType.DMA((2,))]),
        out_shape=jax.ShapeDtypeStruct((t, D), F32),
        compiler_params=_cparams(("arbitrary",)), name="moe_combine_ln")(dest, x, y_rows, g, b)


def _moe_ln(x, rw_t, rb, rw_pad, w_gate, w_up, w_down, layer, g, b, tm_router, blk, tm_comb):
    t = x.shape[0]
    bucket, rank, counts = _router(x, rw_t, rb, tm_router)
    padded = (counts + blk - 1) // blk * blk
    ends = jnp.cumsum(padded)
    dest = ((ends - padded)[bucket] + rank).astype(jnp.int32)
    nblk = t // blk + N_BUCKETS + 1
    src = jnp.zeros((nblk * blk,), jnp.int32).at[dest].set(jnp.arange(t, dtype=jnp.int32))
    blk_bucket = jnp.minimum(jnp.searchsorted(ends, jnp.arange(nblk) * blk, side='right'), N_BUCKETS - 1)
    pair_lo = jnp.array([p[0] for p in _PAIRS], jnp.int32)
    pair_hi = jnp.array([p[1] for p in _PAIRS], jnp.int32)
    grp, pr = blk_bucket // len(_PAIRS), blk_bucket % len(_PAIRS)
    blk_lo = (grp * EXPERTS_PER_GROUP + pair_lo[pr]).astype(jnp.int32)
    blk_hi = (grp * EXPERTS_PER_GROUP + pair_hi[pr]).astype(jnp.int32)
    n_used = (ends[-1:] // blk).astype(jnp.int32)
    y_rows = _ffn(x, src, blk_lo, blk_hi, n_used, rw_pad, w_gate, w_up, w_down, layer, blk)
    return _combine_ln(x, y_rows, dest, g, b, tm_comb)


def _moe_dense_kernel(x_ref, lo_ref, hi_ref, rw_ref, wg_ref, wu_ref, wd_ref, g_ref, b_ref, o_ref, acc_ref):
    e = pl.program_id(0)

    @pl.when(e == 0)
    def _():
        acc_ref[...] = jnp.zeros_like(acc_ref)

    x = x_ref[...]
    xb = x.astype(BF16)
    logits = _dot(xb, rw_ref[...])
    lane = lax.broadcasted_iota(jnp.int32, logits.shape, 1)
    lo, hi = lo_ref[...], hi_ref[...]
    l_lo = jnp.sum(jnp.where(lane == lo, logits, 0.0), axis=-1, keepdims=True)
    l_hi = jnp.sum(jnp.where(lane == hi, logits, 0.0), axis=-1, keepdims=True)
    w_lo = _sigmoid(l_lo - l_hi)
    coef = jnp.where(lo == e, w_lo, 0.0) + jnp.where(hi == e, 1.0 - w_lo, 0.0)
    gate = _dot(xb, wg_ref[0])
    act = gate * _sigmoid(gate) * _dot(xb, wu_ref[0])
    acc_ref[...] += coef * _dot(act.astype(BF16), wd_ref[0])

    @pl.when(e == pl.num_programs(0) - 1)
    def _():
        o_ref[...] = _ln(ALPHA * x + acc_ref[...], g_ref[...], b_ref[...])


def _moe_ln_dense(x, rw_t, rb, rw_pad, w_gate, w_up, w_down, layer, g, b):
    t = x.shape[0]
    bucket, _, _ = _router(x, rw_t, rb, t)
    pair_lo = jnp.array([p[0] for p in _PAIRS], jnp.int32)
    pair_hi = jnp.array([p[1] for p in _PAIRS], jnp.int32)
    grp, pr = bucket // len(_PAIRS), bucket % len(_PAIRS)
    lo = (grp * EXPERTS_PER_GROUP + pair_lo[pr]).astype(jnp.int32).reshape(t, 1)
    hi = (grp * EXPERTS_PER_GROUP + pair_hi[pr]).astype(jnp.int32).reshape(t, 1)
    wg = pl.BlockSpec((None, 1, D, EXPERT_FF), lambda e: (layer, e, 0, 0))
    wd = pl.BlockSpec((None, 1, EXPERT_FF, D), lambda e: (layer, e, 0, 0))
    return pl.pallas_call(
        _moe_dense_kernel, grid=(N_EXPERTS,),
        in_specs=[_full((t, D)), _full((t, 1)), _full((t, 1)), _full(rw_pad.shape), wg, wg, wd,
                  _full((1, D)), _full((1, D))],
        out_specs=_full((t, D)), out_shape=jax.ShapeDtypeStruct((t, D), F32),
        scratch_shapes=[pltpu.VMEM((t, D), F32)],
        compiler_params=_cparams(("arbitrary",)), name="moe_dense")(x, lo, hi, rw_pad, w_gate, w_up, w_down, g, b)


def kernel(x_prompt, x_sample, cache_swa_k, cache_swa_v, state_lru_conv, state_lru_h, state_rwkv_shift, state_rwkv_wkv, cache_mem_k, cache_mem_v, mem_prompt, swa_w_qkv, swa_sinks, swa_w_o, lru_w_in, lru_b_in, lru_conv_w, lru_conv_b, lru_w_a, lru_b_a, lru_w_i, lru_b_i, lru_lambda, lru_w_o, rwkv_mu, rwkv_w_r, rwkv_w_k, rwkv_w_v, rwkv_w0, rwkv_w1, rwkv_w2, rwkv_a0, rwkv_a1, rwkv_a2, rwkv_g1, rwkv_g2, rwkv_k_k, rwkv_k_a, rwkv_r_k, rwkv_gn_g, rwkv_gn_b, rwkv_w_o, mem_w_q, mem_w_kv, mem_w_o, ln_g, ln_b, router_w, router_b, moe_w_gate, moe_w_up, moe_w_down):
    n_p, seq, _ = x_prompt.shape
    n_s, dec_seq, _ = x_sample.shape
    assert dec_seq == 1
    past_len = 8192
    xp = x_prompt.reshape(n_p * seq, D)
    xs = x_sample.reshape(n_s, D)
    row = lambda v: v.reshape(1, -1)
    bf = lambda w: w.astype(BF16)

    rw_t = bf(router_w.T)
    rb = router_b.reshape(N_EXPERTS, 1)
    rw_pad = bf(jnp.pad(router_w, ((0, 0), (0, LANES - N_EXPERTS))))
    wg, wu, wd = bf(moe_w_gate), bf(moe_w_up), bf(moe_w_down)
    mem_p = mem_prompt.reshape(n_p * mem_prompt.shape[1], D)
    m_len = mem_prompt.shape[1]

    swa_k_p, swa_v_p, swa_k_s, swa_v_s = [], [], [], []
    lru_c_p, lru_h_p, lru_c_s, lru_h_s = [], [], [], []
    rw_x_p, rw_s_p, rw_x_s, rw_s_s = [], [], [], []
    mem_k_p, mem_v_p = [], []

    for layer in range(DEPTH):
        kind, i = layer % N_MIXERS, layer // N_MIXERS
        g0, b0 = row(ln_g[layer, 0]), row(ln_b[layer, 0])
        if kind == 0:
            w_qkv, w_o = bf(swa_w_qkv[i]), bf(swa_w_o[i])
            keep = min(WINDOW, seq)
            q, k, v, kv_last = _swa_qkv(xp, w_qkv, jnp.arange(seq), n_p, 512, keep, BF16)
            o = _swa_attn_prompt(q, k, v, swa_sinks[i], n_p, 2)
            swa_k_p.append(kv_last[:, :, :KV_WIDTH].reshape(n_p, keep, SWA_KV_HEADS, HEAD_DIM))
            swa_v_p.append(kv_last[:, :, KV_WIDTH:].reshape(n_p, keep, SWA_KV_HEADS, HEAD_DIM))
            xp = _proj_ln(o, w_o, xp, g0, b0, 512)

            qs, _, _, kv_new = _swa_qkv(xs, w_qkv, jnp.full((n_s,), past_len), 1, n_s, n_s, F32)
            kn, vn = kv_new[0, :, :KV_WIDTH], kv_new[0, :, KV_WIDTH:]
            os_ = _swa_attn_sample(qs, kn, vn, cache_swa_k[i], cache_swa_v[i], swa_sinks[i], 8)
            wb = cache_swa_k.shape[2]
            k_all = jnp.concatenate([cache_swa_k[i], kn.reshape(n_s, 1, SWA_KV_HEADS, HEAD_DIM)], axis=1)
            v_all = jnp.concatenate([cache_swa_v[i], vn.reshape(n_s, 1, SWA_KV_HEADS, HEAD_DIM)], axis=1)
            swa_k_s.append(k_all[:, -wb:])
            swa_v_s.append(v_all[:, -wb:])
            xs = _proj_ln(os_, w_o, xs, g0, b0, n_s)
        elif kind == 1:
            wts = _lru_weights(lru_w_in[i], lru_b_in[i], lru_conv_w[i], lru_conv_b[i], lru_w_a[i], lru_b_a[i],
                               lru_w_i[i], lru_b_i[i], lru_lambda[i], lru_w_o[i])
            xp, conv_last, h_last = _lru_prompt(xp, wts, g0, b0, n_p, 256)
            lru_c_p.append(conv_last[:, SUBLANES - (CONV_W - 1):])
            lru_h_p.append(h_last[:, SUBLANES - 1])
            xs, xb_s, h_s = _lru_sample(xs, state_lru_conv[i], state_lru_h[i], wts, g0, b0)
            lru_c_s.append(jnp.concatenate([state_lru_conv[i][:, 1:], xb_s[:, None]], axis=1))
            lru_h_s.append(h_s)
        else:
            wts = (rwkv_mu[i], bf(rwkv_w_r[i]), bf(rwkv_w_k[i]), bf(rwkv_w_v[i]), row(rwkv_w0[i]), bf(rwkv_w1[i]),
                   bf(rwkv_w2[i]), row(rwkv_a0[i]), bf(rwkv_a1[i]), bf(rwkv_a2[i]), bf(rwkv_g1[i]), bf(rwkv_g2[i]))
            hp = (row(rwkv_k_k[i]), row(rwkv_k_a[i]), row(rwkv_r_k[i]), row(rwkv_gn_g[i]), row(rwkv_gn_b[i]))
            w_o = bf(rwkv_w_o[i])
            rw_x_p.append(xp.reshape(n_p, seq, D)[:, -1])
            rw_x_s.append(xs)
            r, k, v, a, ld, g = _rwkv_pre(xp, jnp.zeros((n_p, SUBLANES, D), F32), wts, n_p, 256, True, BF16)
            o, st = _wkv_prompt(r, k, v, a, ld, g, hp, n_p)
            hd = RWKV_HD
            st = jnp.stack([st[:, :, :hd, :hd], st[:, :, hd:, hd:]], axis=2).reshape(n_p, RWKV_HEADS, hd, hd)
            rw_s_p.append(jnp.swapaxes(st, -1, -2))
            xp = _proj_ln(o, w_o, xp, g0, b0, 512)

            r, k, v, a, ld, g = _rwkv_pre(xs, state_rwkv_shift[i], wts, 1, n_s, False, F32)
            os_, s_new = _wkv_sample(r, k, v, a, ld, g, state_rwkv_wkv[i], hp, 8)
            rw_s_s.append(s_new)
            xs = _proj_ln(os_, w_o, xs, g0, b0, n_s)

        g1, b1 = row(ln_g[layer, 1]), row(ln_b[layer, 1])
        w_q, w_o = bf(mem_w_q[layer]), bf(mem_w_o[layer])
        mkv = _matmul(mem_p, bf(mem_w_kv[layer]), 512)
        mk, mv = mkv[:, :D], mkv[:, D:]
        mem_k_p.append(mk.reshape(n_p, m_len, MEM_HEADS, MEM_HD))
        mem_v_p.append(mv.reshape(n_p, m_len, MEM_HEADS, MEM_HD))
        xp = _mem_attn_prompt(xp, w_q, bf(mk).reshape(n_p, m_len, D), bf(mv).reshape(n_p, m_len, D), w_o, g1, b1,
                              n_p, 512)
        qs = _matmul(xs, w_q, n_s)
        os_ = _mem_attn_sample(qs, cache_mem_k, cache_mem_v, layer, 4)
        xs = _proj_ln(os_, w_o, xs, g1, b1, n_s)

        g2, b2 = row(ln_g[layer, 2]), row(ln_b[layer, 2])
        xp = _moe_ln(xp, rw_t, rb, rw_pad, wg, wu, wd, layer, g2, b2, 512, 256, 256)
        xs = _moe_ln_dense(xs, rw_t, rb, rw_pad, wg, wu, wd, layer, g2, b2)

    return (xp.reshape(n_p, seq, D), xs.reshape(n_s, 1, D),
            jnp.stack(swa_k_p), jnp.stack(swa_v_p), jnp.stack(lru_c_p), jnp.stack(lru_h_p),
            jnp.stack(rw_x_p), jnp.stack(rw_s_p), jnp.stack(mem_k_p), jnp.stack(mem_v_p),
            jnp.stack(swa_k_s), jnp.stack(swa_v_s), jnp.stack(lru_c_s), jnp.stack(lru_h_s),
            jnp.stack(rw_x_s), jnp.stack(rw_s_s))
```

```python
import functools

import jax
import jax.numpy as jnp
from jax import lax
from jax.experimental import pallas as pl
from jax.experimental.pallas import tpu as pltpu

F32 = jnp.float32
BF16 = jnp.bfloat16

D = 1024
DEPTH = 4
N_MIXERS = 3
HEAD_DIM = 64
SWA_HEADS = D // HEAD_DIM
SWA_KV_HEADS = 4
SWA_GROUP = SWA_HEADS // SWA_KV_HEADS
Q_WIDTH = SWA_HEADS * HEAD_DIM
KV_WIDTH = SWA_KV_HEADS * HEAD_DIM
WINDOW = 128
ROT_DIM = HEAD_DIM // 4
ROPE_THETA = 500000.0
LRU_BLOCKS = 16
CONV_W = 4
LRU_C = 8.0
RWKV_HEADS = 16
RWKV_HD = 64
RWKV_GN_EPS = 64e-5
MEM_HEADS = 4
MEM_HD = D // MEM_HEADS
N_EXPERTS = 16
N_GROUPS = 4
EXPERTS_PER_GROUP = 4
EXPERT_FF = 512
LN_EPS = 1e-5
ALPHA = (2.0 * DEPTH) ** 0.25
NEG_INF = -1e30

LANES = 128
SUBLANES = 8
VMEM_LIMIT = 56 * 1024 * 1024
WKV_CHUNK = 64
N_BUCKETS = N_GROUPS * 6
BUCKET_ROWS = 32


def _cparams(sem):
    return pltpu.CompilerParams(dimension_semantics=sem, vmem_limit_bytes=VMEM_LIMIT)


def _dot(a, b):
    return jnp.dot(a, b, preferred_element_type=F32)


def _dot_nt(a, b):
    return lax.dot_general(a, b, (((1,), (1,)), ((), ())), preferred_element_type=F32)


def _dot_tn(a, b):
    return lax.dot_general(a, b, (((0,), (0,)), ((), ())), preferred_element_type=F32)


def _ln(z, g, b):
    mu = jnp.mean(z, axis=-1, keepdims=True)
    zc = z - mu
    var = jnp.mean(zc * zc, axis=-1, keepdims=True)
    return zc * lax.rsqrt(var + LN_EPS) * g + b


def _softplus(z):
    return jnp.maximum(z, 0.0) + jnp.log1p(jnp.exp(-jnp.abs(z)))


def _sigmoid(z):
    return 1.0 / (1.0 + jnp.exp(-z))


def _round_bf16(x):
    return x.astype(BF16).astype(F32)


def _full(shape):
    nd = len(shape)
    return pl.BlockSpec(shape, lambda *_: (0,) * nd)


def _mm_kernel(a_ref, w_ref, o_ref):
    o_ref[...] = _dot(a_ref[...].astype(BF16), w_ref[...]).astype(o_ref.dtype)


def _matmul(a, w, tm, out_dtype=F32):
    t, k = a.shape
    n = w.shape[1]
    return pl.pallas_call(
        _mm_kernel, grid=(t // tm,),
        in_specs=[pl.BlockSpec((tm, k), lambda i: (i, 0)), _full((k, n))],
        out_specs=pl.BlockSpec((tm, n), lambda i: (i, 0)),
        out_shape=jax.ShapeDtypeStruct((t, n), out_dtype),
        compiler_params=_cparams(("parallel",)), name="matmul")(a, w)


def _proj_ln_kernel(a_ref, w_ref, x_ref, g_ref, b_ref, o_ref):
    acc = _dot(a_ref[...].astype(BF16), w_ref[...])
    o_ref[...] = _ln(ALPHA * x_ref[...] + acc, g_ref[...], b_ref[...])


def _proj_ln(a, w, x, g, b, tm):
    t, k = a.shape
    return pl.pallas_call(
        _proj_ln_kernel, grid=(t // tm,),
        in_specs=[pl.BlockSpec((tm, k), lambda i: (i, 0)), _full((k, D)),
                  pl.BlockSpec((tm, D), lambda i: (i, 0)), _full((1, D)), _full((1, D))],
        out_specs=pl.BlockSpec((tm, D), lambda i: (i, 0)),
        out_shape=jax.ShapeDtypeStruct((t, D), F32),
        compiler_params=_cparams(("parallel",)), name="proj_ln")(a, w, x, g, b)


def _rope_tables(pos):
    half = ROT_DIM // 2
    inv_freq = ROPE_THETA ** (-jnp.arange(half, dtype=F32) / half)
    ang = pos.astype(F32)[:, None] * inv_freq
    cos, sin = jnp.cos(ang), jnp.sin(ang)
    one = jnp.ones((pos.shape[0], HEAD_DIM - ROT_DIM), F32)
    zero = jnp.zeros((pos.shape[0], HEAD_DIM - ROT_DIM), F32)
    zh = jnp.zeros_like(sin)
    c = jnp.concatenate([cos, cos, one], axis=1)
    s1 = jnp.concatenate([-sin, zh, zero], axis=1)
    s2 = jnp.concatenate([zh, sin, zero], axis=1)
    rep = LANES // HEAD_DIM
    return jnp.tile(c, (1, rep)), jnp.tile(s1, (1, rep)), jnp.tile(s2, (1, rep))


def _swa_qkv_kernel(x_ref, w_ref, c_ref, s1_ref, s2_ref, q_ref, k_ref, v_ref, kv_ref, *, tm, keep):
    acc = _dot(x_ref[...].astype(BF16), w_ref[...])
    c, s1, s2 = c_ref[...], s1_ref[...], s2_ref[...]
    half = ROT_DIM // 2
    n_q = Q_WIDTH // LANES
    n_k = KV_WIDTH // LANES
    for cg in range(n_q + n_k):
        xg = acc[:, cg * LANES:(cg + 1) * LANES]
        rot = xg * c + pltpu.roll(xg, LANES - half, 1) * s1 + pltpu.roll(xg, half, 1) * s2
        if cg < n_q:
            q_ref[:, cg * LANES:(cg + 1) * LANES] = rot.astype(q_ref.dtype)
        else:
            ck = cg - n_q
            k_ref[:, ck * LANES:(ck + 1) * LANES] = rot.astype(k_ref.dtype)
            kv_ref[0, :, ck * LANES:(ck + 1) * LANES] = rot[tm - keep:, :]
    v = acc[:, Q_WIDTH + KV_WIDTH:]
    v_ref[...] = v.astype(v_ref.dtype)
    kv_ref[0, :, KV_WIDTH:] = v[tm - keep:, :]


def _swa_qkv(x, w_qkv, pos, n_seq, tm, keep, qdtype):
    t = x.shape[0]
    s = t // n_seq
    nb = s // tm
    c, s1, s2 = _rope_tables(pos)
    row = lambda n, i: (n * nb + i, 0)
    tab = pl.BlockSpec((tm, LANES), lambda n, i: (i, 0))
    kern = functools.partial(_swa_qkv_kernel, tm=tm, keep=keep)
    return pl.pallas_call(
        kern, grid=(n_seq, nb),
        in_specs=[pl.BlockSpec((tm, D), row), _full((D, Q_WIDTH + 2 * KV_WIDTH)), tab, tab, tab],
        out_specs=[pl.BlockSpec((tm, Q_WIDTH), row), pl.BlockSpec((tm, KV_WIDTH), row),
                   pl.BlockSpec((tm, KV_WIDTH), row),
                   pl.BlockSpec((1, keep, 2 * KV_WIDTH), lambda n, i: (n, 0, 0))],
        out_shape=[jax.ShapeDtypeStruct((t, Q_WIDTH), qdtype), jax.ShapeDtypeStruct((t, KV_WIDTH), qdtype),
                   jax.ShapeDtypeStruct((t, KV_WIDTH), qdtype),
                   jax.ShapeDtypeStruct((n_seq, keep, 2 * KV_WIDTH), F32)],
        compiler_params=_cparams(("parallel", "arbitrary")), name="swa_qkv")(x, w_qkv, c, s1, s2)


def _swa_attn_kernel(sink_ref, q_ref, kp_ref, kc_ref, vp_ref, vc_ref, o_ref, *, nq):
    j = pl.program_id(1)
    w, grp = WINDOW, SWA_GROUP
    r = lax.broadcasted_iota(jnp.int32, (grp * w, 2 * w), 0) % w
    c = lax.broadcasted_iota(jnp.int32, (grp * w, 2 * w), 1)
    in_prev = jnp.logical_and(c < w, c > r)
    in_cur = jnp.logical_and(c >= w, (c - w) <= r)
    ok_inner = jnp.logical_or(in_prev, in_cur)
    ok_first = jnp.logical_or(jnp.logical_and(in_prev, j > 0), in_cur)
    scale = HEAD_DIM ** -0.5
    combos = [(u, h) for u in range(nq) for h in range(SWA_KV_HEADS)]
    kcat, vcat, q4, sink, ok = [], [], [], [], []
    for u, h in combos:
        sl = slice(h * HEAD_DIM, (h + 1) * HEAD_DIM)
        rows = slice(u * w, (u + 1) * w)
        before = slice((u - 1) * w, u * w)
        k_prev = kp_ref[:, sl] if u == 0 else kc_ref[before, sl]
        v_prev = vp_ref[:, sl] if u == 0 else vc_ref[before, sl]
        kcat.append(jnp.concatenate([k_prev, kc_ref[rows, sl]], axis=0))
        vcat.append(jnp.concatenate([v_prev, vc_ref[rows, sl]], axis=0))
        heads = [h * grp + g for g in range(grp)]
        q4.append(jnp.concatenate([q_ref[rows, hq * HEAD_DIM:(hq + 1) * HEAD_DIM] for hq in heads], axis=0))
        sink.append(jnp.concatenate([jnp.full((w, 1), sink_ref[hq], F32) for hq in heads], axis=0))
        ok.append(ok_first if u == 0 else ok_inner)
    n = range(len(combos))
    s = [jnp.where(ok[i], _dot_nt(q4[i], kcat[i]) * scale, NEG_INF) for i in n]
    m = [jnp.maximum(jnp.max(s[i], axis=-1, keepdims=True), sink[i]) for i in n]
    p = [jnp.exp(s[i] - m[i]) for i in n]
    den = [jnp.sum(p[i], axis=-1, keepdims=True) + jnp.exp(sink[i] - m[i]) for i in n]
    o = [_dot((p[i] / den[i]).astype(BF16), vcat[i]) for i in n]
    for i, (u, h) in enumerate(combos):
        for g in range(grp):
            hq = h * grp + g
            o_ref[u * w:(u + 1) * w, hq * HEAD_DIM:(hq + 1) * HEAD_DIM] = o[i][g * w:(g + 1) * w].astype(o_ref.dtype)


def _swa_attn_prompt(q, k, v, sinks, n_seq, nq):
    t = q.shape[0]
    nb = t // n_seq // WINDOW
    ns = nb // nq
    cur = lambda n, j: (n * ns + j, 0)
    prev = lambda n, j: (n * nb + jnp.maximum(j * nq - 1, 0), 0)
    return pl.pallas_call(
        functools.partial(_swa_attn_kernel, nq=nq), grid=(n_seq, ns),
        in_specs=[pl.BlockSpec(memory_space=pltpu.SMEM), pl.BlockSpec((nq * WINDOW, Q_WIDTH), cur),
                  pl.BlockSpec((WINDOW, KV_WIDTH), prev), pl.BlockSpec((nq * WINDOW, KV_WIDTH), cur),
                  pl.BlockSpec((WINDOW, KV_WIDTH), prev), pl.BlockSpec((nq * WINDOW, KV_WIDTH), cur)],
        out_specs=pl.BlockSpec((nq * WINDOW, Q_WIDTH), cur),
        out_shape=jax.ShapeDtypeStruct((t, Q_WIDTH), BF16),
        compiler_params=_cparams(("parallel", "arbitrary")), name="swa_attn")(sinks, q, k, k, v, v)


def _swa_sample_kernel(sink_ref, q_ref, kn_ref, vn_ref, ck_ref, cv_ref, o_ref, *, wb):
    kidx = lax.broadcasted_iota(jnp.int32, (1, wb, 1), 1)
    valid = (wb - kidx) < WINDOW
    scale = HEAD_DIM ** -0.5
    for h in range(SWA_KV_HEADS):
        sl = slice(h * HEAD_DIM, (h + 1) * HEAD_DIM)
        ck, cv = _round_bf16(ck_ref[:, :, sl]), _round_bf16(cv_ref[:, :, sl])
        kn, vn = _round_bf16(kn_ref[:, :, sl]), _round_bf16(vn_ref[:, :, sl])
        for g in range(SWA_GROUP):
            hq = h * SWA_GROUP + g
            qs = slice(hq * HEAD_DIM, (hq + 1) * HEAD_DIM)
            qh = _round_bf16(q_ref[:, :, qs])
            s = jnp.where(valid, jnp.sum(ck * qh, axis=-1, keepdims=True) * scale, NEG_INF)
            sn = jnp.sum(kn * qh, axis=-1, keepdims=True) * scale
            sink = sink_ref[hq]
            m = jnp.maximum(jnp.maximum(jnp.max(s, axis=1, keepdims=True), sn), sink)
            p = jnp.exp(s - m)
            pn = jnp.exp(sn - m)
            den = jnp.sum(p, axis=1, keepdims=True) + pn + jnp.exp(sink - m)
            p, pn = _round_bf16(p / den), _round_bf16(pn / den)
            o_ref[:, :, qs] = jnp.sum(p * cv, axis=1, keepdims=True) + pn * vn


def _swa_attn_sample(q, kn, vn, cache_k, cache_v, sinks, bs):
    b, wb = cache_k.shape[0], cache_k.shape[1]
    blk3 = lambda w: pl.BlockSpec((bs, 1, w), lambda i: (i, 0, 0))
    cblk = pl.BlockSpec((bs, wb, KV_WIDTH), lambda i: (i, 0, 0))
    out = pl.pallas_call(
        functools.partial(_swa_sample_kernel, wb=wb), grid=(b // bs,),
        in_specs=[pl.BlockSpec(memory_space=pltpu.SMEM), blk3(Q_WIDTH), blk3(KV_WIDTH), blk3(KV_WIDTH), cblk, cblk],
        out_specs=blk3(Q_WIDTH), out_shape=jax.ShapeDtypeStruct((b, 1, Q_WIDTH), F32),
        compiler_params=_cparams(("parallel",)), name="swa_sample")(
            sinks, q.reshape(b, 1, Q_WIDTH), kn.reshape(b, 1, KV_WIDTH), vn.reshape(b, 1, KV_WIDTH),
            cache_k.reshape(b, wb, KV_WIDTH), cache_v.reshape(b, wb, KV_WIDTH))
    return out.reshape(b, Q_WIDTH)


def _gelu_tanh(x):
    return 0.5 * x * (1.0 + jnp.tanh(0.7978845608028654 * (x + 0.044715 * x * x * x)))


def _lru_gates(xc, wa_ref, ba, wi_ref, bi, lam):
    xcb = xc.astype(BF16)
    gw = wa_ref.shape[1]
    ra, ia = [], []
    for gi in range(wa_ref.shape[0]):
        xs = xcb[:, gi * gw:(gi + 1) * gw]
        ra.append(_dot(xs, wa_ref[gi]))
        ia.append(_dot(xs, wi_ref[gi]))
    r = _sigmoid(jnp.concatenate(ra, axis=-1) + ba)
    ig = _sigmoid(jnp.concatenate(ia, axis=-1) + bi)
    log_a = -LRU_C * r * _softplus(-lam)
    a = jnp.exp(log_a)
    b = jnp.sqrt(-jnp.tanh(log_a) * (a * a + 1.0)) * (ig * xc)
    return a, b


def _shift_rows(ext, s, tm):
    return pltpu.roll(ext, s, 0)[SUBLANES:SUBLANES + tm]


def _lru_prompt_kernel(x_ref, win_ref, bin_ref, cw_ref, cb_ref, wa_ref, ba_ref, wi_ref, bi_ref, lam_ref,
                       wo_ref, g_ref, b_ref, o_ref, conv_ref, hl_ref, cx_ref, ch_ref, *, tm):
    i = pl.program_id(1)

    @pl.when(i == 0)
    def _():
        cx_ref[...] = jnp.zeros_like(cx_ref)
        ch_ref[...] = jnp.zeros_like(ch_ref)

    x = x_ref[...]
    xy = _dot(x.astype(BF16), win_ref[...]) + bin_ref[...]
    xb = xy[:, :D]
    y_gate = _gelu_tanh(xy[:, D:])
    ext = jnp.concatenate([cx_ref[...], xb], axis=0)
    cw = cw_ref[...]
    xc = cb_ref[...] + xb * cw[CONV_W - 1:CONV_W]
    for s in range(1, CONV_W):
        xc = xc + _shift_rows(ext, s, tm) * cw[CONV_W - 1 - s:CONV_W - s]
    cx_ref[...] = xb[tm - SUBLANES:]
    conv_ref[0] = xb[tm - SUBLANES:]

    a, b = _lru_gates(xc, wa_ref, ba_ref[...], wi_ref, bi_ref[...], lam_ref[...])
    sub = lax.broadcasted_iota(jnp.int32, (tm, 1), 0) % SUBLANES
    s = 1
    while s < SUBLANES:
        keep = sub >= s
        a_sh = jnp.where(keep, pltpu.roll(a, s, 0), 1.0)
        b_sh = jnp.where(keep, pltpu.roll(b, s, 0), 0.0)
        b = a * b_sh + b
        a = a * a_sh
        s *= 2
    carry = ch_ref[SUBLANES - 1:SUBLANES, :]
    groups = []
    for gi in range(tm // SUBLANES):
        rows = slice(gi * SUBLANES, (gi + 1) * SUBLANES)
        hg = a[rows] * carry + b[rows]
        groups.append(hg)
        carry = hg[SUBLANES - 1:SUBLANES]
    h = jnp.concatenate(groups, axis=0)
    ch_ref[...] = h[tm - SUBLANES:]
    hl_ref[0] = h[tm - SUBLANES:]
    acc = _dot((h * y_gate).astype(BF16), wo_ref[...])
    o_ref[...] = _ln(ALPHA * x + acc, g_ref[...], b_ref[...])


def _lru_weights(w_in, b_in, conv_w, conv_b, w_a, b_a, w_i, b_i, lam, w_o):
    gsz = 4
    ng = LRU_BLOCKS // gsz
    bw = D // LRU_BLOCKS

    def grouped(w):
        w4 = w.reshape(ng, gsz, bw, bw)
        return jnp.einsum('gaij,ab->gaibj', w4, jnp.eye(gsz, dtype=w.dtype)).reshape(ng, gsz * bw, gsz * bw).astype(BF16)

    row = lambda v: v.reshape(1, -1)
    return (w_in.astype(BF16), row(b_in), conv_w, row(conv_b), grouped(w_a), row(b_a), grouped(w_i), row(b_i),
            row(lam), w_o.astype(BF16))


def _lru_prompt(x, wts, g, b, n_seq, tm):
    t = x.shape[0]
    nb = t // n_seq // tm
    row = lambda n, i: (n * nb + i, 0)
    last = pl.BlockSpec((1, SUBLANES, D), lambda n, i: (n, 0, 0))
    w_in, b_in, cw, cb, wa, ba, wi, bi, lam, wo = wts
    return pl.pallas_call(
        functools.partial(_lru_prompt_kernel, tm=tm), grid=(n_seq, nb),
        in_specs=[pl.BlockSpec((tm, D), row), _full(w_in.shape), _full(b_in.shape), _full(cw.shape), _full(cb.shape),
                  _full(wa.shape), _full(ba.shape), _full(wi.shape), _full(bi.shape), _full(lam.shape),
                  _full(wo.shape), _full((1, D)), _full((1, D))],
        out_specs=[pl.BlockSpec((tm, D), row), last, last],
        out_shape=[jax.ShapeDtypeStruct((t, D), F32), jax.ShapeDtypeStruct((n_seq, SUBLANES, D), F32),
                   jax.ShapeDtypeStruct((n_seq, SUBLANES, D), F32)],
        scratch_shapes=[pltpu.VMEM((SUBLANES, D), F32), pltpu.VMEM((SUBLANES, D), F32)],
        compiler_params=_cparams(("parallel", "arbitrary")), name="lru_prompt")(x, *wts, g, b)


def _lru_sample_kernel(x_ref, c0_ref, c1_ref, c2_ref, h0_ref, win_ref, bin_ref, cw_ref, cb_ref, wa_ref, ba_ref,
                       wi_ref, bi_ref, lam_ref, wo_ref, g_ref, b_ref, o_ref, xb_ref, h_ref):
    x = x_ref[...]
    xy = _dot(x.astype(BF16), win_ref[...]) + bin_ref[...]
    xb = xy[:, :D]
    y_gate = _gelu_tanh(xy[:, D:])
    cw = cw_ref[...]
    xc = (cb_ref[...] + c0_ref[...] * cw[0:1] + c1_ref[...] * cw[1:2] + c2_ref[...] * cw[2:3] + xb * cw[3:4])
    a, b = _lru_gates(xc, wa_ref, ba_ref[...], wi_ref, bi_ref[...], lam_ref[...])
    h = a * h0_ref[...] + b
    xb_ref[...] = xb
    h_ref[...] = h
    acc = _dot((h * y_gate).astype(BF16), wo_ref[...])
    o_ref[...] = _ln(ALPHA * x + acc, g_ref[...], b_ref[...])


def _lru_sample(x, conv_state, h0, wts, g, b):
    t = x.shape[0]
    args = (x, conv_state[:, 0], conv_state[:, 1], conv_state[:, 2], h0, *wts, g, b)
    sd = jax.ShapeDtypeStruct((t, D), F32)
    return pl.pallas_call(
        _lru_sample_kernel, grid=(1,),
        in_specs=[_full(a.shape) for a in args],
        out_specs=[_full((t, D))] * 3, out_shape=[sd, sd, sd],
        compiler_params=_cparams(("arbitrary",)), name="lru_sample")(*args)


def _rwkv_pre_kernel(x_ref, xp_ref, mu_ref, wr_ref, wk_ref, wv_ref, w0_ref, w1_ref, w2_ref, a0_ref, a1_ref, a2_ref,
                     g1_ref, g2_ref, r_ref, k_ref, v_ref, a_ref, ld_ref, g_ref, *scratch, tm, seq):
    x = x_ref[...]
    if seq:
        cx_ref, = scratch
        i = pl.program_id(1)

        @pl.when(i == 0)
        def _():
            cx_ref[...] = xp_ref[0]

        x_prev = _shift_rows(jnp.concatenate([cx_ref[...], x], axis=0), 1, tm)
        cx_ref[...] = x[tm - SUBLANES:]
    else:
        x_prev = xp_ref[...]
    xx = x_prev - x
    mu = mu_ref[...]
    mix = lambda j: (x + xx * mu[j:j + 1]).astype(BF16)
    r_ref[...] = _dot(mix(0), wr_ref[...]).astype(r_ref.dtype)
    wl = _dot(jnp.tanh(_dot(mix(1), w1_ref[...])).astype(BF16), w2_ref[...])
    w = -_softplus(-(w0_ref[...] + wl)) - 0.5
    ld_ref[...] = -jnp.exp(w)
    k_ref[...] = _dot(mix(2), wk_ref[...]).astype(k_ref.dtype)
    v_ref[...] = _dot(mix(3), wv_ref[...]).astype(v_ref.dtype)
    al = _dot(_dot(mix(4), a1_ref[...]).astype(BF16), a2_ref[...])
    a_ref[...] = _sigmoid(a0_ref[...] + al).astype(a_ref.dtype)
    g_ref[...] = _dot(_sigmoid(_dot(mix(5), g1_ref[...])).astype(BF16), g2_ref[...]).astype(g_ref.dtype)


def _rwkv_pre(x, x_prev, wts, n_seq, tm, seq, dtype):
    t = x.shape[0]
    nb = t // n_seq // tm
    row = lambda n, i: (n * nb + i, 0)
    xp_spec = pl.BlockSpec((1, SUBLANES, D), lambda n, i: (n, 0, 0)) if seq else pl.BlockSpec((tm, D), row)
    sd = lambda dt: jax.ShapeDtypeStruct((t, D), dt)
    blk = pl.BlockSpec((tm, D), row)
    return pl.pallas_call(
        functools.partial(_rwkv_pre_kernel, tm=tm, seq=seq), grid=(n_seq, nb),
        in_specs=[blk, xp_spec] + [_full(w.shape) for w in wts],
        out_specs=[blk] * 6,
        out_shape=[sd(dtype), sd(dtype), sd(dtype), sd(dtype), sd(F32), sd(dtype)],
        scratch_shapes=[pltpu.VMEM((SUBLANES, D), F32)] if seq else [],
        compiler_params=_cparams(("parallel", "arbitrary")), name="rwkv_pre")(x, x_prev, *wts)


def _seg_sum(x, first):
    s0 = jnp.sum(jnp.where(first, x, 0.0), axis=-1, keepdims=True)
    s1 = jnp.sum(jnp.where(first, 0.0, x), axis=-1, keepdims=True)
    return jnp.where(first, s0, s1)


def _wkv_kernel(r_ref, k_ref, v_ref, a_ref, ld_ref, g_ref, kk_ref, ka_ref, rk_ref, gg_ref, gb_ref,
                o_ref, s_ref, st_ref):
    c = pl.program_id(1)
    L = WKV_CHUNK
    P2 = 2 * L

    @pl.when(c == 0)
    def _():
        st_ref[...] = jnp.zeros_like(st_ref)

    ld_all = ld_ref[...]
    tri = (lax.broadcasted_iota(jnp.int32, (L, L), 0) >= lax.broadcasted_iota(jnp.int32, (L, L), 1)).astype(BF16)
    hi = ld_all.astype(BF16)
    r1 = ld_all - hi.astype(F32)
    mid = r1.astype(BF16)
    lo = (r1 - mid.astype(F32)).astype(BF16)
    cum_all = _dot(tri, hi) + _dot(tri, mid) + _dot(tri, lo)

    lane = lax.broadcasted_iota(jnp.int32, (1, LANES), 1)
    first = lane < RWKV_HD
    ri = lax.broadcasted_iota(jnp.int32, (P2, P2), 0)
    ci = lax.broadcasted_iota(jnp.int32, (P2, P2), 1)
    same_head = (ri // L) == (ci // L)
    rt, ct = ri % L, ci % L
    strict = jnp.logical_and(same_head, rt > ct)
    incl = jnp.logical_and(same_head, rt >= ct)
    eye = ri == ci

    def stack(xv):
        return jnp.concatenate([jnp.where(first, xv, 0.0), jnp.where(first, 0.0, xv)], axis=0).astype(BF16)

    pairs = range(RWKV_HEADS // 2)
    sls = [slice(p * LANES, (p + 1) * LANES) for p in pairs]
    ws, us, ks, rs, ul, kl, vs, g_l, bonus = ([] for _ in range(9))
    for sl in sls:
        rp, kp, vp, ap = (ref[:, sl].astype(F32) for ref in (r_ref, k_ref, v_ref, a_ref))
        ldp, cum = ld_all[:, sl], cum_all[:, sl]
        kk = kp * kk_ref[:, sl]
        kk = kk / jnp.maximum(jnp.sqrt(_seg_sum(kk * kk, first)), 1e-12)
        kmod = kp * (1.0 + (ap - 1.0) * ka_ref[:, sl])
        bp = kk * ap
        cum_l = cum[L - 1:L, :]
        g_inv = jnp.exp(-cum)
        g_to_end = jnp.exp(cum_l - cum)
        ws.append(stack(kk * jnp.exp(cum - ldp)))
        us.append(stack(bp * g_inv))
        ks.append(stack(kmod * g_inv))
        rs.append(stack(rp * jnp.exp(cum)))
        ul.append(stack(bp * g_to_end))
        kl.append(stack(kmod * g_to_end))
        vs.append(stack(vp))
        g_l.append(jnp.exp(cum_l))
        bonus.append(_seg_sum(rp * kmod * rk_ref[:, sl], first) * vp)

    gram = [_dot_nt(jnp.concatenate([ws[p], rs[p]], axis=0), jnp.concatenate([us[p], ks[p]], axis=0)) for p in pairs]
    n_mat = [jnp.where(strict, gram[p][:P2, :P2], 0.0) for p in pairs]
    m_mat = [jnp.where(strict, gram[p][:P2, P2:], 0.0).astype(BF16) for p in pairs]
    nr_mat = [jnp.where(incl, gram[p][P2:, :P2], 0.0).astype(BF16) for p in pairs]
    mr_mat = [jnp.where(incl, gram[p][P2:, P2:], 0.0).astype(BF16) for p in pairs]

    def level_mask(sz):
        sub = jnp.logical_and((rt // sz) % 2 == 1, (ct // sz) % 2 == 0)
        return jnp.logical_and(jnp.logical_and(sub, (rt // (2 * sz)) == (ct // (2 * sz))), same_head)

    x_inv = [jnp.where(eye, 1.0, 0.0) - jnp.where(level_mask(1), n_mat[p], 0.0) for p in pairs]
    sz = 2
    while sz < L:
        mask = level_mask(sz)
        xb = [x_inv[p].astype(BF16) for p in pairs]
        xc = [_dot(xb[p], jnp.where(mask, n_mat[p], 0.0).astype(BF16)).astype(BF16) for p in pairs]
        x_inv = [x_inv[p] - _dot(xc[p], xb[p]) for p in pairs]
        sz *= 2

    a0 = [st_ref[p] for p in pairs]
    a0b = [a0[p].astype(BF16) for p in pairs]
    rhs = [_dot(jnp.concatenate([ws[p], m_mat[p]], axis=1), jnp.concatenate([a0b[p], vs[p]], axis=0)).astype(BF16)
           for p in pairs]
    pm = [(-_dot(x_inv[p].astype(BF16), rhs[p])).astype(BF16) for p in pairs]
    o_st = [_dot(jnp.concatenate([rs[p], nr_mat[p], mr_mat[p]], axis=1),
                 jnp.concatenate([a0b[p], pm[p], vs[p]], axis=0)) for p in pairs]
    for p in pairs:
        g_col = jnp.sum(jnp.where(eye, jnp.broadcast_to(g_l[p], (P2, P2)), 0.0), axis=-1, keepdims=True)
        st_ref[p] = g_col * a0[p] + _dot_tn(jnp.concatenate([ul[p], kl[p]], axis=0),
                                            jnp.concatenate([pm[p], vs[p]], axis=0))

    inv_n = 1.0 / RWKV_HD
    for p, sl in zip(pairs, sls):
        o = o_st[p][:L] + o_st[p][L:]
        mu = _seg_sum(o, first) * inv_n
        oc = o - mu
        var = _seg_sum(oc * oc, first) * inv_n
        on = oc * lax.rsqrt(var + RWKV_GN_EPS) * gg_ref[:, sl] + gb_ref[:, sl]
        o_ref[:, sl] = ((on + bonus[p]) * g_ref[:, sl].astype(F32)).astype(o_ref.dtype)

    s_ref[0] = st_ref[...]


def _wkv_prompt(r, k, v, a, ld, g, hp, n_seq):
    t = r.shape[0]
    L = WKV_CHUNK
    nc = t // n_seq // L
    row = lambda n, c: (n * nc + c, 0)
    blk = pl.BlockSpec((L, D), row)
    npair = RWKV_HEADS // 2
    return pl.pallas_call(
        _wkv_kernel, grid=(n_seq, nc),
        in_specs=[blk] * 6 + [_full((1, D))] * 5,
        out_specs=[blk, pl.BlockSpec((1, npair, LANES, LANES), lambda n, c: (n, 0, 0, 0))],
        out_shape=[jax.ShapeDtypeStruct((t, D), BF16), jax.ShapeDtypeStruct((n_seq, npair, LANES, LANES), F32)],
        scratch_shapes=[pltpu.VMEM((npair, LANES, LANES), F32)],
        compiler_params=_cparams(("parallel", "arbitrary")), name="wkv_chunk")(r, k, v, a, ld, g, *hp)


def _wkv_sample_kernel(r_ref, k_ref, v_ref, a_ref, ld_ref, g_ref, s_ref, kk_ref, ka_ref, rk_ref, gg_ref, gb_ref,
                       o_ref, so_ref):
    hd = RWKV_HD
    eye = lax.broadcasted_iota(jnp.int32, (1, hd, hd), 1) == lax.broadcasted_iota(jnp.int32, (1, hd, hd), 2)
    for h in range(RWKV_HEADS):
        hs = slice(h, h + 1)
        r, k, v, a, ld, g = (ref[:, hs, :] for ref in (r_ref, k_ref, v_ref, a_ref, ld_ref, g_ref))
        s = s_ref[:, h]
        kk = k * kk_ref[hs, :]
        kk = kk / jnp.maximum(jnp.sqrt(jnp.sum(kk * kk, axis=-1, keepdims=True)), 1e-12)
        kmod = k * (1.0 + (a - 1.0) * ka_ref[hs, :])
        skk = jnp.sum(s * kk, axis=-1, keepdims=True)
        v_col = jnp.sum(jnp.where(eye, v, 0.0), axis=-1, keepdims=True)
        s_new = s * jnp.exp(ld) - skk * (kk * a) + v_col * kmod
        so_ref[:, h] = s_new
        o_col = jnp.sum(s_new * r, axis=-1, keepdims=True)
        o = jnp.sum(jnp.where(eye, o_col, 0.0), axis=1, keepdims=True)
        mu = jnp.mean(o, axis=-1, keepdims=True)
        oc = o - mu
        var = jnp.mean(oc * oc, axis=-1, keepdims=True)
        on = oc * lax.rsqrt(var + RWKV_GN_EPS) * gg_ref[hs, :] + gb_ref[hs, :]
        bonus = jnp.sum(r * kmod * rk_ref[hs, :], axis=-1, keepdims=True) * v
        o_ref[:, hs, :] = (on + bonus) * g


def _wkv_sample(r, k, v, a, ld, g, state, hp, bs):
    b = r.shape[0]
    h3 = lambda z: z.reshape(b, RWKV_HEADS, RWKV_HD)
    blk = pl.BlockSpec((bs, RWKV_HEADS, RWKV_HD), lambda i: (i, 0, 0))
    sblk = pl.BlockSpec((bs, RWKV_HEADS, RWKV_HD, RWKV_HD), lambda i: (i, 0, 0, 0))
    hp3 = [z.reshape(RWKV_HEADS, RWKV_HD) for z in hp]
    o, s_new = pl.pallas_call(
        _wkv_sample_kernel, grid=(b // bs,),
        in_specs=[blk] * 6 + [sblk] + [_full((RWKV_HEADS, RWKV_HD))] * 5,
        out_specs=[blk, sblk],
        out_shape=[jax.ShapeDtypeStruct((b, RWKV_HEADS, RWKV_HD), F32), jax.ShapeDtypeStruct(state.shape, F32)],
        compiler_params=_cparams(("parallel",)), name="wkv_sample")(
            h3(r), h3(k), h3(v), h3(a), h3(ld), h3(g), state, *hp3)
    return o.reshape(b, D), s_new


def _mem_prompt_kernel(x_ref, wq_ref, mk_ref, mv_ref, wo_ref, g_ref, b_ref, o_ref):
    x = x_ref[...]
    q = _dot(x.astype(BF16), wq_ref[...]).astype(BF16)
    scale = MEM_HD ** -0.5
    outs = []
    for h in range(MEM_HEADS):
        sl = slice(h * MEM_HD, (h + 1) * MEM_HD)
        s = _dot_nt(q[:, sl], mk_ref[0, :, sl]) * scale
        p = jnp.exp(s - jnp.max(s, axis=-1, keepdims=True))
        den = jnp.sum(p, axis=-1, keepdims=True)
        outs.append(_dot((p / den).astype(BF16), mv_ref[0, :, sl]).astype(BF16))
    acc = _dot(jnp.concatenate(outs, axis=-1), wo_ref[...])
    o_ref[...] = _ln(ALPHA * x + acc, g_ref[...], b_ref[...])


def _mem_attn_prompt(x, w_q, mk, mv, w_o, g, b, n_seq, tm):
    t = x.shape[0]
    nb = t // n_seq // tm
    m = mk.shape[1]
    row = lambda n, i: (n * nb + i, 0)
    mem = pl.BlockSpec((1, m, D), lambda n, i: (n, 0, 0))
    return pl.pallas_call(
        _mem_prompt_kernel, grid=(n_seq, nb),
        in_specs=[pl.BlockSpec((tm, D), row), _full((D, D)), mem, mem, _full((D, D)), _full((1, D)), _full((1, D))],
        out_specs=pl.BlockSpec((tm, D), row), out_shape=jax.ShapeDtypeStruct((t, D), F32),
        compiler_params=_cparams(("parallel", "arbitrary")), name="mem_attn")(x, w_q, mk, mv, w_o, g, b)


def _mem_sample_kernel(q_ref, ck_ref, cv_ref, o_ref, *, bs):
    scale = MEM_HD ** -0.5
    for b in range(bs):
        s = jnp.sum(_round_bf16(ck_ref[b]) * (_round_bf16(q_ref[b]) * scale), axis=-1, keepdims=True)
        p = jnp.exp(s - jnp.max(s, axis=0, keepdims=True))
        p = _round_bf16(p / jnp.sum(p, axis=0, keepdims=True))
        o_ref[b] = jnp.sum(p * _round_bf16(cv_ref[b]), axis=0, keepdims=True)


def _mem_attn_sample(q, cache_k, cache_v, layer, bs):
    _, b, m, nh, hd = cache_k.shape
    qb = pl.BlockSpec((bs, 1, nh, hd), lambda i: (i, 0, 0, 0))
    cb = pl.BlockSpec((None, bs, m, nh, hd), lambda i: (layer, i, 0, 0, 0))
    out = pl.pallas_call(
        functools.partial(_mem_sample_kernel, bs=bs), grid=(b // bs,), in_specs=[qb, cb, cb], out_specs=qb,
        out_shape=jax.ShapeDtypeStruct((b, 1, nh, hd), F32),
        compiler_params=_cparams(("parallel",)), name="mem_sample")(q.reshape(b, 1, nh, hd), cache_k, cache_v)
    return out.reshape(b, D)


_PAIRS = ((0, 1), (0, 2), (0, 3), (1, 2), (1, 3), (2, 3))


def _router_kernel(x_ref, rw_ref, rb_ref, bucket_ref, rank_ref, cnt_ref, base_ref, *, tm):
    i = pl.program_id(0)

    @pl.when(i == 0)
    def _():
        base_ref[...] = jnp.zeros_like(base_ref)

    logits = _dot_nt(rw_ref[...], x_ref[...].astype(BF16))
    e = jnp.exp(logits - jnp.max(logits, axis=0, keepdims=True))
    sel = e / jnp.sum(e, axis=0, keepdims=True) + rb_ref[...]
    s = [sel[j:j + 1, :] for j in range(N_EXPERTS)]
    neg = jnp.float32(-jnp.inf)

    best = jnp.zeros((1, tm), jnp.int32)
    best_score = None
    for gi in range(N_GROUPS):
        s0, s1, s2, s3 = s[4 * gi:4 * gi + 4]
        hi01, lo01, hi23, lo23 = jnp.maximum(s0, s1), jnp.minimum(s0, s1), jnp.maximum(s2, s3), jnp.minimum(s2, s3)
        score = jnp.maximum(hi01, hi23) + jnp.maximum(jnp.minimum(hi01, hi23), jnp.maximum(lo01, lo23))
        if gi == 0:
            best_score = score
        else:
            take = score > best_score
            best = jnp.where(take, gi, best)
            best_score = jnp.where(take, score, best_score)
    vals = []
    for j in range(EXPERTS_PER_GROUP):
        vj = s[j]
        for gi in range(1, N_GROUPS):
            vj = jnp.where(best == gi, s[4 * gi + j], vj)
        vals.append(vj)

    def argmax4(v):
        idx, mx = jnp.zeros((1, tm), jnp.int32), v[0]
        for j in range(1, EXPERTS_PER_GROUP):
            take = v[j] > mx
            idx = jnp.where(take, j, idx)
            mx = jnp.where(take, v[j], mx)
        return idx

    i1 = argmax4(vals)
    i2 = argmax4([jnp.where(i1 == j, neg, vals[j]) for j in range(EXPERTS_PER_GROUP)])
    lo, hi = jnp.minimum(i1, i2), jnp.maximum(i1, i2)
    pair = jnp.zeros((1, tm), jnp.int32)
    for pi, (pa, pb) in enumerate(_PAIRS):
        pair = jnp.where(jnp.logical_and(lo == pa, hi == pb), pi, pair)
    bucket = best * len(_PAIRS) + pair
    bucket_ref[0] = bucket

    onehot = (lax.broadcasted_iota(jnp.int32, (BUCKET_ROWS, tm), 0) == bucket).astype(F32)
    upper = (lax.broadcasted_iota(jnp.int32, (tm, tm), 0) <= lax.broadcasted_iota(jnp.int32, (tm, tm), 1)).astype(BF16)
    cum = _dot(onehot.astype(BF16), upper)
    base = base_ref[...]
    rank = jnp.sum(onehot * (cum + base), axis=0, keepdims=True) - 1.0
    rank_ref[0] = rank.astype(jnp.int32)
    base = base + jnp.sum(onehot, axis=1, keepdims=True)
    base_ref[...] = base
    cnt_ref[...] = jnp.broadcast_to(base, cnt_ref.shape)


def _router(x, rw_t, rb, tm):
    t = x.shape[0]
    nb = t // tm
    ib = pl.BlockSpec((1, 1, tm), lambda i: (i, 0, 0))
    bucket, rank, cnt = pl.pallas_call(
        functools.partial(_router_kernel, tm=tm), grid=(nb,),
        in_specs=[pl.BlockSpec((tm, D), lambda i: (i, 0)), _full(rw_t.shape), _full(rb.shape)],
        out_specs=[ib, ib, _full((BUCKET_ROWS, LANES))],
        out_shape=[jax.ShapeDtypeStruct((nb, 1, tm), jnp.int32), jax.ShapeDtypeStruct((nb, 1, tm), jnp.int32),
                   jax.ShapeDtypeStruct((BUCKET_ROWS, LANES), F32)],
        scratch_shapes=[pltpu.VMEM((BUCKET_ROWS, 1), F32)],
        compiler_params=_cparams(("arbitrary",)), name="router")(x, rw_t, rb)
    return bucket.reshape(t), rank.reshape(t), cnt[:N_BUCKETS, 0].astype(jnp.int32)


def _row_copies(idx_ref, base, src_hbm, dst, sem, n, wait):
    def body(r, carry):
        cp = pltpu.make_async_copy(src_hbm.at[pl.ds(idx_ref[base + r], 1)], dst.at[pl.ds(r, 1)], sem)
        if wait:
            cp.wait()
        else:
            cp.start()
        return carry

    lax.fori_loop(0, n, body, 0, unroll=8)


def _ffn_kernel(src_ref, lo_ref, hi_ref, used_ref, x_hbm, rw_ref, g0_ref, u0_ref, d0_ref, g1_ref, u1_ref, d1_ref,
                o_ref, xbuf, sem, *, blk):
    i = pl.program_id(0)
    used = used_ref[0]
    slot = i % 2

    @pl.when(jnp.logical_and(i == 0, used > 0))
    def _():
        _row_copies(src_ref, 0, x_hbm, xbuf.at[0], sem.at[0], blk, False)

    @pl.when(i + 1 < used)
    def _():
        _row_copies(src_ref, (i + 1) * blk, x_hbm, xbuf.at[1 - slot], sem.at[1 - slot], blk, False)

    @pl.when(i < used)
    def _():
        _row_copies(src_ref, i * blk, x_hbm, xbuf.at[slot], sem.at[slot], blk, True)
        xb = xbuf[slot].astype(BF16)
        logits = _dot(xb, rw_ref[...])
        lane = lax.broadcasted_iota(jnp.int32, logits.shape, 1)
        l_lo = jnp.sum(jnp.where(lane == lo_ref[i], logits, 0.0), axis=-1, keepdims=True)
        l_hi = jnp.sum(jnp.where(lane == hi_ref[i], logits, 0.0), axis=-1, keepdims=True)
        w_lo = _sigmoid(l_lo - l_hi)

        def expert(g_ref, u_ref, d_ref):
            gate = _dot(xb, g_ref[0])
            act = gate * _sigmoid(gate) * _dot(xb, u_ref[0])
            return _dot(act.astype(BF16), d_ref[0])

        y_lo = expert(g0_ref, u0_ref, d0_ref)
        y_hi = expert(g1_ref, u1_ref, d1_ref)
        y = w_lo * y_lo + (1.0 - w_lo) * y_hi
        for c in range(D // LANES):
            o_ref[:, c, :] = y[:, c * LANES:(c + 1) * LANES]

    @pl.when(i >= used)
    def _():
        o_ref[...] = jnp.zeros_like(o_ref)


def _ffn(x, src, blk_lo, blk_hi, n_used, rw, w_gate, w_up, w_down, layer, blk):
    rows = src.shape[0]
    nblk = rows // blk
    wg = lambda sel: pl.BlockSpec((None, 1, D, EXPERT_FF),
                                  lambda i, s, lo, hi, used: (layer, (lo, hi)[sel][i], 0, 0))
    wd = lambda sel: pl.BlockSpec((None, 1, EXPERT_FF, D),
                                  lambda i, s, lo, hi, used: (layer, (lo, hi)[sel][i], 0, 0))
    return pl.pallas_call(
        functools.partial(_ffn_kernel, blk=blk),
        grid_spec=pltpu.PrefetchScalarGridSpec(
            num_scalar_prefetch=4, grid=(nblk,),
            in_specs=[pl.BlockSpec(memory_space=pl.ANY), pl.BlockSpec(rw.shape, lambda i, s, lo, hi, used: (0, 0)),
                      wg(0), wg(0), wd(0), wg(1), wg(1), wd(1)],
            out_specs=pl.BlockSpec((blk, D // LANES, LANES), lambda i, s, lo, hi, used: (i, 0, 0)),
            scratch_shapes=[pltpu.VMEM((2, blk, D), F32), pltpu.SemaphoreType.DMA((2,))]),
        out_shape=jax.ShapeDtypeStruct((rows, D // LANES, LANES), F32),
        compiler_params=_cparams(("arbitrary",)), name="moe_ffn")(
            src, blk_lo, blk_hi, n_used, x, rw, w_gate, w_up, w_down, w_gate, w_up, w_down)


def _combine_ln_kernel(dest_ref, x_ref, y_hbm, g_ref, b_ref, o_ref, ybuf, sem, *, tm):
    i = pl.program_id(0)
    slot = i % 2

    @pl.when(i == 0)
    def _():
        _row_copies(dest_ref, 0, y_hbm, ybuf.at[0], sem.at[0], tm, False)

    @pl.when(i + 1 < pl.num_programs(0))
    def _():
        _row_copies(dest_ref, (i + 1) * tm, y_hbm, ybuf.at[1 - slot], sem.at[1 - slot], tm, False)

    _row_copies(dest_ref, i * tm, y_hbm, ybuf.at[slot], sem.at[slot], tm, True)
    y = jnp.concatenate([ybuf[slot, :, c, :] for c in range(D // LANES)], axis=-1)
    o_ref[...] = _ln(ALPHA * x_ref[...] + y, g_ref[...], b_ref[...])


def _combine_ln(x, y_rows, dest, g, b, tm):
    t = x.shape[0]
    rowb = pl.BlockSpec((tm, D), lambda i, d: (i, 0))
    vec = pl.BlockSpec((1, D), lambda i, d: (0, 0))
    return pl.pallas_call(
        functools.partial(_combine_ln_kernel, tm=tm),
        grid_spec=pltpu.PrefetchScalarGridSpec(
            num_scalar_prefetch=1, grid=(t // tm,),
            in_specs=[rowb, pl.BlockSpec(memory_space=pl.ANY), vec, vec], out_specs=rowb,
            scratch_shapes=[pltpu.VMEM((2, tm, D // LANES, LANES), F32), pltpu.SemaphoreType.DMA((2,))]),
        out_shape=jax.ShapeDtypeStruct((t, D), F32),
        compiler_params=_cparams(("arbitrary",)), name="moe_combine_ln")(dest, x, y_rows, g, b)


def _moe_ln(x, rw_t, rb, rw_pad, w_gate, w_up, w_down, layer, g, b, tm_router, blk, tm_comb):
    t = x.shape[0]
    bucket, rank, counts = _router(x, rw_t, rb, tm_router)
    padded = (counts + blk - 1) // blk * blk
    ends = jnp.cumsum(padded)
    dest = ((ends - padded)[bucket] + rank).astype(jnp.int32)
    nblk = t // blk + N_BUCKETS
    src = jnp.zeros((nblk * blk,), jnp.int32).at[dest].set(jnp.arange(t, dtype=jnp.int32))
    blk_bucket = jnp.minimum(jnp.searchsorted(ends, jnp.arange(nblk) * blk, side='right'), N_BUCKETS - 1)
    pair_lo = jnp.array([p[0] for p in _PAIRS], jnp.int32)
    pair_hi = jnp.array([p[1] for p in _PAIRS], jnp.int32)
    grp, pr = blk_bucket // len(_PAIRS), blk_bucket % len(_PAIRS)
    blk_lo = (grp * EXPERTS_PER_GROUP + pair_lo[pr]).astype(jnp.int32)
    blk_hi = (grp * EXPERTS_PER_GROUP + pair_hi[pr]).astype(jnp.int32)
    n_used = (ends[-1:] // blk).astype(jnp.int32)
    y_rows = _ffn(x, src, blk_lo, blk_hi, n_used, rw_pad, w_gate, w_up, w_down, layer, blk)
    return _combine_ln(x, y_rows, dest, g, b, tm_comb)


def _moe_dense_kernel(x_ref, lo_ref, hi_ref, rw_ref, wg_ref, wu_ref, wd_ref, g_ref, b_ref, o_ref, acc_ref):
    e = pl.program_id(0)

    @pl.when(e == 0)
    def _():
        acc_ref[...] = jnp.zeros_like(acc_ref)

    x = x_ref[...]
    xb = x.astype(BF16)
    logits = _dot(xb, rw_ref[...])
    lane = lax.broadcasted_iota(jnp.int32, logits.shape, 1)
    lo, hi = lo_ref[...], hi_ref[...]
    l_lo = jnp.sum(jnp.where(lane == lo, logits, 0.0), axis=-1, keepdims=True)
    l_hi = jnp.sum(jnp.where(lane == hi, logits, 0.0), axis=-1, keepdims=True)
    w_lo = _sigmoid(l_lo - l_hi)
    coef = jnp.where(lo == e, w_lo, 0.0) + jnp.where(hi == e, 1.0 - w_lo, 0.0)
    gate = _dot(xb, wg_ref[0])
    act = gate * _sigmoid(gate) * _dot(xb, wu_ref[0])
    acc_ref[...] += coef * _dot(act.astype(BF16), wd_ref[0])

    @pl.when(e == pl.num_programs(0) - 1)
    def _():
        o_ref[...] = _ln(ALPHA * x + acc_ref[...], g_ref[...], b_ref[...])


def _moe_ln_dense(x, rw_t, rb, rw_pad, w_gate, w_up, w_down, layer, g, b):
    t = x.shape[0]
    bucket, _, _ = _router(x, rw_t, rb, t)
    pair_lo = jnp.array([p[0] for p in _PAIRS], jnp.int32)
    pair_hi = jnp.array([p[1] for p in _PAIRS], jnp.int32)
    grp, pr = bucket // len(_PAIRS), bucket % len(_PAIRS)
    lo = (grp * EXPERTS_PER_GROUP + pair_lo[pr]).astype(jnp.int32).reshape(t, 1)
    hi = (grp * EXPERTS_PER_GROUP + pair_hi[pr]).astype(jnp.int32).reshape(t, 1)
    wg = pl.BlockSpec((None, 1, D, EXPERT_FF), lambda e: (layer, e, 0, 0))
    wd = pl.BlockSpec((None, 1, EXPERT_FF, D), lambda e: (layer, e, 0, 0))
    return pl.pallas_call(
        _moe_dense_kernel, grid=(N_EXPERTS,),
        in_specs=[_full((t, D)), _full((t, 1)), _full((t, 1)), _full(rw_pad.shape), wg, wg, wd,
                  _full((1, D)), _full((1, D))],
        out_specs=_full((t, D)), out_shape=jax.ShapeDtypeStruct((t, D), F32),
        scratch_shapes=[pltpu.VMEM((t, D), F32)],
        compiler_params=_cparams(("arbitrary",)), name="moe_dense")(x, lo, hi, rw_pad, w_gate, w_up, w_down, g, b)


def kernel(x_prompt, x_sample, cache_swa_k, cache_swa_v, state_lru_conv, state_lru_h, state_rwkv_shift, state_rwkv_wkv, cache_mem_k, cache_mem_v, mem_prompt, swa_w_qkv, swa_sinks, swa_w_o, lru_w_in, lru_b_in, lru_conv_w, lru_conv_b, lru_w_a, lru_b_a, lru_w_i, lru_b_i, lru_lambda, lru_w_o, rwkv_mu, rwkv_w_r, rwkv_w_k, rwkv_w_v, rwkv_w0, rwkv_w1, rwkv_w2, rwkv_a0, rwkv_a1, rwkv_a2, rwkv_g1, rwkv_g2, rwkv_k_k, rwkv_k_a, rwkv_r_k, rwkv_gn_g, rwkv_gn_b, rwkv_w_o, mem_w_q, mem_w_kv, mem_w_o, ln_g, ln_b, router_w, router_b, moe_w_gate, moe_w_up, moe_w_down):
    n_p, seq, _ = x_prompt.shape
    n_s, dec_seq, _ = x_sample.shape
    assert dec_seq == 1
    past_len = 8192
    xp = x_prompt.reshape(n_p * seq, D)
    xs = x_sample.reshape(n_s, D)
    row = lambda v: v.reshape(1, -1)
    bf = lambda w: w.astype(BF16)

    rw_t = bf(router_w.T)
    rb = router_b.reshape(N_EXPERTS, 1)
    rw_pad = bf(jnp.pad(router_w, ((0, 0), (0, LANES - N_EXPERTS))))
    wg, wu, wd = bf(moe_w_gate), bf(moe_w_up), bf(moe_w_down)
    mem_p = mem_prompt.reshape(n_p * mem_prompt.shape[1], D)
    m_len = mem_prompt.shape[1]

    swa_k_p, swa_v_p, swa_k_s, swa_v_s = [], [], [], []
    lru_c_p, lru_h_p, lru_c_s, lru_h_s = [], [], [], []
    rw_x_p, rw_s_p, rw_x_s, rw_s_s = [], [], [], []
    mem_k_p, mem_v_p = [], []

    for layer in range(DEPTH):
        kind, i = layer % N_MIXERS, layer // N_MIXERS
        g0, b0 = row(ln_g[layer, 0]), row(ln_b[layer, 0])
        if kind == 0:
            w_qkv, w_o = bf(swa_w_qkv[i]), bf(swa_w_o[i])
            keep = min(WINDOW, seq)
            q, k, v, kv_last = _swa_qkv(xp, w_qkv, jnp.arange(seq), n_p, 512, keep, BF16)
            o = _swa_attn_prompt(q, k, v, swa_sinks[i], n_p, 2)
            swa_k_p.append(kv_last[:, :, :KV_WIDTH].reshape(n_p, keep, SWA_KV_HEADS, HEAD_DIM))
            swa_v_p.append(kv_last[:, :, KV_WIDTH:].reshape(n_p, keep, SWA_KV_HEADS, HEAD_DIM))
            xp = _proj_ln(o, w_o, xp, g0, b0, 512)

            qs, _, _, kv_new = _swa_qkv(xs, w_qkv, jnp.full((n_s,), past_len), 1, n_s, n_s, F32)
            kn, vn = kv_new[0, :, :KV_WIDTH], kv_new[0, :, KV_WIDTH:]
            os_ = _swa_attn_sample(qs, kn, vn, cache_swa_k[i], cache_swa_v[i], swa_sinks[i], 8)
            wb = cache_swa_k.shape[2]
            k_all = jnp.concatenate([cache_swa_k[i], kn.reshape(n_s, 1, SWA_KV_HEADS, HEAD_DIM)], axis=1)
            v_all = jnp.concatenate([cache_swa_v[i], vn.reshape(n_s, 1, SWA_KV_HEADS, HEAD_DIM)], axis=1)
            swa_k_s.append(k_all[:, -wb:])
            swa_v_s.append(v_all[:, -wb:])
            xs = _proj_ln(os_, w_o, xs, g0, b0, n_s)
        elif kind == 1:
            wts = _lru_weights(lru_w_in[i], lru_b_in[i], lru_conv_w[i], lru_conv_b[i], lru_w_a[i], lru_b_a[i],
                               lru_w_i[i], lru_b_i[i], lru_lambda[i], lru_w_o[i])
            xp, conv_last, h_last = _lru_prompt(xp, wts, g0, b0, n_p, 256)
            lru_c_p.append(conv_last[:, SUBLANES - (CONV_W - 1):])
            lru_h_p.append(h_last[:, SUBLANES - 1])
            xs, xb_s, h_s = _lru_sample(xs, state_lru_conv[i], state_lru_h[i], wts, g0, b0)
            lru_c_s.append(jnp.concatenate([state_lru_conv[i][:, 1:], xb_s[:, None]], axis=1))
            lru_h_s.append(h_s)
        else:
            wts = (rwkv_mu[i], bf(rwkv_w_r[i]), bf(rwkv_w_k[i]), bf(rwkv_w_v[i]), row(rwkv_w0[i]), bf(rwkv_w1[i]),
                   bf(rwkv_w2[i]), row(rwkv_a0[i]), bf(rwkv_a1[i]), bf(rwkv_a2[i]), bf(rwkv_g1[i]), bf(rwkv_g2[i]))
            hp = (row(rwkv_k_k[i]), row(rwkv_k_a[i]), row(rwkv_r_k[i]), row(rwkv_gn_g[i]), row(rwkv_gn_b[i]))
            w_o = bf(rwkv_w_o[i])
            rw_x_p.append(xp.reshape(n_p, seq, D)[:, -1])
            rw_x_s.append(xs)
            r, k, v, a, ld, g = _rwkv_pre(xp, jnp.zeros((n_p, SUBLANES, D), F32), wts, n_p, 256, True, BF16)
            o, st = _wkv_prompt(r, k, v, a, ld, g, hp, n_p)
            hd = RWKV_HD
            st = jnp.stack([st[:, :, :hd, :hd], st[:, :, hd:, hd:]], axis=2).reshape(n_p, RWKV_HEADS, hd, hd)
            rw_s_p.append(jnp.swapaxes(st, -1, -2))
            xp = _proj_ln(o, w_o, xp, g0, b0, 512)

            r, k, v, a, ld, g = _rwkv_pre(xs, state_rwkv_shift[i], wts, 1, n_s, False, F32)
            os_, s_new = _wkv_sample(r, k, v, a, ld, g, state_rwkv_wkv[i], hp, 8)
            rw_s_s.append(s_new)
            xs = _proj_ln(os_, w_o, xs, g0, b0, n_s)

        g1, b1 = row(ln_g[layer, 1]), row(ln_b[layer, 1])
        w_q, w_o = bf(mem_w_q[layer]), bf(mem_w_o[layer])
        mkv = _matmul(mem_p, bf(mem_w_kv[layer]), 512)
        mk, mv = mkv[:, :D], mkv[:, D:]
        mem_k_p.append(mk.reshape(n_p, m_len, MEM_HEADS, MEM_HD))
        mem_v_p.append(mv.reshape(n_p, m_len, MEM_HEADS, MEM_HD))
        xp = _mem_attn_prompt(xp, w_q, bf(mk).reshape(n_p, m_len, D), bf(mv).reshape(n_p, m_len, D), w_o, g1, b1,
                              n_p, 512)
        qs = _matmul(xs, w_q, n_s)
        os_ = _mem_attn_sample(qs, cache_mem_k, cache_mem_v, layer, 4)
        xs = _proj_ln(os_, w_o, xs, g1, b1, n_s)

        g2, b2 = row(ln_g[layer, 2]), row(ln_b[layer, 2])
        xp = _moe_ln(xp, rw_t, rb, rw_pad, wg, wu, wd, layer, g2, b2, 512, 256, 256)
        xs = _moe_ln_dense(xs, rw_t, rb, rw_pad, wg, wu, wd, layer, g2, b2)

    return (xp.reshape(n_p, seq, D), xs.reshape(n_s, 1, D),
            jnp.stack(swa_k_p), jnp.stack(swa_v_p), jnp.stack(lru_c_p), jnp.stack(lru_h_p),
            jnp.stack(rw_x_p), jnp.stack(rw_s_p), jnp.stack(mem_k_p), jnp.stack(mem_v_p),
            jnp.stack(swa_k_s), jnp.stack(swa_v_s), jnp.stack(lru_c_s), jnp.stack(lru_h_s),
            jnp.stack(rw_x_s), jnp.stack(rw_s_s))
```

```python
import functools

import jax
import jax.numpy as jnp
from jax import lax
from jax.experimental import pallas as pl
from jax.experimental.pallas import tpu as pltpu

F32 = jnp.float32
BF16 = jnp.bfloat16

D = 1024
DEPTH = 4
N_MIXERS = 3
HEAD_DIM = 64
SWA_HEADS = D // HEAD_DIM
SWA_KV_HEADS = 4
SWA_GROUP = SWA_HEADS // SWA_KV_HEADS
Q_WIDTH = SWA_HEADS * HEAD_DIM
KV_WIDTH = SWA_KV_HEADS * HEAD_DIM
WINDOW = 128
ROT_DIM = HEAD_DIM // 4
ROPE_THETA = 500000.0
LRU_BLOCKS = 16
CONV_W = 4
LRU_C = 8.0
RWKV_HEADS = 16
RWKV_HD = 64
RWKV_GN_EPS = 64e-5
MEM_HEADS = 4
MEM_HD = D // MEM_HEADS
N_EXPERTS = 16
N_GROUPS = 4
EXPERTS_PER_GROUP = 4
EXPERT_FF = 512
LN_EPS = 1e-5
ALPHA = (2.0 * DEPTH) ** 0.25
NEG_INF = -1e30

LANES = 128
SUBLANES = 8
VMEM_LIMIT = 56 * 1024 * 1024
WKV_CHUNK = 64
N_BUCKETS = N_GROUPS * 6
BUCKET_ROWS = 32


def _cparams(sem):
    return pltpu.CompilerParams(dimension_semantics=sem, vmem_limit_bytes=VMEM_LIMIT)


def _dot(a, b):
    return jnp.dot(a, b, preferred_element_type=F32)


def _dot_nt(a, b):
    return lax.dot_general(a, b, (((1,), (1,)), ((), ())), preferred_element_type=F32)


def _dot_tn(a, b):
    return lax.dot_general(a, b, (((0,), (0,)), ((), ())), preferred_element_type=F32)


def _ln(z, g, b):
    mu = jnp.mean(z, axis=-1, keepdims=True)
    zc = z - mu
    var = jnp.mean(zc * zc, axis=-1, keepdims=True)
    return zc * lax.rsqrt(var + LN_EPS) * g + b


def _softplus(z):
    return jnp.maximum(z, 0.0) + jnp.log1p(jnp.exp(-jnp.abs(z)))


def _sigmoid(z):
    return 1.0 / (1.0 + jnp.exp(-z))


def _round_bf16(x):
    return x.astype(BF16).astype(F32)


def _full(shape):
    nd = len(shape)
    return pl.BlockSpec(shape, lambda *_: (0,) * nd)


def _mm_kernel(a_ref, w_ref, o_ref):
    o_ref[...] = _dot(a_ref[...].astype(BF16), w_ref[...]).astype(o_ref.dtype)


def _matmul(a, w, tm, out_dtype=F32):
    t, k = a.shape
    n = w.shape[1]
    return pl.pallas_call(
        _mm_kernel, grid=(t // tm,),
        in_specs=[pl.BlockSpec((tm, k), lambda i: (i, 0)), _full((k, n))],
        out_specs=pl.BlockSpec((tm, n), lambda i: (i, 0)),
        out_shape=jax.ShapeDtypeStruct((t, n), out_dtype),
        compiler_params=_cparams(("parallel",)), name="matmul")(a, w)


def _proj_ln_kernel(a_ref, w_ref, x_ref, g_ref, b_ref, o_ref):
    acc = _dot(a_ref[...].astype(BF16), w_ref[...])
    o_ref[...] = _ln(ALPHA * x_ref[...] + acc, g_ref[...], b_ref[...])


def _proj_ln(a, w, x, g, b, tm):
    t, k = a.shape
    return pl.pallas_call(
        _proj_ln_kernel, grid=(t // tm,),
        in_specs=[pl.BlockSpec((tm, k), lambda i: (i, 0)), _full((k, D)),
                  pl.BlockSpec((tm, D), lambda i: (i, 0)), _full((1, D)), _full((1, D))],
        out_specs=pl.BlockSpec((tm, D), lambda i: (i, 0)),
        out_shape=jax.ShapeDtypeStruct((t, D), F32),
        compiler_params=_cparams(("parallel",)), name="proj_ln")(a, w, x, g, b)


def _rope_tables(pos):
    half = ROT_DIM // 2
    inv_freq = ROPE_THETA ** (-jnp.arange(half, dtype=F32) / half)
    ang = pos.astype(F32)[:, None] * inv_freq
    cos, sin = jnp.cos(ang), jnp.sin(ang)
    one = jnp.ones((pos.shape[0], HEAD_DIM - ROT_DIM), F32)
    zero = jnp.zeros((pos.shape[0], HEAD_DIM - ROT_DIM), F32)
    zh = jnp.zeros_like(sin)
    c = jnp.concatenate([cos, cos, one], axis=1)
    s1 = jnp.concatenate([-sin, zh, zero], axis=1)
    s2 = jnp.concatenate([zh, sin, zero], axis=1)
    rep = LANES // HEAD_DIM
    return jnp.tile(c, (1, rep)), jnp.tile(s1, (1, rep)), jnp.tile(s2, (1, rep))


def _swa_qkv_kernel(x_ref, w_ref, c_ref, s1_ref, s2_ref, q_ref, k_ref, v_ref, kv_ref, *, tm, keep):
    acc = _dot(x_ref[...].astype(BF16), w_ref[...])
    c, s1, s2 = c_ref[...], s1_ref[...], s2_ref[...]
    half = ROT_DIM // 2
    n_q = Q_WIDTH // LANES
    n_k = KV_WIDTH // LANES
    for cg in range(n_q + n_k):
        xg = acc[:, cg * LANES:(cg + 1) * LANES]
        rot = xg * c + pltpu.roll(xg, LANES - half, 1) * s1 + pltpu.roll(xg, half, 1) * s2
        if cg < n_q:
            q_ref[:, cg * LANES:(cg + 1) * LANES] = rot.astype(q_ref.dtype)
        else:
            ck = cg - n_q
            k_ref[:, ck * LANES:(ck + 1) * LANES] = rot.astype(k_ref.dtype)
            kv_ref[0, :, ck * LANES:(ck + 1) * LANES] = rot[tm - keep:, :]
    v = acc[:, Q_WIDTH + KV_WIDTH:]
    v_ref[...] = v.astype(v_ref.dtype)
    kv_ref[0, :, KV_WIDTH:] = v[tm - keep:, :]


def _swa_qkv(x, w_qkv, pos, n_seq, tm, keep, qdtype):
    t = x.shape[0]
    s = t // n_seq
    nb = s // tm
    c, s1, s2 = _rope_tables(pos)
    row = lambda n, i: (n * nb + i, 0)
    tab = pl.BlockSpec((tm, LANES), lambda n, i: (i, 0))
    kern = functools.partial(_swa_qkv_kernel, tm=tm, keep=keep)
    return pl.pallas_call(
        kern, grid=(n_seq, nb),
        in_specs=[pl.BlockSpec((tm, D), row), _full((D, Q_WIDTH + 2 * KV_WIDTH)), tab, tab, tab],
        out_specs=[pl.BlockSpec((tm, Q_WIDTH), row), pl.BlockSpec((tm, KV_WIDTH), row),
                   pl.BlockSpec((tm, KV_WIDTH), row),
                   pl.BlockSpec((1, keep, 2 * KV_WIDTH), lambda n, i: (n, 0, 0))],
        out_shape=[jax.ShapeDtypeStruct((t, Q_WIDTH), qdtype), jax.ShapeDtypeStruct((t, KV_WIDTH), qdtype),
                   jax.ShapeDtypeStruct((t, KV_WIDTH), qdtype),
                   jax.ShapeDtypeStruct((n_seq, keep, 2 * KV_WIDTH), F32)],
        compiler_params=_cparams(("parallel", "arbitrary")), name="swa_qkv")(x, w_qkv, c, s1, s2)


def _swa_attn_kernel(sink_ref, q_ref, kp_ref, kc_ref, vp_ref, vc_ref, o_ref, *, nq):
    j = pl.program_id(1)
    w, grp = WINDOW, SWA_GROUP
    r = lax.broadcasted_iota(jnp.int32, (grp * w, 2 * w), 0) % w
    c = lax.broadcasted_iota(jnp.int32, (grp * w, 2 * w), 1)
    in_prev = jnp.logical_and(c < w, c > r)
    in_cur = jnp.logical_and(c >= w, (c - w) <= r)
    ok_inner = jnp.logical_or(in_prev, in_cur)
    ok_first = jnp.logical_or(jnp.logical_and(in_prev, j > 0), in_cur)
    scale = HEAD_DIM ** -0.5
    combos = [(u, h) for u in range(nq) for h in range(SWA_KV_HEADS)]
    kcat, vcat, q4, sink, ok = [], [], [], [], []
    for u, h in combos:
        sl = slice(h * HEAD_DIM, (h + 1) * HEAD_DIM)
        rows = slice(u * w, (u + 1) * w)
        before = slice((u - 1) * w, u * w)
        k_prev = kp_ref[:, sl] if u == 0 else kc_ref[before, sl]
        v_prev = vp_ref[:, sl] if u == 0 else vc_ref[before, sl]
        kcat.append(jnp.concatenate([k_prev, kc_ref[rows, sl]], axis=0))
        vcat.append(jnp.concatenate([v_prev, vc_ref[rows, sl]], axis=0))
        heads = [h * grp + g for g in range(grp)]
        q4.append(jnp.concatenate([q_ref[rows, hq * HEAD_DIM:(hq + 1) * HEAD_DIM] for hq in heads], axis=0))
        sink.append(jnp.concatenate([jnp.full((w, 1), sink_ref[hq], F32) for hq in heads], axis=0))
        ok.append(ok_first if u == 0 else ok_inner)
    n = range(len(combos))
    s = [jnp.where(ok[i], _dot_nt(q4[i], kcat[i]) * scale, NEG_INF) for i in n]
    m = [jnp.maximum(jnp.max(s[i], axis=-1, keepdims=True), sink[i]) for i in n]
    p = [jnp.exp(s[i] - m[i]) for i in n]
    den = [jnp.sum(p[i], axis=-1, keepdims=True) + jnp.exp(sink[i] - m[i]) for i in n]
    o = [_dot((p[i] / den[i]).astype(BF16), vcat[i]) for i in n]
    for i, (u, h) in enumerate(combos):
        for g in range(grp):
            hq = h * grp + g
            o_ref[u * w:(u + 1) * w, hq * HEAD_DIM:(hq + 1) * HEAD_DIM] = o[i][g * w:(g + 1) * w].astype(o_ref.dtype)


def _swa_attn_prompt(q, k, v, sinks, n_seq, nq):
    t = q.shape[0]
    nb = t // n_seq // WINDOW
    ns = nb // nq
    cur = lambda n, j: (n * ns + j, 0)
    prev = lambda n, j: (n * nb + jnp.maximum(j * nq - 1, 0), 0)
    return pl.pallas_call(
        functools.partial(_swa_attn_kernel, nq=nq), grid=(n_seq, ns),
        in_specs=[pl.BlockSpec(memory_space=pltpu.SMEM), pl.BlockSpec((nq * WINDOW, Q_WIDTH), cur),
                  pl.BlockSpec((WINDOW, KV_WIDTH), prev), pl.BlockSpec((nq * WINDOW, KV_WIDTH), cur),
                  pl.BlockSpec((WINDOW, KV_WIDTH), prev), pl.BlockSpec((nq * WINDOW, KV_WIDTH), cur)],
        out_specs=pl.BlockSpec((nq * WINDOW, Q_WIDTH), cur),
        out_shape=jax.ShapeDtypeStruct((t, Q_WIDTH), BF16),
        compiler_params=_cparams(("parallel", "arbitrary")), name="swa_attn")(sinks, q, k, k, v, v)


def _swa_sample_kernel(sink_ref, q_ref, kn_ref, vn_ref, ck_ref, cv_ref, o_ref, *, wb):
    kidx = lax.broadcasted_iota(jnp.int32, (1, wb, 1), 1)
    valid = (wb - kidx) < WINDOW
    scale = HEAD_DIM ** -0.5
    for h in range(SWA_KV_HEADS):
        sl = slice(h * HEAD_DIM, (h + 1) * HEAD_DIM)
        ck, cv = _round_bf16(ck_ref[:, :, sl]), _round_bf16(cv_ref[:, :, sl])
        kn, vn = _round_bf16(kn_ref[:, :, sl]), _round_bf16(vn_ref[:, :, sl])
        for g in range(SWA_GROUP):
            hq = h * SWA_GROUP + g
            qs = slice(hq * HEAD_DIM, (hq + 1) * HEAD_DIM)
            qh = _round_bf16(q_ref[:, :, qs])
            s = jnp.where(valid, jnp.sum(ck * qh, axis=-1, keepdims=True) * scale, NEG_INF)
            sn = jnp.sum(kn * qh, axis=-1, keepdims=True) * scale
            sink = sink_ref[hq]
            m = jnp.maximum(jnp.maximum(jnp.max(s, axis=1, keepdims=True), sn), sink)
            p = jnp.exp(s - m)
            pn = jnp.exp(sn - m)
            den = jnp.sum(p, axis=1, keepdims=True) + pn + jnp.exp(sink - m)
            p, pn = _round_bf16(p / den), _round_bf16(pn / den)
            o_ref[:, :, qs] = jnp.sum(p * cv, axis=1, keepdims=True) + pn * vn


def _swa_attn_sample(q, kn, vn, cache_k, cache_v, sinks, bs):
    b, wb = cache_k.shape[0], cache_k.shape[1]
    blk3 = lambda w: pl.BlockSpec((bs, 1, w), lambda i: (i, 0, 0))
    cblk = pl.BlockSpec((bs, wb, KV_WIDTH), lambda i: (i, 0, 0))
    out = pl.pallas_call(
        functools.partial(_swa_sample_kernel, wb=wb), grid=(b // bs,),
        in_specs=[pl.BlockSpec(memory_space=pltpu.SMEM), blk3(Q_WIDTH), blk3(KV_WIDTH), blk3(KV_WIDTH), cblk, cblk],
        out_specs=blk3(Q_WIDTH), out_shape=jax.ShapeDtypeStruct((b, 1, Q_WIDTH), F32),
        compiler_params=_cparams(("parallel",)), name="swa_sample")(
            sinks, q.reshape(b, 1, Q_WIDTH), kn.reshape(b, 1, KV_WIDTH), vn.reshape(b, 1, KV_WIDTH),
            cache_k.reshape(b, wb, KV_WIDTH), cache_v.reshape(b, wb, KV_WIDTH))
    return out.reshape(b, Q_WIDTH)


def _gelu_tanh(x):
    return 0.5 * x * (1.0 + jnp.tanh(0.7978845608028654 * (x + 0.044715 * x * x * x)))


def _lru_gates(xc, wa_ref, ba, wi_ref, bi, lam):
    xcb = xc.astype(BF16)
    gw = wa_ref.shape[1]
    ra, ia = [], []
    for gi in range(wa_ref.shape[0]):
        xs = xcb[:, gi * gw:(gi + 1) * gw]
        ra.append(_dot(xs, wa_ref[gi]))
        ia.append(_dot(xs, wi_ref[gi]))
    r = _sigmoid(jnp.concatenate(ra, axis=-1) + ba)
    ig = _sigmoid(jnp.concatenate(ia, axis=-1) + bi)
    log_a = -LRU_C * r * _softplus(-lam)
    a = jnp.exp(log_a)
    b = jnp.sqrt(-jnp.tanh(log_a) * (a * a + 1.0)) * (ig * xc)
    return a, b


def _shift_rows(ext, s, tm):
    return pltpu.roll(ext, s, 0)[SUBLANES:SUBLANES + tm]


def _lru_prompt_kernel(x_ref, win_ref, bin_ref, cw_ref, cb_ref, wa_ref, ba_ref, wi_ref, bi_ref, lam_ref,
                       wo_ref, g_ref, b_ref, o_ref, conv_ref, hl_ref, cx_ref, ch_ref, *, tm):
    i = pl.program_id(1)

    @pl.when(i == 0)
    def _():
        cx_ref[...] = jnp.zeros_like(cx_ref)
        ch_ref[...] = jnp.zeros_like(ch_ref)

    x = x_ref[...]
    xy = _dot(x.astype(BF16), win_ref[...]) + bin_ref[...]
    xb = xy[:, :D]
    y_gate = _gelu_tanh(xy[:, D:])
    ext = jnp.concatenate([cx_ref[...], xb], axis=0)
    cw = cw_ref[...]
    xc = cb_ref[...] + xb * cw[CONV_W - 1:CONV_W]
    for s in range(1, CONV_W):
        xc = xc + _shift_rows(ext, s, tm) * cw[CONV_W - 1 - s:CONV_W - s]
    cx_ref[...] = xb[tm - SUBLANES:]
    conv_ref[0] = xb[tm - SUBLANES:]

    a, b = _lru_gates(xc, wa_ref, ba_ref[...], wi_ref, bi_ref[...], lam_ref[...])
    sub = lax.broadcasted_iota(jnp.int32, (tm, 1), 0) % SUBLANES
    s = 1
    while s < SUBLANES:
        keep = sub >= s
        a_sh = jnp.where(keep, pltpu.roll(a, s, 0), 1.0)
        b_sh = jnp.where(keep, pltpu.roll(b, s, 0), 0.0)
        b = a * b_sh + b
        a = a * a_sh
        s *= 2
    carry = ch_ref[SUBLANES - 1:SUBLANES, :]
    groups = []
    for gi in range(tm // SUBLANES):
        rows = slice(gi * SUBLANES, (gi + 1) * SUBLANES)
        hg = a[rows] * carry + b[rows]
        groups.append(hg)
        carry = hg[SUBLANES - 1:SUBLANES]
    h = jnp.concatenate(groups, axis=0)
    ch_ref[...] = h[tm - SUBLANES:]
    hl_ref[0] = h[tm - SUBLANES:]
    acc = _dot((h * y_gate).astype(BF16), wo_ref[...])
    o_ref[...] = _ln(ALPHA * x + acc, g_ref[...], b_ref[...])


def _lru_weights(w_in, b_in, conv_w, conv_b, w_a, b_a, w_i, b_i, lam, w_o):
    gsz = 4
    ng = LRU_BLOCKS // gsz
    bw = D // LRU_BLOCKS

    def grouped(w):
        w4 = w.reshape(ng, gsz, bw, bw)
        return jnp.einsum('gaij,ab->gaibj', w4, jnp.eye(gsz, dtype=w.dtype)).reshape(ng, gsz * bw, gsz * bw).astype(BF16)

    row = lambda v: v.reshape(1, -1)
    return (w_in.astype(BF16), row(b_in), conv_w, row(conv_b), grouped(w_a), row(b_a), grouped(w_i), row(b_i),
            row(lam), w_o.astype(BF16))


def _lru_prompt(x, wts, g, b, n_seq, tm):
    t = x.shape[0]
    nb = t // n_seq // tm
    row = lambda n, i: (n * nb + i, 0)
    last = pl.BlockSpec((1, SUBLANES, D), lambda n, i: (n, 0, 0))
    w_in, b_in, cw, cb, wa, ba, wi, bi, lam, wo = wts
    return pl.pallas_call(
        functools.partial(_lru_prompt_kernel, tm=tm), grid=(n_seq, nb),
        in_specs=[pl.BlockSpec((tm, D), row), _full(w_in.shape), _full(b_in.shape), _full(cw.shape), _full(cb.shape),
                  _full(wa.shape), _full(ba.shape), _full(wi.shape), _full(bi.shape), _full(lam.shape),
                  _full(wo.shape), _full((1, D)), _full((1, D))],
        out_specs=[pl.BlockSpec((tm, D), row), last, last],
        out_shape=[jax.ShapeDtypeStruct((t, D), F32), jax.ShapeDtypeStruct((n_seq, SUBLANES, D), F32),
                   jax.ShapeDtypeStruct((n_seq, SUBLANES, D), F32)],
        scratch_shapes=[pltpu.VMEM((SUBLANES, D), F32), pltpu.VMEM((SUBLANES, D), F32)],
        compiler_params=_cparams(("parallel", "arbitrary")), name="lru_prompt")(x, *wts, g, b)


def _lru_sample_kernel(x_ref, c0_ref, c1_ref, c2_ref, h0_ref, win_ref, bin_ref, cw_ref, cb_ref, wa_ref, ba_ref,
                       wi_ref, bi_ref, lam_ref, wo_ref, g_ref, b_ref, o_ref, xb_ref, h_ref):
    x = x_ref[...]
    xy = _dot(x.astype(BF16), win_ref[...]) + bin_ref[...]
    xb = xy[:, :D]
    y_gate = _gelu_tanh(xy[:, D:])
    cw = cw_ref[...]
    xc = (cb_ref[...] + c0_ref[...] * cw[0:1] + c1_ref[...] * cw[1:2] + c2_ref[...] * cw[2:3] + xb * cw[3:4])
    a, b = _lru_gates(xc, wa_ref, ba_ref[...], wi_ref, bi_ref[...], lam_ref[...])
    h = a * h0_ref[...] + b
    xb_ref[...] = xb
    h_ref[...] = h
    acc = _dot((h * y_gate).astype(BF16), wo_ref[...])
    o_ref[...] = _ln(ALPHA * x + acc, g_ref[...], b_ref[...])


def _lru_sample(x, conv_state, h0, wts, g, b):
    t = x.shape[0]
    args = (x, conv_state[:, 0], conv_state[:, 1], conv_state[:, 2], h0, *wts, g, b)
    sd = jax.ShapeDtypeStruct((t, D), F32)
    return pl.pallas_call(
        _lru_sample_kernel, grid=(1,),
        in_specs=[_full(a.shape) for a in args],
        out_specs=[_full((t, D))] * 3, out_shape=[sd, sd, sd],
        compiler_params=_cparams(("arbitrary",)), name="lru_sample")(*args)


def _rwkv_pre_kernel(x_ref, xp_ref, mu_ref, wr_ref, wk_ref, wv_ref, w0_ref, w1_ref, w2_ref, a0_ref, a1_ref, a2_ref,
                     g1_ref, g2_ref, r_ref, k_ref, v_ref, a_ref, ld_ref, g_ref, *scratch, tm, seq):
    x = x_ref[...]
    if seq:
        cx_ref, = scratch
        i = pl.program_id(1)

        @pl.when(i == 0)
        def _():
            cx_ref[...] = xp_ref[0]

        x_prev = _shift_rows(jnp.concatenate([cx_ref[...], x], axis=0), 1, tm)
        cx_ref[...] = x[tm - SUBLANES:]
    else:
        x_prev = xp_ref[...]
    xx = x_prev - x
    mu = mu_ref[...]
    mix = lambda j: (x + xx * mu[j:j + 1]).astype(BF16)
    r_ref[...] = _dot(mix(0), wr_ref[...]).astype(r_ref.dtype)
    wl = _dot(jnp.tanh(_dot(mix(1), w1_ref[...])).astype(BF16), w2_ref[...])
    w = -_softplus(-(w0_ref[...] + wl)) - 0.5
    ld_ref[...] = -jnp.exp(w)
    k_ref[...] = _dot(mix(2), wk_ref[...]).astype(k_ref.dtype)
    v_ref[...] = _dot(mix(3), wv_ref[...]).astype(v_ref.dtype)
    al = _dot(_dot(mix(4), a1_ref[...]).astype(BF16), a2_ref[...])
    a_ref[...] = _sigmoid(a0_ref[...] + al).astype(a_ref.dtype)
    g_ref[...] = _dot(_sigmoid(_dot(mix(5), g1_ref[...])).astype(BF16), g2_ref[...]).astype(g_ref.dtype)


def _rwkv_pre(x, x_prev, wts, n_seq, tm, seq, dtype):
    t = x.shape[0]
    nb = t // n_seq // tm
    row = lambda n, i: (n * nb + i, 0)
    xp_spec = pl.BlockSpec((1, SUBLANES, D), lambda n, i: (n, 0, 0)) if seq else pl.BlockSpec((tm, D), row)
    sd = lambda dt: jax.ShapeDtypeStruct((t, D), dt)
    blk = pl.BlockSpec((tm, D), row)
    return pl.pallas_call(
        functools.partial(_rwkv_pre_kernel, tm=tm, seq=seq), grid=(n_seq, nb),
        in_specs=[blk, xp_spec] + [_full(w.shape) for w in wts],
        out_specs=[blk] * 6,
        out_shape=[sd(dtype), sd(dtype), sd(dtype), sd(dtype), sd(F32), sd(dtype)],
        scratch_shapes=[pltpu.VMEM((SUBLANES, D), F32)] if seq else [],
        compiler_params=_cparams(("parallel", "arbitrary")), name="rwkv_pre")(x, x_prev, *wts)


def _seg_sum(x, first):
    s0 = jnp.sum(jnp.where(first, x, 0.0), axis=-1, keepdims=True)
    s1 = jnp.sum(jnp.where(first, 0.0, x), axis=-1, keepdims=True)
    return jnp.where(first, s0, s1)


def _wkv_kernel(r_ref, k_ref, v_ref, a_ref, ld_ref, g_ref, kk_ref, ka_ref, rk_ref, gg_ref, gb_ref,
                o_ref, s_ref, st_ref):
    c = pl.program_id(1)
    L = WKV_CHUNK
    P2 = 2 * L

    @pl.when(c == 0)
    def _():
        st_ref[...] = jnp.zeros_like(st_ref)

    ld_all = ld_ref[...]
    tri = (lax.broadcasted_iota(jnp.int32, (L, L), 0) >= lax.broadcasted_iota(jnp.int32, (L, L), 1)).astype(BF16)
    hi = ld_all.astype(BF16)
    r1 = ld_all - hi.astype(F32)
    mid = r1.astype(BF16)
    lo = (r1 - mid.astype(F32)).astype(BF16)
    cum_all = _dot(tri, hi) + _dot(tri, mid) + _dot(tri, lo)

    lane = lax.broadcasted_iota(jnp.int32, (1, LANES), 1)
    first = lane < RWKV_HD
    ri = lax.broadcasted_iota(jnp.int32, (P2, P2), 0)
    ci = lax.broadcasted_iota(jnp.int32, (P2, P2), 1)
    same_head = (ri // L) == (ci // L)
    rt, ct = ri % L, ci % L
    strict = jnp.logical_and(same_head, rt > ct)
    incl = jnp.logical_and(same_head, rt >= ct)
    eye = ri == ci

    def stack(xv):
        return jnp.concatenate([jnp.where(first, xv, 0.0), jnp.where(first, 0.0, xv)], axis=0).astype(BF16)

    pairs = range(RWKV_HEADS // 2)
    sls = [slice(p * LANES, (p + 1) * LANES) for p in pairs]
    ws, us, ks, rs, ul, kl, vs, g_l, bonus = ([] for _ in range(9))
    for sl in sls:
        rp, kp, vp, ap = (ref[:, sl].astype(F32) for ref in (r_ref, k_ref, v_ref, a_ref))
        ldp, cum = ld_all[:, sl], cum_all[:, sl]
        kk = kp * kk_ref[:, sl]
        kk = kk / jnp.maximum(jnp.sqrt(_seg_sum(kk * kk, first)), 1e-12)
        kmod = kp * (1.0 + (ap - 1.0) * ka_ref[:, sl])
        bp = kk * ap
        cum_l = cum[L - 1:L, :]
        g_inv = jnp.exp(-cum)
        g_to_end = jnp.exp(cum_l - cum)
        ws.append(stack(kk * jnp.exp(cum - ldp)))
        us.append(stack(bp * g_inv))
        ks.append(stack(kmod * g_inv))
        rs.append(stack(rp * jnp.exp(cum)))
        ul.append(stack(bp * g_to_end))
        kl.append(stack(kmod * g_to_end))
        vs.append(stack(vp))
        g_l.append(jnp.exp(cum_l))
        bonus.append(_seg_sum(rp * kmod * rk_ref[:, sl], first) * vp)

    gram = [_dot_nt(jnp.concatenate([ws[p], rs[p]], axis=0), jnp.concatenate([us[p], ks[p]], axis=0)) for p in pairs]
    n_mat = [jnp.where(strict, gram[p][:P2, :P2], 0.0) for p in pairs]
    m_mat = [jnp.where(strict, gram[p][:P2, P2:], 0.0).astype(BF16) for p in pairs]
    nr_mat = [jnp.where(incl, gram[p][P2:, :P2], 0.0).astype(BF16) for p in pairs]
    mr_mat = [jnp.where(incl, gram[p][P2:, P2:], 0.0).astype(BF16) for p in pairs]

    def level_mask(sz):
        sub = jnp.logical_and((rt // sz) % 2 == 1, (ct // sz) % 2 == 0)
        return jnp.logical_and(jnp.logical_and(sub, (rt // (2 * sz)) == (ct // (2 * sz))), same_head)

    x_inv = [jnp.where(eye, 1.0, 0.0) - jnp.where(level_mask(1), n_mat[p], 0.0) for p in pairs]
    sz = 2
    while sz < L:
        mask = level_mask(sz)
        xb = [x_inv[p].astype(BF16) for p in pairs]
        xc = [_dot(xb[p], jnp.where(mask, n_mat[p], 0.0).astype(BF16)).astype(BF16) for p in pairs]
        x_inv = [x_inv[p] - _dot(xc[p], xb[p]) for p in pairs]
        sz *= 2

    a0 = [st_ref[p] for p in pairs]
    a0b = [a0[p].astype(BF16) for p in pairs]
    rhs = [_dot(jnp.concatenate([ws[p], m_mat[p]], axis=1), jnp.concatenate([a0b[p], vs[p]], axis=0)).astype(BF16)
           for p in pairs]
    pm = [(-_dot(x_inv[p].astype(BF16), rhs[p])).astype(BF16) for p in pairs]
    o_st = [_dot(jnp.concatenate([rs[p], nr_mat[p], mr_mat[p]], axis=1),
                 jnp.concatenate([a0b[p], pm[p], vs[p]], axis=0)) for p in pairs]
    for p in pairs:
        g_col = jnp.sum(jnp.where(eye, jnp.broadcast_to(g_l[p], (P2, P2)), 0.0), axis=-1, keepdims=True)
        st_ref[p] = g_col * a0[p] + _dot_tn(jnp.concatenate([ul[p], kl[p]], axis=0),
                                            jnp.concatenate([pm[p], vs[p]], axis=0))

    inv_n = 1.0 / RWKV_HD
    for p, sl in zip(pairs, sls):
        o = o_st[p][:L] + o_st[p][L:]
        mu = _seg_sum(o, first) * inv_n
        oc = o - mu
        var = _seg_sum(oc * oc, first) * inv_n
        on = oc * lax.rsqrt(var + RWKV_GN_EPS) * gg_ref[:, sl] + gb_ref[:, sl]
        o_ref[:, sl] = ((on + bonus[p]) * g_ref[:, sl].astype(F32)).astype(o_ref.dtype)

    s_ref[0] = st_ref[...]


def _wkv_prompt(r, k, v, a, ld, g, hp, n_seq):
    t = r.shape[0]
    L = WKV_CHUNK
    nc = t // n_seq // L
    row = lambda n, c: (n * nc + c, 0)
    blk = pl.BlockSpec((L, D), row)
    npair = RWKV_HEADS // 2
    return pl.pallas_call(
        _wkv_kernel, grid=(n_seq, nc),
        in_specs=[blk] * 6 + [_full((1, D))] * 5,
        out_specs=[blk, pl.BlockSpec((1, npair, LANES, LANES), lambda n, c: (n, 0, 0, 0))],
        out_shape=[jax.ShapeDtypeStruct((t, D), BF16), jax.ShapeDtypeStruct((n_seq, npair, LANES, LANES), F32)],
        scratch_shapes=[pltpu.VMEM((npair, LANES, LANES), F32)],
        compiler_params=_cparams(("parallel", "arbitrary")), name="wkv_chunk")(r, k, v, a, ld, g, *hp)


def _wkv_sample_kernel(r_ref, k_ref, v_ref, a_ref, ld_ref, g_ref, s_ref, kk_ref, ka_ref, rk_ref, gg_ref, gb_ref,
                       o_ref, so_ref):
    hd = RWKV_HD
    eye = lax.broadcasted_iota(jnp.int32, (1, hd, hd), 1) == lax.broadcasted_iota(jnp.int32, (1, hd, hd), 2)
    for h in range(RWKV_HEADS):
        hs = slice(h, h + 1)
        r, k, v, a, ld, g = (ref[:, hs, :] for ref in (r_ref, k_ref, v_ref, a_ref, ld_ref, g_ref))
        s = s_ref[:, h]
        kk = k * kk_ref[hs, :]
        kk = kk / jnp.maximum(jnp.sqrt(jnp.sum(kk * kk, axis=-1, keepdims=True)), 1e-12)
        kmod = k * (1.0 + (a - 1.0) * ka_ref[hs, :])
        skk = jnp.sum(s * kk, axis=-1, keepdims=True)
        v_col = jnp.sum(jnp.where(eye, v, 0.0), axis=-1, keepdims=True)
        s_new = s * jnp.exp(ld) - skk * (kk * a) + v_col * kmod
        so_ref[:, h] = s_new
        o_col = jnp.sum(s_new * r, axis=-1, keepdims=True)
        o = jnp.sum(jnp.where(eye, o_col, 0.0), axis=1, keepdims=True)
        mu = jnp.mean(o, axis=-1, keepdims=True)
        oc = o - mu
        var = jnp.mean(oc * oc, axis=-1, keepdims=True)
        on = oc * lax.rsqrt(var + RWKV_GN_EPS) * gg_ref[hs, :] + gb_ref[hs, :]
        bonus = jnp.sum(r * kmod * rk_ref[hs, :], axis=-1, keepdims=True) * v
        o_ref[:, hs, :] = (on + bonus) * g


def _wkv_sample(r, k, v, a, ld, g, state, hp, bs):
    b = r.shape[0]
    h3 = lambda z: z.reshape(b, RWKV_HEADS, RWKV_HD)
    blk = pl.BlockSpec((bs, RWKV_HEADS, RWKV_HD), lambda i: (i, 0, 0))
    sblk = pl.BlockSpec((bs, RWKV_HEADS, RWKV_HD, RWKV_HD), lambda i: (i, 0, 0, 0))
    hp3 = [z.reshape(RWKV_HEADS, RWKV_HD) for z in hp]
    o, s_new = pl.pallas_call(
        _wkv_sample_kernel, grid=(b // bs,),
        in_specs=[blk] * 6 + [sblk] + [_full((RWKV_HEADS, RWKV_HD))] * 5,
        out_specs=[blk, sblk],
        out_shape=[jax.ShapeDtypeStruct((b, RWKV_HEADS, RWKV_HD), F32), jax.ShapeDtypeStruct(state.shape, F32)],
        compiler_params=_cparams(("parallel",)), name="wkv_sample")(
            h3(r), h3(k), h3(v), h3(a), h3(ld), h3(g), state, *hp3)
    return o.reshape(b, D), s_new


def _mem_prompt_kernel(x_ref, wq_ref, mk_ref, mv_ref, wo_ref, g_ref, b_ref, o_ref):
    x = x_ref[...]
    q = _dot(x.astype(BF16), wq_ref[...]).astype(BF16)
    scale = MEM_HD ** -0.5
    outs = []
    for h in range(MEM_HEADS):
        sl = slice(h * MEM_HD, (h + 1) * MEM_HD)
        s = _dot_nt(q[:, sl], mk_ref[0, :, sl]) * scale
        p = jnp.exp(s - jnp.max(s, axis=-1, keepdims=True))
        den = jnp.sum(p, axis=-1, keepdims=True)
        outs.append(_dot((p / den).astype(BF16), mv_ref[0, :, sl]).astype(BF16))
    acc = _dot(jnp.concatenate(outs, axis=-1), wo_ref[...])
    o_ref[...] = _ln(ALPHA * x + acc, g_ref[...], b_ref[...])


def _mem_attn_prompt(x, w_q, mk, mv, w_o, g, b, n_seq, tm):
    t = x.shape[0]
    nb = t // n_seq // tm
    m = mk.shape[1]
    row = lambda n, i: (n * nb + i, 0)
    mem = pl.BlockSpec((1, m, D), lambda n, i: (n, 0, 0))
    return pl.pallas_call(
        _mem_prompt_kernel, grid=(n_seq, nb),
        in_specs=[pl.BlockSpec((tm, D), row), _full((D, D)), mem, mem, _full((D, D)), _full((1, D)), _full((1, D))],
        out_specs=pl.BlockSpec((tm, D), row), out_shape=jax.ShapeDtypeStruct((t, D), F32),
        compiler_params=_cparams(("parallel", "arbitrary")), name="mem_attn")(x, w_q, mk, mv, w_o, g, b)


def _mem_sample_kernel(q_ref, ck_ref, cv_ref, o_ref, *, bs):
    scale = MEM_HD ** -0.5
    for b in range(bs):
        s = jnp.sum(_round_bf16(ck_ref[b]) * (_round_bf16(q_ref[b]) * scale), axis=-1, keepdims=True)
        p = jnp.exp(s - jnp.max(s, axis=0, keepdims=True))
        p = _round_bf16(p / jnp.sum(p, axis=0, keepdims=True))
        o_ref[b] = jnp.sum(p * _round_bf16(cv_ref[b]), axis=0, keepdims=True)


def _mem_attn_sample(q, cache_k, cache_v, layer, bs):
    _, b, m, nh, hd = cache_k.shape
    qb = pl.BlockSpec((bs, 1, nh, hd), lambda i: (i, 0, 0, 0))
    cb = pl.BlockSpec((None, bs, m, nh, hd), lambda i: (layer, i, 0, 0, 0))
    out = pl.pallas_call(
        functools.partial(_mem_sample_kernel, bs=bs), grid=(b // bs,), in_specs=[qb, cb, cb], out_specs=qb,
        out_shape=jax.ShapeDtypeStruct((b, 1, nh, hd), F32),
        compiler_params=_cparams(("parallel",)), name="mem_sample")(q.reshape(b, 1, nh, hd), cache_k, cache_v)
    return out.reshape(b, D)


_PAIRS = ((0, 1), (0, 2), (0, 3), (1, 2), (1, 3), (2, 3))


def _router_kernel(x_ref, rw_ref, rb_ref, bucket_ref, rank_ref, cnt_ref, base_ref, *, tm):
    i = pl.program_id(0)

    @pl.when(i == 0)
    def _():
        base_ref[...] = jnp.zeros_like(base_ref)

    logits = _dot_nt(rw_ref[...], x_ref[...].astype(BF16))
    e = jnp.exp(logits - jnp.max(logits, axis=0, keepdims=True))
    sel = e / jnp.sum(e, axis=0, keepdims=True) + rb_ref[...]
    s = [sel[j:j + 1, :] for j in range(N_EXPERTS)]
    neg = jnp.float32(-jnp.inf)

    best = jnp.zeros((1, tm), jnp.int32)
    best_score = None
    for gi in range(N_GROUPS):
        s0, s1, s2, s3 = s[4 * gi:4 * gi + 4]
        hi01, lo01, hi23, lo23 = jnp.maximum(s0, s1), jnp.minimum(s0, s1), jnp.maximum(s2, s3), jnp.minimum(s2, s3)
        score = jnp.maximum(hi01, hi23) + jnp.maximum(jnp.minimum(hi01, hi23), jnp.maximum(lo01, lo23))
        if gi == 0:
            best_score = score
        else:
            take = score > best_score
            best = jnp.where(take, gi, best)
            best_score = jnp.where(take, score, best_score)
    vals = []
    for j in range(EXPERTS_PER_GROUP):
        vj = s[j]
        for gi in range(1, N_GROUPS):
            vj = jnp.where(best == gi, s[4 * gi + j], vj)
        vals.append(vj)

    def argmax4(v):
        idx, mx = jnp.zeros((1, tm), jnp.int32), v[0]
        for j in range(1, EXPERTS_PER_GROUP):
            take = v[j] > mx
            idx = jnp.where(take, j, idx)
            mx = jnp.where(take, v[j], mx)
        return idx

    i1 = argmax4(vals)
    i2 = argmax4([jnp.where(i1 == j, neg, vals[j]) for j in range(EXPERTS_PER_GROUP)])
    lo, hi = jnp.minimum(i1, i2), jnp.maximum(i1, i2)
    pair = jnp.zeros((1, tm), jnp.int32)
    for pi, (pa, pb) in enumerate(_PAIRS):
        pair = jnp.where(jnp.logical_and(lo == pa, hi == pb), pi, pair)
    bucket = best * len(_PAIRS) + pair
    bucket_ref[0] = bucket

    onehot = (lax.broadcasted_iota(jnp.int32, (BUCKET_ROWS, tm), 0) == bucket).astype(F32)
    upper = (lax.broadcasted_iota(jnp.int32, (tm, tm), 0) <= lax.broadcasted_iota(jnp.int32, (tm, tm), 1)).astype(BF16)
    cum = _dot(onehot.astype(BF16), upper)
    base = base_ref[...]
    rank = jnp.sum(onehot * (cum + base), axis=0, keepdims=True) - 1.0
    rank_ref[0] = rank.astype(jnp.int32)
    base = base + jnp.sum(onehot, axis=1, keepdims=True)
    base_ref[...] = base
    cnt_ref[...] = jnp.broadcast_to(base, cnt_ref.shape)


def _router(x, rw_t, rb, tm):
    t = x.shape[0]
    nb = t // tm
    ib = pl.BlockSpec((1, 1, tm), lambda i: (i, 0, 0))
    bucket, rank, cnt = pl.pallas_call(
        functools.partial(_router_kernel, tm=tm), grid=(nb,),
        in_specs=[pl.BlockSpec((tm, D), lambda i: (i, 0)), _full(rw_t.shape), _full(rb.shape)],
        out_specs=[ib, ib, _full((BUCKET_ROWS, LANES))],
        out_shape=[jax.ShapeDtypeStruct((nb, 1, tm), jnp.int32), jax.ShapeDtypeStruct((nb, 1, tm), jnp.int32),
                   jax.ShapeDtypeStruct((BUCKET_ROWS, LANES), F32)],
        scratch_shapes=[pltpu.VMEM((BUCKET_ROWS, 1), F32)],
        compiler_params=_cparams(("arbitrary",)), name="router")(x, rw_t, rb)
    return bucket.reshape(t), rank.reshape(t), cnt[:N_BUCKETS, 0].astype(jnp.int32)


def _row_copies(idx_ref, base, src_hbm, dst, sem, n, wait):
    n_prio = 2

    def body(j, carry):
        for k in range(n_prio):
            r = j * n_prio + k
            cp = pltpu.make_async_copy(src_hbm.at[pl.ds(idx_ref[base + r], 1)], dst.at[pl.ds(r, 1)], sem)
            if wait:
                cp.wait()
            else:
                cp.start(priority=k)
        return carry

    assert n % n_prio == 0
    lax.fori_loop(0, n // n_prio, body, 0, unroll=4)


def _ffn_kernel(src_ref, lo_ref, hi_ref, used_ref, x_hbm, rw_ref, g0_ref, u0_ref, d0_ref, g1_ref, u1_ref, d1_ref,
                o_ref, xbuf, sem, *, blk):
    i = pl.program_id(0)
    used = used_ref[0]
    slot = i % 2

    @pl.when(jnp.logical_and(i == 0, used > 0))
    def _():
        _row_copies(src_ref, 0, x_hbm, xbuf.at[0], sem.at[0], blk, False)

    @pl.when(i + 1 < used)
    def _():
        _row_copies(src_ref, (i + 1) * blk, x_hbm, xbuf.at[1 - slot], sem.at[1 - slot], blk, False)

    @pl.when(i < used)
    def _():
        _row_copies(src_ref, i * blk, x_hbm, xbuf.at[slot], sem.at[slot], blk, True)
        xb = xbuf[slot].astype(BF16)
        logits = _dot(xb, rw_ref[...])
        lane = lax.broadcasted_iota(jnp.int32, logits.shape, 1)
        l_lo = jnp.sum(jnp.where(lane == lo_ref[i], logits, 0.0), axis=-1, keepdims=True)
        l_hi = jnp.sum(jnp.where(lane == hi_ref[i], logits, 0.0), axis=-1, keepdims=True)
        w_lo = _sigmoid(l_lo - l_hi)

        def expert(g_ref, u_ref, d_ref):
            gate = _dot(xb, g_ref[0])
            act = gate * _sigmoid(gate) * _dot(xb, u_ref[0])
            return _dot(act.astype(BF16), d_ref[0])

        y_lo = expert(g0_ref, u0_ref, d0_ref)
        y_hi = expert(g1_ref, u1_ref, d1_ref)
        o_ref[...] = w_lo * y_lo + (1.0 - w_lo) * y_hi

    @pl.when(i >= used)
    def _():
        o_ref[...] = jnp.zeros_like(o_ref)


def _ffn(x, src, blk_lo, blk_hi, n_used, rw, w_gate, w_up, w_down, layer, blk):
    rows = src.shape[0]
    nblk = rows // blk
    wg = lambda sel: pl.BlockSpec((None, 1, D, EXPERT_FF),
                                  lambda i, s, lo, hi, used: (layer, (lo, hi)[sel][i], 0, 0))
    wd = lambda sel: pl.BlockSpec((None, 1, EXPERT_FF, D),
                                  lambda i, s, lo, hi, used: (layer, (lo, hi)[sel][i], 0, 0))
    return pl.pallas_call(
        functools.partial(_ffn_kernel, blk=blk),
        grid_spec=pltpu.PrefetchScalarGridSpec(
            num_scalar_prefetch=4, grid=(nblk,),
            in_specs=[pl.BlockSpec(memory_space=pl.ANY), pl.BlockSpec(rw.shape, lambda i, s, lo, hi, used: (0, 0)),
                      wg(0), wg(0), wd(0), wg(1), wg(1), wd(1)],
            out_specs=pl.BlockSpec((blk, D), lambda i, s, lo, hi, used: (i, 0)),
            scratch_shapes=[pltpu.VMEM((2, blk, D), F32), pltpu.SemaphoreType.DMA((2,))]),
        out_shape=jax.ShapeDtypeStruct((rows, D), F32),
        compiler_params=_cparams(("arbitrary",)), name="moe_ffn")(
            src, blk_lo, blk_hi, n_used, x, rw, w_gate, w_up, w_down, w_gate, w_up, w_down)


def _combine_ln_kernel(dest_ref, x_ref, y_hbm, g_ref, b_ref, o_ref, ybuf, sem, *, tm):
    i = pl.program_id(0)
    slot = i % 2

    @pl.when(i == 0)
    def _():
        _row_copies(dest_ref, 0, y_hbm, ybuf.at[0], sem.at[0], tm, False)

    @pl.when(i + 1 < pl.num_programs(0))
    def _():
        _row_copies(dest_ref, (i + 1) * tm, y_hbm, ybuf.at[1 - slot], sem.at[1 - slot], tm, False)

    _row_copies(dest_ref, i * tm, y_hbm, ybuf.at[slot], sem.at[slot], tm, True)
    o_ref[...] = _ln(ALPHA * x_ref[...] + ybuf[slot], g_ref[...], b_ref[...])


def _combine_ln(x, y_rows, dest, g, b, tm):
    t = x.shape[0]
    rowb = pl.BlockSpec((tm, D), lambda i, d: (i, 0))
    vec = pl.BlockSpec((1, D), lambda i, d: (0, 0))
    return pl.pallas_call(
        functools.partial(_combine_ln_kernel, tm=tm),
        grid_spec=pltpu.PrefetchScalarGridSpec(
            num_scalar_prefetch=1, grid=(t // tm,),
            in_specs=[rowb, pl.BlockSpec(memory_space=pl.ANY), vec, vec], out_specs=rowb,
            scratch_shapes=[pltpu.VMEM((2, tm, D), F32), pltpu.SemaphoreType.DMA((2,))]),
        out_shape=jax.ShapeDtypeStruct((t, D), F32),
        compiler_params=_cparams(("arbitrary",)), name="moe_combine_ln")(dest, x, y_rows, g, b)


def _invert_rows_kernel(dest_ref, src_ref):
    def clear(r, carry):
        src_ref[r] = 0
        return carry

    def put(tok, carry):
        src_ref[dest_ref[tok]] = tok
        return carry

    lax.fori_loop(0, src_ref.shape[0], clear, 0, unroll=8)
    lax.fori_loop(0, dest_ref.shape[0], put, 0, unroll=8)


def _invert_rows(dest, rows):
    smem = pl.BlockSpec(memory_space=pltpu.SMEM)
    return pl.pallas_call(
        _invert_rows_kernel, in_specs=[smem], out_specs=smem,
        out_shape=jax.ShapeDtypeStruct((rows,), jnp.int32), name="invert_rows")(dest)


def _moe_ln(x, rw_t, rb, rw_pad, w_gate, w_up, w_down, layer, g, b, tm_router, blk, tm_comb):
    t = x.shape[0]
    bucket, rank, counts = _router(x, rw_t, rb, tm_router)
    padded = (counts + blk - 1) // blk * blk
    ends = jnp.cumsum(padded)
    dest = ((ends - padded)[bucket] + rank).astype(jnp.int32)
    nblk = t // blk + N_BUCKETS
    src = _invert_rows(dest, nblk * blk)
    blk_bucket = jnp.minimum(jnp.searchsorted(ends, jnp.arange(nblk) * blk, side='right'), N_BUCKETS - 1)
    pair_lo = jnp.array([p[0] for p in _PAIRS], jnp.int32)
    pair_hi = jnp.array([p[1] for p in _PAIRS], jnp.int32)
    grp, pr = blk_bucket // len(_PAIRS), blk_bucket % len(_PAIRS)
    blk_lo = (grp * EXPERTS_PER_GROUP + pair_lo[pr]).astype(jnp.int32)
    blk_hi = (grp * EXPERTS_PER_GROUP + pair_hi[pr]).astype(jnp.int32)
    n_used = (ends[-1:] // blk).astype(jnp.int32)
    y_rows = _ffn(x, src, blk_lo, blk_hi, n_used, rw_pad, w_gate, w_up, w_down, layer, blk)
    return _combine_ln(x, y_rows, dest, g, b, tm_comb)


def _moe_dense_kernel(x_ref, lo_ref, hi_ref, rw_ref, wg_ref, wu_ref, wd_ref, g_ref, b_ref, o_ref, acc_ref):
    e = pl.program_id(0)

    @pl.when(e == 0)
    def _():
        acc_ref[...] = jnp.zeros_like(acc_ref)

    x = x_ref[...]
    xb = x.astype(BF16)
    logits = _dot(xb, rw_ref[...])
    lane = lax.broadcasted_iota(jnp.int32, logits.shape, 1)
    lo, hi = lo_ref[...], hi_ref[...]
    l_lo = jnp.sum(jnp.where(lane == lo, logits, 0.0), axis=-1, keepdims=True)
    l_hi = jnp.sum(jnp.where(lane == hi, logits, 0.0), axis=-1, keepdims=True)
    w_lo = _sigmoid(l_lo - l_hi)
    coef = jnp.where(lo == e, w_lo, 0.0) + jnp.where(hi == e, 1.0 - w_lo, 0.0)
    gate = _dot(xb, wg_ref[0])
    act = gate * _sigmoid(gate) * _dot(xb, wu_ref[0])
    acc_ref[...] += coef * _dot(act.astype(BF16), wd_ref[0])

    @pl.when(e == pl.num_programs(0) - 1)
    def _():
        o_ref[...] = _ln(ALPHA * x + acc_ref[...], g_ref[...], b_ref[...])


def _moe_ln_dense(x, rw_t, rb, rw_pad, w_gate, w_up, w_down, layer, g, b):
    t = x.shape[0]
    bucket, _, _ = _router(x, rw_t, rb, t)
    pair_lo = jnp.array([p[0] for p in _PAIRS], jnp.int32)
    pair_hi = jnp.array([p[1] for p in _PAIRS], jnp.int32)
    grp, pr = bucket // len(_PAIRS), bucket % len(_PAIRS)
    lo = (grp * EXPERTS_PER_GROUP + pair_lo[pr]).astype(jnp.int32).reshape(t, 1)
    hi = (grp * EXPERTS_PER_GROUP + pair_hi[pr]).astype(jnp.int32).reshape(t, 1)
    wg = pl.BlockSpec((None, 1, D, EXPERT_FF), lambda e: (layer, e, 0, 0))
    wd = pl.BlockSpec((None, 1, EXPERT_FF, D), lambda e: (layer, e, 0, 0))
    return pl.pallas_call(
        _moe_dense_kernel, grid=(N_EXPERTS,),
        in_specs=[_full((t, D)), _full((t, 1)), _full((t, 1)), _full(rw_pad.shape), wg, wg, wd,
                  _full((1, D)), _full((1, D))],
        out_specs=_full((t, D)), out_shape=jax.ShapeDtypeStruct((t, D), F32),
        scratch_shapes=[pltpu.VMEM((t, D), F32)],
        compiler_params=_cparams(("arbitrary",)), name="moe_dense")(x, lo, hi, rw_pad, w_gate, w_up, w_down, g, b)


def kernel(x_prompt, x_sample, cache_swa_k, cache_swa_v, state_lru_conv, state_lru_h, state_rwkv_shift, state_rwkv_wkv, cache_mem_k, cache_mem_v, mem_prompt, swa_w_qkv, swa_sinks, swa_w_o, lru_w_in, lru_b_in, lru_conv_w, lru_conv_b, lru_w_a, lru_b_a, lru_w_i, lru_b_i, lru_lambda, lru_w_o, rwkv_mu, rwkv_w_r, rwkv_w_k, rwkv_w_v, rwkv_w0, rwkv_w1, rwkv_w2, rwkv_a0, rwkv_a1, rwkv_a2, rwkv_g1, rwkv_g2, rwkv_k_k, rwkv_k_a, rwkv_r_k, rwkv_gn_g, rwkv_gn_b, rwkv_w_o, mem_w_q, mem_w_kv, mem_w_o, ln_g, ln_b, router_w, router_b, moe_w_gate, moe_w_up, moe_w_down):
    n_p, seq, _ = x_prompt.shape
    n_s, dec_seq, _ = x_sample.shape
    assert dec_seq == 1
    past_len = 8192
    xp = x_prompt.reshape(n_p * seq, D)
    xs = x_sample.reshape(n_s, D)
    row = lambda v: v.reshape(1, -1)
    bf = lambda w: w.astype(BF16)

    rw_t = bf(router_w.T)
    rb = router_b.reshape(N_EXPERTS, 1)
    rw_pad = bf(jnp.pad(router_w, ((0, 0), (0, LANES - N_EXPERTS))))
    wg, wu, wd = bf(moe_w_gate), bf(moe_w_up), bf(moe_w_down)
    mem_p = mem_prompt.reshape(n_p * mem_prompt.shape[1], D)
    m_len = mem_prompt.shape[1]

    swa_k_p, swa_v_p, swa_k_s, swa_v_s = [], [], [], []
    lru_c_p, lru_h_p, lru_c_s, lru_h_s = [], [], [], []
    rw_x_p, rw_s_p, rw_x_s, rw_s_s = [], [], [], []
    mem_k_p, mem_v_p = [], []

    for layer in range(DEPTH):
        kind, i = layer % N_MIXERS, layer // N_MIXERS
        g0, b0 = row(ln_g[layer, 0]), row(ln_b[layer, 0])
        if kind == 0:
            w_qkv, w_o = bf(swa_w_qkv[i]), bf(swa_w_o[i])
            keep = min(WINDOW, seq)
            q, k, v, kv_last = _swa_qkv(xp, w_qkv, jnp.arange(seq), n_p, 512, keep, BF16)
            o = _swa_attn_prompt(q, k, v, swa_sinks[i], n_p, 2)
            swa_k_p.append(kv_last[:, :, :KV_WIDTH].reshape(n_p, keep, SWA_KV_HEADS, HEAD_DIM))
            swa_v_p.append(kv_last[:, :, KV_WIDTH:].reshape(n_p, keep, SWA_KV_HEADS, HEAD_DIM))
            xp = _proj_ln(o, w_o, xp, g0, b0, 512)

            qs, _, _, kv_new = _swa_qkv(xs, w_qkv, jnp.full((n_s,), past_len), 1, n_s, n_s, F32)
            kn, vn = kv_new[0, :, :KV_WIDTH], kv_new[0, :, KV_WIDTH:]
            os_ = _swa_attn_sample(qs, kn, vn, cache_swa_k[i], cache_swa_v[i], swa_sinks[i], 8)
            wb = cache_swa_k.shape[2]
            k_all = jnp.concatenate([cache_swa_k[i], kn.reshape(n_s, 1, SWA_KV_HEADS, HEAD_DIM)], axis=1)
            v_all = jnp.concatenate([cache_swa_v[i], vn.reshape(n_s, 1, SWA_KV_HEADS, HEAD_DIM)], axis=1)
            swa_k_s.append(k_all[:, -wb:])
            swa_v_s.append(v_all[:, -wb:])
            xs = _proj_ln(os_, w_o, xs, g0, b0, n_s)
        elif kind == 1:
            wts = _lru_weights(lru_w_in[i], lru_b_in[i], lru_conv_w[i], lru_conv_b[i], lru_w_a[i], lru_b_a[i],
                               lru_w_i[i], lru_b_i[i], lru_lambda[i], lru_w_o[i])
            xp, conv_last, h_last = _lru_prompt(xp, wts, g0, b0, n_p, 256)
            lru_c_p.append(conv_last[:, SUBLANES - (CONV_W - 1):])
            lru_h_p.append(h_last[:, SUBLANES - 1])
            xs, xb_s, h_s = _lru_sample(xs, state_lru_conv[i], state_lru_h[i], wts, g0, b0)
            lru_c_s.append(jnp.concatenate([state_lru_conv[i][:, 1:], xb_s[:, None]], axis=1))
            lru_h_s.append(h_s)
        else:
            wts = (rwkv_mu[i], bf(rwkv_w_r[i]), bf(rwkv_w_k[i]), bf(rwkv_w_v[i]), row(rwkv_w0[i]), bf(rwkv_w1[i]),
                   bf(rwkv_w2[i]), row(rwkv_a0[i]), bf(rwkv_a1[i]), bf(rwkv_a2[i]), bf(rwkv_g1[i]), bf(rwkv_g2[i]))
            hp = (row(rwkv_k_k[i]), row(rwkv_k_a[i]), row(rwkv_r_k[i]), row(rwkv_gn_g[i]), row(rwkv_gn_b[i]))
            w_o = bf(rwkv_w_o[i])
            rw_x_p.append(xp.reshape(n_p, seq, D)[:, -1])
            rw_x_s.append(xs)
            r, k, v, a, ld, g = _rwkv_pre(xp, jnp.zeros((n_p, SUBLANES, D), F32), wts, n_p, 256, True, BF16)
            o, st = _wkv_prompt(r, k, v, a, ld, g, hp, n_p)
            hd = RWKV_HD
            st = jnp.stack([st[:, :, :hd, :hd], st[:, :, hd:, hd:]], axis=2).reshape(n_p, RWKV_HEADS, hd, hd)
            rw_s_p.append(jnp.swapaxes(st, -1, -2))
            xp = _proj_ln(o, w_o, xp, g0, b0, 512)

            r, k, v, a, ld, g = _rwkv_pre(xs, state_rwkv_shift[i], wts, 1, n_s, False, F32)
            os_, s_new = _wkv_sample(r, k, v, a, ld, g, state_rwkv_wkv[i], hp, 8)
            rw_s_s.append(s_new)
            xs = _proj_ln(os_, w_o, xs, g0, b0, n_s)

        g1, b1 = row(ln_g[layer, 1]), row(ln_b[layer, 1])
        w_q, w_o = bf(mem_w_q[layer]), bf(mem_w_o[layer])
        mkv = _matmul(mem_p, bf(mem_w_kv[layer]), 512)
        mk, mv = mkv[:, :D], mkv[:, D:]
        mem_k_p.append(mk.reshape(n_p, m_len, MEM_HEADS, MEM_HD))
        mem_v_p.append(mv.reshape(n_p, m_len, MEM_HEADS, MEM_HD))
        xp = _mem_attn_prompt(xp, w_q, bf(mk).reshape(n_p, m_len, D), bf(mv).reshape(n_p, m_len, D), w_o, g1, b1,
                              n_p, 512)
        qs = _matmul(xs, w_q, n_s)
        os_ = _mem_attn_sample(qs, cache_mem_k, cache_mem_v, layer, 4)
        xs = _proj_ln(os_, w_o, xs, g1, b1, n_s)

        g2, b2 = row(ln_g[layer, 2]), row(ln_b[layer, 2])
        xp = _moe_ln(xp, rw_t, rb, rw_pad, wg, wu, wd, layer, g2, b2, 512, 256, 256)
        xs = _moe_ln_dense(xs, rw_t, rb, rw_pad, wg, wu, wd, layer, g2, b2)

    return (xp.reshape(n_p, seq, D), xs.reshape(n_s, 1, D),
            jnp.stack(swa_k_p), jnp.stack(swa_v_p), jnp.stack(lru_c_p), jnp.stack(lru_h_p),
            jnp.stack(rw_x_p), jnp.stack(rw_s_p), jnp.stack(mem_k_p), jnp.stack(mem_v_p),
            jnp.stack(swa_k_s), jnp.stack(swa_v_s), jnp.stack(lru_c_s), jnp.stack(lru_h_s),
            jnp.stack(rw_x_s), jnp.stack(rw_s_s))
```

```python
import functools

import jax
import jax.numpy as jnp
from jax import lax
from jax.experimental import pallas as pl
from jax.experimental.pallas import tpu as pltpu

F32 = jnp.float32
BF16 = jnp.bfloat16

D = 1024
DEPTH = 4
N_MIXERS = 3
HEAD_DIM = 64
SWA_HEADS = D // HEAD_DIM
SWA_KV_HEADS = 4
SWA_GROUP = SWA_HEADS // SWA_KV_HEADS
Q_WIDTH = SWA_HEADS * HEAD_DIM
KV_WIDTH = SWA_KV_HEADS * HEAD_DIM
WINDOW = 128
ROT_DIM = HEAD_DIM // 4
ROPE_THETA = 500000.0
LRU_BLOCKS = 16
CONV_W = 4
LRU_C = 8.0
RWKV_HEADS = 16
RWKV_HD = 64
RWKV_GN_EPS = 64e-5
MEM_HEADS = 4
MEM_HD = D // MEM_HEADS
N_EXPERTS = 16
N_GROUPS = 4
EXPERTS_PER_GROUP = 4
EXPERT_FF = 512
LN_EPS = 1e-5
ALPHA = (2.0 * DEPTH) ** 0.25
NEG_INF = -1e30

LANES = 128
SUBLANES = 8
VMEM_LIMIT = 56 * 1024 * 1024
WKV_CHUNK = 64
N_BUCKETS = N_GROUPS * 6
BUCKET_ROWS = 32


def _cparams(sem):
    return pltpu.CompilerParams(dimension_semantics=sem, vmem_limit_bytes=VMEM_LIMIT)


def _dot(a, b):
    return jnp.dot(a, b, preferred_element_type=F32)


def _dot_nt(a, b):
    return lax.dot_general(a, b, (((1,), (1,)), ((), ())), preferred_element_type=F32)


def _dot_tn(a, b):
    return lax.dot_general(a, b, (((0,), (0,)), ((), ())), preferred_element_type=F32)


def _ln(z, g, b):
    mu = jnp.mean(z, axis=-1, keepdims=True)
    zc = z - mu
    var = jnp.mean(zc * zc, axis=-1, keepdims=True)
    return zc * lax.rsqrt(var + LN_EPS) * g + b


def _softplus(z):
    return jnp.maximum(z, 0.0) + jnp.log1p(jnp.exp(-jnp.abs(z)))


def _sigmoid(z):
    return 1.0 / (1.0 + jnp.exp(-z))


def _round_bf16(x):
    return x.astype(BF16).astype(F32)


def _full(shape):
    nd = len(shape)
    return pl.BlockSpec(shape, lambda *_: (0,) * nd)


def _mm_kernel(a_ref, w_ref, o_ref):
    o_ref[...] = _dot(a_ref[...].astype(BF16), w_ref[...]).astype(o_ref.dtype)


def _matmul(a, w, tm, out_dtype=F32):
    t, k = a.shape
    n = w.shape[1]
    return pl.pallas_call(
        _mm_kernel, grid=(t // tm,),
        in_specs=[pl.BlockSpec((tm, k), lambda i: (i, 0)), _full((k, n))],
        out_specs=pl.BlockSpec((tm, n), lambda i: (i, 0)),
        out_shape=jax.ShapeDtypeStruct((t, n), out_dtype),
        compiler_params=_cparams(("parallel",)), name="matmul")(a, w)


def _proj_ln_kernel(a_ref, w_ref, x_ref, g_ref, b_ref, o_ref):
    acc = _dot(a_ref[...].astype(BF16), w_ref[...])
    o_ref[...] = _ln(ALPHA * x_ref[...] + acc, g_ref[...], b_ref[...])


def _proj_ln(a, w, x, g, b, tm):
    t, k = a.shape
    return pl.pallas_call(
        _proj_ln_kernel, grid=(t // tm,),
        in_specs=[pl.BlockSpec((tm, k), lambda i: (i, 0)), _full((k, D)),
                  pl.BlockSpec((tm, D), lambda i: (i, 0)), _full((1, D)), _full((1, D))],
        out_specs=pl.BlockSpec((tm, D), lambda i: (i, 0)),
        out_shape=jax.ShapeDtypeStruct((t, D), F32),
        compiler_params=_cparams(("parallel",)), name="proj_ln")(a, w, x, g, b)


def _rope_tables(pos):
    half = ROT_DIM // 2
    inv_freq = ROPE_THETA ** (-jnp.arange(half, dtype=F32) / half)
    ang = pos.astype(F32)[:, None] * inv_freq
    cos, sin = jnp.cos(ang), jnp.sin(ang)
    one = jnp.ones((pos.shape[0], HEAD_DIM - ROT_DIM), F32)
    zero = jnp.zeros((pos.shape[0], HEAD_DIM - ROT_DIM), F32)
    zh = jnp.zeros_like(sin)
    c = jnp.concatenate([cos, cos, one], axis=1)
    s1 = jnp.concatenate([-sin, zh, zero], axis=1)
    s2 = jnp.concatenate([zh, sin, zero], axis=1)
    rep = LANES // HEAD_DIM
    return jnp.tile(c, (1, rep)), jnp.tile(s1, (1, rep)), jnp.tile(s2, (1, rep))


def _swa_qkv_kernel(x_ref, w_ref, c_ref, s1_ref, s2_ref, q_ref, k_ref, v_ref, kv_ref, *, tm, keep):
    acc = _dot(x_ref[...].astype(BF16), w_ref[...])
    c, s1, s2 = c_ref[...], s1_ref[...], s2_ref[...]
    half = ROT_DIM // 2
    n_q = Q_WIDTH // LANES
    n_k = KV_WIDTH // LANES
    for cg in range(n_q + n_k):
        xg = acc[:, cg * LANES:(cg + 1) * LANES]
        rot = xg * c + pltpu.roll(xg, LANES - half, 1) * s1 + pltpu.roll(xg, half, 1) * s2
        if cg < n_q:
            q_ref[:, cg * LANES:(cg + 1) * LANES] = rot.astype(q_ref.dtype)
        else:
            ck = cg - n_q
            k_ref[:, ck * LANES:(ck + 1) * LANES] = rot.astype(k_ref.dtype)
            kv_ref[0, :, ck * LANES:(ck + 1) * LANES] = rot[tm - keep:, :]
    v = acc[:, Q_WIDTH + KV_WIDTH:]
    v_ref[...] = v.astype(v_ref.dtype)
    kv_ref[0, :, KV_WIDTH:] = v[tm - keep:, :]


def _swa_qkv(x, w_qkv, pos, n_seq, tm, keep, qdtype):
    t = x.shape[0]
    s = t // n_seq
    nb = s // tm
    c, s1, s2 = _rope_tables(pos)
    row = lambda n, i: (n * nb + i, 0)
    tab = pl.BlockSpec((tm, LANES), lambda n, i: (i, 0))
    kern = functools.partial(_swa_qkv_kernel, tm=tm, keep=keep)
    return pl.pallas_call(
        kern, grid=(n_seq, nb),
        in_specs=[pl.BlockSpec((tm, D), row), _full((D, Q_WIDTH + 2 * KV_WIDTH)), tab, tab, tab],
        out_specs=[pl.BlockSpec((tm, Q_WIDTH), row), pl.BlockSpec((tm, KV_WIDTH), row),
                   pl.BlockSpec((tm, KV_WIDTH), row),
                   pl.BlockSpec((1, keep, 2 * KV_WIDTH), lambda n, i: (n, 0, 0))],
        out_shape=[jax.ShapeDtypeStruct((t, Q_WIDTH), qdtype), jax.ShapeDtypeStruct((t, KV_WIDTH), qdtype),
                   jax.ShapeDtypeStruct((t, KV_WIDTH), qdtype),
                   jax.ShapeDtypeStruct((n_seq, keep, 2 * KV_WIDTH), F32)],
        compiler_params=_cparams(("parallel", "arbitrary")), name="swa_qkv")(x, w_qkv, c, s1, s2)


def _swa_attn_kernel(sink_ref, q_ref, kp_ref, kc_ref, vp_ref, vc_ref, o_ref, *, nq):
    j = pl.program_id(1)
    w, grp = WINDOW, SWA_GROUP
    r = lax.broadcasted_iota(jnp.int32, (grp * w, 2 * w), 0) % w
    c = lax.broadcasted_iota(jnp.int32, (grp * w, 2 * w), 1)
    in_prev = jnp.logical_and(c < w, c > r)
    in_cur = jnp.logical_and(c >= w, (c - w) <= r)
    ok_inner = jnp.logical_or(in_prev, in_cur)
    ok_first = jnp.logical_or(jnp.logical_and(in_prev, j > 0), in_cur)
    scale = HEAD_DIM ** -0.5
    combos = [(u, h) for u in range(nq) for h in range(SWA_KV_HEADS)]
    kcat, vcat, q4, sink, ok = [], [], [], [], []
    for u, h in combos:
        sl = slice(h * HEAD_DIM, (h + 1) * HEAD_DIM)
        rows = slice(u * w, (u + 1) * w)
        before = slice((u - 1) * w, u * w)
        k_prev = kp_ref[:, sl] if u == 0 else kc_ref[before, sl]
        v_prev = vp_ref[:, sl] if u == 0 else vc_ref[before, sl]
        kcat.append(jnp.concatenate([k_prev, kc_ref[rows, sl]], axis=0))
        vcat.append(jnp.concatenate([v_prev, vc_ref[rows, sl]], axis=0))
        heads = [h * grp + g for g in range(grp)]
        q4.append(jnp.concatenate([q_ref[rows, hq * HEAD_DIM:(hq + 1) * HEAD_DIM] for hq in heads], axis=0))
        sink.append(jnp.concatenate([jnp.full((w, 1), sink_ref[hq], F32) for hq in heads], axis=0))
        ok.append(ok_first if u == 0 else ok_inner)
    n = range(len(combos))
    s = [jnp.where(ok[i], _dot_nt(q4[i], kcat[i]) * scale, NEG_INF) for i in n]
    m = [jnp.maximum(jnp.max(s[i], axis=-1, keepdims=True), sink[i]) for i in n]
    p = [jnp.exp(s[i] - m[i]) for i in n]
    den = [jnp.sum(p[i], axis=-1, keepdims=True) + jnp.exp(sink[i] - m[i]) for i in n]
    o = [_dot((p[i] / den[i]).astype(BF16), vcat[i]) for i in n]
    for i, (u, h) in enumerate(combos):
        for g in range(grp):
            hq = h * grp + g
            o_ref[u * w:(u + 1) * w, hq * HEAD_DIM:(hq + 1) * HEAD_DIM] = o[i][g * w:(g + 1) * w].astype(o_ref.dtype)


def _swa_attn_prompt(q, k, v, sinks, n_seq, nq):
    t = q.shape[0]
    nb = t // n_seq // WINDOW
    ns = nb // nq
    cur = lambda n, j: (n * ns + j, 0)
    prev = lambda n, j: (n * nb + jnp.maximum(j * nq - 1, 0), 0)
    return pl.pallas_call(
        functools.partial(_swa_attn_kernel, nq=nq), grid=(n_seq, ns),
        in_specs=[pl.BlockSpec(memory_space=pltpu.SMEM), pl.BlockSpec((nq * WINDOW, Q_WIDTH), cur),
                  pl.BlockSpec((WINDOW, KV_WIDTH), prev), pl.BlockSpec((nq * WINDOW, KV_WIDTH), cur),
                  pl.BlockSpec((WINDOW, KV_WIDTH), prev), pl.BlockSpec((nq * WINDOW, KV_WIDTH), cur)],
        out_specs=pl.BlockSpec((nq * WINDOW, Q_WIDTH), cur),
        out_shape=jax.ShapeDtypeStruct((t, Q_WIDTH), BF16),
        compiler_params=_cparams(("parallel", "arbitrary")), name="swa_attn")(sinks, q, k, k, v, v)


def _swa_sample_kernel(sink_ref, q_ref, kn_ref, vn_ref, ck_ref, cv_ref, o_ref, *, bs):
    wb, nkv, hd = ck_ref.shape[1:]
    nq = q_ref.shape[1]
    npad = kn_ref.shape[1]
    row_kv = lax.broadcasted_iota(jnp.int32, (nq, wb * nkv), 0) // SWA_GROUP
    col = lax.broadcasted_iota(jnp.int32, (nq, wb * nkv), 1)
    valid = jnp.logical_and(col % nkv == row_kv, (wb - col // nkv) < WINDOW)
    own_new = (lax.broadcasted_iota(jnp.int32, (nq, npad), 1)
               == lax.broadcasted_iota(jnp.int32, (nq, npad), 0) // SWA_GROUP)
    sink = sink_ref[...]
    scale = HEAD_DIM ** -0.5
    nb = range(bs)
    qb = [q_ref[b].astype(BF16) for b in nb]
    s = [jnp.where(valid, _dot_nt(qb[b], ck_ref[b].reshape(wb * nkv, hd).astype(BF16)) * scale, NEG_INF) for b in nb]
    sn = [jnp.where(own_new, _dot_nt(qb[b], kn_ref[b].astype(BF16)) * scale, NEG_INF) for b in nb]
    m = [jnp.maximum(jnp.maximum(jnp.max(s[b], axis=-1, keepdims=True), jnp.max(sn[b], axis=-1, keepdims=True)), sink)
         for b in nb]
    p = [jnp.where(valid, jnp.exp(s[b] - m[b]), 0.0) for b in nb]
    pn = [jnp.where(own_new, jnp.exp(sn[b] - m[b]), 0.0) for b in nb]
    den = [jnp.sum(p[b], axis=-1, keepdims=True) + jnp.sum(pn[b], axis=-1, keepdims=True) + jnp.exp(sink - m[b])
           for b in nb]
    for b in nb:
        o_ref[b] = (_dot((p[b] / den[b]).astype(BF16), cv_ref[b].reshape(wb * nkv, hd).astype(BF16))
                    + _dot((pn[b] / den[b]).astype(BF16), vn_ref[b].astype(BF16)))


def _swa_attn_sample(q, kn, vn, cache_k, cache_v, layer, sinks, bs):
    _, b, wb, nkv, hd = cache_k.shape
    pad = lambda z: jnp.pad(z.reshape(b, nkv, hd), ((0, 0), (0, SUBLANES - nkv), (0, 0)))
    qblk = pl.BlockSpec((bs, SWA_HEADS, hd), lambda i: (i, 0, 0))
    nblk = pl.BlockSpec((bs, SUBLANES, hd), lambda i: (i, 0, 0))
    cblk = pl.BlockSpec((None, bs, wb, nkv, hd), lambda i: (layer, i, 0, 0, 0))
    out = pl.pallas_call(
        functools.partial(_swa_sample_kernel, bs=bs), grid=(b // bs,),
        in_specs=[_full((SWA_HEADS, 1)), qblk, nblk, nblk, cblk, cblk],
        out_specs=qblk, out_shape=jax.ShapeDtypeStruct((b, SWA_HEADS, hd), F32),
        compiler_params=_cparams(("parallel",)), name="swa_sample")(
            sinks.reshape(SWA_HEADS, 1), q.reshape(b, SWA_HEADS, hd), pad(kn), pad(vn), cache_k, cache_v)
    return out.reshape(b, Q_WIDTH)


def _gelu_tanh(x):
    return 0.5 * x * (1.0 + jnp.tanh(0.7978845608028654 * (x + 0.044715 * x * x * x)))


def _lru_gates(xc, wa_ref, ba, wi_ref, bi, lam):
    xcb = xc.astype(BF16)
    gw = wa_ref.shape[1]
    ra, ia = [], []
    for gi in range(wa_ref.shape[0]):
        xs = xcb[:, gi * gw:(gi + 1) * gw]
        ra.append(_dot(xs, wa_ref[gi]))
        ia.append(_dot(xs, wi_ref[gi]))
    r = _sigmoid(jnp.concatenate(ra, axis=-1) + ba)
    ig = _sigmoid(jnp.concatenate(ia, axis=-1) + bi)
    log_a = -LRU_C * r * _softplus(-lam)
    a = jnp.exp(log_a)
    b = jnp.sqrt(-jnp.tanh(log_a) * (a * a + 1.0)) * (ig * xc)
    return a, b


def _shift_rows(ext, s, tm):
    return pltpu.roll(ext, s, 0)[SUBLANES:SUBLANES + tm]


def _lru_prompt_kernel(x_ref, win_ref, bin_ref, cw_ref, cb_ref, wa_ref, ba_ref, wi_ref, bi_ref, lam_ref,
                       wo_ref, g_ref, b_ref, o_ref, conv_ref, hl_ref, cx_ref, ch_ref, *, tm):
    i = pl.program_id(1)

    @pl.when(i == 0)
    def _():
        cx_ref[...] = jnp.zeros_like(cx_ref)
        ch_ref[...] = jnp.zeros_like(ch_ref)

    x = x_ref[...]
    xy = _dot(x.astype(BF16), win_ref[...]) + bin_ref[...]
    xb = xy[:, :D]
    y_gate = _gelu_tanh(xy[:, D:])
    ext = jnp.concatenate([cx_ref[...], xb], axis=0)
    cw = cw_ref[...]
    xc = cb_ref[...] + xb * cw[CONV_W - 1:CONV_W]
    for s in range(1, CONV_W):
        xc = xc + _shift_rows(ext, s, tm) * cw[CONV_W - 1 - s:CONV_W - s]
    cx_ref[...] = xb[tm - SUBLANES:]
    conv_ref[0] = xb[tm - SUBLANES:]

    a, b = _lru_gates(xc, wa_ref, ba_ref[...], wi_ref, bi_ref[...], lam_ref[...])
    sub = lax.broadcasted_iota(jnp.int32, (tm, 1), 0) % SUBLANES
    s = 1
    while s < SUBLANES:
        keep = sub >= s
        a_sh = jnp.where(keep, pltpu.roll(a, s, 0), 1.0)
        b_sh = jnp.where(keep, pltpu.roll(b, s, 0), 0.0)
        b = a * b_sh + b
        a = a * a_sh
        s *= 2
    carry = ch_ref[SUBLANES - 1:SUBLANES, :]
    groups = []
    for gi in range(tm // SUBLANES):
        rows = slice(gi * SUBLANES, (gi + 1) * SUBLANES)
        hg = a[rows] * carry + b[rows]
        groups.append(hg)
        carry = hg[SUBLANES - 1:SUBLANES]
    h = jnp.concatenate(groups, axis=0)
    ch_ref[...] = h[tm - SUBLANES:]
    hl_ref[0] = h[tm - SUBLANES:]
    acc = _dot((h * y_gate).astype(BF16), wo_ref[...])
    o_ref[...] = _ln(ALPHA * x + acc, g_ref[...], b_ref[...])


def _lru_weights(w_in, b_in, conv_w, conv_b, w_a, b_a, w_i, b_i, lam, w_o):
    gsz = 4
    ng = LRU_BLOCKS // gsz
    bw = D // LRU_BLOCKS

    def grouped(w):
        w4 = w.reshape(ng, gsz, bw, bw)
        return jnp.einsum('gaij,ab->gaibj', w4, jnp.eye(gsz, dtype=w.dtype)).reshape(ng, gsz * bw, gsz * bw).astype(BF16)

    row = lambda v: v.reshape(1, -1)
    return (w_in.astype(BF16), row(b_in), conv_w, row(conv_b), grouped(w_a), row(b_a), grouped(w_i), row(b_i),
            row(lam), w_o.astype(BF16))


def _lru_prompt(x, wts, g, b, n_seq, tm):
    t = x.shape[0]
    nb = t // n_seq // tm
    row = lambda n, i: (n * nb + i, 0)
    last = pl.BlockSpec((1, SUBLANES, D), lambda n, i: (n, 0, 0))
    w_in, b_in, cw, cb, wa, ba, wi, bi, lam, wo = wts
    return pl.pallas_call(
        functools.partial(_lru_prompt_kernel, tm=tm), grid=(n_seq, nb),
        in_specs=[pl.BlockSpec((tm, D), row), _full(w_in.shape), _full(b_in.shape), _full(cw.shape), _full(cb.shape),
                  _full(wa.shape), _full(ba.shape), _full(wi.shape), _full(bi.shape), _full(lam.shape),
                  _full(wo.shape), _full((1, D)), _full((1, D))],
        out_specs=[pl.BlockSpec((tm, D), row), last, last],
        out_shape=[jax.ShapeDtypeStruct((t, D), F32), jax.ShapeDtypeStruct((n_seq, SUBLANES, D), F32),
                   jax.ShapeDtypeStruct((n_seq, SUBLANES, D), F32)],
        scratch_shapes=[pltpu.VMEM((SUBLANES, D), F32), pltpu.VMEM((SUBLANES, D), F32)],
        compiler_params=_cparams(("parallel", "arbitrary")), name="lru_prompt")(x, *wts, g, b)


def _lru_sample_kernel(x_ref, c0_ref, c1_ref, c2_ref, h0_ref, win_ref, bin_ref, cw_ref, cb_ref, wa_ref, ba_ref,
                       wi_ref, bi_ref, lam_ref, wo_ref, g_ref, b_ref, o_ref, xb_ref, h_ref):
    x = x_ref[...]
    xy = _dot(x.astype(BF16), win_ref[...]) + bin_ref[...]
    xb = xy[:, :D]
    y_gate = _gelu_tanh(xy[:, D:])
    cw = cw_ref[...]
    xc = (cb_ref[...] + c0_ref[...] * cw[0:1] + c1_ref[...] * cw[1:2] + c2_ref[...] * cw[2:3] + xb * cw[3:4])
    a, b = _lru_gates(xc, wa_ref, ba_ref[...], wi_ref, bi_ref[...], lam_ref[...])
    h = a * h0_ref[...] + b
    xb_ref[...] = xb
    h_ref[...] = h
    acc = _dot((h * y_gate).astype(BF16), wo_ref[...])
    o_ref[...] = _ln(ALPHA * x + acc, g_ref[...], b_ref[...])


def _lru_sample(x, conv_state, h0, wts, g, b):
    t = x.shape[0]
    args = (x, conv_state[:, 0], conv_state[:, 1], conv_state[:, 2], h0, *wts, g, b)
    sd = jax.ShapeDtypeStruct((t, D), F32)
    return pl.pallas_call(
        _lru_sample_kernel, grid=(1,),
        in_specs=[_full(a.shape) for a in args],
        out_specs=[_full((t, D))] * 3, out_shape=[sd, sd, sd],
        compiler_params=_cparams(("arbitrary",)), name="lru_sample")(*args)


def _rwkv_pre_kernel(x_ref, xp_ref, mu_ref, wr_ref, wk_ref, wv_ref, w0_ref, w1_ref, w2_ref, a0_ref, a1_ref, a2_ref,
                     g1_ref, g2_ref, r_ref, k_ref, v_ref, a_ref, ld_ref, g_ref, *scratch, tm, seq):
    x = x_ref[...]
    if seq:
        cx_ref, = scratch
        i = pl.program_id(1)

        @pl.when(i == 0)
        def _():
            cx_ref[...] = xp_ref[0]

        x_prev = _shift_rows(jnp.concatenate([cx_ref[...], x], axis=0), 1, tm)
        cx_ref[...] = x[tm - SUBLANES:]
    else:
        x_prev = xp_ref[...]
    xx = x_prev - x
    mu = mu_ref[...]
    mix = lambda j: (x + xx * mu[j:j + 1]).astype(BF16)
    r_ref[...] = _dot(mix(0), wr_ref[...]).astype(r_ref.dtype)
    wl = _dot(jnp.tanh(_dot(mix(1), w1_ref[...])).astype(BF16), w2_ref[...])
    w = -_softplus(-(w0_ref[...] + wl)) - 0.5
    ld_ref[...] = -jnp.exp(w)
    k_ref[...] = _dot(mix(2), wk_ref[...]).astype(k_ref.dtype)
    v_ref[...] = _dot(mix(3), wv_ref[...]).astype(v_ref.dtype)
    al = _dot(_dot(mix(4), a1_ref[...]).astype(BF16), a2_ref[...])
    a_ref[...] = _sigmoid(a0_ref[...] + al).astype(a_ref.dtype)
    g_ref[...] = _dot(_sigmoid(_dot(mix(5), g1_ref[...])).astype(BF16), g2_ref[...]).astype(g_ref.dtype)


def _rwkv_pre(x, x_prev, wts, n_seq, tm, seq, dtype):
    t = x.shape[0]
    nb = t // n_seq // tm
    row = lambda n, i: (n * nb + i, 0)
    xp_spec = pl.BlockSpec((1, SUBLANES, D), lambda n, i: (n, 0, 0)) if seq else pl.BlockSpec((tm, D), row)
    sd = lambda dt: jax.ShapeDtypeStruct((t, D), dt)
    blk = pl.BlockSpec((tm, D), row)
    return pl.pallas_call(
        functools.partial(_rwkv_pre_kernel, tm=tm, seq=seq), grid=(n_seq, nb),
        in_specs=[blk, xp_spec] + [_full(w.shape) for w in wts],
        out_specs=[blk] * 6,
        out_shape=[sd(dtype), sd(dtype), sd(dtype), sd(dtype), sd(F32), sd(dtype)],
        scratch_shapes=[pltpu.VMEM((SUBLANES, D), F32)] if seq else [],
        compiler_params=_cparams(("parallel", "arbitrary")), name="rwkv_pre")(x, x_prev, *wts)


def _seg_sum(x, first):
    s0 = jnp.sum(jnp.where(first, x, 0.0), axis=-1, keepdims=True)
    s1 = jnp.sum(jnp.where(first, 0.0, x), axis=-1, keepdims=True)
    return jnp.where(first, s0, s1)


def _wkv_kernel(r_ref, k_ref, v_ref, a_ref, ld_ref, g_ref, kk_ref, ka_ref, rk_ref, gg_ref, gb_ref,
                o_ref, s_ref, st_ref):
    c = pl.program_id(1)
    L = WKV_CHUNK
    P2 = 2 * L

    @pl.when(c == 0)
    def _():
        st_ref[...] = jnp.zeros_like(st_ref)

    ld_all = ld_ref[...]
    tri = (lax.broadcasted_iota(jnp.int32, (L, L), 0) >= lax.broadcasted_iota(jnp.int32, (L, L), 1)).astype(BF16)
    hi = ld_all.astype(BF16)
    r1 = ld_all - hi.astype(F32)
    mid = r1.astype(BF16)
    lo = (r1 - mid.astype(F32)).astype(BF16)
    cum_all = _dot(tri, hi) + _dot(tri, mid) + _dot(tri, lo)

    lane = lax.broadcasted_iota(jnp.int32, (1, LANES), 1)
    first = lane < RWKV_HD
    ri = lax.broadcasted_iota(jnp.int32, (P2, P2), 0)
    ci = lax.broadcasted_iota(jnp.int32, (P2, P2), 1)
    same_head = (ri // L) == (ci // L)
    rt, ct = ri % L, ci % L
    strict = jnp.logical_and(same_head, rt > ct)
    incl = jnp.logical_and(same_head, rt >= ct)
    eye = ri == ci

    def stack(xv):
        return jnp.concatenate([jnp.where(first, xv, 0.0), jnp.where(first, 0.0, xv)], axis=0).astype(BF16)

    pairs = range(RWKV_HEADS // 2)
    sls = [slice(p * LANES, (p + 1) * LANES) for p in pairs]
    ws, us, ks, rs, ul, kl, vs, g_l, bonus = ([] for _ in range(9))
    for sl in sls:
        rp, kp, vp, ap = (ref[:, sl].astype(F32) for ref in (r_ref, k_ref, v_ref, a_ref))
        ldp, cum = ld_all[:, sl], cum_all[:, sl]
        kk = kp * kk_ref[:, sl]
        kk = kk / jnp.maximum(jnp.sqrt(_seg_sum(kk * kk, first)), 1e-12)
        kmod = kp * (1.0 + (ap - 1.0) * ka_ref[:, sl])
        bp = kk * ap
        cum_l = cum[L - 1:L, :]
        g_inv = jnp.exp(-cum)
        g_to_end = jnp.exp(cum_l - cum)
        ws.append(stack(kk * jnp.exp(cum - ldp)))
        us.append(stack(bp * g_inv))
        ks.append(stack(kmod * g_inv))
        rs.append(stack(rp * jnp.exp(cum)))
        ul.append(stack(bp * g_to_end))
        kl.append(stack(kmod * g_to_end))
        vs.append(stack(vp))
        g_l.append(jnp.exp(cum_l))
        bonus.append(_seg_sum(rp * kmod * rk_ref[:, sl], first) * vp)

    gram = [_dot_nt(jnp.concatenate([ws[p], rs[p]], axis=0), jnp.concatenate([us[p], ks[p]], axis=0)) for p in pairs]
    n_mat = [jnp.where(strict, gram[p][:P2, :P2], 0.0) for p in pairs]
    m_mat = [jnp.where(strict, gram[p][:P2, P2:], 0.0).astype(BF16) for p in pairs]
    nr_mat = [jnp.where(incl, gram[p][P2:, :P2], 0.0).astype(BF16) for p in pairs]
    mr_mat = [jnp.where(incl, gram[p][P2:, P2:], 0.0).astype(BF16) for p in pairs]

    def level_mask(sz):
        sub = jnp.logical_and((rt // sz) % 2 == 1, (ct // sz) % 2 == 0)
        return jnp.logical_and(jnp.logical_and(sub, (rt // (2 * sz)) == (ct // (2 * sz))), same_head)

    x_inv = [jnp.where(eye, 1.0, 0.0) - jnp.where(level_mask(1), n_mat[p], 0.0) for p in pairs]
    sz = 2
    while sz < L:
        mask = level_mask(sz)
        xb = [x_inv[p].astype(BF16) for p in pairs]
        xc = [_dot(xb[p], jnp.where(mask, n_mat[p], 0.0).astype(BF16)).astype(BF16) for p in pairs]
        x_inv = [x_inv[p] - _dot(xc[p], xb[p]) for p in pairs]
        sz *= 2

    a0 = [st_ref[p] for p in pairs]
    a0b = [a0[p].astype(BF16) for p in pairs]
    rhs = [_dot(jnp.concatenate([ws[p], m_mat[p]], axis=1), jnp.concatenate([a0b[p], vs[p]], axis=0)).astype(BF16)
           for p in pairs]
    pm = [(-_dot(x_inv[p].astype(BF16), rhs[p])).astype(BF16) for p in pairs]
    o_st = [_dot(jnp.concatenate([rs[p], nr_mat[p], mr_mat[p]], axis=1),
                 jnp.concatenate([a0b[p], pm[p], vs[p]], axis=0)) for p in pairs]
    for p in pairs:
        g_col = jnp.sum(jnp.where(eye, jnp.broadcast_to(g_l[p], (P2, P2)), 0.0), axis=-1, keepdims=True)
        st_ref[p] = g_col * a0[p] + _dot_tn(jnp.concatenate([ul[p], kl[p]], axis=0),
                                            jnp.concatenate([pm[p], vs[p]], axis=0))

    inv_n = 1.0 / RWKV_HD
    for p, sl in zip(pairs, sls):
        o = o_st[p][:L] + o_st[p][L:]
        mu = _seg_sum(o, first) * inv_n
        oc = o - mu
        var = _seg_sum(oc * oc, first) * inv_n
        on = oc * lax.rsqrt(var + RWKV_GN_EPS) * gg_ref[:, sl] + gb_ref[:, sl]
        o_ref[:, sl] = ((on + bonus[p]) * g_ref[:, sl].astype(F32)).astype(o_ref.dtype)

    s_ref[0] = st_ref[...]


def _wkv_prompt(r, k, v, a, ld, g, hp, n_seq):
    t = r.shape[0]
    L = WKV_CHUNK
    nc = t // n_seq // L
    row = lambda n, c: (n * nc + c, 0)
    blk = pl.BlockSpec((L, D), row)
    npair = RWKV_HEADS // 2
    return pl.pallas_call(
        _wkv_kernel, grid=(n_seq, nc),
        in_specs=[blk] * 6 + [_full((1, D))] * 5,
        out_specs=[blk, pl.BlockSpec((1, npair, LANES, LANES), lambda n, c: (n, 0, 0, 0))],
        out_shape=[jax.ShapeDtypeStruct((t, D), BF16), jax.ShapeDtypeStruct((n_seq, npair, LANES, LANES), F32)],
        scratch_shapes=[pltpu.VMEM((npair, LANES, LANES), F32)],
        compiler_params=_cparams(("parallel", "arbitrary")), name="wkv_chunk")(r, k, v, a, ld, g, *hp)


def _wkv_sample_kernel(r_ref, k_ref, v_ref, a_ref, ld_ref, g_ref, s_ref, kk_ref, ka_ref, rk_ref, gg_ref, gb_ref,
                       o_ref, so_ref):
    hd = RWKV_HD
    eye = lax.broadcasted_iota(jnp.int32, (1, hd, hd), 1) == lax.broadcasted_iota(jnp.int32, (1, hd, hd), 2)
    for h in range(RWKV_HEADS):
        hs = slice(h, h + 1)
        r, k, v, a, ld, g = (ref[:, hs, :] for ref in (r_ref, k_ref, v_ref, a_ref, ld_ref, g_ref))
        s = s_ref[:, h]
        kk = k * kk_ref[hs, :]
        kk = kk / jnp.maximum(jnp.sqrt(jnp.sum(kk * kk, axis=-1, keepdims=True)), 1e-12)
        kmod = k * (1.0 + (a - 1.0) * ka_ref[hs, :])
        skk = jnp.sum(s * kk, axis=-1, keepdims=True)
        v_col = jnp.sum(jnp.where(eye, v, 0.0), axis=-1, keepdims=True)
        s_new = s * jnp.exp(ld) - skk * (kk * a) + v_col * kmod
        so_ref[:, h] = s_new
        o_col = jnp.sum(s_new * r, axis=-1, keepdims=True)
        o = jnp.sum(jnp.where(eye, o_col, 0.0), axis=1, keepdims=True)
        mu = jnp.mean(o, axis=-1, keepdims=True)
        oc = o - mu
        var = jnp.mean(oc * oc, axis=-1, keepdims=True)
        on = oc * lax.rsqrt(var + RWKV_GN_EPS) * gg_ref[hs, :] + gb_ref[hs, :]
        bonus = jnp.sum(r * kmod * rk_ref[hs, :], axis=-1, keepdims=True) * v
        o_ref[:, hs, :] = (on + bonus) * g


def _wkv_sample(r, k, v, a, ld, g, state, hp, bs):
    b = r.shape[0]
    h3 = lambda z: z.reshape(b, RWKV_HEADS, RWKV_HD)
    blk = pl.BlockSpec((bs, RWKV_HEADS, RWKV_HD), lambda i: (i, 0, 0))
    sblk = pl.BlockSpec((bs, RWKV_HEADS, RWKV_HD, RWKV_HD), lambda i: (i, 0, 0, 0))
    hp3 = [z.reshape(RWKV_HEADS, RWKV_HD) for z in hp]
    o, s_new = pl.pallas_call(
        _wkv_sample_kernel, grid=(b // bs,),
        in_specs=[blk] * 6 + [sblk] + [_full((RWKV_HEADS, RWKV_HD))] * 5,
        out_specs=[blk, sblk],
        out_shape=[jax.ShapeDtypeStruct((b, RWKV_HEADS, RWKV_HD), F32), jax.ShapeDtypeStruct(state.shape, F32)],
        compiler_params=_cparams(("parallel",)), name="wkv_sample")(
            h3(r), h3(k), h3(v), h3(a), h3(ld), h3(g), state, *hp3)
    return o.reshape(b, D), s_new


def _mem_prompt_kernel(x_ref, wq_ref, mk_ref, mv_ref, wo_ref, g_ref, b_ref, o_ref):
    x = x_ref[...]
    q = _dot(x.astype(BF16), wq_ref[...]).astype(BF16)
    scale = MEM_HD ** -0.5
    outs = []
    for h in range(MEM_HEADS):
        sl = slice(h * MEM_HD, (h + 1) * MEM_HD)
        s = _dot_nt(q[:, sl], mk_ref[0, :, sl]) * scale
        p = jnp.exp(s - jnp.max(s, axis=-1, keepdims=True))
        den = jnp.sum(p, axis=-1, keepdims=True)
        outs.append(_dot((p / den).astype(BF16), mv_ref[0, :, sl]).astype(BF16))
    acc = _dot(jnp.concatenate(outs, axis=-1), wo_ref[...])
    o_ref[...] = _ln(ALPHA * x + acc, g_ref[...], b_ref[...])


def _mem_attn_prompt(x, w_q, mk, mv, w_o, g, b, n_seq, tm):
    t = x.shape[0]
    nb = t // n_seq // tm
    m = mk.shape[1]
    row = lambda n, i: (n * nb + i, 0)
    mem = pl.BlockSpec((1, m, D), lambda n, i: (n, 0, 0))
    return pl.pallas_call(
        _mem_prompt_kernel, grid=(n_seq, nb),
        in_specs=[pl.BlockSpec((tm, D), row), _full((D, D)), mem, mem, _full((D, D)), _full((1, D)), _full((1, D))],
        out_specs=pl.BlockSpec((tm, D), row), out_shape=jax.ShapeDtypeStruct((t, D), F32),
        compiler_params=_cparams(("parallel", "arbitrary")), name="mem_attn")(x, w_q, mk, mv, w_o, g, b)


def _mem_sample_kernel(q_ref, ck_ref, cv_ref, o_ref, *, bs):
    m, nh, hd = ck_ref.shape[1:]
    rows = q_ref.shape[1]
    col_head = lax.broadcasted_iota(jnp.int32, (rows, m * nh), 1) % nh
    own = col_head == lax.broadcasted_iota(jnp.int32, (rows, m * nh), 0)
    scale = MEM_HD ** -0.5
    nb = range(bs)
    s = [jnp.where(own, _dot_nt(q_ref[b].astype(BF16), ck_ref[b].reshape(m * nh, hd).astype(BF16)) * scale, NEG_INF)
         for b in nb]
    p = [jnp.where(own, jnp.exp(s[b] - jnp.max(s[b], axis=-1, keepdims=True)), 0.0) for b in nb]
    den = [jnp.sum(p[b], axis=-1, keepdims=True) for b in nb]
    for b in nb:
        pb = (p[b] / jnp.where(den[b] > 0.0, den[b], 1.0)).astype(BF16)
        o_ref[b] = _dot(pb, cv_ref[b].reshape(m * nh, hd).astype(BF16))


def _mem_attn_sample(q, cache_k, cache_v, layer, bs):
    _, b, m, nh, hd = cache_k.shape
    q3 = jnp.pad(q.reshape(b, nh, hd), ((0, 0), (0, SUBLANES - nh), (0, 0)))
    qb = pl.BlockSpec((bs, SUBLANES, hd), lambda i: (i, 0, 0))
    cb = pl.BlockSpec((None, bs, m, nh, hd), lambda i: (layer, i, 0, 0, 0))
    out = pl.pallas_call(
        functools.partial(_mem_sample_kernel, bs=bs), grid=(b // bs,), in_specs=[qb, cb, cb], out_specs=qb,
        out_shape=jax.ShapeDtypeStruct((b, SUBLANES, hd), F32),
        compiler_params=_cparams(("parallel",)), name="mem_sample")(q3, cache_k, cache_v)
    return out[:, :nh].reshape(b, D)


_PAIRS = ((0, 1), (0, 2), (0, 3), (1, 2), (1, 3), (2, 3))


def _router_kernel(x_ref, rw_ref, rb_ref, bucket_ref, rank_ref, cnt_ref, base_ref, *, tm):
    i = pl.program_id(0)

    @pl.when(i == 0)
    def _():
        base_ref[...] = jnp.zeros_like(base_ref)

    logits = _dot_nt(rw_ref[...], x_ref[...].astype(BF16))
    e = jnp.exp(logits - jnp.max(logits, axis=0, keepdims=True))
    sel = e / jnp.sum(e, axis=0, keepdims=True) + rb_ref[...]
    s = [sel[j:j + 1, :] for j in range(N_EXPERTS)]
    neg = jnp.float32(-jnp.inf)

    best = jnp.zeros((1, tm), jnp.int32)
    best_score = None
    for gi in range(N_GROUPS):
        s0, s1, s2, s3 = s[4 * gi:4 * gi + 4]
        hi01, lo01, hi23, lo23 = jnp.maximum(s0, s1), jnp.minimum(s0, s1), jnp.maximum(s2, s3), jnp.minimum(s2, s3)
        score = jnp.maximum(hi01, hi23) + jnp.maximum(jnp.minimum(hi01, hi23), jnp.maximum(lo01, lo23))
        if gi == 0:
            best_score = score
        else:
            take = score > best_score
            best = jnp.where(take, gi, best)
            best_score = jnp.where(take, score, best_score)
    vals = []
    for j in range(EXPERTS_PER_GROUP):
        vj = s[j]
        for gi in range(1, N_GROUPS):
            vj = jnp.where(best == gi, s[4 * gi + j], vj)
        vals.append(vj)

    def argmax4(v):
        idx, mx = jnp.zeros((1, tm), jnp.int32), v[0]
        for j in range(1, EXPERTS_PER_GROUP):
            take = v[j] > mx
            idx = jnp.where(take, j, idx)
            mx = jnp.where(take, v[j], mx)
        return idx

    i1 = argmax4(vals)
    i2 = argmax4([jnp.where(i1 == j, neg, vals[j]) for j in range(EXPERTS_PER_GROUP)])
    lo, hi = jnp.minimum(i1, i2), jnp.maximum(i1, i2)
    pair = jnp.zeros((1, tm), jnp.int32)
    for pi, (pa, pb) in enumerate(_PAIRS):
        pair = jnp.where(jnp.logical_and(lo == pa, hi == pb), pi, pair)
    bucket = best * len(_PAIRS) + pair
    bucket_ref[0] = bucket

    onehot = (lax.broadcasted_iota(jnp.int32, (BUCKET_ROWS, tm), 0) == bucket).astype(F32)
    upper = (lax.broadcasted_iota(jnp.int32, (tm, tm), 0) <= lax.broadcasted_iota(jnp.int32, (tm, tm), 1)).astype(BF16)
    cum = _dot(onehot.astype(BF16), upper)
    base = base_ref[...]
    rank = jnp.sum(onehot * (cum + base), axis=0, keepdims=True) - 1.0
    rank_ref[0] = rank.astype(jnp.int32)
    base = base + jnp.sum(onehot, axis=1, keepdims=True)
    base_ref[...] = base
    cnt_ref[...] = jnp.broadcast_to(base, cnt_ref.shape)


def _router(x, rw_t, rb, tm):
    t = x.shape[0]
    nb = t // tm
    ib = pl.BlockSpec((1, 1, tm), lambda i: (i, 0, 0))
    bucket, rank, cnt = pl.pallas_call(
        functools.partial(_router_kernel, tm=tm), grid=(nb,),
        in_specs=[pl.BlockSpec((tm, D), lambda i: (i, 0)), _full(rw_t.shape), _full(rb.shape)],
        out_specs=[ib, ib, _full((BUCKET_ROWS, LANES))],
        out_shape=[jax.ShapeDtypeStruct((nb, 1, tm), jnp.int32), jax.ShapeDtypeStruct((nb, 1, tm), jnp.int32),
                   jax.ShapeDtypeStruct((BUCKET_ROWS, LANES), F32)],
        scratch_shapes=[pltpu.VMEM((BUCKET_ROWS, 1), F32)],
        compiler_params=_cparams(("arbitrary",)), name="router")(x, rw_t, rb)
    return bucket.reshape(t), rank.reshape(t), cnt[:N_BUCKETS, 0].astype(jnp.int32)


def _row_copies(idx_ref, base, src_hbm, dst, sem, n, wait):
    n_prio = 2

    def body(j, carry):
        for k in range(n_prio):
            r = j * n_prio + k
            cp = pltpu.make_async_copy(src_hbm.at[pl.ds(idx_ref[base + r], 1)], dst.at[pl.ds(r, 1)], sem)
            if wait:
                cp.wait()
            else:
                cp.start(priority=k)
        return carry

    assert n % n_prio == 0
    lax.fori_loop(0, n // n_prio, body, 0, unroll=4)


def _ffn_kernel(src_ref, lo_ref, hi_ref, used_ref, x_hbm, rw_ref, g0_ref, u0_ref, d0_ref, g1_ref, u1_ref, d1_ref,
                o_ref, xbuf, sem, *, blk):
    i = pl.program_id(0)
    used = used_ref[0]
    slot = i % 2

    @pl.when(jnp.logical_and(i == 0, used > 0))
    def _():
        _row_copies(src_ref, 0, x_hbm, xbuf.at[0], sem.at[0], blk, False)

    @pl.when(i + 1 < used)
    def _():
        _row_copies(src_ref, (i + 1) * blk, x_hbm, xbuf.at[1 - slot], sem.at[1 - slot], blk, False)

    @pl.when(i < used)
    def _():
        _row_copies(src_ref, i * blk, x_hbm, xbuf.at[slot], sem.at[slot], blk, True)
        xb = xbuf[slot].astype(BF16)
        logits = _dot(xb, rw_ref[...])
        lane = lax.broadcasted_iota(jnp.int32, logits.shape, 1)
        l_lo = jnp.sum(jnp.where(lane == lo_ref[i], logits, 0.0), axis=-1, keepdims=True)
        l_hi = jnp.sum(jnp.where(lane == hi_ref[i], logits, 0.0), axis=-1, keepdims=True)
        w_lo = _sigmoid(l_lo - l_hi)

        def expert(g_ref, u_ref, d_ref):
            gate = _dot(xb, g_ref[0])
            act = gate * _sigmoid(gate) * _dot(xb, u_ref[0])
            return _dot(act.astype(BF16), d_ref[0])

        y_lo = expert(g0_ref, u0_ref, d0_ref)
        y_hi = expert(g1_ref, u1_ref, d1_ref)
        o_ref[...] = w_lo * y_lo + (1.0 - w_lo) * y_hi

    @pl.when(i >= used)
    def _():
        o_ref[...] = jnp.zeros_like(o_ref)


def _ffn(x, src, blk_lo, blk_hi, n_used, rw, w_gate, w_up, w_down, layer, blk):
    rows = src.shape[0]
    nblk = rows // blk
    wg = lambda sel: pl.BlockSpec((None, 1, D, EXPERT_FF),
                                  lambda i, s, lo, hi, used: (layer, (lo, hi)[sel][i], 0, 0))
    wd = lambda sel: pl.BlockSpec((None, 1, EXPERT_FF, D),
                                  lambda i, s, lo, hi, used: (layer, (lo, hi)[sel][i], 0, 0))
    return pl.pallas_call(
        functools.partial(_ffn_kernel, blk=blk),
        grid_spec=pltpu.PrefetchScalarGridSpec(
            num_scalar_prefetch=4, grid=(nblk,),
            in_specs=[pl.BlockSpec(memory_space=pl.ANY), pl.BlockSpec(rw.shape, lambda i, s, lo, hi, used: (0, 0)),
                      wg(0), wg(0), wd(0), wg(1), wg(1), wd(1)],
            out_specs=pl.BlockSpec((blk, D), lambda i, s, lo, hi, used: (i, 0)),
            scratch_shapes=[pltpu.VMEM((2, blk, D), F32), pltpu.SemaphoreType.DMA((2,))]),
        out_shape=jax.ShapeDtypeStruct((rows, D), F32),
        compiler_params=_cparams(("arbitrary",)), name="moe_ffn")(
            src, blk_lo, blk_hi, n_used, x, rw, w_gate, w_up, w_down, w_gate, w_up, w_down)


def _combine_ln_kernel(dest_ref, x_ref, y_hbm, g_ref, b_ref, o_ref, ybuf, sem, *, tm):
    i = pl.program_id(0)
    slot = i % 2

    @pl.when(i == 0)
    def _():
        _row_copies(dest_ref, 0, y_hbm, ybuf.at[0], sem.at[0], tm, False)

    @pl.when(i + 1 < pl.num_programs(0))
    def _():
        _row_copies(dest_ref, (i + 1) * tm, y_hbm, ybuf.at[1 - slot], sem.at[1 - slot], tm, False)

    _row_copies(dest_ref, i * tm, y_hbm, ybuf.at[slot], sem.at[slot], tm, True)
    o_ref[...] = _ln(ALPHA * x_ref[...] + ybuf[slot], g_ref[...], b_ref[...])


def _combine_ln(x, y_rows, dest, g, b, tm):
    t = x.shape[0]
    rowb = pl.BlockSpec((tm, D), lambda i, d: (i, 0))
    vec = pl.BlockSpec((1, D), lambda i, d: (0, 0))
    return pl.pallas_call(
        functools.partial(_combine_ln_kernel, tm=tm),
        grid_spec=pltpu.PrefetchScalarGridSpec(
            num_scalar_prefetch=1, grid=(t // tm,),
            in_specs=[rowb, pl.BlockSpec(memory_space=pl.ANY), vec, vec], out_specs=rowb,
            scratch_shapes=[pltpu.VMEM((2, tm, D), F32), pltpu.SemaphoreType.DMA((2,))]),
        out_shape=jax.ShapeDtypeStruct((t, D), F32),
        compiler_params=_cparams(("arbitrary",)), name="moe_combine_ln")(dest, x, y_rows, g, b)


def _invert_rows_kernel(dest_ref, src_ref):
    def clear(r, carry):
        src_ref[r] = 0
        return carry

    def put(tok, carry):
        src_ref[dest_ref[tok]] = tok
        return carry

    lax.fori_loop(0, src_ref.shape[0], clear, 0, unroll=8)
    lax.fori_loop(0, dest_ref.shape[0], put, 0, unroll=8)


def _invert_rows(dest, rows):
    smem = pl.BlockSpec(memory_space=pltpu.SMEM)
    return pl.pallas_call(
        _invert_rows_kernel, in_specs=[smem], out_specs=smem,
        out_shape=jax.ShapeDtypeStruct((rows,), jnp.int32), name="invert_rows")(dest)


def _moe_ln(x, rw_t, rb, rw_pad, w_gate, w_up, w_down, layer, g, b, tm_router, blk, tm_comb):
    t = x.shape[0]
    bucket, rank, counts = _router(x, rw_t, rb, tm_router)
    padded = (counts + blk - 1) // blk * blk
    ends = jnp.cumsum(padded)
    dest = ((ends - padded)[bucket] + rank).astype(jnp.int32)
    nblk = t // blk + N_BUCKETS
    src = _invert_rows(dest, nblk * blk)
    blk_bucket = jnp.minimum(jnp.searchsorted(ends, jnp.arange(nblk) * blk, side='right'), N_BUCKETS - 1)
    pair_lo = jnp.array([p[0] for p in _PAIRS], jnp.int32)
    pair_hi = jnp.array([p[1] for p in _PAIRS], jnp.int32)
    grp, pr = blk_bucket // len(_PAIRS), blk_bucket % len(_PAIRS)
    blk_lo = (grp * EXPERTS_PER_GROUP + pair_lo[pr]).astype(jnp.int32)
    blk_hi = (grp * EXPERTS_PER_GROUP + pair_hi[pr]).astype(jnp.int32)
    n_used = (ends[-1:] // blk).astype(jnp.int32)
    y_rows = _ffn(x, src, blk_lo, blk_hi, n_used, rw_pad, w_gate, w_up, w_down, layer, blk)
    return _combine_ln(x, y_rows, dest, g, b, tm_comb)


def _moe_dense_kernel(x_ref, lo_ref, hi_ref, rw_ref, wg_ref, wu_ref, wd_ref, g_ref, b_ref, o_ref, acc_ref):
    e = pl.program_id(0)

    @pl.when(e == 0)
    def _():
        acc_ref[...] = jnp.zeros_like(acc_ref)

    x = x_ref[...]
    xb = x.astype(BF16)
    logits = _dot(xb, rw_ref[...])
    lane = lax.broadcasted_iota(jnp.int32, logits.shape, 1)
    lo, hi = lo_ref[...], hi_ref[...]
    l_lo = jnp.sum(jnp.where(lane == lo, logits, 0.0), axis=-1, keepdims=True)
    l_hi = jnp.sum(jnp.where(lane == hi, logits, 0.0), axis=-1, keepdims=True)
    w_lo = _sigmoid(l_lo - l_hi)
    coef = jnp.where(lo == e, w_lo, 0.0) + jnp.where(hi == e, 1.0 - w_lo, 0.0)
    gate = _dot(xb, wg_ref[0])
    act = gate * _sigmoid(gate) * _dot(xb, wu_ref[0])
    acc_ref[...] += coef * _dot(act.astype(BF16), wd_ref[0])

    @pl.when(e == pl.num_programs(0) - 1)
    def _():
        o_ref[...] = _ln(ALPHA * x + acc_ref[...], g_ref[...], b_ref[...])


def _moe_ln_dense(x, rw_t, rb, rw_pad, w_gate, w_up, w_down, layer, g, b):
    t = x.shape[0]
    bucket, _, _ = _router(x, rw_t, rb, t)
    pair_lo = jnp.array([p[0] for p in _PAIRS], jnp.int32)
    pair_hi = jnp.array([p[1] for p in _PAIRS], jnp.int32)
    grp, pr = bucket // len(_PAIRS), bucket % len(_PAIRS)
    lo = (grp * EXPERTS_PER_GROUP + pair_lo[pr]).astype(jnp.int32).reshape(t, 1)
    hi = (grp * EXPERTS_PER_GROUP + pair_hi[pr]).astype(jnp.int32).reshape(t, 1)
    wg = pl.BlockSpec((None, 1, D, EXPERT_FF), lambda e: (layer, e, 0, 0))
    wd = pl.BlockSpec((None, 1, EXPERT_FF, D), lambda e: (layer, e, 0, 0))
    return pl.pallas_call(
        _moe_dense_kernel, grid=(N_EXPERTS,),
        in_specs=[_full((t, D)), _full((t, 1)), _full((t, 1)), _full(rw_pad.shape), wg, wg, wd,
                  _full((1, D)), _full((1, D))],
        out_specs=_full((t, D)), out_shape=jax.ShapeDtypeStruct((t, D), F32),
        scratch_shapes=[pltpu.VMEM((t, D), F32)],
        compiler_params=_cparams(("arbitrary",)), name="moe_dense")(x, lo, hi, rw_pad, w_gate, w_up, w_down, g, b)


def kernel(x_prompt, x_sample, cache_swa_k, cache_swa_v, state_lru_conv, state_lru_h, state_rwkv_shift, state_rwkv_wkv, cache_mem_k, cache_mem_v, mem_prompt, swa_w_qkv, swa_sinks, swa_w_o, lru_w_in, lru_b_in, lru_conv_w, lru_conv_b, lru_w_a, lru_b_a, lru_w_i, lru_b_i, lru_lambda, lru_w_o, rwkv_mu, rwkv_w_r, rwkv_w_k, rwkv_w_v, rwkv_w0, rwkv_w1, rwkv_w2, rwkv_a0, rwkv_a1, rwkv_a2, rwkv_g1, rwkv_g2, rwkv_k_k, rwkv_k_a, rwkv_r_k, rwkv_gn_g, rwkv_gn_b, rwkv_w_o, mem_w_q, mem_w_kv, mem_w_o, ln_g, ln_b, router_w, router_b, moe_w_gate, moe_w_up, moe_w_down):
    n_p, seq, _ = x_prompt.shape
    n_s, dec_seq, _ = x_sample.shape
    assert dec_seq == 1
    past_len = 8192
    xp = x_prompt.reshape(n_p * seq, D)
    xs = x_sample.reshape(n_s, D)
    row = lambda v: v.reshape(1, -1)
    bf = lambda w: w.astype(BF16)

    rw_t = bf(router_w.T)
    rb = router_b.reshape(N_EXPERTS, 1)
    rw_pad = bf(jnp.pad(router_w, ((0, 0), (0, LANES - N_EXPERTS))))
    wg, wu, wd = bf(moe_w_gate), bf(moe_w_up), bf(moe_w_down)
    mem_p = mem_prompt.reshape(n_p * mem_prompt.shape[1], D)
    m_len = mem_prompt.shape[1]

    swa_k_p, swa_v_p, swa_k_s, swa_v_s = [], [], [], []
    lru_c_p, lru_h_p, lru_c_s, lru_h_s = [], [], [], []
    rw_x_p, rw_s_p, rw_x_s, rw_s_s = [], [], [], []
    mem_k_p, mem_v_p = [], []

    for layer in range(DEPTH):
        kind, i = layer % N_MIXERS, layer // N_MIXERS
        g0, b0 = row(ln_g[layer, 0]), row(ln_b[layer, 0])
        if kind == 0:
            w_qkv, w_o = bf(swa_w_qkv[i]), bf(swa_w_o[i])
            keep = min(WINDOW, seq)
            q, k, v, kv_last = _swa_qkv(xp, w_qkv, jnp.arange(seq), n_p, 512, keep, BF16)
            o = _swa_attn_prompt(q, k, v, swa_sinks[i], n_p, 2)
            swa_k_p.append(kv_last[:, :, :KV_WIDTH].reshape(n_p, keep, SWA_KV_HEADS, HEAD_DIM))
            swa_v_p.append(kv_last[:, :, KV_WIDTH:].reshape(n_p, keep, SWA_KV_HEADS, HEAD_DIM))
            xp = _proj_ln(o, w_o, xp, g0, b0, 512)

            qs, _, _, kv_new = _swa_qkv(xs, w_qkv, jnp.full((n_s,), past_len), 1, n_s, n_s, F32)
            kn, vn = kv_new[0, :, :KV_WIDTH], kv_new[0, :, KV_WIDTH:]
            os_ = _swa_attn_sample(qs, kn, vn, cache_swa_k, cache_swa_v, i, swa_sinks[i], 8)
            wb = cache_swa_k.shape[2]
            k_all = jnp.concatenate([cache_swa_k[i], kn.reshape(n_s, 1, SWA_KV_HEADS, HEAD_DIM)], axis=1)
            v_all = jnp.concatenate([cache_swa_v[i], vn.reshape(n_s, 1, SWA_KV_HEADS, HEAD_DIM)], axis=1)
            swa_k_s.append(k_all[:, -wb:])
            swa_v_s.append(v_all[:, -wb:])
            xs = _proj_ln(os_, w_o, xs, g0, b0, n_s)
        elif kind == 1:
            wts = _lru_weights(lru_w_in[i], lru_b_in[i], lru_conv_w[i], lru_conv_b[i], lru_w_a[i], lru_b_a[i],
                               lru_w_i[i], lru_b_i[i], lru_lambda[i], lru_w_o[i])
            xp, conv_last, h_last = _lru_prompt(xp, wts, g0, b0, n_p, 256)
            lru_c_p.append(conv_last[:, SUBLANES - (CONV_W - 1):])
            lru_h_p.append(h_last[:, SUBLANES - 1])
            xs, xb_s, h_s = _lru_sample(xs, state_lru_conv[i], state_lru_h[i], wts, g0, b0)
            lru_c_s.append(jnp.concatenate([state_lru_conv[i][:, 1:], xb_s[:, None]], axis=1))
            lru_h_s.append(h_s)
        else:
            wts = (rwkv_mu[i], bf(rwkv_w_r[i]), bf(rwkv_w_k[i]), bf(rwkv_w_v[i]), row(rwkv_w0[i]), bf(rwkv_w1[i]),
                   bf(rwkv_w2[i]), row(rwkv_a0[i]), bf(rwkv_a1[i]), bf(rwkv_a2[i]), bf(rwkv_g1[i]), bf(rwkv_g2[i]))
            hp = (row(rwkv_k_k[i]), row(rwkv_k_a[i]), row(rwkv_r_k[i]), row(rwkv_gn_g[i]), row(rwkv_gn_b[i]))
            w_o = bf(rwkv_w_o[i])
            rw_x_p.append(xp.reshape(n_p, seq, D)[:, -1])
            rw_x_s.append(xs)
            r, k, v, a, ld, g = _rwkv_pre(xp, jnp.zeros((n_p, SUBLANES, D), F32), wts, n_p, 256, True, BF16)
            o, st = _wkv_prompt(r, k, v, a, ld, g, hp, n_p)
            hd = RWKV_HD
            st = jnp.stack([st[:, :, :hd, :hd], st[:, :, hd:, hd:]], axis=2).reshape(n_p, RWKV_HEADS, hd, hd)
            rw_s_p.append(jnp.swapaxes(st, -1, -2))
            xp = _proj_ln(o, w_o, xp, g0, b0, 512)

            r, k, v, a, ld, g = _rwkv_pre(xs, state_rwkv_shift[i], wts, 1, n_s, False, F32)
            os_, s_new = _wkv_sample(r, k, v, a, ld, g, state_rwkv_wkv[i], hp, 8)
            rw_s_s.append(s_new)
            xs = _proj_ln(os_, w_o, xs, g0, b0, n_s)

        g1, b1 = row(ln_g[layer, 1]), row(ln_b[layer, 1])
        w_q, w_o = bf(mem_w_q[layer]), bf(mem_w_o[layer])
        mkv = _matmul(mem_p, bf(mem_w_kv[layer]), 512)
        mk, mv = mkv[:, :D], mkv[:, D:]
        mem_k_p.append(mk.reshape(n_p, m_len, MEM_HEADS, MEM_HD))
        mem_v_p.append(mv.reshape(n_p, m_len, MEM_HEADS, MEM_HD))
        xp = _mem_attn_prompt(xp, w_q, bf(mk).reshape(n_p, m_len, D), bf(mv).reshape(n_p, m_len, D), w_o, g1, b1,
                              n_p, 512)
        qs = _matmul(xs, w_q, n_s)
        os_ = _mem_attn_sample(qs, cache_mem_k, cache_mem_v, layer, 4)
        xs = _proj_ln(os_, w_o, xs, g1, b1, n_s)

        g2, b2 = row(ln_g[layer, 2]), row(ln_b[layer, 2])
        xp = _moe_ln(xp, rw_t, rb, rw_pad, wg, wu, wd, layer, g2, b2, 512, 256, 256)
        xs = _moe_ln_dense(xs, rw_t, rb, rw_pad, wg, wu, wd, layer, g2, b2)

    return (xp.reshape(n_p, seq, D), xs.reshape(n_s, 1, D),
            jnp.stack(swa_k_p), jnp.stack(swa_v_p), jnp.stack(lru_c_p), jnp.stack(lru_h_p),
            jnp.stack(rw_x_p), jnp.stack(rw_s_p), jnp.stack(mem_k_p), jnp.stack(mem_v_p),
            jnp.stack(swa_k_s), jnp.stack(swa_v_s), jnp.stack(lru_c_s), jnp.stack(lru_h_s),
            jnp.stack(rw_x_s), jnp.stack(rw_s_s))
```

```python
import functools

import jax
import jax.numpy as jnp
from jax import lax
from jax.experimental import pallas as pl
from jax.experimental.pallas import tpu as pltpu

F32 = jnp.float32
BF16 = jnp.bfloat16

D = 1024
DEPTH = 4
N_MIXERS = 3
HEAD_DIM = 64
SWA_HEADS = D // HEAD_DIM
SWA_KV_HEADS = 4
SWA_GROUP = SWA_HEADS // SWA_KV_HEADS
Q_WIDTH = SWA_HEADS * HEAD_DIM
KV_WIDTH = SWA_KV_HEADS * HEAD_DIM
WINDOW = 128
ROT_DIM = HEAD_DIM // 4
ROPE_THETA = 500000.0
LRU_BLOCKS = 16
CONV_W = 4
LRU_C = 8.0
RWKV_HEADS = 16
RWKV_HD = 64
RWKV_GN_EPS = 64e-5
MEM_HEADS = 4
MEM_HD = D // MEM_HEADS
N_EXPERTS = 16
N_GROUPS = 4
EXPERTS_PER_GROUP = 4
EXPERT_FF = 512
LN_EPS = 1e-5
ALPHA = (2.0 * DEPTH) ** 0.25
NEG_INF = -1e30

LANES = 128
SUBLANES = 8
VMEM_LIMIT = 56 * 1024 * 1024
WKV_CHUNK = 64
N_BUCKETS = N_GROUPS * 6
BUCKET_ROWS = 32


def _cparams(sem):
    return pltpu.CompilerParams(dimension_semantics=sem, vmem_limit_bytes=VMEM_LIMIT)


def _dot(a, b):
    return jnp.dot(a, b, preferred_element_type=F32)


def _dot_nt(a, b):
    return lax.dot_general(a, b, (((1,), (1,)), ((), ())), preferred_element_type=F32)


def _dot_tn(a, b):
    return lax.dot_general(a, b, (((0,), (0,)), ((), ())), preferred_element_type=F32)


def _ln(z, g, b):
    mu = jnp.mean(z, axis=-1, keepdims=True)
    zc = z - mu
    var = jnp.mean(zc * zc, axis=-1, keepdims=True)
    return zc * lax.rsqrt(var + LN_EPS) * g + b


def _softplus(z):
    return jnp.maximum(z, 0.0) + jnp.log1p(jnp.exp(-jnp.abs(z)))


def _sigmoid(z):
    return 1.0 / (1.0 + jnp.exp(-z))


def _round_bf16(x):
    return x.astype(BF16).astype(F32)


def _full(shape):
    nd = len(shape)
    return pl.BlockSpec(shape, lambda *_: (0,) * nd)


def _mm_kernel(a_ref, w_ref, o_ref):
    o_ref[...] = _dot(a_ref[...].astype(BF16), w_ref[...]).astype(o_ref.dtype)


def _matmul(a, w, tm, out_dtype=F32):
    t, k = a.shape
    n = w.shape[1]
    return pl.pallas_call(
        _mm_kernel, grid=(t // tm,),
        in_specs=[pl.BlockSpec((tm, k), lambda i: (i, 0)), _full((k, n))],
        out_specs=pl.BlockSpec((tm, n), lambda i: (i, 0)),
        out_shape=jax.ShapeDtypeStruct((t, n), out_dtype),
        compiler_params=_cparams(("parallel",)), name="matmul")(a, w)


def _proj_ln_kernel(a_ref, w_ref, x_ref, g_ref, b_ref, o_ref):
    acc = _dot(a_ref[...].astype(BF16), w_ref[...])
    o_ref[...] = _ln(ALPHA * x_ref[...] + acc, g_ref[...], b_ref[...])


def _proj_ln(a, w, x, g, b, tm):
    t, k = a.shape
    return pl.pallas_call(
        _proj_ln_kernel, grid=(t // tm,),
        in_specs=[pl.BlockSpec((tm, k), lambda i: (i, 0)), _full((k, D)),
                  pl.BlockSpec((tm, D), lambda i: (i, 0)), _full((1, D)), _full((1, D))],
        out_specs=pl.BlockSpec((tm, D), lambda i: (i, 0)),
        out_shape=jax.ShapeDtypeStruct((t, D), F32),
        compiler_params=_cparams(("parallel",)), name="proj_ln")(a, w, x, g, b)


def _rope_tables(pos):
    half = ROT_DIM // 2
    inv_freq = ROPE_THETA ** (-jnp.arange(half, dtype=F32) / half)
    ang = pos.astype(F32)[:, None] * inv_freq
    cos, sin = jnp.cos(ang), jnp.sin(ang)
    one = jnp.ones((pos.shape[0], HEAD_DIM - ROT_DIM), F32)
    zero = jnp.zeros((pos.shape[0], HEAD_DIM - ROT_DIM), F32)
    zh = jnp.zeros_like(sin)
    c = jnp.concatenate([cos, cos, one], axis=1)
    s1 = jnp.concatenate([-sin, zh, zero], axis=1)
    s2 = jnp.concatenate([zh, sin, zero], axis=1)
    rep = LANES // HEAD_DIM
    return jnp.tile(c, (1, rep)), jnp.tile(s1, (1, rep)), jnp.tile(s2, (1, rep))


def _swa_qkv_kernel(x_ref, w_ref, c_ref, s1_ref, s2_ref, q_ref, k_ref, v_ref, kv_ref, *, tm, keep):
    acc = _dot(x_ref[...].astype(BF16), w_ref[...])
    c, s1, s2 = c_ref[...], s1_ref[...], s2_ref[...]
    half = ROT_DIM // 2
    n_q = Q_WIDTH // LANES
    n_k = KV_WIDTH // LANES
    for cg in range(n_q + n_k):
        xg = acc[:, cg * LANES:(cg + 1) * LANES]
        rot = xg * c + pltpu.roll(xg, LANES - half, 1) * s1 + pltpu.roll(xg, half, 1) * s2
        if cg < n_q:
            q_ref[:, cg * LANES:(cg + 1) * LANES] = rot.astype(q_ref.dtype)
        else:
            ck = cg - n_q
            k_ref[:, ck * LANES:(ck + 1) * LANES] = rot.astype(k_ref.dtype)
            kv_ref[0, :, ck * LANES:(ck + 1) * LANES] = rot[tm - keep:, :]
    v = acc[:, Q_WIDTH + KV_WIDTH:]
    v_ref[...] = v.astype(v_ref.dtype)
    kv_ref[0, :, KV_WIDTH:] = v[tm - keep:, :]


def _swa_qkv(x, w_qkv, pos, n_seq, tm, keep, qdtype):
    t = x.shape[0]
    s = t // n_seq
    nb = s // tm
    c, s1, s2 = _rope_tables(pos)
    row = lambda n, i: (n * nb + i, 0)
    tab = pl.BlockSpec((tm, LANES), lambda n, i: (i, 0))
    kern = functools.partial(_swa_qkv_kernel, tm=tm, keep=keep)
    return pl.pallas_call(
        kern, grid=(n_seq, nb),
        in_specs=[pl.BlockSpec((tm, D), row), _full((D, Q_WIDTH + 2 * KV_WIDTH)), tab, tab, tab],
        out_specs=[pl.BlockSpec((tm, Q_WIDTH), row), pl.BlockSpec((tm, KV_WIDTH), row),
                   pl.BlockSpec((tm, KV_WIDTH), row),
                   pl.BlockSpec((1, keep, 2 * KV_WIDTH), lambda n, i: (n, 0, 0))],
        out_shape=[jax.ShapeDtypeStruct((t, Q_WIDTH), qdtype), jax.ShapeDtypeStruct((t, KV_WIDTH), qdtype),
                   jax.ShapeDtypeStruct((t, KV_WIDTH), qdtype),
                   jax.ShapeDtypeStruct((n_seq, keep, 2 * KV_WIDTH), F32)],
        compiler_params=_cparams(("parallel", "arbitrary")), name="swa_qkv")(x, w_qkv, c, s1, s2)


def _swa_attn_kernel(sink_ref, q_ref, kp_ref, kc_ref, vp_ref, vc_ref, o_ref, *, nq):
    j = pl.program_id(1)
    w, grp = WINDOW, SWA_GROUP
    r = lax.broadcasted_iota(jnp.int32, (grp * w, 2 * w), 0) % w
    c = lax.broadcasted_iota(jnp.int32, (grp * w, 2 * w), 1)
    in_prev = jnp.logical_and(c < w, c > r)
    in_cur = jnp.logical_and(c >= w, (c - w) <= r)
    ok_inner = jnp.logical_or(in_prev, in_cur)
    ok_first = jnp.logical_or(jnp.logical_and(in_prev, j > 0), in_cur)
    scale = HEAD_DIM ** -0.5
    combos = [(u, h) for u in range(nq) for h in range(SWA_KV_HEADS)]
    kcat, vcat, q4, sink, ok = [], [], [], [], []
    for u, h in combos:
        sl = slice(h * HEAD_DIM, (h + 1) * HEAD_DIM)
        rows = slice(u * w, (u + 1) * w)
        before = slice((u - 1) * w, u * w)
        k_prev = kp_ref[:, sl] if u == 0 else kc_ref[before, sl]
        v_prev = vp_ref[:, sl] if u == 0 else vc_ref[before, sl]
        kcat.append(jnp.concatenate([k_prev, kc_ref[rows, sl]], axis=0))
        vcat.append(jnp.concatenate([v_prev, vc_ref[rows, sl]], axis=0))
        heads = [h * grp + g for g in range(grp)]
        q4.append(jnp.concatenate([q_ref[rows, hq * HEAD_DIM:(hq + 1) * HEAD_DIM] for hq in heads], axis=0))
        sink.append(jnp.concatenate([jnp.full((w, 1), sink_ref[hq], F32) for hq in heads], axis=0))
        ok.append(ok_first if u == 0 else ok_inner)
    n = range(len(combos))
    s = [jnp.where(ok[i], _dot_nt(q4[i], kcat[i]) * scale, NEG_INF) for i in n]
    m = [jnp.maximum(jnp.max(s[i], axis=-1, keepdims=True), sink[i]) for i in n]
    p = [jnp.exp(s[i] - m[i]) for i in n]
    den = [jnp.sum(p[i], axis=-1, keepdims=True) + jnp.exp(sink[i] - m[i]) for i in n]
    o = [_dot((p[i] / den[i]).astype(BF16), vcat[i]) for i in n]
    for i, (u, h) in enumerate(combos):
        for g in range(grp):
            hq = h * grp + g
            o_ref[u * w:(u + 1) * w, hq * HEAD_DIM:(hq + 1) * HEAD_DIM] = o[i][g * w:(g + 1) * w].astype(o_ref.dtype)


def _swa_attn_prompt(q, k, v, sinks, n_seq, nq):
    t = q.shape[0]
    nb = t // n_seq // WINDOW
    ns = nb // nq
    cur = lambda n, j: (n * ns + j, 0)
    prev = lambda n, j: (n * nb + jnp.maximum(j * nq - 1, 0), 0)
    return pl.pallas_call(
        functools.partial(_swa_attn_kernel, nq=nq), grid=(n_seq, ns),
        in_specs=[pl.BlockSpec(memory_space=pltpu.SMEM), pl.BlockSpec((nq * WINDOW, Q_WIDTH), cur),
                  pl.BlockSpec((WINDOW, KV_WIDTH), prev), pl.BlockSpec((nq * WINDOW, KV_WIDTH), cur),
                  pl.BlockSpec((WINDOW, KV_WIDTH), prev), pl.BlockSpec((nq * WINDOW, KV_WIDTH), cur)],
        out_specs=pl.BlockSpec((nq * WINDOW, Q_WIDTH), cur),
        out_shape=jax.ShapeDtypeStruct((t, Q_WIDTH), BF16),
        compiler_params=_cparams(("parallel", "arbitrary")), name="swa_attn")(sinks, q, k, k, v, v)


def _swa_sample_kernel(sink_ref, qbd_ref, q_ref, kn_ref, vnbd_ref, ckt_ref, cvt_ref, o_ref, *, bs):
    nkv, hd, wb = ckt_ref.shape[1:]
    nq = q_ref.shape[1]
    npad = kn_ref.shape[1]
    key = lax.broadcasted_iota(jnp.int32, (nq, wb), 1)
    valid = (wb - key) < WINDOW
    own_new = (lax.broadcasted_iota(jnp.int32, (nq, npad), 1)
               == lax.broadcasted_iota(jnp.int32, (nq, npad), 0) // SWA_GROUP)
    sink = sink_ref[...]
    scale = HEAD_DIM ** -0.5
    nb = range(bs)
    s = [jnp.where(valid, _dot(qbd_ref[b].astype(BF16), ckt_ref[b].reshape(nkv * hd, wb).astype(BF16)) * scale, NEG_INF)
         for b in nb]
    sn = [jnp.where(own_new, _dot_nt(q_ref[b].astype(BF16), kn_ref[b].astype(BF16)) * scale, NEG_INF) for b in nb]
    m = [jnp.maximum(jnp.maximum(jnp.max(s[b], axis=-1, keepdims=True), jnp.max(sn[b], axis=-1, keepdims=True)), sink)
         for b in nb]
    p = [jnp.where(valid, jnp.exp(s[b] - m[b]), 0.0) for b in nb]
    pn = [jnp.where(own_new, jnp.exp(sn[b] - m[b]), 0.0) for b in nb]
    den = [jnp.sum(p[b], axis=-1, keepdims=True) + jnp.sum(pn[b], axis=-1, keepdims=True) + jnp.exp(sink - m[b])
           for b in nb]
    for b in nb:
        o_ref[b] = (_dot_nt((p[b] / den[b]).astype(BF16), cvt_ref[b].reshape(nkv * hd, wb).astype(BF16))
                    + _dot((pn[b] / den[b]).astype(BF16), vnbd_ref[b].astype(BF16)))


def _swa_attn_sample(q, kn, vn, cache_k, cache_v, layer, sinks, bs):
    _, b, wb, nkv, hd = cache_k.shape
    grp = SWA_HEADS // nkv
    eye = jnp.eye(nkv, dtype=q.dtype)
    qbd = jnp.einsum('bhgd,hk->bhgkd', q.reshape(b, nkv, grp, hd), eye).reshape(b, SWA_HEADS, nkv * hd)
    pad = lambda z: jnp.pad(z, ((0, 0), (0, SUBLANES - nkv), (0, 0)))
    vnbd = pad(jnp.einsum('bhd,hk->bhkd', vn.reshape(b, nkv, hd), eye).reshape(b, nkv, nkv * hd))
    blk = lambda r, w: pl.BlockSpec((bs, r, w), lambda i: (i, 0, 0))
    cblk = pl.BlockSpec((None, bs, nkv, hd, wb), lambda i: (layer, i, 0, 0, 0))
    out = pl.pallas_call(
        functools.partial(_swa_sample_kernel, bs=bs), grid=(b // bs,),
        in_specs=[_full((SWA_HEADS, 1)), blk(SWA_HEADS, nkv * hd), blk(SWA_HEADS, hd), blk(SUBLANES, hd),
                  blk(SUBLANES, nkv * hd), cblk, cblk],
        out_specs=blk(SWA_HEADS, nkv * hd), out_shape=jax.ShapeDtypeStruct((b, SWA_HEADS, nkv * hd), F32),
        compiler_params=_cparams(("parallel",)), name="swa_sample")(
            sinks.reshape(SWA_HEADS, 1), qbd, q.reshape(b, SWA_HEADS, hd), pad(kn.reshape(b, nkv, hd)), vnbd,
            jnp.transpose(cache_k, (0, 1, 3, 4, 2)), jnp.transpose(cache_v, (0, 1, 3, 4, 2)))
    o5 = out.reshape(b, nkv, grp, nkv, hd)
    return jnp.stack([o5[:, h, :, h, :] for h in range(nkv)], axis=1).reshape(b, Q_WIDTH)


def _gelu_tanh(x):
    return 0.5 * x * (1.0 + jnp.tanh(0.7978845608028654 * (x + 0.044715 * x * x * x)))


def _lru_gates(xc, wa_ref, ba, wi_ref, bi, lam):
    xcb = xc.astype(BF16)
    gw = wa_ref.shape[1]
    ra, ia = [], []
    for gi in range(wa_ref.shape[0]):
        xs = xcb[:, gi * gw:(gi + 1) * gw]
        ra.append(_dot(xs, wa_ref[gi]))
        ia.append(_dot(xs, wi_ref[gi]))
    r = _sigmoid(jnp.concatenate(ra, axis=-1) + ba)
    ig = _sigmoid(jnp.concatenate(ia, axis=-1) + bi)
    log_a = -LRU_C * r * _softplus(-lam)
    a = jnp.exp(log_a)
    b = jnp.sqrt(-jnp.tanh(log_a) * (a * a + 1.0)) * (ig * xc)
    return a, b


def _shift_rows(ext, s, tm):
    return pltpu.roll(ext, s, 0)[SUBLANES:SUBLANES + tm]


def _lru_prompt_kernel(x_ref, win_ref, bin_ref, cw_ref, cb_ref, wa_ref, ba_ref, wi_ref, bi_ref, lam_ref,
                       wo_ref, g_ref, b_ref, o_ref, conv_ref, hl_ref, cx_ref, ch_ref, *, tm):
    i = pl.program_id(1)

    @pl.when(i == 0)
    def _():
        cx_ref[...] = jnp.zeros_like(cx_ref)
        ch_ref[...] = jnp.zeros_like(ch_ref)

    x = x_ref[...]
    xy = _dot(x.astype(BF16), win_ref[...]) + bin_ref[...]
    xb = xy[:, :D]
    y_gate = _gelu_tanh(xy[:, D:])
    ext = jnp.concatenate([cx_ref[...], xb], axis=0)
    cw = cw_ref[...]
    xc = cb_ref[...] + xb * cw[CONV_W - 1:CONV_W]
    for s in range(1, CONV_W):
        xc = xc + _shift_rows(ext, s, tm) * cw[CONV_W - 1 - s:CONV_W - s]
    cx_ref[...] = xb[tm - SUBLANES:]
    conv_ref[0] = xb[tm - SUBLANES:]

    a, b = _lru_gates(xc, wa_ref, ba_ref[...], wi_ref, bi_ref[...], lam_ref[...])
    sub = lax.broadcasted_iota(jnp.int32, (tm, 1), 0) % SUBLANES
    s = 1
    while s < SUBLANES:
        keep = sub >= s
        a_sh = jnp.where(keep, pltpu.roll(a, s, 0), 1.0)
        b_sh = jnp.where(keep, pltpu.roll(b, s, 0), 0.0)
        b = a * b_sh + b
        a = a * a_sh
        s *= 2
    carry = ch_ref[SUBLANES - 1:SUBLANES, :]
    groups = []
    for gi in range(tm // SUBLANES):
        rows = slice(gi * SUBLANES, (gi + 1) * SUBLANES)
        hg = a[rows] * carry + b[rows]
        groups.append(hg)
        carry = hg[SUBLANES - 1:SUBLANES]
    h = jnp.concatenate(groups, axis=0)
    ch_ref[...] = h[tm - SUBLANES:]
    hl_ref[0] = h[tm - SUBLANES:]
    acc = _dot((h * y_gate).astype(BF16), wo_ref[...])
    o_ref[...] = _ln(ALPHA * x + acc, g_ref[...], b_ref[...])


def _lru_weights(w_in, b_in, conv_w, conv_b, w_a, b_a, w_i, b_i, lam, w_o):
    gsz = 4
    ng = LRU_BLOCKS // gsz
    bw = D // LRU_BLOCKS

    def grouped(w):
        w4 = w.reshape(ng, gsz, bw, bw)
        return jnp.einsum('gaij,ab->gaibj', w4, jnp.eye(gsz, dtype=w.dtype)).reshape(ng, gsz * bw, gsz * bw).astype(BF16)

    row = lambda v: v.reshape(1, -1)
    return (w_in.astype(BF16), row(b_in), conv_w, row(conv_b), grouped(w_a), row(b_a), grouped(w_i), row(b_i),
            row(lam), w_o.astype(BF16))


def _lru_prompt(x, wts, g, b, n_seq, tm):
    t = x.shape[0]
    nb = t // n_seq // tm
    row = lambda n, i: (n * nb + i, 0)
    last = pl.BlockSpec((1, SUBLANES, D), lambda n, i: (n, 0, 0))
    w_in, b_in, cw, cb, wa, ba, wi, bi, lam, wo = wts
    return pl.pallas_call(
        functools.partial(_lru_prompt_kernel, tm=tm), grid=(n_seq, nb),
        in_specs=[pl.BlockSpec((tm, D), row), _full(w_in.shape), _full(b_in.shape), _full(cw.shape), _full(cb.shape),
                  _full(wa.shape), _full(ba.shape), _full(wi.shape), _full(bi.shape), _full(lam.shape),
                  _full(wo.shape), _full((1, D)), _full((1, D))],
        out_specs=[pl.BlockSpec((tm, D), row), last, last],
        out_shape=[jax.ShapeDtypeStruct((t, D), F32), jax.ShapeDtypeStruct((n_seq, SUBLANES, D), F32),
                   jax.ShapeDtypeStruct((n_seq, SUBLANES, D), F32)],
        scratch_shapes=[pltpu.VMEM((SUBLANES, D), F32), pltpu.VMEM((SUBLANES, D), F32)],
        compiler_params=_cparams(("parallel", "arbitrary")), name="lru_prompt")(x, *wts, g, b)


def _lru_sample_kernel(x_ref, c0_ref, c1_ref, c2_ref, h0_ref, win_ref, bin_ref, cw_ref, cb_ref, wa_ref, ba_ref,
                       wi_ref, bi_ref, lam_ref, wo_ref, g_ref, b_ref, o_ref, xb_ref, h_ref):
    x = x_ref[...]
    xy = _dot(x.astype(BF16), win_ref[...]) + bin_ref[...]
    xb = xy[:, :D]
    y_gate = _gelu_tanh(xy[:, D:])
    cw = cw_ref[...]
    xc = (cb_ref[...] + c0_ref[...] * cw[0:1] + c1_ref[...] * cw[1:2] + c2_ref[...] * cw[2:3] + xb * cw[3:4])
    a, b = _lru_gates(xc, wa_ref, ba_ref[...], wi_ref, bi_ref[...], lam_ref[...])
    h = a * h0_ref[...] + b
    xb_ref[...] = xb
    h_ref[...] = h
    acc = _dot((h * y_gate).astype(BF16), wo_ref[...])
    o_ref[...] = _ln(ALPHA * x + acc, g_ref[...], b_ref[...])


def _lru_sample(x, conv_state, h0, wts, g, b):
    t = x.shape[0]
    args = (x, conv_state[:, 0], conv_state[:, 1], conv_state[:, 2], h0, *wts, g, b)
    sd = jax.ShapeDtypeStruct((t, D), F32)
    return pl.pallas_call(
        _lru_sample_kernel, grid=(1,),
        in_specs=[_full(a.shape) for a in args],
        out_specs=[_full((t, D))] * 3, out_shape=[sd, sd, sd],
        compiler_params=_cparams(("arbitrary",)), name="lru_sample")(*args)


def _rwkv_pre_kernel(x_ref, xp_ref, mu_ref, wr_ref, wk_ref, wv_ref, w0_ref, w1_ref, w2_ref, a0_ref, a1_ref, a2_ref,
                     g1_ref, g2_ref, r_ref, k_ref, v_ref, a_ref, ld_ref, g_ref, *scratch, tm, seq):
    x = x_ref[...]
    if seq:
        cx_ref, = scratch
        i = pl.program_id(1)

        @pl.when(i == 0)
        def _():
            cx_ref[...] = xp_ref[0]

        x_prev = _shift_rows(jnp.concatenate([cx_ref[...], x], axis=0), 1, tm)
        cx_ref[...] = x[tm - SUBLANES:]
    else:
        x_prev = xp_ref[...]
    xx = x_prev - x
    mu = mu_ref[...]
    mix = lambda j: (x + xx * mu[j:j + 1]).astype(BF16)
    r_ref[...] = _dot(mix(0), wr_ref[...]).astype(r_ref.dtype)
    wl = _dot(jnp.tanh(_dot(mix(1), w1_ref[...])).astype(BF16), w2_ref[...])
    w = -_softplus(-(w0_ref[...] + wl)) - 0.5
    ld_ref[...] = -jnp.exp(w)
    k_ref[...] = _dot(mix(2), wk_ref[...]).astype(k_ref.dtype)
    v_ref[...] = _dot(mix(3), wv_ref[...]).astype(v_ref.dtype)
    al = _dot(_dot(mix(4), a1_ref[...]).astype(BF16), a2_ref[...])
    a_ref[...] = _sigmoid(a0_ref[...] + al).astype(a_ref.dtype)
    g_ref[...] = _dot(_sigmoid(_dot(mix(5), g1_ref[...])).astype(BF16), g2_ref[...]).astype(g_ref.dtype)


def _rwkv_pre(x, x_prev, wts, n_seq, tm, seq, dtype):
    t = x.shape[0]
    nb = t // n_seq // tm
    row = lambda n, i: (n * nb + i, 0)
    xp_spec = pl.BlockSpec((1, SUBLANES, D), lambda n, i: (n, 0, 0)) if seq else pl.BlockSpec((tm, D), row)
    sd = lambda dt: jax.ShapeDtypeStruct((t, D), dt)
    blk = pl.BlockSpec((tm, D), row)
    return pl.pallas_call(
        functools.partial(_rwkv_pre_kernel, tm=tm, seq=seq), grid=(n_seq, nb),
        in_specs=[blk, xp_spec] + [_full(w.shape) for w in wts],
        out_specs=[blk] * 6,
        out_shape=[sd(dtype), sd(dtype), sd(dtype), sd(dtype), sd(F32), sd(dtype)],
        scratch_shapes=[pltpu.VMEM((SUBLANES, D), F32)] if seq else [],
        compiler_params=_cparams(("parallel", "arbitrary")), name="rwkv_pre")(x, x_prev, *wts)


def _seg_sum(x, first):
    s0 = jnp.sum(jnp.where(first, x, 0.0), axis=-1, keepdims=True)
    s1 = jnp.sum(jnp.where(first, 0.0, x), axis=-1, keepdims=True)
    return jnp.where(first, s0, s1)


def _wkv_kernel(r_ref, k_ref, v_ref, a_ref, ld_ref, g_ref, kk_ref, ka_ref, rk_ref, gg_ref, gb_ref,
                o_ref, s_ref, st_ref):
    c = pl.program_id(1)
    L = WKV_CHUNK
    P2 = 2 * L

    @pl.when(c == 0)
    def _():
        st_ref[...] = jnp.zeros_like(st_ref)

    ld_all = ld_ref[...]
    tri = (lax.broadcasted_iota(jnp.int32, (L, L), 0) >= lax.broadcasted_iota(jnp.int32, (L, L), 1)).astype(BF16)
    hi = ld_all.astype(BF16)
    r1 = ld_all - hi.astype(F32)
    mid = r1.astype(BF16)
    lo = (r1 - mid.astype(F32)).astype(BF16)
    cum_all = _dot(tri, hi) + _dot(tri, mid) + _dot(tri, lo)

    lane = lax.broadcasted_iota(jnp.int32, (1, LANES), 1)
    first = lane < RWKV_HD
    ri = lax.broadcasted_iota(jnp.int32, (P2, P2), 0)
    ci = lax.broadcasted_iota(jnp.int32, (P2, P2), 1)
    same_head = (ri // L) == (ci // L)
    rt, ct = ri % L, ci % L
    strict = jnp.logical_and(same_head, rt > ct)
    incl = jnp.logical_and(same_head, rt >= ct)
    eye = ri == ci

    def stack(xv):
        return jnp.concatenate([jnp.where(first, xv, 0.0), jnp.where(first, 0.0, xv)], axis=0).astype(BF16)

    pairs = range(RWKV_HEADS // 2)
    sls = [slice(p * LANES, (p + 1) * LANES) for p in pairs]
    ws, us, ks, rs, ul, kl, vs, g_l, bonus = ([] for _ in range(9))
    for sl in sls:
        rp, kp, vp, ap = (ref[:, sl].astype(F32) for ref in (r_ref, k_ref, v_ref, a_ref))
        ldp, cum = ld_all[:, sl], cum_all[:, sl]
        kk = kp * kk_ref[:, sl]
        kk = kk / jnp.maximum(jnp.sqrt(_seg_sum(kk * kk, first)), 1e-12)
        kmod = kp * (1.0 + (ap - 1.0) * ka_ref[:, sl])
        bp = kk * ap
        cum_l = cum[L - 1:L, :]
        g_inv = jnp.exp(-cum)
        g_to_end = jnp.exp(cum_l - cum)
        ws.append(stack(kk * jnp.exp(cum - ldp)))
        us.append(stack(bp * g_inv))
        ks.append(stack(kmod * g_inv))
        rs.append(stack(rp * jnp.exp(cum)))
        ul.append(stack(bp * g_to_end))
        kl.append(stack(kmod * g_to_end))
        vs.append(stack(vp))
        g_l.append(jnp.exp(cum_l))
        bonus.append(_seg_sum(rp * kmod * rk_ref[:, sl], first) * vp)

    gram = [_dot_nt(jnp.concatenate([ws[p], rs[p]], axis=0), jnp.concatenate([us[p], ks[p]], axis=0)) for p in pairs]
    n_mat = [jnp.where(strict, gram[p][:P2, :P2], 0.0) for p in pairs]
    m_mat = [jnp.where(strict, gram[p][:P2, P2:], 0.0).astype(BF16) for p in pairs]
    nr_mat = [jnp.where(incl, gram[p][P2:, :P2], 0.0).astype(BF16) for p in pairs]
    mr_mat = [jnp.where(incl, gram[p][P2:, P2:], 0.0).astype(BF16) for p in pairs]

    def level_mask(sz):
        sub = jnp.logical_and((rt // sz) % 2 == 1, (ct // sz) % 2 == 0)
        return jnp.logical_and(jnp.logical_and(sub, (rt // (2 * sz)) == (ct // (2 * sz))), same_head)

    x_inv = [jnp.where(eye, 1.0, 0.0) - jnp.where(level_mask(1), n_mat[p], 0.0) for p in pairs]
    sz = 2
    while sz < L:
        mask = level_mask(sz)
        xb = [x_inv[p].astype(BF16) for p in pairs]
        xc = [_dot(xb[p], jnp.where(mask, n_mat[p], 0.0).astype(BF16)).astype(BF16) for p in pairs]
        x_inv = [x_inv[p] - _dot(xc[p], xb[p]) for p in pairs]
        sz *= 2

    a0 = [st_ref[p] for p in pairs]
    a0b = [a0[p].astype(BF16) for p in pairs]
    rhs = [_dot(jnp.concatenate([ws[p], m_mat[p]], axis=1), jnp.concatenate([a0b[p], vs[p]], axis=0)).astype(BF16)
           for p in pairs]
    pm = [(-_dot(x_inv[p].astype(BF16), rhs[p])).astype(BF16) for p in pairs]
    o_st = [_dot(jnp.concatenate([rs[p], nr_mat[p], mr_mat[p]], axis=1),
                 jnp.concatenate([a0b[p], pm[p], vs[p]], axis=0)) for p in pairs]
    for p in pairs:
        g_col = jnp.sum(jnp.where(eye, jnp.broadcast_to(g_l[p], (P2, P2)), 0.0), axis=-1, keepdims=True)
        st_ref[p] = g_col * a0[p] + _dot_tn(jnp.concatenate([ul[p], kl[p]], axis=0),
                                            jnp.concatenate([pm[p], vs[p]], axis=0))

    inv_n = 1.0 / RWKV_HD
    for p, sl in zip(pairs, sls):
        o = o_st[p][:L] + o_st[p][L:]
        mu = _seg_sum(o, first) * inv_n
        oc = o - mu
        var = _seg_sum(oc * oc, first) * inv_n
        on = oc * lax.rsqrt(var + RWKV_GN_EPS) * gg_ref[:, sl] + gb_ref[:, sl]
        o_ref[:, sl] = ((on + bonus[p]) * g_ref[:, sl].astype(F32)).astype(o_ref.dtype)

    s_ref[0] = st_ref[...]


def _wkv_prompt(r, k, v, a, ld, g, hp, n_seq):
    t = r.shape[0]
    L = WKV_CHUNK
    nc = t // n_seq // L
    row = lambda n, c: (n * nc + c, 0)
    blk = pl.BlockSpec((L, D), row)
    npair = RWKV_HEADS // 2
    return pl.pallas_call(
        _wkv_kernel, grid=(n_seq, nc),
        in_specs=[blk] * 6 + [_full((1, D))] * 5,
        out_specs=[blk, pl.BlockSpec((1, npair, LANES, LANES), lambda n, c: (n, 0, 0, 0))],
        out_shape=[jax.ShapeDtypeStruct((t, D), BF16), jax.ShapeDtypeStruct((n_seq, npair, LANES, LANES), F32)],
        scratch_shapes=[pltpu.VMEM((npair, LANES, LANES), F32)],
        compiler_params=_cparams(("parallel", "arbitrary")), name="wkv_chunk")(r, k, v, a, ld, g, *hp)


def _wkv_sample_kernel(r_ref, k_ref, v_ref, a_ref, ld_ref, g_ref, s_ref, kk_ref, ka_ref, rk_ref, gg_ref, gb_ref,
                       o_ref, so_ref):
    r, k, v, a, ld, g = (ref[0] for ref in (r_ref, k_ref, v_ref, a_ref, ld_ref, g_ref))
    kk = k * kk_ref[0]
    kk = kk / jnp.maximum(jnp.sqrt(jnp.sum(kk * kk, axis=0, keepdims=True)), 1e-12)
    kmod = k * (1.0 + (a - 1.0) * ka_ref[0])
    akk = kk * a
    decay = jnp.exp(ld)

    def value_row(vi, carry):
        s = s_ref[0, vi]
        skk = jnp.sum(s * kk, axis=0, keepdims=True)
        s_new = s * decay - skk * akk + v_ref[0, pl.ds(vi, 1), :] * kmod
        so_ref[0, vi] = s_new
        o_ref[0, pl.ds(vi, 1), :] = jnp.sum(s_new * r, axis=0, keepdims=True)
        return carry

    lax.fori_loop(0, s_ref.shape[1], value_row, 0, unroll=4)
    o = o_ref[0]
    mu = jnp.mean(o, axis=0, keepdims=True)
    oc = o - mu
    var = jnp.mean(oc * oc, axis=0, keepdims=True)
    on = oc * lax.rsqrt(var + RWKV_GN_EPS) * gg_ref[0] + gb_ref[0]
    bonus = jnp.sum(r * kmod * rk_ref[0], axis=0, keepdims=True) * v
    o_ref[0] = (on + bonus) * g


def _wkv_sample(r, k, v, a, ld, g, state, hp):
    b = r.shape[0]
    nh, hd = RWKV_HEADS, RWKV_HD
    t3 = lambda z: jnp.transpose(z.reshape(b, nh, hd), (1, 2, 0))
    vec = pl.BlockSpec((1, hd, b), lambda h: (h, 0, 0))
    par = pl.BlockSpec((1, hd, 1), lambda h: (h, 0, 0))
    sblk = pl.BlockSpec((1, hd, hd, b), lambda h: (h, 0, 0, 0))
    o, s_new = pl.pallas_call(
        _wkv_sample_kernel, grid=(nh,),
        in_specs=[vec] * 6 + [sblk] + [par] * 5,
        out_specs=[vec, sblk],
        out_shape=[jax.ShapeDtypeStruct((nh, hd, b), F32), jax.ShapeDtypeStruct((nh, hd, hd, b), F32)],
        compiler_params=_cparams(("parallel",)), name="wkv_sample")(
            t3(r), t3(k), t3(v), t3(a), t3(ld), t3(g), jnp.transpose(state, (1, 2, 3, 0)),
            *[z.reshape(nh, hd, 1) for z in hp])
    return jnp.transpose(o, (2, 0, 1)).reshape(b, D), jnp.transpose(s_new, (3, 0, 1, 2))


def _mem_prompt_kernel(x_ref, wq_ref, mk_ref, mv_ref, wo_ref, g_ref, b_ref, o_ref):
    x = x_ref[...]
    q = _dot(x.astype(BF16), wq_ref[...]).astype(BF16)
    scale = MEM_HD ** -0.5
    outs = []
    for h in range(MEM_HEADS):
        sl = slice(h * MEM_HD, (h + 1) * MEM_HD)
        s = _dot_nt(q[:, sl], mk_ref[0, :, sl]) * scale
        p = jnp.exp(s - jnp.max(s, axis=-1, keepdims=True))
        den = jnp.sum(p, axis=-1, keepdims=True)
        outs.append(_dot((p / den).astype(BF16), mv_ref[0, :, sl]).astype(BF16))
    acc = _dot(jnp.concatenate(outs, axis=-1), wo_ref[...])
    o_ref[...] = _ln(ALPHA * x + acc, g_ref[...], b_ref[...])


def _mem_attn_prompt(x, w_q, mk, mv, w_o, g, b, n_seq, tm):
    t = x.shape[0]
    nb = t // n_seq // tm
    m = mk.shape[1]
    row = lambda n, i: (n * nb + i, 0)
    mem = pl.BlockSpec((1, m, D), lambda n, i: (n, 0, 0))
    return pl.pallas_call(
        _mem_prompt_kernel, grid=(n_seq, nb),
        in_specs=[pl.BlockSpec((tm, D), row), _full((D, D)), mem, mem, _full((D, D)), _full((1, D)), _full((1, D))],
        out_specs=pl.BlockSpec((tm, D), row), out_shape=jax.ShapeDtypeStruct((t, D), F32),
        compiler_params=_cparams(("parallel", "arbitrary")), name="mem_attn")(x, w_q, mk, mv, w_o, g, b)


def _mem_sample_kernel(q_ref, ck_ref, cv_ref, o_ref, *, bs):
    m, nh, hd = ck_ref.shape[1:]
    rows = q_ref.shape[1]
    col_head = lax.broadcasted_iota(jnp.int32, (rows, m * nh), 1) % nh
    own = col_head == lax.broadcasted_iota(jnp.int32, (rows, m * nh), 0)
    scale = MEM_HD ** -0.5
    nb = range(bs)
    s = [jnp.where(own, _dot_nt(q_ref[b].astype(BF16), ck_ref[b].reshape(m * nh, hd).astype(BF16)) * scale, NEG_INF)
         for b in nb]
    p = [jnp.where(own, jnp.exp(s[b] - jnp.max(s[b], axis=-1, keepdims=True)), 0.0) for b in nb]
    den = [jnp.sum(p[b], axis=-1, keepdims=True) for b in nb]
    for b in nb:
        pb = (p[b] / jnp.where(den[b] > 0.0, den[b], 1.0)).astype(BF16)
        o_ref[b] = _dot(pb, cv_ref[b].reshape(m * nh, hd).astype(BF16))


def _mem_attn_sample(q, cache_k, cache_v, layer, bs):
    _, b, m, nh, hd = cache_k.shape
    q3 = jnp.pad(q.reshape(b, nh, hd), ((0, 0), (0, SUBLANES - nh), (0, 0)))
    qb = pl.BlockSpec((bs, SUBLANES, hd), lambda i: (i, 0, 0))
    cb = pl.BlockSpec((None, bs, m, nh, hd), lambda i: (layer, i, 0, 0, 0))
    out = pl.pallas_call(
        functools.partial(_mem_sample_kernel, bs=bs), grid=(b // bs,), in_specs=[qb, cb, cb], out_specs=qb,
        out_shape=jax.ShapeDtypeStruct((b, SUBLANES, hd), F32),
        compiler_params=_cparams(("parallel",)), name="mem_sample")(q3, cache_k, cache_v)
    return out[:, :nh].reshape(b, D)


_PAIRS = ((0, 1), (0, 2), (0, 3), (1, 2), (1, 3), (2, 3))


def _router_kernel(x_ref, rw_ref, rb_ref, bucket_ref, rank_ref, cnt_ref, base_ref, *, tm):
    i = pl.program_id(0)

    @pl.when(i == 0)
    def _():
        base_ref[...] = jnp.zeros_like(base_ref)

    logits = _dot_nt(rw_ref[...], x_ref[...].astype(BF16))
    e = jnp.exp(logits - jnp.max(logits, axis=0, keepdims=True))
    sel = e / jnp.sum(e, axis=0, keepdims=True) + rb_ref[...]
    s = [sel[j:j + 1, :] for j in range(N_EXPERTS)]
    neg = jnp.float32(-jnp.inf)

    best = jnp.zeros((1, tm), jnp.int32)
    best_score = None
    for gi in range(N_GROUPS):
        s0, s1, s2, s3 = s[4 * gi:4 * gi + 4]
        hi01, lo01, hi23, lo23 = jnp.maximum(s0, s1), jnp.minimum(s0, s1), jnp.maximum(s2, s3), jnp.minimum(s2, s3)
        score = jnp.maximum(hi01, hi23) + jnp.maximum(jnp.minimum(hi01, hi23), jnp.maximum(lo01, lo23))
        if gi == 0:
            best_score = score
        else:
            take = score > best_score
            best = jnp.where(take, gi, best)
            best_score = jnp.where(take, score, best_score)
    vals = []
    for j in range(EXPERTS_PER_GROUP):
        vj = s[j]
        for gi in range(1, N_GROUPS):
            vj = jnp.where(best == gi, s[4 * gi + j], vj)
        vals.append(vj)

    def argmax4(v):
        idx, mx = jnp.zeros((1, tm), jnp.int32), v[0]
        for j in range(1, EXPERTS_PER_GROUP):
            take = v[j] > mx
            idx = jnp.where(take, j, idx)
            mx = jnp.where(take, v[j], mx)
        return idx

    i1 = argmax4(vals)
    i2 = argmax4([jnp.where(i1 == j, neg, vals[j]) for j in range(EXPERTS_PER_GROUP)])
    lo, hi = jnp.minimum(i1, i2), jnp.maximum(i1, i2)
    pair = jnp.zeros((1, tm), jnp.int32)
    for pi, (pa, pb) in enumerate(_PAIRS):
        pair = jnp.where(jnp.logical_and(lo == pa, hi == pb), pi, pair)
    bucket = best * len(_PAIRS) + pair
    bucket_ref[0] = bucket

    onehot = (lax.broadcasted_iota(jnp.int32, (BUCKET_ROWS, tm), 0) == bucket).astype(F32)
    upper = (lax.broadcasted_iota(jnp.int32, (tm, tm), 0) <= lax.broadcasted_iota(jnp.int32, (tm, tm), 1)).astype(BF16)
    cum = _dot(onehot.astype(BF16), upper)
    base = base_ref[...]
    rank = jnp.sum(onehot * (cum + base), axis=0, keepdims=True) - 1.0
    rank_ref[0] = rank.astype(jnp.int32)
    base = base + jnp.sum(onehot, axis=1, keepdims=True)
    base_ref[...] = base
    cnt_ref[...] = jnp.broadcast_to(base, cnt_ref.shape)


def _router(x, rw_t, rb, tm):
    t = x.shape[0]
    nb = t // tm
    ib = pl.BlockSpec((1, 1, tm), lambda i: (i, 0, 0))
    bucket, rank, cnt = pl.pallas_call(
        functools.partial(_router_kernel, tm=tm), grid=(nb,),
        in_specs=[pl.BlockSpec((tm, D), lambda i: (i, 0)), _full(rw_t.shape), _full(rb.shape)],
        out_specs=[ib, ib, _full((BUCKET_ROWS, LANES))],
        out_shape=[jax.ShapeDtypeStruct((nb, 1, tm), jnp.int32), jax.ShapeDtypeStruct((nb, 1, tm), jnp.int32),
                   jax.ShapeDtypeStruct((BUCKET_ROWS, LANES), F32)],
        scratch_shapes=[pltpu.VMEM((BUCKET_ROWS, 1), F32)],
        compiler_params=_cparams(("arbitrary",)), name="router")(x, rw_t, rb)
    return bucket.reshape(t), rank.reshape(t), cnt[:N_BUCKETS, 0].astype(jnp.int32)


def _row_copies(idx_ref, base, src_hbm, dst, sem, n, wait):
    n_prio = 2

    def body(j, carry):
        for k in range(n_prio):
            r = j * n_prio + k
            cp = pltpu.make_async_copy(src_hbm.at[pl.ds(idx_ref[base + r], 1)], dst.at[pl.ds(r, 1)], sem)
            if wait:
                cp.wait()
            else:
                cp.start(priority=k)
        return carry

    assert n % n_prio == 0
    lax.fori_loop(0, n // n_prio, body, 0, unroll=4)


def _ffn_kernel(src_ref, lo_ref, hi_ref, used_ref, x_hbm, rw_ref, g0_ref, u0_ref, d0_ref, g1_ref, u1_ref, d1_ref,
                o_ref, xbuf, sem, *, blk):
    i = pl.program_id(0)
    used = used_ref[0]
    slot = i % 2

    @pl.when(jnp.logical_and(i == 0, used > 0))
    def _():
        _row_copies(src_ref, 0, x_hbm, xbuf.at[0], sem.at[0], blk, False)

    @pl.when(i + 1 < used)
    def _():
        _row_copies(src_ref, (i + 1) * blk, x_hbm, xbuf.at[1 - slot], sem.at[1 - slot], blk, False)

    @pl.when(i < used)
    def _():
        _row_copies(src_ref, i * blk, x_hbm, xbuf.at[slot], sem.at[slot], blk, True)
        xb = xbuf[slot].astype(BF16)
        logits = _dot(xb, rw_ref[...])
        lane = lax.broadcasted_iota(jnp.int32, logits.shape, 1)
        l_lo = jnp.sum(jnp.where(lane == lo_ref[i], logits, 0.0), axis=-1, keepdims=True)
        l_hi = jnp.sum(jnp.where(lane == hi_ref[i], logits, 0.0), axis=-1, keepdims=True)
        w_lo = _sigmoid(l_lo - l_hi)

        def expert(g_ref, u_ref, d_ref):
            gate = _dot(xb, g_ref[0])
            act = gate * _sigmoid(gate) * _dot(xb, u_ref[0])
            return _dot(act.astype(BF16), d_ref[0])

        y_lo = expert(g0_ref, u0_ref, d0_ref)
        y_hi = expert(g1_ref, u1_ref, d1_ref)
        o_ref[...] = w_lo * y_lo + (1.0 - w_lo) * y_hi

    @pl.when(i >= used)
    def _():
        o_ref[...] = jnp.zeros_like(o_ref)


def _ffn(x, src, blk_lo, blk_hi, n_used, rw, w_gate, w_up, w_down, layer, blk):
    rows = src.shape[0]
    nblk = rows // blk
    wg = lambda sel: pl.BlockSpec((None, 1, D, EXPERT_FF),
                                  lambda i, s, lo, hi, used: (layer, (lo, hi)[sel][i], 0, 0))
    wd = lambda sel: pl.BlockSpec((None, 1, EXPERT_FF, D),
                                  lambda i, s, lo, hi, used: (layer, (lo, hi)[sel][i], 0, 0))
    return pl.pallas_call(
        functools.partial(_ffn_kernel, blk=blk),
        grid_spec=pltpu.PrefetchScalarGridSpec(
            num_scalar_prefetch=4, grid=(nblk,),
            in_specs=[pl.BlockSpec(memory_space=pl.ANY), pl.BlockSpec(rw.shape, lambda i, s, lo, hi, used: (0, 0)),
                      wg(0), wg(0), wd(0), wg(1), wg(1), wd(1)],
            out_specs=pl.BlockSpec((blk, D), lambda i, s, lo, hi, used: (i, 0)),
            scratch_shapes=[pltpu.VMEM((2, blk, D), F32), pltpu.SemaphoreType.DMA((2,))]),
        out_shape=jax.ShapeDtypeStruct((rows, D), F32),
        compiler_params=_cparams(("arbitrary",)), name="moe_ffn")(
            src, blk_lo, blk_hi, n_used, x, rw, w_gate, w_up, w_down, w_gate, w_up, w_down)


def _combine_ln_kernel(dest_ref, x_ref, y_hbm, g_ref, b_ref, o_ref, ybuf, sem, *, tm):
    i = pl.program_id(0)
    slot = i % 2

    @pl.when(i == 0)
    def _():
        _row_copies(dest_ref, 0, y_hbm, ybuf.at[0], sem.at[0], tm, False)

    @pl.when(i + 1 < pl.num_programs(0))
    def _():
        _row_copies(dest_ref, (i + 1) * tm, y_hbm, ybuf.at[1 - slot], sem.at[1 - slot], tm, False)

    _row_copies(dest_ref, i * tm, y_hbm, ybuf.at[slot], sem.at[slot], tm, True)
    o_ref[...] = _ln(ALPHA * x_ref[...] + ybuf[slot], g_ref[...], b_ref[...])


def _combine_ln(x, y_rows, dest, g, b, tm):
    t = x.shape[0]
    rowb = pl.BlockSpec((tm, D), lambda i, d: (i, 0))
    vec = pl.BlockSpec((1, D), lambda i, d: (0, 0))
    return pl.pallas_call(
        functools.partial(_combine_ln_kernel, tm=tm),
        grid_spec=pltpu.PrefetchScalarGridSpec(
            num_scalar_prefetch=1, grid=(t // tm,),
            in_specs=[rowb, pl.BlockSpec(memory_space=pl.ANY), vec, vec], out_specs=rowb,
            scratch_shapes=[pltpu.VMEM((2, tm, D), F32), pltpu.SemaphoreType.DMA((2,))]),
        out_shape=jax.ShapeDtypeStruct((t, D), F32),
        compiler_params=_cparams(("arbitrary",)), name="moe_combine_ln")(dest, x, y_rows, g, b)


def _invert_rows_kernel(dest_ref, src_ref):
    def clear(r, carry):
        src_ref[r] = 0
        return carry

    def put(tok, carry):
        src_ref[dest_ref[tok]] = tok
        return carry

    lax.fori_loop(0, src_ref.shape[0], clear, 0, unroll=8)
    lax.fori_loop(0, dest_ref.shape[0], put, 0, unroll=8)


def _invert_rows(dest, rows):
    smem = pl.BlockSpec(memory_space=pltpu.SMEM)
    return pl.pallas_call(
        _invert_rows_kernel, in_specs=[smem], out_specs=smem,
        out_shape=jax.ShapeDtypeStruct((rows,), jnp.int32), name="invert_rows")(dest)


def _moe_ln(x, rw_t, rb, rw_pad, w_gate, w_up, w_down, layer, g, b, tm_router, blk, tm_comb):
    t = x.shape[0]
    bucket, rank, counts = _router(x, rw_t, rb, tm_router)
    padded = (counts + blk - 1) // blk * blk
    ends = jnp.cumsum(padded)
    dest = ((ends - padded)[bucket] + rank).astype(jnp.int32)
    nblk = t // blk + N_BUCKETS
    src = _invert_rows(dest, nblk * blk)
    blk_bucket = jnp.minimum(jnp.searchsorted(ends, jnp.arange(nblk) * blk, side='right'), N_BUCKETS - 1)
    pair_lo = jnp.array([p[0] for p in _PAIRS], jnp.int32)
    pair_hi = jnp.array([p[1] for p in _PAIRS], jnp.int32)
    grp, pr = blk_bucket // len(_PAIRS), blk_bucket % len(_PAIRS)
    blk_lo = (grp * EXPERTS_PER_GROUP + pair_lo[pr]).astype(jnp.int32)
    blk_hi = (grp * EXPERTS_PER_GROUP + pair_hi[pr]).astype(jnp.int32)
    n_used = (ends[-1:] // blk).astype(jnp.int32)
    y_rows = _ffn(x, src, blk_lo, blk_hi, n_used, rw_pad, w_gate, w_up, w_down, layer, blk)
    return _combine_ln(x, y_rows, dest, g, b, tm_comb)


def _moe_dense_kernel(x_ref, lo_ref, hi_ref, rw_ref, wg_ref, wu_ref, wd_ref, g_ref, b_ref, o_ref, acc_ref):
    e = pl.program_id(0)

    @pl.when(e == 0)
    def _():
        acc_ref[...] = jnp.zeros_like(acc_ref)

    x = x_ref[...]
    xb = x.astype(BF16)
    logits = _dot(xb, rw_ref[...])
    lane = lax.broadcasted_iota(jnp.int32, logits.shape, 1)
    lo, hi = lo_ref[...], hi_ref[...]
    l_lo = jnp.sum(jnp.where(lane == lo, logits, 0.0), axis=-1, keepdims=True)
    l_hi = jnp.sum(jnp.where(lane == hi, logits, 0.0), axis=-1, keepdims=True)
    w_lo = _sigmoid(l_lo - l_hi)
    coef = jnp.where(lo == e, w_lo, 0.0) + jnp.where(hi == e, 1.0 - w_lo, 0.0)
    gate = _dot(xb, wg_ref[0])
    act = gate * _sigmoid(gate) * _dot(xb, wu_ref[0])
    acc_ref[...] += coef * _dot(act.astype(BF16), wd_ref[0])

    @pl.when(e == pl.num_programs(0) - 1)
    def _():
        o_ref[...] = _ln(ALPHA * x + acc_ref[...], g_ref[...], b_ref[...])


def _moe_ln_dense(x, rw_t, rb, rw_pad, w_gate, w_up, w_down, layer, g, b):
    t = x.shape[0]
    bucket, _, _ = _router(x, rw_t, rb, t)
    pair_lo = jnp.array([p[0] for p in _PAIRS], jnp.int32)
    pair_hi = jnp.array([p[1] for p in _PAIRS], jnp.int32)
    grp, pr = bucket // len(_PAIRS), bucket % len(_PAIRS)
    lo = (grp * EXPERTS_PER_GROUP + pair_lo[pr]).astype(jnp.int32).reshape(t, 1)
    hi = (grp * EXPERTS_PER_GROUP + pair_hi[pr]).astype(jnp.int32).reshape(t, 1)
    wg = pl.BlockSpec((None, 1, D, EXPERT_FF), lambda e: (layer, e, 0, 0))
    wd = pl.BlockSpec((None, 1, EXPERT_FF, D), lambda e: (layer, e, 0, 0))
    return pl.pallas_call(
        _moe_dense_kernel, grid=(N_EXPERTS,),
        in_specs=[_full((t, D)), _full((t, 1)), _full((t, 1)), _full(rw_pad.shape), wg, wg, wd,
                  _full((1, D)), _full((1, D))],
        out_specs=_full((t, D)), out_shape=jax.ShapeDtypeStruct((t, D), F32),
        scratch_shapes=[pltpu.VMEM((t, D), F32)],
        compiler_params=_cparams(("arbitrary",)), name="moe_dense")(x, lo, hi, rw_pad, w_gate, w_up, w_down, g, b)


def kernel(x_prompt, x_sample, cache_swa_k, cache_swa_v, state_lru_conv, state_lru_h, state_rwkv_shift, state_rwkv_wkv, cache_mem_k, cache_mem_v, mem_prompt, swa_w_qkv, swa_sinks, swa_w_o, lru_w_in, lru_b_in, lru_conv_w, lru_conv_b, lru_w_a, lru_b_a, lru_w_i, lru_b_i, lru_lambda, lru_w_o, rwkv_mu, rwkv_w_r, rwkv_w_k, rwkv_w_v, rwkv_w0, rwkv_w1, rwkv_w2, rwkv_a0, rwkv_a1, rwkv_a2, rwkv_g1, rwkv_g2, rwkv_k_k, rwkv_k_a, rwkv_r_k, rwkv_gn_g, rwkv_gn_b, rwkv_w_o, mem_w_q, mem_w_kv, mem_w_o, ln_g, ln_b, router_w, router_b, moe_w_gate, moe_w_up, moe_w_down):
    n_p, seq, _ = x_prompt.shape
    n_s, dec_seq, _ = x_sample.shape
    assert dec_seq == 1
    past_len = 8192
    xp = x_prompt.reshape(n_p * seq, D)
    xs = x_sample.reshape(n_s, D)
    row = lambda v: v.reshape(1, -1)
    bf = lambda w: w.astype(BF16)

    rw_t = bf(router_w.T)
    rb = router_b.reshape(N_EXPERTS, 1)
    rw_pad = bf(jnp.pad(router_w, ((0, 0), (0, LANES - N_EXPERTS))))
    wg, wu, wd = bf(moe_w_gate), bf(moe_w_up), bf(moe_w_down)
    mem_p = mem_prompt.reshape(n_p * mem_prompt.shape[1], D)
    m_len = mem_prompt.shape[1]

    swa_k_p, swa_v_p, swa_k_s, swa_v_s = [], [], [], []
    lru_c_p, lru_h_p, lru_c_s, lru_h_s = [], [], [], []
    rw_x_p, rw_s_p, rw_x_s, rw_s_s = [], [], [], []
    mem_k_p, mem_v_p = [], []

    for layer in range(DEPTH):
        kind, i = layer % N_MIXERS, layer // N_MIXERS
        g0, b0 = row(ln_g[layer, 0]), row(ln_b[layer, 0])
        if kind == 0:
            w_qkv, w_o = bf(swa_w_qkv[i]), bf(swa_w_o[i])
            keep = min(WINDOW, seq)
            q, k, v, kv_last = _swa_qkv(xp, w_qkv, jnp.arange(seq), n_p, 512, keep, BF16)
            o = _swa_attn_prompt(q, k, v, swa_sinks[i], n_p, 2)
            swa_k_p.append(kv_last[:, :, :KV_WIDTH].reshape(n_p, keep, SWA_KV_HEADS, HEAD_DIM))
            swa_v_p.append(kv_last[:, :, KV_WIDTH:].reshape(n_p, keep, SWA_KV_HEADS, HEAD_DIM))
            xp = _proj_ln(o, w_o, xp, g0, b0, 512)

            qs, _, _, kv_new = _swa_qkv(xs, w_qkv, jnp.full((n_s,), past_len), 1, n_s, n_s, F32)
            kn, vn = kv_new[0, :, :KV_WIDTH], kv_new[0, :, KV_WIDTH:]
            os_ = _swa_attn_sample(qs, kn, vn, cache_swa_k, cache_swa_v, i, swa_sinks[i], 8)
            wb = cache_swa_k.shape[2]
            k_all = jnp.concatenate([cache_swa_k[i], kn.reshape(n_s, 1, SWA_KV_HEADS, HEAD_DIM)], axis=1)
            v_all = jnp.concatenate([cache_swa_v[i], vn.reshape(n_s, 1, SWA_KV_HEADS, HEAD_DIM)], axis=1)
            swa_k_s.append(k_all[:, -wb:])
            swa_v_s.append(v_all[:, -wb:])
            xs = _proj_ln(os_, w_o, xs, g0, b0, n_s)
        elif kind == 1:
            wts = _lru_weights(lru_w_in[i], lru_b_in[i], lru_conv_w[i], lru_conv_b[i], lru_w_a[i], lru_b_a[i],
                               lru_w_i[i], lru_b_i[i], lru_lambda[i], lru_w_o[i])
            xp, conv_last, h_last = _lru_prompt(xp, wts, g0, b0, n_p, 256)
            lru_c_p.append(conv_last[:, SUBLANES - (CONV_W - 1):])
            lru_h_p.append(h_last[:, SUBLANES - 1])
            xs, xb_s, h_s = _lru_sample(xs, state_lru_conv[i], state_lru_h[i], wts, g0, b0)
            lru_c_s.append(jnp.concatenate([state_lru_conv[i][:, 1:], xb_s[:, None]], axis=1))
            lru_h_s.append(h_s)
        else:
            wts = (rwkv_mu[i], bf(rwkv_w_r[i]), bf(rwkv_w_k[i]), bf(rwkv_w_v[i]), row(rwkv_w0[i]), bf(rwkv_w1[i]),
                   bf(rwkv_w2[i]), row(rwkv_a0[i]), bf(rwkv_a1[i]), bf(rwkv_a2[i]), bf(rwkv_g1[i]), bf(rwkv_g2[i]))
            hp = (row(rwkv_k_k[i]), row(rwkv_k_a[i]), row(rwkv_r_k[i]), row(rwkv_gn_g[i]), row(rwkv_gn_b[i]))
            w_o = bf(rwkv_w_o[i])
            rw_x_p.append(xp.reshape(n_p, seq, D)[:, -1])
            rw_x_s.append(xs)
            r, k, v, a, ld, g = _rwkv_pre(xp, jnp.zeros((n_p, SUBLANES, D), F32), wts, n_p, 256, True, BF16)
            o, st = _wkv_prompt(r, k, v, a, ld, g, hp, n_p)
            hd = RWKV_HD
            st = jnp.stack([st[:, :, :hd, :hd], st[:, :, hd:, hd:]], axis=2).reshape(n_p, RWKV_HEADS, hd, hd)
            rw_s_p.append(jnp.swapaxes(st, -1, -2))
            xp = _proj_ln(o, w_o, xp, g0, b0, 512)

            r, k, v, a, ld, g = _rwkv_pre(xs, state_rwkv_shift[i], wts, 1, n_s, False, F32)
            os_, s_new = _wkv_sample(r, k, v, a, ld, g, state_rwkv_wkv[i], hp)
            rw_s_s.append(s_new)
            xs = _proj_ln(os_, w_o, xs, g0, b0, n_s)

        g1, b1 = row(ln_g[layer, 1]), row(ln_b[layer, 1])
        w_q, w_o = bf(mem_w_q[layer]), bf(mem_w_o[layer])
        mkv = _matmul(mem_p, bf(mem_w_kv[layer]), 512)
        mk, mv = mkv[:, :D], mkv[:, D:]
        mem_k_p.append(mk.reshape(n_p, m_len, MEM_HEADS, MEM_HD))
        mem_v_p.append(mv.reshape(n_p, m_len, MEM_HEADS, MEM_HD))
        xp = _mem_attn_prompt(xp, w_q, bf(mk).reshape(n_p, m_len, D), bf(mv).reshape(n_p, m_len, D), w_o, g1, b1,
                              n_p, 512)
        qs = _matmul(xs, w_q, n_s)
        os_ = _mem_attn_sample(qs, cache_mem_k, cache_mem_v, layer, 4)
        xs = _proj_ln(os_, w_o, xs, g1, b1, n_s)

        g2, b2 = row(ln_g[layer, 2]), row(ln_b[layer, 2])
        xp = _moe_ln(xp, rw_t, rb, rw_pad, wg, wu, wd, layer, g2, b2, 512, 256, 256)
        xs = _moe_ln_dense(xs, rw_t, rb, rw_pad, wg, wu, wd, layer, g2, b2)

    return (xp.reshape(n_p, seq, D), xs.reshape(n_s, 1, D),
            jnp.stack(swa_k_p), jnp.stack(swa_v_p), jnp.stack(lru_c_p), jnp.stack(lru_h_p),
            jnp.stack(rw_x_p), jnp.stack(rw_s_p), jnp.stack(mem_k_p), jnp.stack(mem_v_p),
            jnp.stack(swa_k_s), jnp.stack(swa_v_s), jnp.stack(lru_c_s), jnp.stack(lru_h_s),
            jnp.stack(rw_x_s), jnp.stack(rw_s_s))
```

```python
import functools

import jax
import jax.numpy as jnp
from jax import lax
from jax.experimental import pallas as pl
from jax.experimental.pallas import tpu as pltpu

F32 = jnp.float32
BF16 = jnp.bfloat16

D = 1024
DEPTH = 4
N_MIXERS = 3
HEAD_DIM = 64
SWA_HEADS = D // HEAD_DIM
SWA_KV_HEADS = 4
SWA_GROUP = SWA_HEADS // SWA_KV_HEADS
Q_WIDTH = SWA_HEADS * HEAD_DIM
KV_WIDTH = SWA_KV_HEADS * HEAD_DIM
WINDOW = 128
ROT_DIM = HEAD_DIM // 4
ROPE_THETA = 500000.0
LRU_BLOCKS = 16
CONV_W = 4
LRU_C = 8.0
RWKV_HEADS = 16
RWKV_HD = 64
RWKV_GN_EPS = 64e-5
MEM_HEADS = 4
MEM_HD = D // MEM_HEADS
N_EXPERTS = 16
N_GROUPS = 4
EXPERTS_PER_GROUP = 4
EXPERT_FF = 512
LN_EPS = 1e-5
ALPHA = (2.0 * DEPTH) ** 0.25
NEG_INF = -1e30

LANES = 128
SUBLANES = 8
VMEM_LIMIT = 56 * 1024 * 1024
WKV_CHUNK = 64
N_BUCKETS = N_GROUPS * 6
BUCKET_ROWS = 32


def _cparams(sem):
    return pltpu.CompilerParams(dimension_semantics=sem, vmem_limit_bytes=VMEM_LIMIT)


def _dot(a, b):
    return jnp.dot(a, b, preferred_element_type=F32)


def _dot_nt(a, b):
    return lax.dot_general(a, b, (((1,), (1,)), ((), ())), preferred_element_type=F32)


def _dot_tn(a, b):
    return lax.dot_general(a, b, (((0,), (0,)), ((), ())), preferred_element_type=F32)


def _ln(z, g, b):
    mu = jnp.mean(z, axis=-1, keepdims=True)
    zc = z - mu
    var = jnp.mean(zc * zc, axis=-1, keepdims=True)
    return zc * lax.rsqrt(var + LN_EPS) * g + b


def _softplus(z):
    return jnp.maximum(z, 0.0) + jnp.log1p(jnp.exp(-jnp.abs(z)))


def _sigmoid(z):
    return 1.0 / (1.0 + jnp.exp(-z))


def _round_bf16(x):
    return x.astype(BF16).astype(F32)


def _full(shape):
    nd = len(shape)
    return pl.BlockSpec(shape, lambda *_: (0,) * nd)


def _mm_kernel(a_ref, w_ref, o_ref):
    o_ref[...] = _dot(a_ref[...].astype(BF16), w_ref[...]).astype(o_ref.dtype)


def _matmul(a, w, tm, out_dtype=F32):
    t, k = a.shape
    n = w.shape[1]
    return pl.pallas_call(
        _mm_kernel, grid=(t // tm,),
        in_specs=[pl.BlockSpec((tm, k), lambda i: (i, 0)), _full((k, n))],
        out_specs=pl.BlockSpec((tm, n), lambda i: (i, 0)),
        out_shape=jax.ShapeDtypeStruct((t, n), out_dtype),
        compiler_params=_cparams(("parallel",)), name="matmul")(a, w)


def _proj_ln_kernel(a_ref, w_ref, x_ref, g_ref, b_ref, o_ref):
    acc = _dot(a_ref[...].astype(BF16), w_ref[...])
    o_ref[...] = _ln(ALPHA * x_ref[...] + acc, g_ref[...], b_ref[...])


def _proj_ln(a, w, x, g, b, tm):
    t, k = a.shape
    return pl.pallas_call(
        _proj_ln_kernel, grid=(t // tm,),
        in_specs=[pl.BlockSpec((tm, k), lambda i: (i, 0)), _full((k, D)),
                  pl.BlockSpec((tm, D), lambda i: (i, 0)), _full((1, D)), _full((1, D))],
        out_specs=pl.BlockSpec((tm, D), lambda i: (i, 0)),
        out_shape=jax.ShapeDtypeStruct((t, D), F32),
        compiler_params=_cparams(("parallel",)), name="proj_ln")(a, w, x, g, b)


def _rope_tables(pos):
    half = ROT_DIM // 2
    inv_freq = ROPE_THETA ** (-jnp.arange(half, dtype=F32) / half)
    ang = pos.astype(F32)[:, None] * inv_freq
    cos, sin = jnp.cos(ang), jnp.sin(ang)
    one = jnp.ones((pos.shape[0], HEAD_DIM - ROT_DIM), F32)
    zero = jnp.zeros((pos.shape[0], HEAD_DIM - ROT_DIM), F32)
    zh = jnp.zeros_like(sin)
    c = jnp.concatenate([cos, cos, one], axis=1)
    s1 = jnp.concatenate([-sin, zh, zero], axis=1)
    s2 = jnp.concatenate([zh, sin, zero], axis=1)
    rep = LANES // HEAD_DIM
    return jnp.tile(c, (1, rep)), jnp.tile(s1, (1, rep)), jnp.tile(s2, (1, rep))


def _swa_qkv_kernel(x_ref, w_ref, c_ref, s1_ref, s2_ref, q_ref, k_ref, v_ref, kv_ref, *, tm, keep):
    acc = _dot(x_ref[...].astype(BF16), w_ref[...])
    c, s1, s2 = c_ref[...], s1_ref[...], s2_ref[...]
    half = ROT_DIM // 2
    n_q = Q_WIDTH // LANES
    n_k = KV_WIDTH // LANES
    for cg in range(n_q + n_k):
        xg = acc[:, cg * LANES:(cg + 1) * LANES]
        rot = xg * c + pltpu.roll(xg, LANES - half, 1) * s1 + pltpu.roll(xg, half, 1) * s2
        if cg < n_q:
            q_ref[:, cg * LANES:(cg + 1) * LANES] = rot.astype(q_ref.dtype)
        else:
            ck = cg - n_q
            k_ref[:, ck * LANES:(ck + 1) * LANES] = rot.astype(k_ref.dtype)
            kv_ref[0, :, ck * LANES:(ck + 1) * LANES] = rot[tm - keep:, :]
    v = acc[:, Q_WIDTH + KV_WIDTH:]
    v_ref[...] = v.astype(v_ref.dtype)
    kv_ref[0, :, KV_WIDTH:] = v[tm - keep:, :]


def _swa_qkv(x, w_qkv, pos, n_seq, tm, keep, qdtype):
    t = x.shape[0]
    s = t // n_seq
    nb = s // tm
    c, s1, s2 = _rope_tables(pos)
    row = lambda n, i: (n * nb + i, 0)
    tab = pl.BlockSpec((tm, LANES), lambda n, i: (i, 0))
    kern = functools.partial(_swa_qkv_kernel, tm=tm, keep=keep)
    return pl.pallas_call(
        kern, grid=(n_seq, nb),
        in_specs=[pl.BlockSpec((tm, D), row), _full((D, Q_WIDTH + 2 * KV_WIDTH)), tab, tab, tab],
        out_specs=[pl.BlockSpec((tm, Q_WIDTH), row), pl.BlockSpec((tm, KV_WIDTH), row),
                   pl.BlockSpec((tm, KV_WIDTH), row),
                   pl.BlockSpec((1, keep, 2 * KV_WIDTH), lambda n, i: (n, 0, 0))],
        out_shape=[jax.ShapeDtypeStruct((t, Q_WIDTH), qdtype), jax.ShapeDtypeStruct((t, KV_WIDTH), qdtype),
                   jax.ShapeDtypeStruct((t, KV_WIDTH), qdtype),
                   jax.ShapeDtypeStruct((n_seq, keep, 2 * KV_WIDTH), F32)],
        compiler_params=_cparams(("parallel", "arbitrary")), name="swa_qkv")(x, w_qkv, c, s1, s2)


def _swa_attn_kernel(sink_ref, q_ref, kp_ref, kc_ref, vp_ref, vc_ref, o_ref, *, nq):
    j = pl.program_id(1)
    w, grp = WINDOW, SWA_GROUP
    r = lax.broadcasted_iota(jnp.int32, (grp * w, 2 * w), 0) % w
    c = lax.broadcasted_iota(jnp.int32, (grp * w, 2 * w), 1)
    in_prev = jnp.logical_and(c < w, c > r)
    in_cur = jnp.logical_and(c >= w, (c - w) <= r)
    ok_inner = jnp.logical_or(in_prev, in_cur)
    ok_first = jnp.logical_or(jnp.logical_and(in_prev, j > 0), in_cur)
    scale = HEAD_DIM ** -0.5
    combos = [(u, h) for u in range(nq) for h in range(SWA_KV_HEADS)]
    kcat, vcat, q4, sink, ok = [], [], [], [], []
    for u, h in combos:
        sl = slice(h * HEAD_DIM, (h + 1) * HEAD_DIM)
        rows = slice(u * w, (u + 1) * w)
        before = slice((u - 1) * w, u * w)
        k_prev = kp_ref[:, sl] if u == 0 else kc_ref[before, sl]
        v_prev = vp_ref[:, sl] if u == 0 else vc_ref[before, sl]
        kcat.append(jnp.concatenate([k_prev, kc_ref[rows, sl]], axis=0))
        vcat.append(jnp.concatenate([v_prev, vc_ref[rows, sl]], axis=0))
        heads = [h * grp + g for g in range(grp)]
        q4.append(jnp.concatenate([q_ref[rows, hq * HEAD_DIM:(hq + 1) * HEAD_DIM] for hq in heads], axis=0))
        sink.append(jnp.concatenate([jnp.full((w, 1), sink_ref[hq], F32) for hq in heads], axis=0))
        ok.append(ok_first if u == 0 else ok_inner)
    n = range(len(combos))
    s = [jnp.where(ok[i], _dot_nt(q4[i], kcat[i]) * scale, NEG_INF) for i in n]
    m = [jnp.maximum(jnp.max(s[i], axis=-1, keepdims=True), sink[i]) for i in n]
    p = [jnp.exp(s[i] - m[i]) for i in n]
    den = [jnp.sum(p[i], axis=-1, keepdims=True) + jnp.exp(sink[i] - m[i]) for i in n]
    o = [_dot((p[i] / den[i]).astype(BF16), vcat[i]) for i in n]
    for i, (u, h) in enumerate(combos):
        for g in range(grp):
            hq = h * grp + g
            o_ref[u * w:(u + 1) * w, hq * HEAD_DIM:(hq + 1) * HEAD_DIM] = o[i][g * w:(g + 1) * w].astype(o_ref.dtype)


def _swa_attn_prompt(q, k, v, sinks, n_seq, nq):
    t = q.shape[0]
    nb = t // n_seq // WINDOW
    ns = nb // nq
    cur = lambda n, j: (n * ns + j, 0)
    prev = lambda n, j: (n * nb + jnp.maximum(j * nq - 1, 0), 0)
    return pl.pallas_call(
        functools.partial(_swa_attn_kernel, nq=nq), grid=(n_seq, ns),
        in_specs=[pl.BlockSpec(memory_space=pltpu.SMEM), pl.BlockSpec((nq * WINDOW, Q_WIDTH), cur),
                  pl.BlockSpec((WINDOW, KV_WIDTH), prev), pl.BlockSpec((nq * WINDOW, KV_WIDTH), cur),
                  pl.BlockSpec((WINDOW, KV_WIDTH), prev), pl.BlockSpec((nq * WINDOW, KV_WIDTH), cur)],
        out_specs=pl.BlockSpec((nq * WINDOW, Q_WIDTH), cur),
        out_shape=jax.ShapeDtypeStruct((t, Q_WIDTH), BF16),
        compiler_params=_cparams(("parallel", "arbitrary")), name="swa_attn")(sinks, q, k, k, v, v)


def _swa_sample_kernel(sink_ref, qbd_ref, q_ref, kn_ref, vnbd_ref, ckt_ref, cvt_ref, o_ref, *, bs):
    nkv, hd, wb = ckt_ref.shape[1:]
    nq = q_ref.shape[1]
    npad = kn_ref.shape[1]
    key = lax.broadcasted_iota(jnp.int32, (nq, wb), 1)
    valid = (wb - key) < WINDOW
    own_new = (lax.broadcasted_iota(jnp.int32, (nq, npad), 1)
               == lax.broadcasted_iota(jnp.int32, (nq, npad), 0) // SWA_GROUP)
    sink = sink_ref[...]
    scale = HEAD_DIM ** -0.5
    nb = range(bs)
    s = [jnp.where(valid, _dot(qbd_ref[b].astype(BF16), ckt_ref[b].reshape(nkv * hd, wb).astype(BF16)) * scale, NEG_INF)
         for b in nb]
    sn = [jnp.where(own_new, _dot_nt(q_ref[b].astype(BF16), kn_ref[b].astype(BF16)) * scale, NEG_INF) for b in nb]
    m = [jnp.maximum(jnp.maximum(jnp.max(s[b], axis=-1, keepdims=True), jnp.max(sn[b], axis=-1, keepdims=True)), sink)
         for b in nb]
    p = [jnp.where(valid, jnp.exp(s[b] - m[b]), 0.0) for b in nb]
    pn = [jnp.where(own_new, jnp.exp(sn[b] - m[b]), 0.0) for b in nb]
    den = [jnp.sum(p[b], axis=-1, keepdims=True) + jnp.sum(pn[b], axis=-1, keepdims=True) + jnp.exp(sink - m[b])
           for b in nb]
    for b in nb:
        o_ref[b] = (_dot_nt((p[b] / den[b]).astype(BF16), cvt_ref[b].reshape(nkv * hd, wb).astype(BF16))
                    + _dot((pn[b] / den[b]).astype(BF16), vnbd_ref[b].astype(BF16)))


def _swa_attn_sample(q, kn, vn, cache_k, cache_v, layer, sinks, bs):
    _, b, wb, nkv, hd = cache_k.shape
    grp = SWA_HEADS // nkv
    eye = jnp.eye(nkv, dtype=q.dtype)
    qbd = jnp.einsum('bhgd,hk->bhgkd', q.reshape(b, nkv, grp, hd), eye).reshape(b, SWA_HEADS, nkv * hd)
    pad = lambda z: jnp.pad(z, ((0, 0), (0, SUBLANES - nkv), (0, 0)))
    vnbd = pad(jnp.einsum('bhd,hk->bhkd', vn.reshape(b, nkv, hd), eye).reshape(b, nkv, nkv * hd))
    blk = lambda r, w: pl.BlockSpec((bs, r, w), lambda i: (i, 0, 0))
    cblk = pl.BlockSpec((None, bs, nkv, hd, wb), lambda i: (layer, i, 0, 0, 0))
    out = pl.pallas_call(
        functools.partial(_swa_sample_kernel, bs=bs), grid=(b // bs,),
        in_specs=[_full((SWA_HEADS, 1)), blk(SWA_HEADS, nkv * hd), blk(SWA_HEADS, hd), blk(SUBLANES, hd),
                  blk(SUBLANES, nkv * hd), cblk, cblk],
        out_specs=blk(SWA_HEADS, nkv * hd), out_shape=jax.ShapeDtypeStruct((b, SWA_HEADS, nkv * hd), F32),
        compiler_params=_cparams(("parallel",)), name="swa_sample")(
            sinks.reshape(SWA_HEADS, 1), qbd, q.reshape(b, SWA_HEADS, hd), pad(kn.reshape(b, nkv, hd)), vnbd,
            jnp.transpose(cache_k, (0, 1, 3, 4, 2)), jnp.transpose(cache_v, (0, 1, 3, 4, 2)))
    o5 = out.reshape(b, nkv, grp, nkv, hd)
    return jnp.stack([o5[:, h, :, h, :] for h in range(nkv)], axis=1).reshape(b, Q_WIDTH)


def _gelu_tanh(x):
    return 0.5 * x * (1.0 + jnp.tanh(0.7978845608028654 * (x + 0.044715 * x * x * x)))


def _lru_gates(xc, wa_ref, ba, wi_ref, bi, lam):
    xcb = xc.astype(BF16)
    gw = wa_ref.shape[1]
    ra, ia = [], []
    for gi in range(wa_ref.shape[0]):
        xs = xcb[:, gi * gw:(gi + 1) * gw]
        ra.append(_dot(xs, wa_ref[gi]))
        ia.append(_dot(xs, wi_ref[gi]))
    r = _sigmoid(jnp.concatenate(ra, axis=-1) + ba)
    ig = _sigmoid(jnp.concatenate(ia, axis=-1) + bi)
    log_a = -LRU_C * r * _softplus(-lam)
    a = jnp.exp(log_a)
    b = jnp.sqrt(-jnp.tanh(log_a) * (a * a + 1.0)) * (ig * xc)
    return a, b


def _shift_rows(ext, s, tm):
    return pltpu.roll(ext, s, 0)[SUBLANES:SUBLANES + tm]


def _lru_prompt_kernel(x_ref, win_ref, bin_ref, cw_ref, cb_ref, wa_ref, ba_ref, wi_ref, bi_ref, lam_ref,
                       wo_ref, g_ref, b_ref, o_ref, conv_ref, hl_ref, cx_ref, ch_ref, *, tm):
    i = pl.program_id(1)

    @pl.when(i == 0)
    def _():
        cx_ref[...] = jnp.zeros_like(cx_ref)
        ch_ref[...] = jnp.zeros_like(ch_ref)

    x = x_ref[...]
    xy = _dot(x.astype(BF16), win_ref[...]) + bin_ref[...]
    xb = xy[:, :D]
    y_gate = _gelu_tanh(xy[:, D:])
    ext = jnp.concatenate([cx_ref[...], xb], axis=0)
    cw = cw_ref[...]
    xc = cb_ref[...] + xb * cw[CONV_W - 1:CONV_W]
    for s in range(1, CONV_W):
        xc = xc + _shift_rows(ext, s, tm) * cw[CONV_W - 1 - s:CONV_W - s]
    cx_ref[...] = xb[tm - SUBLANES:]
    conv_ref[0] = xb[tm - SUBLANES:]

    a, b = _lru_gates(xc, wa_ref, ba_ref[...], wi_ref, bi_ref[...], lam_ref[...])
    sub = lax.broadcasted_iota(jnp.int32, (tm, 1), 0) % SUBLANES
    s = 1
    while s < SUBLANES:
        keep = sub >= s
        a_sh = jnp.where(keep, pltpu.roll(a, s, 0), 1.0)
        b_sh = jnp.where(keep, pltpu.roll(b, s, 0), 0.0)
        b = a * b_sh + b
        a = a * a_sh
        s *= 2
    carry = ch_ref[SUBLANES - 1:SUBLANES, :]
    groups = []
    for gi in range(tm // SUBLANES):
        rows = slice(gi * SUBLANES, (gi + 1) * SUBLANES)
        hg = a[rows] * carry + b[rows]
        groups.append(hg)
        carry = hg[SUBLANES - 1:SUBLANES]
    h = jnp.concatenate(groups, axis=0)
    ch_ref[...] = h[tm - SUBLANES:]
    hl_ref[0] = h[tm - SUBLANES:]
    acc = _dot((h * y_gate).astype(BF16), wo_ref[...])
    o_ref[...] = _ln(ALPHA * x + acc, g_ref[...], b_ref[...])


def _lru_weights(w_in, b_in, conv_w, conv_b, w_a, b_a, w_i, b_i, lam, w_o):
    gsz = 4
    ng = LRU_BLOCKS // gsz
    bw = D // LRU_BLOCKS

    def grouped(w):
        w4 = w.reshape(ng, gsz, bw, bw)
        return jnp.einsum('gaij,ab->gaibj', w4, jnp.eye(gsz, dtype=w.dtype)).reshape(ng, gsz * bw, gsz * bw).astype(BF16)

    row = lambda v: v.reshape(1, -1)
    return (w_in.astype(BF16), row(b_in), conv_w, row(conv_b), grouped(w_a), row(b_a), grouped(w_i), row(b_i),
            row(lam), w_o.astype(BF16))


def _lru_prompt(x, wts, g, b, n_seq, tm):
    t = x.shape[0]
    nb = t // n_seq // tm
    row = lambda n, i: (n * nb + i, 0)
    last = pl.BlockSpec((1, SUBLANES, D), lambda n, i: (n, 0, 0))
    w_in, b_in, cw, cb, wa, ba, wi, bi, lam, wo = wts
    return pl.pallas_call(
        functools.partial(_lru_prompt_kernel, tm=tm), grid=(n_seq, nb),
        in_specs=[pl.BlockSpec((tm, D), row), _full(w_in.shape), _full(b_in.shape), _full(cw.shape), _full(cb.shape),
                  _full(wa.shape), _full(ba.shape), _full(wi.shape), _full(bi.shape), _full(lam.shape),
                  _full(wo.shape), _full((1, D)), _full((1, D))],
        out_specs=[pl.BlockSpec((tm, D), row), last, last],
        out_shape=[jax.ShapeDtypeStruct((t, D), F32), jax.ShapeDtypeStruct((n_seq, SUBLANES, D), F32),
                   jax.ShapeDtypeStruct((n_seq, SUBLANES, D), F32)],
        scratch_shapes=[pltpu.VMEM((SUBLANES, D), F32), pltpu.VMEM((SUBLANES, D), F32)],
        compiler_params=_cparams(("parallel", "arbitrary")), name="lru_prompt")(x, *wts, g, b)


def _lru_sample_kernel(x_ref, c0_ref, c1_ref, c2_ref, h0_ref, win_ref, bin_ref, cw_ref, cb_ref, wa_ref, ba_ref,
                       wi_ref, bi_ref, lam_ref, wo_ref, g_ref, b_ref, o_ref, xb_ref, h_ref):
    x = x_ref[...]
    xy = _dot(x.astype(BF16), win_ref[...]) + bin_ref[...]
    xb = xy[:, :D]
    y_gate = _gelu_tanh(xy[:, D:])
    cw = cw_ref[...]
    xc = (cb_ref[...] + c0_ref[...] * cw[0:1] + c1_ref[...] * cw[1:2] + c2_ref[...] * cw[2:3] + xb * cw[3:4])
    a, b = _lru_gates(xc, wa_ref, ba_ref[...], wi_ref, bi_ref[...], lam_ref[...])
    h = a * h0_ref[...] + b
    xb_ref[...] = xb
    h_ref[...] = h
    acc = _dot((h * y_gate).astype(BF16), wo_ref[...])
    o_ref[...] = _ln(ALPHA * x + acc, g_ref[...], b_ref[...])


def _lru_sample(x, conv_state, h0, wts, g, b):
    t = x.shape[0]
    args = (x, conv_state[:, 0], conv_state[:, 1], conv_state[:, 2], h0, *wts, g, b)
    sd = jax.ShapeDtypeStruct((t, D), F32)
    return pl.pallas_call(
        _lru_sample_kernel, grid=(1,),
        in_specs=[_full(a.shape) for a in args],
        out_specs=[_full((t, D))] * 3, out_shape=[sd, sd, sd],
        compiler_params=_cparams(("arbitrary",)), name="lru_sample")(*args)


def _rwkv_pre_kernel(x_ref, xp_ref, mu_ref, wr_ref, wk_ref, wv_ref, w0_ref, w1_ref, w2_ref, a0_ref, a1_ref, a2_ref,
                     g1_ref, g2_ref, r_ref, k_ref, v_ref, a_ref, ld_ref, g_ref, *scratch, tm, seq):
    x = x_ref[...]
    if seq:
        cx_ref, = scratch
        i = pl.program_id(1)

        @pl.when(i == 0)
        def _():
            cx_ref[...] = xp_ref[0]

        x_prev = _shift_rows(jnp.concatenate([cx_ref[...], x], axis=0), 1, tm)
        cx_ref[...] = x[tm - SUBLANES:]
    else:
        x_prev = xp_ref[...]
    xx = x_prev - x
    mu = mu_ref[...]
    mix = lambda j: (x + xx * mu[j:j + 1]).astype(BF16)
    r_ref[...] = _dot(mix(0), wr_ref[...]).astype(r_ref.dtype)
    wl = _dot(jnp.tanh(_dot(mix(1), w1_ref[...])).astype(BF16), w2_ref[...])
    w = -_softplus(-(w0_ref[...] + wl)) - 0.5
    ld_ref[...] = -jnp.exp(w)
    k_ref[...] = _dot(mix(2), wk_ref[...]).astype(k_ref.dtype)
    v_ref[...] = _dot(mix(3), wv_ref[...]).astype(v_ref.dtype)
    al = _dot(_dot(mix(4), a1_ref[...]).astype(BF16), a2_ref[...])
    a_ref[...] = _sigmoid(a0_ref[...] + al).astype(a_ref.dtype)
    g_ref[...] = _dot(_sigmoid(_dot(mix(5), g1_ref[...])).astype(BF16), g2_ref[...]).astype(g_ref.dtype)


def _rwkv_pre(x, x_prev, wts, n_seq, tm, seq, dtype):
    t = x.shape[0]
    nb = t // n_seq // tm
    row = lambda n, i: (n * nb + i, 0)
    xp_spec = pl.BlockSpec((1, SUBLANES, D), lambda n, i: (n, 0, 0)) if seq else pl.BlockSpec((tm, D), row)
    sd = lambda dt: jax.ShapeDtypeStruct((t, D), dt)
    blk = pl.BlockSpec((tm, D), row)
    return pl.pallas_call(
        functools.partial(_rwkv_pre_kernel, tm=tm, seq=seq), grid=(n_seq, nb),
        in_specs=[blk, xp_spec] + [_full(w.shape) for w in wts],
        out_specs=[blk] * 6,
        out_shape=[sd(dtype), sd(dtype), sd(dtype), sd(dtype), sd(F32), sd(dtype)],
        scratch_shapes=[pltpu.VMEM((SUBLANES, D), F32)] if seq else [],
        compiler_params=_cparams(("parallel", "arbitrary")), name="rwkv_pre")(x, x_prev, *wts)


def _seg_sum(x, first):
    s0 = jnp.sum(jnp.where(first, x, 0.0), axis=-1, keepdims=True)
    s1 = jnp.sum(jnp.where(first, 0.0, x), axis=-1, keepdims=True)
    return jnp.where(first, s0, s1)


def _wkv_kernel(r_ref, k_ref, v_ref, a_ref, ld_ref, g_ref, kk_ref, ka_ref, rk_ref, gg_ref, gb_ref,
                o_ref, s_ref, st_ref):
    c = pl.program_id(1)
    L = WKV_CHUNK
    P2 = 2 * L
    nch = r_ref.shape[0] // L

    @pl.when(c == 0)
    def _():
        st_ref[...] = jnp.zeros_like(st_ref)

    ld_all = ld_ref[...]
    tri = (lax.broadcasted_iota(jnp.int32, (L, L), 0) >= lax.broadcasted_iota(jnp.int32, (L, L), 1)).astype(BF16)
    hi = ld_all.astype(BF16)
    r1 = ld_all - hi.astype(F32)
    mid = r1.astype(BF16)
    lo = (r1 - mid.astype(F32)).astype(BF16)
    chunk_rows = [slice(ci * L, (ci + 1) * L) for ci in range(nch)]
    cum_ch = [_dot(tri, hi[rw]) + _dot(tri, mid[rw]) + _dot(tri, lo[rw]) for rw in chunk_rows]

    lane = lax.broadcasted_iota(jnp.int32, (1, LANES), 1)
    first = lane < RWKV_HD
    ri = lax.broadcasted_iota(jnp.int32, (P2, P2), 0)
    ci_ = lax.broadcasted_iota(jnp.int32, (P2, P2), 1)
    same_head = (ri // L) == (ci_ // L)
    rt, ct = ri % L, ci_ % L
    strict = jnp.logical_and(same_head, rt > ct)
    incl = jnp.logical_and(same_head, rt >= ct)
    eye = ri == ci_

    def stack(xv):
        return jnp.concatenate([jnp.where(first, xv, 0.0), jnp.where(first, 0.0, xv)], axis=0).astype(BF16)

    npair = RWKV_HEADS // 2
    combos = [(ci, p) for ci in range(nch) for p in range(npair)]
    n = range(len(combos))
    sls = [slice(p * LANES, (p + 1) * LANES) for p in range(npair)]
    ws, us, ks, rs, ul, kl, vs, g_l, bonus = ([] for _ in range(9))
    for ci, p in combos:
        rw, sl = chunk_rows[ci], sls[p]
        rp, kp, vp, ap = (ref[rw, sl].astype(F32) for ref in (r_ref, k_ref, v_ref, a_ref))
        ldp, cum = ld_all[rw, sl], cum_ch[ci][:, sl]
        kk = kp * kk_ref[:, sl]
        kk = kk / jnp.maximum(jnp.sqrt(_seg_sum(kk * kk, first)), 1e-12)
        kmod = kp * (1.0 + (ap - 1.0) * ka_ref[:, sl])
        bp = kk * ap
        cum_l = cum[L - 1:L, :]
        g_inv = jnp.exp(-cum)
        g_to_end = jnp.exp(cum_l - cum)
        ws.append(stack(kk * jnp.exp(cum - ldp)))
        us.append(stack(bp * g_inv))
        ks.append(stack(kmod * g_inv))
        rs.append(stack(rp * jnp.exp(cum)))
        ul.append(stack(bp * g_to_end))
        kl.append(stack(kmod * g_to_end))
        vs.append(stack(vp))
        g_l.append(jnp.exp(cum_l))
        bonus.append(_seg_sum(rp * kmod * rk_ref[:, sl], first) * vp)

    gram = [_dot_nt(jnp.concatenate([ws[q], rs[q]], axis=0), jnp.concatenate([us[q], ks[q]], axis=0)) for q in n]
    n_mat = [jnp.where(strict, gram[q][:P2, :P2], 0.0) for q in n]
    m_mat = [jnp.where(strict, gram[q][:P2, P2:], 0.0).astype(BF16) for q in n]
    nr_mat = [jnp.where(incl, gram[q][P2:, :P2], 0.0).astype(BF16) for q in n]
    mr_mat = [jnp.where(incl, gram[q][P2:, P2:], 0.0).astype(BF16) for q in n]

    def level_mask(sz):
        sub = jnp.logical_and((rt // sz) % 2 == 1, (ct // sz) % 2 == 0)
        return jnp.logical_and(jnp.logical_and(sub, (rt // (2 * sz)) == (ct // (2 * sz))), same_head)

    x_inv = [jnp.where(eye, 1.0, 0.0) - jnp.where(level_mask(1), n_mat[q], 0.0) for q in n]
    sz = 2
    while sz < L:
        mask = level_mask(sz)
        xb = [x_inv[q].astype(BF16) for q in n]
        xc = [_dot(xb[q], jnp.where(mask, n_mat[q], 0.0).astype(BF16)).astype(BF16) for q in n]
        x_inv = [x_inv[q] - _dot(xc[q], xb[q]) for q in n]
        sz *= 2
    x_inv = [x_inv[q].astype(BF16) for q in n]

    state = [st_ref[p] for p in range(npair)]
    inv_n = 1.0 / RWKV_HD
    for ci in range(nch):
        qs = [ci * npair + p for p in range(npair)]
        a0b = [state[p].astype(BF16) for p in range(npair)]
        rhs = [_dot(jnp.concatenate([ws[q], m_mat[q]], axis=1), jnp.concatenate([a0b[p], vs[q]], axis=0)).astype(BF16)
               for p, q in enumerate(qs)]
        pm = [(-_dot(x_inv[q], rhs[p])).astype(BF16) for p, q in enumerate(qs)]
        o_st = [_dot(jnp.concatenate([rs[q], nr_mat[q], mr_mat[q]], axis=1),
                     jnp.concatenate([a0b[p], pm[p], vs[q]], axis=0)) for p, q in enumerate(qs)]
        new_state = []
        for p, q in enumerate(qs):
            g_col = jnp.sum(jnp.where(eye, jnp.broadcast_to(g_l[q], (P2, P2)), 0.0), axis=-1, keepdims=True)
            new_state.append(g_col * state[p] + _dot_tn(jnp.concatenate([ul[q], kl[q]], axis=0),
                                                        jnp.concatenate([pm[p], vs[q]], axis=0)))
        state = new_state
        for p, q in enumerate(qs):
            sl = sls[p]
            o = o_st[p][:L] + o_st[p][L:]
            mu = _seg_sum(o, first) * inv_n
            oc = o - mu
            var = _seg_sum(oc * oc, first) * inv_n
            on = oc * lax.rsqrt(var + RWKV_GN_EPS) * gg_ref[:, sl] + gb_ref[:, sl]
            o_ref[chunk_rows[ci], sl] = ((on + bonus[q]) * g_ref[chunk_rows[ci], sl].astype(F32)).astype(o_ref.dtype)

    for p in range(npair):
        st_ref[p] = state[p]
    s_ref[0] = st_ref[...]


def _wkv_prompt(r, k, v, a, ld, g, hp, n_seq, nch):
    t = r.shape[0]
    L = WKV_CHUNK * nch
    nc = t // n_seq // L
    row = lambda n, c: (n * nc + c, 0)
    blk = pl.BlockSpec((L, D), row)
    npair = RWKV_HEADS // 2
    return pl.pallas_call(
        _wkv_kernel, grid=(n_seq, nc),
        in_specs=[blk] * 6 + [_full((1, D))] * 5,
        out_specs=[blk, pl.BlockSpec((1, npair, LANES, LANES), lambda n, c: (n, 0, 0, 0))],
        out_shape=[jax.ShapeDtypeStruct((t, D), BF16), jax.ShapeDtypeStruct((n_seq, npair, LANES, LANES), F32)],
        scratch_shapes=[pltpu.VMEM((npair, LANES, LANES), F32)],
        compiler_params=_cparams(("parallel", "arbitrary")), name="wkv_chunk")(r, k, v, a, ld, g, *hp)


def _wkv_sample_kernel(r_ref, k_ref, v_ref, a_ref, ld_ref, g_ref, s_ref, kk_ref, ka_ref, rk_ref, gg_ref, gb_ref,
                       o_ref, so_ref):
    r, k, v, a, ld, g = (ref[0] for ref in (r_ref, k_ref, v_ref, a_ref, ld_ref, g_ref))
    kk = k * kk_ref[0]
    kk = kk / jnp.maximum(jnp.sqrt(jnp.sum(kk * kk, axis=0, keepdims=True)), 1e-12)
    kmod = k * (1.0 + (a - 1.0) * ka_ref[0])
    akk = kk * a
    decay = jnp.exp(ld)

    def value_row(vi, carry):
        s = s_ref[0, vi]
        skk = jnp.sum(s * kk, axis=0, keepdims=True)
        s_new = s * decay - skk * akk + v_ref[0, pl.ds(vi, 1), :] * kmod
        so_ref[0, vi] = s_new
        o_ref[0, pl.ds(vi, 1), :] = jnp.sum(s_new * r, axis=0, keepdims=True)
        return carry

    lax.fori_loop(0, s_ref.shape[1], value_row, 0, unroll=4)
    o = o_ref[0]
    mu = jnp.mean(o, axis=0, keepdims=True)
    oc = o - mu
    var = jnp.mean(oc * oc, axis=0, keepdims=True)
    on = oc * lax.rsqrt(var + RWKV_GN_EPS) * gg_ref[0] + gb_ref[0]
    bonus = jnp.sum(r * kmod * rk_ref[0], axis=0, keepdims=True) * v
    o_ref[0] = (on + bonus) * g


def _wkv_sample(r, k, v, a, ld, g, state, hp):
    b = r.shape[0]
    nh, hd = RWKV_HEADS, RWKV_HD
    t3 = lambda z: jnp.transpose(z.reshape(b, nh, hd), (1, 2, 0))
    vec = pl.BlockSpec((1, hd, b), lambda h: (h, 0, 0))
    par = pl.BlockSpec((1, hd, 1), lambda h: (h, 0, 0))
    sblk = pl.BlockSpec((1, hd, hd, b), lambda h: (h, 0, 0, 0))
    o, s_new = pl.pallas_call(
        _wkv_sample_kernel, grid=(nh,),
        in_specs=[vec] * 6 + [sblk] + [par] * 5,
        out_specs=[vec, sblk],
        out_shape=[jax.ShapeDtypeStruct((nh, hd, b), F32), jax.ShapeDtypeStruct((nh, hd, hd, b), F32)],
        compiler_params=_cparams(("parallel",)), name="wkv_sample")(
            t3(r), t3(k), t3(v), t3(a), t3(ld), t3(g), jnp.transpose(state, (1, 2, 3, 0)),
            *[z.reshape(nh, hd, 1) for z in hp])
    return jnp.transpose(o, (2, 0, 1)).reshape(b, D), jnp.transpose(s_new, (3, 0, 1, 2))


def _mem_prompt_kernel(x_ref, wq_ref, mk_ref, mv_ref, wo_ref, g_ref, b_ref, o_ref):
    x = x_ref[...]
    q = _dot(x.astype(BF16), wq_ref[...]).astype(BF16)
    scale = MEM_HD ** -0.5
    sls = [slice(h * MEM_HD, (h + 1) * MEM_HD) for h in range(MEM_HEADS)]
    s = [_dot_nt(q[:, sl], mk_ref[0, :, sl]) * scale for sl in sls]
    p = [jnp.exp(sh - jnp.max(sh, axis=-1, keepdims=True)) for sh in s]
    den = [jnp.sum(ph, axis=-1, keepdims=True) for ph in p]
    outs = [_dot((ph / dh).astype(BF16), mv_ref[0, :, sl]).astype(BF16) for ph, dh, sl in zip(p, den, sls)]
    acc = _dot(jnp.concatenate(outs, axis=-1), wo_ref[...])
    o_ref[...] = _ln(ALPHA * x + acc, g_ref[...], b_ref[...])


def _mem_attn_prompt(x, w_q, mk, mv, w_o, g, b, n_seq, tm):
    t = x.shape[0]
    nb = t // n_seq // tm
    m = mk.shape[1]
    row = lambda n, i: (n * nb + i, 0)
    mem = pl.BlockSpec((1, m, D), lambda n, i: (n, 0, 0))
    return pl.pallas_call(
        _mem_prompt_kernel, grid=(n_seq, nb),
        in_specs=[pl.BlockSpec((tm, D), row), _full((D, D)), mem, mem, _full((D, D)), _full((1, D)), _full((1, D))],
        out_specs=pl.BlockSpec((tm, D), row), out_shape=jax.ShapeDtypeStruct((t, D), F32),
        compiler_params=_cparams(("parallel", "arbitrary")), name="mem_attn")(x, w_q, mk, mv, w_o, g, b)


def _mem_sample_kernel(q_ref, ck_ref, cv_ref, o_ref, *, bs):
    m, nh, hd = ck_ref.shape[1:]
    rows = q_ref.shape[1]
    col_head = lax.broadcasted_iota(jnp.int32, (rows, m * nh), 1) % nh
    own = col_head == lax.broadcasted_iota(jnp.int32, (rows, m * nh), 0)
    scale = MEM_HD ** -0.5
    nb = range(bs)
    s = [jnp.where(own, _dot_nt(q_ref[b].astype(BF16), ck_ref[b].reshape(m * nh, hd).astype(BF16)) * scale, NEG_INF)
         for b in nb]
    p = [jnp.where(own, jnp.exp(s[b] - jnp.max(s[b], axis=-1, keepdims=True)), 0.0) for b in nb]
    den = [jnp.sum(p[b], axis=-1, keepdims=True) for b in nb]
    for b in nb:
        pb = (p[b] / jnp.where(den[b] > 0.0, den[b], 1.0)).astype(BF16)
        o_ref[b] = _dot(pb, cv_ref[b].reshape(m * nh, hd).astype(BF16))


def _mem_attn_sample(q, cache_k, cache_v, layer, bs):
    _, b, m, nh, hd = cache_k.shape
    q3 = jnp.pad(q.reshape(b, nh, hd), ((0, 0), (0, SUBLANES - nh), (0, 0)))
    qb = pl.BlockSpec((bs, SUBLANES, hd), lambda i: (i, 0, 0))
    cb = pl.BlockSpec((None, bs, m, nh, hd), lambda i: (layer, i, 0, 0, 0))
    out = pl.pallas_call(
        functools.partial(_mem_sample_kernel, bs=bs), grid=(b // bs,), in_specs=[qb, cb, cb], out_specs=qb,
        out_shape=jax.ShapeDtypeStruct((b, SUBLANES, hd), F32),
        compiler_params=_cparams(("parallel",)), name="mem_sample")(q3, cache_k, cache_v)
    return out[:, :nh].reshape(b, D)


_PAIRS = ((0, 1), (0, 2), (0, 3), (1, 2), (1, 3), (2, 3))


def _router_kernel(x_ref, rw_ref, rb_ref, bucket_ref, rank_ref, cnt_ref, base_ref, *, tm):
    i = pl.program_id(0)

    @pl.when(i == 0)
    def _():
        base_ref[...] = jnp.zeros_like(base_ref)

    logits = _dot_nt(rw_ref[...], x_ref[...].astype(BF16))
    e = jnp.exp(logits - jnp.max(logits, axis=0, keepdims=True))
    sel = e / jnp.sum(e, axis=0, keepdims=True) + rb_ref[...]
    s = [sel[j:j + 1, :] for j in range(N_EXPERTS)]
    neg = jnp.float32(-jnp.inf)

    best = jnp.zeros((1, tm), jnp.int32)
    best_score = None
    for gi in range(N_GROUPS):
        s0, s1, s2, s3 = s[4 * gi:4 * gi + 4]
        hi01, lo01, hi23, lo23 = jnp.maximum(s0, s1), jnp.minimum(s0, s1), jnp.maximum(s2, s3), jnp.minimum(s2, s3)
        score = jnp.maximum(hi01, hi23) + jnp.maximum(jnp.minimum(hi01, hi23), jnp.maximum(lo01, lo23))
        if gi == 0:
            best_score = score
        else:
            take = score > best_score
            best = jnp.where(take, gi, best)
            best_score = jnp.where(take, score, best_score)
    vals = []
    for j in range(EXPERTS_PER_GROUP):
        vj = s[j]
        for gi in range(1, N_GROUPS):
            vj = jnp.where(best == gi, s[4 * gi + j], vj)
        vals.append(vj)

    def argmax4(v):
        idx, mx = jnp.zeros((1, tm), jnp.int32), v[0]
        for j in range(1, EXPERTS_PER_GROUP):
            take = v[j] > mx
            idx = jnp.where(take, j, idx)
            mx = jnp.where(take, v[j], mx)
        return idx

    i1 = argmax4(vals)
    i2 = argmax4([jnp.where(i1 == j, neg, vals[j]) for j in range(EXPERTS_PER_GROUP)])
    lo, hi = jnp.minimum(i1, i2), jnp.maximum(i1, i2)
    pair = jnp.zeros((1, tm), jnp.int32)
    for pi, (pa, pb) in enumerate(_PAIRS):
        pair = jnp.where(jnp.logical_and(lo == pa, hi == pb), pi, pair)
    bucket = best * len(_PAIRS) + pair
    bucket_ref[0] = bucket

    onehot = (lax.broadcasted_iota(jnp.int32, (BUCKET_ROWS, tm), 0) == bucket).astype(F32)
    upper = (lax.broadcasted_iota(jnp.int32, (tm, tm), 0) <= lax.broadcasted_iota(jnp.int32, (tm, tm), 1)).astype(BF16)
    cum = _dot(onehot.astype(BF16), upper)
    base = base_ref[...]
    rank = jnp.sum(onehot * (cum + base), axis=0, keepdims=True) - 1.0
    rank_ref[0] = rank.astype(jnp.int32)
    base = base + jnp.sum(onehot, axis=1, keepdims=True)
    base_ref[...] = base
    cnt_ref[...] = jnp.broadcast_to(base, cnt_ref.shape)


def _router(x, rw_t, rb, tm):
    t = x.shape[0]
    nb = t // tm
    ib = pl.BlockSpec((1, 1, tm), lambda i: (i, 0, 0))
    bucket, rank, cnt = pl.pallas_call(
        functools.partial(_router_kernel, tm=tm), grid=(nb,),
        in_specs=[pl.BlockSpec((tm, D), lambda i: (i, 0)), _full(rw_t.shape), _full(rb.shape)],
        out_specs=[ib, ib, _full((BUCKET_ROWS, LANES))],
        out_shape=[jax.ShapeDtypeStruct((nb, 1, tm), jnp.int32), jax.ShapeDtypeStruct((nb, 1, tm), jnp.int32),
                   jax.ShapeDtypeStruct((BUCKET_ROWS, LANES), F32)],
        scratch_shapes=[pltpu.VMEM((BUCKET_ROWS, 1), F32)],
        compiler_params=_cparams(("arbitrary",)), name="router")(x, rw_t, rb)
    return bucket.reshape(t), rank.reshape(t), cnt[:N_BUCKETS, 0].astype(jnp.int32)


def _row_copies(idx_ref, base, src_hbm, dst, sem, n, wait):
    n_prio = 2

    def body(j, carry):
        for k in range(n_prio):
            r = j * n_prio + k
            cp = pltpu.make_async_copy(src_hbm.at[pl.ds(idx_ref[base + r], 1)], dst.at[pl.ds(r, 1)], sem)
            if wait:
                cp.wait()
            else:
                cp.start(priority=k)
        return carry

    assert n % n_prio == 0
    lax.fori_loop(0, n // n_prio, body, 0, unroll=4)


def _ffn_kernel(src_ref, lo_ref, hi_ref, used_ref, x_hbm, rw_ref, g0_ref, u0_ref, d0_ref, g1_ref, u1_ref, d1_ref,
                o_ref, xbuf, sem, *, blk):
    i = pl.program_id(0)
    used = used_ref[0]
    slot = i % 2

    @pl.when(jnp.logical_and(i == 0, used > 0))
    def _():
        _row_copies(src_ref, 0, x_hbm, xbuf.at[0], sem.at[0], blk, False)

    @pl.when(i + 1 < used)
    def _():
        _row_copies(src_ref, (i + 1) * blk, x_hbm, xbuf.at[1 - slot], sem.at[1 - slot], blk, False)

    @pl.when(i < used)
    def _():
        _row_copies(src_ref, i * blk, x_hbm, xbuf.at[slot], sem.at[slot], blk, True)
        xb = xbuf[slot].astype(BF16)
        logits = _dot(xb, rw_ref[...])
        lane = lax.broadcasted_iota(jnp.int32, logits.shape, 1)
        l_lo = jnp.sum(jnp.where(lane == lo_ref[i], logits, 0.0), axis=-1, keepdims=True)
        l_hi = jnp.sum(jnp.where(lane == hi_ref[i], logits, 0.0), axis=-1, keepdims=True)
        w_lo = _sigmoid(l_lo - l_hi)

        def expert(g_ref, u_ref, d_ref):
            gate = _dot(xb, g_ref[0])
            act = gate * _sigmoid(gate) * _dot(xb, u_ref[0])
            return _dot(act.astype(BF16), d_ref[0])

        y_lo = expert(g0_ref, u0_ref, d0_ref)
        y_hi = expert(g1_ref, u1_ref, d1_ref)
        o_ref[...] = w_lo * y_lo + (1.0 - w_lo) * y_hi

    @pl.when(i >= used)
    def _():
        o_ref[...] = jnp.zeros_like(o_ref)


def _ffn(x, src, blk_lo, blk_hi, n_used, rw, w_gate, w_up, w_down, layer, blk):
    rows = src.shape[0]
    nblk = rows // blk
    wg = lambda sel: pl.BlockSpec((None, 1, D, EXPERT_FF),
                                  lambda i, s, lo, hi, used: (layer, (lo, hi)[sel][i], 0, 0))
    wd = lambda sel: pl.BlockSpec((None, 1, EXPERT_FF, D),
                                  lambda i, s, lo, hi, used: (layer, (lo, hi)[sel][i], 0, 0))
    return pl.pallas_call(
        functools.partial(_ffn_kernel, blk=blk),
        grid_spec=pltpu.PrefetchScalarGridSpec(
            num_scalar_prefetch=4, grid=(nblk,),
            in_specs=[pl.BlockSpec(memory_space=pl.ANY), pl.BlockSpec(rw.shape, lambda i, s, lo, hi, used: (0, 0)),
                      wg(0), wg(0), wd(0), wg(1), wg(1), wd(1)],
            out_specs=pl.BlockSpec((blk, D), lambda i, s, lo, hi, used: (i, 0)),
            scratch_shapes=[pltpu.VMEM((2, blk, D), F32), pltpu.SemaphoreType.DMA((2,))]),
        out_shape=jax.ShapeDtypeStruct((rows, D), F32),
        compiler_params=_cparams(("arbitrary",)), name="moe_ffn")(
            src, blk_lo, blk_hi, n_used, x, rw, w_gate, w_up, w_down, w_gate, w_up, w_down)


def _combine_ln_kernel(dest_ref, x_ref, y_hbm, g_ref, b_ref, o_ref, ybuf, sem, *, tm):
    i = pl.program_id(0)
    slot = i % 2

    @pl.when(i == 0)
    def _():
        _row_copies(dest_ref, 0, y_hbm, ybuf.at[0], sem.at[0], tm, False)

    @pl.when(i + 1 < pl.num_programs(0))
    def _():
        _row_copies(dest_ref, (i + 1) * tm, y_hbm, ybuf.at[1 - slot], sem.at[1 - slot], tm, False)

    _row_copies(dest_ref, i * tm, y_hbm, ybuf.at[slot], sem.at[slot], tm, True)
    o_ref[...] = _ln(ALPHA * x_ref[...] + ybuf[slot], g_ref[...], b_ref[...])


def _combine_ln(x, y_rows, dest, g, b, tm):
    t = x.shape[0]
    rowb = pl.BlockSpec((tm, D), lambda i, d: (i, 0))
    vec = pl.BlockSpec((1, D), lambda i, d: (0, 0))
    return pl.pallas_call(
        functools.partial(_combine_ln_kernel, tm=tm),
        grid_spec=pltpu.PrefetchScalarGridSpec(
            num_scalar_prefetch=1, grid=(t // tm,),
            in_specs=[rowb, pl.BlockSpec(memory_space=pl.ANY), vec, vec], out_specs=rowb,
            scratch_shapes=[pltpu.VMEM((2, tm, D), F32), pltpu.SemaphoreType.DMA((2,))]),
        out_shape=jax.ShapeDtypeStruct((t, D), F32),
        compiler_params=_cparams(("arbitrary",)), name="moe_combine_ln")(dest, x, y_rows, g, b)


def _invert_rows_kernel(dest_ref, src_ref):
    def clear(r, carry):
        src_ref[r] = 0
        return carry

    def put(tok, carry):
        src_ref[dest_ref[tok]] = tok
        return carry

    lax.fori_loop(0, src_ref.shape[0], clear, 0, unroll=8)
    lax.fori_loop(0, dest_ref.shape[0], put, 0, unroll=8)


def _invert_rows(dest, rows):
    smem = pl.BlockSpec(memory_space=pltpu.SMEM)
    return pl.pallas_call(
        _invert_rows_kernel, in_specs=[smem], out_specs=smem,
        out_shape=jax.ShapeDtypeStruct((rows,), jnp.int32), name="invert_rows")(dest)


def _moe_ln(x, rw_t, rb, rw_pad, w_gate, w_up, w_down, layer, g, b, tm_router, blk, tm_comb):
    t = x.shape[0]
    bucket, rank, counts = _router(x, rw_t, rb, tm_router)
    padded = (counts + blk - 1) // blk * blk
    ends = jnp.cumsum(padded)
    dest = ((ends - padded)[bucket] + rank).astype(jnp.int32)
    nblk = t // blk + N_BUCKETS
    src = _invert_rows(dest, nblk * blk)
    blk_bucket = jnp.minimum(jnp.searchsorted(ends, jnp.arange(nblk) * blk, side='right'), N_BUCKETS - 1)
    pair_lo = jnp.array([p[0] for p in _PAIRS], jnp.int32)
    pair_hi = jnp.array([p[1] for p in _PAIRS], jnp.int32)
    grp, pr = blk_bucket // len(_PAIRS), blk_bucket % len(_PAIRS)
    blk_lo = (grp * EXPERTS_PER_GROUP + pair_lo[pr]).astype(jnp.int32)
    blk_hi = (grp * EXPERTS_PER_GROUP + pair_hi[pr]).astype(jnp.int32)
    n_used = (ends[-1:] // blk).astype(jnp.int32)
    y_rows = _ffn(x, src, blk_lo, blk_hi, n_used, rw_pad, w_gate, w_up, w_down, layer, blk)
    return _combine_ln(x, y_rows, dest, g, b, tm_comb)


def _moe_dense_kernel(x_ref, lo_ref, hi_ref, rw_ref, wg_ref, wu_ref, wd_ref, g_ref, b_ref, o_ref, acc_ref):
    e = pl.program_id(0)

    @pl.when(e == 0)
    def _():
        acc_ref[...] = jnp.zeros_like(acc_ref)

    x = x_ref[...]
    xb = x.astype(BF16)
    logits = _dot(xb, rw_ref[...])
    lane = lax.broadcasted_iota(jnp.int32, logits.shape, 1)
    lo, hi = lo_ref[...], hi_ref[...]
    l_lo = jnp.sum(jnp.where(lane == lo, logits, 0.0), axis=-1, keepdims=True)
    l_hi = jnp.sum(jnp.where(lane == hi, logits, 0.0), axis=-1, keepdims=True)
    w_lo = _sigmoid(l_lo - l_hi)
    coef = jnp.where(lo == e, w_lo, 0.0) + jnp.where(hi == e, 1.0 - w_lo, 0.0)
    gate = _dot(xb, wg_ref[0])
    act = gate * _sigmoid(gate) * _dot(xb, wu_ref[0])
    acc_ref[...] += coef * _dot(act.astype(BF16), wd_ref[0])

    @pl.when(e == pl.num_programs(0) - 1)
    def _():
        o_ref[...] = _ln(ALPHA * x + acc_ref[...], g_ref[...], b_ref[...])


def _moe_ln_dense(x, rw_t, rb, rw_pad, w_gate, w_up, w_down, layer, g, b):
    t = x.shape[0]
    bucket, _, _ = _router(x, rw_t, rb, t)
    pair_lo = jnp.array([p[0] for p in _PAIRS], jnp.int32)
    pair_hi = jnp.array([p[1] for p in _PAIRS], jnp.int32)
    grp, pr = bucket // len(_PAIRS), bucket % len(_PAIRS)
    lo = (grp * EXPERTS_PER_GROUP + pair_lo[pr]).astype(jnp.int32).reshape(t, 1)
    hi = (grp * EXPERTS_PER_GROUP + pair_hi[pr]).astype(jnp.int32).reshape(t, 1)
    wg = pl.BlockSpec((None, 1, D, EXPERT_FF), lambda e: (layer, e, 0, 0))
    wd = pl.BlockSpec((None, 1, EXPERT_FF, D), lambda e: (layer, e, 0, 0))
    return pl.pallas_call(
        _moe_dense_kernel, grid=(N_EXPERTS,),
        in_specs=[_full((t, D)), _full((t, 1)), _full((t, 1)), _full(rw_pad.shape), wg, wg, wd,
                  _full((1, D)), _full((1, D))],
        out_specs=_full((t, D)), out_shape=jax.ShapeDtypeStruct((t, D), F32),
        scratch_shapes=[pltpu.VMEM((t, D), F32)],
        compiler_params=_cparams(("arbitrary",)), name="moe_dense")(x, lo, hi, rw_pad, w_gate, w_up, w_down, g, b)


def kernel(x_prompt, x_sample, cache_swa_k, cache_swa_v, state_lru_conv, state_lru_h, state_rwkv_shift, state_rwkv_wkv, cache_mem_k, cache_mem_v, mem_prompt, swa_w_qkv, swa_sinks, swa_w_o, lru_w_in, lru_b_in, lru_conv_w, lru_conv_b, lru_w_a, lru_b_a, lru_w_i, lru_b_i, lru_lambda, lru_w_o, rwkv_mu, rwkv_w_r, rwkv_w_k, rwkv_w_v, rwkv_w0, rwkv_w1, rwkv_w2, rwkv_a0, rwkv_a1, rwkv_a2, rwkv_g1, rwkv_g2, rwkv_k_k, rwkv_k_a, rwkv_r_k, rwkv_gn_g, rwkv_gn_b, rwkv_w_o, mem_w_q, mem_w_kv, mem_w_o, ln_g, ln_b, router_w, router_b, moe_w_gate, moe_w_up, moe_w_down):
    n_p, seq, _ = x_prompt.shape
    n_s, dec_seq, _ = x_sample.shape
    assert dec_seq == 1
    past_len = 8192
    xp = x_prompt.reshape(n_p * seq, D)
    xs = x_sample.reshape(n_s, D)
    row = lambda v: v.reshape(1, -1)
    bf = lambda w: w.astype(BF16)

    rw_t = bf(router_w.T)
    rb = router_b.reshape(N_EXPERTS, 1)
    rw_pad = bf(jnp.pad(router_w, ((0, 0), (0, LANES - N_EXPERTS))))
    wg, wu, wd = bf(moe_w_gate), bf(moe_w_up), bf(moe_w_down)
    mem_p = mem_prompt.reshape(n_p * mem_prompt.shape[1], D)
    m_len = mem_prompt.shape[1]

    swa_k_p, swa_v_p, swa_k_s, swa_v_s = [], [], [], []
    lru_c_p, lru_h_p, lru_c_s, lru_h_s = [], [], [], []
    rw_x_p, rw_s_p, rw_x_s, rw_s_s = [], [], [], []
    mem_k_p, mem_v_p = [], []

    for layer in range(DEPTH):
        kind, i = layer % N_MIXERS, layer // N_MIXERS
        g0, b0 = row(ln_g[layer, 0]), row(ln_b[layer, 0])
        if kind == 0:
            w_qkv, w_o = bf(swa_w_qkv[i]), bf(swa_w_o[i])
            keep = min(WINDOW, seq)
            q, k, v, kv_last = _swa_qkv(xp, w_qkv, jnp.arange(seq), n_p, 512, keep, BF16)
            o = _swa_attn_prompt(q, k, v, swa_sinks[i], n_p, 2)
            swa_k_p.append(kv_last[:, :, :KV_WIDTH].reshape(n_p, keep, SWA_KV_HEADS, HEAD_DIM))
            swa_v_p.append(kv_last[:, :, KV_WIDTH:].reshape(n_p, keep, SWA_KV_HEADS, HEAD_DIM))
            xp = _proj_ln(o, w_o, xp, g0, b0, 512)

            qs, _, _, kv_new = _swa_qkv(xs, w_qkv, jnp.full((n_s,), past_len), 1, n_s, n_s, F32)
            kn, vn = kv_new[0, :, :KV_WIDTH], kv_new[0, :, KV_WIDTH:]
            os_ = _swa_attn_sample(qs, kn, vn, cache_swa_k, cache_swa_v, i, swa_sinks[i], 8)
            wb = cache_swa_k.shape[2]
            k_all = jnp.concatenate([cache_swa_k[i], kn.reshape(n_s, 1, SWA_KV_HEADS, HEAD_DIM)], axis=1)
            v_all = jnp.concatenate([cache_swa_v[i], vn.reshape(n_s, 1, SWA_KV_HEADS, HEAD_DIM)], axis=1)
            swa_k_s.append(k_all[:, -wb:])
            swa_v_s.append(v_all[:, -wb:])
            xs = _proj_ln(os_, w_o, xs, g0, b0, n_s)
        elif kind == 1:
            wts = _lru_weights(lru_w_in[i], lru_b_in[i], lru_conv_w[i], lru_conv_b[i], lru_w_a[i], lru_b_a[i],
                               lru_w_i[i], lru_b_i[i], lru_lambda[i], lru_w_o[i])
            xp, conv_last, h_last = _lru_prompt(xp, wts, g0, b0, n_p, 256)
            lru_c_p.append(conv_last[:, SUBLANES - (CONV_W - 1):])
            lru_h_p.append(h_last[:, SUBLANES - 1])
            xs, xb_s, h_s = _lru_sample(xs, state_lru_conv[i], state_lru_h[i], wts, g0, b0)
            lru_c_s.append(jnp.concatenate([state_lru_conv[i][:, 1:], xb_s[:, None]], axis=1))
            lru_h_s.append(h_s)
        else:
            wts = (rwkv_mu[i], bf(rwkv_w_r[i]), bf(rwkv_w_k[i]), bf(rwkv_w_v[i]), row(rwkv_w0[i]), bf(rwkv_w1[i]),
                   bf(rwkv_w2[i]), row(rwkv_a0[i]), bf(rwkv_a1[i]), bf(rwkv_a2[i]), bf(rwkv_g1[i]), bf(rwkv_g2[i]))
            hp = (row(rwkv_k_k[i]), row(rwkv_k_a[i]), row(rwkv_r_k[i]), row(rwkv_gn_g[i]), row(rwkv_gn_b[i]))
            w_o = bf(rwkv_w_o[i])
            rw_x_p.append(xp.reshape(n_p, seq, D)[:, -1])
            rw_x_s.append(xs)
            r, k, v, a, ld, g = _rwkv_pre(xp, jnp.zeros((n_p, SUBLANES, D), F32), wts, n_p, 256, True, BF16)
            o, st = _wkv_prompt(r, k, v, a, ld, g, hp, n_p, 2)
            hd = RWKV_HD
            st = jnp.stack([st[:, :, :hd, :hd], st[:, :, hd:, hd:]], axis=2).reshape(n_p, RWKV_HEADS, hd, hd)
            rw_s_p.append(jnp.swapaxes(st, -1, -2))
            xp = _proj_ln(o, w_o, xp, g0, b0, 512)

            r, k, v, a, ld, g = _rwkv_pre(xs, state_rwkv_shift[i], wts, 1, n_s, False, F32)
            os_, s_new = _wkv_sample(r, k, v, a, ld, g, state_rwkv_wkv[i], hp)
            rw_s_s.append(s_new)
            xs = _proj_ln(os_, w_o, xs, g0, b0, n_s)

        g1, b1 = row(ln_g[layer, 1]), row(ln_b[layer, 1])
        w_q, w_o = bf(mem_w_q[layer]), bf(mem_w_o[layer])
        mkv = _matmul(mem_p, bf(mem_w_kv[layer]), 512)
        mk, mv = mkv[:, :D], mkv[:, D:]
        mem_k_p.append(mk.reshape(n_p, m_len, MEM_HEADS, MEM_HD))
        mem_v_p.append(mv.reshape(n_p, m_len, MEM_HEADS, MEM_HD))
        xp = _mem_attn_prompt(xp, w_q, bf(mk).reshape(n_p, m_len, D), bf(mv).reshape(n_p, m_len, D), w_o, g1, b1,
                              n_p, 512)
        qs = _matmul(xs, w_q, n_s)
        os_ = _mem_attn_sample(qs, cache_mem_k, cache_mem_v, layer, 4)
        xs = _proj_ln(os_, w_o, xs, g1, b1, n_s)

        g2, b2 = row(ln_g[layer, 2]), row(ln_b[layer, 2])
        xp = _moe_ln(xp, rw_t, rb, rw_pad, wg, wu, wd, layer, g2, b2, 512, 256, 256)
        xs = _moe_ln_dense(xs, rw_t, rb, rw_pad, wg, wu, wd, layer, g2, b2)

    return (xp.reshape(n_p, seq, D), xs.reshape(n_s, 1, D),
            jnp.stack(swa_k_p), jnp.stack(swa_v_p), jnp.stack(lru_c_p), jnp.stack(lru_h_p),
            jnp.stack(rw_x_p), jnp.stack(rw_s_p), jnp.stack(mem_k_p), jnp.stack(mem_v_p),
            jnp.stack(swa_k_s), jnp.stack(swa_v_s), jnp.stack(lru_c_s), jnp.stack(lru_h_s),
            jnp.stack(rw_x_s), jnp.stack(rw_s_s))
```

```python
import functools

import jax
import jax.numpy as jnp
from jax import lax
from jax.experimental import pallas as pl
from jax.experimental.pallas import tpu as pltpu

F32 = jnp.float32
BF16 = jnp.bfloat16

D = 1024
DEPTH = 4
N_MIXERS = 3
HEAD_DIM = 64
SWA_HEADS = D // HEAD_DIM
SWA_KV_HEADS = 4
SWA_GROUP = SWA_HEADS // SWA_KV_HEADS
Q_WIDTH = SWA_HEADS * HEAD_DIM
KV_WIDTH = SWA_KV_HEADS * HEAD_DIM
WINDOW = 128
ROT_DIM = HEAD_DIM // 4
ROPE_THETA = 500000.0
LRU_BLOCKS = 16
CONV_W = 4
LRU_C = 8.0
RWKV_HEADS = 16
RWKV_HD = 64
RWKV_GN_EPS = 64e-5
MEM_HEADS = 4
MEM_HD = D // MEM_HEADS
N_EXPERTS = 16
N_GROUPS = 4
EXPERTS_PER_GROUP = 4
EXPERT_FF = 512
LN_EPS = 1e-5
ALPHA = (2.0 * DEPTH) ** 0.25
NEG_INF = -1e30

LANES = 128
SUBLANES = 8
VMEM_LIMIT = 56 * 1024 * 1024
WKV_CHUNK = 64
N_BUCKETS = N_GROUPS * 6
BUCKET_ROWS = 32


def _cparams(sem):
    return pltpu.CompilerParams(dimension_semantics=sem, vmem_limit_bytes=VMEM_LIMIT)


def _dot(a, b):
    return jnp.dot(a, b, preferred_element_type=F32)


def _dot_nt(a, b):
    return lax.dot_general(a, b, (((1,), (1,)), ((), ())), preferred_element_type=F32)


def _dot_tn(a, b):
    return lax.dot_general(a, b, (((0,), (0,)), ((), ())), preferred_element_type=F32)


def _ln(z, g, b):
    mu = jnp.mean(z, axis=-1, keepdims=True)
    zc = z - mu
    var = jnp.mean(zc * zc, axis=-1, keepdims=True)
    return zc * lax.rsqrt(var + LN_EPS) * g + b


def _softplus(z):
    return jnp.maximum(z, 0.0) + jnp.log1p(jnp.exp(-jnp.abs(z)))


def _sigmoid(z):
    return 1.0 / (1.0 + jnp.exp(-z))


def _round_bf16(x):
    return x.astype(BF16).astype(F32)


def _full(shape):
    nd = len(shape)
    return pl.BlockSpec(shape, lambda *_: (0,) * nd)


def _mm_kernel(a_ref, w_ref, o_ref):
    o_ref[...] = _dot(a_ref[...].astype(BF16), w_ref[...]).astype(o_ref.dtype)


def _matmul(a, w, tm, out_dtype=F32):
    t, k = a.shape
    n = w.shape[1]
    return pl.pallas_call(
        _mm_kernel, grid=(t // tm,),
        in_specs=[pl.BlockSpec((tm, k), lambda i: (i, 0)), _full((k, n))],
        out_specs=pl.BlockSpec((tm, n), lambda i: (i, 0)),
        out_shape=jax.ShapeDtypeStruct((t, n), out_dtype),
        compiler_params=_cparams(("parallel",)), name="matmul")(a, w)


def _proj_ln_kernel(a_ref, w_ref, x_ref, g_ref, b_ref, o_ref):
    acc = _dot(a_ref[...].astype(BF16), w_ref[...])
    o_ref[...] = _ln(ALPHA * x_ref[...] + acc, g_ref[...], b_ref[...])


def _proj_ln(a, w, x, g, b, tm):
    t, k = a.shape
    return pl.pallas_call(
        _proj_ln_kernel, grid=(t // tm,),
        in_specs=[pl.BlockSpec((tm, k), lambda i: (i, 0)), _full((k, D)),
                  pl.BlockSpec((tm, D), lambda i: (i, 0)), _full((1, D)), _full((1, D))],
        out_specs=pl.BlockSpec((tm, D), lambda i: (i, 0)),
        out_shape=jax.ShapeDtypeStruct((t, D), F32),
        compiler_params=_cparams(("parallel",)), name="proj_ln")(a, w, x, g, b)


def _rope_tables(pos):
    half = ROT_DIM // 2
    inv_freq = ROPE_THETA ** (-jnp.arange(half, dtype=F32) / half)
    ang = pos.astype(F32)[:, None] * inv_freq
    cos, sin = jnp.cos(ang), jnp.sin(ang)
    one = jnp.ones((pos.shape[0], HEAD_DIM - ROT_DIM), F32)
    zero = jnp.zeros((pos.shape[0], HEAD_DIM - ROT_DIM), F32)
    zh = jnp.zeros_like(sin)
    c = jnp.concatenate([cos, cos, one], axis=1)
    s1 = jnp.concatenate([-sin, zh, zero], axis=1)
    s2 = jnp.concatenate([zh, sin, zero], axis=1)
    rep = LANES // HEAD_DIM
    return jnp.tile(c, (1, rep)), jnp.tile(s1, (1, rep)), jnp.tile(s2, (1, rep))


def _swa_qkv_kernel(x_ref, w_ref, c_ref, s1_ref, s2_ref, q_ref, k_ref, v_ref, kv_ref, *, tm, keep):
    acc = _dot(x_ref[...].astype(BF16), w_ref[...])
    c, s1, s2 = c_ref[...], s1_ref[...], s2_ref[...]
    half = ROT_DIM // 2
    n_q = Q_WIDTH // LANES
    n_k = KV_WIDTH // LANES
    for cg in range(n_q + n_k):
        xg = acc[:, cg * LANES:(cg + 1) * LANES]
        rot = xg * c + pltpu.roll(xg, LANES - half, 1) * s1 + pltpu.roll(xg, half, 1) * s2
        if cg < n_q:
            q_ref[:, cg * LANES:(cg + 1) * LANES] = rot.astype(q_ref.dtype)
        else:
            ck = cg - n_q
            k_ref[:, ck * LANES:(ck + 1) * LANES] = rot.astype(k_ref.dtype)
            kv_ref[0, :, ck * LANES:(ck + 1) * LANES] = rot[tm - keep:, :]
    v = acc[:, Q_WIDTH + KV_WIDTH:]
    v_ref[...] = v.astype(v_ref.dtype)
    kv_ref[0, :, KV_WIDTH:] = v[tm - keep:, :]


def _swa_qkv(x, w_qkv, pos, n_seq, tm, keep, qdtype):
    t = x.shape[0]
    s = t // n_seq
    nb = s // tm
    c, s1, s2 = _rope_tables(pos)
    row = lambda n, i: (n * nb + i, 0)
    tab = pl.BlockSpec((tm, LANES), lambda n, i: (i, 0))
    kern = functools.partial(_swa_qkv_kernel, tm=tm, keep=keep)
    return pl.pallas_call(
        kern, grid=(n_seq, nb),
        in_specs=[pl.BlockSpec((tm, D), row), _full((D, Q_WIDTH + 2 * KV_WIDTH)), tab, tab, tab],
        out_specs=[pl.BlockSpec((tm, Q_WIDTH), row), pl.BlockSpec((tm, KV_WIDTH), row),
                   pl.BlockSpec((tm, KV_WIDTH), row),
                   pl.BlockSpec((1, keep, 2 * KV_WIDTH), lambda n, i: (n, 0, 0))],
        out_shape=[jax.ShapeDtypeStruct((t, Q_WIDTH), qdtype), jax.ShapeDtypeStruct((t, KV_WIDTH), qdtype),
                   jax.ShapeDtypeStruct((t, KV_WIDTH), qdtype),
                   jax.ShapeDtypeStruct((n_seq, keep, 2 * KV_WIDTH), F32)],
        compiler_params=_cparams(("parallel", "arbitrary")), name="swa_qkv")(x, w_qkv, c, s1, s2)


def _swa_attn_kernel(sink_ref, q_ref, kp_ref, kc_ref, vp_ref, vc_ref, o_ref, *, nq):
    j = pl.program_id(1)
    w, grp = WINDOW, SWA_GROUP
    r = lax.broadcasted_iota(jnp.int32, (grp * w, 2 * w), 0) % w
    c = lax.broadcasted_iota(jnp.int32, (grp * w, 2 * w), 1)
    in_prev = jnp.logical_and(c < w, c > r)
    in_cur = jnp.logical_and(c >= w, (c - w) <= r)
    ok_inner = jnp.logical_or(in_prev, in_cur)
    ok_first = jnp.logical_or(jnp.logical_and(in_prev, j > 0), in_cur)
    scale = HEAD_DIM ** -0.5
    combos = [(u, h) for u in range(nq) for h in range(SWA_KV_HEADS)]
    kcat, vcat, q4, sink, ok = [], [], [], [], []
    for u, h in combos:
        sl = slice(h * HEAD_DIM, (h + 1) * HEAD_DIM)
        rows = slice(u * w, (u + 1) * w)
        before = slice((u - 1) * w, u * w)
        k_prev = kp_ref[:, sl] if u == 0 else kc_ref[before, sl]
        v_prev = vp_ref[:, sl] if u == 0 else vc_ref[before, sl]
        kcat.append(jnp.concatenate([k_prev, kc_ref[rows, sl]], axis=0))
        vcat.append(jnp.concatenate([v_prev, vc_ref[rows, sl]], axis=0))
        heads = [h * grp + g for g in range(grp)]
        q4.append(jnp.concatenate([q_ref[rows, hq * HEAD_DIM:(hq + 1) * HEAD_DIM] for hq in heads], axis=0))
        sink.append(jnp.concatenate([jnp.full((w, 1), sink_ref[hq], F32) for hq in heads], axis=0))
        ok.append(ok_first if u == 0 else ok_inner)
    n = range(len(combos))
    s = [jnp.where(ok[i], _dot_nt(q4[i], kcat[i]) * scale, NEG_INF) for i in n]
    m = [jnp.maximum(jnp.max(s[i], axis=-1, keepdims=True), sink[i]) for i in n]
    p = [jnp.exp(s[i] - m[i]) for i in n]
    den = [jnp.sum(p[i], axis=-1, keepdims=True) + jnp.exp(sink[i] - m[i]) for i in n]
    o = [_dot((p[i] / den[i]).astype(BF16), vcat[i]) for i in n]
    for i, (u, h) in enumerate(combos):
        for g in range(grp):
            hq = h * grp + g
            o_ref[u * w:(u + 1) * w, hq * HEAD_DIM:(hq + 1) * HEAD_DIM] = o[i][g * w:(g + 1) * w].astype(o_ref.dtype)


def _swa_attn_prompt(q, k, v, sinks, n_seq, nq):
    t = q.shape[0]
    nb = t // n_seq // WINDOW
    ns = nb // nq
    cur = lambda n, j: (n * ns + j, 0)
    prev = lambda n, j: (n * nb + jnp.maximum(j * nq - 1, 0), 0)
    return pl.pallas_call(
        functools.partial(_swa_attn_kernel, nq=nq), grid=(n_seq, ns),
        in_specs=[pl.BlockSpec(memory_space=pltpu.SMEM), pl.BlockSpec((nq * WINDOW, Q_WIDTH), cur),
                  pl.BlockSpec((WINDOW, KV_WIDTH), prev), pl.BlockSpec((nq * WINDOW, KV_WIDTH), cur),
                  pl.BlockSpec((WINDOW, KV_WIDTH), prev), pl.BlockSpec((nq * WINDOW, KV_WIDTH), cur)],
        out_specs=pl.BlockSpec((nq * WINDOW, Q_WIDTH), cur),
        out_shape=jax.ShapeDtypeStruct((t, Q_WIDTH), BF16),
        compiler_params=_cparams(("parallel", "arbitrary")), name="swa_attn")(sinks, q, k, k, v, v)


def _swa_sample_kernel(sink_ref, qbd_ref, q_ref, kn_ref, vnbd_ref, ckt_ref, cvt_ref, o_ref, *, bs):
    nkv, hd, wb = ckt_ref.shape[1:]
    nq = q_ref.shape[1]
    npad = kn_ref.shape[1]
    key = lax.broadcasted_iota(jnp.int32, (nq, wb), 1)
    valid = (wb - key) < WINDOW
    own_new = (lax.broadcasted_iota(jnp.int32, (nq, npad), 1)
               == lax.broadcasted_iota(jnp.int32, (nq, npad), 0) // SWA_GROUP)
    sink = sink_ref[...]
    scale = HEAD_DIM ** -0.5
    nb = range(bs)
    s = [jnp.where(valid, _dot(qbd_ref[b].astype(BF16), ckt_ref[b].reshape(nkv * hd, wb).astype(BF16)) * scale, NEG_INF)
         for b in nb]
    sn = [jnp.where(own_new, _dot_nt(q_ref[b].astype(BF16), kn_ref[b].astype(BF16)) * scale, NEG_INF) for b in nb]
    m = [jnp.maximum(jnp.maximum(jnp.max(s[b], axis=-1, keepdims=True), jnp.max(sn[b], axis=-1, keepdims=True)), sink)
         for b in nb]
    p = [jnp.where(valid, jnp.exp(s[b] - m[b]), 0.0) for b in nb]
    pn = [jnp.where(own_new, jnp.exp(sn[b] - m[b]), 0.0) for b in nb]
    den = [jnp.sum(p[b], axis=-1, keepdims=True) + jnp.sum(pn[b], axis=-1, keepdims=True) + jnp.exp(sink - m[b])
           for b in nb]
    for b in nb:
        o_ref[b] = (_dot_nt((p[b] / den[b]).astype(BF16), cvt_ref[b].reshape(nkv * hd, wb).astype(BF16))
                    + _dot((pn[b] / den[b]).astype(BF16), vnbd_ref[b].astype(BF16)))


def _swa_attn_sample(q, kn, vn, cache_k, cache_v, layer, sinks, bs):
    _, b, wb, nkv, hd = cache_k.shape
    grp = SWA_HEADS // nkv
    eye = jnp.eye(nkv, dtype=q.dtype)
    qbd = jnp.einsum('bhgd,hk->bhgkd', q.reshape(b, nkv, grp, hd), eye).reshape(b, SWA_HEADS, nkv * hd)
    pad = lambda z: jnp.pad(z, ((0, 0), (0, SUBLANES - nkv), (0, 0)))
    vnbd = pad(jnp.einsum('bhd,hk->bhkd', vn.reshape(b, nkv, hd), eye).reshape(b, nkv, nkv * hd))
    blk = lambda r, w: pl.BlockSpec((bs, r, w), lambda i: (i, 0, 0))
    cblk = pl.BlockSpec((None, bs, nkv, hd, wb), lambda i: (layer, i, 0, 0, 0))
    out = pl.pallas_call(
        functools.partial(_swa_sample_kernel, bs=bs), grid=(b // bs,),
        in_specs=[_full((SWA_HEADS, 1)), blk(SWA_HEADS, nkv * hd), blk(SWA_HEADS, hd), blk(SUBLANES, hd),
                  blk(SUBLANES, nkv * hd), cblk, cblk],
        out_specs=blk(SWA_HEADS, nkv * hd), out_shape=jax.ShapeDtypeStruct((b, SWA_HEADS, nkv * hd), F32),
        compiler_params=_cparams(("parallel",)), name="swa_sample")(
            sinks.reshape(SWA_HEADS, 1), qbd, q.reshape(b, SWA_HEADS, hd), pad(kn.reshape(b, nkv, hd)), vnbd,
            jnp.transpose(cache_k, (0, 1, 3, 4, 2)), jnp.transpose(cache_v, (0, 1, 3, 4, 2)))
    o5 = out.reshape(b, nkv, grp, nkv, hd)
    return jnp.stack([o5[:, h, :, h, :] for h in range(nkv)], axis=1).reshape(b, Q_WIDTH)


def _gelu_tanh(x):
    return 0.5 * x * (1.0 + jnp.tanh(0.7978845608028654 * (x + 0.044715 * x * x * x)))


def _lru_gates(xc, wa_ref, ba, wi_ref, bi, lam):
    xcb = xc.astype(BF16)
    gw = wa_ref.shape[1]
    ra, ia = [], []
    for gi in range(wa_ref.shape[0]):
        xs = xcb[:, gi * gw:(gi + 1) * gw]
        ra.append(_dot(xs, wa_ref[gi]))
        ia.append(_dot(xs, wi_ref[gi]))
    r = _sigmoid(jnp.concatenate(ra, axis=-1) + ba)
    ig = _sigmoid(jnp.concatenate(ia, axis=-1) + bi)
    log_a = -LRU_C * r * _softplus(-lam)
    a = jnp.exp(log_a)
    b = jnp.sqrt(-jnp.tanh(log_a) * (a * a + 1.0)) * (ig * xc)
    return a, b


def _shift_rows(ext, s, tm):
    return pltpu.roll(ext, s, 0)[SUBLANES:SUBLANES + tm]


def _lru_prompt_kernel(x_ref, win_ref, bin_ref, cw_ref, cb_ref, wa_ref, ba_ref, wi_ref, bi_ref, lam_ref,
                       wo_ref, g_ref, b_ref, o_ref, conv_ref, hl_ref, cx_ref, ch_ref, *, tm):
    i = pl.program_id(1)

    @pl.when(i == 0)
    def _():
        cx_ref[...] = jnp.zeros_like(cx_ref)
        ch_ref[...] = jnp.zeros_like(ch_ref)

    x = x_ref[...]
    xy = _dot(x.astype(BF16), win_ref[...]) + bin_ref[...]
    xb = xy[:, :D]
    y_gate = _gelu_tanh(xy[:, D:])
    ext = jnp.concatenate([cx_ref[...], xb], axis=0)
    cw = cw_ref[...]
    xc = cb_ref[...] + xb * cw[CONV_W - 1:CONV_W]
    for s in range(1, CONV_W):
        xc = xc + _shift_rows(ext, s, tm) * cw[CONV_W - 1 - s:CONV_W - s]
    cx_ref[...] = xb[tm - SUBLANES:]
    conv_ref[0] = xb[tm - SUBLANES:]

    a, b = _lru_gates(xc, wa_ref, ba_ref[...], wi_ref, bi_ref[...], lam_ref[...])
    sub = lax.broadcasted_iota(jnp.int32, (tm, 1), 0) % SUBLANES
    s = 1
    while s < SUBLANES:
        keep = sub >= s
        a_sh = jnp.where(keep, pltpu.roll(a, s, 0), 1.0)
        b_sh = jnp.where(keep, pltpu.roll(b, s, 0), 0.0)
        b = a * b_sh + b
        a = a * a_sh
        s *= 2
    carry = ch_ref[SUBLANES - 1:SUBLANES, :]
    groups = []
    for gi in range(tm // SUBLANES):
        rows = slice(gi * SUBLANES, (gi + 1) * SUBLANES)
        hg = a[rows] * carry + b[rows]
        groups.append(hg)
        carry = hg[SUBLANES - 1:SUBLANES]
    h = jnp.concatenate(groups, axis=0)
    ch_ref[...] = h[tm - SUBLANES:]
    hl_ref[0] = h[tm - SUBLANES:]
    acc = _dot((h * y_gate).astype(BF16), wo_ref[...])
    o_ref[...] = _ln(ALPHA * x + acc, g_ref[...], b_ref[...])


def _lru_weights(w_in, b_in, conv_w, conv_b, w_a, b_a, w_i, b_i, lam, w_o):
    gsz = 4
    ng = LRU_BLOCKS // gsz
    bw = D // LRU_BLOCKS

    def grouped(w):
        w4 = w.reshape(ng, gsz, bw, bw)
        return jnp.einsum('gaij,ab->gaibj', w4, jnp.eye(gsz, dtype=w.dtype)).reshape(ng, gsz * bw, gsz * bw).astype(BF16)

    row = lambda v: v.reshape(1, -1)
    return (w_in.astype(BF16), row(b_in), conv_w, row(conv_b), grouped(w_a), row(b_a), grouped(w_i), row(b_i),
            row(lam), w_o.astype(BF16))


def _lru_prompt(x, wts, g, b, n_seq, tm):
    t = x.shape[0]
    nb = t // n_seq // tm
    row = lambda n, i: (n * nb + i, 0)
    last = pl.BlockSpec((1, SUBLANES, D), lambda n, i: (n, 0, 0))
    w_in, b_in, cw, cb, wa, ba, wi, bi, lam, wo = wts
    return pl.pallas_call(
        functools.partial(_lru_prompt_kernel, tm=tm), grid=(n_seq, nb),
        in_specs=[pl.BlockSpec((tm, D), row), _full(w_in.shape), _full(b_in.shape), _full(cw.shape), _full(cb.shape),
                  _full(wa.shape), _full(ba.shape), _full(wi.shape), _full(bi.shape), _full(lam.shape),
                  _full(wo.shape), _full((1, D)), _full((1, D))],
        out_specs=[pl.BlockSpec((tm, D), row), last, last],
        out_shape=[jax.ShapeDtypeStruct((t, D), F32), jax.ShapeDtypeStruct((n_seq, SUBLANES, D), F32),
                   jax.ShapeDtypeStruct((n_seq, SUBLANES, D), F32)],
        scratch_shapes=[pltpu.VMEM((SUBLANES, D), F32), pltpu.VMEM((SUBLANES, D), F32)],
        compiler_params=_cparams(("parallel", "arbitrary")), name="lru_prompt")(x, *wts, g, b)


def _lru_sample_kernel(x_ref, c0_ref, c1_ref, c2_ref, h0_ref, win_ref, bin_ref, cw_ref, cb_ref, wa_ref, ba_ref,
                       wi_ref, bi_ref, lam_ref, wo_ref, g_ref, b_ref, o_ref, xb_ref, h_ref):
    x = x_ref[...]
    xy = _dot(x.astype(BF16), win_ref[...]) + bin_ref[...]
    xb = xy[:, :D]
    y_gate = _gelu_tanh(xy[:, D:])
    cw = cw_ref[...]
    xc = (cb_ref[...] + c0_ref[...] * cw[0:1] + c1_ref[...] * cw[1:2] + c2_ref[...] * cw[2:3] + xb * cw[3:4])
    a, b = _lru_gates(xc, wa_ref, ba_ref[...], wi_ref, bi_ref[...], lam_ref[...])
    h = a * h0_ref[...] + b
    xb_ref[...] = xb
    h_ref[...] = h
    acc = _dot((h * y_gate).astype(BF16), wo_ref[...])
    o_ref[...] = _ln(ALPHA * x + acc, g_ref[...], b_ref[...])


def _lru_sample(x, conv_state, h0, wts, g, b):
    t = x.shape[0]
    args = (x, conv_state[:, 0], conv_state[:, 1], conv_state[:, 2], h0, *wts, g, b)
    sd = jax.ShapeDtypeStruct((t, D), F32)
    return pl.pallas_call(
        _lru_sample_kernel, grid=(1,),
        in_specs=[_full(a.shape) for a in args],
        out_specs=[_full((t, D))] * 3, out_shape=[sd, sd, sd],
        compiler_params=_cparams(("arbitrary",)), name="lru_sample")(*args)


def _rwkv_pre_kernel(x_ref, xp_ref, mu_ref, wr_ref, wk_ref, wv_ref, w0_ref, w1_ref, w2_ref, a0_ref, a1_ref, a2_ref,
                     g1_ref, g2_ref, r_ref, k_ref, v_ref, a_ref, ld_ref, g_ref, *scratch, tm, seq):
    x = x_ref[...]
    if seq:
        cx_ref, = scratch
        i = pl.program_id(1)

        @pl.when(i == 0)
        def _():
            cx_ref[...] = xp_ref[0]

        x_prev = _shift_rows(jnp.concatenate([cx_ref[...], x], axis=0), 1, tm)
        cx_ref[...] = x[tm - SUBLANES:]
    else:
        x_prev = xp_ref[...]
    xx = x_prev - x
    mu = mu_ref[...]
    mix = lambda j: (x + xx * mu[j:j + 1]).astype(BF16)
    r_ref[...] = _dot(mix(0), wr_ref[...]).astype(r_ref.dtype)
    wl = _dot(jnp.tanh(_dot(mix(1), w1_ref[...])).astype(BF16), w2_ref[...])
    w = -_softplus(-(w0_ref[...] + wl)) - 0.5
    ld_ref[...] = -jnp.exp(w)
    k_ref[...] = _dot(mix(2), wk_ref[...]).astype(k_ref.dtype)
    v_ref[...] = _dot(mix(3), wv_ref[...]).astype(v_ref.dtype)
    al = _dot(_dot(mix(4), a1_ref[...]).astype(BF16), a2_ref[...])
    a_ref[...] = _sigmoid(a0_ref[...] + al).astype(a_ref.dtype)
    g_ref[...] = _dot(_sigmoid(_dot(mix(5), g1_ref[...])).astype(BF16), g2_ref[...]).astype(g_ref.dtype)


def _rwkv_pre(x, x_prev, wts, n_seq, tm, seq, dtype):
    t = x.shape[0]
    nb = t // n_seq // tm
    row = lambda n, i: (n * nb + i, 0)
    xp_spec = pl.BlockSpec((1, SUBLANES, D), lambda n, i: (n, 0, 0)) if seq else pl.BlockSpec((tm, D), row)
    sd = lambda dt: jax.ShapeDtypeStruct((t, D), dt)
    blk = pl.BlockSpec((tm, D), row)
    return pl.pallas_call(
        functools.partial(_rwkv_pre_kernel, tm=tm, seq=seq), grid=(n_seq, nb),
        in_specs=[blk, xp_spec] + [_full(w.shape) for w in wts],
        out_specs=[blk] * 6,
        out_shape=[sd(dtype), sd(dtype), sd(dtype), sd(dtype), sd(F32), sd(dtype)],
        scratch_shapes=[pltpu.VMEM((SUBLANES, D), F32)] if seq else [],
        compiler_params=_cparams(("parallel", "arbitrary")), name="rwkv_pre")(x, x_prev, *wts)


def _seg_sum(x, first):
    s0 = jnp.sum(jnp.where(first, x, 0.0), axis=-1, keepdims=True)
    s1 = jnp.sum(jnp.where(first, 0.0, x), axis=-1, keepdims=True)
    return jnp.where(first, s0, s1)


def _wkv_kernel(r_ref, k_ref, v_ref, a_ref, ld_ref, g_ref, kk_ref, ka_ref, rk_ref, gg_ref, gb_ref,
                o_ref, s_ref, st_ref):
    c = pl.program_id(1)
    L = WKV_CHUNK
    P2 = 2 * L
    nch = r_ref.shape[0] // L

    @pl.when(c == 0)
    def _():
        st_ref[...] = jnp.zeros_like(st_ref)

    ld_all = ld_ref[...]
    tri = (lax.broadcasted_iota(jnp.int32, (L, L), 0) >= lax.broadcasted_iota(jnp.int32, (L, L), 1)).astype(BF16)
    hi = ld_all.astype(BF16)
    r1 = ld_all - hi.astype(F32)
    mid = r1.astype(BF16)
    lo = (r1 - mid.astype(F32)).astype(BF16)
    chunk_rows = [slice(ci * L, (ci + 1) * L) for ci in range(nch)]
    cum_ch = [_dot(tri, hi[rw]) + _dot(tri, mid[rw]) + _dot(tri, lo[rw]) for rw in chunk_rows]

    lane = lax.broadcasted_iota(jnp.int32, (1, LANES), 1)
    first = lane < RWKV_HD
    ri = lax.broadcasted_iota(jnp.int32, (P2, P2), 0)
    ci_ = lax.broadcasted_iota(jnp.int32, (P2, P2), 1)
    same_head = (ri // L) == (ci_ // L)
    rt, ct = ri % L, ci_ % L
    strict = jnp.logical_and(same_head, rt > ct)
    incl = jnp.logical_and(same_head, rt >= ct)
    eye = ri == ci_

    def stack(xv):
        return jnp.concatenate([jnp.where(first, xv, 0.0), jnp.where(first, 0.0, xv)], axis=0).astype(BF16)

    npair = RWKV_HEADS // 2
    combos = [(ci, p) for ci in range(nch) for p in range(npair)]
    n = range(len(combos))
    sls = [slice(p * LANES, (p + 1) * LANES) for p in range(npair)]
    ws, us, ks, rs, ul, kl, vs, g_l, bonus = ([] for _ in range(9))
    for ci, p in combos:
        rw, sl = chunk_rows[ci], sls[p]
        rp, kp, vp, ap = (ref[rw, sl].astype(F32) for ref in (r_ref, k_ref, v_ref, a_ref))
        ldp, cum = ld_all[rw, sl], cum_ch[ci][:, sl]
        kk = kp * kk_ref[:, sl]
        kk = kk / jnp.maximum(jnp.sqrt(_seg_sum(kk * kk, first)), 1e-12)
        kmod = kp * (1.0 + (ap - 1.0) * ka_ref[:, sl])
        bp = kk * ap
        cum_l = cum[L - 1:L, :]
        g_inv = jnp.exp(-cum)
        g_to_end = jnp.exp(cum_l - cum)
        ws.append(stack(kk * jnp.exp(cum - ldp)))
        us.append(stack(bp * g_inv))
        ks.append(stack(kmod * g_inv))
        rs.append(stack(rp * jnp.exp(cum)))
        ul.append(stack(bp * g_to_end))
        kl.append(stack(kmod * g_to_end))
        vs.append(stack(vp))
        g_l.append(jnp.exp(cum_l))
        bonus.append(_seg_sum(rp * kmod * rk_ref[:, sl], first) * vp)

    gram = [_dot_nt(jnp.concatenate([ws[q], rs[q]], axis=0), jnp.concatenate([us[q], ks[q]], axis=0)) for q in n]
    n_mat = [jnp.where(strict, gram[q][:P2, :P2], 0.0) for q in n]
    m_mat = [jnp.where(strict, gram[q][:P2, P2:], 0.0).astype(BF16) for q in n]
    nr_mat = [jnp.where(incl, gram[q][P2:, :P2], 0.0).astype(BF16) for q in n]
    mr_mat = [jnp.where(incl, gram[q][P2:, P2:], 0.0).astype(BF16) for q in n]

    def level_mask(sz):
        sub = jnp.logical_and((rt // sz) % 2 == 1, (ct // sz) % 2 == 0)
        return jnp.logical_and(jnp.logical_and(sub, (rt // (2 * sz)) == (ct // (2 * sz))), same_head)

    x_inv = [jnp.where(eye, 1.0, 0.0) - jnp.where(level_mask(1), n_mat[q], 0.0) for q in n]
    sz = 2
    while sz < L:
        mask = level_mask(sz)
        xb = [x_inv[q].astype(BF16) for q in n]
        xc = [_dot(xb[q], jnp.where(mask, n_mat[q], 0.0).astype(BF16)).astype(BF16) for q in n]
        x_inv = [x_inv[q] - _dot(xc[q], xb[q]) for q in n]
        sz *= 2
    x_inv = [x_inv[q].astype(BF16) for q in n]

    state = [st_ref[p] for p in range(npair)]
    inv_n = 1.0 / RWKV_HD
    for ci in range(nch):
        qs = [ci * npair + p for p in range(npair)]
        a0b = [state[p].astype(BF16) for p in range(npair)]
        rhs = [_dot(jnp.concatenate([ws[q], m_mat[q]], axis=1), jnp.concatenate([a0b[p], vs[q]], axis=0)).astype(BF16)
               for p, q in enumerate(qs)]
        pm = [(-_dot(x_inv[q], rhs[p])).astype(BF16) for p, q in enumerate(qs)]
        o_st = [_dot(jnp.concatenate([rs[q], nr_mat[q], mr_mat[q]], axis=1),
                     jnp.concatenate([a0b[p], pm[p], vs[q]], axis=0)) for p, q in enumerate(qs)]
        new_state = []
        for p, q in enumerate(qs):
            g_col = jnp.sum(jnp.where(eye, jnp.broadcast_to(g_l[q], (P2, P2)), 0.0), axis=-1, keepdims=True)
            new_state.append(g_col * state[p] + _dot_tn(jnp.concatenate([ul[q], kl[q]], axis=0),
                                                        jnp.concatenate([pm[p], vs[q]], axis=0)))
        state = new_state
        for p, q in enumerate(qs):
            sl = sls[p]
            o = o_st[p][:L] + o_st[p][L:]
            mu = _seg_sum(o, first) * inv_n
            oc = o - mu
            var = _seg_sum(oc * oc, first) * inv_n
            on = oc * lax.rsqrt(var + RWKV_GN_EPS) * gg_ref[:, sl] + gb_ref[:, sl]
            o_ref[chunk_rows[ci], sl] = ((on + bonus[q]) * g_ref[chunk_rows[ci], sl].astype(F32)).astype(o_ref.dtype)

    for p in range(npair):
        st_ref[p] = state[p]
    s_ref[0] = st_ref[...]


def _wkv_prompt(r, k, v, a, ld, g, hp, n_seq, nch):
    t = r.shape[0]
    L = WKV_CHUNK * nch
    nc = t // n_seq // L
    row = lambda n, c: (n * nc + c, 0)
    blk = pl.BlockSpec((L, D), row)
    npair = RWKV_HEADS // 2
    return pl.pallas_call(
        _wkv_kernel, grid=(n_seq, nc),
        in_specs=[blk] * 6 + [_full((1, D))] * 5,
        out_specs=[blk, pl.BlockSpec((1, npair, LANES, LANES), lambda n, c: (n, 0, 0, 0))],
        out_shape=[jax.ShapeDtypeStruct((t, D), BF16), jax.ShapeDtypeStruct((n_seq, npair, LANES, LANES), F32)],
        scratch_shapes=[pltpu.VMEM((npair, LANES, LANES), F32)],
        compiler_params=_cparams(("parallel", "arbitrary")), name="wkv_chunk")(r, k, v, a, ld, g, *hp)


def _wkv_sample_kernel(r_ref, k_ref, v_ref, a_ref, ld_ref, g_ref, s_ref, kk_ref, ka_ref, rk_ref, gg_ref, gb_ref,
                       o_ref, so_ref):
    r, k, v, a, ld, g = (ref[0] for ref in (r_ref, k_ref, v_ref, a_ref, ld_ref, g_ref))
    kk = k * kk_ref[0]
    kk = kk / jnp.maximum(jnp.sqrt(jnp.sum(kk * kk, axis=0, keepdims=True)), 1e-12)
    kmod = k * (1.0 + (a - 1.0) * ka_ref[0])
    akk = kk * a
    decay = jnp.exp(ld)

    def value_row(vi, carry):
        s = s_ref[0, vi]
        skk = jnp.sum(s * kk, axis=0, keepdims=True)
        s_new = s * decay - skk * akk + v_ref[0, pl.ds(vi, 1), :] * kmod
        so_ref[0, vi] = s_new
        o_ref[0, pl.ds(vi, 1), :] = jnp.sum(s_new * r, axis=0, keepdims=True)
        return carry

    lax.fori_loop(0, s_ref.shape[1], value_row, 0, unroll=4)
    o = o_ref[0]
    mu = jnp.mean(o, axis=0, keepdims=True)
    oc = o - mu
    var = jnp.mean(oc * oc, axis=0, keepdims=True)
    on = oc * lax.rsqrt(var + RWKV_GN_EPS) * gg_ref[0] + gb_ref[0]
    bonus = jnp.sum(r * kmod * rk_ref[0], axis=0, keepdims=True) * v
    o_ref[0] = (on + bonus) * g


def _wkv_sample(r, k, v, a, ld, g, state, hp):
    b = r.shape[0]
    nh, hd = RWKV_HEADS, RWKV_HD
    t3 = lambda z: jnp.transpose(z.reshape(b, nh, hd), (1, 2, 0))
    vec = pl.BlockSpec((1, hd, b), lambda h: (h, 0, 0))
    par = pl.BlockSpec((1, hd, 1), lambda h: (h, 0, 0))
    sblk = pl.BlockSpec((1, hd, hd, b), lambda h: (h, 0, 0, 0))
    o, s_new = pl.pallas_call(
        _wkv_sample_kernel, grid=(nh,),
        in_specs=[vec] * 6 + [sblk] + [par] * 5,
        out_specs=[vec, sblk],
        out_shape=[jax.ShapeDtypeStruct((nh, hd, b), F32), jax.ShapeDtypeStruct((nh, hd, hd, b), F32)],
        compiler_params=_cparams(("parallel",)), name="wkv_sample")(
            t3(r), t3(k), t3(v), t3(a), t3(ld), t3(g), jnp.transpose(state, (1, 2, 3, 0)),
            *[z.reshape(nh, hd, 1) for z in hp])
    return jnp.transpose(o, (2, 0, 1)).reshape(b, D), jnp.transpose(s_new, (3, 0, 1, 2))


def _mem_prompt_kernel(x_ref, wq_ref, mk_ref, mv_ref, wo_ref, g_ref, b_ref, o_ref):
    x = x_ref[...]
    q = _dot(x.astype(BF16), wq_ref[...]).astype(BF16)
    scale = MEM_HD ** -0.5
    sls = [slice(h * MEM_HD, (h + 1) * MEM_HD) for h in range(MEM_HEADS)]
    s = [_dot_nt(q[:, sl], mk_ref[0, :, sl]) * scale for sl in sls]
    p = [jnp.exp(sh - jnp.max(sh, axis=-1, keepdims=True)) for sh in s]
    den = [jnp.sum(ph, axis=-1, keepdims=True) for ph in p]
    outs = [_dot((ph / dh).astype(BF16), mv_ref[0, :, sl]).astype(BF16) for ph, dh, sl in zip(p, den, sls)]
    acc = _dot(jnp.concatenate(outs, axis=-1), wo_ref[...])
    o_ref[...] = _ln(ALPHA * x + acc, g_ref[...], b_ref[...])


def _mem_attn_prompt(x, w_q, mk, mv, w_o, g, b, n_seq, tm):
    t = x.shape[0]
    nb = t // n_seq // tm
    m = mk.shape[1]
    row = lambda n, i: (n * nb + i, 0)
    mem = pl.BlockSpec((1, m, D), lambda n, i: (n, 0, 0))
    return pl.pallas_call(
        _mem_prompt_kernel, grid=(n_seq, nb),
        in_specs=[pl.BlockSpec((tm, D), row), _full((D, D)), mem, mem, _full((D, D)), _full((1, D)), _full((1, D))],
        out_specs=pl.BlockSpec((tm, D), row), out_shape=jax.ShapeDtypeStruct((t, D), F32),
        compiler_params=_cparams(("parallel", "arbitrary")), name="mem_attn")(x, w_q, mk, mv, w_o, g, b)


def _mem_sample_kernel(q_ref, ck_ref, cv_ref, o_ref, *, bs):
    m, nh, hd = ck_ref.shape[1:]
    rows = q_ref.shape[1]
    col_head = lax.broadcasted_iota(jnp.int32, (rows, m * nh), 1) % nh
    own = col_head == lax.broadcasted_iota(jnp.int32, (rows, m * nh), 0)
    scale = MEM_HD ** -0.5
    nb = range(bs)
    s = [jnp.where(own, _dot_nt(q_ref[b].astype(BF16), ck_ref[b].reshape(m * nh, hd).astype(BF16)) * scale, NEG_INF)
         for b in nb]
    p = [jnp.where(own, jnp.exp(s[b] - jnp.max(s[b], axis=-1, keepdims=True)), 0.0) for b in nb]
    den = [jnp.sum(p[b], axis=-1, keepdims=True) for b in nb]
    for b in nb:
        pb = (p[b] / jnp.where(den[b] > 0.0, den[b], 1.0)).astype(BF16)
        o_ref[b] = _dot(pb, cv_ref[b].reshape(m * nh, hd).astype(BF16))


def _mem_attn_sample(q, cache_k, cache_v, layer, bs):
    _, b, m, nh, hd = cache_k.shape
    q3 = jnp.pad(q.reshape(b, nh, hd), ((0, 0), (0, SUBLANES - nh), (0, 0)))
    qb = pl.BlockSpec((bs, SUBLANES, hd), lambda i: (i, 0, 0))
    cb = pl.BlockSpec((None, bs, m, nh, hd), lambda i: (layer, i, 0, 0, 0))
    out = pl.pallas_call(
        functools.partial(_mem_sample_kernel, bs=bs), grid=(b // bs,), in_specs=[qb, cb, cb], out_specs=qb,
        out_shape=jax.ShapeDtypeStruct((b, SUBLANES, hd), F32),
        compiler_params=_cparams(("parallel",)), name="mem_sample")(q3, cache_k, cache_v)
    return out[:, :nh].reshape(b, D)


_PAIRS = ((0, 1), (0, 2), (0, 3), (1, 2), (1, 3), (2, 3))


def _router_kernel(x_ref, rw_ref, rb_ref, bucket_ref, rank_ref, cnt_ref, base_ref, *, tm):
    i = pl.program_id(0)

    @pl.when(i == 0)
    def _():
        base_ref[...] = jnp.zeros_like(base_ref)

    logits = _dot_nt(rw_ref[...], x_ref[...].astype(BF16))
    e = jnp.exp(logits - jnp.max(logits, axis=0, keepdims=True))
    sel = e / jnp.sum(e, axis=0, keepdims=True) + rb_ref[...]
    s = [sel[j:j + 1, :] for j in range(N_EXPERTS)]
    neg = jnp.float32(-jnp.inf)

    best = jnp.zeros((1, tm), jnp.int32)
    best_score = None
    for gi in range(N_GROUPS):
        s0, s1, s2, s3 = s[4 * gi:4 * gi + 4]
        hi01, lo01, hi23, lo23 = jnp.maximum(s0, s1), jnp.minimum(s0, s1), jnp.maximum(s2, s3), jnp.minimum(s2, s3)
        score = jnp.maximum(hi01, hi23) + jnp.maximum(jnp.minimum(hi01, hi23), jnp.maximum(lo01, lo23))
        if gi == 0:
            best_score = score
        else:
            take = score > best_score
            best = jnp.where(take, gi, best)
            best_score = jnp.where(take, score, best_score)
    vals = []
    for j in range(EXPERTS_PER_GROUP):
        vj = s[j]
        for gi in range(1, N_GROUPS):
            vj = jnp.where(best == gi, s[4 * gi + j], vj)
        vals.append(vj)

    def argmax4(v):
        idx, mx = jnp.zeros((1, tm), jnp.int32), v[0]
        for j in range(1, EXPERTS_PER_GROUP):
            take = v[j] > mx
            idx = jnp.where(take, j, idx)
            mx = jnp.where(take, v[j], mx)
        return idx

    i1 = argmax4(vals)
    i2 = argmax4([jnp.where(i1 == j, neg, vals[j]) for j in range(EXPERTS_PER_GROUP)])
    lo, hi = jnp.minimum(i1, i2), jnp.maximum(i1, i2)
    pair = jnp.zeros((1, tm), jnp.int32)
    for pi, (pa, pb) in enumerate(_PAIRS):
        pair = jnp.where(jnp.logical_and(lo == pa, hi == pb), pi, pair)
    bucket = best * len(_PAIRS) + pair
    bucket_ref[0] = bucket

    onehot = (lax.broadcasted_iota(jnp.int32, (BUCKET_ROWS, tm), 0) == bucket).astype(F32)
    upper = (lax.broadcasted_iota(jnp.int32, (tm, tm), 0) <= lax.broadcasted_iota(jnp.int32, (tm, tm), 1)).astype(BF16)
    cum = _dot(onehot.astype(BF16), upper)
    base = base_ref[...]
    rank = jnp.sum(onehot * (cum + base), axis=0, keepdims=True) - 1.0
    rank_ref[0] = rank.astype(jnp.int32)
    base = base + jnp.sum(onehot, axis=1, keepdims=True)
    base_ref[...] = base
    cnt_ref[...] = jnp.broadcast_to(base, cnt_ref.shape)


def _router(x, rw_t, rb, tm):
    t = x.shape[0]
    nb = t // tm
    ib = pl.BlockSpec((1, 1, tm), lambda i: (i, 0, 0))
    bucket, rank, cnt = pl.pallas_call(
        functools.partial(_router_kernel, tm=tm), grid=(nb,),
        in_specs=[pl.BlockSpec((tm, D), lambda i: (i, 0)), _full(rw_t.shape), _full(rb.shape)],
        out_specs=[ib, ib, _full((BUCKET_ROWS, LANES))],
        out_shape=[jax.ShapeDtypeStruct((nb, 1, tm), jnp.int32), jax.ShapeDtypeStruct((nb, 1, tm), jnp.int32),
                   jax.ShapeDtypeStruct((BUCKET_ROWS, LANES), F32)],
        scratch_shapes=[pltpu.VMEM((BUCKET_ROWS, 1), F32)],
        compiler_params=_cparams(("arbitrary",)), name="router")(x, rw_t, rb)
    return bucket.reshape(t), rank.reshape(t), cnt[:N_BUCKETS, 0].astype(jnp.int32)


_ROW_GROUP = 8


def _row_copies(idx_ref, base, src_hbm, dst, sem, n, wait):
    def body(j, carry):
        for k in range(_ROW_GROUP):
            r = j * _ROW_GROUP + k
            cp = pltpu.make_async_copy(src_hbm.at[pl.ds(idx_ref[base + r], 1)], dst.at[pl.ds(r, 1)], sem)
            if wait:
                cp.wait()
            else:
                cp.start(priority=k % 2)
        return carry

    lax.fori_loop(0, n // _ROW_GROUP, body, 0)


def _ffn_kernel(src_ref, lo_ref, hi_ref, nrow_ref, used_ref, x_hbm, rw_ref, g0_ref, u0_ref, d0_ref, g1_ref, u1_ref,
                d1_ref, o_ref, xbuf, sem, *, blk):
    i = pl.program_id(0)
    used = used_ref[0]
    slot = i % 2

    @pl.when(jnp.logical_and(i == 0, used > 0))
    def _():
        xbuf[...] = jnp.zeros_like(xbuf)
        _row_copies(src_ref, 0, x_hbm, xbuf.at[0], sem.at[0], nrow_ref[0], False)

    @pl.when(i + 1 < used)
    def _():
        _row_copies(src_ref, (i + 1) * blk, x_hbm, xbuf.at[1 - slot], sem.at[1 - slot], nrow_ref[i + 1], False)

    @pl.when(i < used)
    def _():
        _row_copies(src_ref, i * blk, x_hbm, xbuf.at[slot], sem.at[slot], nrow_ref[i], True)
        xb = xbuf[slot].astype(BF16)
        logits = _dot(xb, rw_ref[...])
        lane = lax.broadcasted_iota(jnp.int32, logits.shape, 1)
        l_lo = jnp.sum(jnp.where(lane == lo_ref[i], logits, 0.0), axis=-1, keepdims=True)
        l_hi = jnp.sum(jnp.where(lane == hi_ref[i], logits, 0.0), axis=-1, keepdims=True)
        w_lo = _sigmoid(l_lo - l_hi)

        def expert(g_ref, u_ref, d_ref):
            gate = _dot(xb, g_ref[0])
            act = gate * _sigmoid(gate) * _dot(xb, u_ref[0])
            return _dot(act.astype(BF16), d_ref[0])

        y_lo = expert(g0_ref, u0_ref, d0_ref)
        y_hi = expert(g1_ref, u1_ref, d1_ref)
        o_ref[...] = w_lo * y_lo + (1.0 - w_lo) * y_hi

    @pl.when(i >= used)
    def _():
        o_ref[...] = jnp.zeros_like(o_ref)


def _ffn(x, src, blk_lo, blk_hi, blk_rows, n_used, rw, w_gate, w_up, w_down, layer, blk):
    rows = src.shape[0]
    nblk = rows // blk
    wg = lambda sel: pl.BlockSpec((None, 1, D, EXPERT_FF),
                                  lambda i, s, lo, hi, nr, used: (layer, (lo, hi)[sel][i], 0, 0))
    wd = lambda sel: pl.BlockSpec((None, 1, EXPERT_FF, D),
                                  lambda i, s, lo, hi, nr, used: (layer, (lo, hi)[sel][i], 0, 0))
    return pl.pallas_call(
        functools.partial(_ffn_kernel, blk=blk),
        grid_spec=pltpu.PrefetchScalarGridSpec(
            num_scalar_prefetch=5, grid=(nblk,),
            in_specs=[pl.BlockSpec(memory_space=pl.ANY),
                      pl.BlockSpec(rw.shape, lambda i, s, lo, hi, nr, used: (0, 0)),
                      wg(0), wg(0), wd(0), wg(1), wg(1), wd(1)],
            out_specs=pl.BlockSpec((blk, D), lambda i, s, lo, hi, nr, used: (i, 0)),
            scratch_shapes=[pltpu.VMEM((2, blk, D), F32), pltpu.SemaphoreType.DMA((2,))]),
        out_shape=jax.ShapeDtypeStruct((rows, D), F32),
        compiler_params=_cparams(("arbitrary",)), name="moe_ffn")(
            src, blk_lo, blk_hi, blk_rows, n_used, x, rw, w_gate, w_up, w_down, w_gate, w_up, w_down)


def _combine_ln_kernel(dest_ref, x_ref, y_hbm, g_ref, b_ref, o_ref, ybuf, sem, *, tm):
    i = pl.program_id(0)
    slot = i % 2

    @pl.when(i == 0)
    def _():
        _row_copies(dest_ref, 0, y_hbm, ybuf.at[0], sem.at[0], tm, False)

    @pl.when(i + 1 < pl.num_programs(0))
    def _():
        _row_copies(dest_ref, (i + 1) * tm, y_hbm, ybuf.at[1 - slot], sem.at[1 - slot], tm, False)

    _row_copies(dest_ref, i * tm, y_hbm, ybuf.at[slot], sem.at[slot], tm, True)
    o_ref[...] = _ln(ALPHA * x_ref[...] + ybuf[slot], g_ref[...], b_ref[...])


def _combine_ln(x, y_rows, dest, g, b, tm):
    t = x.shape[0]
    rowb = pl.BlockSpec((tm, D), lambda i, d: (i, 0))
    vec = pl.BlockSpec((1, D), lambda i, d: (0, 0))
    return pl.pallas_call(
        functools.partial(_combine_ln_kernel, tm=tm),
        grid_spec=pltpu.PrefetchScalarGridSpec(
            num_scalar_prefetch=1, grid=(t // tm,),
            in_specs=[rowb, pl.BlockSpec(memory_space=pl.ANY), vec, vec], out_specs=rowb,
            scratch_shapes=[pltpu.VMEM((2, tm, D), F32), pltpu.SemaphoreType.DMA((2,))]),
        out_shape=jax.ShapeDtypeStruct((t, D), F32),
        compiler_params=_cparams(("arbitrary",)), name="moe_combine_ln")(dest, x, y_rows, g, b)


def _invert_rows_kernel(dest_ref, src_ref):
    def clear(r, carry):
        src_ref[r] = 0
        return carry

    def put(tok, carry):
        src_ref[dest_ref[tok]] = tok
        return carry

    lax.fori_loop(0, src_ref.shape[0], clear, 0, unroll=8)
    lax.fori_loop(0, dest_ref.shape[0], put, 0, unroll=8)


def _invert_rows(dest, rows):
    smem = pl.BlockSpec(memory_space=pltpu.SMEM)
    return pl.pallas_call(
        _invert_rows_kernel, in_specs=[smem], out_specs=smem,
        out_shape=jax.ShapeDtypeStruct((rows,), jnp.int32), name="invert_rows")(dest)


def _moe_ln(x, rw_t, rb, rw_pad, w_gate, w_up, w_down, layer, g, b, tm_router, blk, tm_comb):
    t = x.shape[0]
    bucket, rank, counts = _router(x, rw_t, rb, tm_router)
    padded = (counts + blk - 1) // blk * blk
    ends = jnp.cumsum(padded)
    dest = ((ends - padded)[bucket] + rank).astype(jnp.int32)
    nblk = t // blk + N_BUCKETS
    src = _invert_rows(dest, nblk * blk)
    blk_bucket = jnp.minimum(jnp.searchsorted(ends, jnp.arange(nblk) * blk, side='right'), N_BUCKETS - 1)
    pair_lo = jnp.array([p[0] for p in _PAIRS], jnp.int32)
    pair_hi = jnp.array([p[1] for p in _PAIRS], jnp.int32)
    grp, pr = blk_bucket // len(_PAIRS), blk_bucket % len(_PAIRS)
    blk_lo = (grp * EXPERTS_PER_GROUP + pair_lo[pr]).astype(jnp.int32)
    blk_hi = (grp * EXPERTS_PER_GROUP + pair_hi[pr]).astype(jnp.int32)
    n_used = (ends[-1:] // blk).astype(jnp.int32)
    bucket_end = (ends - padded + counts)[blk_bucket]
    blk_rows = jnp.clip(bucket_end - jnp.arange(nblk) * blk, 0, blk)
    blk_rows = ((blk_rows + _ROW_GROUP - 1) // _ROW_GROUP * _ROW_GROUP).astype(jnp.int32)
    y_rows = _ffn(x, src, blk_lo, blk_hi, blk_rows, n_used, rw_pad, w_gate, w_up, w_down, layer, blk)
    return _combine_ln(x, y_rows, dest, g, b, tm_comb)


def _moe_dense_kernel(x_ref, lo_ref, hi_ref, rw_ref, wg_ref, wu_ref, wd_ref, g_ref, b_ref, o_ref, acc_ref):
    e = pl.program_id(0)

    @pl.when(e == 0)
    def _():
        acc_ref[...] = jnp.zeros_like(acc_ref)

    x = x_ref[...]
    xb = x.astype(BF16)
    logits = _dot(xb, rw_ref[...])
    lane = lax.broadcasted_iota(jnp.int32, logits.shape, 1)
    lo, hi = lo_ref[...], hi_ref[...]
    l_lo = jnp.sum(jnp.where(lane == lo, logits, 0.0), axis=-1, keepdims=True)
    l_hi = jnp.sum(jnp.where(lane == hi, logits, 0.0), axis=-1, keepdims=True)
    w_lo = _sigmoid(l_lo - l_hi)
    coef = jnp.where(lo == e, w_lo, 0.0) + jnp.where(hi == e, 1.0 - w_lo, 0.0)
    gate = _dot(xb, wg_ref[0])
    act = gate * _sigmoid(gate) * _dot(xb, wu_ref[0])
    acc_ref[...] += coef * _dot(act.astype(BF16), wd_ref[0])

    @pl.when(e == pl.num_programs(0) - 1)
    def _():
        o_ref[...] = _ln(ALPHA * x + acc_ref[...], g_ref[...], b_ref[...])


def _moe_ln_dense(x, rw_t, rb, rw_pad, w_gate, w_up, w_down, layer, g, b):
    t = x.shape[0]
    bucket, _, _ = _router(x, rw_t, rb, t)
    pair_lo = jnp.array([p[0] for p in _PAIRS], jnp.int32)
    pair_hi = jnp.array([p[1] for p in _PAIRS], jnp.int32)
    grp, pr = bucket // len(_PAIRS), bucket % len(_PAIRS)
    lo = (grp * EXPERTS_PER_GROUP + pair_lo[pr]).astype(jnp.int32).reshape(t, 1)
    hi = (grp * EXPERTS_PER_GROUP + pair_hi[pr]).astype(jnp.int32).reshape(t, 1)
    wg = pl.BlockSpec((None, 1, D, EXPERT_FF), lambda e: (layer, e, 0, 0))
    wd = pl.BlockSpec((None, 1, EXPERT_FF, D), lambda e: (layer, e, 0, 0))
    return pl.pallas_call(
        _moe_dense_kernel, grid=(N_EXPERTS,),
        in_specs=[_full((t, D)), _full((t, 1)), _full((t, 1)), _full(rw_pad.shape), wg, wg, wd,
                  _full((1, D)), _full((1, D))],
        out_specs=_full((t, D)), out_shape=jax.ShapeDtypeStruct((t, D), F32),
        scratch_shapes=[pltpu.VMEM((t, D), F32)],
        compiler_params=_cparams(("arbitrary",)), name="moe_dense")(x, lo, hi, rw_pad, w_gate, w_up, w_down, g, b)


def kernel(x_prompt, x_sample, cache_swa_k, cache_swa_v, state_lru_conv, state_lru_h, state_rwkv_shift, state_rwkv_wkv, cache_mem_k, cache_mem_v, mem_prompt, swa_w_qkv, swa_sinks, swa_w_o, lru_w_in, lru_b_in, lru_conv_w, lru_conv_b, lru_w_a, lru_b_a, lru_w_i, lru_b_i, lru_lambda, lru_w_o, rwkv_mu, rwkv_w_r, rwkv_w_k, rwkv_w_v, rwkv_w0, rwkv_w1, rwkv_w2, rwkv_a0, rwkv_a1, rwkv_a2, rwkv_g1, rwkv_g2, rwkv_k_k, rwkv_k_a, rwkv_r_k, rwkv_gn_g, rwkv_gn_b, rwkv_w_o, mem_w_q, mem_w_kv, mem_w_o, ln_g, ln_b, router_w, router_b, moe_w_gate, moe_w_up, moe_w_down):
    n_p, seq, _ = x_prompt.shape
    n_s, dec_seq, _ = x_sample.shape
    assert dec_seq == 1
    past_len = 8192
    xp = x_prompt.reshape(n_p * seq, D)
    xs = x_sample.reshape(n_s, D)
    row = lambda v: v.reshape(1, -1)
    bf = lambda w: w.astype(BF16)

    rw_t = bf(router_w.T)
    rb = router_b.reshape(N_EXPERTS, 1)
    rw_pad = bf(jnp.pad(router_w, ((0, 0), (0, LANES - N_EXPERTS))))
    wg, wu, wd = bf(moe_w_gate), bf(moe_w_up), bf(moe_w_down)
    mem_p = mem_prompt.reshape(n_p * mem_prompt.shape[1], D)
    m_len = mem_prompt.shape[1]

    swa_k_p, swa_v_p, swa_k_s, swa_v_s = [], [], [], []
    lru_c_p, lru_h_p, lru_c_s, lru_h_s = [], [], [], []
    rw_x_p, rw_s_p, rw_x_s, rw_s_s = [], [], [], []
    mem_k_p, mem_v_p = [], []

    for layer in range(DEPTH):
        kind, i = layer % N_MIXERS, layer // N_MIXERS
        g0, b0 = row(ln_g[layer, 0]), row(ln_b[layer, 0])
        if kind == 0:
            w_qkv, w_o = bf(swa_w_qkv[i]), bf(swa_w_o[i])
            keep = min(WINDOW, seq)
            q, k, v, kv_last = _swa_qkv(xp, w_qkv, jnp.arange(seq), n_p, 512, keep, BF16)
            o = _swa_attn_prompt(q, k, v, swa_sinks[i], n_p, 2)
            swa_k_p.append(kv_last[:, :, :KV_WIDTH].reshape(n_p, keep, SWA_KV_HEADS, HEAD_DIM))
            swa_v_p.append(kv_last[:, :, KV_WIDTH:].reshape(n_p, keep, SWA_KV_HEADS, HEAD_DIM))
            xp = _proj_ln(o, w_o, xp, g0, b0, 512)

            qs, _, _, kv_new = _swa_qkv(xs, w_qkv, jnp.full((n_s,), past_len), 1, n_s, n_s, F32)
            kn, vn = kv_new[0, :, :KV_WIDTH], kv_new[0, :, KV_WIDTH:]
            os_ = _swa_attn_sample(qs, kn, vn, cache_swa_k, cache_swa_v, i, swa_sinks[i], 8)
            wb = cache_swa_k.shape[2]
            k_all = jnp.concatenate([cache_swa_k[i], kn.reshape(n_s, 1, SWA_KV_HEADS, HEAD_DIM)], axis=1)
            v_all = jnp.concatenate([cache_swa_v[i], vn.reshape(n_s, 1, SWA_KV_HEADS, HEAD_DIM)], axis=1)
            swa_k_s.append(k_all[:, -wb:])
            swa_v_s.append(v_all[:, -wb:])
            xs = _proj_ln(os_, w_o, xs, g0, b0, n_s)
        elif kind == 1:
            wts = _lru_weights(lru_w_in[i], lru_b_in[i], lru_conv_w[i], lru_conv_b[i], lru_w_a[i], lru_b_a[i],
                               lru_w_i[i], lru_b_i[i], lru_lambda[i], lru_w_o[i])
            xp, conv_last, h_last = _lru_prompt(xp, wts, g0, b0, n_p, 256)
            lru_c_p.append(conv_last[:, SUBLANES - (CONV_W - 1):])
            lru_h_p.append(h_last[:, SUBLANES - 1])
            xs, xb_s, h_s = _lru_sample(xs, state_lru_conv[i], state_lru_h[i], wts, g0, b0)
            lru_c_s.append(jnp.concatenate([state_lru_conv[i][:, 1:], xb_s[:, None]], axis=1))
            lru_h_s.append(h_s)
        else:
            wts = (rwkv_mu[i], bf(rwkv_w_r[i]), bf(rwkv_w_k[i]), bf(rwkv_w_v[i]), row(rwkv_w0[i]), bf(rwkv_w1[i]),
                   bf(rwkv_w2[i]), row(rwkv_a0[i]), bf(rwkv_a1[i]), bf(rwkv_a2[i]), bf(rwkv_g1[i]), bf(rwkv_g2[i]))
            hp = (row(rwkv_k_k[i]), row(rwkv_k_a[i]), row(rwkv_r_k[i]), row(rwkv_gn_g[i]), row(rwkv_gn_b[i]))
            w_o = bf(rwkv_w_o[i])
            rw_x_p.append(xp.reshape(n_p, seq, D)[:, -1])
            rw_x_s.append(xs)
            r, k, v, a, ld, g = _rwkv_pre(xp, jnp.zeros((n_p, SUBLANES, D), F32), wts, n_p, 512, True, BF16)
            o, st = _wkv_prompt(r, k, v, a, ld, g, hp, n_p, 2)
            hd = RWKV_HD
            st = jnp.stack([st[:, :, :hd, :hd], st[:, :, hd:, hd:]], axis=2).reshape(n_p, RWKV_HEADS, hd, hd)
            rw_s_p.append(jnp.swapaxes(st, -1, -2))
            xp = _proj_ln(o, w_o, xp, g0, b0, 512)

            r, k, v, a, ld, g = _rwkv_pre(xs, state_rwkv_shift[i], wts, 1, n_s, False, F32)
            os_, s_new = _wkv_sample(r, k, v, a, ld, g, state_rwkv_wkv[i], hp)
            rw_s_s.append(s_new)
            xs = _proj_ln(os_, w_o, xs, g0, b0, n_s)

        g1, b1 = row(ln_g[layer, 1]), row(ln_b[layer, 1])
        w_q, w_o = bf(mem_w_q[layer]), bf(mem_w_o[layer])
        mkv = _matmul(mem_p, bf(mem_w_kv[layer]), 512)
        mk, mv = mkv[:, :D], mkv[:, D:]
        mem_k_p.append(mk.reshape(n_p, m_len, MEM_HEADS, MEM_HD))
        mem_v_p.append(mv.reshape(n_p, m_len, MEM_HEADS, MEM_HD))
        xp = _mem_attn_prompt(xp, w_q, bf(mk).reshape(n_p, m_len, D), bf(mv).reshape(n_p, m_len, D), w_o, g1, b1,
                              n_p, 512)
        qs = _matmul(xs, w_q, n_s)
        os_ = _mem_attn_sample(qs, cache_mem_k, cache_mem_v, layer, 4)
        xs = _proj_ln(os_, w_o, xs, g1, b1, n_s)

        g2, b2 = row(ln_g[layer, 2]), row(ln_b[layer, 2])
        xp = _moe_ln(xp, rw_t, rb, rw_pad, wg, wu, wd, layer, g2, b2, 512, 256, 512)
        xs = _moe_ln_dense(xs, rw_t, rb, rw_pad, wg, wu, wd, layer, g2, b2)

    return (xp.reshape(n_p, seq, D), xs.reshape(n_s, 1, D),
            jnp.stack(swa_k_p), jnp.stack(swa_v_p), jnp.stack(lru_c_p), jnp.stack(lru_h_p),
            jnp.stack(rw_x_p), jnp.stack(rw_s_p), jnp.stack(mem_k_p), jnp.stack(mem_v_p),
            jnp.stack(swa_k_s), jnp.stack(swa_v_s), jnp.stack(lru_c_s), jnp.stack(lru_h_s),
            jnp.stack(rw_x_s), jnp.stack(rw_s_s))
```

```python
import functools

import jax
import jax.numpy as jnp
from jax import lax
from jax.experimental import pallas as pl
from jax.experimental.pallas import tpu as pltpu

F32 = jnp.float32
BF16 = jnp.bfloat16

D = 1024
DEPTH = 4
N_MIXERS = 3
HEAD_DIM = 64
SWA_HEADS = D // HEAD_DIM
SWA_KV_HEADS = 4
SWA_GROUP = SWA_HEADS // SWA_KV_HEADS
Q_WIDTH = SWA_HEADS * HEAD_DIM
KV_WIDTH = SWA_KV_HEADS * HEAD_DIM
WINDOW = 128
ROT_DIM = HEAD_DIM // 4
ROPE_THETA = 500000.0
LRU_BLOCKS = 16
CONV_W = 4
LRU_C = 8.0
RWKV_HEADS = 16
RWKV_HD = 64
RWKV_GN_EPS = 64e-5
MEM_HEADS = 4
MEM_HD = D // MEM_HEADS
N_EXPERTS = 16
N_GROUPS = 4
EXPERTS_PER_GROUP = 4
EXPERT_FF = 512
LN_EPS = 1e-5
ALPHA = (2.0 * DEPTH) ** 0.25
NEG_INF = -1e30

LANES = 128
SUBLANES = 8
VMEM_LIMIT = 56 * 1024 * 1024
WKV_CHUNK = 64
N_BUCKETS = N_GROUPS * 6
BUCKET_ROWS = 32


def _cparams(sem):
    return pltpu.CompilerParams(dimension_semantics=sem, vmem_limit_bytes=VMEM_LIMIT)


def _dot(a, b):
    return jnp.dot(a, b, preferred_element_type=F32)


def _dot_nt(a, b):
    return lax.dot_general(a, b, (((1,), (1,)), ((), ())), preferred_element_type=F32)


def _dot_tn(a, b):
    return lax.dot_general(a, b, (((0,), (0,)), ((), ())), preferred_element_type=F32)


def _ln(z, g, b):
    mu = jnp.mean(z, axis=-1, keepdims=True)
    zc = z - mu
    var = jnp.mean(zc * zc, axis=-1, keepdims=True)
    return zc * lax.rsqrt(var + LN_EPS) * g + b


def _softplus(z):
    return jnp.maximum(z, 0.0) + jnp.log1p(jnp.exp(-jnp.abs(z)))


def _sigmoid(z):
    return 1.0 / (1.0 + jnp.exp(-z))


def _round_bf16(x):
    return x.astype(BF16).astype(F32)


def _full(shape):
    nd = len(shape)
    return pl.BlockSpec(shape, lambda *_: (0,) * nd)


def _mm_kernel(a_ref, w_ref, o_ref):
    o_ref[...] = _dot(a_ref[...].astype(BF16), w_ref[...]).astype(o_ref.dtype)


def _matmul(a, w, tm, out_dtype=F32):
    t, k = a.shape
    n = w.shape[1]
    return pl.pallas_call(
        _mm_kernel, grid=(t // tm,),
        in_specs=[pl.BlockSpec((tm, k), lambda i: (i, 0)), _full((k, n))],
        out_specs=pl.BlockSpec((tm, n), lambda i: (i, 0)),
        out_shape=jax.ShapeDtypeStruct((t, n), out_dtype),
        compiler_params=_cparams(("parallel",)), name="matmul")(a, w)


def _proj_ln_kernel(a_ref, w_ref, x_ref, g_ref, b_ref, o_ref):
    acc = _dot(a_ref[...].astype(BF16), w_ref[...])
    o_ref[...] = _ln(ALPHA * x_ref[...] + acc, g_ref[...], b_ref[...])


def _proj_ln(a, w, x, g, b, tm):
    t, k = a.shape
    return pl.pallas_call(
        _proj_ln_kernel, grid=(t // tm,),
        in_specs=[pl.BlockSpec((tm, k), lambda i: (i, 0)), _full((k, D)),
                  pl.BlockSpec((tm, D), lambda i: (i, 0)), _full((1, D)), _full((1, D))],
        out_specs=pl.BlockSpec((tm, D), lambda i: (i, 0)),
        out_shape=jax.ShapeDtypeStruct((t, D), F32),
        compiler_params=_cparams(("parallel",)), name="proj_ln")(a, w, x, g, b)


def _rope_tables(pos):
    half = ROT_DIM // 2
    inv_freq = ROPE_THETA ** (-jnp.arange(half, dtype=F32) / half)
    ang = pos.astype(F32)[:, None] * inv_freq
    cos, sin = jnp.cos(ang), jnp.sin(ang)
    one = jnp.ones((pos.shape[0], HEAD_DIM - ROT_DIM), F32)
    zero = jnp.zeros((pos.shape[0], HEAD_DIM - ROT_DIM), F32)
    zh = jnp.zeros_like(sin)
    c = jnp.concatenate([cos, cos, one], axis=1)
    s1 = jnp.concatenate([-sin, zh, zero], axis=1)
    s2 = jnp.concatenate([zh, sin, zero], axis=1)
    rep = LANES // HEAD_DIM
    return jnp.tile(c, (1, rep)), jnp.tile(s1, (1, rep)), jnp.tile(s2, (1, rep))


def _swa_qkv_kernel(x_ref, w_ref, c_ref, s1_ref, s2_ref, q_ref, k_ref, v_ref, kv_ref, *, tm, keep):
    acc = _dot(x_ref[...].astype(BF16), w_ref[...])
    c, s1, s2 = c_ref[...], s1_ref[...], s2_ref[...]
    half = ROT_DIM // 2
    n_q = Q_WIDTH // LANES
    n_k = KV_WIDTH // LANES
    for cg in range(n_q + n_k):
        xg = acc[:, cg * LANES:(cg + 1) * LANES]
        rot = xg * c + pltpu.roll(xg, LANES - half, 1) * s1 + pltpu.roll(xg, half, 1) * s2
        if cg < n_q:
            q_ref[:, cg * LANES:(cg + 1) * LANES] = rot.astype(q_ref.dtype)
        else:
            ck = cg - n_q
            k_ref[:, ck * LANES:(ck + 1) * LANES] = rot.astype(k_ref.dtype)
            kv_ref[0, :, ck * LANES:(ck + 1) * LANES] = rot[tm - keep:, :]
    v = acc[:, Q_WIDTH + KV_WIDTH:]
    v_ref[...] = v.astype(v_ref.dtype)
    kv_ref[0, :, KV_WIDTH:] = v[tm - keep:, :]


def _swa_qkv(x, w_qkv, pos, n_seq, tm, keep, qdtype):
    t = x.shape[0]
    s = t // n_seq
    nb = s // tm
    c, s1, s2 = _rope_tables(pos)
    row = lambda n, i: (n * nb + i, 0)
    tab = pl.BlockSpec((tm, LANES), lambda n, i: (i, 0))
    kern = functools.partial(_swa_qkv_kernel, tm=tm, keep=keep)
    return pl.pallas_call(
        kern, grid=(n_seq, nb),
        in_specs=[pl.BlockSpec((tm, D), row), _full((D, Q_WIDTH + 2 * KV_WIDTH)), tab, tab, tab],
        out_specs=[pl.BlockSpec((tm, Q_WIDTH), row), pl.BlockSpec((tm, KV_WIDTH), row),
                   pl.BlockSpec((tm, KV_WIDTH), row),
                   pl.BlockSpec((1, keep, 2 * KV_WIDTH), lambda n, i: (n, 0, 0))],
        out_shape=[jax.ShapeDtypeStruct((t, Q_WIDTH), qdtype), jax.ShapeDtypeStruct((t, KV_WIDTH), qdtype),
                   jax.ShapeDtypeStruct((t, KV_WIDTH), qdtype),
                   jax.ShapeDtypeStruct((n_seq, keep, 2 * KV_WIDTH), F32)],
        compiler_params=_cparams(("parallel", "arbitrary")), name="swa_qkv")(x, w_qkv, c, s1, s2)


def _swa_attn_kernel(sink_ref, q_ref, kp_ref, kc_ref, vp_ref, vc_ref, o_ref, *, nq):
    j = pl.program_id(1)
    w, grp = WINDOW, SWA_GROUP
    r = lax.broadcasted_iota(jnp.int32, (grp * w, 2 * w), 0) % w
    c = lax.broadcasted_iota(jnp.int32, (grp * w, 2 * w), 1)
    in_prev = jnp.logical_and(c < w, c > r)
    in_cur = jnp.logical_and(c >= w, (c - w) <= r)
    ok_inner = jnp.logical_or(in_prev, in_cur)
    ok_first = jnp.logical_or(jnp.logical_and(in_prev, j > 0), in_cur)
    scale = HEAD_DIM ** -0.5
    combos = [(u, h) for u in range(nq) for h in range(SWA_KV_HEADS)]
    kcat, vcat, q4, sink, ok = [], [], [], [], []
    for u, h in combos:
        sl = slice(h * HEAD_DIM, (h + 1) * HEAD_DIM)
        rows = slice(u * w, (u + 1) * w)
        before = slice((u - 1) * w, u * w)
        k_prev = kp_ref[:, sl] if u == 0 else kc_ref[before, sl]
        v_prev = vp_ref[:, sl] if u == 0 else vc_ref[before, sl]
        kcat.append(jnp.concatenate([k_prev, kc_ref[rows, sl]], axis=0))
        vcat.append(jnp.concatenate([v_prev, vc_ref[rows, sl]], axis=0))
        heads = [h * grp + g for g in range(grp)]
        q4.append(jnp.concatenate([q_ref[rows, hq * HEAD_DIM:(hq + 1) * HEAD_DIM] for hq in heads], axis=0))
        sink.append(jnp.concatenate([jnp.full((w, 1), sink_ref[hq], F32) for hq in heads], axis=0))
        ok.append(ok_first if u == 0 else ok_inner)
    n = range(len(combos))
    s = [jnp.where(ok[i], _dot_nt(q4[i], kcat[i]) * scale, NEG_INF) for i in n]
    m = [jnp.maximum(jnp.max(s[i], axis=-1, keepdims=True), sink[i]) for i in n]
    p = [jnp.exp(s[i] - m[i]) for i in n]
    den = [jnp.sum(p[i], axis=-1, keepdims=True) + jnp.exp(sink[i] - m[i]) for i in n]
    o = [_dot((p[i] / den[i]).astype(BF16), vcat[i]) for i in n]
    for i, (u, h) in enumerate(combos):
        for g in range(grp):
            hq = h * grp + g
            o_ref[u * w:(u + 1) * w, hq * HEAD_DIM:(hq + 1) * HEAD_DIM] = o[i][g * w:(g + 1) * w].astype(o_ref.dtype)


def _swa_attn_prompt(q, k, v, sinks, n_seq, nq):
    t = q.shape[0]
    nb = t // n_seq // WINDOW
    ns = nb // nq
    cur = lambda n, j: (n * ns + j, 0)
    prev = lambda n, j: (n * nb + jnp.maximum(j * nq - 1, 0), 0)
    return pl.pallas_call(
        functools.partial(_swa_attn_kernel, nq=nq), grid=(n_seq, ns),
        in_specs=[pl.BlockSpec(memory_space=pltpu.SMEM), pl.BlockSpec((nq * WINDOW, Q_WIDTH), cur),
                  pl.BlockSpec((WINDOW, KV_WIDTH), prev), pl.BlockSpec((nq * WINDOW, KV_WIDTH), cur),
                  pl.BlockSpec((WINDOW, KV_WIDTH), prev), pl.BlockSpec((nq * WINDOW, KV_WIDTH), cur)],
        out_specs=pl.BlockSpec((nq * WINDOW, Q_WIDTH), cur),
        out_shape=jax.ShapeDtypeStruct((t, Q_WIDTH), BF16),
        compiler_params=_cparams(("parallel", "arbitrary")), name="swa_attn")(sinks, q, k, k, v, v)


def _swa_sample_kernel(sink_ref, qbd_ref, q_ref, kn_ref, vnbd_ref, ckt_ref, cvt_ref, o_ref, *, bs):
    nkv, hd, wb = ckt_ref.shape[1:]
    nq = q_ref.shape[1]
    npad = kn_ref.shape[1]
    key = lax.broadcasted_iota(jnp.int32, (nq, wb), 1)
    valid = (wb - key) < WINDOW
    own_new = (lax.broadcasted_iota(jnp.int32, (nq, npad), 1)
               == lax.broadcasted_iota(jnp.int32, (nq, npad), 0) // SWA_GROUP)
    sink = sink_ref[...]
    scale = HEAD_DIM ** -0.5
    nb = range(bs)
    s = [jnp.where(valid, _dot(qbd_ref[b].astype(BF16), ckt_ref[b].reshape(nkv * hd, wb).astype(BF16)) * scale, NEG_INF)
         for b in nb]
    sn = [jnp.where(own_new, _dot_nt(q_ref[b].astype(BF16), kn_ref[b].astype(BF16)) * scale, NEG_INF) for b in nb]
    m = [jnp.maximum(jnp.maximum(jnp.max(s[b], axis=-1, keepdims=True), jnp.max(sn[b], axis=-1, keepdims=True)), sink)
         for b in nb]
    p = [jnp.where(valid, jnp.exp(s[b] - m[b]), 0.0) for b in nb]
    pn = [jnp.where(own_new, jnp.exp(sn[b] - m[b]), 0.0) for b in nb]
    den = [jnp.sum(p[b], axis=-1, keepdims=True) + jnp.sum(pn[b], axis=-1, keepdims=True) + jnp.exp(sink - m[b])
           for b in nb]
    for b in nb:
        o_ref[b] = (_dot_nt((p[b] / den[b]).astype(BF16), cvt_ref[b].reshape(nkv * hd, wb).astype(BF16))
                    + _dot((pn[b] / den[b]).astype(BF16), vnbd_ref[b].astype(BF16)))


def _swa_attn_sample(q, kn, vn, cache_k, cache_v, layer, sinks, bs):
    _, b, wb, nkv, hd = cache_k.shape
    grp = SWA_HEADS // nkv
    eye = jnp.eye(nkv, dtype=q.dtype)
    qbd = jnp.einsum('bhgd,hk->bhgkd', q.reshape(b, nkv, grp, hd), eye).reshape(b, SWA_HEADS, nkv * hd)
    pad = lambda z: jnp.pad(z, ((0, 0), (0, SUBLANES - nkv), (0, 0)))
    vnbd = pad(jnp.einsum('bhd,hk->bhkd', vn.reshape(b, nkv, hd), eye).reshape(b, nkv, nkv * hd))
    blk = lambda r, w: pl.BlockSpec((bs, r, w), lambda i: (i, 0, 0))
    cblk = pl.BlockSpec((None, bs, nkv, hd, wb), lambda i: (layer, i, 0, 0, 0))
    out = pl.pallas_call(
        functools.partial(_swa_sample_kernel, bs=bs), grid=(b // bs,),
        in_specs=[_full((SWA_HEADS, 1)), blk(SWA_HEADS, nkv * hd), blk(SWA_HEADS, hd), blk(SUBLANES, hd),
                  blk(SUBLANES, nkv * hd), cblk, cblk],
        out_specs=blk(SWA_HEADS, nkv * hd), out_shape=jax.ShapeDtypeStruct((b, SWA_HEADS, nkv * hd), F32),
        compiler_params=_cparams(("parallel",)), name="swa_sample")(
            sinks.reshape(SWA_HEADS, 1), qbd, q.reshape(b, SWA_HEADS, hd), pad(kn.reshape(b, nkv, hd)), vnbd,
            jnp.transpose(cache_k, (0, 1, 3, 4, 2)), jnp.transpose(cache_v, (0, 1, 3, 4, 2)))
    o5 = out.reshape(b, nkv, grp, nkv, hd)
    return jnp.stack([o5[:, h, :, h, :] for h in range(nkv)], axis=1).reshape(b, Q_WIDTH)


def _gelu_tanh(x):
    return 0.5 * x * (1.0 + jnp.tanh(0.7978845608028654 * (x + 0.044715 * x * x * x)))


def _lru_gates(xc, wa_ref, ba, wi_ref, bi, lam):
    xcb = xc.astype(BF16)
    gw = wa_ref.shape[1]
    ra, ia = [], []
    for gi in range(wa_ref.shape[0]):
        xs = xcb[:, gi * gw:(gi + 1) * gw]
        ra.append(_dot(xs, wa_ref[gi]))
        ia.append(_dot(xs, wi_ref[gi]))
    r = _sigmoid(jnp.concatenate(ra, axis=-1) + ba)
    ig = _sigmoid(jnp.concatenate(ia, axis=-1) + bi)
    log_a = -LRU_C * r * _softplus(-lam)
    a = jnp.exp(log_a)
    b = jnp.sqrt(-jnp.tanh(log_a) * (a * a + 1.0)) * (ig * xc)
    return a, b


def _shift_rows(ext, s, tm):
    return pltpu.roll(ext, s, 0)[SUBLANES:SUBLANES + tm]


def _lru_prompt_kernel(x_ref, win_ref, bin_ref, cw_ref, cb_ref, wa_ref, ba_ref, wi_ref, bi_ref, lam_ref,
                       wo_ref, g_ref, b_ref, o_ref, conv_ref, hl_ref, cx_ref, ch_ref, *, tm):
    i = pl.program_id(1)

    @pl.when(i == 0)
    def _():
        cx_ref[...] = jnp.zeros_like(cx_ref)
        ch_ref[...] = jnp.zeros_like(ch_ref)

    x = x_ref[...]
    xy = _dot(x.astype(BF16), win_ref[...]) + bin_ref[...]
    xb = xy[:, :D]
    y_gate = _gelu_tanh(xy[:, D:])
    ext = jnp.concatenate([cx_ref[...], xb], axis=0)
    cw = cw_ref[...]
    xc = cb_ref[...] + xb * cw[CONV_W - 1:CONV_W]
    for s in range(1, CONV_W):
        xc = xc + _shift_rows(ext, s, tm) * cw[CONV_W - 1 - s:CONV_W - s]
    cx_ref[...] = xb[tm - SUBLANES:]
    conv_ref[0] = xb[tm - SUBLANES:]

    a, b = _lru_gates(xc, wa_ref, ba_ref[...], wi_ref, bi_ref[...], lam_ref[...])
    sub = lax.broadcasted_iota(jnp.int32, (tm, 1), 0) % SUBLANES
    s = 1
    while s < SUBLANES:
        keep = sub >= s
        a_sh = jnp.where(keep, pltpu.roll(a, s, 0), 1.0)
        b_sh = jnp.where(keep, pltpu.roll(b, s, 0), 0.0)
        b = a * b_sh + b
        a = a * a_sh
        s *= 2
    carry = ch_ref[SUBLANES - 1:SUBLANES, :]
    groups = []
    for gi in range(tm // SUBLANES):
        rows = slice(gi * SUBLANES, (gi + 1) * SUBLANES)
        hg = a[rows] * carry + b[rows]
        groups.append(hg)
        carry = hg[SUBLANES - 1:SUBLANES]
    h = jnp.concatenate(groups, axis=0)
    ch_ref[...] = h[tm - SUBLANES:]
    hl_ref[0] = h[tm - SUBLANES:]
    acc = _dot((h * y_gate).astype(BF16), wo_ref[...])
    o_ref[...] = _ln(ALPHA * x + acc, g_ref[...], b_ref[...])


def _lru_weights(w_in, b_in, conv_w, conv_b, w_a, b_a, w_i, b_i, lam, w_o):
    gsz = 4
    ng = LRU_BLOCKS // gsz
    bw = D // LRU_BLOCKS

    def grouped(w):
        w4 = w.reshape(ng, gsz, bw, bw)
        return jnp.einsum('gaij,ab->gaibj', w4, jnp.eye(gsz, dtype=w.dtype)).reshape(ng, gsz * bw, gsz * bw).astype(BF16)

    row = lambda v: v.reshape(1, -1)
    return (w_in.astype(BF16), row(b_in), conv_w, row(conv_b), grouped(w_a), row(b_a), grouped(w_i), row(b_i),
            row(lam), w_o.astype(BF16))


def _lru_prompt(x, wts, g, b, n_seq, tm):
    t = x.shape[0]
    nb = t // n_seq // tm
    row = lambda n, i: (n * nb + i, 0)
    last = pl.BlockSpec((1, SUBLANES, D), lambda n, i: (n, 0, 0))
    w_in, b_in, cw, cb, wa, ba, wi, bi, lam, wo = wts
    return pl.pallas_call(
        functools.partial(_lru_prompt_kernel, tm=tm), grid=(n_seq, nb),
        in_specs=[pl.BlockSpec((tm, D), row), _full(w_in.shape), _full(b_in.shape), _full(cw.shape), _full(cb.shape),
                  _full(wa.shape), _full(ba.shape), _full(wi.shape), _full(bi.shape), _full(lam.shape),
                  _full(wo.shape), _full((1, D)), _full((1, D))],
        out_specs=[pl.BlockSpec((tm, D), row), last, last],
        out_shape=[jax.ShapeDtypeStruct((t, D), F32), jax.ShapeDtypeStruct((n_seq, SUBLANES, D), F32),
                   jax.ShapeDtypeStruct((n_seq, SUBLANES, D), F32)],
        scratch_shapes=[pltpu.VMEM((SUBLANES, D), F32), pltpu.VMEM((SUBLANES, D), F32)],
        compiler_params=_cparams(("parallel", "arbitrary")), name="lru_prompt")(x, *wts, g, b)


def _lru_sample_kernel(x_ref, c0_ref, c1_ref, c2_ref, h0_ref, win_ref, bin_ref, cw_ref, cb_ref, wa_ref, ba_ref,
                       wi_ref, bi_ref, lam_ref, wo_ref, g_ref, b_ref, o_ref, xb_ref, h_ref):
    x = x_ref[...]
    xy = _dot(x.astype(BF16), win_ref[...]) + bin_ref[...]
    xb = xy[:, :D]
    y_gate = _gelu_tanh(xy[:, D:])
    cw = cw_ref[...]
    xc = (cb_ref[...] + c0_ref[...] * cw[0:1] + c1_ref[...] * cw[1:2] + c2_ref[...] * cw[2:3] + xb * cw[3:4])
    a, b = _lru_gates(xc, wa_ref, ba_ref[...], wi_ref, bi_ref[...], lam_ref[...])
    h = a * h0_ref[...] + b
    xb_ref[...] = xb
    h_ref[...] = h
    acc = _dot((h * y_gate).astype(BF16), wo_ref[...])
    o_ref[...] = _ln(ALPHA * x + acc, g_ref[...], b_ref[...])


def _lru_sample(x, conv_state, h0, wts, g, b):
    t = x.shape[0]
    args = (x, conv_state[:, 0], conv_state[:, 1], conv_state[:, 2], h0, *wts, g, b)
    sd = jax.ShapeDtypeStruct((t, D), F32)
    return pl.pallas_call(
        _lru_sample_kernel, grid=(1,),
        in_specs=[_full(a.shape) for a in args],
        out_specs=[_full((t, D))] * 3, out_shape=[sd, sd, sd],
        compiler_params=_cparams(("arbitrary",)), name="lru_sample")(*args)


def _rwkv_pre_kernel(x_ref, xp_ref, mu_ref, wr_ref, wk_ref, wv_ref, w0_ref, w1_ref, w2_ref, a0_ref, a1_ref, a2_ref,
                     g1_ref, g2_ref, r_ref, k_ref, v_ref, a_ref, ld_ref, g_ref, *scratch, tm, seq):
    x = x_ref[...]
    if seq:
        cx_ref, = scratch
        i = pl.program_id(1)

        @pl.when(i == 0)
        def _():
            cx_ref[...] = xp_ref[0]

        x_prev = _shift_rows(jnp.concatenate([cx_ref[...], x], axis=0), 1, tm)
        cx_ref[...] = x[tm - SUBLANES:]
    else:
        x_prev = xp_ref[...]
    xx = x_prev - x
    mu = mu_ref[...]
    mix = lambda j: (x + xx * mu[j:j + 1]).astype(BF16)
    r_ref[...] = _dot(mix(0), wr_ref[...]).astype(r_ref.dtype)
    wl = _dot(jnp.tanh(_dot(mix(1), w1_ref[...])).astype(BF16), w2_ref[...])
    w = -_softplus(-(w0_ref[...] + wl)) - 0.5
    ld_ref[...] = -jnp.exp(w)
    k_ref[...] = _dot(mix(2), wk_ref[...]).astype(k_ref.dtype)
    v_ref[...] = _dot(mix(3), wv_ref[...]).astype(v_ref.dtype)
    al = _dot(_dot(mix(4), a1_ref[...]).astype(BF16), a2_ref[...])
    a_ref[...] = _sigmoid(a0_ref[...] + al).astype(a_ref.dtype)
    g_ref[...] = _dot(_sigmoid(_dot(mix(5), g1_ref[...])).astype(BF16), g2_ref[...]).astype(g_ref.dtype)


def _rwkv_pre(x, x_prev, wts, n_seq, tm, seq, dtype):
    t = x.shape[0]
    nb = t // n_seq // tm
    row = lambda n, i: (n * nb + i, 0)
    xp_spec = pl.BlockSpec((1, SUBLANES, D), lambda n, i: (n, 0, 0)) if seq else pl.BlockSpec((tm, D), row)
    sd = lambda dt: jax.ShapeDtypeStruct((t, D), dt)
    blk = pl.BlockSpec((tm, D), row)
    return pl.pallas_call(
        functools.partial(_rwkv_pre_kernel, tm=tm, seq=seq), grid=(n_seq, nb),
        in_specs=[blk, xp_spec] + [_full(w.shape) for w in wts],
        out_specs=[blk] * 6,
        out_shape=[sd(dtype), sd(dtype), sd(dtype), sd(dtype), sd(F32), sd(dtype)],
        scratch_shapes=[pltpu.VMEM((SUBLANES, D), F32)] if seq else [],
        compiler_params=_cparams(("parallel", "arbitrary")), name="rwkv_pre")(x, x_prev, *wts)


def _seg_sum(x, first):
    s0 = jnp.sum(jnp.where(first, x, 0.0), axis=-1, keepdims=True)
    s1 = jnp.sum(jnp.where(first, 0.0, x), axis=-1, keepdims=True)
    return jnp.where(first, s0, s1)


def _wkv_kernel(r_ref, k_ref, v_ref, a_ref, ld_ref, g_ref, kk_ref, ka_ref, rk_ref, gg_ref, gb_ref,
                o_ref, s_ref, st_ref):
    c = pl.program_id(1)
    L = WKV_CHUNK
    P2 = 2 * L
    nch = r_ref.shape[0] // L

    @pl.when(c == 0)
    def _():
        st_ref[...] = jnp.zeros_like(st_ref)

    ld_all = ld_ref[...]
    tri = (lax.broadcasted_iota(jnp.int32, (L, L), 0) >= lax.broadcasted_iota(jnp.int32, (L, L), 1)).astype(BF16)
    hi = ld_all.astype(BF16)
    r1 = ld_all - hi.astype(F32)
    mid = r1.astype(BF16)
    lo = (r1 - mid.astype(F32)).astype(BF16)
    chunk_rows = [slice(ci * L, (ci + 1) * L) for ci in range(nch)]
    cum_ch = [_dot(tri, hi[rw]) + _dot(tri, mid[rw]) + _dot(tri, lo[rw]) for rw in chunk_rows]

    lane = lax.broadcasted_iota(jnp.int32, (1, LANES), 1)
    first = lane < RWKV_HD
    ri = lax.broadcasted_iota(jnp.int32, (P2, P2), 0)
    ci_ = lax.broadcasted_iota(jnp.int32, (P2, P2), 1)
    same_head = (ri // L) == (ci_ // L)
    rt, ct = ri % L, ci_ % L
    strict = jnp.logical_and(same_head, rt > ct)
    incl = jnp.logical_and(same_head, rt >= ct)
    eye = ri == ci_

    def stack(xv):
        return jnp.concatenate([jnp.where(first, xv, 0.0), jnp.where(first, 0.0, xv)], axis=0).astype(BF16)

    npair = RWKV_HEADS // 2
    combos = [(ci, p) for ci in range(nch) for p in range(npair)]
    n = range(len(combos))
    sls = [slice(p * LANES, (p + 1) * LANES) for p in range(npair)]
    ws, us, ks, rs, ul, kl, vs, g_l, bonus = ([] for _ in range(9))
    for ci, p in combos:
        rw, sl = chunk_rows[ci], sls[p]
        rp, kp, vp, ap = (ref[rw, sl].astype(F32) for ref in (r_ref, k_ref, v_ref, a_ref))
        ldp, cum = ld_all[rw, sl], cum_ch[ci][:, sl]
        kk = kp * kk_ref[:, sl]
        kk = kk / jnp.maximum(jnp.sqrt(_seg_sum(kk * kk, first)), 1e-12)
        kmod = kp * (1.0 + (ap - 1.0) * ka_ref[:, sl])
        bp = kk * ap
        cum_l = cum[L - 1:L, :]
        g_inv = jnp.exp(-cum)
        g_to_end = jnp.exp(cum_l - cum)
        ws.append(stack(kk * jnp.exp(cum - ldp)))
        us.append(stack(bp * g_inv))
        ks.append(stack(kmod * g_inv))
        rs.append(stack(rp * jnp.exp(cum)))
        ul.append(stack(bp * g_to_end))
        kl.append(stack(kmod * g_to_end))
        vs.append(stack(vp))
        g_l.append(jnp.exp(cum_l))
        bonus.append(_seg_sum(rp * kmod * rk_ref[:, sl], first) * vp)

    gram = [_dot_nt(jnp.concatenate([ws[q], rs[q]], axis=0), jnp.concatenate([us[q], ks[q]], axis=0)) for q in n]
    n_mat = [jnp.where(strict, gram[q][:P2, :P2], 0.0) for q in n]
    m_mat = [jnp.where(strict, gram[q][:P2, P2:], 0.0).astype(BF16) for q in n]
    nr_mat = [jnp.where(incl, gram[q][P2:, :P2], 0.0).astype(BF16) for q in n]
    mr_mat = [jnp.where(incl, gram[q][P2:, P2:], 0.0).astype(BF16) for q in n]

    def level_mask(sz):
        sub = jnp.logical_and((rt // sz) % 2 == 1, (ct // sz) % 2 == 0)
        return jnp.logical_and(jnp.logical_and(sub, (rt // (2 * sz)) == (ct // (2 * sz))), same_head)

    x_inv = [jnp.where(eye, 1.0, 0.0) - jnp.where(level_mask(1), n_mat[q], 0.0) for q in n]
    sz = 2
    while sz < L:
        mask = level_mask(sz)
        xb = [x_inv[q].astype(BF16) for q in n]
        xc = [_dot(xb[q], jnp.where(mask, n_mat[q], 0.0).astype(BF16)).astype(BF16) for q in n]
        x_inv = [x_inv[q] - _dot(xc[q], xb[q]) for q in n]
        sz *= 2
    x_inv = [x_inv[q].astype(BF16) for q in n]

    state = [st_ref[p] for p in range(npair)]
    inv_n = 1.0 / RWKV_HD
    for ci in range(nch):
        qs = [ci * npair + p for p in range(npair)]
        a0b = [state[p].astype(BF16) for p in range(npair)]
        rhs = [_dot(jnp.concatenate([ws[q], m_mat[q]], axis=1), jnp.concatenate([a0b[p], vs[q]], axis=0)).astype(BF16)
               for p, q in enumerate(qs)]
        pm = [(-_dot(x_inv[q], rhs[p])).astype(BF16) for p, q in enumerate(qs)]
        o_st = [_dot(jnp.concatenate([rs[q], nr_mat[q], mr_mat[q]], axis=1),
                     jnp.concatenate([a0b[p], pm[p], vs[q]], axis=0)) for p, q in enumerate(qs)]
        new_state = []
        for p, q in enumerate(qs):
            g_col = jnp.sum(jnp.where(eye, jnp.broadcast_to(g_l[q], (P2, P2)), 0.0), axis=-1, keepdims=True)
            new_state.append(g_col * state[p] + _dot_tn(jnp.concatenate([ul[q], kl[q]], axis=0),
                                                        jnp.concatenate([pm[p], vs[q]], axis=0)))
        state = new_state
        for p, q in enumerate(qs):
            sl = sls[p]
            o = o_st[p][:L] + o_st[p][L:]
            mu = _seg_sum(o, first) * inv_n
            oc = o - mu
            var = _seg_sum(oc * oc, first) * inv_n
            on = oc * lax.rsqrt(var + RWKV_GN_EPS) * gg_ref[:, sl] + gb_ref[:, sl]
            o_ref[chunk_rows[ci], sl] = ((on + bonus[q]) * g_ref[chunk_rows[ci], sl].astype(F32)).astype(o_ref.dtype)

    for p in range(npair):
        st_ref[p] = state[p]
    s_ref[0] = st_ref[...]


def _wkv_prompt(r, k, v, a, ld, g, hp, n_seq, nch):
    t = r.shape[0]
    L = WKV_CHUNK * nch
    nc = t // n_seq // L
    row = lambda n, c: (n * nc + c, 0)
    blk = pl.BlockSpec((L, D), row)
    npair = RWKV_HEADS // 2
    return pl.pallas_call(
        _wkv_kernel, grid=(n_seq, nc),
        in_specs=[blk] * 6 + [_full((1, D))] * 5,
        out_specs=[blk, pl.BlockSpec((1, npair, LANES, LANES), lambda n, c: (n, 0, 0, 0))],
        out_shape=[jax.ShapeDtypeStruct((t, D), BF16), jax.ShapeDtypeStruct((n_seq, npair, LANES, LANES), F32)],
        scratch_shapes=[pltpu.VMEM((npair, LANES, LANES), F32)],
        compiler_params=_cparams(("parallel", "arbitrary")), name="wkv_chunk")(r, k, v, a, ld, g, *hp)


def _wkv_sample_kernel(r_ref, k_ref, v_ref, a_ref, ld_ref, g_ref, s_ref, kk_ref, ka_ref, rk_ref, gg_ref, gb_ref,
                       o_ref, so_ref):
    r, k, v, a, ld, g = (ref[0] for ref in (r_ref, k_ref, v_ref, a_ref, ld_ref, g_ref))
    kk = k * kk_ref[0]
    kk = kk / jnp.maximum(jnp.sqrt(jnp.sum(kk * kk, axis=0, keepdims=True)), 1e-12)
    kmod = k * (1.0 + (a - 1.0) * ka_ref[0])
    akk = kk * a
    decay = jnp.exp(ld)

    def value_row(vi, carry):
        s = s_ref[0, vi]
        skk = jnp.sum(s * kk, axis=0, keepdims=True)
        s_new = s * decay - skk * akk + v_ref[0, pl.ds(vi, 1), :] * kmod
        so_ref[0, vi] = s_new
        o_ref[0, pl.ds(vi, 1), :] = jnp.sum(s_new * r, axis=0, keepdims=True)
        return carry

    lax.fori_loop(0, s_ref.shape[1], value_row, 0, unroll=4)
    o = o_ref[0]
    mu = jnp.mean(o, axis=0, keepdims=True)
    oc = o - mu
    var = jnp.mean(oc * oc, axis=0, keepdims=True)
    on = oc * lax.rsqrt(var + RWKV_GN_EPS) * gg_ref[0] + gb_ref[0]
    bonus = jnp.sum(r * kmod * rk_ref[0], axis=0, keepdims=True) * v
    o_ref[0] = (on + bonus) * g


def _wkv_sample(r, k, v, a, ld, g, state, hp):
    b = r.shape[0]
    nh, hd = RWKV_HEADS, RWKV_HD
    t3 = lambda z: jnp.transpose(z.reshape(b, nh, hd), (1, 2, 0))
    vec = pl.BlockSpec((1, hd, b), lambda h: (h, 0, 0))
    par = pl.BlockSpec((1, hd, 1), lambda h: (h, 0, 0))
    sblk = pl.BlockSpec((1, hd, hd, b), lambda h: (h, 0, 0, 0))
    o, s_new = pl.pallas_call(
        _wkv_sample_kernel, grid=(nh,),
        in_specs=[vec] * 6 + [sblk] + [par] * 5,
        out_specs=[vec, sblk],
        out_shape=[jax.ShapeDtypeStruct((nh, hd, b), F32), jax.ShapeDtypeStruct((nh, hd, hd, b), F32)],
        compiler_params=_cparams(("parallel",)), name="wkv_sample")(
            t3(r), t3(k), t3(v), t3(a), t3(ld), t3(g), jnp.transpose(state, (1, 2, 3, 0)),
            *[z.reshape(nh, hd, 1) for z in hp])
    return jnp.transpose(o, (2, 0, 1)).reshape(b, D), jnp.transpose(s_new, (3, 0, 1, 2))


def _mem_prompt_kernel(x_ref, wq_ref, mk_ref, mv_ref, wo_ref, g_ref, b_ref, o_ref):
    x = x_ref[...]
    q = _dot(x.astype(BF16), wq_ref[...]).astype(BF16)
    scale = MEM_HD ** -0.5
    sls = [slice(h * MEM_HD, (h + 1) * MEM_HD) for h in range(MEM_HEADS)]
    s = [_dot_nt(q[:, sl], mk_ref[0, :, sl]) * scale for sl in sls]
    p = [jnp.exp(sh - jnp.max(sh, axis=-1, keepdims=True)) for sh in s]
    den = [jnp.sum(ph, axis=-1, keepdims=True) for ph in p]
    outs = [_dot((ph / dh).astype(BF16), mv_ref[0, :, sl]).astype(BF16) for ph, dh, sl in zip(p, den, sls)]
    acc = _dot(jnp.concatenate(outs, axis=-1), wo_ref[...])
    o_ref[...] = _ln(ALPHA * x + acc, g_ref[...], b_ref[...])


def _mem_attn_prompt(x, w_q, mk, mv, w_o, g, b, n_seq, tm):
    t = x.shape[0]
    nb = t // n_seq // tm
    m = mk.shape[1]
    row = lambda n, i: (n * nb + i, 0)
    mem = pl.BlockSpec((1, m, D), lambda n, i: (n, 0, 0))
    return pl.pallas_call(
        _mem_prompt_kernel, grid=(n_seq, nb),
        in_specs=[pl.BlockSpec((tm, D), row), _full((D, D)), mem, mem, _full((D, D)), _full((1, D)), _full((1, D))],
        out_specs=pl.BlockSpec((tm, D), row), out_shape=jax.ShapeDtypeStruct((t, D), F32),
        compiler_params=_cparams(("parallel", "arbitrary")), name="mem_attn")(x, w_q, mk, mv, w_o, g, b)


def _mem_sample_kernel(q_ref, ck_ref, cv_ref, o_ref, *, bs):
    m, nh, hd = ck_ref.shape[1:]
    rows = q_ref.shape[1]
    col_head = lax.broadcasted_iota(jnp.int32, (rows, m * nh), 1) % nh
    own = col_head == lax.broadcasted_iota(jnp.int32, (rows, m * nh), 0)
    scale = MEM_HD ** -0.5
    nb = range(bs)
    s = [jnp.where(own, _dot_nt(q_ref[b].astype(BF16), ck_ref[b].reshape(m * nh, hd).astype(BF16)) * scale, NEG_INF)
         for b in nb]
    p = [jnp.where(own, jnp.exp(s[b] - jnp.max(s[b], axis=-1, keepdims=True)), 0.0) for b in nb]
    den = [jnp.sum(p[b], axis=-1, keepdims=True) for b in nb]
    for b in nb:
        pb = (p[b] / jnp.where(den[b] > 0.0, den[b], 1.0)).astype(BF16)
        o_ref[b] = _dot(pb, cv_ref[b].reshape(m * nh, hd).astype(BF16))


def _mem_attn_sample(q, cache_k, cache_v, layer, bs):
    _, b, m, nh, hd = cache_k.shape
    q3 = jnp.pad(q.reshape(b, nh, hd), ((0, 0), (0, SUBLANES - nh), (0, 0)))
    qb = pl.BlockSpec((bs, SUBLANES, hd), lambda i: (i, 0, 0))
    cb = pl.BlockSpec((None, bs, m, nh, hd), lambda i: (layer, i, 0, 0, 0))
    out = pl.pallas_call(
        functools.partial(_mem_sample_kernel, bs=bs), grid=(b // bs,), in_specs=[qb, cb, cb], out_specs=qb,
        out_shape=jax.ShapeDtypeStruct((b, SUBLANES, hd), F32),
        compiler_params=_cparams(("parallel",)), name="mem_sample")(q3, cache_k, cache_v)
    return out[:, :nh].reshape(b, D)


_PAIRS = ((0, 1), (0, 2), (0, 3), (1, 2), (1, 3), (2, 3))


def _router_kernel(x_ref, rw_ref, rb_ref, bucket_ref, rank_ref, cnt_ref, base_ref, *, tm):
    i = pl.program_id(0)

    @pl.when(i == 0)
    def _():
        base_ref[...] = jnp.zeros_like(base_ref)

    logits = _dot_nt(rw_ref[...], x_ref[...].astype(BF16))
    e = jnp.exp(logits - jnp.max(logits, axis=0, keepdims=True))
    sel = e / jnp.sum(e, axis=0, keepdims=True) + rb_ref[...]
    s = [sel[j:j + 1, :] for j in range(N_EXPERTS)]
    neg = jnp.float32(-jnp.inf)

    best = jnp.zeros((1, tm), jnp.int32)
    best_score = None
    for gi in range(N_GROUPS):
        s0, s1, s2, s3 = s[4 * gi:4 * gi + 4]
        hi01, lo01, hi23, lo23 = jnp.maximum(s0, s1), jnp.minimum(s0, s1), jnp.maximum(s2, s3), jnp.minimum(s2, s3)
        score = jnp.maximum(hi01, hi23) + jnp.maximum(jnp.minimum(hi01, hi23), jnp.maximum(lo01, lo23))
        if gi == 0:
            best_score = score
        else:
            take = score > best_score
            best = jnp.where(take, gi, best)
            best_score = jnp.where(take, score, best_score)
    vals = []
    for j in range(EXPERTS_PER_GROUP):
        vj = s[j]
        for gi in range(1, N_GROUPS):
            vj = jnp.where(best == gi, s[4 * gi + j], vj)
        vals.append(vj)

    def argmax4(v):
        idx, mx = jnp.zeros((1, tm), jnp.int32), v[0]
        for j in range(1, EXPERTS_PER_GROUP):
            take = v[j] > mx
            idx = jnp.where(take, j, idx)
            mx = jnp.where(take, v[j], mx)
        return idx

    i1 = argmax4(vals)
    i2 = argmax4([jnp.where(i1 == j, neg, vals[j]) for j in range(EXPERTS_PER_GROUP)])
    lo, hi = jnp.minimum(i1, i2), jnp.maximum(i1, i2)
    pair = jnp.zeros((1, tm), jnp.int32)
    for pi, (pa, pb) in enumerate(_PAIRS):
        pair = jnp.where(jnp.logical_and(lo == pa, hi == pb), pi, pair)
    bucket = best * len(_PAIRS) + pair
    bucket_ref[0] = bucket

    onehot = (lax.broadcasted_iota(jnp.int32, (BUCKET_ROWS, tm), 0) == bucket).astype(F32)
    upper = (lax.broadcasted_iota(jnp.int32, (tm, tm), 0) <= lax.broadcasted_iota(jnp.int32, (tm, tm), 1)).astype(BF16)
    cum = _dot(onehot.astype(BF16), upper)
    base = base_ref[...]
    rank = jnp.sum(onehot * (cum + base), axis=0, keepdims=True) - 1.0
    rank_ref[0] = rank.astype(jnp.int32)
    base = base + jnp.sum(onehot, axis=1, keepdims=True)
    base_ref[...] = base
    cnt_ref[...] = jnp.broadcast_to(base, cnt_ref.shape)


def _router(x, rw_t, rb, tm):
    t = x.shape[0]
    nb = t // tm
    ib = pl.BlockSpec((1, 1, tm), lambda i: (i, 0, 0))
    bucket, rank, cnt = pl.pallas_call(
        functools.partial(_router_kernel, tm=tm), grid=(nb,),
        in_specs=[pl.BlockSpec((tm, D), lambda i: (i, 0)), _full(rw_t.shape), _full(rb.shape)],
        out_specs=[ib, ib, _full((BUCKET_ROWS, LANES))],
        out_shape=[jax.ShapeDtypeStruct((nb, 1, tm), jnp.int32), jax.ShapeDtypeStruct((nb, 1, tm), jnp.int32),
                   jax.ShapeDtypeStruct((BUCKET_ROWS, LANES), F32)],
        scratch_shapes=[pltpu.VMEM((BUCKET_ROWS, 1), F32)],
        compiler_params=_cparams(("arbitrary",)), name="router")(x, rw_t, rb)
    return bucket.reshape(t), rank.reshape(t), cnt[:N_BUCKETS, 0].astype(jnp.int32)


_ROW_GROUP = 8


def _row_copies(idx_ref, base, src_hbm, dst, sem, n, wait):
    def body(j, carry):
        for k in range(_ROW_GROUP):
            r = j * _ROW_GROUP + k
            cp = pltpu.make_async_copy(src_hbm.at[pl.ds(idx_ref[base + r], 1)], dst.at[pl.ds(r, 1)], sem)
            if wait:
                cp.wait()
            else:
                cp.start(priority=k % 2)
        return carry

    lax.fori_loop(0, n // _ROW_GROUP, body, 0)


def _ffn_kernel(src_ref, lo_ref, hi_ref, nrow_ref, used_ref, x_hbm, rw_ref, g0_ref, u0_ref, d0_ref, g1_ref, u1_ref,
                d1_ref, o_ref, xbuf, sem, *, blk):
    i = pl.program_id(0)
    used = used_ref[0]
    nslot = xbuf.shape[0]
    slot = i % nslot

    def start_block(b):
        s = b % nslot
        _row_copies(src_ref, b * blk, x_hbm, xbuf.at[s], sem.at[s], nrow_ref[b], False)

    @pl.when(i == 0)
    def _():
        xbuf[...] = jnp.zeros_like(xbuf)
        for b in range(nslot - 1):
            @pl.when(b < used)
            def _():
                start_block(b)

    @pl.when(i + nslot - 1 < used)
    def _():
        start_block(i + nslot - 1)

    @pl.when(i < used)
    def _():
        _row_copies(src_ref, i * blk, x_hbm, xbuf.at[slot], sem.at[slot], nrow_ref[i], True)
        xb = xbuf[slot].astype(BF16)
        logits = _dot(xb, rw_ref[...])
        lane = lax.broadcasted_iota(jnp.int32, logits.shape, 1)
        l_lo = jnp.sum(jnp.where(lane == lo_ref[i], logits, 0.0), axis=-1, keepdims=True)
        l_hi = jnp.sum(jnp.where(lane == hi_ref[i], logits, 0.0), axis=-1, keepdims=True)
        w_lo = _sigmoid(l_lo - l_hi)

        def expert(g_ref, u_ref, d_ref):
            gate = _dot(xb, g_ref[0])
            act = gate * _sigmoid(gate) * _dot(xb, u_ref[0])
            return _dot(act.astype(BF16), d_ref[0])

        y_lo = expert(g0_ref, u0_ref, d0_ref)
        y_hi = expert(g1_ref, u1_ref, d1_ref)
        o_ref[...] = w_lo * y_lo + (1.0 - w_lo) * y_hi

    @pl.when(i >= used)
    def _():
        o_ref[...] = jnp.zeros_like(o_ref)


def _ffn(x, src, blk_lo, blk_hi, blk_rows, n_used, rw, w_gate, w_up, w_down, layer, blk):
    rows = src.shape[0]
    nblk = rows // blk
    wg = lambda sel: pl.BlockSpec((None, 1, D, EXPERT_FF),
                                  lambda i, s, lo, hi, nr, used: (layer, (lo, hi)[sel][i], 0, 0))
    wd = lambda sel: pl.BlockSpec((None, 1, EXPERT_FF, D),
                                  lambda i, s, lo, hi, nr, used: (layer, (lo, hi)[sel][i], 0, 0))
    return pl.pallas_call(
        functools.partial(_ffn_kernel, blk=blk),
        grid_spec=pltpu.PrefetchScalarGridSpec(
            num_scalar_prefetch=5, grid=(nblk,),
            in_specs=[pl.BlockSpec(memory_space=pl.ANY),
                      pl.BlockSpec(rw.shape, lambda i, s, lo, hi, nr, used: (0, 0)),
                      wg(0), wg(0), wd(0), wg(1), wg(1), wd(1)],
            out_specs=pl.BlockSpec((blk, D), lambda i, s, lo, hi, nr, used: (i, 0)),
            scratch_shapes=[pltpu.VMEM((3, blk, D), F32), pltpu.SemaphoreType.DMA((3,))]),
        out_shape=jax.ShapeDtypeStruct((rows, D), F32),
        compiler_params=_cparams(("arbitrary",)), name="moe_ffn")(
            src, blk_lo, blk_hi, blk_rows, n_used, x, rw, w_gate, w_up, w_down, w_gate, w_up, w_down)


def _combine_ln_kernel(dest_ref, x_ref, y_hbm, g_ref, b_ref, o_ref, ybuf, sem, *, tm):
    i = pl.program_id(0)
    slot = i % 2

    @pl.when(i == 0)
    def _():
        _row_copies(dest_ref, 0, y_hbm, ybuf.at[0], sem.at[0], tm, False)

    @pl.when(i + 1 < pl.num_programs(0))
    def _():
        _row_copies(dest_ref, (i + 1) * tm, y_hbm, ybuf.at[1 - slot], sem.at[1 - slot], tm, False)

    _row_copies(dest_ref, i * tm, y_hbm, ybuf.at[slot], sem.at[slot], tm, True)
    o_ref[...] = _ln(ALPHA * x_ref[...] + ybuf[slot], g_ref[...], b_ref[...])


def _combine_ln(x, y_rows, dest, g, b, tm):
    t = x.shape[0]
    rowb = pl.BlockSpec((tm, D), lambda i, d: (i, 0))
    vec = pl.BlockSpec((1, D), lambda i, d: (0, 0))
    return pl.pallas_call(
        functools.partial(_combine_ln_kernel, tm=tm),
        grid_spec=pltpu.PrefetchScalarGridSpec(
            num_scalar_prefetch=1, grid=(t // tm,),
            in_specs=[rowb, pl.BlockSpec(memory_space=pl.ANY), vec, vec], out_specs=rowb,
            scratch_shapes=[pltpu.VMEM((2, tm, D), F32), pltpu.SemaphoreType.DMA((2,))]),
        out_shape=jax.ShapeDtypeStruct((t, D), F32),
        compiler_params=_cparams(("arbitrary",)), name="moe_combine_ln")(dest, x, y_rows, g, b)


def _invert_rows_kernel(dest_ref, src_ref):
    def clear(r, carry):
        src_ref[r] = 0
        return carry

    def put(tok, carry):
        src_ref[dest_ref[tok]] = tok
        return carry

    lax.fori_loop(0, src_ref.shape[0], clear, 0, unroll=8)
    lax.fori_loop(0, dest_ref.shape[0], put, 0, unroll=8)


def _invert_rows(dest, rows):
    smem = pl.BlockSpec(memory_space=pltpu.SMEM)
    return pl.pallas_call(
        _invert_rows_kernel, in_specs=[smem], out_specs=smem,
        out_shape=jax.ShapeDtypeStruct((rows,), jnp.int32), name="invert_rows")(dest)


def _moe_ln(x, rw_t, rb, rw_pad, w_gate, w_up, w_down, layer, g, b, tm_router, blk, tm_comb):
    t = x.shape[0]
    bucket, rank, counts = _router(x, rw_t, rb, tm_router)
    padded = (counts + blk - 1) // blk * blk
    ends = jnp.cumsum(padded)
    dest = ((ends - padded)[bucket] + rank).astype(jnp.int32)
    nblk = t // blk + N_BUCKETS
    src = _invert_rows(dest, nblk * blk)
    blk_bucket = jnp.minimum(jnp.searchsorted(ends, jnp.arange(nblk) * blk, side='right'), N_BUCKETS - 1)
    pair_lo = jnp.array([p[0] for p in _PAIRS], jnp.int32)
    pair_hi = jnp.array([p[1] for p in _PAIRS], jnp.int32)
    grp, pr = blk_bucket // len(_PAIRS), blk_bucket % len(_PAIRS)
    blk_lo = (grp * EXPERTS_PER_GROUP + pair_lo[pr]).astype(jnp.int32)
    blk_hi = (grp * EXPERTS_PER_GROUP + pair_hi[pr]).astype(jnp.int32)
    n_used = (ends[-1:] // blk).astype(jnp.int32)
    bucket_end = (ends - padded + counts)[blk_bucket]
    blk_rows = jnp.clip(bucket_end - jnp.arange(nblk) * blk, 0, blk)
    blk_rows = ((blk_rows + _ROW_GROUP - 1) // _ROW_GROUP * _ROW_GROUP).astype(jnp.int32)
    y_rows = _ffn(x, src, blk_lo, blk_hi, blk_rows, n_used, rw_pad, w_gate, w_up, w_down, layer, blk)
    return _combine_ln(x, y_rows, dest, g, b, tm_comb)


def _moe_dense_kernel(x_ref, lo_ref, hi_ref, rw_ref, wg_ref, wu_ref, wd_ref, g_ref, b_ref, o_ref, acc_ref):
    e = pl.program_id(0)

    @pl.when(e == 0)
    def _():
        acc_ref[...] = jnp.zeros_like(acc_ref)

    x = x_ref[...]
    xb = x.astype(BF16)
    logits = _dot(xb, rw_ref[...])
    lane = lax.broadcasted_iota(jnp.int32, logits.shape, 1)
    lo, hi = lo_ref[...], hi_ref[...]
    l_lo = jnp.sum(jnp.where(lane == lo, logits, 0.0), axis=-1, keepdims=True)
    l_hi = jnp.sum(jnp.where(lane == hi, logits, 0.0), axis=-1, keepdims=True)
    w_lo = _sigmoid(l_lo - l_hi)
    coef = jnp.where(lo == e, w_lo, 0.0) + jnp.where(hi == e, 1.0 - w_lo, 0.0)
    gate = _dot(xb, wg_ref[0])
    act = gate * _sigmoid(gate) * _dot(xb, wu_ref[0])
    acc_ref[...] += coef * _dot(act.astype(BF16), wd_ref[0])

    @pl.when(e == pl.num_programs(0) - 1)
    def _():
        o_ref[...] = _ln(ALPHA * x + acc_ref[...], g_ref[...], b_ref[...])


def _moe_ln_dense(x, rw_t, rb, rw_pad, w_gate, w_up, w_down, layer, g, b):
    t = x.shape[0]
    bucket, _, _ = _router(x, rw_t, rb, t)
    pair_lo = jnp.array([p[0] for p in _PAIRS], jnp.int32)
    pair_hi = jnp.array([p[1] for p in _PAIRS], jnp.int32)
    grp, pr = bucket // len(_PAIRS), bucket % len(_PAIRS)
    lo = (grp * EXPERTS_PER_GROUP + pair_lo[pr]).astype(jnp.int32).reshape(t, 1)
    hi = (grp * EXPERTS_PER_GROUP + pair_hi[pr]).astype(jnp.int32).reshape(t, 1)
    wg = pl.BlockSpec((None, 1, D, EXPERT_FF), lambda e: (layer, e, 0, 0))
    wd = pl.BlockSpec((None, 1, EXPERT_FF, D), lambda e: (layer, e, 0, 0))
    return pl.pallas_call(
        _moe_dense_kernel, grid=(N_EXPERTS,),
        in_specs=[_full((t, D)), _full((t, 1)), _full((t, 1)), _full(rw_pad.shape), wg, wg, wd,
                  _full((1, D)), _full((1, D))],
        out_specs=_full((t, D)), out_shape=jax.ShapeDtypeStruct((t, D), F32),
        scratch_shapes=[pltpu.VMEM((t, D), F32)],
        compiler_params=_cparams(("arbitrary",)), name="moe_dense")(x, lo, hi, rw_pad, w_gate, w_up, w_down, g, b)


def kernel(x_prompt, x_sample, cache_swa_k, cache_swa_v, state_lru_conv, state_lru_h, state_rwkv_shift, state_rwkv_wkv, cache_mem_k, cache_mem_v, mem_prompt, swa_w_qkv, swa_sinks, swa_w_o, lru_w_in, lru_b_in, lru_conv_w, lru_conv_b, lru_w_a, lru_b_a, lru_w_i, lru_b_i, lru_lambda, lru_w_o, rwkv_mu, rwkv_w_r, rwkv_w_k, rwkv_w_v, rwkv_w0, rwkv_w1, rwkv_w2, rwkv_a0, rwkv_a1, rwkv_a2, rwkv_g1, rwkv_g2, rwkv_k_k, rwkv_k_a, rwkv_r_k, rwkv_gn_g, rwkv_gn_b, rwkv_w_o, mem_w_q, mem_w_kv, mem_w_o, ln_g, ln_b, router_w, router_b, moe_w_gate, moe_w_up, moe_w_down):
    n_p, seq, _ = x_prompt.shape
    n_s, dec_seq, _ = x_sample.shape
    assert dec_seq == 1
    past_len = 8192
    xp = x_prompt.reshape(n_p * seq, D)
    xs = x_sample.reshape(n_s, D)
    row = lambda v: v.reshape(1, -1)
    bf = lambda w: w.astype(BF16)

    rw_t = bf(router_w.T)
    rb = router_b.reshape(N_EXPERTS, 1)
    rw_pad = bf(jnp.pad(router_w, ((0, 0), (0, LANES - N_EXPERTS))))
    wg, wu, wd = bf(moe_w_gate), bf(moe_w_up), bf(moe_w_down)
    mem_p = mem_prompt.reshape(n_p * mem_prompt.shape[1], D)
    m_len = mem_prompt.shape[1]

    swa_k_p, swa_v_p, swa_k_s, swa_v_s = [], [], [], []
    lru_c_p, lru_h_p, lru_c_s, lru_h_s = [], [], [], []
    rw_x_p, rw_s_p, rw_x_s, rw_s_s = [], [], [], []
    mem_k_p, mem_v_p = [], []

    for layer in range(DEPTH):
        kind, i = layer % N_MIXERS, layer // N_MIXERS
        g0, b0 = row(ln_g[layer, 0]), row(ln_b[layer, 0])
        if kind == 0:
            w_qkv, w_o = bf(swa_w_qkv[i]), bf(swa_w_o[i])
            keep = min(WINDOW, seq)
            q, k, v, kv_last = _swa_qkv(xp, w_qkv, jnp.arange(seq), n_p, 512, keep, BF16)
            o = _swa_attn_prompt(q, k, v, swa_sinks[i], n_p, 2)
            swa_k_p.append(kv_last[:, :, :KV_WIDTH].reshape(n_p, keep, SWA_KV_HEADS, HEAD_DIM))
            swa_v_p.append(kv_last[:, :, KV_WIDTH:].reshape(n_p, keep, SWA_KV_HEADS, HEAD_DIM))
            xp = _proj_ln(o, w_o, xp, g0, b0, 512)

            qs, _, _, kv_new = _swa_qkv(xs, w_qkv, jnp.full((n_s,), past_len), 1, n_s, n_s, F32)
            kn, vn = kv_new[0, :, :KV_WIDTH], kv_new[0, :, KV_WIDTH:]
            os_ = _swa_attn_sample(qs, kn, vn, cache_swa_k, cache_swa_v, i, swa_sinks[i], 8)
            wb = cache_swa_k.shape[2]
            k_all = jnp.concatenate([cache_swa_k[i], kn.reshape(n_s, 1, SWA_KV_HEADS, HEAD_DIM)], axis=1)
            v_all = jnp.concatenate([cache_swa_v[i], vn.reshape(n_s, 1, SWA_KV_HEADS, HEAD_DIM)], axis=1)
            swa_k_s.append(k_all[:, -wb:])
            swa_v_s.append(v_all[:, -wb:])
            xs = _proj_ln(os_, w_o, xs, g0, b0, n_s)
        elif kind == 1:
            wts = _lru_weights(lru_w_in[i], lru_b_in[i], lru_conv_w[i], lru_conv_b[i], lru_w_a[i], lru_b_a[i],
                               lru_w_i[i], lru_b_i[i], lru_lambda[i], lru_w_o[i])
            xp, conv_last, h_last = _lru_prompt(xp, wts, g0, b0, n_p, 256)
            lru_c_p.append(conv_last[:, SUBLANES - (CONV_W - 1):])
            lru_h_p.append(h_last[:, SUBLANES - 1])
            xs, xb_s, h_s = _lru_sample(xs, state_lru_conv[i], state_lru_h[i], wts, g0, b0)
            lru_c_s.append(jnp.concatenate([state_lru_conv[i][:, 1:], xb_s[:, None]], axis=1))
            lru_h_s.append(h_s)
        else:
            wts = (rwkv_mu[i], bf(rwkv_w_r[i]), bf(rwkv_w_k[i]), bf(rwkv_w_v[i]), row(rwkv_w0[i]), bf(rwkv_w1[i]),
                   bf(rwkv_w2[i]), row(rwkv_a0[i]), bf(rwkv_a1[i]), bf(rwkv_a2[i]), bf(rwkv_g1[i]), bf(rwkv_g2[i]))
            hp = (row(rwkv_k_k[i]), row(rwkv_k_a[i]), row(rwkv_r_k[i]), row(rwkv_gn_g[i]), row(rwkv_gn_b[i]))
            w_o = bf(rwkv_w_o[i])
            rw_x_p.append(xp.reshape(n_p, seq, D)[:, -1])
            rw_x_s.append(xs)
            r, k, v, a, ld, g = _rwkv_pre(xp, jnp.zeros((n_p, SUBLANES, D), F32), wts, n_p, 512, True, BF16)
            o, st = _wkv_prompt(r, k, v, a, ld, g, hp, n_p, 2)
            hd = RWKV_HD
            st = jnp.stack([st[:, :, :hd, :hd], st[:, :, hd:, hd:]], axis=2).reshape(n_p, RWKV_HEADS, hd, hd)
            rw_s_p.append(jnp.swapaxes(st, -1, -2))
            xp = _proj_ln(o, w_o, xp, g0, b0, 512)

            r, k, v, a, ld, g = _rwkv_pre(xs, state_rwkv_shift[i], wts, 1, n_s, False, F32)
            os_, s_new = _wkv_sample(r, k, v, a, ld, g, state_rwkv_wkv[i], hp)
            rw_s_s.append(s_new)
            xs = _proj_ln(os_, w_o, xs, g0, b0, n_s)

        g1, b1 = row(ln_g[layer, 1]), row(ln_b[layer, 1])
        w_q, w_o = bf(mem_w_q[layer]), bf(mem_w_o[layer])
        mkv = _matmul(mem_p, bf(mem_w_kv[layer]), 512)
        mk, mv = mkv[:, :D], mkv[:, D:]
        mem_k_p.append(mk.reshape(n_p, m_len, MEM_HEADS, MEM_HD))
        mem_v_p.append(mv.reshape(n_p, m_len, MEM_HEADS, MEM_HD))
        xp = _mem_attn_prompt(xp, w_q, bf(mk).reshape(n_p, m_len, D), bf(mv).reshape(n_p, m_len, D), w_o, g1, b1,
                              n_p, 512)
        qs = _matmul(xs, w_q, n_s)
        os_ = _mem_attn_sample(qs, cache_mem_k, cache_mem_v, layer, 4)
        xs = _proj_ln(os_, w_o, xs, g1, b1, n_s)

        g2, b2 = row(ln_g[layer, 2]), row(ln_b[layer, 2])
        xp = _moe_ln(xp, rw_t, rb, rw_pad, wg, wu, wd, layer, g2, b2, 512, 256, 512)
        xs = _moe_ln_dense(xs, rw_t, rb, rw_pad, wg, wu, wd, layer, g2, b2)

    return (xp.reshape(n_p, seq, D), xs.reshape(n_s, 1, D),
            jnp.stack(swa_k_p), jnp.stack(swa_v_p), jnp.stack(lru_c_p), jnp.stack(lru_h_p),
            jnp.stack(rw_x_p), jnp.stack(rw_s_p), jnp.stack(mem_k_p), jnp.stack(mem_v_p),
            jnp.stack(swa_k_s), jnp.stack(swa_v_s), jnp.stack(lru_c_s), jnp.stack(lru_h_s),
            jnp.stack(rw_x_s), jnp.stack(rw_s_s))
```

```python
import functools

import jax
import jax.numpy as jnp
from jax import lax
from jax.experimental import pallas as pl
from jax.experimental.pallas import tpu as pltpu

F32 = jnp.float32
BF16 = jnp.bfloat16

D = 1024
DEPTH = 4
N_MIXERS = 3
HEAD_DIM = 64
SWA_HEADS = D // HEAD_DIM
SWA_KV_HEADS = 4
SWA_GROUP = SWA_HEADS // SWA_KV_HEADS
Q_WIDTH = SWA_HEADS * HEAD_DIM
KV_WIDTH = SWA_KV_HEADS * HEAD_DIM
WINDOW = 128
ROT_DIM = HEAD_DIM // 4
ROPE_THETA = 500000.0
LRU_BLOCKS = 16
CONV_W = 4
LRU_C = 8.0
RWKV_HEADS = 16
RWKV_HD = 64
RWKV_GN_EPS = 64e-5
MEM_HEADS = 4
MEM_HD = D // MEM_HEADS
N_EXPERTS = 16
N_GROUPS = 4
EXPERTS_PER_GROUP = 4
EXPERT_FF = 512
LN_EPS = 1e-5
ALPHA = (2.0 * DEPTH) ** 0.25
NEG_INF = -1e30

LANES = 128
SUBLANES = 8
VMEM_LIMIT = 56 * 1024 * 1024
WKV_CHUNK = 64
N_BUCKETS = N_GROUPS * 6
BUCKET_ROWS = 32


def _cparams(sem):
    return pltpu.CompilerParams(dimension_semantics=sem, vmem_limit_bytes=VMEM_LIMIT)


def _dot(a, b):
    return jnp.dot(a, b, preferred_element_type=F32)


def _dot_nt(a, b):
    return lax.dot_general(a, b, (((1,), (1,)), ((), ())), preferred_element_type=F32)


def _dot_tn(a, b):
    return lax.dot_general(a, b, (((0,), (0,)), ((), ())), preferred_element_type=F32)


def _ln(z, g, b):
    mu = jnp.mean(z, axis=-1, keepdims=True)
    zc = z - mu
    var = jnp.mean(zc * zc, axis=-1, keepdims=True)
    return zc * lax.rsqrt(var + LN_EPS) * g + b


def _softplus(z):
    return jnp.maximum(z, 0.0) + jnp.log1p(jnp.exp(-jnp.abs(z)))


def _sigmoid(z):
    return 1.0 / (1.0 + jnp.exp(-z))


def _round_bf16(x):
    return x.astype(BF16).astype(F32)


def _full(shape):
    nd = len(shape)
    return pl.BlockSpec(shape, lambda *_: (0,) * nd)


def _mm_kernel(a_ref, w_ref, o_ref):
    o_ref[...] = _dot(a_ref[...].astype(BF16), w_ref[...]).astype(o_ref.dtype)


def _matmul(a, w, tm, out_dtype=F32):
    t, k = a.shape
    n = w.shape[1]
    return pl.pallas_call(
        _mm_kernel, grid=(t // tm,),
        in_specs=[pl.BlockSpec((tm, k), lambda i: (i, 0)), _full((k, n))],
        out_specs=pl.BlockSpec((tm, n), lambda i: (i, 0)),
        out_shape=jax.ShapeDtypeStruct((t, n), out_dtype),
        compiler_params=_cparams(("parallel",)), name="matmul")(a, w)


def _proj_ln_kernel(a_ref, w_ref, x_ref, g_ref, b_ref, o_ref):
    acc = _dot(a_ref[...].astype(BF16), w_ref[...])
    o_ref[...] = _ln(ALPHA * x_ref[...] + acc, g_ref[...], b_ref[...])


def _proj_ln(a, w, x, g, b, tm):
    t, k = a.shape
    return pl.pallas_call(
        _proj_ln_kernel, grid=(t // tm,),
        in_specs=[pl.BlockSpec((tm, k), lambda i: (i, 0)), _full((k, D)),
                  pl.BlockSpec((tm, D), lambda i: (i, 0)), _full((1, D)), _full((1, D))],
        out_specs=pl.BlockSpec((tm, D), lambda i: (i, 0)),
        out_shape=jax.ShapeDtypeStruct((t, D), F32),
        compiler_params=_cparams(("parallel",)), name="proj_ln")(a, w, x, g, b)


def _rope_tables(pos):
    half = ROT_DIM // 2
    inv_freq = ROPE_THETA ** (-jnp.arange(half, dtype=F32) / half)
    ang = pos.astype(F32)[:, None] * inv_freq
    cos, sin = jnp.cos(ang), jnp.sin(ang)
    one = jnp.ones((pos.shape[0], HEAD_DIM - ROT_DIM), F32)
    zero = jnp.zeros((pos.shape[0], HEAD_DIM - ROT_DIM), F32)
    zh = jnp.zeros_like(sin)
    c = jnp.concatenate([cos, cos, one], axis=1)
    s1 = jnp.concatenate([-sin, zh, zero], axis=1)
    s2 = jnp.concatenate([zh, sin, zero], axis=1)
    rep = LANES // HEAD_DIM
    return jnp.tile(c, (1, rep)), jnp.tile(s1, (1, rep)), jnp.tile(s2, (1, rep))


def _swa_qkv_kernel(x_ref, w_ref, c_ref, s1_ref, s2_ref, q_ref, k_ref, v_ref, kv_ref, *, tm, keep):
    acc = _dot(x_ref[...].astype(BF16), w_ref[...])
    c, s1, s2 = c_ref[...], s1_ref[...], s2_ref[...]
    half = ROT_DIM // 2
    n_q = Q_WIDTH // LANES
    n_k = KV_WIDTH // LANES
    for cg in range(n_q + n_k):
        xg = acc[:, cg * LANES:(cg + 1) * LANES]
        rot = xg * c + pltpu.roll(xg, LANES - half, 1) * s1 + pltpu.roll(xg, half, 1) * s2
        if cg < n_q:
            q_ref[:, cg * LANES:(cg + 1) * LANES] = rot.astype(q_ref.dtype)
        else:
            ck = cg - n_q
            k_ref[:, ck * LANES:(ck + 1) * LANES] = rot.astype(k_ref.dtype)
            kv_ref[0, :, ck * LANES:(ck + 1) * LANES] = rot[tm - keep:, :]
    v = acc[:, Q_WIDTH + KV_WIDTH:]
    v_ref[...] = v.astype(v_ref.dtype)
    kv_ref[0, :, KV_WIDTH:] = v[tm - keep:, :]


def _swa_qkv(x, w_qkv, pos, n_seq, tm, keep, qdtype):
    t = x.shape[0]
    s = t // n_seq
    nb = s // tm
    c, s1, s2 = _rope_tables(pos)
    row = lambda n, i: (n * nb + i, 0)
    tab = pl.BlockSpec((tm, LANES), lambda n, i: (i, 0))
    kern = functools.partial(_swa_qkv_kernel, tm=tm, keep=keep)
    return pl.pallas_call(
        kern, grid=(n_seq, nb),
        in_specs=[pl.BlockSpec((tm, D), row), _full((D, Q_WIDTH + 2 * KV_WIDTH)), tab, tab, tab],
        out_specs=[pl.BlockSpec((tm, Q_WIDTH), row), pl.BlockSpec((tm, KV_WIDTH), row),
                   pl.BlockSpec((tm, KV_WIDTH), row),
                   pl.BlockSpec((1, keep, 2 * KV_WIDTH), lambda n, i: (n, 0, 0))],
        out_shape=[jax.ShapeDtypeStruct((t, Q_WIDTH), qdtype), jax.ShapeDtypeStruct((t, KV_WIDTH), qdtype),
                   jax.ShapeDtypeStruct((t, KV_WIDTH), qdtype),
                   jax.ShapeDtypeStruct((n_seq, keep, 2 * KV_WIDTH), F32)],
        compiler_params=_cparams(("parallel", "arbitrary")), name="swa_qkv")(x, w_qkv, c, s1, s2)


def _swa_attn_kernel(sink_ref, q_ref, kp_ref, kc_ref, vp_ref, vc_ref, o_ref, *, nq):
    j = pl.program_id(1)
    w, grp = WINDOW, SWA_GROUP
    r = lax.broadcasted_iota(jnp.int32, (grp * w, 2 * w), 0) % w
    c = lax.broadcasted_iota(jnp.int32, (grp * w, 2 * w), 1)
    in_prev = jnp.logical_and(c < w, c > r)
    in_cur = jnp.logical_and(c >= w, (c - w) <= r)
    ok_inner = jnp.logical_or(in_prev, in_cur)
    ok_first = jnp.logical_or(jnp.logical_and(in_prev, j > 0), in_cur)
    scale = HEAD_DIM ** -0.5
    combos = [(u, h) for u in range(nq) for h in range(SWA_KV_HEADS)]
    kcat, vcat, q4, sink, ok = [], [], [], [], []
    for u, h in combos:
        sl = slice(h * HEAD_DIM, (h + 1) * HEAD_DIM)
        rows = slice(u * w, (u + 1) * w)
        before = slice((u - 1) * w, u * w)
        k_prev = kp_ref[:, sl] if u == 0 else kc_ref[before, sl]
        v_prev = vp_ref[:, sl] if u == 0 else vc_ref[before, sl]
        kcat.append(jnp.concatenate([k_prev, kc_ref[rows, sl]], axis=0))
        vcat.append(jnp.concatenate([v_prev, vc_ref[rows, sl]], axis=0))
        heads = [h * grp + g for g in range(grp)]
        q4.append(jnp.concatenate([q_ref[rows, hq * HEAD_DIM:(hq + 1) * HEAD_DIM] for hq in heads], axis=0))
        sink.append(jnp.concatenate([jnp.full((w, 1), sink_ref[hq], F32) for hq in heads], axis=0))
        ok.append(ok_first if u == 0 else ok_inner)
    n = range(len(combos))
    s = [jnp.where(ok[i], _dot_nt(q4[i], kcat[i]) * scale, NEG_INF) for i in n]
    m = [jnp.maximum(jnp.max(s[i], axis=-1, keepdims=True), sink[i]) for i in n]
    p = [jnp.exp(s[i] - m[i]) for i in n]
    den = [jnp.sum(p[i], axis=-1, keepdims=True) + jnp.exp(sink[i] - m[i]) for i in n]
    o = [_dot((p[i] / den[i]).astype(BF16), vcat[i]) for i in n]
    for i, (u, h) in enumerate(combos):
        for g in range(grp):
            hq = h * grp + g
            o_ref[u * w:(u + 1) * w, hq * HEAD_DIM:(hq + 1) * HEAD_DIM] = o[i][g * w:(g + 1) * w].astype(o_ref.dtype)


def _swa_attn_prompt(q, k, v, sinks, n_seq, nq):
    t = q.shape[0]
    nb = t // n_seq // WINDOW
    ns = nb // nq
    cur = lambda n, j: (n * ns + j, 0)
    prev = lambda n, j: (n * nb + jnp.maximum(j * nq - 1, 0), 0)
    return pl.pallas_call(
        functools.partial(_swa_attn_kernel, nq=nq), grid=(n_seq, ns),
        in_specs=[pl.BlockSpec(memory_space=pltpu.SMEM), pl.BlockSpec((nq * WINDOW, Q_WIDTH), cur),
                  pl.BlockSpec((WINDOW, KV_WIDTH), prev), pl.BlockSpec((nq * WINDOW, KV_WIDTH), cur),
                  pl.BlockSpec((WINDOW, KV_WIDTH), prev), pl.BlockSpec((nq * WINDOW, KV_WIDTH), cur)],
        out_specs=pl.BlockSpec((nq * WINDOW, Q_WIDTH), cur),
        out_shape=jax.ShapeDtypeStruct((t, Q_WIDTH), BF16),
        compiler_params=_cparams(("parallel", "arbitrary")), name="swa_attn")(sinks, q, k, k, v, v)


def _swa_sample_kernel(sink_ref, qbd_ref, q_ref, kn_ref, vnbd_ref, ckt_ref, cvt_ref, o_ref, *, bs):
    nkv, hd, wb = ckt_ref.shape[1:]
    nq = q_ref.shape[1]
    npad = kn_ref.shape[1]
    key = lax.broadcasted_iota(jnp.int32, (nq, wb), 1)
    valid = (wb - key) < WINDOW
    own_new = (lax.broadcasted_iota(jnp.int32, (nq, npad), 1)
               == lax.broadcasted_iota(jnp.int32, (nq, npad), 0) // SWA_GROUP)
    sink = sink_ref[...]
    scale = HEAD_DIM ** -0.5
    nb = range(bs)
    s = [jnp.where(valid, _dot(qbd_ref[b].astype(BF16), ckt_ref[b].reshape(nkv * hd, wb).astype(BF16)) * scale, NEG_INF)
         for b in nb]
    sn = [jnp.where(own_new, _dot_nt(q_ref[b].astype(BF16), kn_ref[b].astype(BF16)) * scale, NEG_INF) for b in nb]
    m = [jnp.maximum(jnp.maximum(jnp.max(s[b], axis=-1, keepdims=True), jnp.max(sn[b], axis=-1, keepdims=True)), sink)
         for b in nb]
    p = [jnp.where(valid, jnp.exp(s[b] - m[b]), 0.0) for b in nb]
    pn = [jnp.where(own_new, jnp.exp(sn[b] - m[b]), 0.0) for b in nb]
    den = [jnp.sum(p[b], axis=-1, keepdims=True) + jnp.sum(pn[b], axis=-1, keepdims=True) + jnp.exp(sink - m[b])
           for b in nb]
    for b in nb:
        o_ref[b] = (_dot_nt((p[b] / den[b]).astype(BF16), cvt_ref[b].reshape(nkv * hd, wb).astype(BF16))
                    + _dot((pn[b] / den[b]).astype(BF16), vnbd_ref[b].astype(BF16)))


def _swa_attn_sample(q, kn, vn, cache_k, cache_v, layer, sinks, bs):
    _, b, wb, nkv, hd = cache_k.shape
    grp = SWA_HEADS // nkv
    eye = jnp.eye(nkv, dtype=q.dtype)
    qbd = jnp.einsum('bhgd,hk->bhgkd', q.reshape(b, nkv, grp, hd), eye).reshape(b, SWA_HEADS, nkv * hd)
    pad = lambda z: jnp.pad(z, ((0, 0), (0, SUBLANES - nkv), (0, 0)))
    vnbd = pad(jnp.einsum('bhd,hk->bhkd', vn.reshape(b, nkv, hd), eye).reshape(b, nkv, nkv * hd))
    blk = lambda r, w: pl.BlockSpec((bs, r, w), lambda i: (i, 0, 0))
    cblk = pl.BlockSpec((None, bs, nkv, hd, wb), lambda i: (layer, i, 0, 0, 0))
    out = pl.pallas_call(
        functools.partial(_swa_sample_kernel, bs=bs), grid=(b // bs,),
        in_specs=[_full((SWA_HEADS, 1)), blk(SWA_HEADS, nkv * hd), blk(SWA_HEADS, hd), blk(SUBLANES, hd),
                  blk(SUBLANES, nkv * hd), cblk, cblk],
        out_specs=blk(SWA_HEADS, nkv * hd), out_shape=jax.ShapeDtypeStruct((b, SWA_HEADS, nkv * hd), F32),
        compiler_params=_cparams(("parallel",)), name="swa_sample")(
            sinks.reshape(SWA_HEADS, 1), qbd, q.reshape(b, SWA_HEADS, hd), pad(kn.reshape(b, nkv, hd)), vnbd,
            jnp.transpose(cache_k, (0, 1, 3, 4, 2)), jnp.transpose(cache_v, (0, 1, 3, 4, 2)))
    o5 = out.reshape(b, nkv, grp, nkv, hd)
    return jnp.stack([o5[:, h, :, h, :] for h in range(nkv)], axis=1).reshape(b, Q_WIDTH)


def _gelu_tanh(x):
    return 0.5 * x * (1.0 + jnp.tanh(0.7978845608028654 * (x + 0.044715 * x * x * x)))


def _lru_gates(xc, wa_ref, ba, wi_ref, bi, lam):
    xcb = xc.astype(BF16)
    gw = wa_ref.shape[1]
    ra, ia = [], []
    for gi in range(wa_ref.shape[0]):
        xs = xcb[:, gi * gw:(gi + 1) * gw]
        ra.append(_dot(xs, wa_ref[gi]))
        ia.append(_dot(xs, wi_ref[gi]))
    r = _sigmoid(jnp.concatenate(ra, axis=-1) + ba)
    ig = _sigmoid(jnp.concatenate(ia, axis=-1) + bi)
    log_a = -LRU_C * r * _softplus(-lam)
    a = jnp.exp(log_a)
    b = jnp.sqrt(-jnp.tanh(log_a) * (a * a + 1.0)) * (ig * xc)
    return a, b


def _shift_rows(ext, s, tm):
    return pltpu.roll(ext, s, 0)[SUBLANES:SUBLANES + tm]


def _lru_prompt_kernel(x_ref, win_ref, bin_ref, cw_ref, cb_ref, wa_ref, ba_ref, wi_ref, bi_ref, lam_ref,
                       wo_ref, g_ref, b_ref, o_ref, conv_ref, hl_ref, cx_ref, ch_ref, *, tm):
    i = pl.program_id(1)

    @pl.when(i == 0)
    def _():
        cx_ref[...] = jnp.zeros_like(cx_ref)
        ch_ref[...] = jnp.zeros_like(ch_ref)

    x = x_ref[...]
    xy = _dot(x.astype(BF16), win_ref[...]) + bin_ref[...]
    xb = xy[:, :D]
    y_gate = _gelu_tanh(xy[:, D:])
    ext = jnp.concatenate([cx_ref[...], xb], axis=0)
    cw = cw_ref[...]
    xc = cb_ref[...] + xb * cw[CONV_W - 1:CONV_W]
    for s in range(1, CONV_W):
        xc = xc + _shift_rows(ext, s, tm) * cw[CONV_W - 1 - s:CONV_W - s]
    cx_ref[...] = xb[tm - SUBLANES:]
    conv_ref[0] = xb[tm - SUBLANES:]

    a, b = _lru_gates(xc, wa_ref, ba_ref[...], wi_ref, bi_ref[...], lam_ref[...])
    sub = lax.broadcasted_iota(jnp.int32, (tm, 1), 0) % SUBLANES
    s = 1
    while s < SUBLANES:
        keep = sub >= s
        a_sh = jnp.where(keep, pltpu.roll(a, s, 0), 1.0)
        b_sh = jnp.where(keep, pltpu.roll(b, s, 0), 0.0)
        b = a * b_sh + b
        a = a * a_sh
        s *= 2
    carry = ch_ref[SUBLANES - 1:SUBLANES, :]
    groups = []
    for gi in range(tm // SUBLANES):
        rows = slice(gi * SUBLANES, (gi + 1) * SUBLANES)
        hg = a[rows] * carry + b[rows]
        groups.append(hg)
        carry = hg[SUBLANES - 1:SUBLANES]
    h = jnp.concatenate(groups, axis=0)
    ch_ref[...] = h[tm - SUBLANES:]
    hl_ref[0] = h[tm - SUBLANES:]
    acc = _dot((h * y_gate).astype(BF16), wo_ref[...])
    o_ref[...] = _ln(ALPHA * x + acc, g_ref[...], b_ref[...])


def _lru_weights(w_in, b_in, conv_w, conv_b, w_a, b_a, w_i, b_i, lam, w_o):
    gsz = 4
    ng = LRU_BLOCKS // gsz
    bw = D // LRU_BLOCKS

    def grouped(w):
        w4 = w.reshape(ng, gsz, bw, bw)
        return jnp.einsum('gaij,ab->gaibj', w4, jnp.eye(gsz, dtype=w.dtype)).reshape(ng, gsz * bw, gsz * bw).astype(BF16)

    row = lambda v: v.reshape(1, -1)
    return (w_in.astype(BF16), row(b_in), conv_w, row(conv_b), grouped(w_a), row(b_a), grouped(w_i), row(b_i),
            row(lam), w_o.astype(BF16))


def _lru_prompt(x, wts, g, b, n_seq, tm):
    t = x.shape[0]
    nb = t // n_seq // tm
    row = lambda n, i: (n * nb + i, 0)
    last = pl.BlockSpec((1, SUBLANES, D), lambda n, i: (n, 0, 0))
    w_in, b_in, cw, cb, wa, ba, wi, bi, lam, wo = wts
    return pl.pallas_call(
        functools.partial(_lru_prompt_kernel, tm=tm), grid=(n_seq, nb),
        in_specs=[pl.BlockSpec((tm, D), row), _full(w_in.shape), _full(b_in.shape), _full(cw.shape), _full(cb.shape),
                  _full(wa.shape), _full(ba.shape), _full(wi.shape), _full(bi.shape), _full(lam.shape),
                  _full(wo.shape), _full((1, D)), _full((1, D))],
        out_specs=[pl.BlockSpec((tm, D), row), last, last],
        out_shape=[jax.ShapeDtypeStruct((t, D), F32), jax.ShapeDtypeStruct((n_seq, SUBLANES, D), F32),
                   jax.ShapeDtypeStruct((n_seq, SUBLANES, D), F32)],
        scratch_shapes=[pltpu.VMEM((SUBLANES, D), F32), pltpu.VMEM((SUBLANES, D), F32)],
        compiler_params=_cparams(("parallel", "arbitrary")), name="lru_prompt")(x, *wts, g, b)


def _lru_sample_kernel(x_ref, c0_ref, c1_ref, c2_ref, h0_ref, win_ref, bin_ref, cw_ref, cb_ref, wa_ref, ba_ref,
                       wi_ref, bi_ref, lam_ref, wo_ref, g_ref, b_ref, o_ref, xb_ref, h_ref):
    x = x_ref[...]
    xy = _dot(x.astype(BF16), win_ref[...]) + bin_ref[...]
    xb = xy[:, :D]
    y_gate = _gelu_tanh(xy[:, D:])
    cw = cw_ref[...]
    xc = (cb_ref[...] + c0_ref[...] * cw[0:1] + c1_ref[...] * cw[1:2] + c2_ref[...] * cw[2:3] + xb * cw[3:4])
    a, b = _lru_gates(xc, wa_ref, ba_ref[...], wi_ref, bi_ref[...], lam_ref[...])
    h = a * h0_ref[...] + b
    xb_ref[...] = xb
    h_ref[...] = h
    acc = _dot((h * y_gate).astype(BF16), wo_ref[...])
    o_ref[...] = _ln(ALPHA * x + acc, g_ref[...], b_ref[...])


def _lru_sample(x, conv_state, h0, wts, g, b):
    t = x.shape[0]
    args = (x, conv_state[:, 0], conv_state[:, 1], conv_state[:, 2], h0, *wts, g, b)
    sd = jax.ShapeDtypeStruct((t, D), F32)
    return pl.pallas_call(
        _lru_sample_kernel, grid=(1,),
        in_specs=[_full(a.shape) for a in args],
        out_specs=[_full((t, D))] * 3, out_shape=[sd, sd, sd],
        compiler_params=_cparams(("arbitrary",)), name="lru_sample")(*args)


def _rwkv_pre_kernel(x_ref, xp_ref, mu_ref, wr_ref, wk_ref, wv_ref, w0_ref, w1_ref, w2_ref, a0_ref, a1_ref, a2_ref,
                     g1_ref, g2_ref, r_ref, k_ref, v_ref, a_ref, ld_ref, g_ref, *scratch, tm, seq):
    x = x_ref[...]
    if seq:
        cx_ref, = scratch
        i = pl.program_id(1)

        @pl.when(i == 0)
        def _():
            cx_ref[...] = xp_ref[0]

        x_prev = _shift_rows(jnp.concatenate([cx_ref[...], x], axis=0), 1, tm)
        cx_ref[...] = x[tm - SUBLANES:]
    else:
        x_prev = xp_ref[...]
    xx = x_prev - x
    mu = mu_ref[...]
    mix = lambda j: (x + xx * mu[j:j + 1]).astype(BF16)
    r_ref[...] = _dot(mix(0), wr_ref[...]).astype(r_ref.dtype)
    wl = _dot(jnp.tanh(_dot(mix(1), w1_ref[...])).astype(BF16), w2_ref[...])
    w = -_softplus(-(w0_ref[...] + wl)) - 0.5
    ld_ref[...] = -jnp.exp(w)
    k_ref[...] = _dot(mix(2), wk_ref[...]).astype(k_ref.dtype)
    v_ref[...] = _dot(mix(3), wv_ref[...]).astype(v_ref.dtype)
    al = _dot(_dot(mix(4), a1_ref[...]).astype(BF16), a2_ref[...])
    a_ref[...] = _sigmoid(a0_ref[...] + al).astype(a_ref.dtype)
    g_ref[...] = _dot(_sigmoid(_dot(mix(5), g1_ref[...])).astype(BF16), g2_ref[...]).astype(g_ref.dtype)


def _rwkv_pre(x, x_prev, wts, n_seq, tm, seq, dtype):
    t = x.shape[0]
    nb = t // n_seq // tm
    row = lambda n, i: (n * nb + i, 0)
    xp_spec = pl.BlockSpec((1, SUBLANES, D), lambda n, i: (n, 0, 0)) if seq else pl.BlockSpec((tm, D), row)
    sd = lambda dt: jax.ShapeDtypeStruct((t, D), dt)
    blk = pl.BlockSpec((tm, D), row)
    return pl.pallas_call(
        functools.partial(_rwkv_pre_kernel, tm=tm, seq=seq), grid=(n_seq, nb),
        in_specs=[blk, xp_spec] + [_full(w.shape) for w in wts],
        out_specs=[blk] * 6,
        out_shape=[sd(dtype), sd(dtype), sd(dtype), sd(dtype), sd(F32), sd(dtype)],
        scratch_shapes=[pltpu.VMEM((SUBLANES, D), F32)] if seq else [],
        compiler_params=_cparams(("parallel", "arbitrary")), name="rwkv_pre")(x, x_prev, *wts)


def _seg_sum(x, first):
    s0 = jnp.sum(jnp.where(first, x, 0.0), axis=-1, keepdims=True)
    s1 = jnp.sum(jnp.where(first, 0.0, x), axis=-1, keepdims=True)
    return jnp.where(first, s0, s1)


def _wkv_kernel(r_ref, k_ref, v_ref, a_ref, ld_ref, g_ref, kk_ref, ka_ref, rk_ref, gg_ref, gb_ref,
                o_ref, s_ref, st_ref):
    c = pl.program_id(1)
    L = WKV_CHUNK
    P2 = 2 * L
    nch = r_ref.shape[0] // L

    @pl.when(c == 0)
    def _():
        st_ref[...] = jnp.zeros_like(st_ref)

    ld_all = ld_ref[...]
    tri = (lax.broadcasted_iota(jnp.int32, (L, L), 0) >= lax.broadcasted_iota(jnp.int32, (L, L), 1)).astype(BF16)
    hi = ld_all.astype(BF16)
    r1 = ld_all - hi.astype(F32)
    mid = r1.astype(BF16)
    lo = (r1 - mid.astype(F32)).astype(BF16)
    chunk_rows = [slice(ci * L, (ci + 1) * L) for ci in range(nch)]
    cum_ch = [_dot(tri, hi[rw]) + _dot(tri, mid[rw]) + _dot(tri, lo[rw]) for rw in chunk_rows]

    lane = lax.broadcasted_iota(jnp.int32, (1, LANES), 1)
    first = lane < RWKV_HD
    ri = lax.broadcasted_iota(jnp.int32, (P2, P2), 0)
    ci_ = lax.broadcasted_iota(jnp.int32, (P2, P2), 1)
    same_head = (ri // L) == (ci_ // L)
    rt, ct = ri % L, ci_ % L
    strict = jnp.logical_and(same_head, rt > ct)
    incl = jnp.logical_and(same_head, rt >= ct)
    eye = ri == ci_

    def stack(xv):
        return jnp.concatenate([jnp.where(first, xv, 0.0), jnp.where(first, 0.0, xv)], axis=0).astype(BF16)

    npair = RWKV_HEADS // 2
    combos = [(ci, p) for ci in range(nch) for p in range(npair)]
    n = range(len(combos))
    sls = [slice(p * LANES, (p + 1) * LANES) for p in range(npair)]
    ws, us, ks, rs, ul, kl, vs, g_l, bonus = ([] for _ in range(9))
    for ci, p in combos:
        rw, sl = chunk_rows[ci], sls[p]
        rp, kp, vp, ap = (ref[rw, sl].astype(F32) for ref in (r_ref, k_ref, v_ref, a_ref))
        ldp, cum = ld_all[rw, sl], cum_ch[ci][:, sl]
        kk = kp * kk_ref[:, sl]
        kk = kk / jnp.maximum(jnp.sqrt(_seg_sum(kk * kk, first)), 1e-12)
        kmod = kp * (1.0 + (ap - 1.0) * ka_ref[:, sl])
        bp = kk * ap
        cum_l = cum[L - 1:L, :]
        g_inv = jnp.exp(-cum)
        g_to_end = jnp.exp(cum_l - cum)
        ws.append(stack(kk * jnp.exp(cum - ldp)))
        us.append(stack(bp * g_inv))
        ks.append(stack(kmod * g_inv))
        rs.append(stack(rp * jnp.exp(cum)))
        ul.append(stack(bp * g_to_end))
        kl.append(stack(kmod * g_to_end))
        vs.append(stack(vp))
        g_l.append(jnp.exp(cum_l))
        bonus.append(_seg_sum(rp * kmod * rk_ref[:, sl], first) * vp)

    gram = [_dot_nt(jnp.concatenate([ws[q], rs[q]], axis=0), jnp.concatenate([us[q], ks[q]], axis=0)) for q in n]
    n_mat = [jnp.where(strict, gram[q][:P2, :P2], 0.0) for q in n]
    m_mat = [jnp.where(strict, gram[q][:P2, P2:], 0.0).astype(BF16) for q in n]
    nr_mat = [jnp.where(incl, gram[q][P2:, :P2], 0.0).astype(BF16) for q in n]
    mr_mat = [jnp.where(incl, gram[q][P2:, P2:], 0.0).astype(BF16) for q in n]

    def level_mask(sz):
        sub = jnp.logical_and((rt // sz) % 2 == 1, (ct // sz) % 2 == 0)
        return jnp.logical_and(jnp.logical_and(sub, (rt // (2 * sz)) == (ct // (2 * sz))), same_head)

    x_inv = [jnp.where(eye, 1.0, 0.0) - jnp.where(level_mask(1), n_mat[q], 0.0) for q in n]
    sz = 2
    while sz < L:
        mask = level_mask(sz)
        xb = [x_inv[q].astype(BF16) for q in n]
        xc = [_dot(xb[q], jnp.where(mask, n_mat[q], 0.0).astype(BF16)).astype(BF16) for q in n]
        x_inv = [x_inv[q] - _dot(xc[q], xb[q]) for q in n]
        sz *= 2
    x_inv = [x_inv[q].astype(BF16) for q in n]

    state = [st_ref[p] for p in range(npair)]
    inv_n = 1.0 / RWKV_HD
    for ci in range(nch):
        qs = [ci * npair + p for p in range(npair)]
        a0b = [state[p].astype(BF16) for p in range(npair)]
        rhs = [_dot(jnp.concatenate([ws[q], m_mat[q]], axis=1), jnp.concatenate([a0b[p], vs[q]], axis=0)).astype(BF16)
               for p, q in enumerate(qs)]
        pm = [(-_dot(x_inv[q], rhs[p])).astype(BF16) for p, q in enumerate(qs)]
        o_st = [_dot(jnp.concatenate([rs[q], nr_mat[q], mr_mat[q]], axis=1),
                     jnp.concatenate([a0b[p], pm[p], vs[q]], axis=0)) for p, q in enumerate(qs)]
        new_state = []
        for p, q in enumerate(qs):
            g_col = jnp.sum(jnp.where(eye, jnp.broadcast_to(g_l[q], (P2, P2)), 0.0), axis=-1, keepdims=True)
            new_state.append(g_col * state[p] + _dot_tn(jnp.concatenate([ul[q], kl[q]], axis=0),
                                                        jnp.concatenate([pm[p], vs[q]], axis=0)))
        state = new_state
        for p, q in enumerate(qs):
            sl = sls[p]
            o = o_st[p][:L] + o_st[p][L:]
            mu = _seg_sum(o, first) * inv_n
            oc = o - mu
            var = _seg_sum(oc * oc, first) * inv_n
            on = oc * lax.rsqrt(var + RWKV_GN_EPS) * gg_ref[:, sl] + gb_ref[:, sl]
            o_ref[chunk_rows[ci], sl] = ((on + bonus[q]) * g_ref[chunk_rows[ci], sl].astype(F32)).astype(o_ref.dtype)

    for p in range(npair):
        st_ref[p] = state[p]
    s_ref[0] = st_ref[...]


def _wkv_prompt(r, k, v, a, ld, g, hp, n_seq, nch):
    t = r.shape[0]
    L = WKV_CHUNK * nch
    nc = t // n_seq // L
    row = lambda n, c: (n * nc + c, 0)
    blk = pl.BlockSpec((L, D), row)
    npair = RWKV_HEADS // 2
    return pl.pallas_call(
        _wkv_kernel, grid=(n_seq, nc),
        in_specs=[blk] * 6 + [_full((1, D))] * 5,
        out_specs=[blk, pl.BlockSpec((1, npair, LANES, LANES), lambda n, c: (n, 0, 0, 0))],
        out_shape=[jax.ShapeDtypeStruct((t, D), BF16), jax.ShapeDtypeStruct((n_seq, npair, LANES, LANES), F32)],
        scratch_shapes=[pltpu.VMEM((npair, LANES, LANES), F32)],
        compiler_params=_cparams(("parallel", "arbitrary")), name="wkv_chunk")(r, k, v, a, ld, g, *hp)


def _wkv_sample_kernel(r_ref, k_ref, v_ref, a_ref, ld_ref, g_ref, s_ref, kk_ref, ka_ref, rk_ref, gg_ref, gb_ref,
                       o_ref, so_ref):
    r, k, v, a, ld, g = (ref[0] for ref in (r_ref, k_ref, v_ref, a_ref, ld_ref, g_ref))
    kk = k * kk_ref[0]
    kk = kk / jnp.maximum(jnp.sqrt(jnp.sum(kk * kk, axis=0, keepdims=True)), 1e-12)
    kmod = k * (1.0 + (a - 1.0) * ka_ref[0])
    akk = kk * a
    decay = jnp.exp(ld)

    def value_row(vi, carry):
        s = s_ref[0, vi]
        skk = jnp.sum(s * kk, axis=0, keepdims=True)
        s_new = s * decay - skk * akk + v_ref[0, pl.ds(vi, 1), :] * kmod
        so_ref[0, vi] = s_new
        o_ref[0, pl.ds(vi, 1), :] = jnp.sum(s_new * r, axis=0, keepdims=True)
        return carry

    lax.fori_loop(0, s_ref.shape[1], value_row, 0, unroll=4)
    o = o_ref[0]
    mu = jnp.mean(o, axis=0, keepdims=True)
    oc = o - mu
    var = jnp.mean(oc * oc, axis=0, keepdims=True)
    on = oc * lax.rsqrt(var + RWKV_GN_EPS) * gg_ref[0] + gb_ref[0]
    bonus = jnp.sum(r * kmod * rk_ref[0], axis=0, keepdims=True) * v
    o_ref[0] = (on + bonus) * g


def _wkv_sample(r, k, v, a, ld, g, state, hp):
    b = r.shape[0]
    nh, hd = RWKV_HEADS, RWKV_HD
    t3 = lambda z: jnp.transpose(z.reshape(b, nh, hd), (1, 2, 0))
    vec = pl.BlockSpec((1, hd, b), lambda h: (h, 0, 0))
    par = pl.BlockSpec((1, hd, 1), lambda h: (h, 0, 0))
    sblk = pl.BlockSpec((1, hd, hd, b), lambda h: (h, 0, 0, 0))
    o, s_new = pl.pallas_call(
        _wkv_sample_kernel, grid=(nh,),
        in_specs=[vec] * 6 + [sblk] + [par] * 5,
        out_specs=[vec, sblk],
        out_shape=[jax.ShapeDtypeStruct((nh, hd, b), F32), jax.ShapeDtypeStruct((nh, hd, hd, b), F32)],
        compiler_params=_cparams(("parallel",)), name="wkv_sample")(
            t3(r), t3(k), t3(v), t3(a), t3(ld), t3(g), jnp.transpose(state, (1, 2, 3, 0)),
            *[z.reshape(nh, hd, 1) for z in hp])
    return jnp.transpose(o, (2, 0, 1)).reshape(b, D), jnp.transpose(s_new, (3, 0, 1, 2))


def _mem_prompt_kernel(x_ref, wq_ref, mk_ref, mv_ref, wo_ref, g_ref, b_ref, o_ref):
    x = x_ref[...]
    q = _dot(x.astype(BF16), wq_ref[...]).astype(BF16)
    scale = MEM_HD ** -0.5
    sls = [slice(h * MEM_HD, (h + 1) * MEM_HD) for h in range(MEM_HEADS)]
    s = [_dot_nt(q[:, sl], mk_ref[0, :, sl]) * scale for sl in sls]
    p = [jnp.exp(sh - jnp.max(sh, axis=-1, keepdims=True)) for sh in s]
    den = [jnp.sum(ph, axis=-1, keepdims=True) for ph in p]
    outs = [_dot((ph / dh).astype(BF16), mv_ref[0, :, sl]).astype(BF16) for ph, dh, sl in zip(p, den, sls)]
    acc = _dot(jnp.concatenate(outs, axis=-1), wo_ref[...])
    o_ref[...] = _ln(ALPHA * x + acc, g_ref[...], b_ref[...])


def _mem_attn_prompt(x, w_q, mk, mv, w_o, g, b, n_seq, tm):
    t = x.shape[0]
    nb = t // n_seq // tm
    m = mk.shape[1]
    row = lambda n, i: (n * nb + i, 0)
    mem = pl.BlockSpec((1, m, D), lambda n, i: (n, 0, 0))
    return pl.pallas_call(
        _mem_prompt_kernel, grid=(n_seq, nb),
        in_specs=[pl.BlockSpec((tm, D), row), _full((D, D)), mem, mem, _full((D, D)), _full((1, D)), _full((1, D))],
        out_specs=pl.BlockSpec((tm, D), row), out_shape=jax.ShapeDtypeStruct((t, D), F32),
        compiler_params=_cparams(("parallel", "arbitrary")), name="mem_attn")(x, w_q, mk, mv, w_o, g, b)


def _mem_sample_kernel(q_ref, ck_ref, cv_ref, o_ref, *, bs):
    m, nh, hd = ck_ref.shape[1:]
    rows = q_ref.shape[1]
    col_head = lax.broadcasted_iota(jnp.int32, (rows, m * nh), 1) % nh
    own = col_head == lax.broadcasted_iota(jnp.int32, (rows, m * nh), 0)
    scale = MEM_HD ** -0.5
    nb = range(bs)
    s = [jnp.where(own, _dot_nt(q_ref[b].astype(BF16), ck_ref[b].reshape(m * nh, hd).astype(BF16)) * scale, NEG_INF)
         for b in nb]
    p = [jnp.where(own, jnp.exp(s[b] - jnp.max(s[b], axis=-1, keepdims=True)), 0.0) for b in nb]
    den = [jnp.sum(p[b], axis=-1, keepdims=True) for b in nb]
    for b in nb:
        pb = (p[b] / jnp.where(den[b] > 0.0, den[b], 1.0)).astype(BF16)
        o_ref[b] = _dot(pb, cv_ref[b].reshape(m * nh, hd).astype(BF16))


def _mem_attn_sample(q, cache_k, cache_v, layer, bs):
    _, b, m, nh, hd = cache_k.shape
    q3 = jnp.pad(q.reshape(b, nh, hd), ((0, 0), (0, SUBLANES - nh), (0, 0)))
    qb = pl.BlockSpec((bs, SUBLANES, hd), lambda i: (i, 0, 0))
    cb = pl.BlockSpec((None, bs, m, nh, hd), lambda i: (layer, i, 0, 0, 0))
    out = pl.pallas_call(
        functools.partial(_mem_sample_kernel, bs=bs), grid=(b // bs,), in_specs=[qb, cb, cb], out_specs=qb,
        out_shape=jax.ShapeDtypeStruct((b, SUBLANES, hd), F32),
        compiler_params=_cparams(("parallel",)), name="mem_sample")(q3, cache_k, cache_v)
    return out[:, :nh].reshape(b, D)


_PAIRS = ((0, 1), (0, 2), (0, 3), (1, 2), (1, 3), (2, 3))


def _router_kernel(x_ref, rw_ref, rb_ref, bucket_ref, rank_ref, cnt_ref, base_ref, *, tm):
    i = pl.program_id(0)

    @pl.when(i == 0)
    def _():
        base_ref[...] = jnp.zeros_like(base_ref)

    logits = _dot_nt(rw_ref[...], x_ref[...].astype(BF16))
    e = jnp.exp(logits - jnp.max(logits, axis=0, keepdims=True))
    sel = e / jnp.sum(e, axis=0, keepdims=True) + rb_ref[...]
    s = [sel[j:j + 1, :] for j in range(N_EXPERTS)]
    neg = jnp.float32(-jnp.inf)

    best = jnp.zeros((1, tm), jnp.int32)
    best_score = None
    for gi in range(N_GROUPS):
        s0, s1, s2, s3 = s[4 * gi:4 * gi + 4]
        hi01, lo01, hi23, lo23 = jnp.maximum(s0, s1), jnp.minimum(s0, s1), jnp.maximum(s2, s3), jnp.minimum(s2, s3)
        score = jnp.maximum(hi01, hi23) + jnp.maximum(jnp.minimum(hi01, hi23), jnp.maximum(lo01, lo23))
        if gi == 0:
            best_score = score
        else:
            take = score > best_score
            best = jnp.where(take, gi, best)
            best_score = jnp.where(take, score, best_score)
    vals = []
    for j in range(EXPERTS_PER_GROUP):
        vj = s[j]
        for gi in range(1, N_GROUPS):
            vj = jnp.where(best == gi, s[4 * gi + j], vj)
        vals.append(vj)

    def argmax4(v):
        idx, mx = jnp.zeros((1, tm), jnp.int32), v[0]
        for j in range(1, EXPERTS_PER_GROUP):
            take = v[j] > mx
            idx = jnp.where(take, j, idx)
            mx = jnp.where(take, v[j], mx)
        return idx

    i1 = argmax4(vals)
    i2 = argmax4([jnp.where(i1 == j, neg, vals[j]) for j in range(EXPERTS_PER_GROUP)])
    lo, hi = jnp.minimum(i1, i2), jnp.maximum(i1, i2)
    pair = jnp.zeros((1, tm), jnp.int32)
    for pi, (pa, pb) in enumerate(_PAIRS):
        pair = jnp.where(jnp.logical_and(lo == pa, hi == pb), pi, pair)
    bucket = best * len(_PAIRS) + pair
    bucket_ref[0] = bucket

    onehot = (lax.broadcasted_iota(jnp.int32, (BUCKET_ROWS, tm), 0) == bucket).astype(F32)
    upper = (lax.broadcasted_iota(jnp.int32, (tm, tm), 0) <= lax.broadcasted_iota(jnp.int32, (tm, tm), 1)).astype(BF16)
    cum = _dot(onehot.astype(BF16), upper)
    base = base_ref[...]
    rank = jnp.sum(onehot * (cum + base), axis=0, keepdims=True) - 1.0
    rank_ref[0] = rank.astype(jnp.int32)
    base = base + jnp.sum(onehot, axis=1, keepdims=True)
    base_ref[...] = base
    cnt_ref[...] = jnp.broadcast_to(base, cnt_ref.shape)


def _router(x, rw_t, rb, tm):
    t = x.shape[0]
    nb = t // tm
    ib = pl.BlockSpec((1, 1, tm), lambda i: (i, 0, 0))
    bucket, rank, cnt = pl.pallas_call(
        functools.partial(_router_kernel, tm=tm), grid=(nb,),
        in_specs=[pl.BlockSpec((tm, D), lambda i: (i, 0)), _full(rw_t.shape), _full(rb.shape)],
        out_specs=[ib, ib, _full((BUCKET_ROWS, LANES))],
        out_shape=[jax.ShapeDtypeStruct((nb, 1, tm), jnp.int32), jax.ShapeDtypeStruct((nb, 1, tm), jnp.int32),
                   jax.ShapeDtypeStruct((BUCKET_ROWS, LANES), F32)],
        scratch_shapes=[pltpu.VMEM((BUCKET_ROWS, 1), F32)],
        compiler_params=_cparams(("arbitrary",)), name="router")(x, rw_t, rb)
    return bucket.reshape(t), rank.reshape(t), cnt[:N_BUCKETS, 0].astype(jnp.int32)


_ROW_GROUP = 8


def _row_copies(idx_ref, base, src_hbm, dst, sem, n, wait):
    def body(j, carry):
        for k in range(_ROW_GROUP):
            r = j * _ROW_GROUP + k
            cp = pltpu.make_async_copy(src_hbm.at[pl.ds(idx_ref[base + r], 1)], dst.at[pl.ds(r, 1)], sem)
            if wait:
                cp.wait()
            else:
                cp.start(priority=k % 2)
        return carry

    lax.fori_loop(0, n // _ROW_GROUP, body, 0)


def _ffn_kernel(src_ref, lo_ref, hi_ref, nrow_ref, used_ref, x_hbm, rw_ref, g0_ref, u0_ref, d0_ref, g1_ref, u1_ref,
                d1_ref, o_ref, xbuf, sem, *, blk):
    i = pl.program_id(0)
    used = used_ref[0]
    slot = i % 2

    @pl.when(jnp.logical_and(i == 0, used > 0))
    def _():
        xbuf[...] = jnp.zeros_like(xbuf)
        _row_copies(src_ref, 0, x_hbm, xbuf.at[0], sem.at[0], nrow_ref[0], False)

    @pl.when(i + 1 < used)
    def _():
        _row_copies(src_ref, (i + 1) * blk, x_hbm, xbuf.at[1 - slot], sem.at[1 - slot], nrow_ref[i + 1], False)

    @pl.when(i < used)
    def _():
        _row_copies(src_ref, i * blk, x_hbm, xbuf.at[slot], sem.at[slot], nrow_ref[i], True)
        xb = xbuf[slot].astype(BF16)
        logits = _dot(xb, rw_ref[...])
        lane = lax.broadcasted_iota(jnp.int32, logits.shape, 1)
        l_lo = jnp.sum(jnp.where(lane == lo_ref[i], logits, 0.0), axis=-1, keepdims=True)
        l_hi = jnp.sum(jnp.where(lane == hi_ref[i], logits, 0.0), axis=-1, keepdims=True)
        w_lo = _sigmoid(l_lo - l_hi)

        def expert(g_ref, u_ref, d_ref):
            gate = _dot(xb, g_ref[0])
            act = gate * _sigmoid(gate) * _dot(xb, u_ref[0])
            return _dot(act.astype(BF16), d_ref[0])

        y_lo = expert(g0_ref, u0_ref, d0_ref)
        y_hi = expert(g1_ref, u1_ref, d1_ref)
        o_ref[...] = w_lo * y_lo + (1.0 - w_lo) * y_hi

    @pl.when(i >= used)
    def _():
        o_ref[...] = jnp.zeros_like(o_ref)


def _ffn(x, src, blk_lo, blk_hi, blk_rows, n_used, rw, w_gate, w_up, w_down, layer, blk):
    rows = src.shape[0]
    nblk = rows // blk
    wg = lambda sel: pl.BlockSpec((None, 1, D, EXPERT_FF),
                                  lambda i, s, lo, hi, nr, used: (layer, (lo, hi)[sel][i], 0, 0))
    wd = lambda sel: pl.BlockSpec((None, 1, EXPERT_FF, D),
                                  lambda i, s, lo, hi, nr, used: (layer, (lo, hi)[sel][i], 0, 0))
    return pl.pallas_call(
        functools.partial(_ffn_kernel, blk=blk),
        grid_spec=pltpu.PrefetchScalarGridSpec(
            num_scalar_prefetch=5, grid=(nblk,),
            in_specs=[pl.BlockSpec(memory_space=pl.ANY),
                      pl.BlockSpec(rw.shape, lambda i, s, lo, hi, nr, used: (0, 0)),
                      wg(0), wg(0), wd(0), wg(1), wg(1), wd(1)],
            out_specs=pl.BlockSpec((blk, D), lambda i, s, lo, hi, nr, used: (i, 0)),
            scratch_shapes=[pltpu.VMEM((2, blk, D), F32), pltpu.SemaphoreType.DMA((2,))]),
        out_shape=jax.ShapeDtypeStruct((rows, D), F32),
        compiler_params=_cparams(("arbitrary",)), name="moe_ffn")(
            src, blk_lo, blk_hi, blk_rows, n_used, x, rw, w_gate, w_up, w_down, w_gate, w_up, w_down)


def _combine_ln_kernel(dest_ref, x_ref, y_hbm, g_ref, b_ref, o_ref, ybuf, sem, *, tm):
    i = pl.program_id(0)
    slot = i % 2

    @pl.when(i == 0)
    def _():
        _row_copies(dest_ref, 0, y_hbm, ybuf.at[0], sem.at[0], tm, False)

    @pl.when(i + 1 < pl.num_programs(0))
    def _():
        _row_copies(dest_ref, (i + 1) * tm, y_hbm, ybuf.at[1 - slot], sem.at[1 - slot], tm, False)

    _row_copies(dest_ref, i * tm, y_hbm, ybuf.at[slot], sem.at[slot], tm, True)
    o_ref[...] = _ln(ALPHA * x_ref[...] + ybuf[slot], g_ref[...], b_ref[...])


def _combine_ln(x, y_rows, dest, g, b, tm):
    t = x.shape[0]
    rowb = pl.BlockSpec((tm, D), lambda i, d: (i, 0))
    vec = pl.BlockSpec((1, D), lambda i, d: (0, 0))
    return pl.pallas_call(
        functools.partial(_combine_ln_kernel, tm=tm),
        grid_spec=pltpu.PrefetchScalarGridSpec(
            num_scalar_prefetch=1, grid=(t // tm,),
            in_specs=[rowb, pl.BlockSpec(memory_space=pl.ANY), vec, vec], out_specs=rowb,
            scratch_shapes=[pltpu.VMEM((2, tm, D), F32), pltpu.SemaphoreType.DMA((2,))]),
        out_shape=jax.ShapeDtypeStruct((t, D), F32),
        compiler_params=_cparams(("arbitrary",)), name="moe_combine_ln")(dest, x, y_rows, g, b)


def _invert_rows_kernel(dest_ref, src_ref):
    def clear(r, carry):
        src_ref[r] = 0
        return carry

    def put(tok, carry):
        src_ref[dest_ref[tok]] = tok
        return carry

    lax.fori_loop(0, src_ref.shape[0], clear, 0, unroll=8)
    lax.fori_loop(0, dest_ref.shape[0], put, 0, unroll=8)


def _invert_rows(dest, rows):
    smem = pl.BlockSpec(memory_space=pltpu.SMEM)
    return pl.pallas_call(
        _invert_rows_kernel, in_specs=[smem], out_specs=smem,
        out_shape=jax.ShapeDtypeStruct((rows,), jnp.int32), name="invert_rows")(dest)


def _moe_ln(x, rw_t, rb, rw_pad, w_gate, w_up, w_down, layer, g, b, tm_router, blk, tm_comb):
    t = x.shape[0]
    bucket, rank, counts = _router(x, rw_t, rb, tm_router)
    padded = (counts + blk - 1) // blk * blk
    ends = jnp.cumsum(padded)
    dest = ((ends - padded)[bucket] + rank).astype(jnp.int32)
    nblk = t // blk + N_BUCKETS
    src = _invert_rows(dest, nblk * blk)
    blk_bucket = jnp.minimum(jnp.searchsorted(ends, jnp.arange(nblk) * blk, side='right'), N_BUCKETS - 1)
    pair_lo = jnp.array([p[0] for p in _PAIRS], jnp.int32)
    pair_hi = jnp.array([p[1] for p in _PAIRS], jnp.int32)
    grp, pr = blk_bucket // len(_PAIRS), blk_bucket % len(_PAIRS)
    blk_lo = (grp * EXPERTS_PER_GROUP + pair_lo[pr]).astype(jnp.int32)
    blk_hi = (grp * EXPERTS_PER_GROUP + pair_hi[pr]).astype(jnp.int32)
    n_used = (ends[-1:] // blk).astype(jnp.int32)
    bucket_end = (ends - padded + counts)[blk_bucket]
    blk_rows = jnp.clip(bucket_end - jnp.arange(nblk) * blk, 0, blk)
    blk_rows = ((blk_rows + _ROW_GROUP - 1) // _ROW_GROUP * _ROW_GROUP).astype(jnp.int32)
    y_rows = _ffn(x, src, blk_lo, blk_hi, blk_rows, n_used, rw_pad, w_gate, w_up, w_down, layer, blk)
    return _combine_ln(x, y_rows, dest, g, b, tm_comb)


def _moe_dense_kernel(x_ref, lo_ref, hi_ref, rw_ref, wg_ref, wu_ref, wd_ref, g_ref, b_ref, o_ref, acc_ref):
    e = pl.program_id(0)

    @pl.when(e == 0)
    def _():
        acc_ref[...] = jnp.zeros_like(acc_ref)

    x = x_ref[...]
    xb = x.astype(BF16)
    logits = _dot(xb, rw_ref[...])
    lane = lax.broadcasted_iota(jnp.int32, logits.shape, 1)
    lo, hi = lo_ref[...], hi_ref[...]
    l_lo = jnp.sum(jnp.where(lane == lo, logits, 0.0), axis=-1, keepdims=True)
    l_hi = jnp.sum(jnp.where(lane == hi, logits, 0.0), axis=-1, keepdims=True)
    w_lo = _sigmoid(l_lo - l_hi)
    coef = jnp.where(lo == e, w_lo, 0.0) + jnp.where(hi == e, 1.0 - w_lo, 0.0)
    gate = _dot(xb, wg_ref[0])
    act = gate * _sigmoid(gate) * _dot(xb, wu_ref[0])
    acc_ref[...] += coef * _dot(act.astype(BF16), wd_ref[0])

    @pl.when(e == pl.num_programs(0) - 1)
    def _():
        o_ref[...] = _ln(ALPHA * x + acc_ref[...], g_ref[...], b_ref[...])


def _moe_ln_dense(x, rw_t, rb, rw_pad, w_gate, w_up, w_down, layer, g, b):
    t = x.shape[0]
    bucket, _, _ = _router(x, rw_t, rb, t)
    pair_lo = jnp.array([p[0] for p in _PAIRS], jnp.int32)
    pair_hi = jnp.array([p[1] for p in _PAIRS], jnp.int32)
    grp, pr = bucket // len(_PAIRS), bucket % len(_PAIRS)
    lo = (grp * EXPERTS_PER_GROUP + pair_lo[pr]).astype(jnp.int32).reshape(t, 1)
    hi = (grp * EXPERTS_PER_GROUP + pair_hi[pr]).astype(jnp.int32).reshape(t, 1)
    wg = pl.BlockSpec((None, 1, D, EXPERT_FF), lambda e: (layer, e, 0, 0))
    wd = pl.BlockSpec((None, 1, EXPERT_FF, D), lambda e: (layer, e, 0, 0))
    return pl.pallas_call(
        _moe_dense_kernel, grid=(N_EXPERTS,),
        in_specs=[_full((t, D)), _full((t, 1)), _full((t, 1)), _full(rw_pad.shape), wg, wg, wd,
                  _full((1, D)), _full((1, D))],
        out_specs=_full((t, D)), out_shape=jax.ShapeDtypeStruct((t, D), F32),
        scratch_shapes=[pltpu.VMEM((t, D), F32)],
        compiler_params=_cparams(("arbitrary",)), name="moe_dense")(x, lo, hi, rw_pad, w_gate, w_up, w_down, g, b)


def kernel(x_prompt, x_sample, cache_swa_k, cache_swa_v, state_lru_conv, state_lru_h, state_rwkv_shift, state_rwkv_wkv, cache_mem_k, cache_mem_v, mem_prompt, swa_w_qkv, swa_sinks, swa_w_o, lru_w_in, lru_b_in, lru_conv_w, lru_conv_b, lru_w_a, lru_b_a, lru_w_i, lru_b_i, lru_lambda, lru_w_o, rwkv_mu, rwkv_w_r, rwkv_w_k, rwkv_w_v, rwkv_w0, rwkv_w1, rwkv_w2, rwkv_a0, rwkv_a1, rwkv_a2, rwkv_g1, rwkv_g2, rwkv_k_k, rwkv_k_a, rwkv_r_k, rwkv_gn_g, rwkv_gn_b, rwkv_w_o, mem_w_q, mem_w_kv, mem_w_o, ln_g, ln_b, router_w, router_b, moe_w_gate, moe_w_up, moe_w_down):
    n_p, seq, _ = x_prompt.shape
    n_s, dec_seq, _ = x_sample.shape
    assert dec_seq == 1
    past_len = 8192
    xp = x_prompt.reshape(n_p * seq, D)
    xs = x_sample.reshape(n_s, D)
    row = lambda v: v.reshape(1, -1)
    bf = lambda w: w.astype(BF16)

    rw_t = bf(router_w.T)
    rb = router_b.reshape(N_EXPERTS, 1)
    rw_pad = bf(jnp.pad(router_w, ((0, 0), (0, LANES - N_EXPERTS))))
    wg, wu, wd = bf(moe_w_gate), bf(moe_w_up), bf(moe_w_down)
    mem_p = mem_prompt.reshape(n_p * mem_prompt.shape[1], D)
    m_len = mem_prompt.shape[1]

    swa_k_p, swa_v_p, swa_k_s, swa_v_s = [], [], [], []
    lru_c_p, lru_h_p, lru_c_s, lru_h_s = [], [], [], []
    rw_x_p, rw_s_p, rw_x_s, rw_s_s = [], [], [], []
    mem_k_p, mem_v_p = [], []

    for layer in range(DEPTH):
        kind, i = layer % N_MIXERS, layer // N_MIXERS
        g0, b0 = row(ln_g[layer, 0]), row(ln_b[layer, 0])
        if kind == 0:
            w_qkv, w_o = bf(swa_w_qkv[i]), bf(swa_w_o[i])
            keep = min(WINDOW, seq)
            q, k, v, kv_last = _swa_qkv(xp, w_qkv, jnp.arange(seq), n_p, 1024, keep, BF16)
            o = _swa_attn_prompt(q, k, v, swa_sinks[i], n_p, 4)
            swa_k_p.append(kv_last[:, :, :KV_WIDTH].reshape(n_p, keep, SWA_KV_HEADS, HEAD_DIM))
            swa_v_p.append(kv_last[:, :, KV_WIDTH:].reshape(n_p, keep, SWA_KV_HEADS, HEAD_DIM))
            xp = _proj_ln(o, w_o, xp, g0, b0, 1024)

            qs, _, _, kv_new = _swa_qkv(xs, w_qkv, jnp.full((n_s,), past_len), 1, n_s, n_s, F32)
            kn, vn = kv_new[0, :, :KV_WIDTH], kv_new[0, :, KV_WIDTH:]
            os_ = _swa_attn_sample(qs, kn, vn, cache_swa_k, cache_swa_v, i, swa_sinks[i], 8)
            wb = cache_swa_k.shape[2]
            k_all = jnp.concatenate([cache_swa_k[i], kn.reshape(n_s, 1, SWA_KV_HEADS, HEAD_DIM)], axis=1)
            v_all = jnp.concatenate([cache_swa_v[i], vn.reshape(n_s, 1, SWA_KV_HEADS, HEAD_DIM)], axis=1)
            swa_k_s.append(k_all[:, -wb:])
            swa_v_s.append(v_all[:, -wb:])
            xs = _proj_ln(os_, w_o, xs, g0, b0, n_s)
        elif kind == 1:
            wts = _lru_weights(lru_w_in[i], lru_b_in[i], lru_conv_w[i], lru_conv_b[i], lru_w_a[i], lru_b_a[i],
                               lru_w_i[i], lru_b_i[i], lru_lambda[i], lru_w_o[i])
            xp, conv_last, h_last = _lru_prompt(xp, wts, g0, b0, n_p, 256)
            lru_c_p.append(conv_last[:, SUBLANES - (CONV_W - 1):])
            lru_h_p.append(h_last[:, SUBLANES - 1])
            xs, xb_s, h_s = _lru_sample(xs, state_lru_conv[i], state_lru_h[i], wts, g0, b0)
            lru_c_s.append(jnp.concatenate([state_lru_conv[i][:, 1:], xb_s[:, None]], axis=1))
            lru_h_s.append(h_s)
        else:
            wts = (rwkv_mu[i], bf(rwkv_w_r[i]), bf(rwkv_w_k[i]), bf(rwkv_w_v[i]), row(rwkv_w0[i]), bf(rwkv_w1[i]),
                   bf(rwkv_w2[i]), row(rwkv_a0[i]), bf(rwkv_a1[i]), bf(rwkv_a2[i]), bf(rwkv_g1[i]), bf(rwkv_g2[i]))
            hp = (row(rwkv_k_k[i]), row(rwkv_k_a[i]), row(rwkv_r_k[i]), row(rwkv_gn_g[i]), row(rwkv_gn_b[i]))
            w_o = bf(rwkv_w_o[i])
            rw_x_p.append(xp.reshape(n_p, seq, D)[:, -1])
            rw_x_s.append(xs)
            r, k, v, a, ld, g = _rwkv_pre(xp, jnp.zeros((n_p, SUBLANES, D), F32), wts, n_p, 512, True, BF16)
            o, st = _wkv_prompt(r, k, v, a, ld, g, hp, n_p, 2)
            hd = RWKV_HD
            st = jnp.stack([st[:, :, :hd, :hd], st[:, :, hd:, hd:]], axis=2).reshape(n_p, RWKV_HEADS, hd, hd)
            rw_s_p.append(jnp.swapaxes(st, -1, -2))
            xp = _proj_ln(o, w_o, xp, g0, b0, 1024)

            r, k, v, a, ld, g = _rwkv_pre(xs, state_rwkv_shift[i], wts, 1, n_s, False, F32)
            os_, s_new = _wkv_sample(r, k, v, a, ld, g, state_rwkv_wkv[i], hp)
            rw_s_s.append(s_new)
            xs = _proj_ln(os_, w_o, xs, g0, b0, n_s)

        g1, b1 = row(ln_g[layer, 1]), row(ln_b[layer, 1])
        w_q, w_o = bf(mem_w_q[layer]), bf(mem_w_o[layer])
        mkv = _matmul(mem_p, bf(mem_w_kv[layer]), 512)
        mk, mv = mkv[:, :D], mkv[:, D:]
        mem_k_p.append(mk.reshape(n_p, m_len, MEM_HEADS, MEM_HD))
        mem_v_p.append(mv.reshape(n_p, m_len, MEM_HEADS, MEM_HD))
        xp = _mem_attn_prompt(xp, w_q, bf(mk).reshape(n_p, m_len, D), bf(mv).reshape(n_p, m_len, D), w_o, g1, b1,
                              n_p, 1024)
        qs = _matmul(xs, w_q, n_s)
        os_ = _mem_attn_sample(qs, cache_mem_k, cache_mem_v, layer, 4)
        xs = _proj_ln(os_, w_o, xs, g1, b1, n_s)

        g2, b2 = row(ln_g[layer, 2]), row(ln_b[layer, 2])
        xp = _moe_ln(xp, rw_t, rb, rw_pad, wg, wu, wd, layer, g2, b2, 512, 256, 512)
        xs = _moe_ln_dense(xs, rw_t, rb, rw_pad, wg, wu, wd, layer, g2, b2)

    return (xp.reshape(n_p, seq, D), xs.reshape(n_s, 1, D),
            jnp.stack(swa_k_p), jnp.stack(swa_v_p), jnp.stack(lru_c_p), jnp.stack(lru_h_p),
            jnp.stack(rw_x_p), jnp.stack(rw_s_p), jnp.stack(mem_k_p), jnp.stack(mem_v_p),
            jnp.stack(swa_k_s), jnp.stack(swa_v_s), jnp.stack(lru_c_s), jnp.stack(lru_h_s),
            jnp.stack(rw_x_s), jnp.stack(rw_s_s))
```

```python
import functools

import jax
import jax.numpy as jnp
from jax import lax
from jax.experimental import pallas as pl
from jax.experimental.pallas import tpu as pltpu

F32 = jnp.float32
BF16 = jnp.bfloat16

D = 1024
DEPTH = 4
N_MIXERS = 3
HEAD_DIM = 64
SWA_HEADS = D // HEAD_DIM
SWA_KV_HEADS = 4
SWA_GROUP = SWA_HEADS // SWA_KV_HEADS
Q_WIDTH = SWA_HEADS * HEAD_DIM
KV_WIDTH = SWA_KV_HEADS * HEAD_DIM
WINDOW = 128
ROT_DIM = HEAD_DIM // 4
ROPE_THETA = 500000.0
LRU_BLOCKS = 16
CONV_W = 4
LRU_C = 8.0
RWKV_HEADS = 16
RWKV_HD = 64
RWKV_GN_EPS = 64e-5
MEM_HEADS = 4
MEM_HD = D // MEM_HEADS
N_EXPERTS = 16
N_GROUPS = 4
EXPERTS_PER_GROUP = 4
EXPERT_FF = 512
LN_EPS = 1e-5
ALPHA = (2.0 * DEPTH) ** 0.25
NEG_INF = -1e30

LANES = 128
SUBLANES = 8
VMEM_LIMIT = 56 * 1024 * 1024
WKV_CHUNK = 64
N_BUCKETS = N_GROUPS * 6
BUCKET_ROWS = 32


def _cparams(sem):
    return pltpu.CompilerParams(dimension_semantics=sem, vmem_limit_bytes=VMEM_LIMIT)


def _dot(a, b):
    return jnp.dot(a, b, preferred_element_type=F32)


def _dot_nt(a, b):
    return lax.dot_general(a, b, (((1,), (1,)), ((), ())), preferred_element_type=F32)


def _dot_tn(a, b):
    return lax.dot_general(a, b, (((0,), (0,)), ((), ())), preferred_element_type=F32)


def _ln(z, g, b):
    mu = jnp.mean(z, axis=-1, keepdims=True)
    zc = z - mu
    var = jnp.mean(zc * zc, axis=-1, keepdims=True)
    return zc * lax.rsqrt(var + LN_EPS) * g + b


def _softplus(z):
    return jnp.maximum(z, 0.0) + jnp.log1p(jnp.exp(-jnp.abs(z)))


def _sigmoid(z):
    return 1.0 / (1.0 + jnp.exp(-z))


def _round_bf16(x):
    return x.astype(BF16).astype(F32)


def _full(shape):
    nd = len(shape)
    return pl.BlockSpec(shape, lambda *_: (0,) * nd)


def _mm_kernel(a_ref, w_ref, o_ref):
    o_ref[...] = _dot(a_ref[...].astype(BF16), w_ref[...]).astype(o_ref.dtype)


def _matmul(a, w, tm, out_dtype=F32):
    t, k = a.shape
    n = w.shape[1]
    return pl.pallas_call(
        _mm_kernel, grid=(t // tm,),
        in_specs=[pl.BlockSpec((tm, k), lambda i: (i, 0)), _full((k, n))],
        out_specs=pl.BlockSpec((tm, n), lambda i: (i, 0)),
        out_shape=jax.ShapeDtypeStruct((t, n), out_dtype),
        compiler_params=_cparams(("parallel",)), name="matmul")(a, w)


def _proj_ln_kernel(a_ref, w_ref, x_ref, g_ref, b_ref, o_ref):
    acc = _dot(a_ref[...].astype(BF16), w_ref[...])
    o_ref[...] = _ln(ALPHA * x_ref[...] + acc, g_ref[...], b_ref[...])


def _proj_ln(a, w, x, g, b, tm):
    t, k = a.shape
    return pl.pallas_call(
        _proj_ln_kernel, grid=(t // tm,),
        in_specs=[pl.BlockSpec((tm, k), lambda i: (i, 0)), _full((k, D)),
                  pl.BlockSpec((tm, D), lambda i: (i, 0)), _full((1, D)), _full((1, D))],
        out_specs=pl.BlockSpec((tm, D), lambda i: (i, 0)),
        out_shape=jax.ShapeDtypeStruct((t, D), F32),
        compiler_params=_cparams(("parallel",)), name="proj_ln")(a, w, x, g, b)


def _rope_tables(pos):
    half = ROT_DIM // 2
    inv_freq = ROPE_THETA ** (-jnp.arange(half, dtype=F32) / half)
    ang = pos.astype(F32)[:, None] * inv_freq
    cos, sin = jnp.cos(ang), jnp.sin(ang)
    one = jnp.ones((pos.shape[0], HEAD_DIM - ROT_DIM), F32)
    zero = jnp.zeros((pos.shape[0], HEAD_DIM - ROT_DIM), F32)
    zh = jnp.zeros_like(sin)
    c = jnp.concatenate([cos, cos, one], axis=1)
    s1 = jnp.concatenate([-sin, zh, zero], axis=1)
    s2 = jnp.concatenate([zh, sin, zero], axis=1)
    rep = LANES // HEAD_DIM
    return jnp.tile(c, (1, rep)), jnp.tile(s1, (1, rep)), jnp.tile(s2, (1, rep))


def _swa_qkv_kernel(x_ref, w_ref, c_ref, s1_ref, s2_ref, q_ref, k_ref, v_ref, kv_ref, *, tm, keep):
    acc = _dot(x_ref[...].astype(BF16), w_ref[...])
    c, s1, s2 = c_ref[...], s1_ref[...], s2_ref[...]
    half = ROT_DIM // 2
    n_q = Q_WIDTH // LANES
    n_k = KV_WIDTH // LANES
    for cg in range(n_q + n_k):
        xg = acc[:, cg * LANES:(cg + 1) * LANES]
        rot = xg * c + pltpu.roll(xg, LANES - half, 1) * s1 + pltpu.roll(xg, half, 1) * s2
        if cg < n_q:
            q_ref[:, cg * LANES:(cg + 1) * LANES] = rot.astype(q_ref.dtype)
        else:
            ck = cg - n_q
            k_ref[:, ck * LANES:(ck + 1) * LANES] = rot.astype(k_ref.dtype)
            kv_ref[0, :, ck * LANES:(ck + 1) * LANES] = rot[tm - keep:, :]
    v = acc[:, Q_WIDTH + KV_WIDTH:]
    v_ref[...] = v.astype(v_ref.dtype)
    kv_ref[0, :, KV_WIDTH:] = v[tm - keep:, :]


def _swa_qkv(x, w_qkv, pos, n_seq, tm, keep, qdtype):
    t = x.shape[0]
    s = t // n_seq
    nb = s // tm
    c, s1, s2 = _rope_tables(pos)
    row = lambda n, i: (n * nb + i, 0)
    tab = pl.BlockSpec((tm, LANES), lambda n, i: (i, 0))
    kern = functools.partial(_swa_qkv_kernel, tm=tm, keep=keep)
    return pl.pallas_call(
        kern, grid=(n_seq, nb),
        in_specs=[pl.BlockSpec((tm, D), row), _full((D, Q_WIDTH + 2 * KV_WIDTH)), tab, tab, tab],
        out_specs=[pl.BlockSpec((tm, Q_WIDTH), row), pl.BlockSpec((tm, KV_WIDTH), row),
                   pl.BlockSpec((tm, KV_WIDTH), row),
                   pl.BlockSpec((1, keep, 2 * KV_WIDTH), lambda n, i: (n, 0, 0))],
        out_shape=[jax.ShapeDtypeStruct((t, Q_WIDTH), qdtype), jax.ShapeDtypeStruct((t, KV_WIDTH), qdtype),
                   jax.ShapeDtypeStruct((t, KV_WIDTH), qdtype),
                   jax.ShapeDtypeStruct((n_seq, keep, 2 * KV_WIDTH), F32)],
        compiler_params=_cparams(("parallel", "arbitrary")), name="swa_qkv")(x, w_qkv, c, s1, s2)


def _swa_attn_kernel(sink_ref, q_ref, kp_ref, kc_ref, vp_ref, vc_ref, o_ref, *, nq):
    j = pl.program_id(1)
    w, grp = WINDOW, SWA_GROUP
    r = lax.broadcasted_iota(jnp.int32, (grp * w, 2 * w), 0) % w
    c = lax.broadcasted_iota(jnp.int32, (grp * w, 2 * w), 1)
    in_prev = jnp.logical_and(c < w, c > r)
    in_cur = jnp.logical_and(c >= w, (c - w) <= r)
    ok_inner = jnp.logical_or(in_prev, in_cur)
    ok_first = jnp.logical_or(jnp.logical_and(in_prev, j > 0), in_cur)
    scale = HEAD_DIM ** -0.5
    combos = [(u, h) for u in range(nq) for h in range(SWA_KV_HEADS)]
    kcat, vcat, q4, sink, ok = [], [], [], [], []
    for u, h in combos:
        sl = slice(h * HEAD_DIM, (h + 1) * HEAD_DIM)
        rows = slice(u * w, (u + 1) * w)
        before = slice((u - 1) * w, u * w)
        k_prev = kp_ref[:, sl] if u == 0 else kc_ref[before, sl]
        v_prev = vp_ref[:, sl] if u == 0 else vc_ref[before, sl]
        kcat.append(jnp.concatenate([k_prev, kc_ref[rows, sl]], axis=0))
        vcat.append(jnp.concatenate([v_prev, vc_ref[rows, sl]], axis=0))
        heads = [h * grp + g for g in range(grp)]
        q4.append(jnp.concatenate([q_ref[rows, hq * HEAD_DIM:(hq + 1) * HEAD_DIM] for hq in heads], axis=0))
        sink.append(jnp.concatenate([jnp.full((w, 1), sink_ref[hq], F32) for hq in heads], axis=0))
        ok.append(ok_first if u == 0 else ok_inner)
    n = range(len(combos))
    s = [jnp.where(ok[i], _dot_nt(q4[i], kcat[i]) * scale, NEG_INF) for i in n]
    m = [jnp.maximum(jnp.max(s[i], axis=-1, keepdims=True), sink[i]) for i in n]
    p = [jnp.exp(s[i] - m[i]) for i in n]
    den = [jnp.sum(p[i], axis=-1, keepdims=True) + jnp.exp(sink[i] - m[i]) for i in n]
    o = [_dot((p[i] / den[i]).astype(BF16), vcat[i]) for i in n]
    for i, (u, h) in enumerate(combos):
        for g in range(grp):
            hq = h * grp + g
            o_ref[u * w:(u + 1) * w, hq * HEAD_DIM:(hq + 1) * HEAD_DIM] = o[i][g * w:(g + 1) * w].astype(o_ref.dtype)


def _swa_attn_prompt(q, k, v, sinks, n_seq, nq):
    t = q.shape[0]
    nb = t // n_seq // WINDOW
    ns = nb // nq
    cur = lambda n, j: (n * ns + j, 0)
    prev = lambda n, j: (n * nb + jnp.maximum(j * nq - 1, 0), 0)
    return pl.pallas_call(
        functools.partial(_swa_attn_kernel, nq=nq), grid=(n_seq, ns),
        in_specs=[pl.BlockSpec(memory_space=pltpu.SMEM), pl.BlockSpec((nq * WINDOW, Q_WIDTH), cur),
                  pl.BlockSpec((WINDOW, KV_WIDTH), prev), pl.BlockSpec((nq * WINDOW, KV_WIDTH), cur),
                  pl.BlockSpec((WINDOW, KV_WIDTH), prev), pl.BlockSpec((nq * WINDOW, KV_WIDTH), cur)],
        out_specs=pl.BlockSpec((nq * WINDOW, Q_WIDTH), cur),
        out_shape=jax.ShapeDtypeStruct((t, Q_WIDTH), BF16),
        compiler_params=_cparams(("parallel", "arbitrary")), name="swa_attn")(sinks, q, k, k, v, v)


def _swa_sample_kernel(sink_ref, qbd_ref, q_ref, kn_ref, vnbd_ref, ckt_ref, cvt_ref, o_ref, *, bs):
    nkv, hd, wb = ckt_ref.shape[1:]
    nq = q_ref.shape[1]
    npad = kn_ref.shape[1]
    key = lax.broadcasted_iota(jnp.int32, (nq, wb), 1)
    valid = (wb - key) < WINDOW
    own_new = (lax.broadcasted_iota(jnp.int32, (nq, npad), 1)
               == lax.broadcasted_iota(jnp.int32, (nq, npad), 0) // SWA_GROUP)
    sink = sink_ref[...]
    scale = HEAD_DIM ** -0.5
    nb = range(bs)
    s = [jnp.where(valid, _dot(qbd_ref[b].astype(BF16), ckt_ref[b].reshape(nkv * hd, wb).astype(BF16)) * scale, NEG_INF)
         for b in nb]
    sn = [jnp.where(own_new, _dot_nt(q_ref[b].astype(BF16), kn_ref[b].astype(BF16)) * scale, NEG_INF) for b in nb]
    m = [jnp.maximum(jnp.maximum(jnp.max(s[b], axis=-1, keepdims=True), jnp.max(sn[b], axis=-1, keepdims=True)), sink)
         for b in nb]
    p = [jnp.where(valid, jnp.exp(s[b] - m[b]), 0.0) for b in nb]
    pn = [jnp.where(own_new, jnp.exp(sn[b] - m[b]), 0.0) for b in nb]
    den = [jnp.sum(p[b], axis=-1, keepdims=True) + jnp.sum(pn[b], axis=-1, keepdims=True) + jnp.exp(sink - m[b])
           for b in nb]
    for b in nb:
        o_ref[b] = (_dot_nt((p[b] / den[b]).astype(BF16), cvt_ref[b].reshape(nkv * hd, wb).astype(BF16))
                    + _dot((pn[b] / den[b]).astype(BF16), vnbd_ref[b].astype(BF16)))


def _swa_attn_sample(q, kn, vn, cache_k, cache_v, layer, sinks, bs):
    _, b, wb, nkv, hd = cache_k.shape
    grp = SWA_HEADS // nkv
    eye = jnp.eye(nkv, dtype=q.dtype)
    qbd = jnp.einsum('bhgd,hk->bhgkd', q.reshape(b, nkv, grp, hd), eye).reshape(b, SWA_HEADS, nkv * hd)
    pad = lambda z: jnp.pad(z, ((0, 0), (0, SUBLANES - nkv), (0, 0)))
    vnbd = pad(jnp.einsum('bhd,hk->bhkd', vn.reshape(b, nkv, hd), eye).reshape(b, nkv, nkv * hd))
    blk = lambda r, w: pl.BlockSpec((bs, r, w), lambda i: (i, 0, 0))
    cblk = pl.BlockSpec((None, bs, nkv, hd, wb), lambda i: (layer, i, 0, 0, 0))
    out = pl.pallas_call(
        functools.partial(_swa_sample_kernel, bs=bs), grid=(b // bs,),
        in_specs=[_full((SWA_HEADS, 1)), blk(SWA_HEADS, nkv * hd), blk(SWA_HEADS, hd), blk(SUBLANES, hd),
                  blk(SUBLANES, nkv * hd), cblk, cblk],
        out_specs=blk(SWA_HEADS, nkv * hd), out_shape=jax.ShapeDtypeStruct((b, SWA_HEADS, nkv * hd), F32),
        compiler_params=_cparams(("parallel",)), name="swa_sample")(
            sinks.reshape(SWA_HEADS, 1), qbd, q.reshape(b, SWA_HEADS, hd), pad(kn.reshape(b, nkv, hd)), vnbd,
            jnp.transpose(cache_k, (0, 1, 3, 4, 2)), jnp.transpose(cache_v, (0, 1, 3, 4, 2)))
    o5 = out.reshape(b, nkv, grp, nkv, hd)
    return jnp.stack([o5[:, h, :, h, :] for h in range(nkv)], axis=1).reshape(b, Q_WIDTH)


def _gelu_tanh(x):
    return 0.5 * x * (1.0 + jnp.tanh(0.7978845608028654 * (x + 0.044715 * x * x * x)))


def _lru_gates(xc, wa_ref, ba, wi_ref, bi, lam):
    xcb = xc.astype(BF16)
    gw = wa_ref.shape[1]
    ra, ia = [], []
    for gi in range(wa_ref.shape[0]):
        xs = xcb[:, gi * gw:(gi + 1) * gw]
        ra.append(_dot(xs, wa_ref[gi]))
        ia.append(_dot(xs, wi_ref[gi]))
    r = _sigmoid(jnp.concatenate(ra, axis=-1) + ba)
    ig = _sigmoid(jnp.concatenate(ia, axis=-1) + bi)
    log_a = -LRU_C * r * _softplus(-lam)
    a = jnp.exp(log_a)
    b = jnp.sqrt(-jnp.tanh(log_a) * (a * a + 1.0)) * (ig * xc)
    return a, b


def _shift_rows(ext, s, tm):
    return pltpu.roll(ext, s, 0)[SUBLANES:SUBLANES + tm]


def _lru_prompt_kernel(x_ref, win_ref, bin_ref, cw_ref, cb_ref, wa_ref, ba_ref, wi_ref, bi_ref, lam_ref,
                       wo_ref, g_ref, b_ref, o_ref, conv_ref, hl_ref, cx_ref, ch_ref, *, tm):
    i = pl.program_id(1)

    @pl.when(i == 0)
    def _():
        cx_ref[...] = jnp.zeros_like(cx_ref)
        ch_ref[...] = jnp.zeros_like(ch_ref)

    x = x_ref[...]
    xy = _dot(x.astype(BF16), win_ref[...]) + bin_ref[...]
    xb = xy[:, :D]
    y_gate = _gelu_tanh(xy[:, D:])
    ext = jnp.concatenate([cx_ref[...], xb], axis=0)
    cw = cw_ref[...]
    xc = cb_ref[...] + xb * cw[CONV_W - 1:CONV_W]
    for s in range(1, CONV_W):
        xc = xc + _shift_rows(ext, s, tm) * cw[CONV_W - 1 - s:CONV_W - s]
    cx_ref[...] = xb[tm - SUBLANES:]
    conv_ref[0] = xb[tm - SUBLANES:]

    a, b = _lru_gates(xc, wa_ref, ba_ref[...], wi_ref, bi_ref[...], lam_ref[...])
    sub = lax.broadcasted_iota(jnp.int32, (tm, 1), 0) % SUBLANES
    s = 1
    while s < SUBLANES:
        keep = sub >= s
        a_sh = jnp.where(keep, pltpu.roll(a, s, 0), 1.0)
        b_sh = jnp.where(keep, pltpu.roll(b, s, 0), 0.0)
        b = a * b_sh + b
        a = a * a_sh
        s *= 2
    carry = ch_ref[SUBLANES - 1:SUBLANES, :]
    groups = []
    for gi in range(tm // SUBLANES):
        rows = slice(gi * SUBLANES, (gi + 1) * SUBLANES)
        hg = a[rows] * carry + b[rows]
        groups.append(hg)
        carry = hg[SUBLANES - 1:SUBLANES]
    h = jnp.concatenate(groups, axis=0)
    ch_ref[...] = h[tm - SUBLANES:]
    hl_ref[0] = h[tm - SUBLANES:]
    acc = _dot((h * y_gate).astype(BF16), wo_ref[...])
    o_ref[...] = _ln(ALPHA * x + acc, g_ref[...], b_ref[...])


def _lru_weights(w_in, b_in, conv_w, conv_b, w_a, b_a, w_i, b_i, lam, w_o):
    gsz = 4
    ng = LRU_BLOCKS // gsz
    bw = D // LRU_BLOCKS

    def grouped(w):
        w4 = w.reshape(ng, gsz, bw, bw)
        return jnp.einsum('gaij,ab->gaibj', w4, jnp.eye(gsz, dtype=w.dtype)).reshape(ng, gsz * bw, gsz * bw).astype(BF16)

    row = lambda v: v.reshape(1, -1)
    return (w_in.astype(BF16), row(b_in), conv_w, row(conv_b), grouped(w_a), row(b_a), grouped(w_i), row(b_i),
            row(lam), w_o.astype(BF16))


def _lru_prompt(x, wts, g, b, n_seq, tm):
    t = x.shape[0]
    nb = t // n_seq // tm
    row = lambda n, i: (n * nb + i, 0)
    last = pl.BlockSpec((1, SUBLANES, D), lambda n, i: (n, 0, 0))
    w_in, b_in, cw, cb, wa, ba, wi, bi, lam, wo = wts
    return pl.pallas_call(
        functools.partial(_lru_prompt_kernel, tm=tm), grid=(n_seq, nb),
        in_specs=[pl.BlockSpec((tm, D), row), _full(w_in.shape), _full(b_in.shape), _full(cw.shape), _full(cb.shape),
                  _full(wa.shape), _full(ba.shape), _full(wi.shape), _full(bi.shape), _full(lam.shape),
                  _full(wo.shape), _full((1, D)), _full((1, D))],
        out_specs=[pl.BlockSpec((tm, D), row), last, last],
        out_shape=[jax.ShapeDtypeStruct((t, D), F32), jax.ShapeDtypeStruct((n_seq, SUBLANES, D), F32),
                   jax.ShapeDtypeStruct((n_seq, SUBLANES, D), F32)],
        scratch_shapes=[pltpu.VMEM((SUBLANES, D), F32), pltpu.VMEM((SUBLANES, D), F32)],
        compiler_params=_cparams(("parallel", "arbitrary")), name="lru_prompt")(x, *wts, g, b)


def _lru_sample_kernel(x_ref, c0_ref, c1_ref, c2_ref, h0_ref, win_ref, bin_ref, cw_ref, cb_ref, wa_ref, ba_ref,
                       wi_ref, bi_ref, lam_ref, wo_ref, g_ref, b_ref, o_ref, xb_ref, h_ref):
    x = x_ref[...]
    xy = _dot(x.astype(BF16), win_ref[...]) + bin_ref[...]
    xb = xy[:, :D]
    y_gate = _gelu_tanh(xy[:, D:])
    cw = cw_ref[...]
    xc = (cb_ref[...] + c0_ref[...] * cw[0:1] + c1_ref[...] * cw[1:2] + c2_ref[...] * cw[2:3] + xb * cw[3:4])
    a, b = _lru_gates(xc, wa_ref, ba_ref[...], wi_ref, bi_ref[...], lam_ref[...])
    h = a * h0_ref[...] + b
    xb_ref[...] = xb
    h_ref[...] = h
    acc = _dot((h * y_gate).astype(BF16), wo_ref[...])
    o_ref[...] = _ln(ALPHA * x + acc, g_ref[...], b_ref[...])


def _lru_sample(x, conv_state, h0, wts, g, b):
    t = x.shape[0]
    args = (x, conv_state[:, 0], conv_state[:, 1], conv_state[:, 2], h0, *wts, g, b)
    sd = jax.ShapeDtypeStruct((t, D), F32)
    return pl.pallas_call(
        _lru_sample_kernel, grid=(1,),
        in_specs=[_full(a.shape) for a in args],
        out_specs=[_full((t, D))] * 3, out_shape=[sd, sd, sd],
        compiler_params=_cparams(("arbitrary",)), name="lru_sample")(*args)


def _rwkv_pre_kernel(x_ref, xp_ref, mu_ref, wr_ref, wk_ref, wv_ref, w0_ref, w1_ref, w2_ref, a0_ref, a1_ref, a2_ref,
                     g1_ref, g2_ref, r_ref, k_ref, v_ref, a_ref, ld_ref, g_ref, *scratch, tm, seq):
    x = x_ref[...]
    if seq:
        cx_ref, = scratch
        i = pl.program_id(1)

        @pl.when(i == 0)
        def _():
            cx_ref[...] = xp_ref[0]

        x_prev = _shift_rows(jnp.concatenate([cx_ref[...], x], axis=0), 1, tm)
        cx_ref[...] = x[tm - SUBLANES:]
    else:
        x_prev = xp_ref[...]
    xx = x_prev - x
    mu = mu_ref[...]
    mix = lambda j: (x + xx * mu[j:j + 1]).astype(BF16)
    r_ref[...] = _dot(mix(0), wr_ref[...]).astype(r_ref.dtype)
    wl = _dot(jnp.tanh(_dot(mix(1), w1_ref[...])).astype(BF16), w2_ref[...])
    w = -_softplus(-(w0_ref[...] + wl)) - 0.5
    ld_ref[...] = -jnp.exp(w)
    k_ref[...] = _dot(mix(2), wk_ref[...]).astype(k_ref.dtype)
    v_ref[...] = _dot(mix(3), wv_ref[...]).astype(v_ref.dtype)
    al = _dot(_dot(mix(4), a1_ref[...]).astype(BF16), a2_ref[...])
    a_ref[...] = _sigmoid(a0_ref[...] + al).astype(a_ref.dtype)
    g_ref[...] = _dot(_sigmoid(_dot(mix(5), g1_ref[...])).astype(BF16), g2_ref[...]).astype(g_ref.dtype)


def _rwkv_pre(x, x_prev, wts, n_seq, tm, seq, dtype):
    t = x.shape[0]
    nb = t // n_seq // tm
    row = lambda n, i: (n * nb + i, 0)
    xp_spec = pl.BlockSpec((1, SUBLANES, D), lambda n, i: (n, 0, 0)) if seq else pl.BlockSpec((tm, D), row)
    sd = lambda dt: jax.ShapeDtypeStruct((t, D), dt)
    blk = pl.BlockSpec((tm, D), row)
    return pl.pallas_call(
        functools.partial(_rwkv_pre_kernel, tm=tm, seq=seq), grid=(n_seq, nb),
        in_specs=[blk, xp_spec] + [_full(w.shape) for w in wts],
        out_specs=[blk] * 6,
        out_shape=[sd(dtype), sd(dtype), sd(dtype), sd(dtype), sd(F32), sd(dtype)],
        scratch_shapes=[pltpu.VMEM((SUBLANES, D), F32)] if seq else [],
        compiler_params=_cparams(("parallel", "arbitrary")), name="rwkv_pre")(x, x_prev, *wts)


def _seg_sum(x, first):
    s0 = jnp.sum(jnp.where(first, x, 0.0), axis=-1, keepdims=True)
    s1 = jnp.sum(jnp.where(first, 0.0, x), axis=-1, keepdims=True)
    return jnp.where(first, s0, s1)


def _wkv_kernel(r_ref, k_ref, v_ref, a_ref, ld_ref, g_ref, kk_ref, ka_ref, rk_ref, gg_ref, gb_ref,
                o_ref, s_ref, st_ref):
    c = pl.program_id(1)
    L = WKV_CHUNK
    P2 = 2 * L
    nch = r_ref.shape[0] // L

    @pl.when(c == 0)
    def _():
        st_ref[...] = jnp.zeros_like(st_ref)

    ld_all = ld_ref[...]
    tri = (lax.broadcasted_iota(jnp.int32, (L, L), 0) >= lax.broadcasted_iota(jnp.int32, (L, L), 1)).astype(BF16)
    hi = ld_all.astype(BF16)
    r1 = ld_all - hi.astype(F32)
    mid = r1.astype(BF16)
    lo = (r1 - mid.astype(F32)).astype(BF16)
    chunk_rows = [slice(ci * L, (ci + 1) * L) for ci in range(nch)]
    cum_ch = [_dot(tri, hi[rw]) + _dot(tri, mid[rw]) + _dot(tri, lo[rw]) for rw in chunk_rows]

    lane = lax.broadcasted_iota(jnp.int32, (1, LANES), 1)
    first = lane < RWKV_HD
    ri = lax.broadcasted_iota(jnp.int32, (P2, P2), 0)
    ci_ = lax.broadcasted_iota(jnp.int32, (P2, P2), 1)
    same_head = (ri // L) == (ci_ // L)
    rt, ct = ri % L, ci_ % L
    strict = jnp.logical_and(same_head, rt > ct)
    incl = jnp.logical_and(same_head, rt >= ct)
    eye = ri == ci_

    def stack(xv):
        return jnp.concatenate([jnp.where(first, xv, 0.0), jnp.where(first, 0.0, xv)], axis=0).astype(BF16)

    npair = RWKV_HEADS // 2
    combos = [(ci, p) for ci in range(nch) for p in range(npair)]
    n = range(len(combos))
    sls = [slice(p * LANES, (p + 1) * LANES) for p in range(npair)]
    ws, us, ks, rs, ul, kl, vs, g_l, bonus = ([] for _ in range(9))
    for ci, p in combos:
        rw, sl = chunk_rows[ci], sls[p]
        rp, kp, vp, ap = (ref[rw, sl].astype(F32) for ref in (r_ref, k_ref, v_ref, a_ref))
        ldp, cum = ld_all[rw, sl], cum_ch[ci][:, sl]
        kk = kp * kk_ref[:, sl]
        kk = kk / jnp.maximum(jnp.sqrt(_seg_sum(kk * kk, first)), 1e-12)
        kmod = kp * (1.0 + (ap - 1.0) * ka_ref[:, sl])
        bp = kk * ap
        cum_l = cum[L - 1:L, :]
        g_inv = jnp.exp(-cum)
        g_to_end = jnp.exp(cum_l - cum)
        ws.append(stack(kk * jnp.exp(cum - ldp)))
        us.append(stack(bp * g_inv))
        ks.append(stack(kmod * g_inv))
        rs.append(stack(rp * jnp.exp(cum)))
        ul.append(stack(bp * g_to_end))
        kl.append(stack(kmod * g_to_end))
        vs.append(stack(vp))
        g_l.append(jnp.exp(cum_l))
        bonus.append(_seg_sum(rp * kmod * rk_ref[:, sl], first) * vp)

    gram = [_dot_nt(jnp.concatenate([ws[q], rs[q]], axis=0), jnp.concatenate([us[q], ks[q]], axis=0)) for q in n]
    n_mat = [jnp.where(strict, gram[q][:P2, :P2], 0.0) for q in n]
    m_mat = [jnp.where(strict, gram[q][:P2, P2:], 0.0).astype(BF16) for q in n]
    nr_mat = [jnp.where(incl, gram[q][P2:, :P2], 0.0).astype(BF16) for q in n]
    mr_mat = [jnp.where(incl, gram[q][P2:, P2:], 0.0).astype(BF16) for q in n]

    def level_mask(sz):
        sub = jnp.logical_and((rt // sz) % 2 == 1, (ct // sz) % 2 == 0)
        return jnp.logical_and(jnp.logical_and(sub, (rt // (2 * sz)) == (ct // (2 * sz))), same_head)

    x_inv = [jnp.where(eye, 1.0, 0.0) - jnp.where(level_mask(1), n_mat[q], 0.0) for q in n]
    sz = 2
    while sz < L:
        mask = level_mask(sz)
        xb = [x_inv[q].astype(BF16) for q in n]
        xc = [_dot(xb[q], jnp.where(mask, n_mat[q], 0.0).astype(BF16)).astype(BF16) for q in n]
        x_inv = [x_inv[q] - _dot(xc[q], xb[q]) for q in n]
        sz *= 2
    x_inv = [x_inv[q].astype(BF16) for q in n]

    state = [st_ref[p] for p in range(npair)]
    inv_n = 1.0 / RWKV_HD
    for ci in range(nch):
        qs = [ci * npair + p for p in range(npair)]
        a0b = [state[p].astype(BF16) for p in range(npair)]
        rhs = [_dot(jnp.concatenate([ws[q], m_mat[q]], axis=1), jnp.concatenate([a0b[p], vs[q]], axis=0)).astype(BF16)
               for p, q in enumerate(qs)]
        pm = [(-_dot(x_inv[q], rhs[p])).astype(BF16) for p, q in enumerate(qs)]
        o_st = [_dot(jnp.concatenate([rs[q], nr_mat[q], mr_mat[q]], axis=1),
                     jnp.concatenate([a0b[p], pm[p], vs[q]], axis=0)) for p, q in enumerate(qs)]
        new_state = []
        for p, q in enumerate(qs):
            g_col = jnp.sum(jnp.where(eye, jnp.broadcast_to(g_l[q], (P2, P2)), 0.0), axis=-1, keepdims=True)
            new_state.append(g_col * state[p] + _dot_tn(jnp.concatenate([ul[q], kl[q]], axis=0),
                                                        jnp.concatenate([pm[p], vs[q]], axis=0)))
        state = new_state
        for p, q in enumerate(qs):
            sl = sls[p]
            o = o_st[p][:L] + o_st[p][L:]
            mu = _seg_sum(o, first) * inv_n
            oc = o - mu
            var = _seg_sum(oc * oc, first) * inv_n
            on = oc * lax.rsqrt(var + RWKV_GN_EPS) * gg_ref[:, sl] + gb_ref[:, sl]
            o_ref[chunk_rows[ci], sl] = ((on + bonus[q]) * g_ref[chunk_rows[ci], sl].astype(F32)).astype(o_ref.dtype)

    for p in range(npair):
        st_ref[p] = state[p]
    s_ref[0] = st_ref[...]


def _wkv_prompt(r, k, v, a, ld, g, hp, n_seq, nch):
    t = r.shape[0]
    L = WKV_CHUNK * nch
    nc = t // n_seq // L
    row = lambda n, c: (n * nc + c, 0)
    blk = pl.BlockSpec((L, D), row)
    npair = RWKV_HEADS // 2
    return pl.pallas_call(
        _wkv_kernel, grid=(n_seq, nc),
        in_specs=[blk] * 6 + [_full((1, D))] * 5,
        out_specs=[blk, pl.BlockSpec((1, npair, LANES, LANES), lambda n, c: (n, 0, 0, 0))],
        out_shape=[jax.ShapeDtypeStruct((t, D), BF16), jax.ShapeDtypeStruct((n_seq, npair, LANES, LANES), F32)],
        scratch_shapes=[pltpu.VMEM((npair, LANES, LANES), F32)],
        compiler_params=_cparams(("parallel", "arbitrary")), name="wkv_chunk")(r, k, v, a, ld, g, *hp)


def _wkv_sample_kernel(r_ref, k_ref, v_ref, a_ref, ld_ref, g_ref, s_ref, kk_ref, ka_ref, rk_ref, gg_ref, gb_ref,
                       o_ref, so_ref):
    r, k, v, a, ld, g = (ref[0] for ref in (r_ref, k_ref, v_ref, a_ref, ld_ref, g_ref))
    kk = k * kk_ref[0]
    kk = kk / jnp.maximum(jnp.sqrt(jnp.sum(kk * kk, axis=0, keepdims=True)), 1e-12)
    kmod = k * (1.0 + (a - 1.0) * ka_ref[0])
    akk = kk * a
    decay = jnp.exp(ld)

    def value_row(vi, carry):
        s = s_ref[0, vi]
        skk = jnp.sum(s * kk, axis=0, keepdims=True)
        s_new = s * decay - skk * akk + v_ref[0, pl.ds(vi, 1), :] * kmod
        so_ref[0, vi] = s_new
        o_ref[0, pl.ds(vi, 1), :] = jnp.sum(s_new * r, axis=0, keepdims=True)
        return carry

    lax.fori_loop(0, s_ref.shape[1], value_row, 0, unroll=4)
    o = o_ref[0]
    mu = jnp.mean(o, axis=0, keepdims=True)
    oc = o - mu
    var = jnp.mean(oc * oc, axis=0, keepdims=True)
    on = oc * lax.rsqrt(var + RWKV_GN_EPS) * gg_ref[0] + gb_ref[0]
    bonus = jnp.sum(r * kmod * rk_ref[0], axis=0, keepdims=True) * v
    o_ref[0] = (on + bonus) * g


def _wkv_sample(r, k, v, a, ld, g, state, hp):
    b = r.shape[0]
    nh, hd = RWKV_HEADS, RWKV_HD
    t3 = lambda z: jnp.transpose(z.reshape(b, nh, hd), (1, 2, 0))
    vec = pl.BlockSpec((1, hd, b), lambda h: (h, 0, 0))
    par = pl.BlockSpec((1, hd, 1), lambda h: (h, 0, 0))
    sblk = pl.BlockSpec((1, hd, hd, b), lambda h: (h, 0, 0, 0))
    o, s_new = pl.pallas_call(
        _wkv_sample_kernel, grid=(nh,),
        in_specs=[vec] * 6 + [sblk] + [par] * 5,
        out_specs=[vec, sblk],
        out_shape=[jax.ShapeDtypeStruct((nh, hd, b), F32), jax.ShapeDtypeStruct((nh, hd, hd, b), F32)],
        compiler_params=_cparams(("parallel",)), name="wkv_sample")(
            t3(r), t3(k), t3(v), t3(a), t3(ld), t3(g), jnp.transpose(state, (1, 2, 3, 0)),
            *[z.reshape(nh, hd, 1) for z in hp])
    return jnp.transpose(o, (2, 0, 1)).reshape(b, D), jnp.transpose(s_new, (3, 0, 1, 2))


def _mem_prompt_kernel(x_ref, wq_ref, mk_ref, mv_ref, wo_ref, g_ref, b_ref, o_ref):
    x = x_ref[...]
    q = _dot(x.astype(BF16), wq_ref[...]).astype(BF16)
    scale = MEM_HD ** -0.5
    sls = [slice(h * MEM_HD, (h + 1) * MEM_HD) for h in range(MEM_HEADS)]
    s = [_dot_nt(q[:, sl], mk_ref[0, :, sl]) * scale for sl in sls]
    p = [jnp.exp(sh - jnp.max(sh, axis=-1, keepdims=True)) for sh in s]
    den = [jnp.sum(ph, axis=-1, keepdims=True) for ph in p]
    outs = [_dot((ph / dh).astype(BF16), mv_ref[0, :, sl]).astype(BF16) for ph, dh, sl in zip(p, den, sls)]
    acc = _dot(jnp.concatenate(outs, axis=-1), wo_ref[...])
    o_ref[...] = _ln(ALPHA * x + acc, g_ref[...], b_ref[...])


def _mem_attn_prompt(x, w_q, mk, mv, w_o, g, b, n_seq, tm):
    t = x.shape[0]
    nb = t // n_seq // tm
    m = mk.shape[1]
    row = lambda n, i: (n * nb + i, 0)
    mem = pl.BlockSpec((1, m, D), lambda n, i: (n, 0, 0))
    return pl.pallas_call(
        _mem_prompt_kernel, grid=(n_seq, nb),
        in_specs=[pl.BlockSpec((tm, D), row), _full((D, D)), mem, mem, _full((D, D)), _full((1, D)), _full((1, D))],
        out_specs=pl.BlockSpec((tm, D), row), out_shape=jax.ShapeDtypeStruct((t, D), F32),
        compiler_params=_cparams(("parallel", "arbitrary")), name="mem_attn")(x, w_q, mk, mv, w_o, g, b)


def _mem_sample_kernel(q_ref, ck_ref, cv_ref, o_ref, *, bs):
    m, nh, hd = ck_ref.shape[1:]
    rows = q_ref.shape[1]
    col_head = lax.broadcasted_iota(jnp.int32, (rows, m * nh), 1) % nh
    own = col_head == lax.broadcasted_iota(jnp.int32, (rows, m * nh), 0)
    scale = MEM_HD ** -0.5
    nb = range(bs)
    s = [jnp.where(own, _dot_nt(q_ref[b].astype(BF16), ck_ref[b].reshape(m * nh, hd).astype(BF16)) * scale, NEG_INF)
         for b in nb]
    p = [jnp.where(own, jnp.exp(s[b] - jnp.max(s[b], axis=-1, keepdims=True)), 0.0) for b in nb]
    den = [jnp.sum(p[b], axis=-1, keepdims=True) for b in nb]
    for b in nb:
        pb = (p[b] / jnp.where(den[b] > 0.0, den[b], 1.0)).astype(BF16)
        o_ref[b] = _dot(pb, cv_ref[b].reshape(m * nh, hd).astype(BF16))


def _mem_attn_sample(q, cache_k, cache_v, layer, bs):
    _, b, m, nh, hd = cache_k.shape
    q3 = jnp.pad(q.reshape(b, nh, hd), ((0, 0), (0, SUBLANES - nh), (0, 0)))
    qb = pl.BlockSpec((bs, SUBLANES, hd), lambda i: (i, 0, 0))
    cb = pl.BlockSpec((None, bs, m, nh, hd), lambda i: (layer, i, 0, 0, 0))
    out = pl.pallas_call(
        functools.partial(_mem_sample_kernel, bs=bs), grid=(b // bs,), in_specs=[qb, cb, cb], out_specs=qb,
        out_shape=jax.ShapeDtypeStruct((b, SUBLANES, hd), F32),
        compiler_params=_cparams(("parallel",)), name="mem_sample")(q3, cache_k, cache_v)
    return out[:, :nh].reshape(b, D)


_PAIRS = ((0, 1), (0, 2), (0, 3), (1, 2), (1, 3), (2, 3))


def _router_kernel(x_ref, rw_ref, rb_ref, bucket_ref, rank_ref, cnt_ref, base_ref, *, tm):
    i = pl.program_id(0)

    @pl.when(i == 0)
    def _():
        base_ref[...] = jnp.zeros_like(base_ref)

    logits = _dot_nt(rw_ref[...], x_ref[...].astype(BF16))
    e = jnp.exp(logits - jnp.max(logits, axis=0, keepdims=True))
    sel = e / jnp.sum(e, axis=0, keepdims=True) + rb_ref[...]
    s = [sel[j:j + 1, :] for j in range(N_EXPERTS)]
    neg = jnp.float32(-jnp.inf)

    best = jnp.zeros((1, tm), jnp.int32)
    best_score = None
    for gi in range(N_GROUPS):
        s0, s1, s2, s3 = s[4 * gi:4 * gi + 4]
        hi01, lo01, hi23, lo23 = jnp.maximum(s0, s1), jnp.minimum(s0, s1), jnp.maximum(s2, s3), jnp.minimum(s2, s3)
        score = jnp.maximum(hi01, hi23) + jnp.maximum(jnp.minimum(hi01, hi23), jnp.maximum(lo01, lo23))
        if gi == 0:
            best_score = score
        else:
            take = score > best_score
            best = jnp.where(take, gi, best)
            best_score = jnp.where(take, score, best_score)
    vals = []
    for j in range(EXPERTS_PER_GROUP):
        vj = s[j]
        for gi in range(1, N_GROUPS):
            vj = jnp.where(best == gi, s[4 * gi + j], vj)
        vals.append(vj)

    def argmax4(v):
        idx, mx = jnp.zeros((1, tm), jnp.int32), v[0]
        for j in range(1, EXPERTS_PER_GROUP):
            take = v[j] > mx
            idx = jnp.where(take, j, idx)
            mx = jnp.where(take, v[j], mx)
        return idx

    i1 = argmax4(vals)
    i2 = argmax4([jnp.where(i1 == j, neg, vals[j]) for j in range(EXPERTS_PER_GROUP)])
    lo, hi = jnp.minimum(i1, i2), jnp.maximum(i1, i2)
    pair = jnp.zeros((1, tm), jnp.int32)
    for pi, (pa, pb) in enumerate(_PAIRS):
        pair = jnp.where(jnp.logical_and(lo == pa, hi == pb), pi, pair)
    bucket = best * len(_PAIRS) + pair
    bucket_ref[0] = bucket

    onehot = (lax.broadcasted_iota(jnp.int32, (BUCKET_ROWS, tm), 0) == bucket).astype(F32)
    upper = (lax.broadcasted_iota(jnp.int32, (tm, tm), 0) <= lax.broadcasted_iota(jnp.int32, (tm, tm), 1)).astype(BF16)
    cum = _dot(onehot.astype(BF16), upper)
    base = base_ref[...]
    rank = jnp.sum(onehot * (cum + base), axis=0, keepdims=True) - 1.0
    rank_ref[0] = rank.astype(jnp.int32)
    base = base + jnp.sum(onehot, axis=1, keepdims=True)
    base_ref[...] = base
    cnt_ref[...] = jnp.broadcast_to(base, cnt_ref.shape)


def _router(x, rw_t, rb, tm):
    t = x.shape[0]
    nb = t // tm
    ib = pl.BlockSpec((1, 1, tm), lambda i: (i, 0, 0))
    bucket, rank, cnt = pl.pallas_call(
        functools.partial(_router_kernel, tm=tm), grid=(nb,),
        in_specs=[pl.BlockSpec((tm, D), lambda i: (i, 0)), _full(rw_t.shape), _full(rb.shape)],
        out_specs=[ib, ib, _full((BUCKET_ROWS, LANES))],
        out_shape=[jax.ShapeDtypeStruct((nb, 1, tm), jnp.int32), jax.ShapeDtypeStruct((nb, 1, tm), jnp.int32),
                   jax.ShapeDtypeStruct((BUCKET_ROWS, LANES), F32)],
        scratch_shapes=[pltpu.VMEM((BUCKET_ROWS, 1), F32)],
        compiler_params=_cparams(("arbitrary",)), name="router")(x, rw_t, rb)
    return bucket.reshape(t), rank.reshape(t), cnt[:N_BUCKETS, 0].astype(jnp.int32)


_ROW_GROUP = 8


def _row_copies(idx_ref, base, src_hbm, dst, sem, n, wait):
    def body(j, carry):
        for k in range(_ROW_GROUP):
            r = j * _ROW_GROUP + k
            cp = pltpu.make_async_copy(src_hbm.at[pl.ds(idx_ref[base + r], 1)], dst.at[pl.ds(r, 1)], sem)
            if wait:
                cp.wait()
            else:
                cp.start(priority=k % 2)
        return carry

    lax.fori_loop(0, n // _ROW_GROUP, body, 0)


def _ffn_kernel(src_ref, lo_ref, hi_ref, nrow_ref, used_ref, x_hbm, rw_ref, g0_ref, u0_ref, d0_ref, g1_ref, u1_ref,
                d1_ref, o_ref, xbuf, sem, *, blk):
    i = pl.program_id(0)
    used = used_ref[0]
    slot = i % 2

    @pl.when(jnp.logical_and(i == 0, used > 0))
    def _():
        xbuf[...] = jnp.zeros_like(xbuf)
        _row_copies(src_ref, 0, x_hbm, xbuf.at[0], sem.at[0], nrow_ref[0], False)

    @pl.when(i + 1 < used)
    def _():
        _row_copies(src_ref, (i + 1) * blk, x_hbm, xbuf.at[1 - slot], sem.at[1 - slot], nrow_ref[i + 1], False)

    @pl.when(i < used)
    def _():
        _row_copies(src_ref, i * blk, x_hbm, xbuf.at[slot], sem.at[slot], nrow_ref[i], True)
        xb = xbuf[slot].astype(BF16)
        logits = _dot(xb, rw_ref[...])
        lane = lax.broadcasted_iota(jnp.int32, logits.shape, 1)
        l_lo = jnp.sum(jnp.where(lane == lo_ref[i], logits, 0.0), axis=-1, keepdims=True)
        l_hi = jnp.sum(jnp.where(lane == hi_ref[i], logits, 0.0), axis=-1, keepdims=True)
        w_lo = _sigmoid(l_lo - l_hi)

        def expert(g_ref, u_ref, d_ref):
            gate = _dot(xb, g_ref[0])
            act = gate * _sigmoid(gate) * _dot(xb, u_ref[0])
            return _dot(act.astype(BF16), d_ref[0])

        y_lo = expert(g0_ref, u0_ref, d0_ref)
        y_hi = expert(g1_ref, u1_ref, d1_ref)
        o_ref[...] = w_lo * y_lo + (1.0 - w_lo) * y_hi

    @pl.when(i >= used)
    def _():
        o_ref[...] = jnp.zeros_like(o_ref)


def _ffn(x, src, blk_lo, blk_hi, blk_rows, n_used, rw, w_gate, w_up, w_down, layer, blk):
    rows = src.shape[0]
    nblk = rows // blk
    wg = lambda sel: pl.BlockSpec((None, 1, D, EXPERT_FF),
                                  lambda i, s, lo, hi, nr, used: (layer, (lo, hi)[sel][i], 0, 0))
    wd = lambda sel: pl.BlockSpec((None, 1, EXPERT_FF, D),
                                  lambda i, s, lo, hi, nr, used: (layer, (lo, hi)[sel][i], 0, 0))
    return pl.pallas_call(
        functools.partial(_ffn_kernel, blk=blk),
        grid_spec=pltpu.PrefetchScalarGridSpec(
            num_scalar_prefetch=5, grid=(nblk,),
            in_specs=[pl.BlockSpec(memory_space=pl.ANY),
                      pl.BlockSpec(rw.shape, lambda i, s, lo, hi, nr, used: (0, 0)),
                      wg(0), wg(0), wd(0), wg(1), wg(1), wd(1)],
            out_specs=pl.BlockSpec((blk, D), lambda i, s, lo, hi, nr, used: (i, 0)),
            scratch_shapes=[pltpu.VMEM((2, blk, D), F32), pltpu.SemaphoreType.DMA((2,))]),
        out_shape=jax.ShapeDtypeStruct((rows, D), F32),
        compiler_params=_cparams(("arbitrary",)), name="moe_ffn")(
            src, blk_lo, blk_hi, blk_rows, n_used, x, rw, w_gate, w_up, w_down, w_gate, w_up, w_down)


def _combine_ln_kernel(dest_ref, x_ref, y_hbm, g_ref, b_ref, o_ref, ybuf, sem, *, tm):
    i = pl.program_id(0)
    slot = i % 2

    @pl.when(i == 0)
    def _():
        _row_copies(dest_ref, 0, y_hbm, ybuf.at[0], sem.at[0], tm, False)

    @pl.when(i + 1 < pl.num_programs(0))
    def _():
        _row_copies(dest_ref, (i + 1) * tm, y_hbm, ybuf.at[1 - slot], sem.at[1 - slot], tm, False)

    _row_copies(dest_ref, i * tm, y_hbm, ybuf.at[slot], sem.at[slot], tm, True)
    o_ref[...] = _ln(ALPHA * x_ref[...] + ybuf[slot], g_ref[...], b_ref[...])


def _combine_ln(x, y_rows, dest, g, b, tm):
    t = x.shape[0]
    rowb = pl.BlockSpec((tm, D), lambda i, d: (i, 0))
    vec = pl.BlockSpec((1, D), lambda i, d: (0, 0))
    return pl.pallas_call(
        functools.partial(_combine_ln_kernel, tm=tm),
        grid_spec=pltpu.PrefetchScalarGridSpec(
            num_scalar_prefetch=1, grid=(t // tm,),
            in_specs=[rowb, pl.BlockSpec(memory_space=pl.ANY), vec, vec], out_specs=rowb,
            scratch_shapes=[pltpu.VMEM((2, tm, D), F32), pltpu.SemaphoreType.DMA((2,))]),
        out_shape=jax.ShapeDtypeStruct((t, D), F32),
        compiler_params=_cparams(("arbitrary",)), name="moe_combine_ln")(dest, x, y_rows, g, b)


def _invert_rows_kernel(dest_ref, src_ref):
    def clear(r, carry):
        src_ref[r] = 0
        return carry

    def put(tok, carry):
        src_ref[dest_ref[tok]] = tok
        return carry

    lax.fori_loop(0, src_ref.shape[0], clear, 0, unroll=8)
    lax.fori_loop(0, dest_ref.shape[0], put, 0, unroll=8)


def _invert_rows(dest, rows):
    smem = pl.BlockSpec(memory_space=pltpu.SMEM)
    return pl.pallas_call(
        _invert_rows_kernel, in_specs=[smem], out_specs=smem,
        out_shape=jax.ShapeDtypeStruct((rows,), jnp.int32), name="invert_rows")(dest)


def _moe_ln(x, rw_t, rb, rw_pad, w_gate, w_up, w_down, layer, g, b, tm_router, blk, tm_comb):
    t = x.shape[0]
    bucket, rank, counts = _router(x, rw_t, rb, tm_router)
    padded = (counts + blk - 1) // blk * blk
    ends = jnp.cumsum(padded)
    dest = ((ends - padded)[bucket] + rank).astype(jnp.int32)
    nblk = t // blk + N_BUCKETS
    src = _invert_rows(dest, nblk * blk)
    blk_bucket = jnp.minimum(jnp.searchsorted(ends, jnp.arange(nblk) * blk, side='right'), N_BUCKETS - 1)
    pair_lo = jnp.array([p[0] for p in _PAIRS], jnp.int32)
    pair_hi = jnp.array([p[1] for p in _PAIRS], jnp.int32)
    grp, pr = blk_bucket // len(_PAIRS), blk_bucket % len(_PAIRS)
    blk_lo = (grp * EXPERTS_PER_GROUP + pair_lo[pr]).astype(jnp.int32)
    blk_hi = (grp * EXPERTS_PER_GROUP + pair_hi[pr]).astype(jnp.int32)
    n_used = (ends[-1:] // blk).astype(jnp.int32)
    bucket_end = (ends - padded + counts)[blk_bucket]
    blk_rows = jnp.clip(bucket_end - jnp.arange(nblk) * blk, 0, blk)
    blk_rows = ((blk_rows + _ROW_GROUP - 1) // _ROW_GROUP * _ROW_GROUP).astype(jnp.int32)
    y_rows = _ffn(x, src, blk_lo, blk_hi, blk_rows, n_used, rw_pad, w_gate, w_up, w_down, layer, blk)
    return _combine_ln(x, y_rows, dest, g, b, tm_comb)


def _moe_dense_kernel(x_ref, lo_ref, hi_ref, rw_ref, wg_ref, wu_ref, wd_ref, g_ref, b_ref, o_ref, acc_ref):
    e = pl.program_id(0)

    @pl.when(e == 0)
    def _():
        acc_ref[...] = jnp.zeros_like(acc_ref)

    x = x_ref[...]
    xb = x.astype(BF16)
    logits = _dot(xb, rw_ref[...])
    lane = lax.broadcasted_iota(jnp.int32, logits.shape, 1)
    lo, hi = lo_ref[...], hi_ref[...]
    l_lo = jnp.sum(jnp.where(lane == lo, logits, 0.0), axis=-1, keepdims=True)
    l_hi = jnp.sum(jnp.where(lane == hi, logits, 0.0), axis=-1, keepdims=True)
    w_lo = _sigmoid(l_lo - l_hi)
    coef = jnp.where(lo == e, w_lo, 0.0) + jnp.where(hi == e, 1.0 - w_lo, 0.0)
    gate = _dot(xb, wg_ref[0])
    act = gate * _sigmoid(gate) * _dot(xb, wu_ref[0])
    acc_ref[...] += coef * _dot(act.astype(BF16), wd_ref[0])

    @pl.when(e == pl.num_programs(0) - 1)
    def _():
        o_ref[...] = _ln(ALPHA * x + acc_ref[...], g_ref[...], b_ref[...])


def _moe_ln_dense(x, rw_t, rb, rw_pad, w_gate, w_up, w_down, layer, g, b):
    t = x.shape[0]
    bucket, _, _ = _router(x, rw_t, rb, t)
    pair_lo = jnp.array([p[0] for p in _PAIRS], jnp.int32)
    pair_hi = jnp.array([p[1] for p in _PAIRS], jnp.int32)
    grp, pr = bucket // len(_PAIRS), bucket % len(_PAIRS)
    lo = (grp * EXPERTS_PER_GROUP + pair_lo[pr]).astype(jnp.int32).reshape(t, 1)
    hi = (grp * EXPERTS_PER_GROUP + pair_hi[pr]).astype(jnp.int32).reshape(t, 1)
    wg = pl.BlockSpec((None, 1, D, EXPERT_FF), lambda e: (layer, e, 0, 0))
    wd = pl.BlockSpec((None, 1, EXPERT_FF, D), lambda e: (layer, e, 0, 0))
    return pl.pallas_call(
        _moe_dense_kernel, grid=(N_EXPERTS,),
        in_specs=[_full((t, D)), _full((t, 1)), _full((t, 1)), _full(rw_pad.shape), wg, wg, wd,
                  _full((1, D)), _full((1, D))],
        out_specs=_full((t, D)), out_shape=jax.ShapeDtypeStruct((t, D), F32),
        scratch_shapes=[pltpu.VMEM((t, D), F32)],
        compiler_params=_cparams(("arbitrary",)), name="moe_dense")(x, lo, hi, rw_pad, w_gate, w_up, w_down, g, b)


def kernel(x_prompt, x_sample, cache_swa_k, cache_swa_v, state_lru_conv, state_lru_h, state_rwkv_shift, state_rwkv_wkv, cache_mem_k, cache_mem_v, mem_prompt, swa_w_qkv, swa_sinks, swa_w_o, lru_w_in, lru_b_in, lru_conv_w, lru_conv_b, lru_w_a, lru_b_a, lru_w_i, lru_b_i, lru_lambda, lru_w_o, rwkv_mu, rwkv_w_r, rwkv_w_k, rwkv_w_v, rwkv_w0, rwkv_w1, rwkv_w2, rwkv_a0, rwkv_a1, rwkv_a2, rwkv_g1, rwkv_g2, rwkv_k_k, rwkv_k_a, rwkv_r_k, rwkv_gn_g, rwkv_gn_b, rwkv_w_o, mem_w_q, mem_w_kv, mem_w_o, ln_g, ln_b, router_w, router_b, moe_w_gate, moe_w_up, moe_w_down):
    n_p, seq, _ = x_prompt.shape
    n_s, dec_seq, _ = x_sample.shape
    assert dec_seq == 1
    past_len = 8192
    xp = x_prompt.reshape(n_p * seq, D)
    xs = x_sample.reshape(n_s, D)
    row = lambda v: v.reshape(1, -1)
    bf = lambda w: w.astype(BF16)

    rw_t = bf(router_w.T)
    rb = router_b.reshape(N_EXPERTS, 1)
    rw_pad = bf(jnp.pad(router_w, ((0, 0), (0, LANES - N_EXPERTS))))
    wg, wu, wd = bf(moe_w_gate), bf(moe_w_up), bf(moe_w_down)
    mem_p = mem_prompt.reshape(n_p * mem_prompt.shape[1], D)
    m_len = mem_prompt.shape[1]

    swa_k_p, swa_v_p, swa_k_s, swa_v_s = [], [], [], []
    lru_c_p, lru_h_p, lru_c_s, lru_h_s = [], [], [], []
    rw_x_p, rw_s_p, rw_x_s, rw_s_s = [], [], [], []
    mem_k_p, mem_v_p = [], []

    for layer in range(DEPTH):
        kind, i = layer % N_MIXERS, layer // N_MIXERS
        g0, b0 = row(ln_g[layer, 0]), row(ln_b[layer, 0])
        if kind == 0:
            w_qkv, w_o = bf(swa_w_qkv[i]), bf(swa_w_o[i])
            keep = min(WINDOW, seq)
            q, k, v, kv_last = _swa_qkv(xp, w_qkv, jnp.arange(seq), n_p, 1024, keep, BF16)
            o = _swa_attn_prompt(q, k, v, swa_sinks[i], n_p, 4)
            swa_k_p.append(kv_last[:, :, :KV_WIDTH].reshape(n_p, keep, SWA_KV_HEADS, HEAD_DIM))
            swa_v_p.append(kv_last[:, :, KV_WIDTH:].reshape(n_p, keep, SWA_KV_HEADS, HEAD_DIM))
            xp = _proj_ln(o, w_o, xp, g0, b0, 1024)

            qs, _, _, kv_new = _swa_qkv(xs, w_qkv, jnp.full((n_s,), past_len), 1, n_s, n_s, F32)
            kn, vn = kv_new[0, :, :KV_WIDTH], kv_new[0, :, KV_WIDTH:]
            os_ = _swa_attn_sample(qs, kn, vn, cache_swa_k, cache_swa_v, i, swa_sinks[i], 8)
            wb = cache_swa_k.shape[2]
            k_all = jnp.concatenate([cache_swa_k[i], kn.reshape(n_s, 1, SWA_KV_HEADS, HEAD_DIM)], axis=1)
            v_all = jnp.concatenate([cache_swa_v[i], vn.reshape(n_s, 1, SWA_KV_HEADS, HEAD_DIM)], axis=1)
            swa_k_s.append(k_all[:, -wb:])
            swa_v_s.append(v_all[:, -wb:])
            xs = _proj_ln(os_, w_o, xs, g0, b0, n_s)
        elif kind == 1:
            wts = _lru_weights(lru_w_in[i], lru_b_in[i], lru_conv_w[i], lru_conv_b[i], lru_w_a[i], lru_b_a[i],
                               lru_w_i[i], lru_b_i[i], lru_lambda[i], lru_w_o[i])
            xp, conv_last, h_last = _lru_prompt(xp, wts, g0, b0, n_p, 256)
            lru_c_p.append(conv_last[:, SUBLANES - (CONV_W - 1):])
            lru_h_p.append(h_last[:, SUBLANES - 1])
            xs, xb_s, h_s = _lru_sample(xs, state_lru_conv[i], state_lru_h[i], wts, g0, b0)
            lru_c_s.append(jnp.concatenate([state_lru_conv[i][:, 1:], xb_s[:, None]], axis=1))
            lru_h_s.append(h_s)
        else:
            wts = (rwkv_mu[i], bf(rwkv_w_r[i]), bf(rwkv_w_k[i]), bf(rwkv_w_v[i]), row(rwkv_w0[i]), bf(rwkv_w1[i]),
                   bf(rwkv_w2[i]), row(rwkv_a0[i]), bf(rwkv_a1[i]), bf(rwkv_a2[i]), bf(rwkv_g1[i]), bf(rwkv_g2[i]))
            hp = (row(rwkv_k_k[i]), row(rwkv_k_a[i]), row(rwkv_r_k[i]), row(rwkv_gn_g[i]), row(rwkv_gn_b[i]))
            w_o = bf(rwkv_w_o[i])
            rw_x_p.append(xp.reshape(n_p, seq, D)[:, -1])
            rw_x_s.append(xs)
            r, k, v, a, ld, g = _rwkv_pre(xp, jnp.zeros((n_p, SUBLANES, D), F32), wts, n_p, 512, True, BF16)
            o, st = _wkv_prompt(r, k, v, a, ld, g, hp, n_p, 4)
            hd = RWKV_HD
            st = jnp.stack([st[:, :, :hd, :hd], st[:, :, hd:, hd:]], axis=2).reshape(n_p, RWKV_HEADS, hd, hd)
            rw_s_p.append(jnp.swapaxes(st, -1, -2))
            xp = _proj_ln(o, w_o, xp, g0, b0, 1024)

            r, k, v, a, ld, g = _rwkv_pre(xs, state_rwkv_shift[i], wts, 1, n_s, False, F32)
            os_, s_new = _wkv_sample(r, k, v, a, ld, g, state_rwkv_wkv[i], hp)
            rw_s_s.append(s_new)
            xs = _proj_ln(os_, w_o, xs, g0, b0, n_s)

        g1, b1 = row(ln_g[layer, 1]), row(ln_b[layer, 1])
        w_q, w_o = bf(mem_w_q[layer]), bf(mem_w_o[layer])
        mkv = _matmul(mem_p, bf(mem_w_kv[layer]), 512)
        mk, mv = mkv[:, :D], mkv[:, D:]
        mem_k_p.append(mk.reshape(n_p, m_len, MEM_HEADS, MEM_HD))
        mem_v_p.append(mv.reshape(n_p, m_len, MEM_HEADS, MEM_HD))
        xp = _mem_attn_prompt(xp, w_q, bf(mk).reshape(n_p, m_len, D), bf(mv).reshape(n_p, m_len, D), w_o, g1, b1,
                              n_p, 1024)
        qs = _matmul(xs, w_q, n_s)
        os_ = _mem_attn_sample(qs, cache_mem_k, cache_mem_v, layer, 4)
        xs = _proj_ln(os_, w_o, xs, g1, b1, n_s)

        g2, b2 = row(ln_g[layer, 2]), row(ln_b[layer, 2])
        xp = _moe_ln(xp, rw_t, rb, rw_pad, wg, wu, wd, layer, g2, b2, 512, 256, 512)
        xs = _moe_ln_dense(xs, rw_t, rb, rw_pad, wg, wu, wd, layer, g2, b2)

    return (xp.reshape(n_p, seq, D), xs.reshape(n_s, 1, D),
            jnp.stack(swa_k_p), jnp.stack(swa_v_p), jnp.stack(lru_c_p), jnp.stack(lru_h_p),
            jnp.stack(rw_x_p), jnp.stack(rw_s_p), jnp.stack(mem_k_p), jnp.stack(mem_v_p),
            jnp.stack(swa_k_s), jnp.stack(swa_v_s), jnp.stack(lru_c_s), jnp.stack(lru_h_s),
            jnp.stack(rw_x_s), jnp.stack(rw_s_s))
```

```python
import functools

import jax
import jax.numpy as jnp
from jax import lax
from jax.experimental import pallas as pl
from jax.experimental.pallas import tpu as pltpu

F32 = jnp.float32
BF16 = jnp.bfloat16

D = 1024
DEPTH = 4
N_MIXERS = 3
HEAD_DIM = 64
SWA_HEADS = D // HEAD_DIM
SWA_KV_HEADS = 4
SWA_GROUP = SWA_HEADS // SWA_KV_HEADS
Q_WIDTH = SWA_HEADS * HEAD_DIM
KV_WIDTH = SWA_KV_HEADS * HEAD_DIM
WINDOW = 128
ROT_DIM = HEAD_DIM // 4
ROPE_THETA = 500000.0
LRU_BLOCKS = 16
CONV_W = 4
LRU_C = 8.0
RWKV_HEADS = 16
RWKV_HD = 64
RWKV_GN_EPS = 64e-5
MEM_HEADS = 4
MEM_HD = D // MEM_HEADS
N_EXPERTS = 16
N_GROUPS = 4
EXPERTS_PER_GROUP = 4
EXPERT_FF = 512
LN_EPS = 1e-5
ALPHA = (2.0 * DEPTH) ** 0.25
NEG_INF = -1e30

LANES = 128
SUBLANES = 8
VMEM_LIMIT = 56 * 1024 * 1024
WKV_CHUNK = 64
N_BUCKETS = N_GROUPS * 6
BUCKET_ROWS = 32


def _cparams(sem):
    return pltpu.CompilerParams(dimension_semantics=sem, vmem_limit_bytes=VMEM_LIMIT)


def _dot(a, b):
    return jnp.dot(a, b, preferred_element_type=F32)


def _dot_nt(a, b):
    return lax.dot_general(a, b, (((1,), (1,)), ((), ())), preferred_element_type=F32)


def _dot_tn(a, b):
    return lax.dot_general(a, b, (((0,), (0,)), ((), ())), preferred_element_type=F32)


def _ln(z, g, b):
    mu = jnp.mean(z, axis=-1, keepdims=True)
    zc = z - mu
    var = jnp.mean(zc * zc, axis=-1, keepdims=True)
    return zc * lax.rsqrt(var + LN_EPS) * g + b


def _softplus(z):
    return jnp.maximum(z, 0.0) + jnp.log1p(jnp.exp(-jnp.abs(z)))


def _sigmoid(z):
    return 0.5 * jnp.tanh(0.5 * z) + 0.5


def _round_bf16(x):
    return x.astype(BF16).astype(F32)


def _full(shape):
    nd = len(shape)
    return pl.BlockSpec(shape, lambda *_: (0,) * nd)


def _mm_kernel(a_ref, w_ref, o_ref):
    o_ref[...] = _dot(a_ref[...].astype(BF16), w_ref[...]).astype(o_ref.dtype)


def _matmul(a, w, tm, out_dtype=F32):
    t, k = a.shape
    n = w.shape[1]
    return pl.pallas_call(
        _mm_kernel, grid=(t // tm,),
        in_specs=[pl.BlockSpec((tm, k), lambda i: (i, 0)), _full((k, n))],
        out_specs=pl.BlockSpec((tm, n), lambda i: (i, 0)),
        out_shape=jax.ShapeDtypeStruct((t, n), out_dtype),
        compiler_params=_cparams(("parallel",)), name="matmul")(a, w)


def _proj_ln_kernel(a_ref, w_ref, x_ref, g_ref, b_ref, o_ref):
    acc = _dot(a_ref[...].astype(BF16), w_ref[...])
    o_ref[...] = _ln(ALPHA * x_ref[...] + acc, g_ref[...], b_ref[...])


def _proj_ln(a, w, x, g, b, tm):
    t, k = a.shape
    return pl.pallas_call(
        _proj_ln_kernel, grid=(t // tm,),
        in_specs=[pl.BlockSpec((tm, k), lambda i: (i, 0)), _full((k, D)),
                  pl.BlockSpec((tm, D), lambda i: (i, 0)), _full((1, D)), _full((1, D))],
        out_specs=pl.BlockSpec((tm, D), lambda i: (i, 0)),
        out_shape=jax.ShapeDtypeStruct((t, D), F32),
        compiler_params=_cparams(("parallel",)), name="proj_ln")(a, w, x, g, b)


def _rope_tables(pos):
    half = ROT_DIM // 2
    inv_freq = ROPE_THETA ** (-jnp.arange(half, dtype=F32) / half)
    ang = pos.astype(F32)[:, None] * inv_freq
    cos, sin = jnp.cos(ang), jnp.sin(ang)
    one = jnp.ones((pos.shape[0], HEAD_DIM - ROT_DIM), F32)
    zero = jnp.zeros((pos.shape[0], HEAD_DIM - ROT_DIM), F32)
    zh = jnp.zeros_like(sin)
    c = jnp.concatenate([cos, cos, one], axis=1)
    s1 = jnp.concatenate([-sin, zh, zero], axis=1)
    s2 = jnp.concatenate([zh, sin, zero], axis=1)
    rep = LANES // HEAD_DIM
    return jnp.tile(c, (1, rep)), jnp.tile(s1, (1, rep)), jnp.tile(s2, (1, rep))


def _swa_qkv_kernel(x_ref, w_ref, c_ref, s1_ref, s2_ref, q_ref, k_ref, v_ref, kv_ref, *, tm, keep):
    acc = _dot(x_ref[...].astype(BF16), w_ref[...])
    c, s1, s2 = c_ref[...], s1_ref[...], s2_ref[...]
    half = ROT_DIM // 2
    n_q = Q_WIDTH // LANES
    n_k = KV_WIDTH // LANES
    for cg in range(n_q + n_k):
        xg = acc[:, cg * LANES:(cg + 1) * LANES]
        rot = xg * c + pltpu.roll(xg, LANES - half, 1) * s1 + pltpu.roll(xg, half, 1) * s2
        if cg < n_q:
            q_ref[:, cg * LANES:(cg + 1) * LANES] = rot.astype(q_ref.dtype)
        else:
            ck = cg - n_q
            k_ref[:, ck * LANES:(ck + 1) * LANES] = rot.astype(k_ref.dtype)
            kv_ref[0, :, ck * LANES:(ck + 1) * LANES] = rot[tm - keep:, :]
    v = acc[:, Q_WIDTH + KV_WIDTH:]
    v_ref[...] = v.astype(v_ref.dtype)
    kv_ref[0, :, KV_WIDTH:] = v[tm - keep:, :]


def _swa_qkv(x, w_qkv, pos, n_seq, tm, keep, qdtype):
    t = x.shape[0]
    s = t // n_seq
    nb = s // tm
    c, s1, s2 = _rope_tables(pos)
    row = lambda n, i: (n * nb + i, 0)
    tab = pl.BlockSpec((tm, LANES), lambda n, i: (i, 0))
    kern = functools.partial(_swa_qkv_kernel, tm=tm, keep=keep)
    return pl.pallas_call(
        kern, grid=(n_seq, nb),
        in_specs=[pl.BlockSpec((tm, D), row), _full((D, Q_WIDTH + 2 * KV_WIDTH)), tab, tab, tab],
        out_specs=[pl.BlockSpec((tm, Q_WIDTH), row), pl.BlockSpec((tm, KV_WIDTH), row),
                   pl.BlockSpec((tm, KV_WIDTH), row),
                   pl.BlockSpec((1, keep, 2 * KV_WIDTH), lambda n, i: (n, 0, 0))],
        out_shape=[jax.ShapeDtypeStruct((t, Q_WIDTH), qdtype), jax.ShapeDtypeStruct((t, KV_WIDTH), qdtype),
                   jax.ShapeDtypeStruct((t, KV_WIDTH), qdtype),
                   jax.ShapeDtypeStruct((n_seq, keep, 2 * KV_WIDTH), F32)],
        compiler_params=_cparams(("parallel", "arbitrary")), name="swa_qkv")(x, w_qkv, c, s1, s2)


def _swa_attn_kernel(sink_ref, q_ref, kp_ref, kc_ref, vp_ref, vc_ref, o_ref, *, nq):
    j = pl.program_id(1)
    w, grp = WINDOW, SWA_GROUP
    r = lax.broadcasted_iota(jnp.int32, (grp * w, 2 * w), 0) % w
    c = lax.broadcasted_iota(jnp.int32, (grp * w, 2 * w), 1)
    in_prev = jnp.logical_and(c < w, c > r)
    in_cur = jnp.logical_and(c >= w, (c - w) <= r)
    ok_inner = jnp.logical_or(in_prev, in_cur)
    ok_first = jnp.logical_or(jnp.logical_and(in_prev, j > 0), in_cur)
    scale = HEAD_DIM ** -0.5
    combos = [(u, h) for u in range(nq) for h in range(SWA_KV_HEADS)]
    kcat, vcat, q4, sink, ok = [], [], [], [], []
    for u, h in combos:
        sl = slice(h * HEAD_DIM, (h + 1) * HEAD_DIM)
        rows = slice(u * w, (u + 1) * w)
        before = slice((u - 1) * w, u * w)
        k_prev = kp_ref[:, sl] if u == 0 else kc_ref[before, sl]
        v_prev = vp_ref[:, sl] if u == 0 else vc_ref[before, sl]
        kcat.append(jnp.concatenate([k_prev, kc_ref[rows, sl]], axis=0))
        vcat.append(jnp.concatenate([v_prev, vc_ref[rows, sl]], axis=0))
        heads = [h * grp + g for g in range(grp)]
        q4.append(jnp.concatenate([q_ref[rows, hq * HEAD_DIM:(hq + 1) * HEAD_DIM] for hq in heads], axis=0))
        sink.append(jnp.concatenate([jnp.full((w, 1), sink_ref[hq], F32) for hq in heads], axis=0))
        ok.append(ok_first if u == 0 else ok_inner)
    n = range(len(combos))
    s = [jnp.where(ok[i], _dot_nt(q4[i], kcat[i]) * scale, NEG_INF) for i in n]
    m = [jnp.maximum(jnp.max(s[i], axis=-1, keepdims=True), sink[i]) for i in n]
    p = [jnp.exp(s[i] - m[i]) for i in n]
    den = [jnp.sum(p[i], axis=-1, keepdims=True) + jnp.exp(sink[i] - m[i]) for i in n]
    o = [_dot((p[i] / den[i]).astype(BF16), vcat[i]) for i in n]
    for i, (u, h) in enumerate(combos):
        for g in range(grp):
            hq = h * grp + g
            o_ref[u * w:(u + 1) * w, hq * HEAD_DIM:(hq + 1) * HEAD_DIM] = o[i][g * w:(g + 1) * w].astype(o_ref.dtype)


def _swa_attn_prompt(q, k, v, sinks, n_seq, nq):
    t = q.shape[0]
    nb = t // n_seq // WINDOW
    ns = nb // nq
    cur = lambda n, j: (n * ns + j, 0)
    prev = lambda n, j: (n * nb + jnp.maximum(j * nq - 1, 0), 0)
    return pl.pallas_call(
        functools.partial(_swa_attn_kernel, nq=nq), grid=(n_seq, ns),
        in_specs=[pl.BlockSpec(memory_space=pltpu.SMEM), pl.BlockSpec((nq * WINDOW, Q_WIDTH), cur),
                  pl.BlockSpec((WINDOW, KV_WIDTH), prev), pl.BlockSpec((nq * WINDOW, KV_WIDTH), cur),
                  pl.BlockSpec((WINDOW, KV_WIDTH), prev), pl.BlockSpec((nq * WINDOW, KV_WIDTH), cur)],
        out_specs=pl.BlockSpec((nq * WINDOW, Q_WIDTH), cur),
        out_shape=jax.ShapeDtypeStruct((t, Q_WIDTH), BF16),
        compiler_params=_cparams(("parallel", "arbitrary")), name="swa_attn")(sinks, q, k, k, v, v)


def _swa_sample_kernel(sink_ref, qbd_ref, q_ref, kn_ref, vnbd_ref, ckt_ref, cvt_ref, o_ref, *, bs):
    nkv, hd, wb = ckt_ref.shape[1:]
    nq = q_ref.shape[1]
    npad = kn_ref.shape[1]
    key = lax.broadcasted_iota(jnp.int32, (nq, wb), 1)
    valid = (wb - key) < WINDOW
    own_new = (lax.broadcasted_iota(jnp.int32, (nq, npad), 1)
               == lax.broadcasted_iota(jnp.int32, (nq, npad), 0) // SWA_GROUP)
    sink = sink_ref[...]
    scale = HEAD_DIM ** -0.5
    nb = range(bs)
    s = [jnp.where(valid, _dot(qbd_ref[b].astype(BF16), ckt_ref[b].reshape(nkv * hd, wb).astype(BF16)) * scale, NEG_INF)
         for b in nb]
    sn = [jnp.where(own_new, _dot_nt(q_ref[b].astype(BF16), kn_ref[b].astype(BF16)) * scale, NEG_INF) for b in nb]
    m = [jnp.maximum(jnp.maximum(jnp.max(s[b], axis=-1, keepdims=True), jnp.max(sn[b], axis=-1, keepdims=True)), sink)
         for b in nb]
    p = [jnp.where(valid, jnp.exp(s[b] - m[b]), 0.0) for b in nb]
    pn = [jnp.where(own_new, jnp.exp(sn[b] - m[b]), 0.0) for b in nb]
    den = [jnp.sum(p[b], axis=-1, keepdims=True) + jnp.sum(pn[b], axis=-1, keepdims=True) + jnp.exp(sink - m[b])
           for b in nb]
    for b in nb:
        o_ref[b] = (_dot_nt((p[b] / den[b]).astype(BF16), cvt_ref[b].reshape(nkv * hd, wb).astype(BF16))
                    + _dot((pn[b] / den[b]).astype(BF16), vnbd_ref[b].astype(BF16)))


def _swa_attn_sample(q, kn, vn, cache_k, cache_v, layer, sinks, bs):
    _, b, wb, nkv, hd = cache_k.shape
    grp = SWA_HEADS // nkv
    eye = jnp.eye(nkv, dtype=q.dtype)
    qbd = jnp.einsum('bhgd,hk->bhgkd', q.reshape(b, nkv, grp, hd), eye).reshape(b, SWA_HEADS, nkv * hd)
    pad = lambda z: jnp.pad(z, ((0, 0), (0, SUBLANES - nkv), (0, 0)))
    vnbd = pad(jnp.einsum('bhd,hk->bhkd', vn.reshape(b, nkv, hd), eye).reshape(b, nkv, nkv * hd))
    blk = lambda r, w: pl.BlockSpec((bs, r, w), lambda i: (i, 0, 0))
    cblk = pl.BlockSpec((None, bs, nkv, hd, wb), lambda i: (layer, i, 0, 0, 0))
    out = pl.pallas_call(
        functools.partial(_swa_sample_kernel, bs=bs), grid=(b // bs,),
        in_specs=[_full((SWA_HEADS, 1)), blk(SWA_HEADS, nkv * hd), blk(SWA_HEADS, hd), blk(SUBLANES, hd),
                  blk(SUBLANES, nkv * hd), cblk, cblk],
        out_specs=blk(SWA_HEADS, nkv * hd), out_shape=jax.ShapeDtypeStruct((b, SWA_HEADS, nkv * hd), F32),
        compiler_params=_cparams(("parallel",)), name="swa_sample")(
            sinks.reshape(SWA_HEADS, 1), qbd, q.reshape(b, SWA_HEADS, hd), pad(kn.reshape(b, nkv, hd)), vnbd,
            jnp.transpose(cache_k, (0, 1, 3, 4, 2)), jnp.transpose(cache_v, (0, 1, 3, 4, 2)))
    o5 = out.reshape(b, nkv, grp, nkv, hd)
    return jnp.stack([o5[:, h, :, h, :] for h in range(nkv)], axis=1).reshape(b, Q_WIDTH)


def _gelu_tanh(x):
    return 0.5 * x * (1.0 + jnp.tanh(0.7978845608028654 * (x + 0.044715 * x * x * x)))


def _lru_gates(xc, wa_ref, ba, wi_ref, bi, lam):
    xcb = xc.astype(BF16)
    gw = wa_ref.shape[1]
    ra, ia = [], []
    for gi in range(wa_ref.shape[0]):
        xs = xcb[:, gi * gw:(gi + 1) * gw]
        ra.append(_dot(xs, wa_ref[gi]))
        ia.append(_dot(xs, wi_ref[gi]))
    r = _sigmoid(jnp.concatenate(ra, axis=-1) + ba)
    ig = _sigmoid(jnp.concatenate(ia, axis=-1) + bi)
    log_a = -LRU_C * r * _softplus(-lam)
    a = jnp.exp(log_a)
    b = jnp.sqrt(-jnp.tanh(log_a) * (a * a + 1.0)) * (ig * xc)
    return a, b


def _shift_rows(ext, s, tm):
    return pltpu.roll(ext, s, 0)[SUBLANES:SUBLANES + tm]


def _lru_prompt_kernel(x_ref, win_ref, bin_ref, cw_ref, cb_ref, wa_ref, ba_ref, wi_ref, bi_ref, lam_ref,
                       wo_ref, g_ref, b_ref, o_ref, conv_ref, hl_ref, cx_ref, ch_ref, *, tm):
    i = pl.program_id(1)

    @pl.when(i == 0)
    def _():
        cx_ref[...] = jnp.zeros_like(cx_ref)
        ch_ref[...] = jnp.zeros_like(ch_ref)

    x = x_ref[...]
    xy = _dot(x.astype(BF16), win_ref[...]) + bin_ref[...]
    xb = xy[:, :D]
    y_gate = _gelu_tanh(xy[:, D:])
    ext = jnp.concatenate([cx_ref[...], xb], axis=0)
    cw = cw_ref[...]
    xc = cb_ref[...] + xb * cw[CONV_W - 1:CONV_W]
    for s in range(1, CONV_W):
        xc = xc + _shift_rows(ext, s, tm) * cw[CONV_W - 1 - s:CONV_W - s]
    cx_ref[...] = xb[tm - SUBLANES:]
    conv_ref[0] = xb[tm - SUBLANES:]

    a, b = _lru_gates(xc, wa_ref, ba_ref[...], wi_ref, bi_ref[...], lam_ref[...])
    sub = lax.broadcasted_iota(jnp.int32, (tm, 1), 0) % SUBLANES
    s = 1
    while s < SUBLANES:
        keep = sub >= s
        a_sh = jnp.where(keep, pltpu.roll(a, s, 0), 1.0)
        b_sh = jnp.where(keep, pltpu.roll(b, s, 0), 0.0)
        b = a * b_sh + b
        a = a * a_sh
        s *= 2
    carry = ch_ref[SUBLANES - 1:SUBLANES, :]
    groups = []
    for gi in range(tm // SUBLANES):
        rows = slice(gi * SUBLANES, (gi + 1) * SUBLANES)
        hg = a[rows] * carry + b[rows]
        groups.append(hg)
        carry = hg[SUBLANES - 1:SUBLANES]
    h = jnp.concatenate(groups, axis=0)
    ch_ref[...] = h[tm - SUBLANES:]
    hl_ref[0] = h[tm - SUBLANES:]
    acc = _dot((h * y_gate).astype(BF16), wo_ref[...])
    o_ref[...] = _ln(ALPHA * x + acc, g_ref[...], b_ref[...])


def _lru_weights(w_in, b_in, conv_w, conv_b, w_a, b_a, w_i, b_i, lam, w_o):
    gsz = 4
    ng = LRU_BLOCKS // gsz
    bw = D // LRU_BLOCKS

    def grouped(w):
        w4 = w.reshape(ng, gsz, bw, bw)
        return jnp.einsum('gaij,ab->gaibj', w4, jnp.eye(gsz, dtype=w.dtype)).reshape(ng, gsz * bw, gsz * bw).astype(BF16)

    row = lambda v: v.reshape(1, -1)
    return (w_in.astype(BF16), row(b_in), conv_w, row(conv_b), grouped(w_a), row(b_a), grouped(w_i), row(b_i),
            row(lam), w_o.astype(BF16))


def _lru_prompt(x, wts, g, b, n_seq, tm):
    t = x.shape[0]
    nb = t // n_seq // tm
    row = lambda n, i: (n * nb + i, 0)
    last = pl.BlockSpec((1, SUBLANES, D), lambda n, i: (n, 0, 0))
    w_in, b_in, cw, cb, wa, ba, wi, bi, lam, wo = wts
    return pl.pallas_call(
        functools.partial(_lru_prompt_kernel, tm=tm), grid=(n_seq, nb),
        in_specs=[pl.BlockSpec((tm, D), row), _full(w_in.shape), _full(b_in.shape), _full(cw.shape), _full(cb.shape),
                  _full(wa.shape), _full(ba.shape), _full(wi.shape), _full(bi.shape), _full(lam.shape),
                  _full(wo.shape), _full((1, D)), _full((1, D))],
        out_specs=[pl.BlockSpec((tm, D), row), last, last],
        out_shape=[jax.ShapeDtypeStruct((t, D), F32), jax.ShapeDtypeStruct((n_seq, SUBLANES, D), F32),
                   jax.ShapeDtypeStruct((n_seq, SUBLANES, D), F32)],
        scratch_shapes=[pltpu.VMEM((SUBLANES, D), F32), pltpu.VMEM((SUBLANES, D), F32)],
        compiler_params=_cparams(("parallel", "arbitrary")), name="lru_prompt")(x, *wts, g, b)


def _lru_sample_kernel(x_ref, c0_ref, c1_ref, c2_ref, h0_ref, win_ref, bin_ref, cw_ref, cb_ref, wa_ref, ba_ref,
                       wi_ref, bi_ref, lam_ref, wo_ref, g_ref, b_ref, o_ref, xb_ref, h_ref):
    x = x_ref[...]
    xy = _dot(x.astype(BF16), win_ref[...]) + bin_ref[...]
    xb = xy[:, :D]
    y_gate = _gelu_tanh(xy[:, D:])
    cw = cw_ref[...]
    xc = (cb_ref[...] + c0_ref[...] * cw[0:1] + c1_ref[...] * cw[1:2] + c2_ref[...] * cw[2:3] + xb * cw[3:4])
    a, b = _lru_gates(xc, wa_ref, ba_ref[...], wi_ref, bi_ref[...], lam_ref[...])
    h = a * h0_ref[...] + b
    xb_ref[...] = xb
    h_ref[...] = h
    acc = _dot((h * y_gate).astype(BF16), wo_ref[...])
    o_ref[...] = _ln(ALPHA * x + acc, g_ref[...], b_ref[...])


def _lru_sample(x, conv_state, h0, wts, g, b):
    t = x.shape[0]
    args = (x, conv_state[:, 0], conv_state[:, 1], conv_state[:, 2], h0, *wts, g, b)
    sd = jax.ShapeDtypeStruct((t, D), F32)
    return pl.pallas_call(
        _lru_sample_kernel, grid=(1,),
        in_specs=[_full(a.shape) for a in args],
        out_specs=[_full((t, D))] * 3, out_shape=[sd, sd, sd],
        compiler_params=_cparams(("arbitrary",)), name="lru_sample")(*args)


def _rwkv_pre_kernel(x_ref, xp_ref, mu_ref, wr_ref, wk_ref, wv_ref, w0_ref, w1_ref, w2_ref, a0_ref, a1_ref, a2_ref,
                     g1_ref, g2_ref, r_ref, k_ref, v_ref, a_ref, ld_ref, g_ref, *scratch, tm, seq):
    x = x_ref[...]
    if seq:
        cx_ref, = scratch
        i = pl.program_id(1)

        @pl.when(i == 0)
        def _():
            cx_ref[...] = xp_ref[0]

        x_prev = _shift_rows(jnp.concatenate([cx_ref[...], x], axis=0), 1, tm)
        cx_ref[...] = x[tm - SUBLANES:]
    else:
        x_prev = xp_ref[...]
    xx = x_prev - x
    mu = mu_ref[...]
    mix = lambda j: (x + xx * mu[j:j + 1]).astype(BF16)
    r_ref[...] = _dot(mix(0), wr_ref[...]).astype(r_ref.dtype)
    wl = _dot(jnp.tanh(_dot(mix(1), w1_ref[...])).astype(BF16), w2_ref[...])
    w = -_softplus(-(w0_ref[...] + wl)) - 0.5
    ld_ref[...] = -jnp.exp(w)
    k_ref[...] = _dot(mix(2), wk_ref[...]).astype(k_ref.dtype)
    v_ref[...] = _dot(mix(3), wv_ref[...]).astype(v_ref.dtype)
    al = _dot(_dot(mix(4), a1_ref[...]).astype(BF16), a2_ref[...])
    a_ref[...] = _sigmoid(a0_ref[...] + al).astype(a_ref.dtype)
    g_ref[...] = _dot(_sigmoid(_dot(mix(5), g1_ref[...])).astype(BF16), g2_ref[...]).astype(g_ref.dtype)


def _rwkv_pre(x, x_prev, wts, n_seq, tm, seq, dtype):
    t = x.shape[0]
    nb = t // n_seq // tm
    row = lambda n, i: (n * nb + i, 0)
    xp_spec = pl.BlockSpec((1, SUBLANES, D), lambda n, i: (n, 0, 0)) if seq else pl.BlockSpec((tm, D), row)
    sd = lambda dt: jax.ShapeDtypeStruct((t, D), dt)
    blk = pl.BlockSpec((tm, D), row)
    return pl.pallas_call(
        functools.partial(_rwkv_pre_kernel, tm=tm, seq=seq), grid=(n_seq, nb),
        in_specs=[blk, xp_spec] + [_full(w.shape) for w in wts],
        out_specs=[blk] * 6,
        out_shape=[sd(dtype), sd(dtype), sd(dtype), sd(dtype), sd(F32), sd(dtype)],
        scratch_shapes=[pltpu.VMEM((SUBLANES, D), F32)] if seq else [],
        compiler_params=_cparams(("parallel", "arbitrary")), name="rwkv_pre")(x, x_prev, *wts)


def _seg_sum(x, first):
    s0 = jnp.sum(jnp.where(first, x, 0.0), axis=-1, keepdims=True)
    s1 = jnp.sum(jnp.where(first, 0.0, x), axis=-1, keepdims=True)
    return jnp.where(first, s0, s1)


def _wkv_kernel(r_ref, k_ref, v_ref, a_ref, ld_ref, g_ref, kk_ref, ka_ref, rk_ref, gg_ref, gb_ref,
                o_ref, s_ref, st_ref):
    c = pl.program_id(1)
    L = WKV_CHUNK
    P2 = 2 * L
    nch = r_ref.shape[0] // L

    @pl.when(c == 0)
    def _():
        st_ref[...] = jnp.zeros_like(st_ref)

    ld_all = ld_ref[...]
    tri = (lax.broadcasted_iota(jnp.int32, (L, L), 0) >= lax.broadcasted_iota(jnp.int32, (L, L), 1)).astype(BF16)
    hi = ld_all.astype(BF16)
    r1 = ld_all - hi.astype(F32)
    mid = r1.astype(BF16)
    lo = (r1 - mid.astype(F32)).astype(BF16)
    chunk_rows = [slice(ci * L, (ci + 1) * L) for ci in range(nch)]
    cum_ch = [_dot(tri, hi[rw]) + _dot(tri, mid[rw]) + _dot(tri, lo[rw]) for rw in chunk_rows]

    lane = lax.broadcasted_iota(jnp.int32, (1, LANES), 1)
    first = lane < RWKV_HD
    ri = lax.broadcasted_iota(jnp.int32, (P2, P2), 0)
    ci_ = lax.broadcasted_iota(jnp.int32, (P2, P2), 1)
    same_head = (ri // L) == (ci_ // L)
    rt, ct = ri % L, ci_ % L
    strict = jnp.logical_and(same_head, rt > ct)
    incl = jnp.logical_and(same_head, rt >= ct)
    eye = ri == ci_

    def stack(xv):
        return jnp.concatenate([jnp.where(first, xv, 0.0), jnp.where(first, 0.0, xv)], axis=0).astype(BF16)

    npair = RWKV_HEADS // 2
    combos = [(ci, p) for ci in range(nch) for p in range(npair)]
    n = range(len(combos))
    sls = [slice(p * LANES, (p + 1) * LANES) for p in range(npair)]
    ws, us, ks, rs, ul, kl, vs, g_l, bonus = ([] for _ in range(9))
    for ci, p in combos:
        rw, sl = chunk_rows[ci], sls[p]
        rp, kp, vp, ap = (ref[rw, sl].astype(F32) for ref in (r_ref, k_ref, v_ref, a_ref))
        ldp, cum = ld_all[rw, sl], cum_ch[ci][:, sl]
        kk = kp * kk_ref[:, sl]
        kk = kk / jnp.maximum(jnp.sqrt(_seg_sum(kk * kk, first)), 1e-12)
        kmod = kp * (1.0 + (ap - 1.0) * ka_ref[:, sl])
        bp = kk * ap
        cum_l = cum[L - 1:L, :]
        g_inv = jnp.exp(-cum)
        g_to_end = jnp.exp(cum_l - cum)
        ws.append(stack(kk * jnp.exp(cum - ldp)))
        us.append(stack(bp * g_inv))
        ks.append(stack(kmod * g_inv))
        rs.append(stack(rp * jnp.exp(cum)))
        ul.append(stack(bp * g_to_end))
        kl.append(stack(kmod * g_to_end))
        vs.append(stack(vp))
        g_l.append(jnp.exp(cum_l))
        bonus.append(_seg_sum(rp * kmod * rk_ref[:, sl], first) * vp)

    gram = [_dot_nt(jnp.concatenate([ws[q], rs[q]], axis=0), jnp.concatenate([us[q], ks[q]], axis=0)) for q in n]
    n_mat = [jnp.where(strict, gram[q][:P2, :P2], 0.0) for q in n]
    m_mat = [jnp.where(strict, gram[q][:P2, P2:], 0.0).astype(BF16) for q in n]
    nr_mat = [jnp.where(incl, gram[q][P2:, :P2], 0.0).astype(BF16) for q in n]
    mr_mat = [jnp.where(incl, gram[q][P2:, P2:], 0.0).astype(BF16) for q in n]

    def level_mask(sz):
        sub = jnp.logical_and((rt // sz) % 2 == 1, (ct // sz) % 2 == 0)
        return jnp.logical_and(jnp.logical_and(sub, (rt // (2 * sz)) == (ct // (2 * sz))), same_head)

    x_inv = [jnp.where(eye, 1.0, 0.0) - jnp.where(level_mask(1), n_mat[q], 0.0) for q in n]
    sz = 2
    while sz < L:
        mask = level_mask(sz)
        xb = [x_inv[q].astype(BF16) for q in n]
        xc = [_dot(xb[q], jnp.where(mask, n_mat[q], 0.0).astype(BF16)).astype(BF16) for q in n]
        x_inv = [x_inv[q] - _dot(xc[q], xb[q]) for q in n]
        sz *= 2
    x_inv = [x_inv[q].astype(BF16) for q in n]

    state = [st_ref[p] for p in range(npair)]
    inv_n = 1.0 / RWKV_HD
    for ci in range(nch):
        qs = [ci * npair + p for p in range(npair)]
        a0b = [state[p].astype(BF16) for p in range(npair)]
        rhs = [_dot(jnp.concatenate([ws[q], m_mat[q]], axis=1), jnp.concatenate([a0b[p], vs[q]], axis=0)).astype(BF16)
               for p, q in enumerate(qs)]
        pm = [(-_dot(x_inv[q], rhs[p])).astype(BF16) for p, q in enumerate(qs)]
        o_st = [_dot(jnp.concatenate([rs[q], nr_mat[q], mr_mat[q]], axis=1),
                     jnp.concatenate([a0b[p], pm[p], vs[q]], axis=0)) for p, q in enumerate(qs)]
        new_state = []
        for p, q in enumerate(qs):
            g_col = jnp.sum(jnp.where(eye, jnp.broadcast_to(g_l[q], (P2, P2)), 0.0), axis=-1, keepdims=True)
            new_state.append(g_col * state[p] + _dot_tn(jnp.concatenate([ul[q], kl[q]], axis=0),
                                                        jnp.concatenate([pm[p], vs[q]], axis=0)))
        state = new_state
        for p, q in enumerate(qs):
            sl = sls[p]
            o = o_st[p][:L] + o_st[p][L:]
            mu = _seg_sum(o, first) * inv_n
            oc = o - mu
            var = _seg_sum(oc * oc, first) * inv_n
            on = oc * lax.rsqrt(var + RWKV_GN_EPS) * gg_ref[:, sl] + gb_ref[:, sl]
            o_ref[chunk_rows[ci], sl] = ((on + bonus[q]) * g_ref[chunk_rows[ci], sl].astype(F32)).astype(o_ref.dtype)

    for p in range(npair):
        st_ref[p] = state[p]
    s_ref[0] = st_ref[...]


def _wkv_prompt(r, k, v, a, ld, g, hp, n_seq, nch):
    t = r.shape[0]
    L = WKV_CHUNK * nch
    nc = t // n_seq // L
    row = lambda n, c: (n * nc + c, 0)
    blk = pl.BlockSpec((L, D), row)
    npair = RWKV_HEADS // 2
    return pl.pallas_call(
        _wkv_kernel, grid=(n_seq, nc),
        in_specs=[blk] * 6 + [_full((1, D))] * 5,
        out_specs=[blk, pl.BlockSpec((1, npair, LANES, LANES), lambda n, c: (n, 0, 0, 0))],
        out_shape=[jax.ShapeDtypeStruct((t, D), BF16), jax.ShapeDtypeStruct((n_seq, npair, LANES, LANES), F32)],
        scratch_shapes=[pltpu.VMEM((npair, LANES, LANES), F32)],
        compiler_params=_cparams(("parallel", "arbitrary")), name="wkv_chunk")(r, k, v, a, ld, g, *hp)


def _wkv_sample_kernel(r_ref, k_ref, v_ref, a_ref, ld_ref, g_ref, s_ref, kk_ref, ka_ref, rk_ref, gg_ref, gb_ref,
                       o_ref, so_ref):
    r, k, v, a, ld, g = (ref[0] for ref in (r_ref, k_ref, v_ref, a_ref, ld_ref, g_ref))
    kk = k * kk_ref[0]
    kk = kk / jnp.maximum(jnp.sqrt(jnp.sum(kk * kk, axis=0, keepdims=True)), 1e-12)
    kmod = k * (1.0 + (a - 1.0) * ka_ref[0])
    akk = kk * a
    decay = jnp.exp(ld)

    def value_row(vi, carry):
        s = s_ref[0, vi]
        skk = jnp.sum(s * kk, axis=0, keepdims=True)
        s_new = s * decay - skk * akk + v_ref[0, pl.ds(vi, 1), :] * kmod
        so_ref[0, vi] = s_new
        o_ref[0, pl.ds(vi, 1), :] = jnp.sum(s_new * r, axis=0, keepdims=True)
        return carry

    lax.fori_loop(0, s_ref.shape[1], value_row, 0, unroll=4)
    o = o_ref[0]
    mu = jnp.mean(o, axis=0, keepdims=True)
    oc = o - mu
    var = jnp.mean(oc * oc, axis=0, keepdims=True)
    on = oc * lax.rsqrt(var + RWKV_GN_EPS) * gg_ref[0] + gb_ref[0]
    bonus = jnp.sum(r * kmod * rk_ref[0], axis=0, keepdims=True) * v
    o_ref[0] = (on + bonus) * g


def _wkv_sample(r, k, v, a, ld, g, state, hp):
    b = r.shape[0]
    nh, hd = RWKV_HEADS, RWKV_HD
    t3 = lambda z: jnp.transpose(z.reshape(b, nh, hd), (1, 2, 0))
    vec = pl.BlockSpec((1, hd, b), lambda h: (h, 0, 0))
    par = pl.BlockSpec((1, hd, 1), lambda h: (h, 0, 0))
    sblk = pl.BlockSpec((1, hd, hd, b), lambda h: (h, 0, 0, 0))
    o, s_new = pl.pallas_call(
        _wkv_sample_kernel, grid=(nh,),
        in_specs=[vec] * 6 + [sblk] + [par] * 5,
        out_specs=[vec, sblk],
        out_shape=[jax.ShapeDtypeStruct((nh, hd, b), F32), jax.ShapeDtypeStruct((nh, hd, hd, b), F32)],
        compiler_params=_cparams(("parallel",)), name="wkv_sample")(
            t3(r), t3(k), t3(v), t3(a), t3(ld), t3(g), jnp.transpose(state, (1, 2, 3, 0)),
            *[z.reshape(nh, hd, 1) for z in hp])
    return jnp.transpose(o, (2, 0, 1)).reshape(b, D), jnp.transpose(s_new, (3, 0, 1, 2))


def _mem_prompt_kernel(x_ref, wq_ref, mk_ref, mv_ref, wo_ref, g_ref, b_ref, o_ref):
    x = x_ref[...]
    q = _dot(x.astype(BF16), wq_ref[...]).astype(BF16)
    scale = MEM_HD ** -0.5
    sls = [slice(h * MEM_HD, (h + 1) * MEM_HD) for h in range(MEM_HEADS)]
    s = [_dot_nt(q[:, sl], mk_ref[0, :, sl]) * scale for sl in sls]
    p = [jnp.exp(sh - jnp.max(sh, axis=-1, keepdims=True)) for sh in s]
    den = [jnp.sum(ph, axis=-1, keepdims=True) for ph in p]
    outs = [_dot((ph / dh).astype(BF16), mv_ref[0, :, sl]).astype(BF16) for ph, dh, sl in zip(p, den, sls)]
    acc = _dot(jnp.concatenate(outs, axis=-1), wo_ref[...])
    o_ref[...] = _ln(ALPHA * x + acc, g_ref[...], b_ref[...])


def _mem_attn_prompt(x, w_q, mk, mv, w_o, g, b, n_seq, tm):
    t = x.shape[0]
    nb = t // n_seq // tm
    m = mk.shape[1]
    row = lambda n, i: (n * nb + i, 0)
    mem = pl.BlockSpec((1, m, D), lambda n, i: (n, 0, 0))
    return pl.pallas_call(
        _mem_prompt_kernel, grid=(n_seq, nb),
        in_specs=[pl.BlockSpec((tm, D), row), _full((D, D)), mem, mem, _full((D, D)), _full((1, D)), _full((1, D))],
        out_specs=pl.BlockSpec((tm, D), row), out_shape=jax.ShapeDtypeStruct((t, D), F32),
        compiler_params=_cparams(("parallel", "arbitrary")), name="mem_attn")(x, w_q, mk, mv, w_o, g, b)


def _mem_sample_kernel(q_ref, ck_ref, cv_ref, o_ref, *, bs):
    m, nh, hd = ck_ref.shape[1:]
    rows = q_ref.shape[1]
    col_head = lax.broadcasted_iota(jnp.int32, (rows, m * nh), 1) % nh
    own = col_head == lax.broadcasted_iota(jnp.int32, (rows, m * nh), 0)
    scale = MEM_HD ** -0.5
    nb = range(bs)
    s = [jnp.where(own, _dot_nt(q_ref[b].astype(BF16), ck_ref[b].reshape(m * nh, hd).astype(BF16)) * scale, NEG_INF)
         for b in nb]
    p = [jnp.where(own, jnp.exp(s[b] - jnp.max(s[b], axis=-1, keepdims=True)), 0.0) for b in nb]
    den = [jnp.sum(p[b], axis=-1, keepdims=True) for b in nb]
    for b in nb:
        pb = (p[b] / jnp.where(den[b] > 0.0, den[b], 1.0)).astype(BF16)
        o_ref[b] = _dot(pb, cv_ref[b].reshape(m * nh, hd).astype(BF16))


def _mem_attn_sample(q, cache_k, cache_v, layer, bs):
    _, b, m, nh, hd = cache_k.shape
    q3 = jnp.pad(q.reshape(b, nh, hd), ((0, 0), (0, SUBLANES - nh), (0, 0)))
    qb = pl.BlockSpec((bs, SUBLANES, hd), lambda i: (i, 0, 0))
    cb = pl.BlockSpec((None, bs, m, nh, hd), lambda i: (layer, i, 0, 0, 0))
    out = pl.pallas_call(
        functools.partial(_mem_sample_kernel, bs=bs), grid=(b // bs,), in_specs=[qb, cb, cb], out_specs=qb,
        out_shape=jax.ShapeDtypeStruct((b, SUBLANES, hd), F32),
        compiler_params=_cparams(("parallel",)), name="mem_sample")(q3, cache_k, cache_v)
    return out[:, :nh].reshape(b, D)


_PAIRS = ((0, 1), (0, 2), (0, 3), (1, 2), (1, 3), (2, 3))


def _router_kernel(x_ref, rw_ref, rb_ref, bucket_ref, rank_ref, cnt_ref, base_ref, *, tm):
    i = pl.program_id(0)

    @pl.when(i == 0)
    def _():
        base_ref[...] = jnp.zeros_like(base_ref)

    logits = _dot_nt(rw_ref[...], x_ref[...].astype(BF16))
    e = jnp.exp(logits - jnp.max(logits, axis=0, keepdims=True))
    sel = e / jnp.sum(e, axis=0, keepdims=True) + rb_ref[...]
    s = [sel[j:j + 1, :] for j in range(N_EXPERTS)]
    neg = jnp.float32(-jnp.inf)

    best = jnp.zeros((1, tm), jnp.int32)
    best_score = None
    for gi in range(N_GROUPS):
        s0, s1, s2, s3 = s[4 * gi:4 * gi + 4]
        hi01, lo01, hi23, lo23 = jnp.maximum(s0, s1), jnp.minimum(s0, s1), jnp.maximum(s2, s3), jnp.minimum(s2, s3)
        score = jnp.maximum(hi01, hi23) + jnp.maximum(jnp.minimum(hi01, hi23), jnp.maximum(lo01, lo23))
        if gi == 0:
            best_score = score
        else:
            take = score > best_score
            best = jnp.where(take, gi, best)
            best_score = jnp.where(take, score, best_score)
    vals = []
    for j in range(EXPERTS_PER_GROUP):
        vj = s[j]
        for gi in range(1, N_GROUPS):
            vj = jnp.where(best == gi, s[4 * gi + j], vj)
        vals.append(vj)

    def argmax4(v):
        idx, mx = jnp.zeros((1, tm), jnp.int32), v[0]
        for j in range(1, EXPERTS_PER_GROUP):
            take = v[j] > mx
            idx = jnp.where(take, j, idx)
            mx = jnp.where(take, v[j], mx)
        return idx

    i1 = argmax4(vals)
    i2 = argmax4([jnp.where(i1 == j, neg, vals[j]) for j in range(EXPERTS_PER_GROUP)])
    lo, hi = jnp.minimum(i1, i2), jnp.maximum(i1, i2)
    pair = jnp.zeros((1, tm), jnp.int32)
    for pi, (pa, pb) in enumerate(_PAIRS):
        pair = jnp.where(jnp.logical_and(lo == pa, hi == pb), pi, pair)
    bucket = best * len(_PAIRS) + pair
    bucket_ref[0] = bucket

    onehot = (lax.broadcasted_iota(jnp.int32, (BUCKET_ROWS, tm), 0) == bucket).astype(F32)
    upper = (lax.broadcasted_iota(jnp.int32, (tm, tm), 0) <= lax.broadcasted_iota(jnp.int32, (tm, tm), 1)).astype(BF16)
    cum = _dot(onehot.astype(BF16), upper)
    base = base_ref[...]
    rank = jnp.sum(onehot * (cum + base), axis=0, keepdims=True) - 1.0
    rank_ref[0] = rank.astype(jnp.int32)
    base = base + jnp.sum(onehot, axis=1, keepdims=True)
    base_ref[...] = base
    cnt_ref[...] = jnp.broadcast_to(base, cnt_ref.shape)


def _router(x, rw_t, rb, tm):
    t = x.shape[0]
    nb = t // tm
    ib = pl.BlockSpec((1, 1, tm), lambda i: (i, 0, 0))
    bucket, rank, cnt = pl.pallas_call(
        functools.partial(_router_kernel, tm=tm), grid=(nb,),
        in_specs=[pl.BlockSpec((tm, D), lambda i: (i, 0)), _full(rw_t.shape), _full(rb.shape)],
        out_specs=[ib, ib, _full((BUCKET_ROWS, LANES))],
        out_shape=[jax.ShapeDtypeStruct((nb, 1, tm), jnp.int32), jax.ShapeDtypeStruct((nb, 1, tm), jnp.int32),
                   jax.ShapeDtypeStruct((BUCKET_ROWS, LANES), F32)],
        scratch_shapes=[pltpu.VMEM((BUCKET_ROWS, 1), F32)],
        compiler_params=_cparams(("arbitrary",)), name="router")(x, rw_t, rb)
    return bucket.reshape(t), rank.reshape(t), cnt[:N_BUCKETS, 0].astype(jnp.int32)


_ROW_GROUP = 8


def _row_copies(idx_ref, base, src_hbm, dst, sem, n, wait):
    def body(j, carry):
        for k in range(_ROW_GROUP):
            r = j * _ROW_GROUP + k
            cp = pltpu.make_async_copy(src_hbm.at[pl.ds(idx_ref[base + r], 1)], dst.at[pl.ds(r, 1)], sem)
            if wait:
                cp.wait()
            else:
                cp.start(priority=k % 2)
        return carry

    lax.fori_loop(0, n // _ROW_GROUP, body, 0)


def _ffn_kernel(src_ref, lo_ref, hi_ref, nrow_ref, used_ref, x_hbm, rw_ref, g0_ref, u0_ref, d0_ref, g1_ref, u1_ref,
                d1_ref, o_ref, xbuf, sem, *, blk):
    i = pl.program_id(0)
    used = used_ref[0]
    slot = i % 2

    @pl.when(jnp.logical_and(i == 0, used > 0))
    def _():
        xbuf[...] = jnp.zeros_like(xbuf)
        _row_copies(src_ref, 0, x_hbm, xbuf.at[0], sem.at[0], nrow_ref[0], False)

    @pl.when(i + 1 < used)
    def _():
        _row_copies(src_ref, (i + 1) * blk, x_hbm, xbuf.at[1 - slot], sem.at[1 - slot], nrow_ref[i + 1], False)

    @pl.when(i < used)
    def _():
        _row_copies(src_ref, i * blk, x_hbm, xbuf.at[slot], sem.at[slot], nrow_ref[i], True)
        xb = xbuf[slot].astype(BF16)
        logits = _dot(xb, rw_ref[...])
        lane = lax.broadcasted_iota(jnp.int32, logits.shape, 1)
        l_lo = jnp.sum(jnp.where(lane == lo_ref[i], logits, 0.0), axis=-1, keepdims=True)
        l_hi = jnp.sum(jnp.where(lane == hi_ref[i], logits, 0.0), axis=-1, keepdims=True)
        w_lo = _sigmoid(l_lo - l_hi)

        def expert(g_ref, u_ref, d_ref):
            gate = _dot(xb, g_ref[0])
            act = gate * _sigmoid(gate) * _dot(xb, u_ref[0])
            return _dot(act.astype(BF16), d_ref[0])

        y_lo = expert(g0_ref, u0_ref, d0_ref)
        y_hi = expert(g1_ref, u1_ref, d1_ref)
        o_ref[...] = w_lo * y_lo + (1.0 - w_lo) * y_hi

    @pl.when(i >= used)
    def _():
        o_ref[...] = jnp.zeros_like(o_ref)


def _ffn(x, src, blk_lo, blk_hi, blk_rows, n_used, rw, w_gate, w_up, w_down, layer, blk):
    rows = src.shape[0]
    nblk = rows // blk
    wg = lambda sel: pl.BlockSpec((None, 1, D, EXPERT_FF),
                                  lambda i, s, lo, hi, nr, used: (layer, (lo, hi)[sel][i], 0, 0))
    wd = lambda sel: pl.BlockSpec((None, 1, EXPERT_FF, D),
                                  lambda i, s, lo, hi, nr, used: (layer, (lo, hi)[sel][i], 0, 0))
    return pl.pallas_call(
        functools.partial(_ffn_kernel, blk=blk),
        grid_spec=pltpu.PrefetchScalarGridSpec(
            num_scalar_prefetch=5, grid=(nblk,),
            in_specs=[pl.BlockSpec(memory_space=pl.ANY),
                      pl.BlockSpec(rw.shape, lambda i, s, lo, hi, nr, used: (0, 0)),
                      wg(0), wg(0), wd(0), wg(1), wg(1), wd(1)],
            out_specs=pl.BlockSpec((blk, D), lambda i, s, lo, hi, nr, used: (i, 0)),
            scratch_shapes=[pltpu.VMEM((2, blk, D), F32), pltpu.SemaphoreType.DMA((2,))]),
        out_shape=jax.ShapeDtypeStruct((rows, D), F32),
        compiler_params=_cparams(("arbitrary",)), name="moe_ffn")(
            src, blk_lo, blk_hi, blk_rows, n_used, x, rw, w_gate, w_up, w_down, w_gate, w_up, w_down)


def _combine_ln_kernel(dest_ref, x_ref, y_hbm, g_ref, b_ref, o_ref, ybuf, sem, *, tm):
    i = pl.program_id(0)
    slot = i % 2

    @pl.when(i == 0)
    def _():
        _row_copies(dest_ref, 0, y_hbm, ybuf.at[0], sem.at[0], tm, False)

    @pl.when(i + 1 < pl.num_programs(0))
    def _():
        _row_copies(dest_ref, (i + 1) * tm, y_hbm, ybuf.at[1 - slot], sem.at[1 - slot], tm, False)

    _row_copies(dest_ref, i * tm, y_hbm, ybuf.at[slot], sem.at[slot], tm, True)
    o_ref[...] = _ln(ALPHA * x_ref[...] + ybuf[slot], g_ref[...], b_ref[...])


def _combine_ln(x, y_rows, dest, g, b, tm):
    t = x.shape[0]
    rowb = pl.BlockSpec((tm, D), lambda i, d: (i, 0))
    vec = pl.BlockSpec((1, D), lambda i, d: (0, 0))
    return pl.pallas_call(
        functools.partial(_combine_ln_kernel, tm=tm),
        grid_spec=pltpu.PrefetchScalarGridSpec(
            num_scalar_prefetch=1, grid=(t // tm,),
            in_specs=[rowb, pl.BlockSpec(memory_space=pl.ANY), vec, vec], out_specs=rowb,
            scratch_shapes=[pltpu.VMEM((2, tm, D), F32), pltpu.SemaphoreType.DMA((2,))]),
        out_shape=jax.ShapeDtypeStruct((t, D), F32),
        compiler_params=_cparams(("arbitrary",)), name="moe_combine_ln")(dest, x, y_rows, g, b)


def _invert_rows_kernel(dest_ref, src_ref):
    def clear(r, carry):
        src_ref[r] = 0
        return carry

    def put(tok, carry):
        src_ref[dest_ref[tok]] = tok
        return carry

    lax.fori_loop(0, src_ref.shape[0], clear, 0, unroll=8)
    lax.fori_loop(0, dest_ref.shape[0], put, 0, unroll=8)


def _invert_rows(dest, rows):
    smem = pl.BlockSpec(memory_space=pltpu.SMEM)
    return pl.pallas_call(
        _invert_rows_kernel, in_specs=[smem], out_specs=smem,
        out_shape=jax.ShapeDtypeStruct((rows,), jnp.int32), name="invert_rows")(dest)


def _moe_ln(x, rw_t, rb, rw_pad, w_gate, w_up, w_down, layer, g, b, tm_router, blk, tm_comb):
    t = x.shape[0]
    bucket, rank, counts = _router(x, rw_t, rb, tm_router)
    padded = (counts + blk - 1) // blk * blk
    ends = jnp.cumsum(padded)
    dest = ((ends - padded)[bucket] + rank).astype(jnp.int32)
    nblk = t // blk + N_BUCKETS
    src = _invert_rows(dest, nblk * blk)
    blk_bucket = jnp.minimum(jnp.searchsorted(ends, jnp.arange(nblk) * blk, side='right'), N_BUCKETS - 1)
    pair_lo = jnp.array([p[0] for p in _PAIRS], jnp.int32)
    pair_hi = jnp.array([p[1] for p in _PAIRS], jnp.int32)
    grp, pr = blk_bucket // len(_PAIRS), blk_bucket % len(_PAIRS)
    blk_lo = (grp * EXPERTS_PER_GROUP + pair_lo[pr]).astype(jnp.int32)
    blk_hi = (grp * EXPERTS_PER_GROUP + pair_hi[pr]).astype(jnp.int32)
    n_used = (ends[-1:] // blk).astype(jnp.int32)
    bucket_end = (ends - padded + counts)[blk_bucket]
    blk_rows = jnp.clip(bucket_end - jnp.arange(nblk) * blk, 0, blk)
    blk_rows = ((blk_rows + _ROW_GROUP - 1) // _ROW_GROUP * _ROW_GROUP).astype(jnp.int32)
    y_rows = _ffn(x, src, blk_lo, blk_hi, blk_rows, n_used, rw_pad, w_gate, w_up, w_down, layer, blk)
    return _combine_ln(x, y_rows, dest, g, b, tm_comb)


def _moe_dense_kernel(x_ref, lo_ref, hi_ref, rw_ref, wg_ref, wu_ref, wd_ref, g_ref, b_ref, o_ref, acc_ref):
    e = pl.program_id(0)

    @pl.when(e == 0)
    def _():
        acc_ref[...] = jnp.zeros_like(acc_ref)

    x = x_ref[...]
    xb = x.astype(BF16)
    logits = _dot(xb, rw_ref[...])
    lane = lax.broadcasted_iota(jnp.int32, logits.shape, 1)
    lo, hi = lo_ref[...], hi_ref[...]
    l_lo = jnp.sum(jnp.where(lane == lo, logits, 0.0), axis=-1, keepdims=True)
    l_hi = jnp.sum(jnp.where(lane == hi, logits, 0.0), axis=-1, keepdims=True)
    w_lo = _sigmoid(l_lo - l_hi)
    coef = jnp.where(lo == e, w_lo, 0.0) + jnp.where(hi == e, 1.0 - w_lo, 0.0)
    gate = _dot(xb, wg_ref[0])
    act = gate * _sigmoid(gate) * _dot(xb, wu_ref[0])
    acc_ref[...] += coef * _dot(act.astype(BF16), wd_ref[0])

    @pl.when(e == pl.num_programs(0) - 1)
    def _():
        o_ref[...] = _ln(ALPHA * x + acc_ref[...], g_ref[...], b_ref[...])


def _moe_ln_dense(x, rw_t, rb, rw_pad, w_gate, w_up, w_down, layer, g, b):
    t = x.shape[0]
    bucket, _, _ = _router(x, rw_t, rb, t)
    pair_lo = jnp.array([p[0] for p in _PAIRS], jnp.int32)
    pair_hi = jnp.array([p[1] for p in _PAIRS], jnp.int32)
    grp, pr = bucket // len(_PAIRS), bucket % len(_PAIRS)
    lo = (grp * EXPERTS_PER_GROUP + pair_lo[pr]).astype(jnp.int32).reshape(t, 1)
    hi = (grp * EXPERTS_PER_GROUP + pair_hi[pr]).astype(jnp.int32).reshape(t, 1)
    wg = pl.BlockSpec((None, 1, D, EXPERT_FF), lambda e: (layer, e, 0, 0))
    wd = pl.BlockSpec((None, 1, EXPERT_FF, D), lambda e: (layer, e, 0, 0))
    return pl.pallas_call(
        _moe_dense_kernel, grid=(N_EXPERTS,),
        in_specs=[_full((t, D)), _full((t, 1)), _full((t, 1)), _full(rw_pad.shape), wg, wg, wd,
                  _full((1, D)), _full((1, D))],
        out_specs=_full((t, D)), out_shape=jax.ShapeDtypeStruct((t, D), F32),
        scratch_shapes=[pltpu.VMEM((t, D), F32)],
        compiler_params=_cparams(("arbitrary",)), name="moe_dense")(x, lo, hi, rw_pad, w_gate, w_up, w_down, g, b)


def kernel(x_prompt, x_sample, cache_swa_k, cache_swa_v, state_lru_conv, state_lru_h, state_rwkv_shift, state_rwkv_wkv, cache_mem_k, cache_mem_v, mem_prompt, swa_w_qkv, swa_sinks, swa_w_o, lru_w_in, lru_b_in, lru_conv_w, lru_conv_b, lru_w_a, lru_b_a, lru_w_i, lru_b_i, lru_lambda, lru_w_o, rwkv_mu, rwkv_w_r, rwkv_w_k, rwkv_w_v, rwkv_w0, rwkv_w1, rwkv_w2, rwkv_a0, rwkv_a1, rwkv_a2, rwkv_g1, rwkv_g2, rwkv_k_k, rwkv_k_a, rwkv_r_k, rwkv_gn_g, rwkv_gn_b, rwkv_w_o, mem_w_q, mem_w_kv, mem_w_o, ln_g, ln_b, router_w, router_b, moe_w_gate, moe_w_up, moe_w_down):
    n_p, seq, _ = x_prompt.shape
    n_s, dec_seq, _ = x_sample.shape
    assert dec_seq == 1
    past_len = 8192
    xp = x_prompt.reshape(n_p * seq, D)
    xs = x_sample.reshape(n_s, D)
    row = lambda v: v.reshape(1, -1)
    bf = lambda w: w.astype(BF16)

    rw_t = bf(router_w.T)
    rb = router_b.reshape(N_EXPERTS, 1)
    rw_pad = bf(jnp.pad(router_w, ((0, 0), (0, LANES - N_EXPERTS))))
    wg, wu, wd = bf(moe_w_gate), bf(moe_w_up), bf(moe_w_down)
    mem_p = mem_prompt.reshape(n_p * mem_prompt.shape[1], D)
    m_len = mem_prompt.shape[1]

    swa_k_p, swa_v_p, swa_k_s, swa_v_s = [], [], [], []
    lru_c_p, lru_h_p, lru_c_s, lru_h_s = [], [], [], []
    rw_x_p, rw_s_p, rw_x_s, rw_s_s = [], [], [], []
    mem_k_p, mem_v_p = [], []

    for layer in range(DEPTH):
        kind, i = layer % N_MIXERS, layer // N_MIXERS
        g0, b0 = row(ln_g[layer, 0]), row(ln_b[layer, 0])
        if kind == 0:
            w_qkv, w_o = bf(swa_w_qkv[i]), bf(swa_w_o[i])
            keep = min(WINDOW, seq)
            q, k, v, kv_last = _swa_qkv(xp, w_qkv, jnp.arange(seq), n_p, 1024, keep, BF16)
            o = _swa_attn_prompt(q, k, v, swa_sinks[i], n_p, 4)
            swa_k_p.append(kv_last[:, :, :KV_WIDTH].reshape(n_p, keep, SWA_KV_HEADS, HEAD_DIM))
            swa_v_p.append(kv_last[:, :, KV_WIDTH:].reshape(n_p, keep, SWA_KV_HEADS, HEAD_DIM))
            xp = _proj_ln(o, w_o, xp, g0, b0, 1024)

            qs, _, _, kv_new = _swa_qkv(xs, w_qkv, jnp.full((n_s,), past_len), 1, n_s, n_s, F32)
            kn, vn = kv_new[0, :, :KV_WIDTH], kv_new[0, :, KV_WIDTH:]
            os_ = _swa_attn_sample(qs, kn, vn, cache_swa_k, cache_swa_v, i, swa_sinks[i], 8)
            wb = cache_swa_k.shape[2]
            k_all = jnp.concatenate([cache_swa_k[i], kn.reshape(n_s, 1, SWA_KV_HEADS, HEAD_DIM)], axis=1)
            v_all = jnp.concatenate([cache_swa_v[i], vn.reshape(n_s, 1, SWA_KV_HEADS, HEAD_DIM)], axis=1)
            swa_k_s.append(k_all[:, -wb:])
            swa_v_s.append(v_all[:, -wb:])
            xs = _proj_ln(os_, w_o, xs, g0, b0, n_s)
        elif kind == 1:
            wts = _lru_weights(lru_w_in[i], lru_b_in[i], lru_conv_w[i], lru_conv_b[i], lru_w_a[i], lru_b_a[i],
                               lru_w_i[i], lru_b_i[i], lru_lambda[i], lru_w_o[i])
            xp, conv_last, h_last = _lru_prompt(xp, wts, g0, b0, n_p, 256)
            lru_c_p.append(conv_last[:, SUBLANES - (CONV_W - 1):])
            lru_h_p.append(h_last[:, SUBLANES - 1])
            xs, xb_s, h_s = _lru_sample(xs, state_lru_conv[i], state_lru_h[i], wts, g0, b0)
            lru_c_s.append(jnp.concatenate([state_lru_conv[i][:, 1:], xb_s[:, None]], axis=1))
            lru_h_s.append(h_s)
        else:
            wts = (rwkv_mu[i], bf(rwkv_w_r[i]), bf(rwkv_w_k[i]), bf(rwkv_w_v[i]), row(rwkv_w0[i]), bf(rwkv_w1[i]),
                   bf(rwkv_w2[i]), row(rwkv_a0[i]), bf(rwkv_a1[i]), bf(rwkv_a2[i]), bf(rwkv_g1[i]), bf(rwkv_g2[i]))
            hp = (row(rwkv_k_k[i]), row(rwkv_k_a[i]), row(rwkv_r_k[i]), row(rwkv_gn_g[i]), row(rwkv_gn_b[i]))
            w_o = bf(rwkv_w_o[i])
            rw_x_p.append(xp.reshape(n_p, seq, D)[:, -1])
            rw_x_s.append(xs)
            r, k, v, a, ld, g = _rwkv_pre(xp, jnp.zeros((n_p, SUBLANES, D), F32), wts, n_p, 512, True, BF16)
            o, st = _wkv_prompt(r, k, v, a, ld, g, hp, n_p, 4)
            hd = RWKV_HD
            st = jnp.stack([st[:, :, :hd, :hd], st[:, :, hd:, hd:]], axis=2).reshape(n_p, RWKV_HEADS, hd, hd)
            rw_s_p.append(jnp.swapaxes(st, -1, -2))
            xp = _proj_ln(o, w_o, xp, g0, b0, 1024)

            r, k, v, a, ld, g = _rwkv_pre(xs, state_rwkv_shift[i], wts, 1, n_s, False, F32)
            os_, s_new = _wkv_sample(r, k, v, a, ld, g, state_rwkv_wkv[i], hp)
            rw_s_s.append(s_new)
            xs = _proj_ln(os_, w_o, xs, g0, b0, n_s)

        g1, b1 = row(ln_g[layer, 1]), row(ln_b[layer, 1])
        w_q, w_o = bf(mem_w_q[layer]), bf(mem_w_o[layer])
        mkv = _matmul(mem_p, bf(mem_w_kv[layer]), 512)
        mk, mv = mkv[:, :D], mkv[:, D:]
        mem_k_p.append(mk.reshape(n_p, m_len, MEM_HEADS, MEM_HD))
        mem_v_p.append(mv.reshape(n_p, m_len, MEM_HEADS, MEM_HD))
        xp = _mem_attn_prompt(xp, w_q, bf(mk).reshape(n_p, m_len, D), bf(mv).reshape(n_p, m_len, D), w_o, g1, b1,
                              n_p, 1024)
        qs = _matmul(xs, w_q, n_s)
        os_ = _mem_attn_sample(qs, cache_mem_k, cache_mem_v, layer, 4)
        xs = _proj_ln(os_, w_o, xs, g1, b1, n_s)

        g2, b2 = row(ln_g[layer, 2]), row(ln_b[layer, 2])
        xp = _moe_ln(xp, rw_t, rb, rw_pad, wg, wu, wd, layer, g2, b2, 512, 256, 512)
        xs = _moe_ln_dense(xs, rw_t, rb, rw_pad, wg, wu, wd, layer, g2, b2)

    return (xp.reshape(n_p, seq, D), xs.reshape(n_s, 1, D),
            jnp.stack(swa_k_p), jnp.stack(swa_v_p), jnp.stack(lru_c_p), jnp.stack(lru_h_p),
            jnp.stack(rw_x_p), jnp.stack(rw_s_p), jnp.stack(mem_k_p), jnp.stack(mem_v_p),
            jnp.stack(swa_k_s), jnp.stack(swa_v_s), jnp.stack(lru_c_s), jnp.stack(lru_h_s),
            jnp.stack(rw_x_s), jnp.stack(rw_s_s))
```
